```python
import jax, jax.numpy as jnp
from jax import lax
import numpy as np

D_MODEL = 1024
BATCH = 8
SEQ = 8192
DEPTH = 4

CHUNK = 64
N_MIXERS = 2
CONV_WIDTH = 31
RET_HEADS = 4
RET_QK_DIM = D_MODEL // RET_HEADS
RET_V_DIM = 2 * D_MODEL // RET_HEADS
RET_QK_TOTAL = RET_HEADS * RET_QK_DIM
RET_V_TOTAL = RET_HEADS * RET_V_DIM
RET_IN_WIDTH = 2 * RET_QK_TOTAL + 2 * RET_V_TOTAL
D_FF = 4 * D_MODEL
ROPE_BASE = 10000.0
EPS = 1e-6
N_CONV_LAYERS = (DEPTH + 1) // 2
N_RET_LAYERS = DEPTH // 2

kernel_name = "hybrid_conformer_retention_adaln_trunk"


def rmsnorm(x, g):
    xf = x.astype(jnp.float32)
    y = xf * lax.rsqrt(jnp.mean(xf * xf, axis=-1, keepdims=True) + EPS)
    return (y * g.astype(jnp.float32)).astype(x.dtype)


def modulate(h, shift, scale):
    return h * (1.0 + scale[:, None, :]) + shift[:, None, :]


def conformer_conv(h, w_pw1, b_pw1, w_dw, b_dw, ln_g, ln_b, w_pw2, b_pw2):
    u = h @ w_pw1 + b_pw1
    a, g = jnp.split(u, 2, axis=-1)
    u = a * jax.nn.sigmoid(g)
    u = lax.conv_general_dilated(
        u, w_dw[:, None, :], window_strides=(1,), padding=[(CONV_WIDTH - 1, 0)],
        dimension_numbers=('NWC', 'WIO', 'NWC'), feature_group_count=D_MODEL) + b_dw
    uf = u.astype(jnp.float32)
    mu = jnp.mean(uf, axis=-1, keepdims=True)
    var = jnp.mean(jnp.square(uf - mu), axis=-1, keepdims=True)
    u = ((uf - mu) * lax.rsqrt(var + EPS) * ln_g + ln_b).astype(h.dtype)
    u = jax.nn.silu(u)
    return u @ w_pw2 + b_pw2


def rope_tables(seq):
    pos = jnp.arange(seq, dtype=jnp.float32)
    inv = ROPE_BASE ** (-jnp.arange(0, RET_QK_DIM, 2, dtype=jnp.float32) / RET_QK_DIM)
    ang = pos[:, None] * inv[None, :]
    return jnp.cos(ang), jnp.sin(ang)


def apply_rope(x, cos, sin):
    half = RET_QK_DIM // 2
    x1, x2 = x[..., :half], x[..., half:]
    c = cos[None, :, None, :].astype(x.dtype)
    s = sin[None, :, None, :].astype(x.dtype)
    return jnp.concatenate([x1 * c - x2 * s, x2 * c + x1 * s], axis=-1)


def retention(h, w_in, gn_g, gn_b, w_out, cos, sin, log_gamma):
    b, s, _ = h.shape
    nc = s // CHUNK
    proj = h @ w_in
    q, k, v, gate = jnp.split(
        proj, [RET_QK_TOTAL, 2 * RET_QK_TOTAL, 2 * RET_QK_TOTAL + RET_V_TOTAL], axis=-1)
    q = apply_rope(q.reshape(b, s, RET_HEADS, RET_QK_DIM), cos, sin)
    k = apply_rope(k.reshape(b, s, RET_HEADS, RET_QK_DIM), cos, sin) * (RET_QK_DIM ** -0.5)
    v = v.reshape(b, s, RET_HEADS, RET_V_DIM)

    def to_chunks(t):
        return t.reshape(b, nc, CHUNK, RET_HEADS, t.shape[-1]).transpose(0, 1, 3, 2, 4)

    qc, kc, vc = to_chunks(q), to_chunks(k), to_chunks(v)
    idx = jnp.arange(CHUNK, dtype=jnp.float32)
    d_intra = jnp.exp(log_gamma[:, None, None] * jnp.abs(idx[:, None] - idx[None, :]))
    scores = jnp.einsum('bnhcd,bnhed->bnhce', qc, kc) * d_intra.astype(qc.dtype)
    intra = jnp.einsum('bnhce,bnhef->bnhcf', scores, vc)

    xi = jnp.exp(log_gamma[:, None] * (idx + 1.0))
    zeta = jnp.exp(log_gamma[:, None] * (CHUNK - 1.0 - idx))
    chunk_decay = jnp.exp(log_gamma * CHUNK)

    def step(state, inp):
        qj, kj, vj = inp
        cross = jnp.einsum('bhcd,bhdf->bhcf', qj * xi[..., None], state)
        state = state * chunk_decay[:, None, None] + jnp.einsum(
            'bhcd,bhcf->bhdf', kj * zeta[..., None], vj)
        return state, cross

    state0 = jnp.zeros((b, RET_HEADS, RET_QK_DIM, RET_V_DIM), jnp.float32)
    xs = (qc.transpose(1, 0, 2, 3, 4), kc.transpose(1, 0, 2, 3, 4), vc.transpose(1, 0, 2, 3, 4))
    _, cross = lax.scan(step, state0, xs)
    y = intra + cross.transpose(1, 0, 2, 3, 4).astype(intra.dtype)
    y = y.transpose(0, 1, 3, 2, 4).reshape(b, s, RET_HEADS, RET_V_DIM)
    yf = y.astype(jnp.float32)
    mu = jnp.mean(yf, axis=-1, keepdims=True)
    var = jnp.mean(jnp.square(yf - mu), axis=-1, keepdims=True)
    y = ((yf - mu) * lax.rsqrt(var + EPS) * gn_g + gn_b).astype(h.dtype)
    y = jax.nn.silu(gate) * y.reshape(b, s, RET_V_TOTAL)
    return y @ w_out


def _fwd_setup_inputs(seed: int = 0) -> dict:
    key = jax.random.key(seed)
    ks = jax.random.split(key, 24)
    f32 = jnp.float32
    D = D_MODEL

    def nrm(k, shape, std):
        return jax.random.normal(k, shape, f32) * std

    return {
        "x": nrm(ks[0], (BATCH, SEQ, D), 1.0),
        "c": nrm(ks[1], (BATCH, D), 1.0),
        "ada_w": nrm(ks[2], (DEPTH, D, 6 * D), 0.5 * D ** -0.5),
        "ada_b": nrm(ks[3], (DEPTH, 6 * D), 0.02),
        "norm_mix_g": 1.0 + nrm(ks[4], (DEPTH, D), 0.02),
        "norm_mlp_g": 1.0 + nrm(ks[5], (DEPTH, D), 0.02),
        "conv_w_pw1": nrm(ks[6], (N_CONV_LAYERS, D, 2 * D), D ** -0.5),
        "conv_b_pw1": nrm(ks[7], (N_CONV_LAYERS, 2 * D), 0.02),
        "conv_w_dw": nrm(ks[8], (N_CONV_LAYERS, CONV_WIDTH, D), CONV_WIDTH ** -0.5),
        "conv_b_dw": nrm(ks[9], (N_CONV_LAYERS, D), 0.02),
        "conv_ln_g": 1.0 + nrm(ks[10], (N_CONV_LAYERS, D), 0.02),
        "conv_ln_b": nrm(ks[11], (N_CONV_LAYERS, D), 0.02),
        "conv_w_pw2": nrm(ks[12], (N_CONV_LAYERS, D, D), D ** -0.5),
        "conv_b_pw2": nrm(ks[13], (N_CONV_LAYERS, D), 0.02),
        "ret_w_in": nrm(ks[14], (N_RET_LAYERS, D, RET_IN_WIDTH), D ** -0.5),
        "ret_gn_g": 1.0 + nrm(ks[15], (N_RET_LAYERS, RET_HEADS, RET_V_DIM), 0.02),
        "ret_gn_b": nrm(ks[16], (N_RET_LAYERS, RET_HEADS, RET_V_DIM), 0.02),
        "ret_w_out": nrm(ks[17], (N_RET_LAYERS, RET_V_TOTAL, D), RET_V_TOTAL ** -0.5),
        "mlp_w1": nrm(ks[18], (DEPTH, D, D_FF), D ** -0.5),
        "mlp_w2": nrm(ks[19], (DEPTH, D_FF, D), D_FF ** -0.5),
        "final_norm_g": 1.0 + nrm(ks[20], (D,), 0.02),
    }


def _fwd_reference(x, c, ada_w, ada_b, norm_mix_g, norm_mlp_g, conv_w_pw1, conv_b_pw1, conv_w_dw,
              conv_b_dw, conv_ln_g, conv_ln_b, conv_w_pw2, conv_b_pw2, ret_w_in, ret_gn_g,
              ret_gn_b, ret_w_out, mlp_w1, mlp_w2, final_norm_g):
    seq = x.shape[1]
    cos, sin = rope_tables(seq)
    log_gamma = jnp.log(1.0 - 2.0 ** (-5.0 - jnp.arange(RET_HEADS, dtype=jnp.float32)))
    cond = jax.nn.silu(c)
    for i in range(DEPTH):
        mod = cond @ ada_w[i] + ada_b[i]
        sh1, sc1, g1, sh2, sc2, g2 = jnp.split(mod, 6, axis=-1)
        h = modulate(rmsnorm(x, norm_mix_g[i]), sh1, sc1)
        j = i // N_MIXERS
        if i % N_MIXERS == 0:
            y = conformer_conv(h, conv_w_pw1[j], conv_b_pw1[j], conv_w_dw[j], conv_b_dw[j],
                               conv_ln_g[j], conv_ln_b[j], conv_w_pw2[j], conv_b_pw2[j])
        else:
            y = retention(h, ret_w_in[j], ret_gn_g[j], ret_gn_b[j], ret_w_out[j],
                          cos, sin, log_gamma)
        x = x + g1[:, None, :] * y
        h = modulate(rmsnorm(x, norm_mlp_g[i]), sh2, sc2)
        x = x + g2[:, None, :] * (jnp.square(jax.nn.relu(h @ mlp_w1[i])) @ mlp_w2[i])
    return rmsnorm(x, final_norm_g)


import jax as _jax
import jax.numpy as _jnp

TWIN_FORMAT = 'train_step'
FWD_PARAMS = ['x', 'c', 'ada_w', 'ada_b', 'norm_mix_g', 'norm_mlp_g', 'conv_w_pw1', 'conv_b_pw1', 'conv_w_dw', 'conv_b_dw', 'conv_ln_g', 'conv_ln_b', 'conv_w_pw2', 'conv_b_pw2', 'ret_w_in', 'ret_gn_g', 'ret_gn_b', 'ret_w_out', 'mlp_w1', 'mlp_w2', 'final_norm_g']
TWIN_WEIGHTS = ['ada_w', 'ada_b', 'norm_mix_g', 'norm_mlp_g', 'conv_w_pw1', 'conv_b_pw1', 'conv_w_dw', 'conv_b_dw', 'conv_ln_g', 'conv_ln_b', 'conv_w_pw2', 'conv_b_pw2', 'ret_w_in', 'ret_gn_g', 'ret_gn_b', 'ret_w_out', 'mlp_w1', 'mlp_w2', 'final_norm_g']
TWIN_DIFF_INPUT = 'x'
TWIN_INPUTS = ['x', 'c', 'ada_w', 'ada_b', 'norm_mix_g', 'norm_mlp_g', 'conv_w_pw1', 'conv_b_pw1', 'conv_w_dw', 'conv_b_dw', 'conv_ln_g', 'conv_ln_b', 'conv_w_pw2', 'conv_b_pw2', 'ret_w_in', 'ret_gn_g', 'ret_gn_b', 'ret_w_out', 'mlp_w1', 'mlp_w2', 'final_norm_g', 'loss_target', 'm_ada_w', 'm_ada_b', 'm_norm_mix_g', 'm_norm_mlp_g', 'm_conv_w_pw1', 'm_conv_b_pw1', 'm_conv_w_dw', 'm_conv_b_dw', 'm_conv_ln_g', 'm_conv_ln_b', 'm_conv_w_pw2', 'm_conv_b_pw2', 'm_ret_w_in', 'm_ret_gn_g', 'm_ret_gn_b', 'm_ret_w_out', 'm_mlp_w1', 'm_mlp_w2', 'm_final_norm_g', 'v_ada_w', 'v_ada_b', 'v_norm_mix_g', 'v_norm_mlp_g', 'v_conv_w_pw1', 'v_conv_b_pw1', 'v_conv_w_dw', 'v_conv_b_dw', 'v_conv_ln_g', 'v_conv_ln_b', 'v_conv_w_pw2', 'v_conv_b_pw2', 'v_ret_w_in', 'v_ret_gn_g', 'v_ret_gn_b', 'v_ret_w_out', 'v_mlp_w1', 'v_mlp_w2', 'v_final_norm_g']
TWIN_OUTPUTS = ['loss', 'grad_x', 'grad_ada_w', 'grad_ada_b', 'grad_norm_mix_g', 'grad_norm_mlp_g', 'grad_conv_w_pw1', 'grad_conv_b_pw1', 'grad_conv_w_dw', 'grad_conv_b_dw', 'grad_conv_ln_g', 'grad_conv_ln_b', 'grad_conv_w_pw2', 'grad_conv_b_pw2', 'grad_ret_w_in', 'grad_ret_gn_g', 'grad_ret_gn_b', 'grad_ret_w_out', 'grad_mlp_w1', 'grad_mlp_w2', 'grad_final_norm_g', 'delta_ada_w', 'delta_ada_b', 'delta_norm_mix_g', 'delta_norm_mlp_g', 'delta_conv_w_pw1', 'delta_conv_b_pw1', 'delta_conv_w_dw', 'delta_conv_b_dw', 'delta_conv_ln_g', 'delta_conv_ln_b', 'delta_conv_w_pw2', 'delta_conv_b_pw2', 'delta_ret_w_in', 'delta_ret_gn_g', 'delta_ret_gn_b', 'delta_ret_w_out', 'delta_mlp_w1', 'delta_mlp_w2', 'delta_final_norm_g', 'new_m_ada_w', 'new_m_ada_b', 'new_m_norm_mix_g', 'new_m_norm_mlp_g', 'new_m_conv_w_pw1', 'new_m_conv_b_pw1', 'new_m_conv_w_dw', 'new_m_conv_b_dw', 'new_m_conv_ln_g', 'new_m_conv_ln_b', 'new_m_conv_w_pw2', 'new_m_conv_b_pw2', 'new_m_ret_w_in', 'new_m_ret_gn_g', 'new_m_ret_gn_b', 'new_m_ret_w_out', 'new_m_mlp_w1', 'new_m_mlp_w2', 'new_m_final_norm_g', 'new_v_ada_w', 'new_v_ada_b', 'new_v_norm_mix_g', 'new_v_norm_mlp_g', 'new_v_conv_w_pw1', 'new_v_conv_b_pw1', 'new_v_conv_w_dw', 'new_v_conv_b_dw', 'new_v_conv_ln_g', 'new_v_conv_ln_b', 'new_v_conv_w_pw2', 'new_v_conv_b_pw2', 'new_v_ret_w_in', 'new_v_ret_gn_g', 'new_v_ret_gn_b', 'new_v_ret_w_out', 'new_v_mlp_w1', 'new_v_mlp_w2', 'new_v_final_norm_g']
TWIN_LEAF_KINDS = {'loss': 'loss', 'grad_x': 'grad_x', 'grad_ada_w': 'grad_w', 'grad_ada_b': 'grad_w', 'grad_norm_mix_g': 'grad_w', 'grad_norm_mlp_g': 'grad_w', 'grad_conv_w_pw1': 'grad_w', 'grad_conv_b_pw1': 'grad_w', 'grad_conv_w_dw': 'grad_w', 'grad_conv_b_dw': 'grad_w', 'grad_conv_ln_g': 'grad_w', 'grad_conv_ln_b': 'grad_w', 'grad_conv_w_pw2': 'grad_w', 'grad_conv_b_pw2': 'grad_w', 'grad_ret_w_in': 'grad_w', 'grad_ret_gn_g': 'grad_w', 'grad_ret_gn_b': 'grad_w', 'grad_ret_w_out': 'grad_w', 'grad_mlp_w1': 'grad_w', 'grad_mlp_w2': 'grad_w', 'grad_final_norm_g': 'grad_w', 'delta_ada_w': 'delta_w', 'delta_ada_b': 'delta_w', 'delta_norm_mix_g': 'delta_w', 'delta_norm_mlp_g': 'delta_w', 'delta_conv_w_pw1': 'delta_w', 'delta_conv_b_pw1': 'delta_w', 'delta_conv_w_dw': 'delta_w', 'delta_conv_b_dw': 'delta_w', 'delta_conv_ln_g': 'delta_w', 'delta_conv_ln_b': 'delta_w', 'delta_conv_w_pw2': 'delta_w', 'delta_conv_b_pw2': 'delta_w', 'delta_ret_w_in': 'delta_w', 'delta_ret_gn_g': 'delta_w', 'delta_ret_gn_b': 'delta_w', 'delta_ret_w_out': 'delta_w', 'delta_mlp_w1': 'delta_w', 'delta_mlp_w2': 'delta_w', 'delta_final_norm_g': 'delta_w', 'new_m_ada_w': 'new_m', 'new_m_ada_b': 'new_m', 'new_m_norm_mix_g': 'new_m', 'new_m_norm_mlp_g': 'new_m', 'new_m_conv_w_pw1': 'new_m', 'new_m_conv_b_pw1': 'new_m', 'new_m_conv_w_dw': 'new_m', 'new_m_conv_b_dw': 'new_m', 'new_m_conv_ln_g': 'new_m', 'new_m_conv_ln_b': 'new_m', 'new_m_conv_w_pw2': 'new_m', 'new_m_conv_b_pw2': 'new_m', 'new_m_ret_w_in': 'new_m', 'new_m_ret_gn_g': 'new_m', 'new_m_ret_gn_b': 'new_m', 'new_m_ret_w_out': 'new_m', 'new_m_mlp_w1': 'new_m', 'new_m_mlp_w2': 'new_m', 'new_m_final_norm_g': 'new_m', 'new_v_ada_w': 'new_v', 'new_v_ada_b': 'new_v', 'new_v_norm_mix_g': 'new_v', 'new_v_norm_mlp_g': 'new_v', 'new_v_conv_w_pw1': 'new_v', 'new_v_conv_b_pw1': 'new_v', 'new_v_conv_w_dw': 'new_v', 'new_v_conv_b_dw': 'new_v', 'new_v_conv_ln_g': 'new_v', 'new_v_conv_ln_b': 'new_v', 'new_v_conv_w_pw2': 'new_v', 'new_v_conv_b_pw2': 'new_v', 'new_v_ret_w_in': 'new_v', 'new_v_ret_gn_g': 'new_v', 'new_v_ret_gn_b': 'new_v', 'new_v_ret_w_out': 'new_v', 'new_v_mlp_w1': 'new_v', 'new_v_mlp_w2': 'new_v', 'new_v_final_norm_g': 'new_v'}


def _forward(args):
    return _fwd_reference(*[args[k] for k in FWD_PARAMS])


def _output_shape():
    def fwd():
        inp = _fwd_setup_inputs(0)
        return _fwd_reference(*[inp[k] for k in FWD_PARAMS])
    out = _jax.eval_shape(fwd)
    return out.shape, out.dtype

N_MICROBATCH = 1
ADAM_LR = 0.001
ADAM_B1 = 0.9
ADAM_B2 = 0.999
ADAM_EPS = 1e-08
ADAM_WD = 0.01
ADAM_STEP = 10
PER_EXAMPLE_BATCH_AXIS = {'x': 0, 'c': 0, 'loss_target': 0}
SHARED_INPUTS = []
_WEIGHT_DTYPES = {'ada_w': _jnp.float32, 'ada_b': _jnp.float32, 'norm_mix_g': _jnp.float32, 'norm_mlp_g': _jnp.float32, 'conv_w_pw1': _jnp.float32, 'conv_b_pw1': _jnp.float32, 'conv_w_dw': _jnp.float32, 'conv_b_dw': _jnp.float32, 'conv_ln_g': _jnp.float32, 'conv_ln_b': _jnp.float32, 'conv_w_pw2': _jnp.float32, 'conv_b_pw2': _jnp.float32, 'ret_w_in': _jnp.float32, 'ret_gn_g': _jnp.float32, 'ret_gn_b': _jnp.float32, 'ret_w_out': _jnp.float32, 'mlp_w1': _jnp.float32, 'mlp_w2': _jnp.float32, 'final_norm_g': _jnp.float32}
MOMENT_SCALE = {'ada_w': 1.104689e-01, 'ada_b': 1.959174e-01, 'norm_mix_g': 7.031455e-02, 'norm_mlp_g': 1.035836e-01, 'conv_w_pw1': 3.667562e-02, 'conv_b_pw1': 4.387953e-02, 'conv_w_dw': 4.828725e-02, 'conv_b_dw': 9.495977e-02, 'conv_ln_g': 6.295374e-02, 'conv_ln_b': 6.039193e-02, 'conv_w_pw2': 4.838338e-02, 'conv_b_pw2': 1.040040e-01, 'ret_w_in': 3.931982e-02, 'ret_gn_g': 3.101322e-02, 'ret_gn_b': 3.160655e-02, 'ret_w_out': 4.327574e-02, 'mlp_w1': 5.540169e-02, 'mlp_w2': 1.028239e-01, 'final_norm_g': 6.457395e+01}


def _to_microbatches(a, axis):
    t = _jnp.moveaxis(a, axis, 0)
    t = t.reshape((N_MICROBATCH, t.shape[0] // N_MICROBATCH) + t.shape[1:])
    return _jnp.moveaxis(t, 1, axis + 1)


def setup_inputs(seed: int = 0) -> dict:
    inp = _fwd_setup_inputs(seed)
    key = _jax.random.fold_in(_jax.random.key(seed), 7919)
    shape, _ = _output_shape()
    out = dict(inp)
    out["loss_target"] = _jax.random.normal(_jax.random.fold_in(key, 0), shape, _jnp.float32)
    for i, name in enumerate(TWIN_WEIGHTS):
        w = inp[name].astype(_jnp.float32)
        if MOMENT_SCALE is None:
            s = _jnp.sqrt(_jnp.mean(_jnp.square(w)) + 1e-30)
        else:
            s = MOMENT_SCALE[name]
        km, kv = _jax.random.split(_jax.random.fold_in(key, i + 1))
        out[name] = w
        out["m_" + name] = s * _jax.random.normal(km, w.shape, _jnp.float32)
        out["v_" + name] = (s * s) * _jax.random.uniform(kv, w.shape, _jnp.float32, 0.5, 1.5)
    if N_MICROBATCH > 1:
        for name, axis in PER_EXAMPLE_BATCH_AXIS.items():
            out[name] = _to_microbatches(out[name], axis)
    return {'x': out['x'], 'c': out['c'], 'ada_w': out['ada_w'], 'ada_b': out['ada_b'], 'norm_mix_g': out['norm_mix_g'], 'norm_mlp_g': out['norm_mlp_g'], 'conv_w_pw1': out['conv_w_pw1'], 'conv_b_pw1': out['conv_b_pw1'], 'conv_w_dw': out['conv_w_dw'], 'conv_b_dw': out['conv_b_dw'], 'conv_ln_g': out['conv_ln_g'], 'conv_ln_b': out['conv_ln_b'], 'conv_w_pw2': out['conv_w_pw2'], 'conv_b_pw2': out['conv_b_pw2'], 'ret_w_in': out['ret_w_in'], 'ret_gn_g': out['ret_gn_g'], 'ret_gn_b': out['ret_gn_b'], 'ret_w_out': out['ret_w_out'], 'mlp_w1': out['mlp_w1'], 'mlp_w2': out['mlp_w2'], 'final_norm_g': out['final_norm_g'], 'loss_target': out['loss_target'], 'm_ada_w': out['m_ada_w'], 'm_ada_b': out['m_ada_b'], 'm_norm_mix_g': out['m_norm_mix_g'], 'm_norm_mlp_g': out['m_norm_mlp_g'], 'm_conv_w_pw1': out['m_conv_w_pw1'], 'm_conv_b_pw1': out['m_conv_b_pw1'], 'm_conv_w_dw': out['m_conv_w_dw'], 'm_conv_b_dw': out['m_conv_b_dw'], 'm_conv_ln_g': out['m_conv_ln_g'], 'm_conv_ln_b': out['m_conv_ln_b'], 'm_conv_w_pw2': out['m_conv_w_pw2'], 'm_conv_b_pw2': out['m_conv_b_pw2'], 'm_ret_w_in': out['m_ret_w_in'], 'm_ret_gn_g': out['m_ret_gn_g'], 'm_ret_gn_b': out['m_ret_gn_b'], 'm_ret_w_out': out['m_ret_w_out'], 'm_mlp_w1': out['m_mlp_w1'], 'm_mlp_w2': out['m_mlp_w2'], 'm_final_norm_g': out['m_final_norm_g'], 'v_ada_w': out['v_ada_w'], 'v_ada_b': out['v_ada_b'], 'v_norm_mix_g': out['v_norm_mix_g'], 'v_norm_mlp_g': out['v_norm_mlp_g'], 'v_conv_w_pw1': out['v_conv_w_pw1'], 'v_conv_b_pw1': out['v_conv_b_pw1'], 'v_conv_w_dw': out['v_conv_w_dw'], 'v_conv_b_dw': out['v_conv_b_dw'], 'v_conv_ln_g': out['v_conv_ln_g'], 'v_conv_ln_b': out['v_conv_ln_b'], 'v_conv_w_pw2': out['v_conv_w_pw2'], 'v_conv_b_pw2': out['v_conv_b_pw2'], 'v_ret_w_in': out['v_ret_w_in'], 'v_ret_gn_g': out['v_ret_gn_g'], 'v_ret_gn_b': out['v_ret_gn_b'], 'v_ret_w_out': out['v_ret_w_out'], 'v_mlp_w1': out['v_mlp_w1'], 'v_mlp_w2': out['v_mlp_w2'], 'v_final_norm_g': out['v_final_norm_g']}


def _loss(weights, diff, rest, loss_target):
    with _jax.named_scope("forward"):
        args = {**rest, TWIN_DIFF_INPUT: diff, **{k: w.astype(_WEIGHT_DTYPES[k]) for k, w in weights.items()}}
        y = _forward(args)
    with _jax.named_scope("loss_head"):
        err = _jnp.square(y.astype(_jnp.float32) - loss_target)
        return 0.5 * _jnp.sum(_jnp.mean(err, axis=-1)) if err.ndim else 0.5 * err


def _adamw(w, g, m, v):
    m = ADAM_B1 * m + (1.0 - ADAM_B1) * g
    v = ADAM_B2 * v + (1.0 - ADAM_B2) * _jnp.square(g)
    m_hat = m / (1.0 - ADAM_B1 ** ADAM_STEP)
    v_hat = v / (1.0 - ADAM_B2 ** ADAM_STEP)
    delta = -ADAM_LR * (m_hat / (_jnp.sqrt(v_hat) + ADAM_EPS) + ADAM_WD * w)
    return delta, m, v


def reference(x, c, ada_w, ada_b, norm_mix_g, norm_mlp_g, conv_w_pw1, conv_b_pw1, conv_w_dw, conv_b_dw, conv_ln_g, conv_ln_b, conv_w_pw2, conv_b_pw2, ret_w_in, ret_gn_g, ret_gn_b, ret_w_out, mlp_w1, mlp_w2, final_norm_g, loss_target, m_ada_w, m_ada_b, m_norm_mix_g, m_norm_mlp_g, m_conv_w_pw1, m_conv_b_pw1, m_conv_w_dw, m_conv_b_dw, m_conv_ln_g, m_conv_ln_b, m_conv_w_pw2, m_conv_b_pw2, m_ret_w_in, m_ret_gn_g, m_ret_gn_b, m_ret_w_out, m_mlp_w1, m_mlp_w2, m_final_norm_g, v_ada_w, v_ada_b, v_norm_mix_g, v_norm_mlp_g, v_conv_w_pw1, v_conv_b_pw1, v_conv_w_dw, v_conv_b_dw, v_conv_ln_g, v_conv_ln_b, v_conv_w_pw2, v_conv_b_pw2, v_ret_w_in, v_ret_gn_g, v_ret_gn_b, v_ret_w_out, v_mlp_w1, v_mlp_w2, v_final_norm_g):
    given = dict(x=x, c=c, ada_w=ada_w, ada_b=ada_b, norm_mix_g=norm_mix_g, norm_mlp_g=norm_mlp_g, conv_w_pw1=conv_w_pw1, conv_b_pw1=conv_b_pw1, conv_w_dw=conv_w_dw, conv_b_dw=conv_b_dw, conv_ln_g=conv_ln_g, conv_ln_b=conv_ln_b, conv_w_pw2=conv_w_pw2, conv_b_pw2=conv_b_pw2, ret_w_in=ret_w_in, ret_gn_g=ret_gn_g, ret_gn_b=ret_gn_b, ret_w_out=ret_w_out, mlp_w1=mlp_w1, mlp_w2=mlp_w2, final_norm_g=final_norm_g, loss_target=loss_target, m_ada_w=m_ada_w, m_ada_b=m_ada_b, m_norm_mix_g=m_norm_mix_g, m_norm_mlp_g=m_norm_mlp_g, m_conv_w_pw1=m_conv_w_pw1, m_conv_b_pw1=m_conv_b_pw1, m_conv_w_dw=m_conv_w_dw, m_conv_b_dw=m_conv_b_dw, m_conv_ln_g=m_conv_ln_g, m_conv_ln_b=m_conv_ln_b, m_conv_w_pw2=m_conv_w_pw2, m_conv_b_pw2=m_conv_b_pw2, m_ret_w_in=m_ret_w_in, m_ret_gn_g=m_ret_gn_g, m_ret_gn_b=m_ret_gn_b, m_ret_w_out=m_ret_w_out, m_mlp_w1=m_mlp_w1, m_mlp_w2=m_mlp_w2, m_final_norm_g=m_final_norm_g, v_ada_w=v_ada_w, v_ada_b=v_ada_b, v_norm_mix_g=v_norm_mix_g, v_norm_mlp_g=v_norm_mlp_g, v_conv_w_pw1=v_conv_w_pw1, v_conv_b_pw1=v_conv_b_pw1, v_conv_w_dw=v_conv_w_dw, v_conv_b_dw=v_conv_b_dw, v_conv_ln_g=v_conv_ln_g, v_conv_ln_b=v_conv_ln_b, v_conv_w_pw2=v_conv_w_pw2, v_conv_b_pw2=v_conv_b_pw2, v_ret_w_in=v_ret_w_in, v_ret_gn_g=v_ret_gn_g, v_ret_gn_b=v_ret_gn_b, v_ret_w_out=v_ret_w_out, v_mlp_w1=v_mlp_w1, v_mlp_w2=v_mlp_w2, v_final_norm_g=v_final_norm_g)
    weights = {n: given[n] for n in TWIN_WEIGHTS}
    shared = {n: given[n] for n in SHARED_INPUTS}
    per_example = {n: given[n] for n in ['x', 'c']}
    grad_fn = _jax.value_and_grad(_loss, argnums=(0, 1))

    def one_microbatch(ex, loss_target):
        ex = dict(ex)
        diff = ex.pop(TWIN_DIFF_INPUT)
        return grad_fn(weights, diff, {**shared, **ex}, loss_target)

    if N_MICROBATCH == 1:
        loss, (grad_w, grad_x) = one_microbatch(per_example, given["loss_target"])
    else:
        def body(carry, xs):
            loss_sum, grad_sum = carry
            l_k, (gw_k, gx_k) = one_microbatch(xs[0], xs[1])
            with _jax.named_scope("update"):
                return (loss_sum + l_k, _jax.tree.map(_jnp.add, grad_sum, gw_k)), gx_k

        init = (_jnp.zeros((), _jnp.float32), _jax.tree.map(_jnp.zeros_like, weights))
        (loss, grad_w), grad_x = _jax.lax.scan(body, init, (per_example, given["loss_target"]))
    with _jax.named_scope("update"):
        delta_w, new_m, new_v = {}, {}, {}
        for n in TWIN_WEIGHTS:
            delta_w[n], new_m[n], new_v[n] = _adamw(weights[n], grad_w[n], given["m_" + n], given["v_" + n])
    return (loss, grad_x, *[grad_w[n] for n in TWIN_WEIGHTS], *[delta_w[n] for n in TWIN_WEIGHTS],
            *[new_m[n] for n in TWIN_WEIGHTS], *[new_v[n] for n in TWIN_WEIGHTS])
```

```python
import functools

import jax
import jax.numpy as jnp
from jax import lax
from jax.experimental import pallas as pl
from jax.experimental.pallas import tpu as pltpu

F32, BF16 = jnp.float32, jnp.bfloat16
AXES = ("x", "y", "c")
NDEV = 8
DEPTH = 4
EPS = 1e-6
CHUNK = 64
CONV_WIDTH = 31
HALO = 32
RET_HEADS = 4
RET_BLOCK = 256
ROPE_BASE = 10000.0
LANES = 128
ADAM_LR, ADAM_B1, ADAM_B2, ADAM_EPS, ADAM_WD, ADAM_STEP = 0.001, 0.9, 0.999, 1e-08, 0.01, 10
VMEM_LIMIT = 56 * 1024 * 1024
VMEM_BLOCK_BUDGET = 36 * 1024 * 1024
MESH = pl.DeviceIdType.MESH
NT_DIMS = (((1,), (1,)), ((), ()))
TN_DIMS = (((0,), (0,)), ((), ()))


def _call(body, *, name, out_shape, in_specs, out_specs, grid=(), scratch=(), sem=None, aliases=None):
    params = dict(vmem_limit_bytes=VMEM_LIMIT)
    if sem is not None:
        params["dimension_semantics"] = sem
    return pl.pallas_call(body, name=name, grid=grid, in_specs=in_specs, out_specs=out_specs, out_shape=out_shape,
                          scratch_shapes=list(scratch), input_output_aliases=aliases or {},
                          compiler_params=pltpu.CompilerParams(**params))


def _row_tile(rows, want):
    t = min(rows, want)
    while rows % t:
        t //= 2
    return t


def _sds(shape, dtype):
    return jax.ShapeDtypeStruct(tuple(shape), dtype)


def _sigmoid(v):
    return 1.0 / (1.0 + jnp.exp(-v))


def _exchange(groups, *, gather, name):
    flat = [a for g in groups for a in g]
    n_in = len(flat)
    out_shapes = []
    for g in groups:
        s = g[0].shape if gather else g[0].shape[1:]
        lead = (NDEV,) if len(g) == 1 else (NDEV, len(g))
        out_shapes.append(_sds(lead + tuple(s), g[0].dtype))
    n_g = len(groups)

    def body(*refs):
        ins, outs = refs[:n_in], refs[n_in:n_in + n_g]
        send_sems, recv_sems, loc_sems = refs[n_in + n_g:]
        x, y, c = lax.axis_index("x"), lax.axis_index("y"), lax.axis_index("c")
        me = 4 * x + 2 * y + c
        locs, k = [], 0
        for gi, g in enumerate(groups):
            for li in range(len(g)):
                src = ins[k] if gather else ins[k].at[me]
                dst = outs[gi].at[me] if len(g) == 1 else outs[gi].at[me, li]
                cp = pltpu.make_async_copy(src, dst, loc_sems.at[k])
                cp.start()
                locs.append(cp)
                k += 1
        k0 = 0
        for gi, g in enumerate(groups):
            for r in range(1, NDEV):
                px = 1 - x if r & 4 else x
                py = 1 - y if r & 2 else y
                pc = 1 - c if r & 1 else c
                peer = 4 * px + 2 * py + pc
                for li in range(len(g)):
                    src = ins[k0 + li] if gather else ins[k0 + li].at[peer]
                    dst = outs[gi].at[me] if len(g) == 1 else outs[gi].at[me, li]
                    pltpu.make_async_remote_copy(src_ref=src, dst_ref=dst, send_sem=send_sems.at[gi, r - 1],
                                                 recv_sem=recv_sems.at[gi, r - 1], device_id=(px, py, pc),
                                                 device_id_type=MESH).start()
            k0 += len(g)
        for gi, g in enumerate(groups):
            for r in range(1, NDEV):
                px = 1 - x if r & 4 else x
                py = 1 - y if r & 2 else y
                pc = 1 - c if r & 1 else c
                peer = 4 * px + 2 * py + pc
                slab = pltpu.make_async_remote_copy(src_ref=outs[gi].at[me], dst_ref=outs[gi].at[peer],
                                                    send_sem=send_sems.at[gi, r - 1], recv_sem=recv_sems.at[gi, r - 1],
                                                    device_id=(px, py, pc), device_id_type=MESH)
                slab.wait_send()
                slab.wait_recv()
        for cp in locs:
            cp.wait()

    hbm = pl.BlockSpec(memory_space=pltpu.HBM)
    outs = _call(body, name=name, out_shape=tuple(out_shapes), in_specs=[hbm] * n_in, out_specs=tuple([hbm] * n_g),
                 scratch=[pltpu.SemaphoreType.DMA((n_g, NDEV - 1)), pltpu.SemaphoreType.DMA((n_g, NDEV - 1)),
                          pltpu.SemaphoreType.DMA((n_in,))])(*flat)
    return list(outs)


def _norm_mod_fwd(x, gain, sc, sh, *, name):
    S, D = x.shape
    tm = _row_tile(S, 512)

    def body(x_ref, g_ref, sc_ref, sh_ref, h_ref):
        xv = x_ref[...]
        r = lax.rsqrt(jnp.mean(xv * xv, axis=-1, keepdims=True) + EPS)
        h_ref[...] = ((xv * r) * g_ref[...] * (1.0 + sc_ref[...]) + sh_ref[...]).astype(BF16)

    row = pl.BlockSpec((tm, D), lambda i: (i, 0))
    vec = pl.BlockSpec((1, D), lambda i: (0, 0))
    return _call(body, name=name, grid=(S // tm,), in_specs=[row, vec, vec, vec], out_specs=row,
                 out_shape=_sds((S, D), BF16), sem=("parallel",))(x, gain, sc, sh)


def _norm_mod_bwd(x, gain, sc, dh, dres, *, name):
    S, D = x.shape
    tm = _row_tile(S, 512)

    def body(x_ref, g_ref, sc_ref, dh_ref, dres_ref, dx_ref, dsc_ref, dsh_ref, dg_ref):
        @pl.when(pl.program_id(0) == 0)
        def _():
            dsc_ref[...] = jnp.zeros_like(dsc_ref)
            dsh_ref[...] = jnp.zeros_like(dsh_ref)
            dg_ref[...] = jnp.zeros_like(dg_ref)

        xv = x_ref[...]
        r = lax.rsqrt(jnp.mean(xv * xv, axis=-1, keepdims=True) + EPS)
        xhat = xv * r
        gain_v = g_ref[...]
        dhv = dh_ref[...]
        dsc_ref[...] += jnp.sum(dhv * (xhat * gain_v), axis=0, keepdims=True)
        dsh_ref[...] += jnp.sum(dhv, axis=0, keepdims=True)
        dxn = dhv * (1.0 + sc_ref[...])
        dg_ref[...] += jnp.sum(dxn * xhat, axis=0, keepdims=True)
        dxhat = dxn * gain_v
        dx_ref[...] = dres_ref[...] + r * (dxhat - xhat * jnp.mean(dxhat * xhat, axis=-1, keepdims=True))

    row = pl.BlockSpec((tm, D), lambda i: (i, 0))
    vec = pl.BlockSpec((1, D), lambda i: (0, 0))
    return _call(body, name=name, grid=(S // tm,), in_specs=[row, vec, vec, row, row], out_specs=(row, vec, vec, vec),
                 out_shape=(_sds((S, D), F32), _sds((1, D), F32), _sds((1, D), F32), _sds((1, D), F32)),
                 sem=("arbitrary",))(x, gain, sc, dh, dres)


def _gate_bwd(dres, y, gate, *, name):
    S, D = dres.shape
    tm = _row_tile(S, 512)

    def body(d_ref, y_ref, g_ref, dy_ref, dg_ref, db_ref):
        @pl.when(pl.program_id(0) == 0)
        def _():
            dg_ref[...] = jnp.zeros_like(dg_ref)
            db_ref[...] = jnp.zeros_like(db_ref)

        d = d_ref[...]
        dy = d * g_ref[...]
        dy_ref[...] = dy.astype(BF16)
        dg_ref[...] += jnp.sum(d * y_ref[...].astype(F32), axis=0, keepdims=True)
        db_ref[...] += jnp.sum(dy, axis=0, keepdims=True)

    row = pl.BlockSpec((tm, D), lambda i: (i, 0))
    vec = pl.BlockSpec((1, D), lambda i: (0, 0))
    return _call(body, name=name, grid=(S // tm,), in_specs=[row, row, vec], out_specs=(row, vec, vec),
                 out_shape=(_sds((S, D), BF16), _sds((1, D), F32), _sds((1, D), F32)),
                 sem=("arbitrary",))(dres, y, gate)


def _final_loss(x, gain, target, *, name):
    S, D = x.shape
    tm = _row_tile(S, 512)

    def body(x_ref, g_ref, t_ref, loss_ref, dx_ref, dg_ref):
        @pl.when(pl.program_id(0) == 0)
        def _():
            loss_ref[...] = jnp.zeros_like(loss_ref)
            dg_ref[...] = jnp.zeros_like(dg_ref)

        xv = x_ref[...]
        r = lax.rsqrt(jnp.mean(xv * xv, axis=-1, keepdims=True) + EPS)
        xhat = xv * r
        gv = g_ref[...]
        err = xhat * gv - t_ref[...]
        row_loss = jnp.mean(err * err, axis=-1, keepdims=True)
        loss_ref[...] += 0.5 * jnp.sum(row_loss, axis=0, keepdims=True)
        dy = err * (1.0 / D)
        dg_ref[...] += jnp.sum(dy * xhat, axis=0, keepdims=True)
        dxhat = dy * gv
        dx_ref[...] = r * (dxhat - xhat * jnp.mean(dxhat * xhat, axis=-1, keepdims=True))

    row = pl.BlockSpec((tm, D), lambda i: (i, 0))
    vec = pl.BlockSpec((1, D), lambda i: (0, 0))
    one = pl.BlockSpec((1, 1), lambda i: (0, 0))
    return _call(body, name=name, grid=(S // tm,), in_specs=[row, vec, row], out_specs=(one, row, vec),
                 out_shape=(_sds((1, 1), F32), _sds((S, D), F32), _sds((1, D), F32)),
                 sem=("arbitrary",))(x, gain, target)


def _pick_tm(M, bytes_per_row, fixed_bytes):
    for tm in (1024, 512, 256, 128):
        if M % tm == 0 and 2 * tm * bytes_per_row + fixed_bytes <= VMEM_BLOCK_BUDGET:
            return tm
    return _row_tile(M, 128)


def _mm_nn(a, w, *, name, layer=None, bias=None, relu2=False, res=None, gate=None, out_dtype=BF16):
    M, K = a.shape
    if w.ndim == 4:
        nj, ns = w.shape[0], w.shape[3]
        w_spec = pl.BlockSpec((None, None, K, ns), lambda i, j: (j, layer, 0, 0))
    else:
        nj, ns = 1, w.shape[1]
        w_spec = pl.BlockSpec((K, ns), lambda i, j: (0, 0))
    N = nj * ns
    residual = res is not None
    out_bytes = (4 + 4 + 2) if residual else jnp.dtype(out_dtype).itemsize
    tm = _pick_tm(M, K * a.dtype.itemsize + ns * out_bytes, 2 * K * ns * 2)

    def body(*refs):
        it = iter(refs)
        a_ref, w_ref = next(it), next(it)
        b_ref = next(it) if bias is not None else None
        res_ref, gate_ref = (next(it), next(it)) if residual else (None, None)
        out_ref = next(it)
        av = a_ref[...]
        if relu2:
            av = jnp.square(jnp.maximum(av.astype(F32), 0.0))
        acc = jnp.dot(av.astype(BF16), w_ref[...], preferred_element_type=F32)
        if b_ref is not None:
            acc = acc + b_ref[...]
        if residual:
            raw_ref = next(it)
            raw_ref[...] = acc.astype(BF16)
            out_ref[...] = res_ref[...] + gate_ref[...] * acc
        else:
            out_ref[...] = acc.astype(out_dtype)

    tile = pl.BlockSpec((tm, ns), lambda i, j: (i, j))
    vec = pl.BlockSpec((1, ns), lambda i, j: (0, j))
    in_specs, args = [pl.BlockSpec((tm, K), lambda i, j: (i, 0)), w_spec], [a, w]
    if bias is not None:
        in_specs.append(vec)
        args.append(bias)
    if residual:
        in_specs += [tile, vec]
        args += [res, gate]
        out_specs = (tile, tile)
        out_shape = (_sds((M, N), F32), _sds((M, N), BF16))
    else:
        out_specs = tile
        out_shape = _sds((M, N), out_dtype)
    return _call(body, name=name, grid=(M // tm, nj), in_specs=in_specs, out_specs=out_specs, out_shape=out_shape,
                 sem=("parallel", "parallel"))(*args)


def _mm_nt(g, w, *, name, layer=None, z=None, out_dtype=F32):
    M, N = g.shape
    col = w.ndim == 4
    if col:
        nsh, _, K, ns = w.shape
        w_spec = pl.BlockSpec((nsh, None, K, ns), lambda i: (0, layer, 0, 0))
    else:
        K = w.shape[0]
        w_spec = pl.BlockSpec((K, N), lambda i: (0, 0))
    kc = min(K, 1024)
    obytes = jnp.dtype(out_dtype).itemsize
    tm = _pick_tm(M, N * g.dtype.itemsize + K * obytes + (K * 2 if z is not None else 0), 2 * K * N * 2 + 512 * K * 4)

    def body(*refs):
        it = iter(refs)
        g_ref, w_ref = next(it), next(it)
        z_ref = next(it) if z is not None else None
        out_ref = next(it)
        if col:
            acc = None
            for d in range(nsh):
                part = lax.dot_general(g_ref[:, d * ns:(d + 1) * ns].astype(BF16), w_ref[d], NT_DIMS,
                                       preferred_element_type=F32)
                acc = part if acc is None else acc + part
            out_ref[...] = acc.astype(out_dtype)
        else:
            gb = g_ref[...].astype(BF16)
            for cki in range(K // kc):
                cols = slice(cki * kc, (cki + 1) * kc)
                part = lax.dot_general(gb, w_ref[cols, :], NT_DIMS, preferred_element_type=F32)
                if z_ref is not None:
                    part = part * (2.0 * jnp.maximum(z_ref[:, cols].astype(F32), 0.0))
                out_ref[:, cols] = part.astype(out_dtype)

    in_specs, args = [pl.BlockSpec((tm, N), lambda i: (i, 0)), w_spec], [g, w]
    if z is not None:
        in_specs.append(pl.BlockSpec((tm, K), lambda i: (i, 0)))
        args.append(z)
    return _call(body, name=name, grid=(M // tm,), in_specs=in_specs, out_specs=pl.BlockSpec((tm, K), lambda i: (i, 0)),
                 out_shape=_sds((M, K), out_dtype), sem=("parallel",))(*args)


def _mm_tn(a, g, *, name, col_shards=None, relu2=False):
    M, K = a.shape
    N = g.shape[1]
    acc_budget = 8 * 1024 * 1024
    tm = _row_tile(M, 512)
    nm = M // tm
    if col_shards:
        ns = N // col_shards
        spc = col_shards
        while spc > 1 and K * ns * spc * 4 > acc_budget:
            spc //= 2
        grid = (col_shards // spc, nm)
        a_spec = pl.BlockSpec((tm, K), lambda c, m: (m, 0))
        g_spec = pl.BlockSpec((tm, spc * ns), lambda c, m: (m, c))
        out_spec = pl.BlockSpec((spc, K, ns), lambda c, m: (c, 0, 0))
        out_shape = _sds((col_shards, K, ns), BF16)
        acc_shape = (K, spc * ns)
    else:
        tk = K
        while tk > 128 and tk * N * 4 > acc_budget:
            tk //= 2
        grid = (K // tk, nm)
        a_spec = pl.BlockSpec((tm, tk), lambda c, m: (m, c))
        g_spec = pl.BlockSpec((tm, N), lambda c, m: (m, 0))
        out_spec = pl.BlockSpec((tk, N), lambda c, m: (c, 0))
        out_shape = _sds((K, N), BF16)
        acc_shape = (tk, N)

    def body(a_ref, g_ref, out_ref, acc_ref):
        m = pl.program_id(1)

        @pl.when(m == 0)
        def _():
            acc_ref[...] = jnp.zeros_like(acc_ref)

        av = a_ref[...]
        if relu2:
            av = jnp.square(jnp.maximum(av.astype(F32), 0.0))
        acc_ref[...] += lax.dot_general(av.astype(BF16), g_ref[...].astype(BF16), TN_DIMS, preferred_element_type=F32)

        @pl.when(m == nm - 1)
        def _():
            if col_shards:
                for s in range(spc):
                    out_ref[s] = acc_ref[:, s * ns:(s + 1) * ns].astype(BF16)
            else:
                out_ref[...] = acc_ref[...].astype(BF16)

    return _call(body, name=name, grid=grid, in_specs=[a_spec, g_spec], out_specs=out_spec, out_shape=out_shape,
                 scratch=[pltpu.VMEM(acc_shape, F32)], sem=("parallel", "arbitrary"))(a, g)


CONV_TILE = 256
CONV_GROUP = 16


def _glu(u, ch):
    return u[:, :ch] * _sigmoid(u[:, ch:])


def _channel_mean(v):
    n = v.shape[1] * v.shape[2]
    return jnp.sum(jnp.sum(v, axis=2, keepdims=True), axis=1, keepdims=True) * (1.0 / n)


def _fill_glu(buf, u_ref, uh_ref, ch, tile):
    first = pl.program_id(0) == 0
    buf[0:HALO] = jnp.where(first, 0.0, _glu(uh_ref[...], ch))
    buf[HALO:HALO + tile] = _glu(u_ref[...], ch)


def _dwconv_group(win, w_ref, rows):
    acc = None
    for k in range(CONV_WIDTH):
        term = win[HALO - CONV_WIDTH + 1 + k:HALO - CONV_WIDTH + 1 + k + rows] * w_ref[k]
        acc = term if acc is None else acc + term
    return acc


def _conv_specs(S, ch, tile):
    per = tile // HALO
    u_spec = pl.BlockSpec((tile, 2 * ch, LANES), lambda i: (i, 0, 0))
    uh_spec = pl.BlockSpec((HALO, 2 * ch, LANES), lambda i: (jnp.maximum(i * per - 1, 0), 0, 0))
    x_spec = pl.BlockSpec((tile, ch, LANES), lambda i: (i, 0, 0))
    xn_spec = pl.BlockSpec((HALO, ch, LANES), lambda i: (jnp.minimum((i + 1) * per, S // HALO - 1), 0, 0))
    w_spec = pl.BlockSpec((CONV_WIDTH, ch, LANES), lambda i: (0, 0, 0))
    v_spec = pl.BlockSpec((1, ch, LANES), lambda i: (0, 0, 0))
    return u_spec, uh_spec, x_spec, xn_spec, w_spec, v_spec


def _conv_mid_fwd(u3, w3, bdw3, lng3, lnb3, *, name):
    S, ch2, _ = u3.shape
    ch = ch2 // 2
    tile = _row_tile(S, CONV_TILE)
    rows = _row_tile(tile, CONV_GROUP)
    u_spec, uh_spec, x_spec, _, w_spec, v_spec = _conv_specs(S, ch, tile)

    def body(u_ref, uh_ref, w_ref, b_ref, g_ref, bb_ref, s_ref, buf):
        _fill_glu(buf, u_ref, uh_ref, ch, tile)

        def group(r, carry):
            t0 = r * rows
            dwo = _dwconv_group(buf[pl.ds(t0, rows + HALO)], w_ref, rows) + b_ref[...]
            mu = _channel_mean(dwo)
            cen = dwo - mu
            rstd = lax.rsqrt(_channel_mean(cen * cen) + EPS)
            ln = cen * rstd * g_ref[...] + bb_ref[...]
            s_ref[pl.ds(t0, rows)] = ln * _sigmoid(ln)
            return carry

        lax.fori_loop(0, tile // rows, group, 0)

    return _call(body, name=name, grid=(S // tile,), in_specs=[u_spec, uh_spec, w_spec, v_spec, v_spec, v_spec],
                 out_specs=x_spec, out_shape=_sds((S, ch, LANES), F32), scratch=[pltpu.VMEM((tile + HALO, ch, LANES), F32)],
                 sem=("parallel",))(u3, u3, w3, bdw3, lng3, lnb3)


def _conv_mid_bwd_ln(u3, ds3, w3, bdw3, lng3, lnb3, *, name):
    S, ch2, _ = u3.shape
    ch = ch2 // 2
    tile = _row_tile(S, CONV_TILE)
    rows = _row_tile(tile, CONV_GROUP)
    u_spec, uh_spec, x_spec, _, w_spec, v_spec = _conv_specs(S, ch, tile)

    def body(u_ref, uh_ref, ds_ref, w_ref, b_ref, g_ref, bb_ref, ddw_ref, dg_ref, db_ref, dbdw_ref, buf):
        @pl.when(pl.program_id(0) == 0)
        def _():
            dg_ref[...] = jnp.zeros_like(dg_ref)
            db_ref[...] = jnp.zeros_like(db_ref)
            dbdw_ref[...] = jnp.zeros_like(dbdw_ref)

        _fill_glu(buf, u_ref, uh_ref, ch, tile)

        def group(r, carry):
            t0 = r * rows
            dwo = _dwconv_group(buf[pl.ds(t0, rows + HALO)], w_ref, rows) + b_ref[...]
            mu = _channel_mean(dwo)
            cen = dwo - mu
            rstd = lax.rsqrt(_channel_mean(cen * cen) + EPS)
            nrm = cen * rstd
            gv = g_ref[...]
            ln = nrm * gv + bb_ref[...]
            sg = _sigmoid(ln)
            dln = ds_ref[pl.ds(t0, rows)] * (sg * (1.0 + ln * (1.0 - sg)))
            dg_ref[...] += jnp.sum(dln * nrm, axis=0, keepdims=True)
            db_ref[...] += jnp.sum(dln, axis=0, keepdims=True)
            dn = dln * gv
            ddw = rstd * (dn - _channel_mean(dn) - nrm * _channel_mean(dn * nrm))
            dbdw_ref[...] += jnp.sum(ddw, axis=0, keepdims=True)
            ddw_ref[pl.ds(t0, rows)] = ddw
            return carry

        lax.fori_loop(0, tile // rows, group, 0)

    vsh = _sds((1, ch, LANES), F32)
    return _call(body, name=name, grid=(S // tile,), in_specs=[u_spec, uh_spec, x_spec, w_spec, v_spec, v_spec, v_spec],
                 out_specs=(x_spec, v_spec, v_spec, v_spec), out_shape=(_sds((S, ch, LANES), F32), vsh, vsh, vsh),
                 scratch=[pltpu.VMEM((tile + HALO, ch, LANES), F32)], sem=("arbitrary",))(u3, u3, ds3, w3, bdw3, lng3, lnb3)


def _conv_mid_bwd_dw(u3, ddw3, w3, *, name):
    S, ch2, _ = u3.shape
    ch = ch2 // 2
    tile = _row_tile(S, CONV_TILE)
    rows = _row_tile(tile, CONV_GROUP)
    last = S // tile - 1
    u_spec, uh_spec, x_spec, xn_spec, w_spec, _ = _conv_specs(S, ch, tile)
    b_spec = pl.BlockSpec((1, 2 * ch, LANES), lambda i: (0, 0, 0))

    def body(u_ref, uh_ref, d_ref, dn_ref, w_ref, du_ref, dw_ref, db_ref, gbuf, dbuf):
        @pl.when(pl.program_id(0) == 0)
        def _():
            dw_ref[...] = jnp.zeros_like(dw_ref)
            db_ref[...] = jnp.zeros_like(db_ref)

        _fill_glu(gbuf, u_ref, uh_ref, ch, tile)
        dbuf[0:tile] = d_ref[...]
        dbuf[tile:tile + HALO] = jnp.where(pl.program_id(0) == last, 0.0, dn_ref[...])

        def group(r, carry):
            t0 = r * rows
            dwin = dbuf[pl.ds(t0, rows + HALO)]
            gwin = gbuf[pl.ds(t0, rows + HALO)]
            ddw = dwin[0:rows]
            dglu = None
            for k in range(CONV_WIDTH):
                back = CONV_WIDTH - 1 - k
                term = dwin[back:back + rows] * w_ref[k]
                dglu = term if dglu is None else dglu + term
                lo = HALO - CONV_WIDTH + 1 + k
                dw_ref[k] += jnp.sum(ddw * gwin[lo:lo + rows], axis=0)
            uv = u_ref[pl.ds(t0, rows)]
            av, sg = uv[:, :ch], _sigmoid(uv[:, ch:])
            da = dglu * sg
            dg = dglu * av * sg * (1.0 - sg)
            du_ref[pl.ds(t0, rows), 0:ch] = da
            du_ref[pl.ds(t0, rows), ch:2 * ch] = dg
            db_ref[:, 0:ch] += jnp.sum(da, axis=0, keepdims=True)
            db_ref[:, ch:2 * ch] += jnp.sum(dg, axis=0, keepdims=True)
            return carry

        lax.fori_loop(0, tile // rows, group, 0)

    return _call(body, name=name, grid=(S // tile,), in_specs=[u_spec, uh_spec, x_spec, xn_spec, w_spec],
                 out_specs=(u_spec, w_spec, b_spec),
                 out_shape=(_sds((S, 2 * ch, LANES), F32), _sds((CONV_WIDTH, ch, LANES), F32), _sds((1, 2 * ch, LANES), F32)),
                 scratch=[pltpu.VMEM((tile + HALO, ch, LANES), F32), pltpu.VMEM((tile + HALO, ch, LANES), F32)],
                 sem=("arbitrary",))(u3, u3, ddw3, ddw3, w3)


def _ret_tables(S, dk):
    B = min(RET_BLOCK, S)
    lg = jnp.log(1.0 - 2.0 ** (-5.0 - jnp.arange(RET_HEADS, dtype=F32)))
    idx = jnp.arange(B, dtype=F32)
    diff = idx[:, None] - idx[None, :]
    cq, ck = (jnp.arange(B) // CHUNK)[:, None], (jnp.arange(B) // CHUNK)[None, :]
    dist = jnp.where(cq == ck, jnp.abs(diff), diff)
    mask = jnp.where(ck <= cq, jnp.exp(lg[:, None, None] * dist[None]), 0.0)
    xi = jnp.exp(lg[:, None] * (idx + 1.0))[..., None]
    zeta = jnp.exp(lg[:, None] * (B - 1.0 - idx))[..., None]
    gam = jnp.broadcast_to(jnp.exp(lg * B)[:, None, None], (RET_HEADS, 8, LANES))
    pos = jnp.arange(S, dtype=F32)
    inv = ROPE_BASE ** (-jnp.arange(0, dk, 2, dtype=F32) / dk)
    ang = pos[:, None] * inv[None, :]
    return dict(B=B, mask=mask, xi=xi, zeta=zeta, gam=gam, cos=jnp.cos(ang), sin=jnp.sin(ang))


def _rope(v, cs, sn):
    half = v.shape[1] // 2
    v1, v2 = v[:, :half], v[:, half:]
    return jnp.concatenate([v1 * cs - v2 * sn, v2 * cs + v1 * sn], axis=-1)


def _rope_t(d, cs, sn):
    half = d.shape[1] // 2
    d1, d2 = d[:, :half], d[:, half:]
    return jnp.concatenate([d1 * cs + d2 * sn, d2 * cs - d1 * sn], axis=-1)


def _dot(a, b):
    return jnp.dot(a.astype(BF16), b.astype(BF16), preferred_element_type=F32)


def _dot_nt(a, b):
    return lax.dot_general(a.astype(BF16), b.astype(BF16), NT_DIMS, preferred_element_type=F32)


def _dot_tn(a, b):
    return lax.dot_general(a.astype(BF16), b.astype(BF16), TN_DIMS, preferred_element_type=F32)


def _ret_specs(S, D, B, RB, reverse):
    dk, dv = D // RET_HEADS, 2 * D // RET_HEADS
    nb = S // RB
    blk = (lambda ib: nb - 1 - ib) if reverse else (lambda ib: ib)
    q = pl.BlockSpec((RB, dk), lambda h, ib: (blk(ib), h))
    k = pl.BlockSpec((RB, dk), lambda h, ib: (blk(ib), RET_HEADS + h))
    v = pl.BlockSpec((RB, dv), lambda h, ib: (blk(ib), RET_HEADS + h))
    gate = pl.BlockSpec((RB, dv), lambda h, ib: (blk(ib), 2 * RET_HEADS + h))
    yv = pl.BlockSpec((RB, dv), lambda h, ib: (blk(ib), h))
    rope = pl.BlockSpec((RB, dk // 2), lambda h, ib: (blk(ib), 0))
    mask = pl.BlockSpec((None, B, B), lambda h, ib: (h, 0, 0))
    dec = pl.BlockSpec((None, B, 1), lambda h, ib: (h, 0, 0))
    gam = pl.BlockSpec((None, 8, LANES), lambda h, ib: (h, 0, 0))
    gn = pl.BlockSpec((1, dv), lambda h, ib: (0, h))
    return dict(q=q, k=k, v=v, gate=gate, yv=yv, rope=rope, mask=mask, dec=dec, gam=gam, gn=gn)


def _group_norm(yr, gv, bv):
    mu = jnp.mean(yr, axis=-1, keepdims=True)
    cen = yr - mu
    rstd = lax.rsqrt(jnp.mean(cen * cen, axis=-1, keepdims=True) + EPS)
    nrm = cen * rstd
    return nrm, rstd, nrm * gv + bv


def _ret_fwd(proj, tb, gng, gnb, *, name):
    S, D = proj.shape[0], proj.shape[1] // 6
    dk, dv = D // RET_HEADS, 2 * D // RET_HEADS
    B = tb["B"]
    RB = _row_tile(S, 2 * B)
    nsub = RB // B
    sp = _ret_specs(S, D, B, RB, False)
    scale = dk ** -0.5

    def body(q_ref, k_ref, v_ref, gt_ref, cos_ref, sin_ref, mask_ref, xi_ref, zeta_ref, gam_ref, gng_ref, gnb_ref,
             yr_ref, yg_ref, state):
        @pl.when(pl.program_id(1) == 0)
        def _():
            state[...] = jnp.zeros_like(state)

        for sb in range(nsub):
            rows = slice(sb * B, (sb + 1) * B)
            cs, sn = cos_ref[rows, :], sin_ref[rows, :]
            q = _rope(q_ref[rows, :].astype(F32), cs, sn)
            k = _rope(k_ref[rows, :].astype(F32), cs, sn) * scale
            vb = v_ref[rows, :]
            p = _dot_nt(q, k) * mask_ref[...]
            st = state[...]
            yr = _dot(p, vb) + _dot(q * xi_ref[...], st)
            state[...] = st * gam_ref[0:1, 0:1] + _dot_tn(k * zeta_ref[...], vb)
            _, _, gn = _group_norm(yr, gng_ref[...], gnb_ref[...])
            gt = gt_ref[rows, :].astype(F32)
            yr_ref[rows, :] = yr.astype(BF16)
            yg_ref[rows, :] = (gt * _sigmoid(gt) * gn).astype(BF16)

    return _call(body, name=name, grid=(RET_HEADS, S // RB),
                 in_specs=[sp["q"], sp["k"], sp["v"], sp["gate"], sp["rope"], sp["rope"], sp["mask"], sp["dec"], sp["dec"],
                           sp["gam"], sp["gn"], sp["gn"]],
                 out_specs=(sp["yv"], sp["yv"]), out_shape=(_sds((S, 2 * D), BF16), _sds((S, 2 * D), BF16)),
                 scratch=[pltpu.VMEM((dk, dv), F32)], sem=("parallel", "arbitrary"))(
                     proj, proj, proj, proj, tb["cos"], tb["sin"], tb["mask"], tb["xi"], tb["zeta"], tb["gam"], gng, gnb)


def _ret_bwd_q(proj, yr, dyg, tb, gng, gnb, *, name):
    S, D = proj.shape[0], proj.shape[1] // 6
    dk, dv = D // RET_HEADS, 2 * D // RET_HEADS
    B = tb["B"]
    RB = _row_tile(S, 2 * B)
    nsub = RB // B
    sp = _ret_specs(S, D, B, RB, False)
    scale = dk ** -0.5

    def body(q_ref, k_ref, v_ref, gt_ref, yr_ref, dyg_ref, cos_ref, sin_ref, mask_ref, xi_ref, zeta_ref, gam_ref,
             gng_ref, gnb_ref, dq_ref, dgt_ref, dyr_ref, dgg_ref, dgb_ref, state):
        @pl.when(pl.program_id(1) == 0)
        def _():
            state[...] = jnp.zeros_like(state)
            dgg_ref[...] = jnp.zeros_like(dgg_ref)
            dgb_ref[...] = jnp.zeros_like(dgb_ref)

        for sb in range(nsub):
            rows = slice(sb * B, (sb + 1) * B)
            cs, sn = cos_ref[rows, :], sin_ref[rows, :]
            q = _rope(q_ref[rows, :].astype(F32), cs, sn)
            k = _rope(k_ref[rows, :].astype(F32), cs, sn) * scale
            vb = v_ref[rows, :]
            gv = gng_ref[...]
            nrm, rstd, gn = _group_norm(yr_ref[rows, :].astype(F32), gv, gnb_ref[...])
            gt = gt_ref[rows, :].astype(F32)
            sg = _sigmoid(gt)
            dyg = dyg_ref[rows, :].astype(F32)
            dgt_ref[rows, :] = (dyg * gn * (sg * (1.0 + gt * (1.0 - sg)))).astype(BF16)
            dgn = dyg * (gt * sg)
            dgg_ref[...] += jnp.sum(dgn * nrm, axis=0, keepdims=True)
            dgb_ref[...] += jnp.sum(dgn, axis=0, keepdims=True)
            dn = dgn * gv
            dyr = rstd * (dn - jnp.mean(dn, axis=-1, keepdims=True) - nrm * jnp.mean(dn * nrm, axis=-1, keepdims=True))
            dyr_ref[rows, :] = dyr.astype(BF16)
            dp = _dot_nt(dyr, vb) * mask_ref[...]
            st = state[...]
            dq = _dot(dp, k) + _dot_nt(dyr, st) * xi_ref[...]
            dq_ref[rows, :] = _rope_t(dq, cs, sn).astype(BF16)
            state[...] = st * gam_ref[0:1, 0:1] + _dot_tn(k * zeta_ref[...], vb)

    qout = pl.BlockSpec((RB, dk), lambda h, ib: (ib, h))
    return _call(body, name=name, grid=(RET_HEADS, S // RB),
                 in_specs=[sp["q"], sp["k"], sp["v"], sp["gate"], sp["yv"], sp["yv"], sp["rope"], sp["rope"], sp["mask"],
                           sp["dec"], sp["dec"], sp["gam"], sp["gn"], sp["gn"]],
                 out_specs=(qout, sp["yv"], sp["yv"], sp["gn"], sp["gn"]),
                 out_shape=(_sds((S, D), BF16), _sds((S, 2 * D), BF16), _sds((S, 2 * D), BF16), _sds((1, 2 * D), F32),
                            _sds((1, 2 * D), F32)),
                 scratch=[pltpu.VMEM((dk, dv), F32)], sem=("parallel", "arbitrary"))(
                     proj, proj, proj, proj, yr, dyg, tb["cos"], tb["sin"], tb["mask"], tb["xi"], tb["zeta"], tb["gam"],
                     gng, gnb)


def _ret_bwd_kv(proj, dyr, tb, *, name):
    S, D = proj.shape[0], proj.shape[1] // 6
    dk, dv = D // RET_HEADS, 2 * D // RET_HEADS
    B = tb["B"]
    RB = _row_tile(S, 2 * B)
    nsub = RB // B
    nb = S // RB
    sp = _ret_specs(S, D, B, RB, True)
    scale = dk ** -0.5

    def body(q_ref, k_ref, v_ref, dyr_ref, cos_ref, sin_ref, mask_ref, xi_ref, zeta_ref, gam_ref, dk_ref, dv_ref, dstate):
        @pl.when(pl.program_id(1) == 0)
        def _():
            dstate[...] = jnp.zeros_like(dstate)

        for sb in reversed(range(nsub)):
            rows = slice(sb * B, (sb + 1) * B)
            cs, sn = cos_ref[rows, :], sin_ref[rows, :]
            q = _rope(q_ref[rows, :].astype(F32), cs, sn)
            k = _rope(k_ref[rows, :].astype(F32), cs, sn) * scale
            vb = v_ref[rows, :]
            dyr = dyr_ref[rows, :]
            mk = mask_ref[...]
            p = _dot_nt(q, k) * mk
            dp = _dot_nt(dyr, vb) * mk
            ds = dstate[...]
            zt = zeta_ref[...]
            dkr = _dot_tn(dp, q) + _dot_nt(vb, ds) * zt
            dk_ref[rows, :] = _rope_t(dkr * scale, cs, sn).astype(BF16)
            dv_ref[rows, :] = (_dot_tn(p, dyr) + _dot(k * zt, ds)).astype(BF16)
            dstate[...] = ds * gam_ref[0:1, 0:1] + _dot_tn(q * xi_ref[...], dyr)

    kout = pl.BlockSpec((RB, dk), lambda h, ib: (nb - 1 - ib, h))
    return _call(body, name=name, grid=(RET_HEADS, nb),
                 in_specs=[sp["q"], sp["k"], sp["v"], sp["yv"], sp["rope"], sp["rope"], sp["mask"], sp["dec"], sp["dec"],
                           sp["gam"]],
                 out_specs=(kout, sp["yv"]), out_shape=(_sds((S, D), BF16), _sds((S, 2 * D), BF16)),
                 scratch=[pltpu.VMEM((dk, dv), F32)], sem=("parallel", "arbitrary"))(
                     proj, proj, proj, dyr, tb["cos"], tb["sin"], tb["mask"], tb["xi"], tb["zeta"], tb["gam"])


def _ada_fwd(c_all, ada_w, *, name):
    L, D, ns = ada_w.shape

    def body(c_ref, w_ref, out_ref):
        cv = c_ref[...]
        cond = cv * _sigmoid(cv)
        out_ref[...] = jnp.dot(cond.astype(BF16), w_ref[...].astype(BF16), preferred_element_type=F32)

    return _call(body, name=name, grid=(L,), in_specs=[pl.BlockSpec((NDEV, D), lambda l: (0, 0)),
                                                      pl.BlockSpec((None, D, ns), lambda l: (l, 0, 0))],
                 out_specs=pl.BlockSpec((None, NDEV, ns), lambda l: (l, 0, 0)), out_shape=_sds((L, NDEV, ns), F32),
                 sem=("parallel",))(c_all, ada_w)


def _ada_bwd(c_all, dmod_cols, *, name):
    L, _, ns = dmod_cols.shape
    D = c_all.shape[1]

    def body(c_ref, d_ref, out_ref):
        cv = c_ref[...]
        cond = cv * _sigmoid(cv)
        out_ref[...] = lax.dot_general(cond.astype(BF16), d_ref[...].astype(BF16), TN_DIMS, preferred_element_type=F32)

    return _call(body, name=name, grid=(L,), in_specs=[pl.BlockSpec((NDEV, D), lambda l: (0, 0)),
                                                      pl.BlockSpec((None, NDEV, ns), lambda l: (l, 0, 0))],
                 out_specs=pl.BlockSpec((None, D, ns), lambda l: (l, 0, 0)), out_shape=_sds((L, D, ns), F32),
                 sem=("parallel",))(c_all, dmod_cols)


def _adamw(w, m, v, parts, *, name):
    shape = w.shape
    cols = shape[-1]
    rows = w.size // cols
    n = parts.shape[0]
    tr = rows
    for cand in (256, 128, 64, 32, 16, 8):
        if rows % cand == 0:
            tr = cand
            break
    c1 = 1.0 - ADAM_B1 ** ADAM_STEP
    c2 = 1.0 - ADAM_B2 ** ADAM_STEP

    def body(w_ref, m_ref, v_ref, p_ref, g_ref, d_ref, m2_ref, v2_ref):
        g = p_ref[0].astype(F32)
        for i in range(1, n):
            g = g + p_ref[i].astype(F32)
        m2 = ADAM_B1 * m_ref[...] + (1.0 - ADAM_B1) * g
        v2 = ADAM_B2 * v_ref[...] + (1.0 - ADAM_B2) * (g * g)
        g_ref[...] = g
        m2_ref[...] = m2
        v2_ref[...] = v2
        d_ref[...] = -ADAM_LR * ((m2 / c1) / (jnp.sqrt(v2 / c2) + ADAM_EPS) + ADAM_WD * w_ref[...])

    mat = pl.BlockSpec((tr, cols), lambda i: (i, 0))
    outs = _call(body, name=name, grid=(rows // tr,), in_specs=[mat, mat, mat, pl.BlockSpec((n, tr, cols), lambda i: (0, i, 0))],
                 out_specs=(mat, mat, mat, mat), out_shape=tuple(_sds((rows, cols), F32) for _ in range(4)),
                 sem=("parallel",))(w.reshape(rows, cols), m.reshape(rows, cols), v.reshape(rows, cols),
                                    parts.reshape(n, rows, cols))
    return tuple(o.reshape(shape) for o in outs)


SMALL = ("ada_b", "norm_mix_g", "norm_mlp_g", "conv_b_pw1", "conv_b_dw", "conv_ln_g", "conv_ln_b", "conv_b_pw2",
         "final_norm_g")
WEIGHTS = ("ada_w", "ada_b", "norm_mix_g", "norm_mlp_g", "conv_w_pw1", "conv_b_pw1", "conv_w_dw", "conv_b_dw", "conv_ln_g",
           "conv_ln_b", "conv_w_pw2", "conv_b_pw2", "ret_w_in", "ret_gn_g", "ret_gn_b", "ret_w_out", "mlp_w1", "mlp_w2",
           "final_norm_g")


def kernel(x, c, ada_w, ada_b, norm_mix_g, norm_mlp_g, conv_w_pw1, conv_b_pw1, conv_w_dw, conv_b_dw, conv_ln_g, conv_ln_b, conv_w_pw2, conv_b_pw2, ret_w_in, ret_gn_g, ret_gn_b, ret_w_out, mlp_w1, mlp_w2, final_norm_g, loss_target, m_ada_w, m_ada_b, m_norm_mix_g, m_norm_mlp_g, m_conv_w_pw1, m_conv_b_pw1, m_conv_w_dw, m_conv_b_dw, m_conv_ln_g, m_conv_ln_b, m_conv_w_pw2, m_conv_b_pw2, m_ret_w_in, m_ret_gn_g, m_ret_gn_b, m_ret_w_out, m_mlp_w1, m_mlp_w2, m_final_norm_g, v_ada_w, v_ada_b, v_norm_mix_g, v_norm_mlp_g, v_conv_w_pw1, v_conv_b_pw1, v_conv_w_dw, v_conv_b_dw, v_conv_ln_g, v_conv_ln_b, v_conv_w_pw2, v_conv_b_pw2, v_ret_w_in, v_ret_gn_g, v_ret_gn_b, v_ret_w_out, v_mlp_w1, v_mlp_w2, v_final_norm_g):
    W = dict(ada_w=ada_w, ada_b=ada_b, norm_mix_g=norm_mix_g, norm_mlp_g=norm_mlp_g, conv_w_pw1=conv_w_pw1,
             conv_b_pw1=conv_b_pw1, conv_w_dw=conv_w_dw, conv_b_dw=conv_b_dw, conv_ln_g=conv_ln_g, conv_ln_b=conv_ln_b,
             conv_w_pw2=conv_w_pw2, conv_b_pw2=conv_b_pw2, ret_w_in=ret_w_in, ret_gn_g=ret_gn_g, ret_gn_b=ret_gn_b,
             ret_w_out=ret_w_out, mlp_w1=mlp_w1, mlp_w2=mlp_w2, final_norm_g=final_norm_g)
    Mo = dict(ada_w=m_ada_w, ada_b=m_ada_b, norm_mix_g=m_norm_mix_g, norm_mlp_g=m_norm_mlp_g, conv_w_pw1=m_conv_w_pw1,
              conv_b_pw1=m_conv_b_pw1, conv_w_dw=m_conv_w_dw, conv_b_dw=m_conv_b_dw, conv_ln_g=m_conv_ln_g,
              conv_ln_b=m_conv_ln_b, conv_w_pw2=m_conv_w_pw2, conv_b_pw2=m_conv_b_pw2, ret_w_in=m_ret_w_in,
              ret_gn_g=m_ret_gn_g, ret_gn_b=m_ret_gn_b, ret_w_out=m_ret_w_out, mlp_w1=m_mlp_w1, mlp_w2=m_mlp_w2,
              final_norm_g=m_final_norm_g)
    Vo = dict(ada_w=v_ada_w, ada_b=v_ada_b, norm_mix_g=v_norm_mix_g, norm_mlp_g=v_norm_mlp_g, conv_w_pw1=v_conv_w_pw1,
              conv_b_pw1=v_conv_b_pw1, conv_w_dw=v_conv_w_dw, conv_b_dw=v_conv_b_dw, conv_ln_g=v_conv_ln_g,
              conv_ln_b=v_conv_ln_b, conv_w_pw2=v_conv_w_pw2, conv_b_pw2=v_conv_b_pw2, ret_w_in=v_ret_w_in,
              ret_gn_g=v_ret_gn_g, ret_gn_b=v_ret_gn_b, ret_w_out=v_ret_w_out, mlp_w1=v_mlp_w1, mlp_w2=v_mlp_w2,
              final_norm_g=v_final_norm_g)

    S, D = x.shape[1], x.shape[2]
    CH = D // LANES
    n_conv, n_ret = conv_w_pw1.shape[0], ret_w_in.shape[0]
    me = 4 * lax.axis_index("x") + 2 * lax.axis_index("y") + lax.axis_index("c")
    xs = x.reshape(S, D)
    target = loss_target.reshape(S, D)

    gathered = _exchange(
        [[conv_w_pw1.astype(BF16)], [ret_w_in.astype(BF16)], [mlp_w1.astype(BF16)]]
        + [[conv_w_pw2[l].astype(BF16)] for l in range(n_conv)]
        + [[ret_w_out[l].astype(BF16)] for l in range(n_ret)]
        + [[mlp_w2[l].astype(BF16)] for l in range(DEPTH)]
        + [[conv_w_dw], [ret_gn_g], [ret_gn_b], [c]], gather=True, name="gather_weights")
    pw1_g, win_g, w1_g = gathered[0:3]
    pos = 3
    pw2_g = [gathered[pos + l].reshape(D, D) for l in range(n_conv)]
    pos += n_conv
    wout_g = [gathered[pos + l].reshape(2 * D, D) for l in range(n_ret)]
    pos += n_ret
    w2_g = [gathered[pos + l].reshape(4 * D, D) for l in range(DEPTH)]
    pos += DEPTH
    dw_g, gng_g, gnb_g, c_g = gathered[pos:pos + 4]
    dw3 = jnp.transpose(dw_g, (1, 2, 0, 3)).reshape(n_conv, CONV_WIDTH, CH, LANES)
    gng_full = jnp.transpose(gng_g, (1, 2, 0, 3)).reshape(n_ret, 1, 2 * D)
    gnb_full = jnp.transpose(gnb_g, (1, 2, 0, 3)).reshape(n_ret, 1, 2 * D)
    c_all = c_g.reshape(NDEV, D)

    mod_cols = _ada_fwd(c_all, ada_w, name="ada_fwd")
    mod_all = _exchange([[mod_cols]], gather=True, name="gather_mod")[0]
    mod = lax.dynamic_index_in_dim(mod_all, me, axis=2, keepdims=False)
    mod = jnp.transpose(mod, (1, 0, 2)).reshape(DEPTH, 6 * D) + ada_b
    mods = [[mod[i, j * D:(j + 1) * D].reshape(1, D) for j in range(6)] for i in range(DEPTH)]
    tb = _ret_tables(S, D // RET_HEADS)

    def vec(a):
        return a.reshape(1, -1)

    saved = []
    xcur = xs
    for i in range(DEPTH):
        sh1, sc1, g1, sh2, sc2, g2 = mods[i]
        j = i // 2
        st = dict(x_in=xcur)
        h = _norm_mod_fwd(xcur, vec(norm_mix_g[i]), sc1, sh1, name=f"norm_mix_fwd{i}")
        st["h"] = h
        if i % 2 == 0:
            u = _mm_nn(h, pw1_g, layer=j, bias=vec(conv_b_pw1[j]), out_dtype=F32, name=f"pw1_fwd{i}")
            u3 = u.reshape(S, 2 * CH, LANES)
            s3 = _conv_mid_fwd(u3, dw3[j], conv_b_dw[j].reshape(1, CH, LANES), conv_ln_g[j].reshape(1, CH, LANES),
                               conv_ln_b[j].reshape(1, CH, LANES), name=f"conv_mid_fwd{i}")
            s2 = s3.reshape(S, D)
            xcur, y_raw = _mm_nn(s2, pw2_g[j], bias=vec(conv_b_pw2[j]), res=xcur, gate=g1, name=f"pw2_fwd{i}")
            st.update(u3=u3, s2=s2, y_raw=y_raw)
        else:
            proj = _mm_nn(h, win_g, layer=j, name=f"ret_in_fwd{i}")
            yr, yg = _ret_fwd(proj, tb, gng_full[j], gnb_full[j], name=f"ret_fwd{i}")
            xcur, y_raw = _mm_nn(yg, wout_g[j], res=xcur, gate=g1, name=f"ret_out_fwd{i}")
            st.update(proj=proj, yr=yr, yg=yg, y_raw=y_raw)
        st["x_mid"] = xcur
        h2 = _norm_mod_fwd(xcur, vec(norm_mlp_g[i]), sc2, sh2, name=f"norm_mlp_fwd{i}")
        z = _mm_nn(h2, w1_g, layer=i, name=f"mlp1_fwd{i}")
        xcur, o_raw = _mm_nn(z, w2_g[i], relu2=True, res=xcur, gate=g2, name=f"mlp2_fwd{i}")
        st.update(h2=h2, z=z, o_raw=o_raw)
        saved.append(st)

    loss_local, dx, d_final_g = _final_loss(xcur, vec(final_norm_g), target, name="final_loss")
    loss = lax.psum(loss_local[0, 0], AXES)

    dmod_rows = [None] * DEPTH
    d_mix_g, d_mlp_g = [None] * DEPTH, [None] * DEPTH
    d_pw1, d_pw2, d_win, d_wout = [None] * n_conv, [None] * n_conv, [None] * n_ret, [None] * n_ret
    d_w1, d_w2 = [None] * DEPTH, [None] * DEPTH
    d_bpw1, d_bdw, d_lng, d_lnb, d_bpw2, d_dw = ([None] * n_conv for _ in range(6))
    d_gng, d_gnb = [None] * n_ret, [None] * n_ret
    for i in reversed(range(DEPTH)):
        sh1, sc1, g1, sh2, sc2, g2 = mods[i]
        j = i // 2
        st = saved[i]
        do, dg2, _ = _gate_bwd(dx, st["o_raw"], g2, name=f"mlp_gate_bwd{i}")
        dz = _mm_nt(do, w2_g[i], z=st["z"], out_dtype=BF16, name=f"mlp2_bwd_x{i}")
        d_w2[i] = _mm_tn(st["z"], do, relu2=True, name=f"mlp2_bwd_w{i}")
        dh2 = _mm_nt(dz, w1_g, layer=i, name=f"mlp1_bwd_x{i}")
        d_w1[i] = _mm_tn(st["h2"], dz, col_shards=NDEV, name=f"mlp1_bwd_w{i}")
        dx, dsc2, dsh2, d_mlp_g[i] = _norm_mod_bwd(st["x_mid"], vec(norm_mlp_g[i]), sc2, dh2, dx, name=f"norm_mlp_bwd{i}")
        dy, dg1, dby = _gate_bwd(dx, st["y_raw"], g1, name=f"mix_gate_bwd{i}")
        if i % 2 == 0:
            d_bpw2[j] = dby
            ds = _mm_nt(dy, pw2_g[j], name=f"pw2_bwd_x{i}")
            d_pw2[j] = _mm_tn(st["s2"], dy, name=f"pw2_bwd_w{i}")
            ddw3, dlg, dlb, dbd = _conv_mid_bwd_ln(
                st["u3"], ds.reshape(S, CH, LANES), dw3[j], conv_b_dw[j].reshape(1, CH, LANES),
                conv_ln_g[j].reshape(1, CH, LANES), conv_ln_b[j].reshape(1, CH, LANES), name=f"conv_mid_bwd_ln{i}")
            du3, ddw_w, dbu = _conv_mid_bwd_dw(st["u3"], ddw3, dw3[j], name=f"conv_mid_bwd_dw{i}")
            d_lng[j], d_lnb[j], d_bdw[j] = dlg.reshape(1, D), dlb.reshape(1, D), dbd.reshape(1, D)
            d_dw[j], d_bpw1[j] = ddw_w.reshape(CONV_WIDTH, D), dbu.reshape(2, D)
            du = du3.reshape(S, 2 * D)
            dh = _mm_nt(du, pw1_g, layer=j, name=f"pw1_bwd_x{i}")
            d_pw1[j] = _mm_tn(st["h"], du, col_shards=NDEV, name=f"pw1_bwd_w{i}")
        else:
            dyg = _mm_nt(dy, wout_g[j], out_dtype=BF16, name=f"ret_out_bwd_x{i}")
            d_wout[j] = _mm_tn(st["yg"], dy, name=f"ret_out_bwd_w{i}")
            dq, dgt, dyr, d_gng[j], d_gnb[j] = _ret_bwd_q(st["proj"], st["yr"], dyg, tb, gng_full[j], gnb_full[j],
                                                          name=f"ret_bwd_q{i}")
            dk_, dv_ = _ret_bwd_kv(st["proj"], dyr, tb, name=f"ret_bwd_kv{i}")
            dproj = jnp.concatenate([dq, dk_, dv_, dgt], axis=1)
            dh = _mm_nt(dproj, win_g, layer=j, name=f"ret_in_bwd_x{i}")
            d_win[j] = _mm_tn(st["h"], dproj, col_shards=NDEV, name=f"ret_in_bwd_w{i}")
        dx, dsc1, dsh1, d_mix_g[i] = _norm_mod_bwd(st["x_in"], vec(norm_mix_g[i]), sc1, dh, dx, name=f"norm_mix_bwd{i}")
        dmod_rows[i] = jnp.concatenate([dsh1, dsc1, dg1, dsh2, dsc2, dg2], axis=0)
    grad_x = dx.reshape(1, S, D)

    small_local = jnp.concatenate(dmod_rows + d_mix_g + d_mlp_g + d_bpw1 + d_bdw + d_lng + d_lnb + d_bpw2 + [d_final_g],
                                  axis=0)
    small_all = _exchange([[small_local]], gather=True, name="gather_small_grads")[0]

    def pack(src):
        return jnp.concatenate([src[n].reshape(-1, D) for n in SMALL], axis=0)

    sm = _adamw(pack(W), pack(Mo), pack(Vo), small_all, name="adamw_small")
    results = {}
    row = 0
    for n in SMALL:
        cnt = W[n].size // D
        results[n] = tuple(o[row:row + cnt].reshape(W[n].shape) for o in sm)
        row += cnt

    ns_ada = ada_w.shape[2]
    dmod_all = small_all[:, :6 * DEPTH, :].reshape(NDEV, DEPTH, 6 * D)
    dmod_cols = jnp.transpose(lax.dynamic_slice_in_dim(dmod_all, me * ns_ada, ns_ada, axis=2), (1, 0, 2))
    g_ada = _ada_bwd(c_all, dmod_cols, name="ada_bwd")
    results["ada_w"] = _adamw(ada_w, m_ada_w, v_ada_w, g_ada[None], name="adamw_ada_w")

    def gn_parts(d):
        return jnp.transpose(d.reshape(RET_HEADS, NDEV, -1), (1, 0, 2))

    parts = _exchange(
        [d_pw1, d_win, d_w1, [a.reshape(NDEV, D // NDEV, D) for a in d_pw2],
         [a.reshape(NDEV, 2 * D // NDEV, D) for a in d_wout], [a.reshape(NDEV, 4 * D // NDEV, D) for a in d_w2],
         [jnp.transpose(a.reshape(CONV_WIDTH, NDEV, D // NDEV), (1, 0, 2)) for a in d_dw],
         [gn_parts(a) for a in d_gng], [gn_parts(a) for a in d_gnb]], gather=False, name="exchange_grads")
    for n, p in zip(("conv_w_pw1", "ret_w_in", "mlp_w1", "conv_w_pw2", "ret_w_out", "mlp_w2", "conv_w_dw", "ret_gn_g",
                     "ret_gn_b"), parts):
        results[n] = _adamw(W[n], Mo[n], Vo[n], p, name=f"adamw_{n}")

    outs = [loss, grad_x]
    for kind in range(4):
        outs += [results[n][kind] for n in WEIGHTS]
    return tuple(outs)
```

```python
import functools

import jax
import jax.numpy as jnp
from jax import lax
from jax.experimental import pallas as pl
from jax.experimental.pallas import tpu as pltpu

F32, BF16 = jnp.float32, jnp.bfloat16
AXES = ("x", "y", "c")
NDEV = 8
DEPTH = 4
EPS = 1e-6
CHUNK = 64
CONV_WIDTH = 31
HALO = 32
RET_HEADS = 4
RET_BLOCK = 256
ROPE_BASE = 10000.0
LANES = 128
ADAM_LR, ADAM_B1, ADAM_B2, ADAM_EPS, ADAM_WD, ADAM_STEP = 0.001, 0.9, 0.999, 1e-08, 0.01, 10
VMEM_LIMIT = 56 * 1024 * 1024
VMEM_BLOCK_BUDGET = 36 * 1024 * 1024
MESH = pl.DeviceIdType.MESH
NT_DIMS = (((1,), (1,)), ((), ()))
TN_DIMS = (((0,), (0,)), ((), ()))


def _call(body, *, name, out_shape, in_specs, out_specs, grid=(), scratch=(), sem=None, aliases=None):
    params = dict(vmem_limit_bytes=VMEM_LIMIT)
    if sem is not None:
        params["dimension_semantics"] = sem
    return pl.pallas_call(body, name=name, grid=grid, in_specs=in_specs, out_specs=out_specs, out_shape=out_shape,
                          scratch_shapes=list(scratch), input_output_aliases=aliases or {},
                          compiler_params=pltpu.CompilerParams(**params))


def _row_tile(rows, want):
    t = min(rows, want)
    while rows % t:
        t //= 2
    return t


def _sds(shape, dtype):
    return jax.ShapeDtypeStruct(tuple(shape), dtype)


def _sigmoid(v):
    return 1.0 / (1.0 + jnp.exp(-v))


def _exchange(groups, *, gather, name):
    flat = [a for g in groups for a in g]
    n_in = len(flat)
    out_shapes = []
    for g in groups:
        s = g[0].shape if gather else g[0].shape[1:]
        lead = (NDEV,) if len(g) == 1 else (NDEV, len(g))
        out_shapes.append(_sds(lead + tuple(s), g[0].dtype))
    n_g = len(groups)

    def body(*refs):
        ins, outs = refs[:n_in], refs[n_in:n_in + n_g]
        send_sems, recv_sems, loc_sems = refs[n_in + n_g:]
        x, y, c = lax.axis_index("x"), lax.axis_index("y"), lax.axis_index("c")
        me = 4 * x + 2 * y + c
        locs, k = [], 0
        for gi, g in enumerate(groups):
            for li in range(len(g)):
                src = ins[k] if gather else ins[k].at[me]
                dst = outs[gi].at[me] if len(g) == 1 else outs[gi].at[me, li]
                cp = pltpu.make_async_copy(src, dst, loc_sems.at[k])
                cp.start()
                locs.append(cp)
                k += 1
        k0 = 0
        for gi, g in enumerate(groups):
            for r in range(1, NDEV):
                px = 1 - x if r & 4 else x
                py = 1 - y if r & 2 else y
                pc = 1 - c if r & 1 else c
                peer = 4 * px + 2 * py + pc
                for li in range(len(g)):
                    src = ins[k0 + li] if gather else ins[k0 + li].at[peer]
                    dst = outs[gi].at[me] if len(g) == 1 else outs[gi].at[me, li]
                    pltpu.make_async_remote_copy(src_ref=src, dst_ref=dst, send_sem=send_sems.at[gi * (NDEV - 1) + r - 1],
                                                 recv_sem=recv_sems.at[gi * (NDEV - 1) + r - 1], device_id=(px, py, pc),
                                                 device_id_type=MESH).start()
            k0 += len(g)
        for gi, g in enumerate(groups):
            for r in range(1, NDEV):
                px = 1 - x if r & 4 else x
                py = 1 - y if r & 2 else y
                pc = 1 - c if r & 1 else c
                peer = 4 * px + 2 * py + pc
                slab = pltpu.make_async_remote_copy(src_ref=outs[gi].at[me], dst_ref=outs[gi].at[peer],
                                                    send_sem=send_sems.at[gi * (NDEV - 1) + r - 1], recv_sem=recv_sems.at[gi * (NDEV - 1) + r - 1],
                                                    device_id=(px, py, pc), device_id_type=MESH)
                slab.wait_send()
                slab.wait_recv()
        for cp in locs:
            cp.wait()

    hbm = pl.BlockSpec(memory_space=pltpu.HBM)
    outs = _call(body, name=name, out_shape=tuple(out_shapes), in_specs=[hbm] * n_in, out_specs=tuple([hbm] * n_g),
                 scratch=[pltpu.SemaphoreType.DMA((n_g * (NDEV - 1),)), pltpu.SemaphoreType.DMA((n_g * (NDEV - 1),)),
                          pltpu.SemaphoreType.DMA((n_in,))])(*flat)
    return list(outs)


def _peer_of(x, y, c, r):
    return (1 - x if r & 4 else x, 1 - y if r & 2 else y, 1 - c if r & 1 else c)


def _exchange_start(groups, *, gather, name, after=None):
    flat = [pltpu.with_memory_space_constraint(a, pltpu.HBM) for g in groups for a in g]
    n_in, n_g = len(flat), len(groups)
    land_shapes = []
    for g in groups:
        s = g[0].shape if gather else g[0].shape[1:]
        lead = (NDEV,) if len(g) == 1 else (NDEV, len(g))
        land_shapes.append((lead + tuple(s), g[0].dtype))
    lands = [pltpu.with_memory_space_constraint(lax.empty(s, d), pltpu.HBM) for s, d in land_shapes]
    n_after = 0 if after is None else 1

    def body(*refs):
        ins, land = refs[:n_in], refs[n_in:n_in + n_g]
        send_sems, recv_sems, loc_sems = refs[n_in + n_g + n_after:n_in + n_g + n_after + 3]
        token = refs[-1]
        x, y, c = lax.axis_index("x"), lax.axis_index("y"), lax.axis_index("c")
        me = 4 * x + 2 * y + c
        k = 0
        for gi, g in enumerate(groups):
            for li in range(len(g)):
                dst = land[gi].at[me] if len(g) == 1 else land[gi].at[me, li]
                pltpu.make_async_copy(ins[k] if gather else ins[k].at[me], dst, loc_sems.at[k]).start()
                k += 1
        k0 = 0
        for gi, g in enumerate(groups):
            for r in range(1, NDEV):
                px, py, pc = _peer_of(x, y, c, r)
                peer = 4 * px + 2 * py + pc
                for li in range(len(g)):
                    dst = land[gi].at[me] if len(g) == 1 else land[gi].at[me, li]
                    pltpu.make_async_remote_copy(src_ref=ins[k0 + li] if gather else ins[k0 + li].at[peer], dst_ref=dst,
                                                 send_sem=send_sems.at[gi * (NDEV - 1) + r - 1], recv_sem=recv_sems.at[gi * (NDEV - 1) + r - 1],
                                                 device_id=(px, py, pc), device_id_type=MESH).start()
            k0 += len(g)
        token[...] = jnp.zeros_like(token)

    hbm = pl.BlockSpec(memory_space=pltpu.HBM)
    sem = pl.BlockSpec(memory_space=pltpu.SEMAPHORE)
    args = flat + lands + ([after] if n_after else [])
    outs = pl.pallas_call(body, name=name,
        out_shape=(pltpu.SemaphoreType.DMA((n_g * (NDEV - 1),)), pltpu.SemaphoreType.DMA((n_g * (NDEV - 1),)),
                   pltpu.SemaphoreType.DMA((n_in,)), *[pltpu.HBM(a.shape, a.dtype) for a in flat],
                   *[pltpu.HBM(s, d) for s, d in land_shapes], _sds((8, LANES), F32)),
        in_specs=[hbm] * (n_in + n_g) + [pl.BlockSpec(memory_space=pl.ANY)] * n_after,
        out_specs=(sem, sem, sem, *[hbm] * (n_in + n_g), pl.BlockSpec(memory_space=pltpu.VMEM)),
        input_output_aliases={k: 3 + k for k in range(n_in + n_g)},
        compiler_params=pltpu.CompilerParams(has_side_effects=pltpu.SideEffectType.DATAFLOW_SIDE_EFFECTING))(*args)
    handle = dict(sems=outs[0:3], srcs=list(outs[3:3 + n_in]), lands=list(outs[3 + n_in:3 + n_in + n_g]),
                  sizes=[len(g) for g in groups], gather=gather)
    return handle, outs[-1]


def _exchange_wait(handle, *, name, after):
    srcs, lands, sizes, gather = handle["srcs"], handle["lands"], handle["sizes"], handle["gather"]
    n_in, n_g = len(srcs), len(lands)

    def body(*refs):
        ins, land = refs[:n_in], refs[n_in:n_in + n_g]
        send_sems, recv_sems, loc_sems = refs[n_in + n_g:n_in + n_g + 3]
        x, y, c = lax.axis_index("x"), lax.axis_index("y"), lax.axis_index("c")
        me = 4 * x + 2 * y + c
        for gi in range(n_g):
            for r in range(1, NDEV):
                px, py, pc = _peer_of(x, y, c, r)
                peer = 4 * px + 2 * py + pc
                slab = pltpu.make_async_remote_copy(src_ref=land[gi].at[me], dst_ref=land[gi].at[peer],
                                                    send_sem=send_sems.at[gi * (NDEV - 1) + r - 1], recv_sem=recv_sems.at[gi * (NDEV - 1) + r - 1],
                                                    device_id=(px, py, pc), device_id_type=MESH)
                slab.wait_send()
                slab.wait_recv()
        k = 0
        for gi in range(n_g):
            for li in range(sizes[gi]):
                dst = land[gi].at[me] if sizes[gi] == 1 else land[gi].at[me, li]
                pltpu.make_async_copy(ins[k] if gather else ins[k].at[me], dst, loc_sems.at[k]).wait()
                k += 1

    hbm = pl.BlockSpec(memory_space=pltpu.HBM)
    sem = pl.BlockSpec(memory_space=pltpu.SEMAPHORE)
    outs = pl.pallas_call(body, name=name, out_shape=tuple(pltpu.HBM(a.shape, a.dtype) for a in srcs + lands),
        in_specs=[hbm] * (n_in + n_g) + [sem] * 3 + [pl.BlockSpec(memory_space=pl.ANY)],
        out_specs=tuple([hbm] * (n_in + n_g)), input_output_aliases={k: k for k in range(n_in + n_g)},
        compiler_params=pltpu.CompilerParams(has_side_effects=pltpu.SideEffectType.DATAFLOW_SIDE_EFFECTING))(
            *srcs, *lands, *handle["sems"], after)
    return list(outs[n_in:])


def _norm_mod_fwd(x, gain, sc, sh, *, name, after=None):
    S, D = x.shape
    tm = _row_tile(S, 512)
    extra = [] if after is None else [after]

    def body(x_ref, g_ref, sc_ref, sh_ref, *rest):
        xv = x_ref[...]
        r = lax.rsqrt(jnp.mean(xv * xv, axis=-1, keepdims=True) + EPS)
        rest[-1][...] = ((xv * r) * g_ref[...] * (1.0 + sc_ref[...]) + sh_ref[...]).astype(BF16)

    row = pl.BlockSpec((tm, D), lambda i: (i, 0))
    vec = pl.BlockSpec((1, D), lambda i: (0, 0))
    return _call(body, name=name, grid=(S // tm,),
                 in_specs=[row, vec, vec, vec] + [pl.BlockSpec(memory_space=pl.ANY)] * len(extra), out_specs=row,
                 out_shape=_sds((S, D), BF16), sem=("parallel",))(x, gain, sc, sh, *extra)


def _gate_part(first, dx, y_ref, g_ref, dy_ref, dg_ref, db_ref):
    @pl.when(first)
    def _():
        dg_ref[...] = jnp.zeros_like(dg_ref)
        db_ref[...] = jnp.zeros_like(db_ref)

    dy = dx * g_ref[...]
    dy_ref[...] = dy.astype(BF16)
    dg_ref[...] += jnp.sum(dx * y_ref[...].astype(F32), axis=0, keepdims=True)
    db_ref[...] += jnp.sum(dy, axis=0, keepdims=True)


def _norm_mod_bwd(x, gain, sc, dh, dres, y_prev, gate_prev, *, name):
    S, D = x.shape
    tm = _row_tile(S, 512)
    gated = y_prev is not None

    def body(x_ref, g_ref, sc_ref, dh_ref, dres_ref, *rest):
        if gated:
            y_ref, gp_ref = rest[0], rest[1]
            rest = rest[2:]
        dx_ref, dsc_ref, dsh_ref, dg_ref = rest[:4]
        first = pl.program_id(0) == 0

        @pl.when(first)
        def _():
            dsc_ref[...] = jnp.zeros_like(dsc_ref)
            dsh_ref[...] = jnp.zeros_like(dsh_ref)
            dg_ref[...] = jnp.zeros_like(dg_ref)

        xv = x_ref[...]
        r = lax.rsqrt(jnp.mean(xv * xv, axis=-1, keepdims=True) + EPS)
        xhat = xv * r
        gain_v = g_ref[...]
        dhv = dh_ref[...]
        dsc_ref[...] += jnp.sum(dhv * (xhat * gain_v), axis=0, keepdims=True)
        dsh_ref[...] += jnp.sum(dhv, axis=0, keepdims=True)
        dxn = dhv * (1.0 + sc_ref[...])
        dg_ref[...] += jnp.sum(dxn * xhat, axis=0, keepdims=True)
        dxhat = dxn * gain_v
        dx = dres_ref[...] + r * (dxhat - xhat * jnp.mean(dxhat * xhat, axis=-1, keepdims=True))
        dx_ref[...] = dx
        if gated:
            _gate_part(first, dx, y_ref, gp_ref, *rest[4:7])

    row = pl.BlockSpec((tm, D), lambda i: (i, 0))
    vec = pl.BlockSpec((1, D), lambda i: (0, 0))
    vsh = _sds((1, D), F32)
    in_specs, args = [row, vec, vec, row, row], [x, gain, sc, dh, dres]
    out_specs, out_shape = [row, vec, vec, vec], [_sds((S, D), F32), vsh, vsh, vsh]
    if gated:
        in_specs += [row, vec]
        args += [y_prev, gate_prev]
        out_specs += [row, vec, vec]
        out_shape += [_sds((S, D), BF16), vsh, vsh]
    return _call(body, name=name, grid=(S // tm,), in_specs=in_specs, out_specs=tuple(out_specs), out_shape=tuple(out_shape),
                 sem=("arbitrary",))(*args)


def _final_loss(x, gain, target, y_prev, gate_prev, *, name):
    S, D = x.shape
    tm = _row_tile(S, 512)

    def body(x_ref, g_ref, t_ref, y_ref, gp_ref, loss_ref, dx_ref, dg_ref, dy_ref, dgp_ref, dbp_ref):
        first = pl.program_id(0) == 0

        @pl.when(first)
        def _():
            loss_ref[...] = jnp.zeros_like(loss_ref)
            dg_ref[...] = jnp.zeros_like(dg_ref)

        xv = x_ref[...]
        r = lax.rsqrt(jnp.mean(xv * xv, axis=-1, keepdims=True) + EPS)
        xhat = xv * r
        gv = g_ref[...]
        err = xhat * gv - t_ref[...]
        row_loss = jnp.mean(err * err, axis=-1, keepdims=True)
        loss_ref[...] += 0.5 * jnp.sum(row_loss, axis=0, keepdims=True)
        dy = err * (1.0 / D)
        dg_ref[...] += jnp.sum(dy * xhat, axis=0, keepdims=True)
        dxhat = dy * gv
        dx = r * (dxhat - xhat * jnp.mean(dxhat * xhat, axis=-1, keepdims=True))
        dx_ref[...] = dx
        _gate_part(first, dx, y_ref, gp_ref, dy_ref, dgp_ref, dbp_ref)

    row = pl.BlockSpec((tm, D), lambda i: (i, 0))
    vec = pl.BlockSpec((1, D), lambda i: (0, 0))
    one = pl.BlockSpec((1, 1), lambda i: (0, 0))
    vsh = _sds((1, D), F32)
    return _call(body, name=name, grid=(S // tm,), in_specs=[row, vec, row, row, vec],
                 out_specs=(one, row, vec, row, vec, vec),
                 out_shape=(_sds((1, 1), F32), _sds((S, D), F32), vsh, _sds((S, D), BF16), vsh, vsh),
                 sem=("arbitrary",))(x, gain, target, y_prev, gate_prev)


def _pick_tm(M, bytes_per_row, fixed_bytes):
    for tm in (1024, 512, 256, 128):
        if M % tm == 0 and 2 * tm * bytes_per_row + fixed_bytes <= VMEM_BLOCK_BUDGET:
            return tm
    return _row_tile(M, 128)


def _mm_nn(a, w, *, name, bias=None, relu2=False, res=None, gate=None, out_dtype=BF16):
    M, K = a.shape
    if w.ndim == 3:
        nj, ns = w.shape[0], w.shape[2]
        w_spec = pl.BlockSpec((None, K, ns), lambda i, j: (j, 0, 0))
    else:
        nj, ns = 1, w.shape[1]
        w_spec = pl.BlockSpec((K, ns), lambda i, j: (0, 0))
    N = nj * ns
    residual = res is not None
    out_bytes = (4 + 4 + 2) if residual else jnp.dtype(out_dtype).itemsize
    tm = _pick_tm(M, K * a.dtype.itemsize + ns * out_bytes, 2 * K * ns * 2)

    def body(*refs):
        it = iter(refs)
        a_ref, w_ref = next(it), next(it)
        b_ref = next(it) if bias is not None else None
        res_ref, gate_ref = (next(it), next(it)) if residual else (None, None)
        out_ref = next(it)
        av = a_ref[...]
        if relu2:
            av = jnp.square(jnp.maximum(av.astype(F32), 0.0))
        acc = jnp.dot(av.astype(BF16), w_ref[...], preferred_element_type=F32)
        if b_ref is not None:
            acc = acc + b_ref[...]
        if residual:
            raw_ref = next(it)
            raw_ref[...] = acc.astype(BF16)
            out_ref[...] = res_ref[...] + gate_ref[...] * acc
        else:
            out_ref[...] = acc.astype(out_dtype)

    tile = pl.BlockSpec((tm, ns), lambda i, j: (i, j))
    vec = pl.BlockSpec((1, ns), lambda i, j: (0, j))
    in_specs, args = [pl.BlockSpec((tm, K), lambda i, j: (i, 0)), w_spec], [a, w]
    if bias is not None:
        in_specs.append(vec)
        args.append(bias)
    if residual:
        in_specs += [tile, vec]
        args += [res, gate]
        out_specs = (tile, tile)
        out_shape = (_sds((M, N), F32), _sds((M, N), BF16))
    else:
        out_specs = tile
        out_shape = _sds((M, N), out_dtype)
    return _call(body, name=name, grid=(M // tm, nj), in_specs=in_specs, out_specs=out_specs, out_shape=out_shape,
                 sem=("parallel", "parallel"))(*args)


def _mm_nt(g, w, *, name, z=None, out_dtype=F32, after=None):
    M, N = g.shape
    col = w.ndim == 3
    if col:
        nsh, K, ns = w.shape
        w_spec = pl.BlockSpec((nsh, K, ns), lambda i: (0, 0, 0))
    else:
        K = w.shape[0]
        w_spec = pl.BlockSpec((K, N), lambda i: (0, 0))
    kc = min(K, 1024)
    obytes = jnp.dtype(out_dtype).itemsize
    tm = _pick_tm(M, N * g.dtype.itemsize + K * obytes + (K * 2 if z is not None else 0), 2 * K * N * 2 + 512 * K * 4)

    def body(*refs):
        it = iter(refs)
        g_ref, w_ref = next(it), next(it)
        z_ref = next(it) if z is not None else None
        if after is not None:
            next(it)
        out_ref = next(it)
        if col:
            acc = None
            for d in range(nsh):
                part = lax.dot_general(g_ref[:, d * ns:(d + 1) * ns].astype(BF16), w_ref[d], NT_DIMS,
                                       preferred_element_type=F32)
                acc = part if acc is None else acc + part
            out_ref[...] = acc.astype(out_dtype)
        else:
            gb = g_ref[...].astype(BF16)
            for cki in range(K // kc):
                cols = slice(cki * kc, (cki + 1) * kc)
                part = lax.dot_general(gb, w_ref[cols, :], NT_DIMS, preferred_element_type=F32)
                if z_ref is not None:
                    part = part * (2.0 * jnp.maximum(z_ref[:, cols].astype(F32), 0.0))
                out_ref[:, cols] = part.astype(out_dtype)

    in_specs, args = [pl.BlockSpec((tm, N), lambda i: (i, 0)), w_spec], [g, w]
    if z is not None:
        in_specs.append(pl.BlockSpec((tm, K), lambda i: (i, 0)))
        args.append(z)
    if after is not None:
        in_specs.append(pl.BlockSpec(memory_space=pl.ANY))
        args.append(after)
    return _call(body, name=name, grid=(M // tm,), in_specs=in_specs, out_specs=pl.BlockSpec((tm, K), lambda i: (i, 0)),
                 out_shape=_sds((M, K), out_dtype), sem=("parallel",))(*args)


def _mm_tn(a, g, *, name, col_shards=None, relu2=False):
    M, K = a.shape
    N = g.shape[1]
    acc_budget = 8 * 1024 * 1024
    tm = _row_tile(M, 512)
    nm = M // tm
    if col_shards:
        ns = N // col_shards
        spc = col_shards
        while spc > 1 and K * ns * spc * 4 > acc_budget:
            spc //= 2
        grid = (col_shards // spc, nm)
        a_spec = pl.BlockSpec((tm, K), lambda c, m: (m, 0))
        g_spec = pl.BlockSpec((tm, spc * ns), lambda c, m: (m, c))
        out_spec = pl.BlockSpec((spc, K, ns), lambda c, m: (c, 0, 0))
        out_shape = _sds((col_shards, K, ns), BF16)
        acc_shape = (K, spc * ns)
    else:
        tk = K
        while tk > 128 and tk * N * 4 > acc_budget:
            tk //= 2
        grid = (K // tk, nm)
        a_spec = pl.BlockSpec((tm, tk), lambda c, m: (m, c))
        g_spec = pl.BlockSpec((tm, N), lambda c, m: (m, 0))
        out_spec = pl.BlockSpec((tk, N), lambda c, m: (c, 0))
        out_shape = _sds((K, N), BF16)
        acc_shape = (tk, N)

    def body(a_ref, g_ref, out_ref, acc_ref):
        m = pl.program_id(1)

        @pl.when(m == 0)
        def _():
            acc_ref[...] = jnp.zeros_like(acc_ref)

        av = a_ref[...]
        if relu2:
            av = jnp.square(jnp.maximum(av.astype(F32), 0.0))
        acc_ref[...] += lax.dot_general(av.astype(BF16), g_ref[...].astype(BF16), TN_DIMS, preferred_element_type=F32)

        @pl.when(m == nm - 1)
        def _():
            if col_shards:
                for s in range(spc):
                    out_ref[s] = acc_ref[:, s * ns:(s + 1) * ns].astype(BF16)
            else:
                out_ref[...] = acc_ref[...].astype(BF16)

    return _call(body, name=name, grid=grid, in_specs=[a_spec, g_spec], out_specs=out_spec, out_shape=out_shape,
                 scratch=[pltpu.VMEM(acc_shape, F32)], sem=("parallel", "arbitrary"))(a, g)


CONV_TILE = 256
CONV_GROUP = 8


def _glu(u, ch):
    return u[:, :ch] * _sigmoid(u[:, ch:])


def _channel_mean(v):
    n = v.shape[1] * v.shape[2]
    return jnp.sum(jnp.sum(v, axis=2, keepdims=True), axis=1, keepdims=True) * (1.0 / n)


def _fill_glu(buf, u_ref, uh_ref, ch, tile):
    first = pl.program_id(0) == 0
    buf[0:HALO] = jnp.where(first, 0.0, _glu(uh_ref[...], ch))
    buf[HALO:HALO + tile] = _glu(u_ref[...], ch)


def _dwconv_group(win, w_ref, rows):
    acc = None
    for k in range(CONV_WIDTH):
        term = win[HALO - CONV_WIDTH + 1 + k:HALO - CONV_WIDTH + 1 + k + rows] * w_ref[k]
        acc = term if acc is None else acc + term
    return acc


def _conv_specs(S, ch, tile):
    per = tile // HALO
    u_spec = pl.BlockSpec((tile, 2 * ch, LANES), lambda i: (i, 0, 0))
    uh_spec = pl.BlockSpec((HALO, 2 * ch, LANES), lambda i: (jnp.maximum(i * per - 1, 0), 0, 0))
    x_spec = pl.BlockSpec((tile, ch, LANES), lambda i: (i, 0, 0))
    xn_spec = pl.BlockSpec((HALO, ch, LANES), lambda i: (jnp.minimum((i + 1) * per, S // HALO - 1), 0, 0))
    w_spec = pl.BlockSpec((CONV_WIDTH, ch, LANES), lambda i: (0, 0, 0))
    v_spec = pl.BlockSpec((1, ch, LANES), lambda i: (0, 0, 0))
    return u_spec, uh_spec, x_spec, xn_spec, w_spec, v_spec


def _conv_mid_fwd(u3, w3, bdw3, lng3, lnb3, *, name):
    S, ch2, _ = u3.shape
    ch = ch2 // 2
    tile = _row_tile(S, CONV_TILE)
    rows = _row_tile(tile, CONV_GROUP)
    u_spec, uh_spec, x_spec, _, w_spec, v_spec = _conv_specs(S, ch, tile)

    def body(u_ref, uh_ref, w_ref, b_ref, g_ref, bb_ref, s_ref, buf):
        _fill_glu(buf, u_ref, uh_ref, ch, tile)

        def group(r, carry):
            t0 = r * rows
            dwo = _dwconv_group(buf[pl.ds(t0, rows + HALO)], w_ref, rows) + b_ref[...]
            mu = _channel_mean(dwo)
            cen = dwo - mu
            rstd = lax.rsqrt(_channel_mean(cen * cen) + EPS)
            ln = cen * rstd * g_ref[...] + bb_ref[...]
            s_ref[pl.ds(t0, rows)] = ln * _sigmoid(ln)
            return carry

        lax.fori_loop(0, tile // rows, group, 0)

    return _call(body, name=name, grid=(S // tile,), in_specs=[u_spec, uh_spec, w_spec, v_spec, v_spec, v_spec],
                 out_specs=x_spec, out_shape=_sds((S, ch, LANES), F32), scratch=[pltpu.VMEM((tile + HALO, ch, LANES), F32)],
                 sem=("parallel",))(u3, u3, w3, bdw3, lng3, lnb3)


def _conv_mid_bwd_ln(u3, ds3, w3, bdw3, lng3, lnb3, *, name):
    S, ch2, _ = u3.shape
    ch = ch2 // 2
    tile = _row_tile(S, CONV_TILE)
    rows = _row_tile(tile, CONV_GROUP)
    u_spec, uh_spec, x_spec, _, w_spec, v_spec = _conv_specs(S, ch, tile)

    def body(u_ref, uh_ref, ds_ref, w_ref, b_ref, g_ref, bb_ref, ddw_ref, dg_ref, db_ref, dbdw_ref, buf):
        @pl.when(pl.program_id(0) == 0)
        def _():
            dg_ref[...] = jnp.zeros_like(dg_ref)
            db_ref[...] = jnp.zeros_like(db_ref)
            dbdw_ref[...] = jnp.zeros_like(dbdw_ref)

        _fill_glu(buf, u_ref, uh_ref, ch, tile)

        def group(r, carry):
            t0 = r * rows
            dwo = _dwconv_group(buf[pl.ds(t0, rows + HALO)], w_ref, rows) + b_ref[...]
            mu = _channel_mean(dwo)
            cen = dwo - mu
            rstd = lax.rsqrt(_channel_mean(cen * cen) + EPS)
            nrm = cen * rstd
            gv = g_ref[...]
            ln = nrm * gv + bb_ref[...]
            sg = _sigmoid(ln)
            dln = ds_ref[pl.ds(t0, rows)] * (sg * (1.0 + ln * (1.0 - sg)))
            dg_ref[...] += jnp.sum(dln * nrm, axis=0, keepdims=True)
            db_ref[...] += jnp.sum(dln, axis=0, keepdims=True)
            dn = dln * gv
            ddw = rstd * (dn - _channel_mean(dn) - nrm * _channel_mean(dn * nrm))
            dbdw_ref[...] += jnp.sum(ddw, axis=0, keepdims=True)
            ddw_ref[pl.ds(t0, rows)] = ddw
            return carry

        lax.fori_loop(0, tile // rows, group, 0)

    vsh = _sds((1, ch, LANES), F32)
    return _call(body, name=name, grid=(S // tile,), in_specs=[u_spec, uh_spec, x_spec, w_spec, v_spec, v_spec, v_spec],
                 out_specs=(x_spec, v_spec, v_spec, v_spec), out_shape=(_sds((S, ch, LANES), F32), vsh, vsh, vsh),
                 scratch=[pltpu.VMEM((tile + HALO, ch, LANES), F32)], sem=("arbitrary",))(u3, u3, ds3, w3, bdw3, lng3, lnb3)


def _conv_mid_bwd_dw(u3, ddw3, w3, *, name):
    S, ch2, _ = u3.shape
    ch = ch2 // 2
    tile = _row_tile(S, CONV_TILE)
    rows = _row_tile(tile, CONV_GROUP)
    last = S // tile - 1
    u_spec, uh_spec, x_spec, xn_spec, w_spec, _ = _conv_specs(S, ch, tile)
    b_spec = pl.BlockSpec((1, 2 * ch, LANES), lambda i: (0, 0, 0))

    def body(u_ref, uh_ref, d_ref, dn_ref, w_ref, du_ref, dw_ref, db_ref, gbuf, dbuf):
        @pl.when(pl.program_id(0) == 0)
        def _():
            dw_ref[...] = jnp.zeros_like(dw_ref)
            db_ref[...] = jnp.zeros_like(db_ref)

        _fill_glu(gbuf, u_ref, uh_ref, ch, tile)
        dbuf[0:tile] = d_ref[...]
        dbuf[tile:tile + HALO] = jnp.where(pl.program_id(0) == last, 0.0, dn_ref[...])

        def group(r, carry):
            t0 = r * rows
            dwin = dbuf[pl.ds(t0, rows + HALO)]
            gwin = gbuf[pl.ds(t0, rows + HALO)]
            ddw = dwin[0:rows]
            dglu = None
            for k in range(CONV_WIDTH):
                back = CONV_WIDTH - 1 - k
                term = dwin[back:back + rows] * w_ref[k]
                dglu = term if dglu is None else dglu + term
                lo = HALO - CONV_WIDTH + 1 + k
                dw_ref[k] += jnp.sum(ddw * gwin[lo:lo + rows], axis=0)
            uv = u_ref[pl.ds(t0, rows)]
            av, sg = uv[:, :ch], _sigmoid(uv[:, ch:])
            da = dglu * sg
            dg = dglu * av * sg * (1.0 - sg)
            du_ref[pl.ds(t0, rows), 0:ch] = da
            du_ref[pl.ds(t0, rows), ch:2 * ch] = dg
            db_ref[:, 0:ch] += jnp.sum(da, axis=0, keepdims=True)
            db_ref[:, ch:2 * ch] += jnp.sum(dg, axis=0, keepdims=True)
            return carry

        lax.fori_loop(0, tile // rows, group, 0)

    return _call(body, name=name, grid=(S // tile,), in_specs=[u_spec, uh_spec, x_spec, xn_spec, w_spec],
                 out_specs=(u_spec, w_spec, b_spec),
                 out_shape=(_sds((S, 2 * ch, LANES), F32), _sds((CONV_WIDTH, ch, LANES), F32), _sds((1, 2 * ch, LANES), F32)),
                 scratch=[pltpu.VMEM((tile + HALO, ch, LANES), F32), pltpu.VMEM((tile + HALO, ch, LANES), F32)],
                 sem=("arbitrary",))(u3, u3, ddw3, ddw3, w3)


def _ret_tables(S, dk):
    B = min(RET_BLOCK, S)
    lg = jnp.log(1.0 - 2.0 ** (-5.0 - jnp.arange(RET_HEADS, dtype=F32)))
    idx = jnp.arange(B, dtype=F32)
    diff = idx[:, None] - idx[None, :]
    cq, ck = (jnp.arange(B) // CHUNK)[:, None], (jnp.arange(B) // CHUNK)[None, :]
    dist = jnp.where(cq == ck, jnp.abs(diff), diff)
    mask = jnp.where(ck <= cq, jnp.exp(lg[:, None, None] * dist[None]), 0.0)
    xi = jnp.exp(lg[:, None] * (idx + 1.0))[..., None]
    zeta = jnp.exp(lg[:, None] * (B - 1.0 - idx))[..., None]
    gam = jnp.broadcast_to(jnp.exp(lg * B)[:, None, None], (RET_HEADS, 8, LANES))
    pos = jnp.arange(S, dtype=F32)
    inv = ROPE_BASE ** (-jnp.arange(0, dk, 2, dtype=F32) / dk)
    ang = pos[:, None] * inv[None, :]
    return dict(B=B, mask=mask, xi=xi, zeta=zeta, gam=gam, cos=jnp.cos(ang), sin=jnp.sin(ang))


def _rope(v, cs, sn):
    half = v.shape[1] // 2
    v1, v2 = v[:, :half], v[:, half:]
    return jnp.concatenate([v1 * cs - v2 * sn, v2 * cs + v1 * sn], axis=-1)


def _rope_t(d, cs, sn):
    half = d.shape[1] // 2
    d1, d2 = d[:, :half], d[:, half:]
    return jnp.concatenate([d1 * cs + d2 * sn, d2 * cs - d1 * sn], axis=-1)


def _dot(a, b):
    return jnp.dot(a.astype(BF16), b.astype(BF16), preferred_element_type=F32)


def _dot_nt(a, b):
    return lax.dot_general(a.astype(BF16), b.astype(BF16), NT_DIMS, preferred_element_type=F32)


def _dot_tn(a, b):
    return lax.dot_general(a.astype(BF16), b.astype(BF16), TN_DIMS, preferred_element_type=F32)


def _ret_specs(S, D, B, RB, reverse):
    dk, dv = D // RET_HEADS, 2 * D // RET_HEADS
    nb = S // RB
    blk = (lambda ib: nb - 1 - ib) if reverse else (lambda ib: ib)
    q = pl.BlockSpec((RB, dk), lambda h, ib: (blk(ib), h))
    k = pl.BlockSpec((RB, dk), lambda h, ib: (blk(ib), RET_HEADS + h))
    v = pl.BlockSpec((RB, dv), lambda h, ib: (blk(ib), RET_HEADS + h))
    gate = pl.BlockSpec((RB, dv), lambda h, ib: (blk(ib), 2 * RET_HEADS + h))
    yv = pl.BlockSpec((RB, dv), lambda h, ib: (blk(ib), h))
    rope = pl.BlockSpec((RB, dk // 2), lambda h, ib: (blk(ib), 0))
    mask = pl.BlockSpec((None, B, B), lambda h, ib: (h, 0, 0))
    dec = pl.BlockSpec((None, B, 1), lambda h, ib: (h, 0, 0))
    gam = pl.BlockSpec((None, 8, LANES), lambda h, ib: (h, 0, 0))
    gn = pl.BlockSpec((1, dv), lambda h, ib: (0, h))
    return dict(q=q, k=k, v=v, gate=gate, yv=yv, rope=rope, mask=mask, dec=dec, gam=gam, gn=gn)


def _group_norm(yr, gv, bv):
    mu = jnp.mean(yr, axis=-1, keepdims=True)
    cen = yr - mu
    rstd = lax.rsqrt(jnp.mean(cen * cen, axis=-1, keepdims=True) + EPS)
    nrm = cen * rstd
    return nrm, rstd, nrm * gv + bv


def _ret_fwd(proj, tb, gng, gnb, *, name):
    S, D = proj.shape[0], proj.shape[1] // 6
    dk, dv = D // RET_HEADS, 2 * D // RET_HEADS
    B = tb["B"]
    RB = _row_tile(S, 2 * B)
    nsub = RB // B
    sp = _ret_specs(S, D, B, RB, False)
    scale = dk ** -0.5

    def body(q_ref, k_ref, v_ref, gt_ref, cos_ref, sin_ref, mask_ref, xi_ref, zeta_ref, gam_ref, gng_ref, gnb_ref,
             yr_ref, yg_ref, state):
        @pl.when(pl.program_id(1) == 0)
        def _():
            state[...] = jnp.zeros_like(state)

        for sb in range(nsub):
            rows = slice(sb * B, (sb + 1) * B)
            cs, sn = cos_ref[rows, :], sin_ref[rows, :]
            q = _rope(q_ref[rows, :].astype(F32), cs, sn)
            k = _rope(k_ref[rows, :].astype(F32), cs, sn) * scale
            vb = v_ref[rows, :]
            p = _dot_nt(q, k) * mask_ref[...]
            st = state[...]
            yr = _dot(p, vb) + _dot(q * xi_ref[...], st)
            state[...] = st * gam_ref[0:1, 0:1] + _dot_tn(k * zeta_ref[...], vb)
            _, _, gn = _group_norm(yr, gng_ref[...], gnb_ref[...])
            gt = gt_ref[rows, :].astype(F32)
            yr_ref[rows, :] = yr.astype(BF16)
            yg_ref[rows, :] = (gt * _sigmoid(gt) * gn).astype(BF16)

    return _call(body, name=name, grid=(RET_HEADS, S // RB),
                 in_specs=[sp["q"], sp["k"], sp["v"], sp["gate"], sp["rope"], sp["rope"], sp["mask"], sp["dec"], sp["dec"],
                           sp["gam"], sp["gn"], sp["gn"]],
                 out_specs=(sp["yv"], sp["yv"]), out_shape=(_sds((S, 2 * D), BF16), _sds((S, 2 * D), BF16)),
                 scratch=[pltpu.VMEM((dk, dv), F32)], sem=("parallel", "arbitrary"))(
                     proj, proj, proj, proj, tb["cos"], tb["sin"], tb["mask"], tb["xi"], tb["zeta"], tb["gam"], gng, gnb)


def _ret_bwd_q(proj, yr, dyg, tb, gng, gnb, *, name):
    S, D = proj.shape[0], proj.shape[1] // 6
    dk, dv = D // RET_HEADS, 2 * D // RET_HEADS
    B = tb["B"]
    RB = _row_tile(S, 2 * B)
    nsub = RB // B
    sp = _ret_specs(S, D, B, RB, False)
    scale = dk ** -0.5

    def body(q_ref, k_ref, v_ref, gt_ref, yr_ref, dyg_ref, cos_ref, sin_ref, mask_ref, xi_ref, zeta_ref, gam_ref,
             gng_ref, gnb_ref, dq_ref, dgt_ref, dyr_ref, dgg_ref, dgb_ref, state):
        @pl.when(pl.program_id(1) == 0)
        def _():
            state[...] = jnp.zeros_like(state)
            dgg_ref[...] = jnp.zeros_like(dgg_ref)
            dgb_ref[...] = jnp.zeros_like(dgb_ref)

        for sb in range(nsub):
            rows = slice(sb * B, (sb + 1) * B)
            cs, sn = cos_ref[rows, :], sin_ref[rows, :]
            q = _rope(q_ref[rows, :].astype(F32), cs, sn)
            k = _rope(k_ref[rows, :].astype(F32), cs, sn) * scale
            vb = v_ref[rows, :]
            gv = gng_ref[...]
            nrm, rstd, gn = _group_norm(yr_ref[rows, :].astype(F32), gv, gnb_ref[...])
            gt = gt_ref[rows, :].astype(F32)
            sg = _sigmoid(gt)
            dyg = dyg_ref[rows, :].astype(F32)
            dgt_ref[rows, :] = (dyg * gn * (sg * (1.0 + gt * (1.0 - sg)))).astype(BF16)
            dgn = dyg * (gt * sg)
            dgg_ref[...] += jnp.sum(dgn * nrm, axis=0, keepdims=True)
            dgb_ref[...] += jnp.sum(dgn, axis=0, keepdims=True)
            dn = dgn * gv
            dyr = rstd * (dn - jnp.mean(dn, axis=-1, keepdims=True) - nrm * jnp.mean(dn * nrm, axis=-1, keepdims=True))
            dyr_ref[rows, :] = dyr.astype(BF16)
            dp = _dot_nt(dyr, vb) * mask_ref[...]
            st = state[...]
            dq = _dot(dp, k) + _dot_nt(dyr, st) * xi_ref[...]
            dq_ref[rows, :] = _rope_t(dq, cs, sn).astype(BF16)
            state[...] = st * gam_ref[0:1, 0:1] + _dot_tn(k * zeta_ref[...], vb)

    qout = pl.BlockSpec((RB, dk), lambda h, ib: (ib, h))
    return _call(body, name=name, grid=(RET_HEADS, S // RB),
                 in_specs=[sp["q"], sp["k"], sp["v"], sp["gate"], sp["yv"], sp["yv"], sp["rope"], sp["rope"], sp["mask"],
                           sp["dec"], sp["dec"], sp["gam"], sp["gn"], sp["gn"]],
                 out_specs=(qout, sp["yv"], sp["yv"], sp["gn"], sp["gn"]),
                 out_shape=(_sds((S, D), BF16), _sds((S, 2 * D), BF16), _sds((S, 2 * D), BF16), _sds((1, 2 * D), F32),
                            _sds((1, 2 * D), F32)),
                 scratch=[pltpu.VMEM((dk, dv), F32)], sem=("parallel", "arbitrary"))(
                     proj, proj, proj, proj, yr, dyg, tb["cos"], tb["sin"], tb["mask"], tb["xi"], tb["zeta"], tb["gam"],
                     gng, gnb)


def _ret_bwd_kv(proj, dyr, tb, *, name):
    S, D = proj.shape[0], proj.shape[1] // 6
    dk, dv = D // RET_HEADS, 2 * D // RET_HEADS
    B = tb["B"]
    RB = _row_tile(S, 2 * B)
    nsub = RB // B
    nb = S // RB
    sp = _ret_specs(S, D, B, RB, True)
    scale = dk ** -0.5

    def body(q_ref, k_ref, v_ref, dyr_ref, cos_ref, sin_ref, mask_ref, xi_ref, zeta_ref, gam_ref, dk_ref, dv_ref, dstate):
        @pl.when(pl.program_id(1) == 0)
        def _():
            dstate[...] = jnp.zeros_like(dstate)

        for sb in reversed(range(nsub)):
            rows = slice(sb * B, (sb + 1) * B)
            cs, sn = cos_ref[rows, :], sin_ref[rows, :]
            q = _rope(q_ref[rows, :].astype(F32), cs, sn)
            k = _rope(k_ref[rows, :].astype(F32), cs, sn) * scale
            vb = v_ref[rows, :]
            dyr = dyr_ref[rows, :]
            mk = mask_ref[...]
            p = _dot_nt(q, k) * mk
            dp = _dot_nt(dyr, vb) * mk
            ds = dstate[...]
            zt = zeta_ref[...]
            dkr = _dot_tn(dp, q) + _dot_nt(vb, ds) * zt
            dk_ref[rows, :] = _rope_t(dkr * scale, cs, sn).astype(BF16)
            dv_ref[rows, :] = (_dot_tn(p, dyr) + _dot(k * zt, ds)).astype(BF16)
            dstate[...] = ds * gam_ref[0:1, 0:1] + _dot_tn(q * xi_ref[...], dyr)

    kout = pl.BlockSpec((RB, dk), lambda h, ib: (nb - 1 - ib, h))
    return _call(body, name=name, grid=(RET_HEADS, nb),
                 in_specs=[sp["q"], sp["k"], sp["v"], sp["yv"], sp["rope"], sp["rope"], sp["mask"], sp["dec"], sp["dec"],
                           sp["gam"]],
                 out_specs=(kout, sp["yv"]), out_shape=(_sds((S, D), BF16), _sds((S, 2 * D), BF16)),
                 scratch=[pltpu.VMEM((dk, dv), F32)], sem=("parallel", "arbitrary"))(
                     proj, proj, proj, dyr, tb["cos"], tb["sin"], tb["mask"], tb["xi"], tb["zeta"], tb["gam"])


def _ada_fwd(c_all, ada_w, *, name):
    L, D, ns = ada_w.shape

    def body(c_ref, w_ref, out_ref):
        cv = c_ref[...]
        cond = cv * _sigmoid(cv)
        out_ref[...] = jnp.dot(cond.astype(BF16), w_ref[...].astype(BF16), preferred_element_type=F32)

    return _call(body, name=name, grid=(L,), in_specs=[pl.BlockSpec((NDEV, D), lambda l: (0, 0)),
                                                      pl.BlockSpec((None, D, ns), lambda l: (l, 0, 0))],
                 out_specs=pl.BlockSpec((None, NDEV, ns), lambda l: (l, 0, 0)), out_shape=_sds((L, NDEV, ns), F32),
                 sem=("parallel",))(c_all, ada_w)


def _ada_bwd(c_all, dmod_cols, *, name):
    L, _, ns = dmod_cols.shape
    D = c_all.shape[1]

    def body(c_ref, d_ref, out_ref):
        cv = c_ref[...]
        cond = cv * _sigmoid(cv)
        out_ref[...] = lax.dot_general(cond.astype(BF16), d_ref[...].astype(BF16), TN_DIMS, preferred_element_type=F32)

    return _call(body, name=name, grid=(L,), in_specs=[pl.BlockSpec((NDEV, D), lambda l: (0, 0)),
                                                      pl.BlockSpec((None, NDEV, ns), lambda l: (l, 0, 0))],
                 out_specs=pl.BlockSpec((None, D, ns), lambda l: (l, 0, 0)), out_shape=_sds((L, D, ns), F32),
                 sem=("parallel",))(c_all, dmod_cols)


def _adamw(w, m, v, parts, *, name):
    shape = w.shape
    L, cols = len(parts), shape[-1]
    rows = w.size // (cols * L)
    n = parts[0].shape[0]
    tr = rows
    for cand in (256, 128, 64, 32, 16, 8):
        if rows % cand == 0:
            tr = cand
            break
    c1 = 1.0 - ADAM_B1 ** ADAM_STEP
    c2 = 1.0 - ADAM_B2 ** ADAM_STEP

    def body(w_ref, m_ref, v_ref, *rest):
        p_refs = rest[:L]
        g_ref, d_ref, m2_ref, v2_ref = rest[L:]
        layer = pl.program_id(0)
        for l in range(L):
            @pl.when(layer == l)
            def _(p_ref=p_refs[l]):
                g = p_ref[0].astype(F32)
                for i in range(1, n):
                    g = g + p_ref[i].astype(F32)
                m2 = ADAM_B1 * m_ref[...] + (1.0 - ADAM_B1) * g
                v2 = ADAM_B2 * v_ref[...] + (1.0 - ADAM_B2) * (g * g)
                g_ref[...] = g
                m2_ref[...] = m2
                v2_ref[...] = v2
                d_ref[...] = -ADAM_LR * ((m2 / c1) / (jnp.sqrt(v2 / c2) + ADAM_EPS) + ADAM_WD * w_ref[...])

    mat = pl.BlockSpec((None, tr, cols), lambda l, i: (l, i, 0))

    def part_spec(k):
        return pl.BlockSpec((n, tr, cols), lambda l, i: (0, jnp.where(l == k, i, 0), 0))

    outs = _call(body, name=name, grid=(L, rows // tr), in_specs=[mat, mat, mat] + [part_spec(k) for k in range(L)],
                 out_specs=(mat, mat, mat, mat), out_shape=tuple(_sds((L, rows, cols), F32) for _ in range(4)),
                 sem=("parallel", "parallel"))(w.reshape(L, rows, cols), m.reshape(L, rows, cols), v.reshape(L, rows, cols),
                                               *[p.reshape(n, rows, cols) for p in parts])
    return tuple(o.reshape(shape) for o in outs)


SMALL = ("ada_b", "norm_mix_g", "norm_mlp_g", "conv_b_pw1", "conv_b_dw", "conv_ln_g", "conv_ln_b", "conv_b_pw2",
         "final_norm_g")
WEIGHTS = ("ada_w", "ada_b", "norm_mix_g", "norm_mlp_g", "conv_w_pw1", "conv_b_pw1", "conv_w_dw", "conv_b_dw", "conv_ln_g",
           "conv_ln_b", "conv_w_pw2", "conv_b_pw2", "ret_w_in", "ret_gn_g", "ret_gn_b", "ret_w_out", "mlp_w1", "mlp_w2",
           "final_norm_g")


def kernel(x, c, ada_w, ada_b, norm_mix_g, norm_mlp_g, conv_w_pw1, conv_b_pw1, conv_w_dw, conv_b_dw, conv_ln_g, conv_ln_b, conv_w_pw2, conv_b_pw2, ret_w_in, ret_gn_g, ret_gn_b, ret_w_out, mlp_w1, mlp_w2, final_norm_g, loss_target, m_ada_w, m_ada_b, m_norm_mix_g, m_norm_mlp_g, m_conv_w_pw1, m_conv_b_pw1, m_conv_w_dw, m_conv_b_dw, m_conv_ln_g, m_conv_ln_b, m_conv_w_pw2, m_conv_b_pw2, m_ret_w_in, m_ret_gn_g, m_ret_gn_b, m_ret_w_out, m_mlp_w1, m_mlp_w2, m_final_norm_g, v_ada_w, v_ada_b, v_norm_mix_g, v_norm_mlp_g, v_conv_w_pw1, v_conv_b_pw1, v_conv_w_dw, v_conv_b_dw, v_conv_ln_g, v_conv_ln_b, v_conv_w_pw2, v_conv_b_pw2, v_ret_w_in, v_ret_gn_g, v_ret_gn_b, v_ret_w_out, v_mlp_w1, v_mlp_w2, v_final_norm_g):
    W = dict(ada_w=ada_w, ada_b=ada_b, norm_mix_g=norm_mix_g, norm_mlp_g=norm_mlp_g, conv_w_pw1=conv_w_pw1,
             conv_b_pw1=conv_b_pw1, conv_w_dw=conv_w_dw, conv_b_dw=conv_b_dw, conv_ln_g=conv_ln_g, conv_ln_b=conv_ln_b,
             conv_w_pw2=conv_w_pw2, conv_b_pw2=conv_b_pw2, ret_w_in=ret_w_in, ret_gn_g=ret_gn_g, ret_gn_b=ret_gn_b,
             ret_w_out=ret_w_out, mlp_w1=mlp_w1, mlp_w2=mlp_w2, final_norm_g=final_norm_g)
    Mo = dict(ada_w=m_ada_w, ada_b=m_ada_b, norm_mix_g=m_norm_mix_g, norm_mlp_g=m_norm_mlp_g, conv_w_pw1=m_conv_w_pw1,
              conv_b_pw1=m_conv_b_pw1, conv_w_dw=m_conv_w_dw, conv_b_dw=m_conv_b_dw, conv_ln_g=m_conv_ln_g,
              conv_ln_b=m_conv_ln_b, conv_w_pw2=m_conv_w_pw2, conv_b_pw2=m_conv_b_pw2, ret_w_in=m_ret_w_in,
              ret_gn_g=m_ret_gn_g, ret_gn_b=m_ret_gn_b, ret_w_out=m_ret_w_out, mlp_w1=m_mlp_w1, mlp_w2=m_mlp_w2,
              final_norm_g=m_final_norm_g)
    Vo = dict(ada_w=v_ada_w, ada_b=v_ada_b, norm_mix_g=v_norm_mix_g, norm_mlp_g=v_norm_mlp_g, conv_w_pw1=v_conv_w_pw1,
              conv_b_pw1=v_conv_b_pw1, conv_w_dw=v_conv_w_dw, conv_b_dw=v_conv_b_dw, conv_ln_g=v_conv_ln_g,
              conv_ln_b=v_conv_ln_b, conv_w_pw2=v_conv_w_pw2, conv_b_pw2=v_conv_b_pw2, ret_w_in=v_ret_w_in,
              ret_gn_g=v_ret_gn_g, ret_gn_b=v_ret_gn_b, ret_w_out=v_ret_w_out, mlp_w1=v_mlp_w1, mlp_w2=v_mlp_w2,
              final_norm_g=v_final_norm_g)

    S, D = x.shape[1], x.shape[2]
    CH = D // LANES
    n_conv, n_ret = conv_w_pw1.shape[0], ret_w_in.shape[0]
    me = 4 * lax.axis_index("x") + 2 * lax.axis_index("y") + lax.axis_index("c")
    xs = x.reshape(S, D)
    target = loss_target.reshape(S, D)

    def mixer_shards(i):
        j = i // 2
        if i % 2 == 0:
            return [[conv_w_pw1[j].astype(BF16)], [conv_w_pw2[j].astype(BF16)]]
        return [[ret_w_in[j].astype(BF16)], [ret_w_out[j].astype(BF16)]]

    def mlp_shards(i):
        return [[mlp_w1[i].astype(BF16)], [mlp_w2[i].astype(BF16)]]

    def mixer_weights(i, got):
        return got[0], got[1].reshape(-1, D)

    def mlp_weights(got):
        return got[0], got[1].reshape(4 * D, D)

    first_mixer, _ = _exchange_start(mixer_shards(0), gather=True, name="gather_start_mix0")
    small = _exchange([[conv_w_dw], [ret_gn_g], [ret_gn_b], [c]], gather=True, name="gather_small")
    dw_g, gng_g, gnb_g, c_g = small
    dw3 = jnp.transpose(dw_g, (1, 2, 0, 3)).reshape(n_conv, CONV_WIDTH, CH, LANES)
    gng_full = jnp.transpose(gng_g, (1, 2, 0, 3)).reshape(n_ret, 1, 2 * D)
    gnb_full = jnp.transpose(gnb_g, (1, 2, 0, 3)).reshape(n_ret, 1, 2 * D)
    c_all = c_g.reshape(NDEV, D)

    mod_cols = _ada_fwd(c_all, ada_w, name="ada_fwd")
    mod_all = _exchange([[mod_cols]], gather=True, name="gather_mod")[0]
    mod = lax.dynamic_index_in_dim(mod_all, me, axis=2, keepdims=False)
    mod = jnp.transpose(mod, (1, 0, 2)).reshape(DEPTH, 6 * D) + ada_b
    mods = [[mod[i, j * D:(j + 1) * D].reshape(1, D) for j in range(6)] for i in range(DEPTH)]
    tb = _ret_tables(S, D // RET_HEADS)

    def vec(a):
        return a.reshape(1, -1)

    mix_w = mixer_weights(0, _exchange_wait(first_mixer, name="gather_wait_mix0", after=mod))
    mlp_handle, token = _exchange_start(mlp_shards(0), gather=True, name="gather_start_mlp0", after=mix_w[0])
    saved = []
    weights = []
    xcur = xs
    for i in range(DEPTH):
        sh1, sc1, g1, sh2, sc2, g2 = mods[i]
        j = i // 2
        if i + 1 < DEPTH:
            next_handle, token = _exchange_start(mixer_shards(i + 1) + mlp_shards(i + 1), gather=True,
                                                 name=f"gather_start{i + 1}", after=mix_w[0])
        st = dict(x_in=xcur)
        h = _norm_mod_fwd(xcur, vec(norm_mix_g[i]), sc1, sh1, name=f"norm_mix_fwd{i}", after=token)
        st["h"] = h
        if i % 2 == 0:
            u = _mm_nn(h, mix_w[0], bias=vec(conv_b_pw1[j]), out_dtype=F32, name=f"pw1_fwd{i}")
            u3 = u.reshape(S, 2 * CH, LANES)
            s3 = _conv_mid_fwd(u3, dw3[j], conv_b_dw[j].reshape(1, CH, LANES), conv_ln_g[j].reshape(1, CH, LANES),
                               conv_ln_b[j].reshape(1, CH, LANES), name=f"conv_mid_fwd{i}")
            s2 = s3.reshape(S, D)
            xcur, y_raw = _mm_nn(s2, mix_w[1], bias=vec(conv_b_pw2[j]), res=xcur, gate=g1, name=f"pw2_fwd{i}")
            st.update(u3=u3, s2=s2, y_raw=y_raw)
        else:
            proj = _mm_nn(h, mix_w[0], name=f"ret_in_fwd{i}")
            yr, yg = _ret_fwd(proj, tb, gng_full[j], gnb_full[j], name=f"ret_fwd{i}")
            xcur, y_raw = _mm_nn(yg, mix_w[1], res=xcur, gate=g1, name=f"ret_out_fwd{i}")
            st.update(proj=proj, yr=yr, yg=yg, y_raw=y_raw)
        st["x_mid"] = xcur
        if i == 0:
            mlp_w = mlp_weights(_exchange_wait(mlp_handle, name="gather_wait_mlp0", after=xcur))
        h2 = _norm_mod_fwd(xcur, vec(norm_mlp_g[i]), sc2, sh2, name=f"norm_mlp_fwd{i}")
        z = _mm_nn(h2, mlp_w[0], name=f"mlp1_fwd{i}")
        xcur, o_raw = _mm_nn(z, mlp_w[1], relu2=True, res=xcur, gate=g2, name=f"mlp2_fwd{i}")
        st.update(h2=h2, z=z, o_raw=o_raw)
        saved.append(st)
        weights.append(mix_w + mlp_w)
        if i + 1 < DEPTH:
            got = _exchange_wait(next_handle, name=f"gather_wait{i + 1}", after=xcur)
            mix_w, mlp_w = mixer_weights(i + 1, got[0:2]), mlp_weights(got[2:4])

    g2_last = mods[DEPTH - 1][5]
    loss_local, dx, d_final_g, dy, dgate, _ = _final_loss(xcur, vec(final_norm_g), target, saved[-1]["o_raw"], g2_last,
                                                          name="final_loss")
    loss = lax.psum(loss_local[0, 0], AXES)

    dmod_rows = [None] * DEPTH
    d_mix_g, d_mlp_g = [None] * DEPTH, [None] * DEPTH
    d_pw1, d_pw2, d_win, d_wout = [None] * n_conv, [None] * n_conv, [None] * n_ret, [None] * n_ret
    d_w1, d_w2 = [None] * DEPTH, [None] * DEPTH
    d_bpw1, d_bdw, d_lng, d_lnb, d_bpw2, d_dw = ([None] * n_conv for _ in range(6))
    d_gng, d_gnb = [None] * n_ret, [None] * n_ret

    def gn_parts(d):
        return jnp.transpose(d.reshape(RET_HEADS, NDEV, -1), (1, 0, 2))

    grad_handles = [None] * DEPTH
    token = None
    for i in reversed(range(DEPTH)):
        sh1, sc1, g1, sh2, sc2, g2 = mods[i]
        j = i // 2
        st = saved[i]
        mix_a, mix_b, w1_i, w2_i = weights[i]
        do, dg2 = dy, dgate
        dz = _mm_nt(do, w2_i, z=st["z"], out_dtype=BF16, name=f"mlp2_bwd_x{i}", after=token)
        d_w2[i] = _mm_tn(st["z"], do, relu2=True, name=f"mlp2_bwd_w{i}")
        dh2 = _mm_nt(dz, w1_i, name=f"mlp1_bwd_x{i}")
        d_w1[i] = _mm_tn(st["h2"], dz, col_shards=NDEV, name=f"mlp1_bwd_w{i}")
        mlp_groups = [[d_w1[i]], [d_w2[i].reshape(NDEV, 4 * D // NDEV, D)]]
        dx, dsc2, dsh2, d_mlp_g[i], dy, dg1, dby = _norm_mod_bwd(st["x_mid"], vec(norm_mlp_g[i]), sc2, dh2, dx, st["y_raw"],
                                                                 g1, name=f"norm_mlp_bwd{i}")
        token = None
        if i == 0:
            mlp0_handle, token = _exchange_start(mlp_groups, gather=False, name="grads_start_mlp0")
            mlp_groups = []
        if i % 2 == 0:
            d_bpw2[j] = dby
            ds = _mm_nt(dy, mix_b, name=f"pw2_bwd_x{i}", after=token)
            d_pw2[j] = _mm_tn(st["s2"], dy, name=f"pw2_bwd_w{i}")
            ddw3, dlg, dlb, dbd = _conv_mid_bwd_ln(
                st["u3"], ds.reshape(S, CH, LANES), dw3[j], conv_b_dw[j].reshape(1, CH, LANES),
                conv_ln_g[j].reshape(1, CH, LANES), conv_ln_b[j].reshape(1, CH, LANES), name=f"conv_mid_bwd_ln{i}")
            du3, ddw_w, dbu = _conv_mid_bwd_dw(st["u3"], ddw3, dw3[j], name=f"conv_mid_bwd_dw{i}")
            d_lng[j], d_lnb[j], d_bdw[j] = dlg.reshape(1, D), dlb.reshape(1, D), dbd.reshape(1, D)
            d_dw[j], d_bpw1[j] = ddw_w.reshape(CONV_WIDTH, D), dbu.reshape(2, D)
            du = du3.reshape(S, 2 * D)
            dh = _mm_nt(du, mix_a, name=f"pw1_bwd_x{i}")
            d_pw1[j] = _mm_tn(st["h"], du, col_shards=NDEV, name=f"pw1_bwd_w{i}")
            mix_groups = [[d_pw1[j]], [d_pw2[j].reshape(NDEV, D // NDEV, D)],
                          [jnp.transpose(d_dw[j].reshape(CONV_WIDTH, NDEV, D // NDEV), (1, 0, 2))]]
        else:
            dyg = _mm_nt(dy, mix_b, out_dtype=BF16, name=f"ret_out_bwd_x{i}")
            d_wout[j] = _mm_tn(st["yg"], dy, name=f"ret_out_bwd_w{i}")
            dq, dgt, dyr, d_gng[j], d_gnb[j] = _ret_bwd_q(st["proj"], st["yr"], dyg, tb, gng_full[j], gnb_full[j],
                                                          name=f"ret_bwd_q{i}")
            dk_, dv_ = _ret_bwd_kv(st["proj"], dyr, tb, name=f"ret_bwd_kv{i}")
            dproj = jnp.concatenate([dq, dk_, dv_, dgt], axis=1)
            dh = _mm_nt(dproj, mix_a, name=f"ret_in_bwd_x{i}")
            d_win[j] = _mm_tn(st["h"], dproj, col_shards=NDEV, name=f"ret_in_bwd_w{i}")
            mix_groups = [[d_win[j]], [d_wout[j].reshape(NDEV, 2 * D // NDEV, D)], [gn_parts(d_gng[j])],
                          [gn_parts(d_gnb[j])]]
        grad_handles[i], token = _exchange_start(mix_groups + mlp_groups, gather=False, name=f"grads_start{i}")
        y_prev, gate_prev = (saved[i - 1]["o_raw"], mods[i - 1][5]) if i > 0 else (None, None)
        outs = _norm_mod_bwd(st["x_in"], vec(norm_mix_g[i]), sc1, dh, dx, y_prev, gate_prev, name=f"norm_mix_bwd{i}")
        dx, dsc1, dsh1, d_mix_g[i] = outs[:4]
        if i > 0:
            dy, dgate = outs[4], outs[5]
        dmod_rows[i] = jnp.concatenate([dsh1, dsc1, dg1, dsh2, dsc2, dg2], axis=0)
    grad_x = dx.reshape(1, S, D)

    small_local = jnp.concatenate(dmod_rows + d_mix_g + d_mlp_g + d_bpw1 + d_bdw + d_lng + d_lnb + d_bpw2 + [d_final_g],
                                  axis=0)
    small_all = _exchange([[small_local]], gather=True, name="gather_small_grads")[0]

    def pack(src):
        return jnp.concatenate([src[n].reshape(-1, D) for n in SMALL], axis=0)[None]

    sm = _adamw(pack(W), pack(Mo), pack(Vo), [small_all], name="adamw_small")
    results = {}
    row = 0
    for n in SMALL:
        cnt = W[n].size // D
        results[n] = tuple(o[0, row:row + cnt].reshape(W[n].shape) for o in sm)
        row += cnt

    ns_ada = ada_w.shape[2]
    dmod_all = small_all[:, :6 * DEPTH, :].reshape(NDEV, DEPTH, 6 * D)
    dmod_cols = jnp.transpose(lax.dynamic_slice_in_dim(dmod_all, me * ns_ada, ns_ada, axis=2), (1, 0, 2))
    g_ada = _ada_bwd(c_all, dmod_cols, name="ada_bwd")
    flat_ada = (1, DEPTH * D, ns_ada)
    ada_res = _adamw(ada_w.reshape(flat_ada), m_ada_w.reshape(flat_ada), v_ada_w.reshape(flat_ada),
                     [g_ada.reshape(flat_ada)], name="adamw_ada_w")
    results["ada_w"] = tuple(o.reshape(ada_w.shape) for o in ada_res)

    def update(names, parts):
        for n in names:
            results[n] = _adamw(W[n], Mo[n], Vo[n], parts[n], name=f"adamw_{n}")

    got = {i: _exchange_wait(grad_handles[i], name=f"grads_wait{i}", after=dx) for i in range(DEPTH - 1, 0, -1)}
    ret_layers = [i for i in range(DEPTH) if i % 2 == 1]
    update(("ret_w_in", "ret_w_out", "ret_gn_g", "ret_gn_b"),
           dict(ret_w_in=[got[i][0] for i in ret_layers], ret_w_out=[got[i][1] for i in ret_layers],
                ret_gn_g=[got[i][2] for i in ret_layers], ret_gn_b=[got[i][3] for i in ret_layers]))
    got_mlp0 = _exchange_wait(mlp0_handle, name="grads_wait_mlp0", after=results["ret_w_in"][0])
    update(("mlp_w1", "mlp_w2"),
           dict(mlp_w1=[got_mlp0[0]] + [got[i][-2] for i in range(1, DEPTH)],
                mlp_w2=[got_mlp0[1]] + [got[i][-1] for i in range(1, DEPTH)]))
    got[0] = _exchange_wait(grad_handles[0], name="grads_wait0", after=results["mlp_w1"][0])
    conv_layers = [i for i in range(DEPTH) if i % 2 == 0]
    update(("conv_w_pw1", "conv_w_pw2", "conv_w_dw"),
           dict(conv_w_pw1=[got[i][0] for i in conv_layers], conv_w_pw2=[got[i][1] for i in conv_layers],
                conv_w_dw=[got[i][2] for i in conv_layers]))

    outs = [loss, grad_x]
    for kind in range(4):
        outs += [results[n][kind] for n in WEIGHTS]
    return tuple(outs)
```

```python
import functools

import jax
import jax.numpy as jnp
from jax import lax
from jax.experimental import pallas as pl
from jax.experimental.pallas import tpu as pltpu

F32, BF16 = jnp.float32, jnp.bfloat16
AXES = ("x", "y", "c")
NDEV = 8
DEPTH = 4
EPS = 1e-6
CHUNK = 64
CONV_WIDTH = 31
HALO = 32
RET_HEADS = 4
RET_BLOCK = 256
ROPE_BASE = 10000.0
LANES = 128
ADAM_LR, ADAM_B1, ADAM_B2, ADAM_EPS, ADAM_WD, ADAM_STEP = 0.001, 0.9, 0.999, 1e-08, 0.01, 10
VMEM_LIMIT = 56 * 1024 * 1024
VMEM_BLOCK_BUDGET = 36 * 1024 * 1024
MESH = pl.DeviceIdType.MESH
NT_DIMS = (((1,), (1,)), ((), ()))
TN_DIMS = (((0,), (0,)), ((), ()))


def _call(body, *, name, out_shape, in_specs, out_specs, grid=(), scratch=(), sem=None, aliases=None):
    params = dict(vmem_limit_bytes=VMEM_LIMIT)
    if sem is not None:
        params["dimension_semantics"] = sem
    return pl.pallas_call(body, name=name, grid=grid, in_specs=in_specs, out_specs=out_specs, out_shape=out_shape,
                          scratch_shapes=list(scratch), input_output_aliases=aliases or {},
                          compiler_params=pltpu.CompilerParams(**params))


def _row_tile(rows, want):
    t = min(rows, want)
    while rows % t:
        t //= 2
    return t


def _sds(shape, dtype):
    return jax.ShapeDtypeStruct(tuple(shape), dtype)


def _sigmoid(v):
    return 1.0 / (1.0 + jnp.exp(-v))


def _exchange(groups, *, gather, name):
    flat = [a for g in groups for a in g]
    n_in = len(flat)
    out_shapes = []
    for g in groups:
        s = g[0].shape if gather else g[0].shape[1:]
        lead = (NDEV,) if len(g) == 1 else (NDEV, len(g))
        out_shapes.append(_sds(lead + tuple(s), g[0].dtype))
    n_g = len(groups)

    def body(*refs):
        ins, outs = refs[:n_in], refs[n_in:n_in + n_g]
        send_sems, recv_sems, loc_sems = refs[n_in + n_g:]
        x, y, c = lax.axis_index("x"), lax.axis_index("y"), lax.axis_index("c")
        me = 4 * x + 2 * y + c
        locs, k = [], 0
        for gi, g in enumerate(groups):
            for li in range(len(g)):
                src = ins[k] if gather else ins[k].at[me]
                dst = outs[gi].at[me] if len(g) == 1 else outs[gi].at[me, li]
                cp = pltpu.make_async_copy(src, dst, loc_sems.at[k])
                cp.start()
                locs.append(cp)
                k += 1
        k0 = 0
        for gi, g in enumerate(groups):
            for r in range(1, NDEV):
                px = 1 - x if r & 4 else x
                py = 1 - y if r & 2 else y
                pc = 1 - c if r & 1 else c
                peer = 4 * px + 2 * py + pc
                for li in range(len(g)):
                    src = ins[k0 + li] if gather else ins[k0 + li].at[peer]
                    dst = outs[gi].at[me] if len(g) == 1 else outs[gi].at[me, li]
                    pltpu.make_async_remote_copy(src_ref=src, dst_ref=dst, send_sem=send_sems.at[gi * (NDEV - 1) + r - 1],
                                                 recv_sem=recv_sems.at[gi * (NDEV - 1) + r - 1], device_id=(px, py, pc),
                                                 device_id_type=MESH).start()
            k0 += len(g)
        for gi, g in enumerate(groups):
            for r in range(1, NDEV):
                px = 1 - x if r & 4 else x
                py = 1 - y if r & 2 else y
                pc = 1 - c if r & 1 else c
                peer = 4 * px + 2 * py + pc
                slab = pltpu.make_async_remote_copy(src_ref=outs[gi].at[me], dst_ref=outs[gi].at[peer],
                                                    send_sem=send_sems.at[gi * (NDEV - 1) + r - 1], recv_sem=recv_sems.at[gi * (NDEV - 1) + r - 1],
                                                    device_id=(px, py, pc), device_id_type=MESH)
                slab.wait_send()
                slab.wait_recv()
        for cp in locs:
            cp.wait()

    hbm = pl.BlockSpec(memory_space=pltpu.HBM)
    outs = _call(body, name=name, out_shape=tuple(out_shapes), in_specs=[hbm] * n_in, out_specs=tuple([hbm] * n_g),
                 scratch=[pltpu.SemaphoreType.DMA((n_g * (NDEV - 1),)), pltpu.SemaphoreType.DMA((n_g * (NDEV - 1),)),
                          pltpu.SemaphoreType.DMA((n_in,))])(*flat)
    return list(outs)


def _peer_of(x, y, c, r):
    return (1 - x if r & 4 else x, 1 - y if r & 2 else y, 1 - c if r & 1 else c)


def _exchange_start(groups, *, gather, name, after=None):
    flat = [pltpu.with_memory_space_constraint(a, pltpu.HBM) for g in groups for a in g]
    n_in, n_g = len(flat), len(groups)
    land_shapes = []
    for g in groups:
        s = g[0].shape if gather else g[0].shape[1:]
        lead = (NDEV,) if len(g) == 1 else (NDEV, len(g))
        land_shapes.append((lead + tuple(s), g[0].dtype))
    lands = [pltpu.with_memory_space_constraint(lax.empty(s, d), pltpu.HBM) for s, d in land_shapes]
    n_after = 0 if after is None else 1

    def body(*refs):
        ins, land = refs[:n_in], refs[n_in:n_in + n_g]
        send_sems, recv_sems, loc_sems = refs[n_in + n_g + n_after:n_in + n_g + n_after + 3]
        token = refs[-1]
        x, y, c = lax.axis_index("x"), lax.axis_index("y"), lax.axis_index("c")
        me = 4 * x + 2 * y + c
        k = 0
        for gi, g in enumerate(groups):
            for li in range(len(g)):
                dst = land[gi].at[me] if len(g) == 1 else land[gi].at[me, li]
                pltpu.make_async_copy(ins[k] if gather else ins[k].at[me], dst, loc_sems.at[k]).start()
                k += 1
        k0 = 0
        for gi, g in enumerate(groups):
            for r in range(1, NDEV):
                px, py, pc = _peer_of(x, y, c, r)
                peer = 4 * px + 2 * py + pc
                for li in range(len(g)):
                    dst = land[gi].at[me] if len(g) == 1 else land[gi].at[me, li]
                    pltpu.make_async_remote_copy(src_ref=ins[k0 + li] if gather else ins[k0 + li].at[peer], dst_ref=dst,
                                                 send_sem=send_sems.at[gi * (NDEV - 1) + r - 1], recv_sem=recv_sems.at[gi * (NDEV - 1) + r - 1],
                                                 device_id=(px, py, pc), device_id_type=MESH).start()
            k0 += len(g)
        token[...] = jnp.zeros_like(token)

    hbm = pl.BlockSpec(memory_space=pltpu.HBM)
    sem = pl.BlockSpec(memory_space=pltpu.SEMAPHORE)
    args = flat + lands + ([after] if n_after else [])
    outs = pl.pallas_call(body, name=name,
        out_shape=(pltpu.SemaphoreType.DMA((n_g * (NDEV - 1),)), pltpu.SemaphoreType.DMA((n_g * (NDEV - 1),)),
                   pltpu.SemaphoreType.DMA((n_in,)), *[pltpu.HBM(a.shape, a.dtype) for a in flat],
                   *[pltpu.HBM(s, d) for s, d in land_shapes], _sds((8, LANES), F32)),
        in_specs=[hbm] * (n_in + n_g) + [pl.BlockSpec(memory_space=pl.ANY)] * n_after,
        out_specs=(sem, sem, sem, *[hbm] * (n_in + n_g), pl.BlockSpec(memory_space=pltpu.VMEM)),
        input_output_aliases={k: 3 + k for k in range(n_in + n_g)},
        compiler_params=pltpu.CompilerParams(has_side_effects=pltpu.SideEffectType.DATAFLOW_SIDE_EFFECTING))(*args)
    handle = dict(sems=outs[0:3], srcs=list(outs[3:3 + n_in]), lands=list(outs[3 + n_in:3 + n_in + n_g]),
                  sizes=[len(g) for g in groups], gather=gather)
    return handle, outs[-1]


def _exchange_wait(handle, *, name, after):
    srcs, lands, sizes, gather = handle["srcs"], handle["lands"], handle["sizes"], handle["gather"]
    n_in, n_g = len(srcs), len(lands)

    def body(*refs):
        ins, land = refs[:n_in], refs[n_in:n_in + n_g]
        send_sems, recv_sems, loc_sems = refs[n_in + n_g:n_in + n_g + 3]
        x, y, c = lax.axis_index("x"), lax.axis_index("y"), lax.axis_index("c")
        me = 4 * x + 2 * y + c
        for gi in range(n_g):
            for r in range(1, NDEV):
                px, py, pc = _peer_of(x, y, c, r)
                peer = 4 * px + 2 * py + pc
                slab = pltpu.make_async_remote_copy(src_ref=land[gi].at[me], dst_ref=land[gi].at[peer],
                                                    send_sem=send_sems.at[gi * (NDEV - 1) + r - 1], recv_sem=recv_sems.at[gi * (NDEV - 1) + r - 1],
                                                    device_id=(px, py, pc), device_id_type=MESH)
                slab.wait_send()
                slab.wait_recv()
        k = 0
        for gi in range(n_g):
            for li in range(sizes[gi]):
                dst = land[gi].at[me] if sizes[gi] == 1 else land[gi].at[me, li]
                pltpu.make_async_copy(ins[k] if gather else ins[k].at[me], dst, loc_sems.at[k]).wait()
                k += 1

    hbm = pl.BlockSpec(memory_space=pltpu.HBM)
    sem = pl.BlockSpec(memory_space=pltpu.SEMAPHORE)
    outs = pl.pallas_call(body, name=name, out_shape=tuple(pltpu.HBM(a.shape, a.dtype) for a in srcs + lands),
        in_specs=[hbm] * (n_in + n_g) + [sem] * 3 + [pl.BlockSpec(memory_space=pl.ANY)],
        out_specs=tuple([hbm] * (n_in + n_g)), input_output_aliases={k: k for k in range(n_in + n_g)},
        compiler_params=pltpu.CompilerParams(has_side_effects=pltpu.SideEffectType.DATAFLOW_SIDE_EFFECTING))(
            *srcs, *lands, *handle["sems"], after)
    return list(outs[n_in:])


def _norm_mod_fwd(x, gain, sc, sh, *, name, after=None):
    S, D = x.shape
    tm = _row_tile(S, 512)
    extra = [] if after is None else [after]

    def body(x_ref, g_ref, sc_ref, sh_ref, *rest):
        xv = x_ref[...]
        r = lax.rsqrt(jnp.mean(xv * xv, axis=-1, keepdims=True) + EPS)
        rest[-1][...] = ((xv * r) * g_ref[...] * (1.0 + sc_ref[...]) + sh_ref[...]).astype(BF16)

    row = pl.BlockSpec((tm, D), lambda i: (i, 0))
    vec = pl.BlockSpec((1, D), lambda i: (0, 0))
    return _call(body, name=name, grid=(S // tm,),
                 in_specs=[row, vec, vec, vec] + [pl.BlockSpec(memory_space=pl.ANY)] * len(extra), out_specs=row,
                 out_shape=_sds((S, D), BF16), sem=("parallel",))(x, gain, sc, sh, *extra)


def _gate_part(first, dx, y_ref, g_ref, dy_ref, dg_ref, db_ref):
    @pl.when(first)
    def _():
        dg_ref[...] = jnp.zeros_like(dg_ref)
        db_ref[...] = jnp.zeros_like(db_ref)

    dy = dx * g_ref[...]
    dy_ref[...] = dy.astype(BF16)
    dg_ref[...] += jnp.sum(dx * y_ref[...].astype(F32), axis=0, keepdims=True)
    db_ref[...] += jnp.sum(dy, axis=0, keepdims=True)


def _norm_mod_bwd(x, gain, sc, dh, dres, y_prev, gate_prev, *, name):
    S, D = x.shape
    tm = _row_tile(S, 512)
    gated = y_prev is not None

    def body(x_ref, g_ref, sc_ref, dh_ref, dres_ref, *rest):
        if gated:
            y_ref, gp_ref = rest[0], rest[1]
            rest = rest[2:]
        dx_ref, dsc_ref, dsh_ref, dg_ref = rest[:4]
        first = pl.program_id(0) == 0

        @pl.when(first)
        def _():
            dsc_ref[...] = jnp.zeros_like(dsc_ref)
            dsh_ref[...] = jnp.zeros_like(dsh_ref)
            dg_ref[...] = jnp.zeros_like(dg_ref)

        xv = x_ref[...]
        r = lax.rsqrt(jnp.mean(xv * xv, axis=-1, keepdims=True) + EPS)
        xhat = xv * r
        gain_v = g_ref[...]
        dhv = dh_ref[...]
        dsc_ref[...] += jnp.sum(dhv * (xhat * gain_v), axis=0, keepdims=True)
        dsh_ref[...] += jnp.sum(dhv, axis=0, keepdims=True)
        dxn = dhv * (1.0 + sc_ref[...])
        dg_ref[...] += jnp.sum(dxn * xhat, axis=0, keepdims=True)
        dxhat = dxn * gain_v
        dx = dres_ref[...] + r * (dxhat - xhat * jnp.mean(dxhat * xhat, axis=-1, keepdims=True))
        dx_ref[...] = dx
        if gated:
            _gate_part(first, dx, y_ref, gp_ref, *rest[4:7])

    row = pl.BlockSpec((tm, D), lambda i: (i, 0))
    vec = pl.BlockSpec((1, D), lambda i: (0, 0))
    vsh = _sds((1, D), F32)
    in_specs, args = [row, vec, vec, row, row], [x, gain, sc, dh, dres]
    out_specs, out_shape = [row, vec, vec, vec], [_sds((S, D), F32), vsh, vsh, vsh]
    if gated:
        in_specs += [row, vec]
        args += [y_prev, gate_prev]
        out_specs += [row, vec, vec]
        out_shape += [_sds((S, D), BF16), vsh, vsh]
    return _call(body, name=name, grid=(S // tm,), in_specs=in_specs, out_specs=tuple(out_specs), out_shape=tuple(out_shape),
                 sem=("arbitrary",))(*args)


def _final_loss(x, gain, target, y_prev, gate_prev, *, name):
    S, D = x.shape
    tm = _row_tile(S, 512)

    def body(x_ref, g_ref, t_ref, y_ref, gp_ref, loss_ref, dx_ref, dg_ref, dy_ref, dgp_ref, dbp_ref):
        first = pl.program_id(0) == 0

        @pl.when(first)
        def _():
            loss_ref[...] = jnp.zeros_like(loss_ref)
            dg_ref[...] = jnp.zeros_like(dg_ref)

        xv = x_ref[...]
        r = lax.rsqrt(jnp.mean(xv * xv, axis=-1, keepdims=True) + EPS)
        xhat = xv * r
        gv = g_ref[...]
        err = xhat * gv - t_ref[...]
        row_loss = jnp.mean(err * err, axis=-1, keepdims=True)
        loss_ref[...] += 0.5 * jnp.sum(row_loss, axis=0, keepdims=True)
        dy = err * (1.0 / D)
        dg_ref[...] += jnp.sum(dy * xhat, axis=0, keepdims=True)
        dxhat = dy * gv
        dx = r * (dxhat - xhat * jnp.mean(dxhat * xhat, axis=-1, keepdims=True))
        dx_ref[...] = dx
        _gate_part(first, dx, y_ref, gp_ref, dy_ref, dgp_ref, dbp_ref)

    row = pl.BlockSpec((tm, D), lambda i: (i, 0))
    vec = pl.BlockSpec((1, D), lambda i: (0, 0))
    one = pl.BlockSpec((1, 1), lambda i: (0, 0))
    vsh = _sds((1, D), F32)
    return _call(body, name=name, grid=(S // tm,), in_specs=[row, vec, row, row, vec],
                 out_specs=(one, row, vec, row, vec, vec),
                 out_shape=(_sds((1, 1), F32), _sds((S, D), F32), vsh, _sds((S, D), BF16), vsh, vsh),
                 sem=("arbitrary",))(x, gain, target, y_prev, gate_prev)


def _pick_tm(M, bytes_per_row, fixed_bytes):
    for tm in (1024, 512, 256, 128):
        if M % tm == 0 and 2 * tm * bytes_per_row + fixed_bytes <= VMEM_BLOCK_BUDGET:
            return tm
    return _row_tile(M, 128)


def _mm_nn(a, w, *, name, bias=None, relu2=False, ln=None, res=None, gate=None, out_dtype=BF16):
    M, K = a.shape
    col = w.ndim == 3
    if col:
        nsh, ns = w.shape[0], w.shape[2]
        w_spec = pl.BlockSpec((nsh, K, ns), lambda i: (0, 0, 0))
    else:
        nsh, ns = 1, w.shape[1]
        w_spec = pl.BlockSpec((K, ns), lambda i: (0, 0))
    N = nsh * ns
    residual = res is not None
    out_bytes = (4 + 4 + 2) if residual else jnp.dtype(out_dtype).itemsize
    tm = _pick_tm(M, K * a.dtype.itemsize + N * out_bytes, 2 * K * N * 2)

    def body(*refs):
        it = iter(refs)
        a_ref, w_ref = next(it), next(it)
        b_ref = next(it) if bias is not None else None
        lg_ref, lb_ref = (next(it), next(it)) if ln is not None else (None, None)
        res_ref, gate_ref = (next(it), next(it)) if residual else (None, None)
        out_ref = next(it)
        raw_ref = next(it) if residual else None
        av = a_ref[...]
        if relu2:
            av = jnp.square(jnp.maximum(av.astype(F32), 0.0))
        if ln is not None:
            av, _ = _ln_silu(av, lg_ref[...], lb_ref[...])
        ab = av.astype(BF16)
        for d in range(nsh):
            cols = slice(d * ns, (d + 1) * ns)
            acc = jnp.dot(ab, w_ref[d] if col else w_ref[...], preferred_element_type=F32)
            if b_ref is not None:
                acc = acc + b_ref[:, cols]
            if residual:
                raw_ref[:, cols] = acc.astype(BF16)
                out_ref[:, cols] = res_ref[:, cols] + gate_ref[:, cols] * acc
            else:
                out_ref[:, cols] = acc.astype(out_dtype)

    tile = pl.BlockSpec((tm, N), lambda i: (i, 0))
    vec = pl.BlockSpec((1, N), lambda i: (0, 0))
    in_specs, args = [pl.BlockSpec((tm, K), lambda i: (i, 0)), w_spec], [a, w]
    if bias is not None:
        in_specs.append(vec)
        args.append(bias)
    if ln is not None:
        in_specs += [pl.BlockSpec((1, K), lambda i: (0, 0))] * 2
        args += list(ln)
    if residual:
        in_specs += [tile, vec]
        args += [res, gate]
        out_specs = (tile, tile)
        out_shape = (_sds((M, N), F32), _sds((M, N), BF16))
    else:
        out_specs = tile
        out_shape = _sds((M, N), out_dtype)
    return _call(body, name=name, grid=(M // tm,), in_specs=in_specs, out_specs=out_specs, out_shape=out_shape,
                 sem=("parallel",))(*args)


def _mm_nt(g, w, *, name, z=None, out_dtype=F32, after=None):
    M, N = g.shape
    col = w.ndim == 3
    if col:
        nsh, K, ns = w.shape
        w_spec = pl.BlockSpec((nsh, K, ns), lambda i: (0, 0, 0))
    else:
        K = w.shape[0]
        w_spec = pl.BlockSpec((K, N), lambda i: (0, 0))
    kc = min(K, 1024)
    obytes = jnp.dtype(out_dtype).itemsize
    tm = _pick_tm(M, N * g.dtype.itemsize + K * obytes + (K * 2 if z is not None else 0), 2 * K * N * 2 + 512 * K * 4)

    def body(*refs):
        it = iter(refs)
        g_ref, w_ref = next(it), next(it)
        z_ref = next(it) if z is not None else None
        if after is not None:
            next(it)
        out_ref = next(it)
        if col:
            acc = None
            for d in range(nsh):
                part = lax.dot_general(g_ref[:, d * ns:(d + 1) * ns].astype(BF16), w_ref[d], NT_DIMS,
                                       preferred_element_type=F32)
                acc = part if acc is None else acc + part
            out_ref[...] = acc.astype(out_dtype)
        else:
            gb = g_ref[...].astype(BF16)
            for cki in range(K // kc):
                cols = slice(cki * kc, (cki + 1) * kc)
                part = lax.dot_general(gb, w_ref[cols, :], NT_DIMS, preferred_element_type=F32)
                if z_ref is not None:
                    part = part * (2.0 * jnp.maximum(z_ref[:, cols].astype(F32), 0.0))
                out_ref[:, cols] = part.astype(out_dtype)

    in_specs, args = [pl.BlockSpec((tm, N), lambda i: (i, 0)), w_spec], [g, w]
    if z is not None:
        in_specs.append(pl.BlockSpec((tm, K), lambda i: (i, 0)))
        args.append(z)
    if after is not None:
        in_specs.append(pl.BlockSpec(memory_space=pl.ANY))
        args.append(after)
    return _call(body, name=name, grid=(M // tm,), in_specs=in_specs, out_specs=pl.BlockSpec((tm, K), lambda i: (i, 0)),
                 out_shape=_sds((M, K), out_dtype), sem=("parallel",))(*args)


def _mm_tn(a, g, *, name, col_shards=None, relu2=False, ln=None):
    M, K = a.shape
    N = g.shape[1]
    acc_budget = 8 * 1024 * 1024
    tm = _row_tile(M, 512)
    nm = M // tm
    if col_shards:
        ns = N // col_shards
        spc = col_shards
        while spc > 1 and K * ns * spc * 4 > acc_budget:
            spc //= 2
        grid = (col_shards // spc, nm)
        a_spec = pl.BlockSpec((tm, K), lambda c, m: (m, 0))
        g_spec = pl.BlockSpec((tm, spc * ns), lambda c, m: (m, c))
        out_spec = pl.BlockSpec((spc, K, ns), lambda c, m: (c, 0, 0))
        out_shape = _sds((col_shards, K, ns), BF16)
        acc_shape = (K, spc * ns)
    else:
        tk = K
        while tk > 128 and tk * N * 4 > acc_budget:
            tk //= 2
        grid = (K // tk, nm)
        a_spec = pl.BlockSpec((tm, tk), lambda c, m: (m, c))
        g_spec = pl.BlockSpec((tm, N), lambda c, m: (m, 0))
        out_spec = pl.BlockSpec((tk, N), lambda c, m: (c, 0))
        out_shape = _sds((K, N), BF16)
        acc_shape = (tk, N)
        assert ln is None or tk == K
    in_specs, args = [a_spec, g_spec], [a, g]
    if ln is not None:
        in_specs += [pl.BlockSpec((1, K), lambda c, m: (0, 0))] * 2
        args += list(ln)

    def body(a_ref, g_ref, *rest):
        out_ref, acc_ref = rest[-2:]
        m = pl.program_id(1)

        @pl.when(m == 0)
        def _():
            acc_ref[...] = jnp.zeros_like(acc_ref)

        av = a_ref[...]
        if relu2:
            av = jnp.square(jnp.maximum(av.astype(F32), 0.0))
        if ln is not None:
            av, _ = _ln_silu(av, rest[0][...], rest[1][...])
        acc_ref[...] += lax.dot_general(av.astype(BF16), g_ref[...].astype(BF16), TN_DIMS, preferred_element_type=F32)

        @pl.when(m == nm - 1)
        def _():
            if col_shards:
                for s in range(spc):
                    out_ref[s] = acc_ref[:, s * ns:(s + 1) * ns].astype(BF16)
            else:
                out_ref[...] = acc_ref[...].astype(BF16)

    return _call(body, name=name, grid=grid, in_specs=in_specs, out_specs=out_spec, out_shape=out_shape,
                 scratch=[pltpu.VMEM(acc_shape, F32)], sem=("parallel", "arbitrary"))(*args)


CONV_TILE = 256
CONV_GROUP = 32


def _glu(u, ch):
    return u[:, :ch] * _sigmoid(u[:, ch:])


def _fill_glu(buf, u_ref, uh_ref, ch, tile):
    first = pl.program_id(0) == 0
    buf[0:HALO] = jnp.where(first, 0.0, _glu(uh_ref[...], ch))
    buf[HALO:HALO + tile] = _glu(u_ref[...], ch)


CONV_SUB = 4


def _conv_specs(S, ch, tile):
    per = tile // HALO
    u_spec = pl.BlockSpec((tile, 2 * ch, LANES), lambda i: (i, 0, 0))
    uh_spec = pl.BlockSpec((HALO, 2 * ch, LANES), lambda i: (jnp.maximum(i * per - 1, 0), 0, 0))
    x_spec = pl.BlockSpec((tile, ch, LANES), lambda i: (i, 0, 0))
    xn_spec = pl.BlockSpec((HALO, ch, LANES), lambda i: (jnp.minimum((i + 1) * per, S // HALO - 1), 0, 0))
    w_spec = pl.BlockSpec((CONV_WIDTH, ch, LANES), lambda i: (0, 0, 0))
    v_spec = pl.BlockSpec((1, ch, LANES), lambda i: (0, 0, 0))
    return u_spec, uh_spec, x_spec, xn_spec, w_spec, v_spec


def _conv_mid_fwd(u3, w3, bdw3, *, name):
    S, ch2, _ = u3.shape
    ch = ch2 // 2
    tile = _row_tile(S, CONV_TILE)
    sub = _row_tile(tile, CONV_SUB)
    u_spec, uh_spec, x_spec, _, w_spec, v_spec = _conv_specs(S, ch, tile)

    def body(u_ref, uh_ref, w_ref, b_ref, o_ref, buf):
        _fill_glu(buf, u_ref, uh_ref, ch, tile)

        def step(q, carry):
            acc = [b_ref[...], None]
            for k in range(CONV_WIDTH):
                term = buf[pl.ds(q * sub + (HALO - CONV_WIDTH + 1 + k), sub)] * w_ref[k]
                acc[k % 2] = term if acc[k % 2] is None else acc[k % 2] + term
            o_ref[pl.ds(q * sub, sub)] = acc[0] + acc[1]
            return carry

        lax.fori_loop(0, tile // sub, step, 0)

    return _call(body, name=name, grid=(S // tile,), in_specs=[u_spec, uh_spec, w_spec, v_spec], out_specs=x_spec,
                 out_shape=_sds((S, ch, LANES), F32), scratch=[pltpu.VMEM((tile + HALO, ch, LANES), F32)],
                 sem=("parallel",))(u3, u3, w3, bdw3)


def _ln_silu(v, gv, bv):
    mu = jnp.mean(v, axis=-1, keepdims=True)
    cen = v - mu
    rstd = lax.rsqrt(jnp.mean(cen * cen, axis=-1, keepdims=True) + EPS)
    nrm = cen * rstd
    ln = nrm * gv + bv
    sg = _sigmoid(ln)
    return ln * sg, (nrm, rstd, ln, sg)


def _ln_silu_bwd(dwo, ds, lng, lnb, *, name):
    S, D = dwo.shape
    tm = _row_tile(S, 512)

    def body(v_ref, ds_ref, g_ref, b_ref, ddw_ref, dg_ref, db_ref, dbdw_ref):
        @pl.when(pl.program_id(0) == 0)
        def _():
            dg_ref[...] = jnp.zeros_like(dg_ref)
            db_ref[...] = jnp.zeros_like(db_ref)
            dbdw_ref[...] = jnp.zeros_like(dbdw_ref)

        gv = g_ref[...]
        _, (nrm, rstd, ln, sg) = _ln_silu(v_ref[...], gv, b_ref[...])
        dln = ds_ref[...] * (sg * (1.0 + ln * (1.0 - sg)))
        dg_ref[...] += jnp.sum(dln * nrm, axis=0, keepdims=True)
        db_ref[...] += jnp.sum(dln, axis=0, keepdims=True)
        dn = dln * gv
        ddw = rstd * (dn - jnp.mean(dn, axis=-1, keepdims=True) - nrm * jnp.mean(dn * nrm, axis=-1, keepdims=True))
        dbdw_ref[...] += jnp.sum(ddw, axis=0, keepdims=True)
        ddw_ref[...] = ddw

    row = pl.BlockSpec((tm, D), lambda i: (i, 0))
    vec = pl.BlockSpec((1, D), lambda i: (0, 0))
    vsh = _sds((1, D), F32)
    return _call(body, name=name, grid=(S // tm,), in_specs=[row, row, vec, vec], out_specs=(row, vec, vec, vec),
                 out_shape=(_sds((S, D), F32), vsh, vsh, vsh), sem=("arbitrary",))(dwo, ds, lng, lnb)


def _conv_mid_bwd_dw(u3, ddw3, w3, *, name):
    S, ch2, _ = u3.shape
    ch = ch2 // 2
    tile = _row_tile(S, CONV_TILE)
    rows = _row_tile(tile, CONV_GROUP)
    sub = _row_tile(rows, CONV_SUB)
    last = S // tile - 1
    u_spec, uh_spec, x_spec, xn_spec, w_spec, _ = _conv_specs(S, ch, tile)
    b_spec = pl.BlockSpec((1, 2 * ch, LANES), lambda i: (0, 0, 0))

    def body(u_ref, uh_ref, d_ref, dn_ref, w_ref, du_ref, dw_ref, db_ref, gbuf, dbuf, stage):
        @pl.when(pl.program_id(0) == 0)
        def _():
            dw_ref[...] = jnp.zeros_like(dw_ref)
            db_ref[...] = jnp.zeros_like(db_ref)

        _fill_glu(gbuf, u_ref, uh_ref, ch, tile)
        dbuf[0:tile] = d_ref[...]
        dbuf[tile:tile + HALO] = jnp.where(pl.program_id(0) == last, 0.0, dn_ref[...])

        def group(r, carry):
            t0 = r * rows
            def step(q, c):
                s0 = t0 + q * sub
                ddw = dbuf[pl.ds(s0, sub)]
                acc = [None, None]
                for k in range(CONV_WIDTH):
                    term = dbuf[pl.ds(s0 + (CONV_WIDTH - 1 - k), sub)] * w_ref[k]
                    acc[k % 2] = term if acc[k % 2] is None else acc[k % 2] + term
                    dw_ref[k] += jnp.sum(ddw * gbuf[pl.ds(s0 + (HALO - CONV_WIDTH + 1 + k), sub)], axis=0)
                stage[pl.ds(q * sub, sub)] = acc[0] + acc[1]
                return c

            lax.fori_loop(0, rows // sub, step, 0)
            dglu = stage[...]
            uv = u_ref[pl.ds(t0, rows)]
            av, sg = uv[:, :ch], _sigmoid(uv[:, ch:])
            da = dglu * sg
            dg = dglu * av * sg * (1.0 - sg)
            du_ref[pl.ds(t0, rows), 0:ch] = da
            du_ref[pl.ds(t0, rows), ch:2 * ch] = dg
            db_ref[:, 0:ch] += jnp.sum(da, axis=0, keepdims=True)
            db_ref[:, ch:2 * ch] += jnp.sum(dg, axis=0, keepdims=True)
            return carry

        lax.fori_loop(0, tile // rows, group, 0)

    return _call(body, name=name, grid=(S // tile,), in_specs=[u_spec, uh_spec, x_spec, xn_spec, w_spec],
                 out_specs=(u_spec, w_spec, b_spec),
                 out_shape=(_sds((S, 2 * ch, LANES), F32), _sds((CONV_WIDTH, ch, LANES), F32), _sds((1, 2 * ch, LANES), F32)),
                 scratch=[pltpu.VMEM((tile + HALO, ch, LANES), F32), pltpu.VMEM((tile + HALO, ch, LANES), F32),
                          pltpu.VMEM((rows, ch, LANES), F32)],
                 sem=("arbitrary",))(u3, u3, ddw3, ddw3, w3)


def _ret_tables(S, dk):
    B = min(RET_BLOCK, S)
    lg = jnp.log(1.0 - 2.0 ** (-5.0 - jnp.arange(RET_HEADS, dtype=F32)))
    idx = jnp.arange(B, dtype=F32)
    diff = idx[:, None] - idx[None, :]
    cq, ck = (jnp.arange(B) // CHUNK)[:, None], (jnp.arange(B) // CHUNK)[None, :]
    dist = jnp.where(cq == ck, jnp.abs(diff), diff)
    mask = jnp.where(ck <= cq, jnp.exp(lg[:, None, None] * dist[None]), 0.0)
    xi = jnp.exp(lg[:, None] * (idx + 1.0))[..., None]
    zeta = jnp.exp(lg[:, None] * (B - 1.0 - idx))[..., None]
    gam = jnp.broadcast_to(jnp.exp(lg * B)[:, None, None], (RET_HEADS, 8, LANES))
    pos = jnp.arange(S, dtype=F32)
    inv = ROPE_BASE ** (-jnp.arange(0, dk, 2, dtype=F32) / dk)
    ang = pos[:, None] * inv[None, :]
    return dict(B=B, mask=mask, xi=xi, zeta=zeta, gam=gam, cos=jnp.cos(ang), sin=jnp.sin(ang))


def _rope(v, cs, sn):
    half = v.shape[1] // 2
    v1, v2 = v[:, :half], v[:, half:]
    return jnp.concatenate([v1 * cs - v2 * sn, v2 * cs + v1 * sn], axis=-1)


def _rope_t(d, cs, sn):
    half = d.shape[1] // 2
    d1, d2 = d[:, :half], d[:, half:]
    return jnp.concatenate([d1 * cs + d2 * sn, d2 * cs - d1 * sn], axis=-1)


def _dot(a, b):
    return jnp.dot(a.astype(BF16), b.astype(BF16), preferred_element_type=F32)


def _dot_nt(a, b):
    return lax.dot_general(a.astype(BF16), b.astype(BF16), NT_DIMS, preferred_element_type=F32)


def _dot_tn(a, b):
    return lax.dot_general(a.astype(BF16), b.astype(BF16), TN_DIMS, preferred_element_type=F32)


def _ret_specs(S, D, B, RB, reverse):
    dk, dv = D // RET_HEADS, 2 * D // RET_HEADS
    nb = S // RB
    blk = (lambda ib: nb - 1 - ib) if reverse else (lambda ib: ib)
    q = pl.BlockSpec((RB, dk), lambda h, ib: (blk(ib), h))
    k = pl.BlockSpec((RB, dk), lambda h, ib: (blk(ib), RET_HEADS + h))
    v = pl.BlockSpec((RB, dv), lambda h, ib: (blk(ib), RET_HEADS + h))
    gate = pl.BlockSpec((RB, dv), lambda h, ib: (blk(ib), 2 * RET_HEADS + h))
    yv = pl.BlockSpec((RB, dv), lambda h, ib: (blk(ib), h))
    rope = pl.BlockSpec((RB, dk // 2), lambda h, ib: (blk(ib), 0))
    mask = pl.BlockSpec((None, B, B), lambda h, ib: (h, 0, 0))
    dec = pl.BlockSpec((None, B, 1), lambda h, ib: (h, 0, 0))
    gam = pl.BlockSpec((None, 8, LANES), lambda h, ib: (h, 0, 0))
    gn = pl.BlockSpec((1, dv), lambda h, ib: (0, h))
    return dict(q=q, k=k, v=v, gate=gate, yv=yv, rope=rope, mask=mask, dec=dec, gam=gam, gn=gn)


def _group_norm(yr, gv, bv):
    mu = jnp.mean(yr, axis=-1, keepdims=True)
    cen = yr - mu
    rstd = lax.rsqrt(jnp.mean(cen * cen, axis=-1, keepdims=True) + EPS)
    nrm = cen * rstd
    return nrm, rstd, nrm * gv + bv


def _ret_fwd(proj, tb, gng, gnb, *, name):
    S, D = proj.shape[0], proj.shape[1] // 6
    dk, dv = D // RET_HEADS, 2 * D // RET_HEADS
    B = tb["B"]
    RB = _row_tile(S, 2 * B)
    nsub = RB // B
    sp = _ret_specs(S, D, B, RB, False)
    scale = dk ** -0.5

    def body(q_ref, k_ref, v_ref, gt_ref, cos_ref, sin_ref, mask_ref, xi_ref, zeta_ref, gam_ref, gng_ref, gnb_ref,
             yr_ref, yg_ref, state):
        @pl.when(pl.program_id(1) == 0)
        def _():
            state[...] = jnp.zeros_like(state)

        for sb in range(nsub):
            rows = slice(sb * B, (sb + 1) * B)
            cs, sn = cos_ref[rows, :], sin_ref[rows, :]
            q = _rope(q_ref[rows, :].astype(F32), cs, sn)
            k = _rope(k_ref[rows, :].astype(F32), cs, sn) * scale
            vb = v_ref[rows, :]
            p = _dot_nt(q, k) * mask_ref[...]
            st = state[...]
            yr = _dot(p, vb) + _dot(q * xi_ref[...], st)
            state[...] = st * gam_ref[0:1, 0:1] + _dot_tn(k * zeta_ref[...], vb)
            _, _, gn = _group_norm(yr, gng_ref[...], gnb_ref[...])
            gt = gt_ref[rows, :].astype(F32)
            yr_ref[rows, :] = yr.astype(BF16)
            yg_ref[rows, :] = (gt * _sigmoid(gt) * gn).astype(BF16)

    return _call(body, name=name, grid=(RET_HEADS, S // RB),
                 in_specs=[sp["q"], sp["k"], sp["v"], sp["gate"], sp["rope"], sp["rope"], sp["mask"], sp["dec"], sp["dec"],
                           sp["gam"], sp["gn"], sp["gn"]],
                 out_specs=(sp["yv"], sp["yv"]), out_shape=(_sds((S, 2 * D), BF16), _sds((S, 2 * D), BF16)),
                 scratch=[pltpu.VMEM((dk, dv), F32)], sem=("parallel", "arbitrary"))(
                     proj, proj, proj, proj, tb["cos"], tb["sin"], tb["mask"], tb["xi"], tb["zeta"], tb["gam"], gng, gnb)


def _ret_bwd_q(proj, yr, dyg, tb, gng, gnb, *, name):
    S, D = proj.shape[0], proj.shape[1] // 6
    dk, dv = D // RET_HEADS, 2 * D // RET_HEADS
    B = tb["B"]
    RB = _row_tile(S, 2 * B)
    nsub = RB // B
    sp = _ret_specs(S, D, B, RB, False)
    scale = dk ** -0.5

    def body(q_ref, k_ref, v_ref, gt_ref, yr_ref, dyg_ref, cos_ref, sin_ref, mask_ref, xi_ref, zeta_ref, gam_ref,
             gng_ref, gnb_ref, dq_ref, dgt_ref, dyr_ref, dgg_ref, dgb_ref, state):
        @pl.when(pl.program_id(1) == 0)
        def _():
            state[...] = jnp.zeros_like(state)
            dgg_ref[...] = jnp.zeros_like(dgg_ref)
            dgb_ref[...] = jnp.zeros_like(dgb_ref)

        for sb in range(nsub):
            rows = slice(sb * B, (sb + 1) * B)
            cs, sn = cos_ref[rows, :], sin_ref[rows, :]
            q = _rope(q_ref[rows, :].astype(F32), cs, sn)
            k = _rope(k_ref[rows, :].astype(F32), cs, sn) * scale
            vb = v_ref[rows, :]
            gv = gng_ref[...]
            nrm, rstd, gn = _group_norm(yr_ref[rows, :].astype(F32), gv, gnb_ref[...])
            gt = gt_ref[rows, :].astype(F32)
            sg = _sigmoid(gt)
            dyg = dyg_ref[rows, :].astype(F32)
            dgt_ref[rows, :] = (dyg * gn * (sg * (1.0 + gt * (1.0 - sg)))).astype(BF16)
            dgn = dyg * (gt * sg)
            dgg_ref[...] += jnp.sum(dgn * nrm, axis=0, keepdims=True)
            dgb_ref[...] += jnp.sum(dgn, axis=0, keepdims=True)
            dn = dgn * gv
            dyr = rstd * (dn - jnp.mean(dn, axis=-1, keepdims=True) - nrm * jnp.mean(dn * nrm, axis=-1, keepdims=True))
            dyr_ref[rows, :] = dyr.astype(BF16)
            dp = _dot_nt(dyr, vb) * mask_ref[...]
            st = state[...]
            dq = _dot(dp, k) + _dot_nt(dyr, st) * xi_ref[...]
            dq_ref[rows, :] = _rope_t(dq, cs, sn).astype(BF16)
            state[...] = st * gam_ref[0:1, 0:1] + _dot_tn(k * zeta_ref[...], vb)

    qout = pl.BlockSpec((RB, dk), lambda h, ib: (ib, h))
    return _call(body, name=name, grid=(RET_HEADS, S // RB),
                 in_specs=[sp["q"], sp["k"], sp["v"], sp["gate"], sp["yv"], sp["yv"], sp["rope"], sp["rope"], sp["mask"],
                           sp["dec"], sp["dec"], sp["gam"], sp["gn"], sp["gn"]],
                 out_specs=(qout, sp["yv"], sp["yv"], sp["gn"], sp["gn"]),
                 out_shape=(_sds((S, D), BF16), _sds((S, 2 * D), BF16), _sds((S, 2 * D), BF16), _sds((1, 2 * D), F32),
                            _sds((1, 2 * D), F32)),
                 scratch=[pltpu.VMEM((dk, dv), F32)], sem=("parallel", "arbitrary"))(
                     proj, proj, proj, proj, yr, dyg, tb["cos"], tb["sin"], tb["mask"], tb["xi"], tb["zeta"], tb["gam"],
                     gng, gnb)


def _ret_bwd_kv(proj, dyr, tb, *, name):
    S, D = proj.shape[0], proj.shape[1] // 6
    dk, dv = D // RET_HEADS, 2 * D // RET_HEADS
    B = tb["B"]
    RB = _row_tile(S, 2 * B)
    nsub = RB // B
    nb = S // RB
    sp = _ret_specs(S, D, B, RB, True)
    scale = dk ** -0.5

    def body(q_ref, k_ref, v_ref, dyr_ref, cos_ref, sin_ref, mask_ref, xi_ref, zeta_ref, gam_ref, dk_ref, dv_ref, dstate):
        @pl.when(pl.program_id(1) == 0)
        def _():
            dstate[...] = jnp.zeros_like(dstate)

        for sb in reversed(range(nsub)):
            rows = slice(sb * B, (sb + 1) * B)
            cs, sn = cos_ref[rows, :], sin_ref[rows, :]
            q = _rope(q_ref[rows, :].astype(F32), cs, sn)
            k = _rope(k_ref[rows, :].astype(F32), cs, sn) * scale
            vb = v_ref[rows, :]
            dyr = dyr_ref[rows, :]
            mk = mask_ref[...]
            p = _dot_nt(q, k) * mk
            dp = _dot_nt(dyr, vb) * mk
            ds = dstate[...]
            zt = zeta_ref[...]
            dkr = _dot_tn(dp, q) + _dot_nt(vb, ds) * zt
            dk_ref[rows, :] = _rope_t(dkr * scale, cs, sn).astype(BF16)
            dv_ref[rows, :] = (_dot_tn(p, dyr) + _dot(k * zt, ds)).astype(BF16)
            dstate[...] = ds * gam_ref[0:1, 0:1] + _dot_tn(q * xi_ref[...], dyr)

    kout = pl.BlockSpec((RB, dk), lambda h, ib: (nb - 1 - ib, h))
    return _call(body, name=name, grid=(RET_HEADS, nb),
                 in_specs=[sp["q"], sp["k"], sp["v"], sp["yv"], sp["rope"], sp["rope"], sp["mask"], sp["dec"], sp["dec"],
                           sp["gam"]],
                 out_specs=(kout, sp["yv"]), out_shape=(_sds((S, D), BF16), _sds((S, 2 * D), BF16)),
                 scratch=[pltpu.VMEM((dk, dv), F32)], sem=("parallel", "arbitrary"))(
                     proj, proj, proj, dyr, tb["cos"], tb["sin"], tb["mask"], tb["xi"], tb["zeta"], tb["gam"])


def _ada_fwd(c_all, ada_w, *, name):
    L, D, ns = ada_w.shape

    def body(c_ref, w_ref, out_ref):
        cv = c_ref[...]
        cond = cv * _sigmoid(cv)
        out_ref[...] = jnp.dot(cond.astype(BF16), w_ref[...].astype(BF16), preferred_element_type=F32)

    return _call(body, name=name, grid=(L,), in_specs=[pl.BlockSpec((NDEV, D), lambda l: (0, 0)),
                                                      pl.BlockSpec((None, D, ns), lambda l: (l, 0, 0))],
                 out_specs=pl.BlockSpec((None, NDEV, ns), lambda l: (l, 0, 0)), out_shape=_sds((L, NDEV, ns), F32),
                 sem=("parallel",))(c_all, ada_w)


def _ada_bwd(c_all, dmod_cols, *, name):
    L, _, ns = dmod_cols.shape
    D = c_all.shape[1]

    def body(c_ref, d_ref, out_ref):
        cv = c_ref[...]
        cond = cv * _sigmoid(cv)
        out_ref[...] = lax.dot_general(cond.astype(BF16), d_ref[...].astype(BF16), TN_DIMS, preferred_element_type=F32)

    return _call(body, name=name, grid=(L,), in_specs=[pl.BlockSpec((NDEV, D), lambda l: (0, 0)),
                                                      pl.BlockSpec((None, NDEV, ns), lambda l: (l, 0, 0))],
                 out_specs=pl.BlockSpec((None, D, ns), lambda l: (l, 0, 0)), out_shape=_sds((L, D, ns), F32),
                 sem=("parallel",))(c_all, dmod_cols)


def _adamw(w, m, v, parts, *, name):
    shape = w.shape
    L, cols = len(parts), shape[-1]
    rows = w.size // (cols * L)
    n = parts[0].shape[0]
    tr = rows
    for cand in (256, 128, 64, 32, 16, 8):
        if rows % cand == 0:
            tr = cand
            break
    c1 = 1.0 - ADAM_B1 ** ADAM_STEP
    c2 = 1.0 - ADAM_B2 ** ADAM_STEP

    def body(w_ref, m_ref, v_ref, *rest):
        p_refs = rest[:L]
        g_ref, d_ref, m2_ref, v2_ref = rest[L:]
        layer = pl.program_id(0)
        for l in range(L):
            @pl.when(layer == l)
            def _(p_ref=p_refs[l]):
                g = p_ref[0].astype(F32)
                for i in range(1, n):
                    g = g + p_ref[i].astype(F32)
                m2 = ADAM_B1 * m_ref[...] + (1.0 - ADAM_B1) * g
                v2 = ADAM_B2 * v_ref[...] + (1.0 - ADAM_B2) * (g * g)
                g_ref[...] = g
                m2_ref[...] = m2
                v2_ref[...] = v2
                d_ref[...] = -ADAM_LR * ((m2 / c1) / (jnp.sqrt(v2 / c2) + ADAM_EPS) + ADAM_WD * w_ref[...])

    mat = pl.BlockSpec((None, tr, cols), lambda l, i: (l, i, 0))

    def part_spec(k):
        return pl.BlockSpec((n, tr, cols), lambda l, i: (0, jnp.where(l == k, i, 0), 0))

    outs = _call(body, name=name, grid=(L, rows // tr), in_specs=[mat, mat, mat] + [part_spec(k) for k in range(L)],
                 out_specs=(mat, mat, mat, mat), out_shape=tuple(_sds((L, rows, cols), F32) for _ in range(4)),
                 sem=("parallel", "parallel"))(w.reshape(L, rows, cols), m.reshape(L, rows, cols), v.reshape(L, rows, cols),
                                               *[p.reshape(n, rows, cols) for p in parts])
    return tuple(o.reshape(shape) for o in outs)


SMALL = ("ada_b", "norm_mix_g", "norm_mlp_g", "conv_b_pw1", "conv_b_dw", "conv_ln_g", "conv_ln_b", "conv_b_pw2",
         "final_norm_g")
WEIGHTS = ("ada_w", "ada_b", "norm_mix_g", "norm_mlp_g", "conv_w_pw1", "conv_b_pw1", "conv_w_dw", "conv_b_dw", "conv_ln_g",
           "conv_ln_b", "conv_w_pw2", "conv_b_pw2", "ret_w_in", "ret_gn_g", "ret_gn_b", "ret_w_out", "mlp_w1", "mlp_w2",
           "final_norm_g")


def kernel(x, c, ada_w, ada_b, norm_mix_g, norm_mlp_g, conv_w_pw1, conv_b_pw1, conv_w_dw, conv_b_dw, conv_ln_g, conv_ln_b, conv_w_pw2, conv_b_pw2, ret_w_in, ret_gn_g, ret_gn_b, ret_w_out, mlp_w1, mlp_w2, final_norm_g, loss_target, m_ada_w, m_ada_b, m_norm_mix_g, m_norm_mlp_g, m_conv_w_pw1, m_conv_b_pw1, m_conv_w_dw, m_conv_b_dw, m_conv_ln_g, m_conv_ln_b, m_conv_w_pw2, m_conv_b_pw2, m_ret_w_in, m_ret_gn_g, m_ret_gn_b, m_ret_w_out, m_mlp_w1, m_mlp_w2, m_final_norm_g, v_ada_w, v_ada_b, v_norm_mix_g, v_norm_mlp_g, v_conv_w_pw1, v_conv_b_pw1, v_conv_w_dw, v_conv_b_dw, v_conv_ln_g, v_conv_ln_b, v_conv_w_pw2, v_conv_b_pw2, v_ret_w_in, v_ret_gn_g, v_ret_gn_b, v_ret_w_out, v_mlp_w1, v_mlp_w2, v_final_norm_g):
    W = dict(ada_w=ada_w, ada_b=ada_b, norm_mix_g=norm_mix_g, norm_mlp_g=norm_mlp_g, conv_w_pw1=conv_w_pw1,
             conv_b_pw1=conv_b_pw1, conv_w_dw=conv_w_dw, conv_b_dw=conv_b_dw, conv_ln_g=conv_ln_g, conv_ln_b=conv_ln_b,
             conv_w_pw2=conv_w_pw2, conv_b_pw2=conv_b_pw2, ret_w_in=ret_w_in, ret_gn_g=ret_gn_g, ret_gn_b=ret_gn_b,
             ret_w_out=ret_w_out, mlp_w1=mlp_w1, mlp_w2=mlp_w2, final_norm_g=final_norm_g)
    Mo = dict(ada_w=m_ada_w, ada_b=m_ada_b, norm_mix_g=m_norm_mix_g, norm_mlp_g=m_norm_mlp_g, conv_w_pw1=m_conv_w_pw1,
              conv_b_pw1=m_conv_b_pw1, conv_w_dw=m_conv_w_dw, conv_b_dw=m_conv_b_dw, conv_ln_g=m_conv_ln_g,
              conv_ln_b=m_conv_ln_b, conv_w_pw2=m_conv_w_pw2, conv_b_pw2=m_conv_b_pw2, ret_w_in=m_ret_w_in,
              ret_gn_g=m_ret_gn_g, ret_gn_b=m_ret_gn_b, ret_w_out=m_ret_w_out, mlp_w1=m_mlp_w1, mlp_w2=m_mlp_w2,
              final_norm_g=m_final_norm_g)
    Vo = dict(ada_w=v_ada_w, ada_b=v_ada_b, norm_mix_g=v_norm_mix_g, norm_mlp_g=v_norm_mlp_g, conv_w_pw1=v_conv_w_pw1,
              conv_b_pw1=v_conv_b_pw1, conv_w_dw=v_conv_w_dw, conv_b_dw=v_conv_b_dw, conv_ln_g=v_conv_ln_g,
              conv_ln_b=v_conv_ln_b, conv_w_pw2=v_conv_w_pw2, conv_b_pw2=v_conv_b_pw2, ret_w_in=v_ret_w_in,
              ret_gn_g=v_ret_gn_g, ret_gn_b=v_ret_gn_b, ret_w_out=v_ret_w_out, mlp_w1=v_mlp_w1, mlp_w2=v_mlp_w2,
              final_norm_g=v_final_norm_g)

    S, D = x.shape[1], x.shape[2]
    CH = D // LANES
    n_conv, n_ret = conv_w_pw1.shape[0], ret_w_in.shape[0]
    me = 4 * lax.axis_index("x") + 2 * lax.axis_index("y") + lax.axis_index("c")
    xs = x.reshape(S, D)
    target = loss_target.reshape(S, D)

    def mixer_shards(i):
        j = i // 2
        if i % 2 == 0:
            return [[conv_w_pw1[j].astype(BF16)], [conv_w_pw2[j].astype(BF16)]]
        return [[ret_w_in[j].astype(BF16)], [ret_w_out[j].astype(BF16)]]

    def mlp_shards(i):
        return [[mlp_w1[i].astype(BF16)], [mlp_w2[i].astype(BF16)]]

    def mixer_weights(i, got):
        return got[0], got[1].reshape(-1, D)

    def mlp_weights(got):
        return got[0], got[1].reshape(4 * D, D)

    first_handle, _ = _exchange_start(mixer_shards(0)[:1], gather=True, name="gather_start_first")
    small = _exchange([[conv_w_dw], [ret_gn_g], [ret_gn_b], [c]], gather=True, name="gather_small")
    dw_g, gng_g, gnb_g, c_g = small
    dw3 = jnp.transpose(dw_g, (1, 2, 0, 3)).reshape(n_conv, CONV_WIDTH, CH, LANES)
    gng_full = jnp.transpose(gng_g, (1, 2, 0, 3)).reshape(n_ret, 1, 2 * D)
    gnb_full = jnp.transpose(gnb_g, (1, 2, 0, 3)).reshape(n_ret, 1, 2 * D)
    c_all = c_g.reshape(NDEV, D)

    mod_cols = _ada_fwd(c_all, ada_w, name="ada_fwd")
    mod_all = _exchange([[mod_cols]], gather=True, name="gather_mod")[0]
    mod = lax.dynamic_index_in_dim(mod_all, me, axis=2, keepdims=False)
    mod = jnp.transpose(mod, (1, 0, 2)).reshape(DEPTH, 6 * D) + ada_b
    mods = [[mod[i, j * D:(j + 1) * D].reshape(1, D) for j in range(6)] for i in range(DEPTH)]
    tb = _ret_tables(S, D // RET_HEADS)

    def vec(a):
        return a.reshape(1, -1)

    mix_w = (_exchange_wait(first_handle, name="gather_wait_first", after=mod)[0], None)
    rest_handle, token = _exchange_start(mixer_shards(0)[1:] + mlp_shards(0), gather=True, name="gather_start_rest0",
                                         after=mix_w[0])
    saved = []
    weights = []
    xcur = xs
    for i in range(DEPTH):
        sh1, sc1, g1, sh2, sc2, g2 = mods[i]
        j = i // 2
        if i + 1 < DEPTH:
            next_handle, token = _exchange_start(mixer_shards(i + 1) + mlp_shards(i + 1), gather=True,
                                                 name=f"gather_start{i + 1}", after=mix_w[0])
        st = dict(x_in=xcur)
        h = _norm_mod_fwd(xcur, vec(norm_mix_g[i]), sc1, sh1, name=f"norm_mix_fwd{i}", after=token)
        st["h"] = h
        if i % 2 == 0:
            u = _mm_nn(h, mix_w[0], bias=vec(conv_b_pw1[j]), out_dtype=F32, name=f"pw1_fwd{i}")
            u3 = u.reshape(S, 2 * CH, LANES)
            dwo = _conv_mid_fwd(u3, dw3[j], conv_b_dw[j].reshape(1, CH, LANES), name=f"conv_mid_fwd{i}").reshape(S, D)
            if i == 0:
                got = _exchange_wait(rest_handle, name="gather_wait_rest0", after=dwo)
                mix_w, mlp_w = (mix_w[0], got[0].reshape(-1, D)), mlp_weights(got[1:3])
            xcur, y_raw = _mm_nn(dwo, mix_w[1], ln=(vec(conv_ln_g[j]), vec(conv_ln_b[j])), bias=vec(conv_b_pw2[j]), res=xcur,
                                 gate=g1, name=f"pw2_fwd{i}")
            st.update(u3=u3, dwo=dwo, y_raw=y_raw)
        else:
            proj = _mm_nn(h, mix_w[0], name=f"ret_in_fwd{i}")
            yr, yg = _ret_fwd(proj, tb, gng_full[j], gnb_full[j], name=f"ret_fwd{i}")
            xcur, y_raw = _mm_nn(yg, mix_w[1], res=xcur, gate=g1, name=f"ret_out_fwd{i}")
            st.update(proj=proj, yr=yr, yg=yg, y_raw=y_raw)
        st["x_mid"] = xcur
        h2 = _norm_mod_fwd(xcur, vec(norm_mlp_g[i]), sc2, sh2, name=f"norm_mlp_fwd{i}")
        z = _mm_nn(h2, mlp_w[0], name=f"mlp1_fwd{i}")
        xcur, o_raw = _mm_nn(z, mlp_w[1], relu2=True, res=xcur, gate=g2, name=f"mlp2_fwd{i}")
        st.update(h2=h2, z=z, o_raw=o_raw)
        saved.append(st)
        weights.append(mix_w + mlp_w)
        if i + 1 < DEPTH:
            got = _exchange_wait(next_handle, name=f"gather_wait{i + 1}", after=xcur)
            mix_w, mlp_w = mixer_weights(i + 1, got[0:2]), mlp_weights(got[2:4])

    g2_last = mods[DEPTH - 1][5]
    loss_local, dx, d_final_g, dy, dgate, _ = _final_loss(xcur, vec(final_norm_g), target, saved[-1]["o_raw"], g2_last,
                                                          name="final_loss")
    loss = lax.psum(loss_local[0, 0], AXES)

    dmod_rows = [None] * DEPTH
    d_mix_g, d_mlp_g = [None] * DEPTH, [None] * DEPTH
    d_pw1, d_pw2, d_win, d_wout = [None] * n_conv, [None] * n_conv, [None] * n_ret, [None] * n_ret
    d_w1, d_w2 = [None] * DEPTH, [None] * DEPTH
    d_bpw1, d_bdw, d_lng, d_lnb, d_bpw2, d_dw = ([None] * n_conv for _ in range(6))
    d_gng, d_gnb = [None] * n_ret, [None] * n_ret

    def gn_parts(d):
        return jnp.transpose(d.reshape(RET_HEADS, NDEV, -1), (1, 0, 2))

    grad_handles = [None] * DEPTH
    token = None
    for i in reversed(range(DEPTH)):
        sh1, sc1, g1, sh2, sc2, g2 = mods[i]
        j = i // 2
        st = saved[i]
        mix_a, mix_b, w1_i, w2_i = weights[i]
        do, dg2 = dy, dgate
        dz = _mm_nt(do, w2_i, z=st["z"], out_dtype=BF16, name=f"mlp2_bwd_x{i}", after=token)
        d_w2[i] = _mm_tn(st["z"], do, relu2=True, name=f"mlp2_bwd_w{i}")
        dh2 = _mm_nt(dz, w1_i, name=f"mlp1_bwd_x{i}")
        d_w1[i] = _mm_tn(st["h2"], dz, col_shards=NDEV, name=f"mlp1_bwd_w{i}")
        mlp_groups = [[d_w1[i]], [d_w2[i].reshape(NDEV, 4 * D // NDEV, D)]]
        dx, dsc2, dsh2, d_mlp_g[i], dy, dg1, dby = _norm_mod_bwd(st["x_mid"], vec(norm_mlp_g[i]), sc2, dh2, dx, st["y_raw"],
                                                                 g1, name=f"norm_mlp_bwd{i}")
        token = None
        if i == 0:
            mlp0_handle, token = _exchange_start(mlp_groups, gather=False, name="grads_start_mlp0")
            mlp_groups = []
        if i % 2 == 0:
            d_bpw2[j] = dby
            ds = _mm_nt(dy, mix_b, name=f"pw2_bwd_x{i}", after=token)
            ln_gb = (vec(conv_ln_g[j]), vec(conv_ln_b[j]))
            d_pw2[j] = _mm_tn(st["dwo"], dy, ln=ln_gb, name=f"pw2_bwd_w{i}")
            ddw, d_lng[j], d_lnb[j], d_bdw[j] = _ln_silu_bwd(st["dwo"], ds, *ln_gb, name=f"conv_ln_bwd{i}")
            du3, ddw_w, dbu = _conv_mid_bwd_dw(st["u3"], ddw.reshape(S, CH, LANES), dw3[j], name=f"conv_mid_bwd_dw{i}")
            d_dw[j], d_bpw1[j] = ddw_w.reshape(CONV_WIDTH, D), dbu.reshape(2, D)
            du = du3.reshape(S, 2 * D)
            mix_groups = [[d_pw2[j].reshape(NDEV, D // NDEV, D)],
                          [jnp.transpose(d_dw[j].reshape(CONV_WIDTH, NDEV, D // NDEV), (1, 0, 2))]]
            token = None
            if i == 0:
                conv0_handle, token = _exchange_start(mix_groups, gather=False, name="grads_start_conv0")
                mix_groups = []
            dh = _mm_nt(du, mix_a, name=f"pw1_bwd_x{i}", after=token)
            d_pw1[j] = _mm_tn(st["h"], du, col_shards=NDEV, name=f"pw1_bwd_w{i}")
            mix_groups = [[d_pw1[j]]] + mix_groups
        else:
            dyg = _mm_nt(dy, mix_b, out_dtype=BF16, name=f"ret_out_bwd_x{i}")
            d_wout[j] = _mm_tn(st["yg"], dy, name=f"ret_out_bwd_w{i}")
            dq, dgt, dyr, d_gng[j], d_gnb[j] = _ret_bwd_q(st["proj"], st["yr"], dyg, tb, gng_full[j], gnb_full[j],
                                                          name=f"ret_bwd_q{i}")
            dk_, dv_ = _ret_bwd_kv(st["proj"], dyr, tb, name=f"ret_bwd_kv{i}")
            dproj = jnp.concatenate([dq, dk_, dv_, dgt], axis=1)
            dh = _mm_nt(dproj, mix_a, name=f"ret_in_bwd_x{i}")
            d_win[j] = _mm_tn(st["h"], dproj, col_shards=NDEV, name=f"ret_in_bwd_w{i}")
            mix_groups = [[d_win[j]], [d_wout[j].reshape(NDEV, 2 * D // NDEV, D)], [gn_parts(d_gng[j])],
                          [gn_parts(d_gnb[j])]]
        grad_handles[i], token = _exchange_start(mix_groups + mlp_groups, gather=False, name=f"grads_start{i}")
        y_prev, gate_prev = (saved[i - 1]["o_raw"], mods[i - 1][5]) if i > 0 else (None, None)
        outs = _norm_mod_bwd(st["x_in"], vec(norm_mix_g[i]), sc1, dh, dx, y_prev, gate_prev, name=f"norm_mix_bwd{i}")
        dx, dsc1, dsh1, d_mix_g[i] = outs[:4]
        if i > 0:
            dy, dgate = outs[4], outs[5]
        dmod_rows[i] = jnp.concatenate([dsh1, dsc1, dg1, dsh2, dsc2, dg2], axis=0)
    grad_x = dx.reshape(1, S, D)

    small_local = jnp.concatenate(dmod_rows + d_mix_g + d_mlp_g + d_bpw1 + d_bdw + d_lng + d_lnb + d_bpw2 + [d_final_g],
                                  axis=0)
    small_all = _exchange([[small_local]], gather=True, name="gather_small_grads")[0]

    def pack(src):
        return jnp.concatenate([src[n].reshape(-1, D) for n in SMALL], axis=0)[None]

    sm = _adamw(pack(W), pack(Mo), pack(Vo), [small_all], name="adamw_small")
    results = {}
    row = 0
    for n in SMALL:
        cnt = W[n].size // D
        results[n] = tuple(o[0, row:row + cnt].reshape(W[n].shape) for o in sm)
        row += cnt

    ns_ada = ada_w.shape[2]
    dmod_all = small_all[:, :6 * DEPTH, :].reshape(NDEV, DEPTH, 6 * D)
    dmod_cols = jnp.transpose(lax.dynamic_slice_in_dim(dmod_all, me * ns_ada, ns_ada, axis=2), (1, 0, 2))
    g_ada = _ada_bwd(c_all, dmod_cols, name="ada_bwd")
    flat_ada = (1, DEPTH * D, ns_ada)
    ada_res = _adamw(ada_w.reshape(flat_ada), m_ada_w.reshape(flat_ada), v_ada_w.reshape(flat_ada),
                     [g_ada.reshape(flat_ada)], name="adamw_ada_w")
    results["ada_w"] = tuple(o.reshape(ada_w.shape) for o in ada_res)

    def update(names, parts):
        for n in names:
            results[n] = _adamw(W[n], Mo[n], Vo[n], parts[n], name=f"adamw_{n}")

    got = {i: _exchange_wait(grad_handles[i], name=f"grads_wait{i}", after=dx) for i in range(DEPTH - 1, 0, -1)}
    ret_layers = [i for i in range(DEPTH) if i % 2 == 1]
    update(("ret_w_in", "ret_w_out", "ret_gn_g", "ret_gn_b"),
           dict(ret_w_in=[got[i][0] for i in ret_layers], ret_w_out=[got[i][1] for i in ret_layers],
                ret_gn_g=[got[i][2] for i in ret_layers], ret_gn_b=[got[i][3] for i in ret_layers]))
    got_mlp0 = _exchange_wait(mlp0_handle, name="grads_wait_mlp0", after=results["ret_w_in"][0])
    update(("mlp_w1", "mlp_w2"),
           dict(mlp_w1=[got_mlp0[0]] + [got[i][-2] for i in range(1, DEPTH)],
                mlp_w2=[got_mlp0[1]] + [got[i][-1] for i in range(1, DEPTH)]))
    got_conv0 = _exchange_wait(conv0_handle, name="grads_wait_conv0", after=results["mlp_w1"][0])
    got[0] = _exchange_wait(grad_handles[0], name="grads_wait0", after=got_conv0[0]) + got_conv0
    conv_layers = [i for i in range(DEPTH) if i % 2 == 0]
    update(("conv_w_pw1", "conv_w_pw2", "conv_w_dw"),
           dict(conv_w_pw1=[got[i][0] for i in conv_layers], conv_w_pw2=[got[i][1] for i in conv_layers],
                conv_w_dw=[got[i][2] for i in conv_layers]))

    outs = [loss, grad_x]
    for kind in range(4):
        outs += [results[n][kind] for n in WEIGHTS]
    return tuple(outs)
```

```python
import functools

import jax
import jax.numpy as jnp
from jax import lax
from jax.experimental import pallas as pl
from jax.experimental.pallas import tpu as pltpu

F32, BF16 = jnp.float32, jnp.bfloat16
AXES = ("x", "y", "c")
NDEV = 8
DEPTH = 4
EPS = 1e-6
CHUNK = 64
CONV_WIDTH = 31
HALO = 32
RET_HEADS = 4
RET_BLOCK = 256
ROPE_BASE = 10000.0
LANES = 128
ADAM_LR, ADAM_B1, ADAM_B2, ADAM_EPS, ADAM_WD, ADAM_STEP = 0.001, 0.9, 0.999, 1e-08, 0.01, 10
VMEM_LIMIT = 56 * 1024 * 1024
VMEM_BLOCK_BUDGET = 36 * 1024 * 1024
MESH = pl.DeviceIdType.MESH
NT_DIMS = (((1,), (1,)), ((), ()))
TN_DIMS = (((0,), (0,)), ((), ()))


def _call(body, *, name, out_shape, in_specs, out_specs, grid=(), scratch=(), sem=None, aliases=None):
    params = dict(vmem_limit_bytes=VMEM_LIMIT)
    if sem is not None:
        params["dimension_semantics"] = sem
    return pl.pallas_call(body, name=name, grid=grid, in_specs=in_specs, out_specs=out_specs, out_shape=out_shape,
                          scratch_shapes=list(scratch), input_output_aliases=aliases or {},
                          compiler_params=pltpu.CompilerParams(**params))


def _row_tile(rows, want):
    t = min(rows, want)
    while rows % t:
        t //= 2
    return t


def _sds(shape, dtype):
    return jax.ShapeDtypeStruct(tuple(shape), dtype)


def _sigmoid(v):
    return 1.0 / (1.0 + jnp.exp(-v))


def _exchange(groups, *, gather, name):
    flat = [a for g in groups for a in g]
    n_in = len(flat)
    out_shapes = []
    for g in groups:
        s = g[0].shape if gather else g[0].shape[1:]
        lead = (NDEV,) if len(g) == 1 else (NDEV, len(g))
        out_shapes.append(_sds(lead + tuple(s), g[0].dtype))
    n_g = len(groups)

    def body(*refs):
        ins, outs = refs[:n_in], refs[n_in:n_in + n_g]
        send_sems, recv_sems, loc_sems = refs[n_in + n_g:]
        x, y, c = lax.axis_index("x"), lax.axis_index("y"), lax.axis_index("c")
        me = 4 * x + 2 * y + c
        locs, k = [], 0
        for gi, g in enumerate(groups):
            for li in range(len(g)):
                src = ins[k] if gather else ins[k].at[me]
                dst = outs[gi].at[me] if len(g) == 1 else outs[gi].at[me, li]
                cp = pltpu.make_async_copy(src, dst, loc_sems.at[k])
                cp.start()
                locs.append(cp)
                k += 1
        k0 = 0
        for gi, g in enumerate(groups):
            for r in range(1, NDEV):
                px = 1 - x if r & 4 else x
                py = 1 - y if r & 2 else y
                pc = 1 - c if r & 1 else c
                peer = 4 * px + 2 * py + pc
                for li in range(len(g)):
                    src = ins[k0 + li] if gather else ins[k0 + li].at[peer]
                    dst = outs[gi].at[me] if len(g) == 1 else outs[gi].at[me, li]
                    pltpu.make_async_remote_copy(src_ref=src, dst_ref=dst, send_sem=send_sems.at[gi * (NDEV - 1) + r - 1],
                                                 recv_sem=recv_sems.at[gi * (NDEV - 1) + r - 1], device_id=(px, py, pc),
                                                 device_id_type=MESH).start()
            k0 += len(g)
        for gi, g in enumerate(groups):
            for r in range(1, NDEV):
                px = 1 - x if r & 4 else x
                py = 1 - y if r & 2 else y
                pc = 1 - c if r & 1 else c
                peer = 4 * px + 2 * py + pc
                slab = pltpu.make_async_remote_copy(src_ref=outs[gi].at[me], dst_ref=outs[gi].at[peer],
                                                    send_sem=send_sems.at[gi * (NDEV - 1) + r - 1], recv_sem=recv_sems.at[gi * (NDEV - 1) + r - 1],
                                                    device_id=(px, py, pc), device_id_type=MESH)
                slab.wait_send()
                slab.wait_recv()
        for cp in locs:
            cp.wait()

    hbm = pl.BlockSpec(memory_space=pltpu.HBM)
    outs = _call(body, name=name, out_shape=tuple(out_shapes), in_specs=[hbm] * n_in, out_specs=tuple([hbm] * n_g),
                 scratch=[pltpu.SemaphoreType.DMA((n_g * (NDEV - 1),)), pltpu.SemaphoreType.DMA((n_g * (NDEV - 1),)),
                          pltpu.SemaphoreType.DMA((n_in,))])(*flat)
    return list(outs)


def _peer_of(x, y, c, r):
    return (1 - x if r & 4 else x, 1 - y if r & 2 else y, 1 - c if r & 1 else c)


def _exchange_start(groups, *, gather, name, after=None):
    flat = [pltpu.with_memory_space_constraint(a, pltpu.HBM) for g in groups for a in g]
    n_in, n_g = len(flat), len(groups)
    land_shapes = []
    for g in groups:
        s = g[0].shape if gather else g[0].shape[1:]
        lead = (NDEV,) if len(g) == 1 else (NDEV, len(g))
        land_shapes.append((lead + tuple(s), g[0].dtype))
    lands = [pltpu.with_memory_space_constraint(lax.empty(s, d), pltpu.HBM) for s, d in land_shapes]
    n_after = 0 if after is None else 1

    def body(*refs):
        ins, land = refs[:n_in], refs[n_in:n_in + n_g]
        send_sems, recv_sems, loc_sems = refs[n_in + n_g + n_after:n_in + n_g + n_after + 3]
        token = refs[-1]
        x, y, c = lax.axis_index("x"), lax.axis_index("y"), lax.axis_index("c")
        me = 4 * x + 2 * y + c
        k = 0
        for gi, g in enumerate(groups):
            for li in range(len(g)):
                dst = land[gi].at[me] if len(g) == 1 else land[gi].at[me, li]
                pltpu.make_async_copy(ins[k] if gather else ins[k].at[me], dst, loc_sems.at[k]).start()
                k += 1
        k0 = 0
        for gi, g in enumerate(groups):
            for r in range(1, NDEV):
                px, py, pc = _peer_of(x, y, c, r)
                peer = 4 * px + 2 * py + pc
                for li in range(len(g)):
                    dst = land[gi].at[me] if len(g) == 1 else land[gi].at[me, li]
                    pltpu.make_async_remote_copy(src_ref=ins[k0 + li] if gather else ins[k0 + li].at[peer], dst_ref=dst,
                                                 send_sem=send_sems.at[gi * (NDEV - 1) + r - 1], recv_sem=recv_sems.at[gi * (NDEV - 1) + r - 1],
                                                 device_id=(px, py, pc), device_id_type=MESH).start()
            k0 += len(g)
        token[...] = jnp.zeros_like(token)

    hbm = pl.BlockSpec(memory_space=pltpu.HBM)
    sem = pl.BlockSpec(memory_space=pltpu.SEMAPHORE)
    args = flat + lands + ([after] if n_after else [])
    outs = pl.pallas_call(body, name=name,
        out_shape=(pltpu.SemaphoreType.DMA((n_g * (NDEV - 1),)), pltpu.SemaphoreType.DMA((n_g * (NDEV - 1),)),
                   pltpu.SemaphoreType.DMA((n_in,)), *[pltpu.HBM(a.shape, a.dtype) for a in flat],
                   *[pltpu.HBM(s, d) for s, d in land_shapes], _sds((8, LANES), F32)),
        in_specs=[hbm] * (n_in + n_g) + [pl.BlockSpec(memory_space=pl.ANY)] * n_after,
        out_specs=(sem, sem, sem, *[hbm] * (n_in + n_g), pl.BlockSpec(memory_space=pltpu.VMEM)),
        input_output_aliases={k: 3 + k for k in range(n_in + n_g)},
        compiler_params=pltpu.CompilerParams(has_side_effects=pltpu.SideEffectType.DATAFLOW_SIDE_EFFECTING))(*args)
    handle = dict(sems=outs[0:3], srcs=list(outs[3:3 + n_in]), lands=list(outs[3 + n_in:3 + n_in + n_g]),
                  sizes=[len(g) for g in groups], gather=gather)
    return handle, outs[-1]


def _exchange_wait(handle, *, name, after):
    srcs, lands, sizes, gather = handle["srcs"], handle["lands"], handle["sizes"], handle["gather"]
    n_in, n_g = len(srcs), len(lands)

    def body(*refs):
        ins, land = refs[:n_in], refs[n_in:n_in + n_g]
        send_sems, recv_sems, loc_sems = refs[n_in + n_g:n_in + n_g + 3]
        x, y, c = lax.axis_index("x"), lax.axis_index("y"), lax.axis_index("c")
        me = 4 * x + 2 * y + c
        for gi in range(n_g):
            for r in range(1, NDEV):
                px, py, pc = _peer_of(x, y, c, r)
                peer = 4 * px + 2 * py + pc
                slab = pltpu.make_async_remote_copy(src_ref=land[gi].at[me], dst_ref=land[gi].at[peer],
                                                    send_sem=send_sems.at[gi * (NDEV - 1) + r - 1], recv_sem=recv_sems.at[gi * (NDEV - 1) + r - 1],
                                                    device_id=(px, py, pc), device_id_type=MESH)
                slab.wait_send()
                slab.wait_recv()
        k = 0
        for gi in range(n_g):
            for li in range(sizes[gi]):
                dst = land[gi].at[me] if sizes[gi] == 1 else land[gi].at[me, li]
                pltpu.make_async_copy(ins[k] if gather else ins[k].at[me], dst, loc_sems.at[k]).wait()
                k += 1

    hbm = pl.BlockSpec(memory_space=pltpu.HBM)
    sem = pl.BlockSpec(memory_space=pltpu.SEMAPHORE)
    outs = pl.pallas_call(body, name=name, out_shape=tuple(pltpu.HBM(a.shape, a.dtype) for a in srcs + lands),
        in_specs=[hbm] * (n_in + n_g) + [sem] * 3 + [pl.BlockSpec(memory_space=pl.ANY)],
        out_specs=tuple([hbm] * (n_in + n_g)), input_output_aliases={k: k for k in range(n_in + n_g)},
        compiler_params=pltpu.CompilerParams(has_side_effects=pltpu.SideEffectType.DATAFLOW_SIDE_EFFECTING))(
            *srcs, *lands, *handle["sems"], after)
    return list(outs[n_in:])


def _gate_part(first, dx, y_ref, g_ref, dy_ref, dg_ref, db_ref):
    @pl.when(first)
    def _():
        dg_ref[...] = jnp.zeros_like(dg_ref)
        db_ref[...] = jnp.zeros_like(db_ref)

    dy = dx * g_ref[...]
    dy_ref[...] = dy.astype(BF16)
    dg_ref[...] += jnp.sum(dx * y_ref[...].astype(F32), axis=0, keepdims=True)
    db_ref[...] += jnp.sum(dy, axis=0, keepdims=True)


def _norm_mod_bwd(x, gain, sc, dh, dres, y_prev, gate_prev, *, name):
    S, D = x.shape
    tm = _row_tile(S, 512)
    gated = y_prev is not None

    def body(x_ref, g_ref, sc_ref, dh_ref, dres_ref, *rest):
        if gated:
            y_ref, gp_ref = rest[0], rest[1]
            rest = rest[2:]
        dx_ref, dsc_ref, dsh_ref, dg_ref = rest[:4]
        first = pl.program_id(0) == 0

        @pl.when(first)
        def _():
            dsc_ref[...] = jnp.zeros_like(dsc_ref)
            dsh_ref[...] = jnp.zeros_like(dsh_ref)
            dg_ref[...] = jnp.zeros_like(dg_ref)

        xv = x_ref[...]
        r = lax.rsqrt(jnp.mean(xv * xv, axis=-1, keepdims=True) + EPS)
        xhat = xv * r
        gain_v = g_ref[...]
        dhv = dh_ref[...]
        dsc_ref[...] += jnp.sum(dhv * (xhat * gain_v), axis=0, keepdims=True)
        dsh_ref[...] += jnp.sum(dhv, axis=0, keepdims=True)
        dxn = dhv * (1.0 + sc_ref[...])
        dg_ref[...] += jnp.sum(dxn * xhat, axis=0, keepdims=True)
        dxhat = dxn * gain_v
        dx = dres_ref[...] + r * (dxhat - xhat * jnp.mean(dxhat * xhat, axis=-1, keepdims=True))
        dx_ref[...] = dx
        if gated:
            _gate_part(first, dx, y_ref, gp_ref, *rest[4:7])

    row = pl.BlockSpec((tm, D), lambda i: (i, 0))
    vec = pl.BlockSpec((1, D), lambda i: (0, 0))
    vsh = _sds((1, D), F32)
    in_specs, args = [row, vec, vec, row, row], [x, gain, sc, dh, dres]
    out_specs, out_shape = [row, vec, vec, vec], [_sds((S, D), F32), vsh, vsh, vsh]
    if gated:
        in_specs += [row, vec]
        args += [y_prev, gate_prev]
        out_specs += [row, vec, vec]
        out_shape += [_sds((S, D), BF16), vsh, vsh]
    return _call(body, name=name, grid=(S // tm,), in_specs=in_specs, out_specs=tuple(out_specs), out_shape=tuple(out_shape),
                 sem=("arbitrary",))(*args)


def _final_loss(x, gain, target, y_prev, gate_prev, *, name):
    S, D = x.shape
    tm = _row_tile(S, 512)

    def body(x_ref, g_ref, t_ref, y_ref, gp_ref, loss_ref, dx_ref, dg_ref, dy_ref, dgp_ref, dbp_ref):
        first = pl.program_id(0) == 0

        @pl.when(first)
        def _():
            loss_ref[...] = jnp.zeros_like(loss_ref)
            dg_ref[...] = jnp.zeros_like(dg_ref)

        xv = x_ref[...]
        r = lax.rsqrt(jnp.mean(xv * xv, axis=-1, keepdims=True) + EPS)
        xhat = xv * r
        gv = g_ref[...]
        err = xhat * gv - t_ref[...]
        row_loss = jnp.mean(err * err, axis=-1, keepdims=True)
        loss_ref[...] += 0.5 * jnp.sum(row_loss, axis=0, keepdims=True)
        dy = err * (1.0 / D)
        dg_ref[...] += jnp.sum(dy * xhat, axis=0, keepdims=True)
        dxhat = dy * gv
        dx = r * (dxhat - xhat * jnp.mean(dxhat * xhat, axis=-1, keepdims=True))
        dx_ref[...] = dx
        _gate_part(first, dx, y_ref, gp_ref, dy_ref, dgp_ref, dbp_ref)

    row = pl.BlockSpec((tm, D), lambda i: (i, 0))
    vec = pl.BlockSpec((1, D), lambda i: (0, 0))
    one = pl.BlockSpec((1, 1), lambda i: (0, 0))
    vsh = _sds((1, D), F32)
    return _call(body, name=name, grid=(S // tm,), in_specs=[row, vec, row, row, vec],
                 out_specs=(one, row, vec, row, vec, vec),
                 out_shape=(_sds((1, 1), F32), _sds((S, D), F32), vsh, _sds((S, D), BF16), vsh, vsh),
                 sem=("arbitrary",))(x, gain, target, y_prev, gate_prev)


def _pick_tm(M, bytes_per_row, fixed_bytes):
    for tm in (1024, 512, 256, 128):
        if M % tm == 0 and 2 * tm * bytes_per_row + fixed_bytes <= VMEM_BLOCK_BUDGET:
            return tm
    return _row_tile(M, 128)


def _mm_nn(a, w, *, name, bias=None, relu2=False, ln=None, norm=None, res=None, gate=None, out_dtype=BF16, after=None):
    M, K = a.shape
    col = w.ndim == 3
    if col:
        nsh, ns = w.shape[0], w.shape[2]
        w_spec = pl.BlockSpec((nsh, K, ns), lambda i: (0, 0, 0))
    else:
        nsh, ns = 1, w.shape[1]
        w_spec = pl.BlockSpec((K, ns), lambda i: (0, 0))
    N = nsh * ns
    residual = res is not None
    out_bytes = (4 + 4 + 2) if residual else jnp.dtype(out_dtype).itemsize
    tm = _pick_tm(M, K * a.dtype.itemsize + N * out_bytes + (K * 2 if norm is not None else 0), 2 * K * N * 2)

    def body(*refs):
        it = iter(refs)
        a_ref, w_ref = next(it), next(it)
        b_ref = next(it) if bias is not None else None
        lg_ref, lb_ref = (next(it), next(it)) if ln is not None else (None, None)
        ng_ref, nsc_ref, nsh_ref = (next(it), next(it), next(it)) if norm is not None else (None, None, None)
        res_ref, gate_ref = (next(it), next(it)) if residual else (None, None)
        if after is not None:
            next(it)
        out_ref = next(it)
        raw_ref = next(it) if residual else None
        av = a_ref[...]
        if relu2:
            av = jnp.square(jnp.maximum(av.astype(F32), 0.0))
        if ln is not None:
            av, _ = _ln_silu(av, lg_ref[...], lb_ref[...])
        if norm is not None:
            r = lax.rsqrt(jnp.mean(av * av, axis=-1, keepdims=True) + EPS)
            av = (av * r) * ng_ref[...] * (1.0 + nsc_ref[...]) + nsh_ref[...]
        ab = av.astype(BF16)
        if norm is not None:
            next(it)[...] = ab
        for d in range(nsh):
            cols = slice(d * ns, (d + 1) * ns)
            acc = jnp.dot(ab, w_ref[d] if col else w_ref[...], preferred_element_type=F32)
            if b_ref is not None:
                acc = acc + b_ref[:, cols]
            if residual:
                raw_ref[:, cols] = acc.astype(BF16)
                out_ref[:, cols] = res_ref[:, cols] + gate_ref[:, cols] * acc
            else:
                out_ref[:, cols] = acc.astype(out_dtype)

    tile = pl.BlockSpec((tm, N), lambda i: (i, 0))
    vec = pl.BlockSpec((1, N), lambda i: (0, 0))
    in_specs, args = [pl.BlockSpec((tm, K), lambda i: (i, 0)), w_spec], [a, w]
    if bias is not None:
        in_specs.append(vec)
        args.append(bias)
    if ln is not None:
        in_specs += [pl.BlockSpec((1, K), lambda i: (0, 0))] * 2
        args += list(ln)
    if norm is not None:
        in_specs += [pl.BlockSpec((1, K), lambda i: (0, 0))] * 3
        args += list(norm)
    if residual:
        in_specs += [tile, vec]
        args += [res, gate]
        out_specs = [tile, tile]
        out_shape = [_sds((M, N), F32), _sds((M, N), BF16)]
    else:
        out_specs = [tile]
        out_shape = [_sds((M, N), out_dtype)]
    if after is not None:
        in_specs.append(pl.BlockSpec(memory_space=pl.ANY))
        args.append(after)
    if norm is not None:
        out_specs.append(pl.BlockSpec((tm, K), lambda i: (i, 0)))
        out_shape.append(_sds((M, K), BF16))
    outs = _call(body, name=name, grid=(M // tm,), in_specs=in_specs, out_specs=tuple(out_specs), out_shape=tuple(out_shape),
                 sem=("parallel",))(*args)
    return outs[0] if len(outs) == 1 else outs


def _mm_nt(g, w, *, name, z=None, out_dtype=F32, after=None):
    M, N = g.shape
    col = w.ndim == 3
    if col:
        nsh, K, ns = w.shape
        w_spec = pl.BlockSpec((nsh, K, ns), lambda i: (0, 0, 0))
    else:
        K = w.shape[0]
        w_spec = pl.BlockSpec((K, N), lambda i: (0, 0))
    kc = min(K, 1024)
    obytes = jnp.dtype(out_dtype).itemsize
    tm = _pick_tm(M, N * g.dtype.itemsize + K * obytes + (K * 2 if z is not None else 0), 2 * K * N * 2 + 512 * K * 4)

    def body(*refs):
        it = iter(refs)
        g_ref, w_ref = next(it), next(it)
        z_ref = next(it) if z is not None else None
        if after is not None:
            next(it)
        out_ref = next(it)
        if col:
            acc = None
            for d in range(nsh):
                part = lax.dot_general(g_ref[:, d * ns:(d + 1) * ns].astype(BF16), w_ref[d], NT_DIMS,
                                       preferred_element_type=F32)
                acc = part if acc is None else acc + part
            out_ref[...] = acc.astype(out_dtype)
        else:
            gb = g_ref[...].astype(BF16)
            for cki in range(K // kc):
                cols = slice(cki * kc, (cki + 1) * kc)
                part = lax.dot_general(gb, w_ref[cols, :], NT_DIMS, preferred_element_type=F32)
                if z_ref is not None:
                    part = part * (2.0 * jnp.maximum(z_ref[:, cols].astype(F32), 0.0))
                out_ref[:, cols] = part.astype(out_dtype)

    in_specs, args = [pl.BlockSpec((tm, N), lambda i: (i, 0)), w_spec], [g, w]
    if z is not None:
        in_specs.append(pl.BlockSpec((tm, K), lambda i: (i, 0)))
        args.append(z)
    if after is not None:
        in_specs.append(pl.BlockSpec(memory_space=pl.ANY))
        args.append(after)
    return _call(body, name=name, grid=(M // tm,), in_specs=in_specs, out_specs=pl.BlockSpec((tm, K), lambda i: (i, 0)),
                 out_shape=_sds((M, K), out_dtype), sem=("parallel",))(*args)


def _mm_tn(a, g, *, name, col_shards=None, relu2=False, ln=None):
    M, K = a.shape
    N = g.shape[1]
    acc_budget = 8 * 1024 * 1024
    tm = _row_tile(M, 512)
    nm = M // tm
    if col_shards:
        ns = N // col_shards
        spc = col_shards
        while spc > 1 and K * ns * spc * 4 > acc_budget:
            spc //= 2
        grid = (col_shards // spc, nm)
        a_spec = pl.BlockSpec((tm, K), lambda c, m: (m, 0))
        g_spec = pl.BlockSpec((tm, spc * ns), lambda c, m: (m, c))
        out_spec = pl.BlockSpec((spc, K, ns), lambda c, m: (c, 0, 0))
        out_shape = _sds((col_shards, K, ns), BF16)
        acc_shape = (K, spc * ns)
    else:
        tk = K
        while tk > 128 and tk * N * 4 > acc_budget:
            tk //= 2
        grid = (K // tk, nm)
        a_spec = pl.BlockSpec((tm, tk), lambda c, m: (m, c))
        g_spec = pl.BlockSpec((tm, N), lambda c, m: (m, 0))
        out_spec = pl.BlockSpec((tk, N), lambda c, m: (c, 0))
        out_shape = _sds((K, N), BF16)
        acc_shape = (tk, N)
        assert ln is None or tk == K
    in_specs, args = [a_spec, g_spec], [a, g]
    if ln is not None:
        in_specs += [pl.BlockSpec((1, K), lambda c, m: (0, 0))] * 2
        args += list(ln)

    def body(a_ref, g_ref, *rest):
        out_ref, acc_ref = rest[-2:]
        m = pl.program_id(1)

        @pl.when(m == 0)
        def _():
            acc_ref[...] = jnp.zeros_like(acc_ref)

        av = a_ref[...]
        if relu2:
            av = jnp.square(jnp.maximum(av.astype(F32), 0.0))
        if ln is not None:
            av, _ = _ln_silu(av, rest[0][...], rest[1][...])
        acc_ref[...] += lax.dot_general(av.astype(BF16), g_ref[...].astype(BF16), TN_DIMS, preferred_element_type=F32)

        @pl.when(m == nm - 1)
        def _():
            if col_shards:
                for s in range(spc):
                    out_ref[s] = acc_ref[:, s * ns:(s + 1) * ns].astype(BF16)
            else:
                out_ref[...] = acc_ref[...].astype(BF16)

    return _call(body, name=name, grid=grid, in_specs=in_specs, out_specs=out_spec, out_shape=out_shape,
                 scratch=[pltpu.VMEM(acc_shape, F32)], sem=("parallel", "arbitrary"))(*args)


CONV_TILE = 256
CONV_GROUP = 32


def _glu(u, ch):
    return u[:, :ch] * _sigmoid(u[:, ch:])


def _fill_glu(buf, u_ref, uh_ref, ch, tile):
    first = pl.program_id(0) == 0
    buf[0:HALO] = jnp.where(first, 0.0, _glu(uh_ref[...], ch))
    buf[HALO:HALO + tile] = _glu(u_ref[...], ch)


CONV_SUB = 4


def _conv_specs(S, ch, tile):
    per = tile // HALO
    u_spec = pl.BlockSpec((tile, 2 * ch, LANES), lambda i: (i, 0, 0))
    uh_spec = pl.BlockSpec((HALO, 2 * ch, LANES), lambda i: (jnp.maximum(i * per - 1, 0), 0, 0))
    x_spec = pl.BlockSpec((tile, ch, LANES), lambda i: (i, 0, 0))
    xn_spec = pl.BlockSpec((HALO, ch, LANES), lambda i: (jnp.minimum((i + 1) * per, S // HALO - 1), 0, 0))
    w_spec = pl.BlockSpec((CONV_WIDTH, ch, LANES), lambda i: (0, 0, 0))
    v_spec = pl.BlockSpec((1, ch, LANES), lambda i: (0, 0, 0))
    return u_spec, uh_spec, x_spec, xn_spec, w_spec, v_spec


def _conv_mid_fwd(u3, w3, bdw3, *, name):
    S, ch2, _ = u3.shape
    ch = ch2 // 2
    tile = _row_tile(S, CONV_TILE)
    sub = _row_tile(tile, CONV_SUB)
    u_spec, uh_spec, x_spec, _, w_spec, v_spec = _conv_specs(S, ch, tile)

    def body(u_ref, uh_ref, w_ref, b_ref, o_ref, buf):
        _fill_glu(buf, u_ref, uh_ref, ch, tile)

        def step(q, carry):
            acc = [b_ref[...], None]
            for k in range(CONV_WIDTH):
                term = buf[pl.ds(q * sub + (HALO - CONV_WIDTH + 1 + k), sub)] * w_ref[k]
                acc[k % 2] = term if acc[k % 2] is None else acc[k % 2] + term
            o_ref[pl.ds(q * sub, sub)] = acc[0] + acc[1]
            return carry

        lax.fori_loop(0, tile // sub, step, 0)

    return _call(body, name=name, grid=(S // tile,), in_specs=[u_spec, uh_spec, w_spec, v_spec], out_specs=x_spec,
                 out_shape=_sds((S, ch, LANES), F32), scratch=[pltpu.VMEM((tile + HALO, ch, LANES), F32)],
                 sem=("parallel",))(u3, u3, w3, bdw3)


def _ln_silu(v, gv, bv):
    mu = jnp.mean(v, axis=-1, keepdims=True)
    cen = v - mu
    rstd = lax.rsqrt(jnp.mean(cen * cen, axis=-1, keepdims=True) + EPS)
    nrm = cen * rstd
    ln = nrm * gv + bv
    sg = _sigmoid(ln)
    return ln * sg, (nrm, rstd, ln, sg)


def _ln_silu_bwd(dwo, ds, lng, lnb, *, name):
    S, D = dwo.shape
    tm = _row_tile(S, 512)

    def body(v_ref, ds_ref, g_ref, b_ref, ddw_ref, dg_ref, db_ref, dbdw_ref):
        @pl.when(pl.program_id(0) == 0)
        def _():
            dg_ref[...] = jnp.zeros_like(dg_ref)
            db_ref[...] = jnp.zeros_like(db_ref)
            dbdw_ref[...] = jnp.zeros_like(dbdw_ref)

        gv = g_ref[...]
        _, (nrm, rstd, ln, sg) = _ln_silu(v_ref[...], gv, b_ref[...])
        dln = ds_ref[...] * (sg * (1.0 + ln * (1.0 - sg)))
        dg_ref[...] += jnp.sum(dln * nrm, axis=0, keepdims=True)
        db_ref[...] += jnp.sum(dln, axis=0, keepdims=True)
        dn = dln * gv
        ddw = rstd * (dn - jnp.mean(dn, axis=-1, keepdims=True) - nrm * jnp.mean(dn * nrm, axis=-1, keepdims=True))
        dbdw_ref[...] += jnp.sum(ddw, axis=0, keepdims=True)
        ddw_ref[...] = ddw

    row = pl.BlockSpec((tm, D), lambda i: (i, 0))
    vec = pl.BlockSpec((1, D), lambda i: (0, 0))
    vsh = _sds((1, D), F32)
    return _call(body, name=name, grid=(S // tm,), in_specs=[row, row, vec, vec], out_specs=(row, vec, vec, vec),
                 out_shape=(_sds((S, D), F32), vsh, vsh, vsh), sem=("arbitrary",))(dwo, ds, lng, lnb)


def _conv_mid_bwd_dw(u3, ddw3, w3, *, name):
    S, ch2, _ = u3.shape
    ch = ch2 // 2
    tile = _row_tile(S, CONV_TILE)
    rows = _row_tile(tile, CONV_GROUP)
    sub = _row_tile(rows, CONV_SUB)
    last = S // tile - 1
    u_spec, uh_spec, x_spec, xn_spec, w_spec, _ = _conv_specs(S, ch, tile)
    b_spec = pl.BlockSpec((1, 2 * ch, LANES), lambda i: (0, 0, 0))

    def body(u_ref, uh_ref, d_ref, dn_ref, w_ref, du_ref, dw_ref, db_ref, gbuf, dbuf, stage):
        @pl.when(pl.program_id(0) == 0)
        def _():
            dw_ref[...] = jnp.zeros_like(dw_ref)
            db_ref[...] = jnp.zeros_like(db_ref)

        _fill_glu(gbuf, u_ref, uh_ref, ch, tile)
        dbuf[0:tile] = d_ref[...]
        dbuf[tile:tile + HALO] = jnp.where(pl.program_id(0) == last, 0.0, dn_ref[...])

        def group(r, carry):
            t0 = r * rows
            def step(q, c):
                s0 = t0 + q * sub
                ddw = dbuf[pl.ds(s0, sub)]
                acc = [None, None]
                for k in range(CONV_WIDTH):
                    term = dbuf[pl.ds(s0 + (CONV_WIDTH - 1 - k), sub)] * w_ref[k]
                    acc[k % 2] = term if acc[k % 2] is None else acc[k % 2] + term
                    dw_ref[k] += jnp.sum(ddw * gbuf[pl.ds(s0 + (HALO - CONV_WIDTH + 1 + k), sub)], axis=0)
                stage[pl.ds(q * sub, sub)] = acc[0] + acc[1]
                return c

            lax.fori_loop(0, rows // sub, step, 0)
            dglu = stage[...]
            uv = u_ref[pl.ds(t0, rows)]
            av, sg = uv[:, :ch], _sigmoid(uv[:, ch:])
            da = dglu * sg
            dg = dglu * av * sg * (1.0 - sg)
            du_ref[pl.ds(t0, rows), 0:ch] = da
            du_ref[pl.ds(t0, rows), ch:2 * ch] = dg
            db_ref[:, 0:ch] += jnp.sum(da, axis=0, keepdims=True)
            db_ref[:, ch:2 * ch] += jnp.sum(dg, axis=0, keepdims=True)
            return carry

        lax.fori_loop(0, tile // rows, group, 0)

    return _call(body, name=name, grid=(S // tile,), in_specs=[u_spec, uh_spec, x_spec, xn_spec, w_spec],
                 out_specs=(u_spec, w_spec, b_spec),
                 out_shape=(_sds((S, 2 * ch, LANES), F32), _sds((CONV_WIDTH, ch, LANES), F32), _sds((1, 2 * ch, LANES), F32)),
                 scratch=[pltpu.VMEM((tile + HALO, ch, LANES), F32), pltpu.VMEM((tile + HALO, ch, LANES), F32),
                          pltpu.VMEM((rows, ch, LANES), F32)],
                 sem=("arbitrary",))(u3, u3, ddw3, ddw3, w3)


def _ret_tables(S, dk):
    B = min(RET_BLOCK, S)
    lg = jnp.log(1.0 - 2.0 ** (-5.0 - jnp.arange(RET_HEADS, dtype=F32)))
    idx = jnp.arange(B, dtype=F32)
    diff = idx[:, None] - idx[None, :]
    cq, ck = (jnp.arange(B) // CHUNK)[:, None], (jnp.arange(B) // CHUNK)[None, :]
    dist = jnp.where(cq == ck, jnp.abs(diff), diff)
    mask = jnp.where(ck <= cq, jnp.exp(lg[:, None, None] * dist[None]), 0.0)
    xi = jnp.exp(lg[:, None] * (idx + 1.0))[..., None]
    zeta = jnp.exp(lg[:, None] * (B - 1.0 - idx))[..., None]
    gam = jnp.broadcast_to(jnp.exp(lg * B)[:, None, None], (RET_HEADS, 8, LANES))
    pos = jnp.arange(S, dtype=F32)
    inv = ROPE_BASE ** (-jnp.arange(0, dk, 2, dtype=F32) / dk)
    ang = pos[:, None] * inv[None, :]
    return dict(B=B, mask=mask, xi=xi, zeta=zeta, gam=gam, cos=jnp.cos(ang), sin=jnp.sin(ang))


def _rope(v, cs, sn):
    half = v.shape[1] // 2
    v1, v2 = v[:, :half], v[:, half:]
    return jnp.concatenate([v1 * cs - v2 * sn, v2 * cs + v1 * sn], axis=-1)


def _rope_t(d, cs, sn):
    half = d.shape[1] // 2
    d1, d2 = d[:, :half], d[:, half:]
    return jnp.concatenate([d1 * cs + d2 * sn, d2 * cs - d1 * sn], axis=-1)


def _dot(a, b):
    return jnp.dot(a.astype(BF16), b.astype(BF16), preferred_element_type=F32)


def _dot_nt(a, b):
    return lax.dot_general(a.astype(BF16), b.astype(BF16), NT_DIMS, preferred_element_type=F32)


def _dot_tn(a, b):
    return lax.dot_general(a.astype(BF16), b.astype(BF16), TN_DIMS, preferred_element_type=F32)


def _ret_specs(S, D, B, RB, reverse):
    dk, dv = D // RET_HEADS, 2 * D // RET_HEADS
    nb = S // RB
    blk = (lambda ib: nb - 1 - ib) if reverse else (lambda ib: ib)
    q = pl.BlockSpec((RB, dk), lambda h, ib: (blk(ib), h))
    k = pl.BlockSpec((RB, dk), lambda h, ib: (blk(ib), RET_HEADS + h))
    v = pl.BlockSpec((RB, dv), lambda h, ib: (blk(ib), RET_HEADS + h))
    gate = pl.BlockSpec((RB, dv), lambda h, ib: (blk(ib), 2 * RET_HEADS + h))
    yv = pl.BlockSpec((RB, dv), lambda h, ib: (blk(ib), h))
    rope = pl.BlockSpec((RB, dk // 2), lambda h, ib: (blk(ib), 0))
    mask = pl.BlockSpec((None, B, B), lambda h, ib: (h, 0, 0))
    dec = pl.BlockSpec((None, B, 1), lambda h, ib: (h, 0, 0))
    gam = pl.BlockSpec((None, 8, LANES), lambda h, ib: (h, 0, 0))
    gn = pl.BlockSpec((1, dv), lambda h, ib: (0, h))
    return dict(q=q, k=k, v=v, gate=gate, yv=yv, rope=rope, mask=mask, dec=dec, gam=gam, gn=gn)


def _group_norm(yr, gv, bv):
    mu = jnp.mean(yr, axis=-1, keepdims=True)
    cen = yr - mu
    rstd = lax.rsqrt(jnp.mean(cen * cen, axis=-1, keepdims=True) + EPS)
    nrm = cen * rstd
    return nrm, rstd, nrm * gv + bv


def _ret_fwd(proj, tb, gng, gnb, *, name):
    S, D = proj.shape[0], proj.shape[1] // 6
    dk, dv = D // RET_HEADS, 2 * D // RET_HEADS
    B = tb["B"]
    RB = _row_tile(S, 2 * B)
    nsub = RB // B
    sp = _ret_specs(S, D, B, RB, False)
    scale = dk ** -0.5

    def body(q_ref, k_ref, v_ref, gt_ref, cos_ref, sin_ref, mask_ref, xi_ref, zeta_ref, gam_ref, gng_ref, gnb_ref,
             yr_ref, yg_ref, state):
        @pl.when(pl.program_id(1) == 0)
        def _():
            state[...] = jnp.zeros_like(state)

        for sb in range(nsub):
            rows = slice(sb * B, (sb + 1) * B)
            cs, sn = cos_ref[rows, :], sin_ref[rows, :]
            q = _rope(q_ref[rows, :].astype(F32), cs, sn)
            k = _rope(k_ref[rows, :].astype(F32), cs, sn) * scale
            vb = v_ref[rows, :]
            p = _dot_nt(q, k) * mask_ref[...]
            st = state[...]
            yr = _dot(p, vb) + _dot(q * xi_ref[...], st)
            state[...] = st * gam_ref[0:1, 0:1] + _dot_tn(k * zeta_ref[...], vb)
            _, _, gn = _group_norm(yr, gng_ref[...], gnb_ref[...])
            gt = gt_ref[rows, :].astype(F32)
            yr_ref[rows, :] = yr.astype(BF16)
            yg_ref[rows, :] = (gt * _sigmoid(gt) * gn).astype(BF16)

    return _call(body, name=name, grid=(RET_HEADS, S // RB),
                 in_specs=[sp["q"], sp["k"], sp["v"], sp["gate"], sp["rope"], sp["rope"], sp["mask"], sp["dec"], sp["dec"],
                           sp["gam"], sp["gn"], sp["gn"]],
                 out_specs=(sp["yv"], sp["yv"]), out_shape=(_sds((S, 2 * D), BF16), _sds((S, 2 * D), BF16)),
                 scratch=[pltpu.VMEM((dk, dv), F32)], sem=("parallel", "arbitrary"))(
                     proj, proj, proj, proj, tb["cos"], tb["sin"], tb["mask"], tb["xi"], tb["zeta"], tb["gam"], gng, gnb)


def _ret_bwd_q(proj, yr, dyg, tb, gng, gnb, *, name):
    S, D = proj.shape[0], proj.shape[1] // 6
    dk, dv = D // RET_HEADS, 2 * D // RET_HEADS
    B = tb["B"]
    RB = _row_tile(S, 2 * B)
    nsub = RB // B
    sp = _ret_specs(S, D, B, RB, False)
    scale = dk ** -0.5

    def body(q_ref, k_ref, v_ref, gt_ref, yr_ref, dyg_ref, cos_ref, sin_ref, mask_ref, xi_ref, zeta_ref, gam_ref,
             gng_ref, gnb_ref, dq_ref, dgt_ref, dyr_ref, dgg_ref, dgb_ref, state):
        @pl.when(pl.program_id(1) == 0)
        def _():
            state[...] = jnp.zeros_like(state)
            dgg_ref[...] = jnp.zeros_like(dgg_ref)
            dgb_ref[...] = jnp.zeros_like(dgb_ref)

        for sb in range(nsub):
            rows = slice(sb * B, (sb + 1) * B)
            cs, sn = cos_ref[rows, :], sin_ref[rows, :]
            q = _rope(q_ref[rows, :].astype(F32), cs, sn)
            k = _rope(k_ref[rows, :].astype(F32), cs, sn) * scale
            vb = v_ref[rows, :]
            gv = gng_ref[...]
            nrm, rstd, gn = _group_norm(yr_ref[rows, :].astype(F32), gv, gnb_ref[...])
            gt = gt_ref[rows, :].astype(F32)
            sg = _sigmoid(gt)
            dyg = dyg_ref[rows, :].astype(F32)
            dgt_ref[rows, :] = (dyg * gn * (sg * (1.0 + gt * (1.0 - sg)))).astype(BF16)
            dgn = dyg * (gt * sg)
            dgg_ref[...] += jnp.sum(dgn * nrm, axis=0, keepdims=True)
            dgb_ref[...] += jnp.sum(dgn, axis=0, keepdims=True)
            dn = dgn * gv
            dyr = rstd * (dn - jnp.mean(dn, axis=-1, keepdims=True) - nrm * jnp.mean(dn * nrm, axis=-1, keepdims=True))
            dyr_ref[rows, :] = dyr.astype(BF16)
            dp = _dot_nt(dyr, vb) * mask_ref[...]
            st = state[...]
            dq = _dot(dp, k) + _dot_nt(dyr, st) * xi_ref[...]
            dq_ref[rows, :] = _rope_t(dq, cs, sn).astype(BF16)
            state[...] = st * gam_ref[0:1, 0:1] + _dot_tn(k * zeta_ref[...], vb)

    qout = pl.BlockSpec((RB, dk), lambda h, ib: (ib, h))
    return _call(body, name=name, grid=(RET_HEADS, S // RB),
                 in_specs=[sp["q"], sp["k"], sp["v"], sp["gate"], sp["yv"], sp["yv"], sp["rope"], sp["rope"], sp["mask"],
                           sp["dec"], sp["dec"], sp["gam"], sp["gn"], sp["gn"]],
                 out_specs=(qout, sp["yv"], sp["yv"], sp["gn"], sp["gn"]),
                 out_shape=(_sds((S, D), BF16), _sds((S, 2 * D), BF16), _sds((S, 2 * D), BF16), _sds((1, 2 * D), F32),
                            _sds((1, 2 * D), F32)),
                 scratch=[pltpu.VMEM((dk, dv), F32)], sem=("parallel", "arbitrary"))(
                     proj, proj, proj, proj, yr, dyg, tb["cos"], tb["sin"], tb["mask"], tb["xi"], tb["zeta"], tb["gam"],
                     gng, gnb)


def _ret_bwd_kv(proj, dyr, dq, dgt, tb, *, name):
    S, D = proj.shape[0], proj.shape[1] // 6
    dk, dv = D // RET_HEADS, 2 * D // RET_HEADS
    B = tb["B"]
    RB = _row_tile(S, 2 * B)
    nsub = RB // B
    nb = S // RB
    scale = dk ** -0.5

    def body(p_ref, dyr_ref, dq_ref, dgt_ref, cos_ref, sin_ref, mask_ref, xi_ref, zeta_ref, gam_ref, out_ref, dstate):
        @pl.when(pl.program_id(0) == 0)
        def _():
            dstate[...] = jnp.zeros_like(dstate)

        out_ref[:, 0:D] = dq_ref[...]
        out_ref[:, 4 * D:6 * D] = dgt_ref[...]
        for sb in reversed(range(nsub)):
            rows = slice(sb * B, (sb + 1) * B)
            cs, sn = cos_ref[rows, :], sin_ref[rows, :]
            for h in range(RET_HEADS):
                kcols = slice(D + h * dk, D + (h + 1) * dk)
                vcols = slice(2 * D + h * dv, 2 * D + (h + 1) * dv)
                q = _rope(p_ref[rows, h * dk:(h + 1) * dk].astype(F32), cs, sn)
                k = _rope(p_ref[rows, kcols].astype(F32), cs, sn) * scale
                vb = p_ref[rows, vcols]
                dyr_h = dyr_ref[rows, h * dv:(h + 1) * dv]
                mk = mask_ref[h]
                p = _dot_nt(q, k) * mk
                dp = _dot_nt(dyr_h, vb) * mk
                ds = dstate[h]
                zt = zeta_ref[h]
                dkr = _dot_tn(dp, q) + _dot_nt(vb, ds) * zt
                out_ref[rows, kcols] = _rope_t(dkr * scale, cs, sn).astype(BF16)
                out_ref[rows, vcols] = (_dot_tn(p, dyr_h) + _dot(k * zt, ds)).astype(BF16)
                dstate[h] = ds * gam_ref[h, 0:1, 0:1] + _dot_tn(q * xi_ref[h], dyr_h)

    def rev(width):
        return pl.BlockSpec((RB, width), lambda ib: (nb - 1 - ib, 0))

    def whole(a):
        return pl.BlockSpec(a.shape, lambda ib: (0,) * a.ndim)

    return _call(body, name=name, grid=(nb,),
                 in_specs=[rev(6 * D), rev(2 * D), rev(D), rev(2 * D), rev(dk // 2), rev(dk // 2), whole(tb["mask"]),
                           whole(tb["xi"]), whole(tb["zeta"]), whole(tb["gam"])],
                 out_specs=rev(6 * D), out_shape=_sds((S, 6 * D), BF16), scratch=[pltpu.VMEM((RET_HEADS, dk, dv), F32)],
                 sem=("arbitrary",))(proj, dyr, dq, dgt, tb["cos"], tb["sin"], tb["mask"], tb["xi"], tb["zeta"], tb["gam"])


def _ada_fwd(c_all, ada_w, *, name):
    L, D, ns = ada_w.shape

    def body(c_ref, w_ref, out_ref):
        cv = c_ref[...]
        cond = cv * _sigmoid(cv)
        out_ref[...] = jnp.dot(cond.astype(BF16), w_ref[...].astype(BF16), preferred_element_type=F32)

    return _call(body, name=name, grid=(L,), in_specs=[pl.BlockSpec((NDEV, D), lambda l: (0, 0)),
                                                      pl.BlockSpec((None, D, ns), lambda l: (l, 0, 0))],
                 out_specs=pl.BlockSpec((None, NDEV, ns), lambda l: (l, 0, 0)), out_shape=_sds((L, NDEV, ns), F32),
                 sem=("parallel",))(c_all, ada_w)


def _ada_bwd(c_all, dmod_cols, *, name):
    L, _, ns = dmod_cols.shape
    D = c_all.shape[1]

    def body(c_ref, d_ref, out_ref):
        cv = c_ref[...]
        cond = cv * _sigmoid(cv)
        out_ref[...] = lax.dot_general(cond.astype(BF16), d_ref[...].astype(BF16), TN_DIMS, preferred_element_type=F32)

    return _call(body, name=name, grid=(L,), in_specs=[pl.BlockSpec((NDEV, D), lambda l: (0, 0)),
                                                      pl.BlockSpec((None, NDEV, ns), lambda l: (l, 0, 0))],
                 out_specs=pl.BlockSpec((None, D, ns), lambda l: (l, 0, 0)), out_shape=_sds((L, D, ns), F32),
                 sem=("parallel",))(c_all, dmod_cols)


def _adamw(w, m, v, parts, *, name):
    shape = w.shape
    L, cols = len(parts), shape[-1]
    rows = w.size // (cols * L)
    n = parts[0].shape[0]
    tr = rows
    for cand in (256, 128, 64, 32, 16, 8):
        if rows % cand == 0:
            tr = cand
            break
    c1 = 1.0 - ADAM_B1 ** ADAM_STEP
    c2 = 1.0 - ADAM_B2 ** ADAM_STEP

    def body(w_ref, m_ref, v_ref, *rest):
        p_refs = rest[:L]
        g_ref, d_ref, m2_ref, v2_ref = rest[L:]
        layer = pl.program_id(0)
        for l in range(L):
            @pl.when(layer == l)
            def _(p_ref=p_refs[l]):
                g = p_ref[0].astype(F32)
                for i in range(1, n):
                    g = g + p_ref[i].astype(F32)
                m2 = ADAM_B1 * m_ref[...] + (1.0 - ADAM_B1) * g
                v2 = ADAM_B2 * v_ref[...] + (1.0 - ADAM_B2) * (g * g)
                g_ref[...] = g
                m2_ref[...] = m2
                v2_ref[...] = v2
                d_ref[...] = -ADAM_LR * ((m2 / c1) / (jnp.sqrt(v2 / c2) + ADAM_EPS) + ADAM_WD * w_ref[...])

    mat = pl.BlockSpec((None, tr, cols), lambda l, i: (l, i, 0))

    def part_spec(k):
        return pl.BlockSpec((n, tr, cols), lambda l, i: (0, jnp.where(l == k, i, 0), 0))

    outs = _call(body, name=name, grid=(L, rows // tr), in_specs=[mat, mat, mat] + [part_spec(k) for k in range(L)],
                 out_specs=(mat, mat, mat, mat), out_shape=tuple(_sds((L, rows, cols), F32) for _ in range(4)),
                 sem=("parallel", "parallel"))(w.reshape(L, rows, cols), m.reshape(L, rows, cols), v.reshape(L, rows, cols),
                                               *[p.reshape(n, rows, cols) for p in parts])
    return tuple(o.reshape(shape) for o in outs)


SMALL = ("ada_b", "norm_mix_g", "norm_mlp_g", "conv_b_pw1", "conv_b_dw", "conv_ln_g", "conv_ln_b", "conv_b_pw2",
         "final_norm_g")
WEIGHTS = ("ada_w", "ada_b", "norm_mix_g", "norm_mlp_g", "conv_w_pw1", "conv_b_pw1", "conv_w_dw", "conv_b_dw", "conv_ln_g",
           "conv_ln_b", "conv_w_pw2", "conv_b_pw2", "ret_w_in", "ret_gn_g", "ret_gn_b", "ret_w_out", "mlp_w1", "mlp_w2",
           "final_norm_g")


def kernel(x, c, ada_w, ada_b, norm_mix_g, norm_mlp_g, conv_w_pw1, conv_b_pw1, conv_w_dw, conv_b_dw, conv_ln_g, conv_ln_b, conv_w_pw2, conv_b_pw2, ret_w_in, ret_gn_g, ret_gn_b, ret_w_out, mlp_w1, mlp_w2, final_norm_g, loss_target, m_ada_w, m_ada_b, m_norm_mix_g, m_norm_mlp_g, m_conv_w_pw1, m_conv_b_pw1, m_conv_w_dw, m_conv_b_dw, m_conv_ln_g, m_conv_ln_b, m_conv_w_pw2, m_conv_b_pw2, m_ret_w_in, m_ret_gn_g, m_ret_gn_b, m_ret_w_out, m_mlp_w1, m_mlp_w2, m_final_norm_g, v_ada_w, v_ada_b, v_norm_mix_g, v_norm_mlp_g, v_conv_w_pw1, v_conv_b_pw1, v_conv_w_dw, v_conv_b_dw, v_conv_ln_g, v_conv_ln_b, v_conv_w_pw2, v_conv_b_pw2, v_ret_w_in, v_ret_gn_g, v_ret_gn_b, v_ret_w_out, v_mlp_w1, v_mlp_w2, v_final_norm_g):
    W = dict(ada_w=ada_w, ada_b=ada_b, norm_mix_g=norm_mix_g, norm_mlp_g=norm_mlp_g, conv_w_pw1=conv_w_pw1,
             conv_b_pw1=conv_b_pw1, conv_w_dw=conv_w_dw, conv_b_dw=conv_b_dw, conv_ln_g=conv_ln_g, conv_ln_b=conv_ln_b,
             conv_w_pw2=conv_w_pw2, conv_b_pw2=conv_b_pw2, ret_w_in=ret_w_in, ret_gn_g=ret_gn_g, ret_gn_b=ret_gn_b,
             ret_w_out=ret_w_out, mlp_w1=mlp_w1, mlp_w2=mlp_w2, final_norm_g=final_norm_g)
    Mo = dict(ada_w=m_ada_w, ada_b=m_ada_b, norm_mix_g=m_norm_mix_g, norm_mlp_g=m_norm_mlp_g, conv_w_pw1=m_conv_w_pw1,
              conv_b_pw1=m_conv_b_pw1, conv_w_dw=m_conv_w_dw, conv_b_dw=m_conv_b_dw, conv_ln_g=m_conv_ln_g,
              conv_ln_b=m_conv_ln_b, conv_w_pw2=m_conv_w_pw2, conv_b_pw2=m_conv_b_pw2, ret_w_in=m_ret_w_in,
              ret_gn_g=m_ret_gn_g, ret_gn_b=m_ret_gn_b, ret_w_out=m_ret_w_out, mlp_w1=m_mlp_w1, mlp_w2=m_mlp_w2,
              final_norm_g=m_final_norm_g)
    Vo = dict(ada_w=v_ada_w, ada_b=v_ada_b, norm_mix_g=v_norm_mix_g, norm_mlp_g=v_norm_mlp_g, conv_w_pw1=v_conv_w_pw1,
              conv_b_pw1=v_conv_b_pw1, conv_w_dw=v_conv_w_dw, conv_b_dw=v_conv_b_dw, conv_ln_g=v_conv_ln_g,
              conv_ln_b=v_conv_ln_b, conv_w_pw2=v_conv_w_pw2, conv_b_pw2=v_conv_b_pw2, ret_w_in=v_ret_w_in,
              ret_gn_g=v_ret_gn_g, ret_gn_b=v_ret_gn_b, ret_w_out=v_ret_w_out, mlp_w1=v_mlp_w1, mlp_w2=v_mlp_w2,
              final_norm_g=v_final_norm_g)

    S, D = x.shape[1], x.shape[2]
    CH = D // LANES
    n_conv, n_ret = conv_w_pw1.shape[0], ret_w_in.shape[0]
    me = 4 * lax.axis_index("x") + 2 * lax.axis_index("y") + lax.axis_index("c")
    xs = x.reshape(S, D)
    target = loss_target.reshape(S, D)

    def mixer_shards(i):
        j = i // 2
        if i % 2 == 0:
            return [[conv_w_pw1[j].astype(BF16)], [conv_w_pw2[j].astype(BF16)]]
        return [[ret_w_in[j].astype(BF16)], [ret_w_out[j].astype(BF16)]]

    def mlp_shards(i):
        return [[mlp_w1[i].astype(BF16)], [mlp_w2[i].astype(BF16)]]

    def mixer_weights(i, got):
        return got[0], got[1].reshape(-1, D)

    def mlp_weights(got):
        return got[0], got[1].reshape(4 * D, D)

    first_handle, _ = _exchange_start(mixer_shards(0)[:1], gather=True, name="gather_start_first")
    small = _exchange([[conv_w_dw], [ret_gn_g], [ret_gn_b], [c]], gather=True, name="gather_small")
    dw_g, gng_g, gnb_g, c_g = small
    dw3 = jnp.transpose(dw_g, (1, 2, 0, 3)).reshape(n_conv, CONV_WIDTH, CH, LANES)
    gng_full = jnp.transpose(gng_g, (1, 2, 0, 3)).reshape(n_ret, 1, 2 * D)
    gnb_full = jnp.transpose(gnb_g, (1, 2, 0, 3)).reshape(n_ret, 1, 2 * D)
    c_all = c_g.reshape(NDEV, D)

    mod_cols = _ada_fwd(c_all, ada_w, name="ada_fwd")
    mod_all = _exchange([[mod_cols]], gather=True, name="gather_mod")[0]
    mod = lax.dynamic_index_in_dim(mod_all, me, axis=2, keepdims=False)
    mod = jnp.transpose(mod, (1, 0, 2)).reshape(DEPTH, 6 * D) + ada_b
    mods = [[mod[i, j * D:(j + 1) * D].reshape(1, D) for j in range(6)] for i in range(DEPTH)]
    tb = _ret_tables(S, D // RET_HEADS)

    def vec(a):
        return a.reshape(1, -1)

    mix_w = (_exchange_wait(first_handle, name="gather_wait_first", after=mod)[0], None)
    handle, token = _exchange_start(mixer_shards(0)[1:] + mlp_shards(0), gather=True, name="gather_start_rest0",
                                    after=mix_w[0])
    saved = []
    weights = []
    xcur = xs
    for i in range(DEPTH):
        sh1, sc1, g1, sh2, sc2, g2 = mods[i]
        j = i // 2
        if i > 0:
            mix_w = mixer_weights(i, _exchange_wait(handle, name=f"gather_wait_mix{i}", after=xcur))
            handle, token = _exchange_start(mlp_shards(i), gather=True, name=f"gather_start_mlp{i}", after=mix_w[0])
        st = dict(x_in=xcur)
        norm1 = (vec(norm_mix_g[i]), sc1, sh1)
        if i % 2 == 0:
            u, h = _mm_nn(xcur, mix_w[0], norm=norm1, bias=vec(conv_b_pw1[j]), out_dtype=F32, name=f"pw1_fwd{i}", after=token)
            u3 = u.reshape(S, 2 * CH, LANES)
            dwo = _conv_mid_fwd(u3, dw3[j], conv_b_dw[j].reshape(1, CH, LANES), name=f"conv_mid_fwd{i}").reshape(S, D)
            if i == 0:
                got = _exchange_wait(handle, name="gather_wait_rest0", after=dwo)
                mix_w, mlp_w = (mix_w[0], got[0].reshape(-1, D)), mlp_weights(got[1:3])
                handle, token = _exchange_start(mixer_shards(1), gather=True, name="gather_start_mix1", after=got[0])
            xcur, y_raw = _mm_nn(dwo, mix_w[1], ln=(vec(conv_ln_g[j]), vec(conv_ln_b[j])), bias=vec(conv_b_pw2[j]), res=xcur,
                                 gate=g1, name=f"pw2_fwd{i}")
            st.update(u3=u3, dwo=dwo, y_raw=y_raw)
        else:
            proj, h = _mm_nn(xcur, mix_w[0], norm=norm1, name=f"ret_in_fwd{i}", after=token)
            yr, yg = _ret_fwd(proj, tb, gng_full[j], gnb_full[j], name=f"ret_fwd{i}")
            xcur, y_raw = _mm_nn(yg, mix_w[1], res=xcur, gate=g1, name=f"ret_out_fwd{i}")
            st.update(proj=proj, yr=yr, yg=yg, y_raw=y_raw)
        st.update(h=h, x_mid=xcur)
        if i > 0:
            mlp_w = mlp_weights(_exchange_wait(handle, name=f"gather_wait_mlp{i}", after=xcur))
            if i + 1 < DEPTH:
                handle, token = _exchange_start(mixer_shards(i + 1), gather=True, name=f"gather_start_mix{i + 1}",
                                                after=mlp_w[0])
        z, h2 = _mm_nn(xcur, mlp_w[0], norm=(vec(norm_mlp_g[i]), sc2, sh2), name=f"mlp1_fwd{i}", after=token)
        xcur, o_raw = _mm_nn(z, mlp_w[1], relu2=True, res=xcur, gate=g2, name=f"mlp2_fwd{i}")
        st.update(h2=h2, z=z, o_raw=o_raw)
        saved.append(st)
        weights.append(mix_w + mlp_w)

    g2_last = mods[DEPTH - 1][5]
    loss_local, dx, d_final_g, dy, dgate, _ = _final_loss(xcur, vec(final_norm_g), target, saved[-1]["o_raw"], g2_last,
                                                          name="final_loss")
    loss = lax.psum(loss_local[0, 0], AXES)

    dmod_rows = [None] * DEPTH
    d_mix_g, d_mlp_g = [None] * DEPTH, [None] * DEPTH
    d_pw1, d_pw2, d_win, d_wout = [None] * n_conv, [None] * n_conv, [None] * n_ret, [None] * n_ret
    d_w1, d_w2 = [None] * DEPTH, [None] * DEPTH
    d_bpw1, d_bdw, d_lng, d_lnb, d_bpw2, d_dw = ([None] * n_conv for _ in range(6))
    d_gng, d_gnb = [None] * n_ret, [None] * n_ret

    def gn_parts(d):
        return jnp.transpose(d.reshape(RET_HEADS, NDEV, -1), (1, 0, 2))

    grad_handles = [None] * DEPTH
    token = None
    for i in reversed(range(DEPTH)):
        sh1, sc1, g1, sh2, sc2, g2 = mods[i]
        j = i // 2
        st = saved[i]
        mix_a, mix_b, w1_i, w2_i = weights[i]
        do, dg2 = dy, dgate
        dz = _mm_nt(do, w2_i, z=st["z"], out_dtype=BF16, name=f"mlp2_bwd_x{i}", after=token)
        d_w2[i] = _mm_tn(st["z"], do, relu2=True, name=f"mlp2_bwd_w{i}")
        dh2 = _mm_nt(dz, w1_i, name=f"mlp1_bwd_x{i}")
        d_w1[i] = _mm_tn(st["h2"], dz, col_shards=NDEV, name=f"mlp1_bwd_w{i}")
        mlp_groups = [[d_w1[i]], [d_w2[i].reshape(NDEV, 4 * D // NDEV, D)]]
        dx, dsc2, dsh2, d_mlp_g[i], dy, dg1, dby = _norm_mod_bwd(st["x_mid"], vec(norm_mlp_g[i]), sc2, dh2, dx, st["y_raw"],
                                                                 g1, name=f"norm_mlp_bwd{i}")
        token = None
        if i == 0:
            mlp0_handle, token = _exchange_start(mlp_groups, gather=False, name="grads_start_mlp0")
            mlp_groups = []
        if i % 2 == 0:
            d_bpw2[j] = dby
            ds = _mm_nt(dy, mix_b, name=f"pw2_bwd_x{i}", after=token)
            ln_gb = (vec(conv_ln_g[j]), vec(conv_ln_b[j]))
            d_pw2[j] = _mm_tn(st["dwo"], dy, ln=ln_gb, name=f"pw2_bwd_w{i}")
            ddw, d_lng[j], d_lnb[j], d_bdw[j] = _ln_silu_bwd(st["dwo"], ds, *ln_gb, name=f"conv_ln_bwd{i}")
            du3, ddw_w, dbu = _conv_mid_bwd_dw(st["u3"], ddw.reshape(S, CH, LANES), dw3[j], name=f"conv_mid_bwd_dw{i}")
            d_dw[j], d_bpw1[j] = ddw_w.reshape(CONV_WIDTH, D), dbu.reshape(2, D)
            du = du3.reshape(S, 2 * D)
            mix_groups = [[d_pw2[j].reshape(NDEV, D // NDEV, D)],
                          [jnp.transpose(d_dw[j].reshape(CONV_WIDTH, NDEV, D // NDEV), (1, 0, 2))]]
            token = None
            if i == 0:
                conv0_handle, token = _exchange_start(mix_groups, gather=False, name="grads_start_conv0")
                mix_groups = []
            dh = _mm_nt(du, mix_a, name=f"pw1_bwd_x{i}", after=token)
            d_pw1[j] = _mm_tn(st["h"], du, col_shards=NDEV, name=f"pw1_bwd_w{i}")
            mix_groups = [[d_pw1[j]]] + mix_groups
        else:
            dyg = _mm_nt(dy, mix_b, out_dtype=BF16, name=f"ret_out_bwd_x{i}")
            d_wout[j] = _mm_tn(st["yg"], dy, name=f"ret_out_bwd_w{i}")
            dq, dgt, dyr, d_gng[j], d_gnb[j] = _ret_bwd_q(st["proj"], st["yr"], dyg, tb, gng_full[j], gnb_full[j],
                                                          name=f"ret_bwd_q{i}")
            dproj = _ret_bwd_kv(st["proj"], dyr, dq, dgt, tb, name=f"ret_bwd_kv{i}")
            dh = _mm_nt(dproj, mix_a, name=f"ret_in_bwd_x{i}")
            d_win[j] = _mm_tn(st["h"], dproj, col_shards=NDEV, name=f"ret_in_bwd_w{i}")
            mix_groups = [[d_win[j]], [d_wout[j].reshape(NDEV, 2 * D // NDEV, D)], [gn_parts(d_gng[j])],
                          [gn_parts(d_gnb[j])]]
        grad_handles[i], token = _exchange_start(mix_groups + mlp_groups, gather=False, name=f"grads_start{i}")
        y_prev, gate_prev = (saved[i - 1]["o_raw"], mods[i - 1][5]) if i > 0 else (None, None)
        outs = _norm_mod_bwd(st["x_in"], vec(norm_mix_g[i]), sc1, dh, dx, y_prev, gate_prev, name=f"norm_mix_bwd{i}")
        dx, dsc1, dsh1, d_mix_g[i] = outs[:4]
        if i > 0:
            dy, dgate = outs[4], outs[5]
        dmod_rows[i] = jnp.concatenate([dsh1, dsc1, dg1, dsh2, dsc2, dg2], axis=0)
    grad_x = dx.reshape(1, S, D)

    small_local = jnp.concatenate(dmod_rows + d_mix_g + d_mlp_g + d_bpw1 + d_bdw + d_lng + d_lnb + d_bpw2 + [d_final_g],
                                  axis=0)
    small_all = _exchange([[small_local]], gather=True, name="gather_small_grads")[0]

    def pack(src):
        return jnp.concatenate([src[n].reshape(-1, D) for n in SMALL], axis=0)[None]

    sm = _adamw(pack(W), pack(Mo), pack(Vo), [small_all], name="adamw_small")
    results = {}
    row = 0
    for n in SMALL:
        cnt = W[n].size // D
        results[n] = tuple(o[0, row:row + cnt].reshape(W[n].shape) for o in sm)
        row += cnt

    ns_ada = ada_w.shape[2]
    dmod_all = small_all[:, :6 * DEPTH, :].reshape(NDEV, DEPTH, 6 * D)
    dmod_cols = jnp.transpose(lax.dynamic_slice_in_dim(dmod_all, me * ns_ada, ns_ada, axis=2), (1, 0, 2))
    g_ada = _ada_bwd(c_all, dmod_cols, name="ada_bwd")
    flat_ada = (1, DEPTH * D, ns_ada)
    ada_res = _adamw(ada_w.reshape(flat_ada), m_ada_w.reshape(flat_ada), v_ada_w.reshape(flat_ada),
                     [g_ada.reshape(flat_ada)], name="adamw_ada_w")
    results["ada_w"] = tuple(o.reshape(ada_w.shape) for o in ada_res)

    def update(names, parts):
        for n in names:
            results[n] = _adamw(W[n], Mo[n], Vo[n], parts[n], name=f"adamw_{n}")

    got = {i: _exchange_wait(grad_handles[i], name=f"grads_wait{i}", after=dx) for i in range(DEPTH - 1, 0, -1)}
    ret_layers = [i for i in range(DEPTH) if i % 2 == 1]
    update(("ret_w_in", "ret_w_out", "ret_gn_g", "ret_gn_b"),
           dict(ret_w_in=[got[i][0] for i in ret_layers], ret_w_out=[got[i][1] for i in ret_layers],
                ret_gn_g=[got[i][2] for i in ret_layers], ret_gn_b=[got[i][3] for i in ret_layers]))
    got_mlp0 = _exchange_wait(mlp0_handle, name="grads_wait_mlp0", after=results["ret_w_in"][0])
    update(("mlp_w1", "mlp_w2"),
           dict(mlp_w1=[got_mlp0[0]] + [got[i][-2] for i in range(1, DEPTH)],
                mlp_w2=[got_mlp0[1]] + [got[i][-1] for i in range(1, DEPTH)]))
    got_conv0 = _exchange_wait(conv0_handle, name="grads_wait_conv0", after=results["mlp_w1"][0])
    got[0] = _exchange_wait(grad_handles[0], name="grads_wait0", after=got_conv0[0]) + got_conv0
    conv_layers = [i for i in range(DEPTH) if i % 2 == 0]
    update(("conv_w_pw1", "conv_w_pw2", "conv_w_dw"),
           dict(conv_w_pw1=[got[i][0] for i in conv_layers], conv_w_pw2=[got[i][1] for i in conv_layers],
                conv_w_dw=[got[i][2] for i in conv_layers]))

    outs = [loss, grad_x]
    for kind in range(4):
        outs += [results[n][kind] for n in WEIGHTS]
    return tuple(outs)
```

```python
import functools

import jax
import jax.numpy as jnp
from jax import lax
from jax.experimental import pallas as pl
from jax.experimental.pallas import tpu as pltpu

F32, BF16 = jnp.float32, jnp.bfloat16
AXES = ("x", "y", "c")
NDEV = 8
DEPTH = 4
EPS = 1e-6
CHUNK = 64
CONV_WIDTH = 31
HALO = 32
RET_HEADS = 4
RET_BLOCK = 256
ROPE_BASE = 10000.0
LANES = 128
ADAM_LR, ADAM_B1, ADAM_B2, ADAM_EPS, ADAM_WD, ADAM_STEP = 0.001, 0.9, 0.999, 1e-08, 0.01, 10
VMEM_LIMIT = 56 * 1024 * 1024
VMEM_BLOCK_BUDGET = 44 * 1024 * 1024
MESH = pl.DeviceIdType.MESH
NT_DIMS = (((1,), (1,)), ((), ()))
TN_DIMS = (((0,), (0,)), ((), ()))


def _call(body, *, name, out_shape, in_specs, out_specs, grid=(), scratch=(), sem=None, aliases=None):
    params = dict(vmem_limit_bytes=VMEM_LIMIT)
    if sem is not None:
        params["dimension_semantics"] = sem
    return pl.pallas_call(body, name=name, grid=grid, in_specs=in_specs, out_specs=out_specs, out_shape=out_shape,
                          scratch_shapes=list(scratch), input_output_aliases=aliases or {},
                          compiler_params=pltpu.CompilerParams(**params))


def _row_tile(rows, want):
    t = min(rows, want)
    while rows % t:
        t //= 2
    return t


def _sds(shape, dtype):
    return jax.ShapeDtypeStruct(tuple(shape), dtype)


def _sigmoid(v):
    return 1.0 / (1.0 + jnp.exp(-v))


def _exchange(groups, *, gather, name):
    flat = [a for g in groups for a in g]
    n_in = len(flat)
    out_shapes = []
    for g in groups:
        s = g[0].shape if gather else g[0].shape[1:]
        lead = (NDEV,) if len(g) == 1 else (NDEV, len(g))
        out_shapes.append(_sds(lead + tuple(s), g[0].dtype))
    n_g = len(groups)

    def body(*refs):
        ins, outs = refs[:n_in], refs[n_in:n_in + n_g]
        send_sems, recv_sems, loc_sems = refs[n_in + n_g:]
        x, y, c = lax.axis_index("x"), lax.axis_index("y"), lax.axis_index("c")
        me = 4 * x + 2 * y + c
        locs, k = [], 0
        for gi, g in enumerate(groups):
            for li in range(len(g)):
                src = ins[k] if gather else ins[k].at[me]
                dst = outs[gi].at[me] if len(g) == 1 else outs[gi].at[me, li]
                cp = pltpu.make_async_copy(src, dst, loc_sems.at[k])
                cp.start()
                locs.append(cp)
                k += 1
        k0 = 0
        for gi, g in enumerate(groups):
            for r in range(1, NDEV):
                px = 1 - x if r & 4 else x
                py = 1 - y if r & 2 else y
                pc = 1 - c if r & 1 else c
                peer = 4 * px + 2 * py + pc
                for li in range(len(g)):
                    src = ins[k0 + li] if gather else ins[k0 + li].at[peer]
                    dst = outs[gi].at[me] if len(g) == 1 else outs[gi].at[me, li]
                    pltpu.make_async_remote_copy(src_ref=src, dst_ref=dst, send_sem=send_sems.at[gi * (NDEV - 1) + r - 1],
                                                 recv_sem=recv_sems.at[gi * (NDEV - 1) + r - 1], device_id=(px, py, pc),
                                                 device_id_type=MESH).start()
            k0 += len(g)
        for gi, g in enumerate(groups):
            for r in range(1, NDEV):
                px = 1 - x if r & 4 else x
                py = 1 - y if r & 2 else y
                pc = 1 - c if r & 1 else c
                peer = 4 * px + 2 * py + pc
                slab = pltpu.make_async_remote_copy(src_ref=outs[gi].at[me], dst_ref=outs[gi].at[peer],
                                                    send_sem=send_sems.at[gi * (NDEV - 1) + r - 1], recv_sem=recv_sems.at[gi * (NDEV - 1) + r - 1],
                                                    device_id=(px, py, pc), device_id_type=MESH)
                slab.wait_send()
                slab.wait_recv()
        for cp in locs:
            cp.wait()

    hbm = pl.BlockSpec(memory_space=pltpu.HBM)
    outs = _call(body, name=name, out_shape=tuple(out_shapes), in_specs=[hbm] * n_in, out_specs=tuple([hbm] * n_g),
                 scratch=[pltpu.SemaphoreType.DMA((n_g * (NDEV - 1),)), pltpu.SemaphoreType.DMA((n_g * (NDEV - 1),)),
                          pltpu.SemaphoreType.DMA((n_in,))])(*flat)
    return list(outs)


def _peer_of(x, y, c, r):
    return (1 - x if r & 4 else x, 1 - y if r & 2 else y, 1 - c if r & 1 else c)


def _exchange_start(groups, *, gather, name, after=None):
    flat = [pltpu.with_memory_space_constraint(a, pltpu.HBM) for g in groups for a in g]
    n_in, n_g = len(flat), len(groups)
    land_shapes = []
    for g in groups:
        s = g[0].shape if gather else g[0].shape[1:]
        lead = (NDEV,) if len(g) == 1 else (NDEV, len(g))
        land_shapes.append((lead + tuple(s), g[0].dtype))
    lands = [pltpu.with_memory_space_constraint(lax.empty(s, d), pltpu.HBM) for s, d in land_shapes]
    n_after = 0 if after is None else 1

    def body(*refs):
        ins, land = refs[:n_in], refs[n_in:n_in + n_g]
        send_sems, recv_sems, loc_sems = refs[n_in + n_g + n_after:n_in + n_g + n_after + 3]
        token = refs[-1]
        x, y, c = lax.axis_index("x"), lax.axis_index("y"), lax.axis_index("c")
        me = 4 * x + 2 * y + c
        k = 0
        for gi, g in enumerate(groups):
            for li in range(len(g)):
                dst = land[gi].at[me] if len(g) == 1 else land[gi].at[me, li]
                pltpu.make_async_copy(ins[k] if gather else ins[k].at[me], dst, loc_sems.at[k]).start()
                k += 1
        k0 = 0
        for gi, g in enumerate(groups):
            for r in range(1, NDEV):
                px, py, pc = _peer_of(x, y, c, r)
                peer = 4 * px + 2 * py + pc
                for li in range(len(g)):
                    dst = land[gi].at[me] if len(g) == 1 else land[gi].at[me, li]
                    pltpu.make_async_remote_copy(src_ref=ins[k0 + li] if gather else ins[k0 + li].at[peer], dst_ref=dst,
                                                 send_sem=send_sems.at[gi * (NDEV - 1) + r - 1], recv_sem=recv_sems.at[gi * (NDEV - 1) + r - 1],
                                                 device_id=(px, py, pc), device_id_type=MESH).start()
            k0 += len(g)
        token[...] = jnp.zeros_like(token)

    hbm = pl.BlockSpec(memory_space=pltpu.HBM)
    sem = pl.BlockSpec(memory_space=pltpu.SEMAPHORE)
    args = flat + lands + ([after] if n_after else [])
    outs = pl.pallas_call(body, name=name,
        out_shape=(pltpu.SemaphoreType.DMA((n_g * (NDEV - 1),)), pltpu.SemaphoreType.DMA((n_g * (NDEV - 1),)),
                   pltpu.SemaphoreType.DMA((n_in,)), *[pltpu.HBM(a.shape, a.dtype) for a in flat],
                   *[pltpu.HBM(s, d) for s, d in land_shapes], _sds((8, LANES), F32)),
        in_specs=[hbm] * (n_in + n_g) + [pl.BlockSpec(memory_space=pl.ANY)] * n_after,
        out_specs=(sem, sem, sem, *[hbm] * (n_in + n_g), pl.BlockSpec(memory_space=pltpu.VMEM)),
        input_output_aliases={k: 3 + k for k in range(n_in + n_g)},
        compiler_params=pltpu.CompilerParams(has_side_effects=pltpu.SideEffectType.DATAFLOW_SIDE_EFFECTING))(*args)
    handle = dict(sems=outs[0:3], srcs=list(outs[3:3 + n_in]), lands=list(outs[3 + n_in:3 + n_in + n_g]),
                  sizes=[len(g) for g in groups], gather=gather)
    return handle, outs[-1]


def _exchange_wait(handle, *, name, after):
    srcs, lands, sizes, gather = handle["srcs"], handle["lands"], handle["sizes"], handle["gather"]
    n_in, n_g = len(srcs), len(lands)

    def body(*refs):
        ins, land = refs[:n_in], refs[n_in:n_in + n_g]
        send_sems, recv_sems, loc_sems = refs[n_in + n_g:n_in + n_g + 3]
        x, y, c = lax.axis_index("x"), lax.axis_index("y"), lax.axis_index("c")
        me = 4 * x + 2 * y + c
        for gi in range(n_g):
            for r in range(1, NDEV):
                px, py, pc = _peer_of(x, y, c, r)
                peer = 4 * px + 2 * py + pc
                slab = pltpu.make_async_remote_copy(src_ref=land[gi].at[me], dst_ref=land[gi].at[peer],
                                                    send_sem=send_sems.at[gi * (NDEV - 1) + r - 1], recv_sem=recv_sems.at[gi * (NDEV - 1) + r - 1],
                                                    device_id=(px, py, pc), device_id_type=MESH)
                slab.wait_send()
                slab.wait_recv()
        k = 0
        for gi in range(n_g):
            for li in range(sizes[gi]):
                dst = land[gi].at[me] if sizes[gi] == 1 else land[gi].at[me, li]
                pltpu.make_async_copy(ins[k] if gather else ins[k].at[me], dst, loc_sems.at[k]).wait()
                k += 1

    hbm = pl.BlockSpec(memory_space=pltpu.HBM)
    sem = pl.BlockSpec(memory_space=pltpu.SEMAPHORE)
    outs = pl.pallas_call(body, name=name, out_shape=tuple(pltpu.HBM(a.shape, a.dtype) for a in srcs + lands),
        in_specs=[hbm] * (n_in + n_g) + [sem] * 3 + [pl.BlockSpec(memory_space=pl.ANY)],
        out_specs=tuple([hbm] * (n_in + n_g)), input_output_aliases={k: k for k in range(n_in + n_g)},
        compiler_params=pltpu.CompilerParams(has_side_effects=pltpu.SideEffectType.DATAFLOW_SIDE_EFFECTING))(
            *srcs, *lands, *handle["sems"], after)
    return list(outs[n_in:])


def _gate_part(first, dx, y_ref, g_ref, dy_ref, dg_ref, db_ref):
    @pl.when(first)
    def _():
        dg_ref[...] = jnp.zeros_like(dg_ref)
        db_ref[...] = jnp.zeros_like(db_ref)

    dy = dx * g_ref[...]
    dy_ref[...] = dy.astype(BF16)
    dg_ref[...] += jnp.sum(dx * y_ref[...].astype(F32), axis=0, keepdims=True)
    db_ref[...] += jnp.sum(dy, axis=0, keepdims=True)


def _norm_bwd_part(first, dhv, x_ref, g_ref, sc_ref, dres_ref, dx_ref, dsc_ref, dsh_ref, dg_ref):
    @pl.when(first)
    def _():
        dsc_ref[...] = jnp.zeros_like(dsc_ref)
        dsh_ref[...] = jnp.zeros_like(dsh_ref)
        dg_ref[...] = jnp.zeros_like(dg_ref)

    xv = x_ref[...]
    r = lax.rsqrt(jnp.mean(xv * xv, axis=-1, keepdims=True) + EPS)
    xhat = xv * r
    gain_v = g_ref[...]
    dsc_ref[...] += jnp.sum(dhv * (xhat * gain_v), axis=0, keepdims=True)
    dsh_ref[...] += jnp.sum(dhv, axis=0, keepdims=True)
    dxn = dhv * (1.0 + sc_ref[...])
    dg_ref[...] += jnp.sum(dxn * xhat, axis=0, keepdims=True)
    dxhat = dxn * gain_v
    dx = dres_ref[...] + r * (dxhat - xhat * jnp.mean(dxhat * xhat, axis=-1, keepdims=True))
    dx_ref[...] = dx
    return dx


def _final_loss(x, gain, target, y_prev, gate_prev, *, name):
    S, D = x.shape
    tm = _row_tile(S, 512)

    def body(x_ref, g_ref, t_ref, y_ref, gp_ref, loss_ref, dx_ref, dg_ref, dy_ref, dgp_ref, dbp_ref):
        first = pl.program_id(0) == 0

        @pl.when(first)
        def _():
            loss_ref[...] = jnp.zeros_like(loss_ref)
            dg_ref[...] = jnp.zeros_like(dg_ref)

        xv = x_ref[...]
        r = lax.rsqrt(jnp.mean(xv * xv, axis=-1, keepdims=True) + EPS)
        xhat = xv * r
        gv = g_ref[...]
        err = xhat * gv - t_ref[...]
        row_loss = jnp.mean(err * err, axis=-1, keepdims=True)
        loss_ref[...] += 0.5 * jnp.sum(row_loss, axis=0, keepdims=True)
        dy = err * (1.0 / D)
        dg_ref[...] += jnp.sum(dy * xhat, axis=0, keepdims=True)
        dxhat = dy * gv
        dx = r * (dxhat - xhat * jnp.mean(dxhat * xhat, axis=-1, keepdims=True))
        dx_ref[...] = dx
        _gate_part(first, dx, y_ref, gp_ref, dy_ref, dgp_ref, dbp_ref)

    row = pl.BlockSpec((tm, D), lambda i: (i, 0))
    vec = pl.BlockSpec((1, D), lambda i: (0, 0))
    one = pl.BlockSpec((1, 1), lambda i: (0, 0))
    vsh = _sds((1, D), F32)
    return _call(body, name=name, grid=(S // tm,), in_specs=[row, vec, row, row, vec],
                 out_specs=(one, row, vec, row, vec, vec),
                 out_shape=(_sds((1, 1), F32), _sds((S, D), F32), vsh, _sds((S, D), BF16), vsh, vsh),
                 sem=("arbitrary",))(x, gain, target, y_prev, gate_prev)


def _pick_tm(M, bytes_per_row, fixed_bytes):
    for tm in (1024, 512, 256, 128):
        if M % tm == 0 and 2 * tm * bytes_per_row + fixed_bytes <= VMEM_BLOCK_BUDGET:
            return tm
    return _row_tile(M, 128)


def _mm_nn(a, w, *, name, bias=None, relu2=False, ln=None, norm=None, res=None, gate=None, out_dtype=BF16, after=None):
    M, K = a.shape
    col = w.ndim == 3
    if col:
        nsh, ns = w.shape[0], w.shape[2]
        w_spec = pl.BlockSpec((nsh, K, ns), lambda i: (0, 0, 0))
    else:
        nsh, ns = 1, w.shape[1]
        w_spec = pl.BlockSpec((K, ns), lambda i: (0, 0))
    N = nsh * ns
    residual = res is not None
    out_bytes = (4 + 4 + 2) if residual else jnp.dtype(out_dtype).itemsize
    tm = _pick_tm(M, K * a.dtype.itemsize + N * out_bytes + (K * 2 if norm is not None else 0), 2 * K * N * 2)

    def body(*refs):
        it = iter(refs)
        a_ref, w_ref = next(it), next(it)
        b_ref = next(it) if bias is not None else None
        lg_ref, lb_ref = (next(it), next(it)) if ln is not None else (None, None)
        ng_ref, nsc_ref, nsh_ref = (next(it), next(it), next(it)) if norm is not None else (None, None, None)
        res_ref, gate_ref = (next(it), next(it)) if residual else (None, None)
        if after is not None:
            next(it)
        out_ref = next(it)
        raw_ref = next(it) if residual else None
        av = a_ref[...]
        if relu2:
            av = jnp.square(jnp.maximum(av.astype(F32), 0.0))
        if ln is not None:
            av, _ = _ln_silu(av, lg_ref[...], lb_ref[...])
        if norm is not None:
            r = lax.rsqrt(jnp.mean(av * av, axis=-1, keepdims=True) + EPS)
            av = (av * r) * ng_ref[...] * (1.0 + nsc_ref[...]) + nsh_ref[...]
        ab = av.astype(BF16)
        if norm is not None:
            next(it)[...] = ab
        for d in range(nsh):
            cols = slice(d * ns, (d + 1) * ns)
            acc = jnp.dot(ab, w_ref[d] if col else w_ref[...], preferred_element_type=F32)
            if b_ref is not None:
                acc = acc + b_ref[:, cols]
            if residual:
                raw_ref[:, cols] = acc.astype(BF16)
                out_ref[:, cols] = res_ref[:, cols] + gate_ref[:, cols] * acc
            else:
                out_ref[:, cols] = acc.astype(out_dtype)

    tile = pl.BlockSpec((tm, N), lambda i: (i, 0))
    vec = pl.BlockSpec((1, N), lambda i: (0, 0))
    in_specs, args = [pl.BlockSpec((tm, K), lambda i: (i, 0)), w_spec], [a, w]
    if bias is not None:
        in_specs.append(vec)
        args.append(bias)
    if ln is not None:
        in_specs += [pl.BlockSpec((1, K), lambda i: (0, 0))] * 2
        args += list(ln)
    if norm is not None:
        in_specs += [pl.BlockSpec((1, K), lambda i: (0, 0))] * 3
        args += list(norm)
    if residual:
        in_specs += [tile, vec]
        args += [res, gate]
        out_specs = [tile, tile]
        out_shape = [_sds((M, N), F32), _sds((M, N), BF16)]
    else:
        out_specs = [tile]
        out_shape = [_sds((M, N), out_dtype)]
    if after is not None:
        in_specs.append(pl.BlockSpec(memory_space=pl.ANY))
        args.append(after)
    if norm is not None:
        out_specs.append(pl.BlockSpec((tm, K), lambda i: (i, 0)))
        out_shape.append(_sds((M, K), BF16))
    outs = _call(body, name=name, grid=(M // tm,), in_specs=in_specs, out_specs=tuple(out_specs), out_shape=tuple(out_shape),
                 sem=("parallel",))(*args)
    return outs[0] if len(outs) == 1 else outs


def _mm_nt(g, w, *, name, z=None, out_dtype=F32, after=None, norm=None, gated=None):
    M, N = g.shape
    col = w.ndim == 3
    if col:
        nsh, K, ns = w.shape
        w_spec = pl.BlockSpec((nsh, K, ns), lambda i: (0, 0, 0))
    else:
        K = w.shape[0]
        w_spec = pl.BlockSpec((K, N), lambda i: (0, 0))
    assert norm is None or col
    kc = min(K, 1024)
    obytes = jnp.dtype(out_dtype).itemsize
    row_bytes = N * g.dtype.itemsize + K * obytes + (K * 2 if z is not None else 0)
    if norm is not None:
        row_bytes += 2 * K * 4 + (K * 4 if gated is not None else 0)
    tm = _pick_tm(M, row_bytes, 2 * K * N * 2 + 512 * K * 4)

    def body(*refs):
        it = iter(refs)
        g_ref, w_ref = next(it), next(it)
        z_ref = next(it) if z is not None else None
        norm_in = [next(it) for _ in range(4)] if norm is not None else None
        gate_in = [next(it) for _ in range(2)] if gated is not None else None
        if after is not None:
            next(it)
        out_ref = next(it)
        if col:
            acc = None
            for d in range(nsh):
                part = lax.dot_general(g_ref[:, d * ns:(d + 1) * ns].astype(BF16), w_ref[d], NT_DIMS,
                                       preferred_element_type=F32)
                acc = part if acc is None else acc + part
            if norm is None:
                out_ref[...] = acc.astype(out_dtype)
            else:
                first = pl.program_id(0) == 0
                dx = _norm_bwd_part(first, acc, *norm_in, out_ref, next(it), next(it), next(it))
                if gated is not None:
                    _gate_part(first, dx, *gate_in, next(it), next(it), next(it))
        else:
            gb = g_ref[...].astype(BF16)
            for cki in range(K // kc):
                cols = slice(cki * kc, (cki + 1) * kc)
                part = lax.dot_general(gb, w_ref[cols, :], NT_DIMS, preferred_element_type=F32)
                if z_ref is not None:
                    part = part * (2.0 * jnp.maximum(z_ref[:, cols].astype(F32), 0.0))
                out_ref[:, cols] = part.astype(out_dtype)

    row = pl.BlockSpec((tm, K), lambda i: (i, 0))
    vec = pl.BlockSpec((1, K), lambda i: (0, 0))
    vsh = _sds((1, K), F32)
    in_specs, args = [pl.BlockSpec((tm, N), lambda i: (i, 0)), w_spec], [g, w]
    out_specs, out_shape = [row], [_sds((M, K), out_dtype)]
    if z is not None:
        in_specs.append(row)
        args.append(z)
    if norm is not None:
        x, gain, sc, dres = norm
        in_specs += [row, vec, vec, row]
        args += [x, gain, sc, dres]
        out_specs += [vec, vec, vec]
        out_shape += [vsh, vsh, vsh]
    if gated is not None:
        in_specs += [row, vec]
        args += list(gated)
        out_specs += [row, vec, vec]
        out_shape += [_sds((M, K), BF16), vsh, vsh]
    if after is not None:
        in_specs.append(pl.BlockSpec(memory_space=pl.ANY))
        args.append(after)
    outs = _call(body, name=name, grid=(M // tm,), in_specs=in_specs, out_specs=tuple(out_specs), out_shape=tuple(out_shape),
                 sem=("parallel",) if norm is None else ("arbitrary",))(*args)
    return outs[0] if len(outs) == 1 else outs


def _mm_tn(a, g, *, name, col_shards=None, relu2=False, ln=None):
    M, K = a.shape
    N = g.shape[1]
    acc_budget = 8 * 1024 * 1024
    tm = _row_tile(M, 512)
    nm = M // tm
    if col_shards:
        ns = N // col_shards
        spc = col_shards
        while spc > 1 and K * ns * spc * 4 > acc_budget:
            spc //= 2
        grid = (col_shards // spc, nm)
        a_spec = pl.BlockSpec((tm, K), lambda c, m: (m, 0))
        g_spec = pl.BlockSpec((tm, spc * ns), lambda c, m: (m, c))
        out_spec = pl.BlockSpec((spc, K, ns), lambda c, m: (c, 0, 0))
        out_shape = _sds((col_shards, K, ns), BF16)
        acc_shape = (K, spc * ns)
    else:
        tk = K
        while tk > 128 and tk * N * 4 > acc_budget:
            tk //= 2
        grid = (K // tk, nm)
        a_spec = pl.BlockSpec((tm, tk), lambda c, m: (m, c))
        g_spec = pl.BlockSpec((tm, N), lambda c, m: (m, 0))
        out_spec = pl.BlockSpec((tk, N), lambda c, m: (c, 0))
        out_shape = _sds((K, N), BF16)
        acc_shape = (tk, N)
        assert ln is None or tk == K
    in_specs, args = [a_spec, g_spec], [a, g]
    if ln is not None:
        in_specs += [pl.BlockSpec((1, K), lambda c, m: (0, 0))] * 2
        args += list(ln)

    def body(a_ref, g_ref, *rest):
        out_ref, acc_ref = rest[-2:]
        m = pl.program_id(1)

        @pl.when(m == 0)
        def _():
            acc_ref[...] = jnp.zeros_like(acc_ref)

        av = a_ref[...]
        if relu2:
            av = jnp.square(jnp.maximum(av.astype(F32), 0.0))
        if ln is not None:
            av, _ = _ln_silu(av, rest[0][...], rest[1][...])
        acc_ref[...] += lax.dot_general(av.astype(BF16), g_ref[...].astype(BF16), TN_DIMS, preferred_element_type=F32)

        @pl.when(m == nm - 1)
        def _():
            if col_shards:
                for s in range(spc):
                    out_ref[s] = acc_ref[:, s * ns:(s + 1) * ns].astype(BF16)
            else:
                out_ref[...] = acc_ref[...].astype(BF16)

    return _call(body, name=name, grid=grid, in_specs=in_specs, out_specs=out_spec, out_shape=out_shape,
                 scratch=[pltpu.VMEM(acc_shape, F32)], sem=("parallel", "arbitrary"))(*args)


CONV_TILE = 256
CONV_GROUP = 32


def _glu(u, ch):
    return u[:, :ch] * _sigmoid(u[:, ch:])


def _fill_glu(buf, u_ref, uh_ref, ch, tile):
    first = pl.program_id(0) == 0
    buf[0:HALO] = jnp.where(first, 0.0, _glu(uh_ref[...], ch))
    buf[HALO:HALO + tile] = _glu(u_ref[...], ch)


CONV_SUB = 4


def _conv_specs(S, ch, tile):
    per = tile // HALO
    u_spec = pl.BlockSpec((tile, 2 * ch, LANES), lambda i: (i, 0, 0))
    uh_spec = pl.BlockSpec((HALO, 2 * ch, LANES), lambda i: (jnp.maximum(i * per - 1, 0), 0, 0))
    x_spec = pl.BlockSpec((tile, ch, LANES), lambda i: (i, 0, 0))
    xn_spec = pl.BlockSpec((HALO, ch, LANES), lambda i: (jnp.minimum((i + 1) * per, S // HALO - 1), 0, 0))
    w_spec = pl.BlockSpec((CONV_WIDTH, ch, LANES), lambda i: (0, 0, 0))
    v_spec = pl.BlockSpec((1, ch, LANES), lambda i: (0, 0, 0))
    return u_spec, uh_spec, x_spec, xn_spec, w_spec, v_spec


def _conv_mid_fwd(u3, w3, bdw3, *, name):
    S, ch2, _ = u3.shape
    ch = ch2 // 2
    tile = _row_tile(S, CONV_TILE)
    sub = _row_tile(tile, CONV_SUB)
    u_spec, uh_spec, x_spec, _, w_spec, v_spec = _conv_specs(S, ch, tile)

    def body(u_ref, uh_ref, w_ref, b_ref, o_ref, buf):
        _fill_glu(buf, u_ref, uh_ref, ch, tile)

        def step(q, carry):
            acc = [b_ref[...], None]
            for k in range(CONV_WIDTH):
                term = buf[pl.ds(q * sub + (HALO - CONV_WIDTH + 1 + k), sub)] * w_ref[k]
                acc[k % 2] = term if acc[k % 2] is None else acc[k % 2] + term
            o_ref[pl.ds(q * sub, sub)] = acc[0] + acc[1]
            return carry

        lax.fori_loop(0, tile // sub, step, 0)

    return _call(body, name=name, grid=(S // tile,), in_specs=[u_spec, uh_spec, w_spec, v_spec], out_specs=x_spec,
                 out_shape=_sds((S, ch, LANES), F32), scratch=[pltpu.VMEM((tile + HALO, ch, LANES), F32)],
                 sem=("parallel",))(u3, u3, w3, bdw3)


def _ln_silu(v, gv, bv):
    mu = jnp.mean(v, axis=-1, keepdims=True)
    cen = v - mu
    rstd = lax.rsqrt(jnp.mean(cen * cen, axis=-1, keepdims=True) + EPS)
    nrm = cen * rstd
    ln = nrm * gv + bv
    sg = _sigmoid(ln)
    return ln * sg, (nrm, rstd, ln, sg)


def _ln_silu_bwd(dwo, ds, lng, lnb, *, name):
    S, D = dwo.shape
    tm = _row_tile(S, 512)

    def body(v_ref, ds_ref, g_ref, b_ref, ddw_ref, dg_ref, db_ref, dbdw_ref):
        @pl.when(pl.program_id(0) == 0)
        def _():
            dg_ref[...] = jnp.zeros_like(dg_ref)
            db_ref[...] = jnp.zeros_like(db_ref)
            dbdw_ref[...] = jnp.zeros_like(dbdw_ref)

        gv = g_ref[...]
        _, (nrm, rstd, ln, sg) = _ln_silu(v_ref[...], gv, b_ref[...])
        dln = ds_ref[...] * (sg * (1.0 + ln * (1.0 - sg)))
        dg_ref[...] += jnp.sum(dln * nrm, axis=0, keepdims=True)
        db_ref[...] += jnp.sum(dln, axis=0, keepdims=True)
        dn = dln * gv
        ddw = rstd * (dn - jnp.mean(dn, axis=-1, keepdims=True) - nrm * jnp.mean(dn * nrm, axis=-1, keepdims=True))
        dbdw_ref[...] += jnp.sum(ddw, axis=0, keepdims=True)
        ddw_ref[...] = ddw

    row = pl.BlockSpec((tm, D), lambda i: (i, 0))
    vec = pl.BlockSpec((1, D), lambda i: (0, 0))
    vsh = _sds((1, D), F32)
    return _call(body, name=name, grid=(S // tm,), in_specs=[row, row, vec, vec], out_specs=(row, vec, vec, vec),
                 out_shape=(_sds((S, D), F32), vsh, vsh, vsh), sem=("arbitrary",))(dwo, ds, lng, lnb)


def _conv_mid_bwd_dw(u3, ddw3, w3, *, name):
    S, ch2, _ = u3.shape
    ch = ch2 // 2
    tile = _row_tile(S, CONV_TILE)
    rows = _row_tile(tile, CONV_GROUP)
    sub = _row_tile(rows, CONV_SUB)
    last = S // tile - 1
    u_spec, uh_spec, x_spec, xn_spec, w_spec, _ = _conv_specs(S, ch, tile)
    b_spec = pl.BlockSpec((1, 2 * ch, LANES), lambda i: (0, 0, 0))

    def body(u_ref, uh_ref, d_ref, dn_ref, w_ref, du_ref, dw_ref, db_ref, gbuf, dbuf, stage):
        @pl.when(pl.program_id(0) == 0)
        def _():
            dw_ref[...] = jnp.zeros_like(dw_ref)
            db_ref[...] = jnp.zeros_like(db_ref)

        _fill_glu(gbuf, u_ref, uh_ref, ch, tile)
        dbuf[0:tile] = d_ref[...]
        dbuf[tile:tile + HALO] = jnp.where(pl.program_id(0) == last, 0.0, dn_ref[...])

        def group(r, carry):
            t0 = r * rows
            def step(q, c):
                s0 = t0 + q * sub
                ddw = dbuf[pl.ds(s0, sub)]
                acc = [None, None]
                for k in range(CONV_WIDTH):
                    term = dbuf[pl.ds(s0 + (CONV_WIDTH - 1 - k), sub)] * w_ref[k]
                    acc[k % 2] = term if acc[k % 2] is None else acc[k % 2] + term
                    dw_ref[k] += jnp.sum(ddw * gbuf[pl.ds(s0 + (HALO - CONV_WIDTH + 1 + k), sub)], axis=0)
                stage[pl.ds(q * sub, sub)] = acc[0] + acc[1]
                return c

            lax.fori_loop(0, rows // sub, step, 0)
            dglu = stage[...]
            uv = u_ref[pl.ds(t0, rows)]
            av, sg = uv[:, :ch], _sigmoid(uv[:, ch:])
            da = dglu * sg
            dg = dglu * av * sg * (1.0 - sg)
            du_ref[pl.ds(t0, rows), 0:ch] = da
            du_ref[pl.ds(t0, rows), ch:2 * ch] = dg
            db_ref[:, 0:ch] += jnp.sum(da, axis=0, keepdims=True)
            db_ref[:, ch:2 * ch] += jnp.sum(dg, axis=0, keepdims=True)
            return carry

        lax.fori_loop(0, tile // rows, group, 0)

    return _call(body, name=name, grid=(S // tile,), in_specs=[u_spec, uh_spec, x_spec, xn_spec, w_spec],
                 out_specs=(u_spec, w_spec, b_spec),
                 out_shape=(_sds((S, 2 * ch, LANES), F32), _sds((CONV_WIDTH, ch, LANES), F32), _sds((1, 2 * ch, LANES), F32)),
                 scratch=[pltpu.VMEM((tile + HALO, ch, LANES), F32), pltpu.VMEM((tile + HALO, ch, LANES), F32),
                          pltpu.VMEM((rows, ch, LANES), F32)],
                 sem=("arbitrary",))(u3, u3, ddw3, ddw3, w3)


def _ret_tables(S, dk):
    B = min(RET_BLOCK, S)
    lg = jnp.log(1.0 - 2.0 ** (-5.0 - jnp.arange(RET_HEADS, dtype=F32)))
    idx = jnp.arange(B, dtype=F32)
    diff = idx[:, None] - idx[None, :]
    cq, ck = (jnp.arange(B) // CHUNK)[:, None], (jnp.arange(B) // CHUNK)[None, :]
    dist = jnp.where(cq == ck, jnp.abs(diff), diff)
    mask = jnp.where(ck <= cq, jnp.exp(lg[:, None, None] * dist[None]), 0.0)
    xi = jnp.exp(lg[:, None] * (idx + 1.0))[..., None]
    zeta = jnp.exp(lg[:, None] * (B - 1.0 - idx))[..., None]
    gam = jnp.broadcast_to(jnp.exp(lg * B)[:, None, None], (RET_HEADS, 8, LANES))
    pos = jnp.arange(S, dtype=F32)
    inv = ROPE_BASE ** (-jnp.arange(0, dk, 2, dtype=F32) / dk)
    ang = pos[:, None] * inv[None, :]
    return dict(B=B, mask=mask, xi=xi, zeta=zeta, gam=gam, cos=jnp.cos(ang), sin=jnp.sin(ang))


def _rope(v, cs, sn):
    half = v.shape[1] // 2
    v1, v2 = v[:, :half], v[:, half:]
    return jnp.concatenate([v1 * cs - v2 * sn, v2 * cs + v1 * sn], axis=-1)


def _rope_t(d, cs, sn):
    half = d.shape[1] // 2
    d1, d2 = d[:, :half], d[:, half:]
    return jnp.concatenate([d1 * cs + d2 * sn, d2 * cs - d1 * sn], axis=-1)


def _dot(a, b):
    return jnp.dot(a.astype(BF16), b.astype(BF16), preferred_element_type=F32)


def _dot_nt(a, b):
    return lax.dot_general(a.astype(BF16), b.astype(BF16), NT_DIMS, preferred_element_type=F32)


def _dot_tn(a, b):
    return lax.dot_general(a.astype(BF16), b.astype(BF16), TN_DIMS, preferred_element_type=F32)


def _ret_specs(S, D, B, RB, reverse):
    dk, dv = D // RET_HEADS, 2 * D // RET_HEADS
    nb = S // RB
    blk = (lambda ib: nb - 1 - ib) if reverse else (lambda ib: ib)
    q = pl.BlockSpec((RB, dk), lambda h, ib: (blk(ib), h))
    k = pl.BlockSpec((RB, dk), lambda h, ib: (blk(ib), RET_HEADS + h))
    v = pl.BlockSpec((RB, dv), lambda h, ib: (blk(ib), RET_HEADS + h))
    gate = pl.BlockSpec((RB, dv), lambda h, ib: (blk(ib), 2 * RET_HEADS + h))
    yv = pl.BlockSpec((RB, dv), lambda h, ib: (blk(ib), h))
    rope = pl.BlockSpec((RB, dk // 2), lambda h, ib: (blk(ib), 0))
    mask = pl.BlockSpec((None, B, B), lambda h, ib: (h, 0, 0))
    dec = pl.BlockSpec((None, B, 1), lambda h, ib: (h, 0, 0))
    gam = pl.BlockSpec((None, 8, LANES), lambda h, ib: (h, 0, 0))
    gn = pl.BlockSpec((1, dv), lambda h, ib: (0, h))
    return dict(q=q, k=k, v=v, gate=gate, yv=yv, rope=rope, mask=mask, dec=dec, gam=gam, gn=gn)


def _group_norm(yr, gv, bv):
    mu = jnp.mean(yr, axis=-1, keepdims=True)
    cen = yr - mu
    rstd = lax.rsqrt(jnp.mean(cen * cen, axis=-1, keepdims=True) + EPS)
    nrm = cen * rstd
    return nrm, rstd, nrm * gv + bv


def _ret_fwd(proj, tb, gng, gnb, *, name):
    S, D = proj.shape[0], proj.shape[1] // 6
    dk, dv = D // RET_HEADS, 2 * D // RET_HEADS
    B = tb["B"]
    RB = _row_tile(S, 2 * B)
    nsub = RB // B
    sp = _ret_specs(S, D, B, RB, False)
    scale = dk ** -0.5

    def body(q_ref, k_ref, v_ref, gt_ref, cos_ref, sin_ref, mask_ref, xi_ref, zeta_ref, gam_ref, gng_ref, gnb_ref,
             yr_ref, yg_ref, state):
        @pl.when(pl.program_id(1) == 0)
        def _():
            state[...] = jnp.zeros_like(state)

        for sb in range(nsub):
            rows = slice(sb * B, (sb + 1) * B)
            cs, sn = cos_ref[rows, :], sin_ref[rows, :]
            q = _rope(q_ref[rows, :].astype(F32), cs, sn)
            k = _rope(k_ref[rows, :].astype(F32), cs, sn) * scale
            vb = v_ref[rows, :]
            p = _dot_nt(q, k) * mask_ref[...]
            st = state[...]
            yr = _dot(p, vb) + _dot(q * xi_ref[...], st)
            state[...] = st * gam_ref[0:1, 0:1] + _dot_tn(k * zeta_ref[...], vb)
            _, _, gn = _group_norm(yr, gng_ref[...], gnb_ref[...])
            gt = gt_ref[rows, :].astype(F32)
            yr_ref[rows, :] = yr.astype(BF16)
            yg_ref[rows, :] = (gt * _sigmoid(gt) * gn).astype(BF16)

    return _call(body, name=name, grid=(RET_HEADS, S // RB),
                 in_specs=[sp["q"], sp["k"], sp["v"], sp["gate"], sp["rope"], sp["rope"], sp["mask"], sp["dec"], sp["dec"],
                           sp["gam"], sp["gn"], sp["gn"]],
                 out_specs=(sp["yv"], sp["yv"]), out_shape=(_sds((S, 2 * D), BF16), _sds((S, 2 * D), BF16)),
                 scratch=[pltpu.VMEM((dk, dv), F32)], sem=("parallel", "arbitrary"))(
                     proj, proj, proj, proj, tb["cos"], tb["sin"], tb["mask"], tb["xi"], tb["zeta"], tb["gam"], gng, gnb)


def _ret_bwd_q(proj, yr, dyg, tb, gng, gnb, *, name):
    S, D = proj.shape[0], proj.shape[1] // 6
    dk, dv = D // RET_HEADS, 2 * D // RET_HEADS
    B = tb["B"]
    RB = _row_tile(S, 2 * B)
    nsub = RB // B
    sp = _ret_specs(S, D, B, RB, False)
    scale = dk ** -0.5

    def body(q_ref, k_ref, v_ref, gt_ref, yr_ref, dyg_ref, cos_ref, sin_ref, mask_ref, xi_ref, zeta_ref, gam_ref,
             gng_ref, gnb_ref, dq_ref, dgt_ref, dyr_ref, dgg_ref, dgb_ref, state):
        @pl.when(pl.program_id(1) == 0)
        def _():
            state[...] = jnp.zeros_like(state)
            dgg_ref[...] = jnp.zeros_like(dgg_ref)
            dgb_ref[...] = jnp.zeros_like(dgb_ref)

        for sb in range(nsub):
            rows = slice(sb * B, (sb + 1) * B)
            cs, sn = cos_ref[rows, :], sin_ref[rows, :]
            q = _rope(q_ref[rows, :].astype(F32), cs, sn)
            k = _rope(k_ref[rows, :].astype(F32), cs, sn) * scale
            vb = v_ref[rows, :]
            gv = gng_ref[...]
            nrm, rstd, gn = _group_norm(yr_ref[rows, :].astype(F32), gv, gnb_ref[...])
            gt = gt_ref[rows, :].astype(F32)
            sg = _sigmoid(gt)
            dyg = dyg_ref[rows, :].astype(F32)
            dgt_ref[rows, :] = (dyg * gn * (sg * (1.0 + gt * (1.0 - sg)))).astype(BF16)
            dgn = dyg * (gt * sg)
            dgg_ref[...] += jnp.sum(dgn * nrm, axis=0, keepdims=True)
            dgb_ref[...] += jnp.sum(dgn, axis=0, keepdims=True)
            dn = dgn * gv
            dyr = rstd * (dn - jnp.mean(dn, axis=-1, keepdims=True) - nrm * jnp.mean(dn * nrm, axis=-1, keepdims=True))
            dyr_ref[rows, :] = dyr.astype(BF16)
            dp = _dot_nt(dyr, vb) * mask_ref[...]
            st = state[...]
            dq = _dot(dp, k) + _dot_nt(dyr, st) * xi_ref[...]
            dq_ref[rows, :] = _rope_t(dq, cs, sn).astype(BF16)
            state[...] = st * gam_ref[0:1, 0:1] + _dot_tn(k * zeta_ref[...], vb)

    qout = pl.BlockSpec((RB, dk), lambda h, ib: (ib, h))
    return _call(body, name=name, grid=(RET_HEADS, S // RB),
                 in_specs=[sp["q"], sp["k"], sp["v"], sp["gate"], sp["yv"], sp["yv"], sp["rope"], sp["rope"], sp["mask"],
                           sp["dec"], sp["dec"], sp["gam"], sp["gn"], sp["gn"]],
                 out_specs=(qout, sp["yv"], sp["yv"], sp["gn"], sp["gn"]),
                 out_shape=(_sds((S, D), BF16), _sds((S, 2 * D), BF16), _sds((S, 2 * D), BF16), _sds((1, 2 * D), F32),
                            _sds((1, 2 * D), F32)),
                 scratch=[pltpu.VMEM((dk, dv), F32)], sem=("parallel", "arbitrary"))(
                     proj, proj, proj, proj, yr, dyg, tb["cos"], tb["sin"], tb["mask"], tb["xi"], tb["zeta"], tb["gam"],
                     gng, gnb)


def _ret_bwd_kv(proj, dyr, dq, dgt, tb, *, name):
    S, D = proj.shape[0], proj.shape[1] // 6
    dk, dv = D // RET_HEADS, 2 * D // RET_HEADS
    B = tb["B"]
    RB = _row_tile(S, 2 * B)
    nsub = RB // B
    nb = S // RB
    scale = dk ** -0.5

    def body(p_ref, dyr_ref, dq_ref, dgt_ref, cos_ref, sin_ref, mask_ref, xi_ref, zeta_ref, gam_ref, out_ref, dstate):
        @pl.when(pl.program_id(0) == 0)
        def _():
            dstate[...] = jnp.zeros_like(dstate)

        out_ref[:, 0:D] = dq_ref[...]
        out_ref[:, 4 * D:6 * D] = dgt_ref[...]
        for sb in reversed(range(nsub)):
            rows = slice(sb * B, (sb + 1) * B)
            cs, sn = cos_ref[rows, :], sin_ref[rows, :]
            for h in range(RET_HEADS):
                kcols = slice(D + h * dk, D + (h + 1) * dk)
                vcols = slice(2 * D + h * dv, 2 * D + (h + 1) * dv)
                q = _rope(p_ref[rows, h * dk:(h + 1) * dk].astype(F32), cs, sn)
                k = _rope(p_ref[rows, kcols].astype(F32), cs, sn) * scale
                vb = p_ref[rows, vcols]
                dyr_h = dyr_ref[rows, h * dv:(h + 1) * dv]
                mk = mask_ref[h]
                p = _dot_nt(q, k) * mk
                dp = _dot_nt(dyr_h, vb) * mk
                ds = dstate[h]
                zt = zeta_ref[h]
                dkr = _dot_tn(dp, q) + _dot_nt(vb, ds) * zt
                out_ref[rows, kcols] = _rope_t(dkr * scale, cs, sn).astype(BF16)
                out_ref[rows, vcols] = (_dot_tn(p, dyr_h) + _dot(k * zt, ds)).astype(BF16)
                dstate[h] = ds * gam_ref[h, 0:1, 0:1] + _dot_tn(q * xi_ref[h], dyr_h)

    def rev(width):
        return pl.BlockSpec((RB, width), lambda ib: (nb - 1 - ib, 0))

    def whole(a):
        return pl.BlockSpec(a.shape, lambda ib: (0,) * a.ndim)

    return _call(body, name=name, grid=(nb,),
                 in_specs=[rev(6 * D), rev(2 * D), rev(D), rev(2 * D), rev(dk // 2), rev(dk // 2), whole(tb["mask"]),
                           whole(tb["xi"]), whole(tb["zeta"]), whole(tb["gam"])],
                 out_specs=rev(6 * D), out_shape=_sds((S, 6 * D), BF16), scratch=[pltpu.VMEM((RET_HEADS, dk, dv), F32)],
                 sem=("arbitrary",))(proj, dyr, dq, dgt, tb["cos"], tb["sin"], tb["mask"], tb["xi"], tb["zeta"], tb["gam"])


def _ada_fwd(c_all, ada_w, *, name):
    L, D, ns = ada_w.shape

    def body(c_ref, w_ref, out_ref):
        cv = c_ref[...]
        cond = cv * _sigmoid(cv)
        out_ref[...] = jnp.dot(cond.astype(BF16), w_ref[...].astype(BF16), preferred_element_type=F32)

    return _call(body, name=name, grid=(L,), in_specs=[pl.BlockSpec((NDEV, D), lambda l: (0, 0)),
                                                      pl.BlockSpec((None, D, ns), lambda l: (l, 0, 0))],
                 out_specs=pl.BlockSpec((None, NDEV, ns), lambda l: (l, 0, 0)), out_shape=_sds((L, NDEV, ns), F32),
                 sem=("parallel",))(c_all, ada_w)


def _ada_bwd(c_all, dmod_cols, *, name):
    L, _, ns = dmod_cols.shape
    D = c_all.shape[1]

    def body(c_ref, d_ref, out_ref):
        cv = c_ref[...]
        cond = cv * _sigmoid(cv)
        out_ref[...] = lax.dot_general(cond.astype(BF16), d_ref[...].astype(BF16), TN_DIMS, preferred_element_type=F32)

    return _call(body, name=name, grid=(L,), in_specs=[pl.BlockSpec((NDEV, D), lambda l: (0, 0)),
                                                      pl.BlockSpec((None, NDEV, ns), lambda l: (l, 0, 0))],
                 out_specs=pl.BlockSpec((None, D, ns), lambda l: (l, 0, 0)), out_shape=_sds((L, D, ns), F32),
                 sem=("parallel",))(c_all, dmod_cols)


def _adamw(w, m, v, parts, *, name):
    shape = w.shape
    L, cols = len(parts), shape[-1]
    rows = w.size // (cols * L)
    n = parts[0].shape[0]
    tr = rows
    for cand in (256, 128, 64, 32, 16, 8):
        if rows % cand == 0:
            tr = cand
            break
    c1 = 1.0 - ADAM_B1 ** ADAM_STEP
    c2 = 1.0 - ADAM_B2 ** ADAM_STEP

    def body(w_ref, m_ref, v_ref, *rest):
        p_refs = rest[:L]
        g_ref, d_ref, m2_ref, v2_ref = rest[L:]
        layer = pl.program_id(0)
        for l in range(L):
            @pl.when(layer == l)
            def _(p_ref=p_refs[l]):
                g = p_ref[0].astype(F32)
                for i in range(1, n):
                    g = g + p_ref[i].astype(F32)
                m2 = ADAM_B1 * m_ref[...] + (1.0 - ADAM_B1) * g
                v2 = ADAM_B2 * v_ref[...] + (1.0 - ADAM_B2) * (g * g)
                g_ref[...] = g
                m2_ref[...] = m2
                v2_ref[...] = v2
                d_ref[...] = -ADAM_LR * ((m2 / c1) / (jnp.sqrt(v2 / c2) + ADAM_EPS) + ADAM_WD * w_ref[...])

    mat = pl.BlockSpec((None, tr, cols), lambda l, i: (l, i, 0))

    def part_spec(k):
        return pl.BlockSpec((n, tr, cols), lambda l, i: (0, jnp.where(l == k, i, 0), 0))

    outs = _call(body, name=name, grid=(L, rows // tr), in_specs=[mat, mat, mat] + [part_spec(k) for k in range(L)],
                 out_specs=(mat, mat, mat, mat), out_shape=tuple(_sds((L, rows, cols), F32) for _ in range(4)),
                 sem=("parallel", "parallel"))(w.reshape(L, rows, cols), m.reshape(L, rows, cols), v.reshape(L, rows, cols),
                                               *[p.reshape(n, rows, cols) for p in parts])
    return tuple(o.reshape(shape) for o in outs)


SMALL = ("ada_b", "norm_mix_g", "norm_mlp_g", "conv_b_pw1", "conv_b_dw", "conv_ln_g", "conv_ln_b", "conv_b_pw2",
         "final_norm_g")
WEIGHTS = ("ada_w", "ada_b", "norm_mix_g", "norm_mlp_g", "conv_w_pw1", "conv_b_pw1", "conv_w_dw", "conv_b_dw", "conv_ln_g",
           "conv_ln_b", "conv_w_pw2", "conv_b_pw2", "ret_w_in", "ret_gn_g", "ret_gn_b", "ret_w_out", "mlp_w1", "mlp_w2",
           "final_norm_g")


def kernel(x, c, ada_w, ada_b, norm_mix_g, norm_mlp_g, conv_w_pw1, conv_b_pw1, conv_w_dw, conv_b_dw, conv_ln_g, conv_ln_b, conv_w_pw2, conv_b_pw2, ret_w_in, ret_gn_g, ret_gn_b, ret_w_out, mlp_w1, mlp_w2, final_norm_g, loss_target, m_ada_w, m_ada_b, m_norm_mix_g, m_norm_mlp_g, m_conv_w_pw1, m_conv_b_pw1, m_conv_w_dw, m_conv_b_dw, m_conv_ln_g, m_conv_ln_b, m_conv_w_pw2, m_conv_b_pw2, m_ret_w_in, m_ret_gn_g, m_ret_gn_b, m_ret_w_out, m_mlp_w1, m_mlp_w2, m_final_norm_g, v_ada_w, v_ada_b, v_norm_mix_g, v_norm_mlp_g, v_conv_w_pw1, v_conv_b_pw1, v_conv_w_dw, v_conv_b_dw, v_conv_ln_g, v_conv_ln_b, v_conv_w_pw2, v_conv_b_pw2, v_ret_w_in, v_ret_gn_g, v_ret_gn_b, v_ret_w_out, v_mlp_w1, v_mlp_w2, v_final_norm_g):
    W = dict(ada_w=ada_w, ada_b=ada_b, norm_mix_g=norm_mix_g, norm_mlp_g=norm_mlp_g, conv_w_pw1=conv_w_pw1,
             conv_b_pw1=conv_b_pw1, conv_w_dw=conv_w_dw, conv_b_dw=conv_b_dw, conv_ln_g=conv_ln_g, conv_ln_b=conv_ln_b,
             conv_w_pw2=conv_w_pw2, conv_b_pw2=conv_b_pw2, ret_w_in=ret_w_in, ret_gn_g=ret_gn_g, ret_gn_b=ret_gn_b,
             ret_w_out=ret_w_out, mlp_w1=mlp_w1, mlp_w2=mlp_w2, final_norm_g=final_norm_g)
    Mo = dict(ada_w=m_ada_w, ada_b=m_ada_b, norm_mix_g=m_norm_mix_g, norm_mlp_g=m_norm_mlp_g, conv_w_pw1=m_conv_w_pw1,
              conv_b_pw1=m_conv_b_pw1, conv_w_dw=m_conv_w_dw, conv_b_dw=m_conv_b_dw, conv_ln_g=m_conv_ln_g,
              conv_ln_b=m_conv_ln_b, conv_w_pw2=m_conv_w_pw2, conv_b_pw2=m_conv_b_pw2, ret_w_in=m_ret_w_in,
              ret_gn_g=m_ret_gn_g, ret_gn_b=m_ret_gn_b, ret_w_out=m_ret_w_out, mlp_w1=m_mlp_w1, mlp_w2=m_mlp_w2,
              final_norm_g=m_final_norm_g)
    Vo = dict(ada_w=v_ada_w, ada_b=v_ada_b, norm_mix_g=v_norm_mix_g, norm_mlp_g=v_norm_mlp_g, conv_w_pw1=v_conv_w_pw1,
              conv_b_pw1=v_conv_b_pw1, conv_w_dw=v_conv_w_dw, conv_b_dw=v_conv_b_dw, conv_ln_g=v_conv_ln_g,
              conv_ln_b=v_conv_ln_b, conv_w_pw2=v_conv_w_pw2, conv_b_pw2=v_conv_b_pw2, ret_w_in=v_ret_w_in,
              ret_gn_g=v_ret_gn_g, ret_gn_b=v_ret_gn_b, ret_w_out=v_ret_w_out, mlp_w1=v_mlp_w1, mlp_w2=v_mlp_w2,
              final_norm_g=v_final_norm_g)

    S, D = x.shape[1], x.shape[2]
    CH = D // LANES
    n_conv, n_ret = conv_w_pw1.shape[0], ret_w_in.shape[0]
    me = 4 * lax.axis_index("x") + 2 * lax.axis_index("y") + lax.axis_index("c")
    xs = x.reshape(S, D)
    target = loss_target.reshape(S, D)

    def mixer_shards(i):
        j = i // 2
        if i % 2 == 0:
            return [[conv_w_pw1[j].astype(BF16)], [conv_w_pw2[j].astype(BF16)]]
        return [[ret_w_in[j].astype(BF16)], [ret_w_out[j].astype(BF16)]]

    def mlp_shards(i):
        return [[mlp_w1[i].astype(BF16)], [mlp_w2[i].astype(BF16)]]

    def mixer_weights(i, got):
        return got[0], got[1].reshape(-1, D)

    def mlp_weights(got):
        return got[0], got[1].reshape(4 * D, D)

    first_handle, _ = _exchange_start(mixer_shards(0)[:1], gather=True, name="gather_start_first")
    small = _exchange([[conv_w_dw], [ret_gn_g], [ret_gn_b], [c]], gather=True, name="gather_small")
    dw_g, gng_g, gnb_g, c_g = small
    dw3 = jnp.transpose(dw_g, (1, 2, 0, 3)).reshape(n_conv, CONV_WIDTH, CH, LANES)
    gng_full = jnp.transpose(gng_g, (1, 2, 0, 3)).reshape(n_ret, 1, 2 * D)
    gnb_full = jnp.transpose(gnb_g, (1, 2, 0, 3)).reshape(n_ret, 1, 2 * D)
    c_all = c_g.reshape(NDEV, D)

    mod_cols = _ada_fwd(c_all, ada_w, name="ada_fwd")
    mod_all = _exchange([[mod_cols]], gather=True, name="gather_mod")[0]
    mod = lax.dynamic_index_in_dim(mod_all, me, axis=2, keepdims=False)
    mod = jnp.transpose(mod, (1, 0, 2)).reshape(DEPTH, 6 * D) + ada_b
    mods = [[mod[i, j * D:(j + 1) * D].reshape(1, D) for j in range(6)] for i in range(DEPTH)]
    tb = _ret_tables(S, D // RET_HEADS)

    def vec(a):
        return a.reshape(1, -1)

    mix_w = (_exchange_wait(first_handle, name="gather_wait_first", after=mod)[0], None)
    handle, token = _exchange_start(mixer_shards(0)[1:] + mlp_shards(0), gather=True, name="gather_start_rest0",
                                    after=mix_w[0])
    saved = []
    weights = []
    xcur = xs
    for i in range(DEPTH):
        sh1, sc1, g1, sh2, sc2, g2 = mods[i]
        j = i // 2
        if i > 0:
            mix_w = mixer_weights(i, _exchange_wait(handle, name=f"gather_wait_mix{i}", after=xcur))
            handle, token = _exchange_start(mlp_shards(i), gather=True, name=f"gather_start_mlp{i}", after=mix_w[0])
        st = dict(x_in=xcur)
        norm1 = (vec(norm_mix_g[i]), sc1, sh1)
        if i % 2 == 0:
            u, h = _mm_nn(xcur, mix_w[0], norm=norm1, bias=vec(conv_b_pw1[j]), out_dtype=F32, name=f"pw1_fwd{i}", after=token)
            u3 = u.reshape(S, 2 * CH, LANES)
            dwo = _conv_mid_fwd(u3, dw3[j], conv_b_dw[j].reshape(1, CH, LANES), name=f"conv_mid_fwd{i}").reshape(S, D)
            if i == 0:
                got = _exchange_wait(handle, name="gather_wait_rest0", after=dwo)
                mix_w, mlp_w = (mix_w[0], got[0].reshape(-1, D)), mlp_weights(got[1:3])
                handle, token = _exchange_start(mixer_shards(1), gather=True, name="gather_start_mix1", after=got[0])
            xcur, y_raw = _mm_nn(dwo, mix_w[1], ln=(vec(conv_ln_g[j]), vec(conv_ln_b[j])), bias=vec(conv_b_pw2[j]), res=xcur,
                                 gate=g1, name=f"pw2_fwd{i}")
            st.update(u3=u3, dwo=dwo, y_raw=y_raw)
        else:
            proj, h = _mm_nn(xcur, mix_w[0], norm=norm1, name=f"ret_in_fwd{i}", after=token)
            yr, yg = _ret_fwd(proj, tb, gng_full[j], gnb_full[j], name=f"ret_fwd{i}")
            xcur, y_raw = _mm_nn(yg, mix_w[1], res=xcur, gate=g1, name=f"ret_out_fwd{i}")
            st.update(proj=proj, yr=yr, yg=yg, y_raw=y_raw)
        st.update(h=h, x_mid=xcur)
        if i > 0:
            mlp_w = mlp_weights(_exchange_wait(handle, name=f"gather_wait_mlp{i}", after=xcur))
            if i + 1 < DEPTH:
                handle, token = _exchange_start(mixer_shards(i + 1), gather=True, name=f"gather_start_mix{i + 1}",
                                                after=mlp_w[0])
        z, h2 = _mm_nn(xcur, mlp_w[0], norm=(vec(norm_mlp_g[i]), sc2, sh2), name=f"mlp1_fwd{i}", after=token)
        xcur, o_raw = _mm_nn(z, mlp_w[1], relu2=True, res=xcur, gate=g2, name=f"mlp2_fwd{i}")
        st.update(h2=h2, z=z, o_raw=o_raw)
        saved.append(st)
        weights.append(mix_w + mlp_w)

    g2_last = mods[DEPTH - 1][5]
    loss_local, dx, d_final_g, dy, dgate, _ = _final_loss(xcur, vec(final_norm_g), target, saved[-1]["o_raw"], g2_last,
                                                          name="final_loss")
    loss = lax.psum(loss_local[0, 0], AXES)

    dmod_rows = [None] * DEPTH
    d_mix_g, d_mlp_g = [None] * DEPTH, [None] * DEPTH
    d_pw1, d_pw2, d_win, d_wout = [None] * n_conv, [None] * n_conv, [None] * n_ret, [None] * n_ret
    d_w1, d_w2 = [None] * DEPTH, [None] * DEPTH
    d_bpw1, d_bdw, d_lng, d_lnb, d_bpw2, d_dw = ([None] * n_conv for _ in range(6))
    d_gng, d_gnb = [None] * n_ret, [None] * n_ret

    def gn_parts(d):
        return jnp.transpose(d.reshape(RET_HEADS, NDEV, -1), (1, 0, 2))

    grad_handles = [None] * DEPTH
    token = None
    for i in reversed(range(DEPTH)):
        sh1, sc1, g1, sh2, sc2, g2 = mods[i]
        j = i // 2
        st = saved[i]
        mix_a, mix_b, w1_i, w2_i = weights[i]
        do, dg2 = dy, dgate
        dz = _mm_nt(do, w2_i, z=st["z"], out_dtype=BF16, name=f"mlp2_bwd_x{i}", after=token)
        d_w2[i] = _mm_tn(st["z"], do, relu2=True, name=f"mlp2_bwd_w{i}")
        dx, dsc2, dsh2, d_mlp_g[i], dy, dg1, dby = _mm_nt(dz, w1_i, norm=(st["x_mid"], vec(norm_mlp_g[i]), sc2, dx),
                                                          gated=(st["y_raw"], g1), name=f"mlp1_bwd_x{i}")
        d_w1[i] = _mm_tn(st["h2"], dz, col_shards=NDEV, name=f"mlp1_bwd_w{i}")
        mlp_groups = [[d_w1[i]], [d_w2[i].reshape(NDEV, 4 * D // NDEV, D)]]
        token = None
        if i == 0:
            mlp0_handle, token = _exchange_start(mlp_groups, gather=False, name="grads_start_mlp0")
            mlp_groups = []
        if i % 2 == 0:
            d_bpw2[j] = dby
            ds = _mm_nt(dy, mix_b, name=f"pw2_bwd_x{i}", after=token)
            ln_gb = (vec(conv_ln_g[j]), vec(conv_ln_b[j]))
            d_pw2[j] = _mm_tn(st["dwo"], dy, ln=ln_gb, name=f"pw2_bwd_w{i}")
            ddw, d_lng[j], d_lnb[j], d_bdw[j] = _ln_silu_bwd(st["dwo"], ds, *ln_gb, name=f"conv_ln_bwd{i}")
            du3, ddw_w, dbu = _conv_mid_bwd_dw(st["u3"], ddw.reshape(S, CH, LANES), dw3[j], name=f"conv_mid_bwd_dw{i}")
            d_dw[j], d_bpw1[j] = ddw_w.reshape(CONV_WIDTH, D), dbu.reshape(2, D)
            du = du3.reshape(S, 2 * D)
            d_pw1[j] = _mm_tn(st["h"], du, col_shards=NDEV, name=f"pw1_bwd_w{i}")
            mix_groups = [[d_pw1[j]], [d_pw2[j].reshape(NDEV, D // NDEV, D)],
                          [jnp.transpose(d_dw[j].reshape(CONV_WIDTH, NDEV, D // NDEV), (1, 0, 2))]]
            mix_in, mix_name = du, f"pw1_bwd_x{i}"
        else:
            dyg = _mm_nt(dy, mix_b, out_dtype=BF16, name=f"ret_out_bwd_x{i}")
            d_wout[j] = _mm_tn(st["yg"], dy, name=f"ret_out_bwd_w{i}")
            dq, dgt, dyr, d_gng[j], d_gnb[j] = _ret_bwd_q(st["proj"], st["yr"], dyg, tb, gng_full[j], gnb_full[j],
                                                          name=f"ret_bwd_q{i}")
            dproj = _ret_bwd_kv(st["proj"], dyr, dq, dgt, tb, name=f"ret_bwd_kv{i}")
            d_win[j] = _mm_tn(st["h"], dproj, col_shards=NDEV, name=f"ret_in_bwd_w{i}")
            mix_in, mix_name = dproj, f"ret_in_bwd_x{i}"
            mix_groups = [[d_win[j]], [d_wout[j].reshape(NDEV, 2 * D // NDEV, D)], [gn_parts(d_gng[j])],
                          [gn_parts(d_gnb[j])]]
        grad_handles[i], token = _exchange_start(mix_groups + mlp_groups, gather=False, name=f"grads_start{i}")
        gated = (saved[i - 1]["o_raw"], mods[i - 1][5]) if i > 0 else None
        outs = _mm_nt(mix_in, mix_a, norm=(st["x_in"], vec(norm_mix_g[i]), sc1, dx), gated=gated, name=mix_name, after=token)
        dx, dsc1, dsh1, d_mix_g[i] = outs[:4]
        if i > 0:
            dy, dgate = outs[4], outs[5]
        dmod_rows[i] = jnp.concatenate([dsh1, dsc1, dg1, dsh2, dsc2, dg2], axis=0)
    grad_x = dx.reshape(1, S, D)

    small_local = jnp.concatenate(dmod_rows + d_mix_g + d_mlp_g + d_bpw1 + d_bdw + d_lng + d_lnb + d_bpw2 + [d_final_g],
                                  axis=0)
    small_all = _exchange([[small_local]], gather=True, name="gather_small_grads")[0]

    def pack(src):
        return jnp.concatenate([src[n].reshape(-1, D) for n in SMALL], axis=0)[None]

    sm = _adamw(pack(W), pack(Mo), pack(Vo), [small_all], name="adamw_small")
    results = {}
    row = 0
    for n in SMALL:
        cnt = W[n].size // D
        results[n] = tuple(o[0, row:row + cnt].reshape(W[n].shape) for o in sm)
        row += cnt

    ns_ada = ada_w.shape[2]
    dmod_all = small_all[:, :6 * DEPTH, :].reshape(NDEV, DEPTH, 6 * D)
    dmod_cols = jnp.transpose(lax.dynamic_slice_in_dim(dmod_all, me * ns_ada, ns_ada, axis=2), (1, 0, 2))
    g_ada = _ada_bwd(c_all, dmod_cols, name="ada_bwd")
    flat_ada = (1, DEPTH * D, ns_ada)
    ada_res = _adamw(ada_w.reshape(flat_ada), m_ada_w.reshape(flat_ada), v_ada_w.reshape(flat_ada),
                     [g_ada.reshape(flat_ada)], name="adamw_ada_w")
    results["ada_w"] = tuple(o.reshape(ada_w.shape) for o in ada_res)

    def update(names, parts):
        for n in names:
            results[n] = _adamw(W[n], Mo[n], Vo[n], parts[n], name=f"adamw_{n}")

    got = {i: _exchange_wait(grad_handles[i], name=f"grads_wait{i}", after=dx) for i in range(DEPTH - 1, 0, -1)}
    ret_layers = [i for i in range(DEPTH) if i % 2 == 1]
    update(("ret_w_in", "ret_w_out", "ret_gn_g", "ret_gn_b"),
           dict(ret_w_in=[got[i][0] for i in ret_layers], ret_w_out=[got[i][1] for i in ret_layers],
                ret_gn_g=[got[i][2] for i in ret_layers], ret_gn_b=[got[i][3] for i in ret_layers]))
    got_mlp0 = _exchange_wait(mlp0_handle, name="grads_wait_mlp0", after=results["ret_w_in"][0])
    update(("mlp_w1", "mlp_w2"),
           dict(mlp_w1=[got_mlp0[0]] + [got[i][-2] for i in range(1, DEPTH)],
                mlp_w2=[got_mlp0[1]] + [got[i][-1] for i in range(1, DEPTH)]))
    got[0] = _exchange_wait(grad_handles[0], name="grads_wait0", after=results["mlp_w1"][0])
    conv_layers = [i for i in range(DEPTH) if i % 2 == 0]
    update(("conv_w_pw1", "conv_w_pw2", "conv_w_dw"),
           dict(conv_w_pw1=[got[i][0] for i in conv_layers], conv_w_pw2=[got[i][1] for i in conv_layers],
                conv_w_dw=[got[i][2] for i in conv_layers]))

    outs = [loss, grad_x]
    for kind in range(4):
        outs += [results[n][kind] for n in WEIGHTS]
    return tuple(outs)
```

```python
import functools

import jax
import jax.numpy as jnp
import numpy as np
from jax import lax
from jax.experimental import pallas as pl
from jax.experimental.pallas import tpu as pltpu

F32, BF16 = jnp.float32, jnp.bfloat16
AXES = ("x", "y", "c")
NDEV = 8
DEPTH = 4
EPS = 1e-6
CHUNK = 64
CONV_WIDTH = 31
HALO = 32
RET_HEADS = 4
RET_BLOCK = 256
ROPE_BASE = 10000.0
LANES = 128
ADAM_LR, ADAM_B1, ADAM_B2, ADAM_EPS, ADAM_WD, ADAM_STEP = 0.001, 0.9, 0.999, 1e-08, 0.01, 10
VMEM_LIMIT = 56 * 1024 * 1024
VMEM_BLOCK_BUDGET = 44 * 1024 * 1024
MESH = pl.DeviceIdType.MESH
NT_DIMS = (((1,), (1,)), ((), ()))
TN_DIMS = (((0,), (0,)), ((), ()))


def _call(body, *, name, out_shape, in_specs, out_specs, grid=(), scratch=(), sem=None, aliases=None):
    params = dict(vmem_limit_bytes=VMEM_LIMIT)
    if sem is not None:
        params["dimension_semantics"] = sem
    return pl.pallas_call(body, name=name, grid=grid, in_specs=in_specs, out_specs=out_specs, out_shape=out_shape,
                          scratch_shapes=list(scratch), input_output_aliases=aliases or {},
                          compiler_params=pltpu.CompilerParams(**params))


def _row_tile(rows, want):
    t = min(rows, want)
    while rows % t:
        t //= 2
    return t


def _sds(shape, dtype):
    return jax.ShapeDtypeStruct(tuple(shape), dtype)


def _sigmoid(v):
    return 1.0 / (1.0 + jnp.exp(-v))


def _exchange(groups, *, gather, name):
    flat = [a for g in groups for a in g]
    n_in = len(flat)
    out_shapes = []
    for g in groups:
        s = g[0].shape if gather else g[0].shape[1:]
        lead = (NDEV,) if len(g) == 1 else (NDEV, len(g))
        out_shapes.append(_sds(lead + tuple(s), g[0].dtype))
    n_g = len(groups)

    def body(*refs):
        ins, outs = refs[:n_in], refs[n_in:n_in + n_g]
        send_sems, recv_sems, loc_sems = refs[n_in + n_g:]
        x, y, c = lax.axis_index("x"), lax.axis_index("y"), lax.axis_index("c")
        me = 4 * x + 2 * y + c
        locs, k = [], 0
        for gi, g in enumerate(groups):
            for li in range(len(g)):
                src = ins[k] if gather else ins[k].at[me]
                dst = outs[gi].at[me] if len(g) == 1 else outs[gi].at[me, li]
                cp = pltpu.make_async_copy(src, dst, loc_sems.at[k])
                cp.start()
                locs.append(cp)
                k += 1
        k0 = 0
        for gi, g in enumerate(groups):
            for r in range(1, NDEV):
                px = 1 - x if r & 4 else x
                py = 1 - y if r & 2 else y
                pc = 1 - c if r & 1 else c
                peer = 4 * px + 2 * py + pc
                for li in range(len(g)):
                    src = ins[k0 + li] if gather else ins[k0 + li].at[peer]
                    dst = outs[gi].at[me] if len(g) == 1 else outs[gi].at[me, li]
                    pltpu.make_async_remote_copy(src_ref=src, dst_ref=dst, send_sem=send_sems.at[gi * (NDEV - 1) + r - 1],
                                                 recv_sem=recv_sems.at[gi * (NDEV - 1) + r - 1], device_id=(px, py, pc),
                                                 device_id_type=MESH).start()
            k0 += len(g)
        for gi, g in enumerate(groups):
            for r in range(1, NDEV):
                px = 1 - x if r & 4 else x
                py = 1 - y if r & 2 else y
                pc = 1 - c if r & 1 else c
                peer = 4 * px + 2 * py + pc
                slab = pltpu.make_async_remote_copy(src_ref=outs[gi].at[me], dst_ref=outs[gi].at[peer],
                                                    send_sem=send_sems.at[gi * (NDEV - 1) + r - 1], recv_sem=recv_sems.at[gi * (NDEV - 1) + r - 1],
                                                    device_id=(px, py, pc), device_id_type=MESH)
                slab.wait_send()
                slab.wait_recv()
        for cp in locs:
            cp.wait()

    hbm = pl.BlockSpec(memory_space=pltpu.HBM)
    outs = _call(body, name=name, out_shape=tuple(out_shapes), in_specs=[hbm] * n_in, out_specs=tuple([hbm] * n_g),
                 scratch=[pltpu.SemaphoreType.DMA((n_g * (NDEV - 1),)), pltpu.SemaphoreType.DMA((n_g * (NDEV - 1),)),
                          pltpu.SemaphoreType.DMA((n_in,))])(*flat)
    return list(outs)


def _peer_of(x, y, c, r):
    return (1 - x if r & 4 else x, 1 - y if r & 2 else y, 1 - c if r & 1 else c)


def _exchange_start(groups, *, gather, name, after=None):
    flat = [pltpu.with_memory_space_constraint(a, pltpu.HBM) for g in groups for a in g]
    n_in, n_g = len(flat), len(groups)
    land_shapes = []
    for g in groups:
        s = g[0].shape if gather else g[0].shape[1:]
        lead = (NDEV,) if len(g) == 1 else (NDEV, len(g))
        land_shapes.append((lead + tuple(s), g[0].dtype))
    lands = [pltpu.with_memory_space_constraint(lax.empty(s, d), pltpu.HBM) for s, d in land_shapes]
    n_after = 0 if after is None else 1

    def body(*refs):
        ins, land = refs[:n_in], refs[n_in:n_in + n_g]
        send_sems, recv_sems, loc_sems = refs[n_in + n_g + n_after:n_in + n_g + n_after + 3]
        token = refs[-1]
        x, y, c = lax.axis_index("x"), lax.axis_index("y"), lax.axis_index("c")
        me = 4 * x + 2 * y + c
        k = 0
        for gi, g in enumerate(groups):
            for li in range(len(g)):
                dst = land[gi].at[me] if len(g) == 1 else land[gi].at[me, li]
                pltpu.make_async_copy(ins[k] if gather else ins[k].at[me], dst, loc_sems.at[k]).start()
                k += 1
        k0 = 0
        for gi, g in enumerate(groups):
            for r in range(1, NDEV):
                px, py, pc = _peer_of(x, y, c, r)
                peer = 4 * px + 2 * py + pc
                for li in range(len(g)):
                    dst = land[gi].at[me] if len(g) == 1 else land[gi].at[me, li]
                    pltpu.make_async_remote_copy(src_ref=ins[k0 + li] if gather else ins[k0 + li].at[peer], dst_ref=dst,
                                                 send_sem=send_sems.at[gi * (NDEV - 1) + r - 1], recv_sem=recv_sems.at[gi * (NDEV - 1) + r - 1],
                                                 device_id=(px, py, pc), device_id_type=MESH).start()
            k0 += len(g)
        token[...] = jnp.zeros_like(token)

    hbm = pl.BlockSpec(memory_space=pltpu.HBM)
    sem = pl.BlockSpec(memory_space=pltpu.SEMAPHORE)
    args = flat + lands + ([after] if n_after else [])
    outs = pl.pallas_call(body, name=name,
        out_shape=(pltpu.SemaphoreType.DMA((n_g * (NDEV - 1),)), pltpu.SemaphoreType.DMA((n_g * (NDEV - 1),)),
                   pltpu.SemaphoreType.DMA((n_in,)), *[pltpu.HBM(a.shape, a.dtype) for a in flat],
                   *[pltpu.HBM(s, d) for s, d in land_shapes], _sds((8, LANES), F32)),
        in_specs=[hbm] * (n_in + n_g) + [pl.BlockSpec(memory_space=pl.ANY)] * n_after,
        out_specs=(sem, sem, sem, *[hbm] * (n_in + n_g), pl.BlockSpec(memory_space=pltpu.VMEM)),
        input_output_aliases={k: 3 + k for k in range(n_in + n_g)},
        compiler_params=pltpu.CompilerParams(has_side_effects=pltpu.SideEffectType.DATAFLOW_SIDE_EFFECTING))(*args)
    handle = dict(sems=outs[0:3], srcs=list(outs[3:3 + n_in]), lands=list(outs[3 + n_in:3 + n_in + n_g]),
                  sizes=[len(g) for g in groups], gather=gather)
    return handle, outs[-1]


def _exchange_wait(handle, *, name, after):
    srcs, lands, sizes, gather = handle["srcs"], handle["lands"], handle["sizes"], handle["gather"]
    n_in, n_g = len(srcs), len(lands)

    def body(*refs):
        ins, land = refs[:n_in], refs[n_in:n_in + n_g]
        send_sems, recv_sems, loc_sems = refs[n_in + n_g:n_in + n_g + 3]
        x, y, c = lax.axis_index("x"), lax.axis_index("y"), lax.axis_index("c")
        me = 4 * x + 2 * y + c
        for gi in range(n_g):
            for r in range(1, NDEV):
                px, py, pc = _peer_of(x, y, c, r)
                peer = 4 * px + 2 * py + pc
                slab = pltpu.make_async_remote_copy(src_ref=land[gi].at[me], dst_ref=land[gi].at[peer],
                                                    send_sem=send_sems.at[gi * (NDEV - 1) + r - 1], recv_sem=recv_sems.at[gi * (NDEV - 1) + r - 1],
                                                    device_id=(px, py, pc), device_id_type=MESH)
                slab.wait_send()
                slab.wait_recv()
        k = 0
        for gi in range(n_g):
            for li in range(sizes[gi]):
                dst = land[gi].at[me] if sizes[gi] == 1 else land[gi].at[me, li]
                pltpu.make_async_copy(ins[k] if gather else ins[k].at[me], dst, loc_sems.at[k]).wait()
                k += 1

    hbm = pl.BlockSpec(memory_space=pltpu.HBM)
    sem = pl.BlockSpec(memory_space=pltpu.SEMAPHORE)
    outs = pl.pallas_call(body, name=name, out_shape=tuple(pltpu.HBM(a.shape, a.dtype) for a in srcs + lands),
        in_specs=[hbm] * (n_in + n_g) + [sem] * 3 + [pl.BlockSpec(memory_space=pl.ANY)],
        out_specs=tuple([hbm] * (n_in + n_g)), input_output_aliases={k: k for k in range(n_in + n_g)},
        compiler_params=pltpu.CompilerParams(has_side_effects=pltpu.SideEffectType.DATAFLOW_SIDE_EFFECTING))(
            *srcs, *lands, *handle["sems"], after)
    return list(outs[n_in:])


def _gate_part(first, dx, y_ref, g_ref, dy_ref, dg_ref, db_ref):
    @pl.when(first)
    def _():
        dg_ref[...] = jnp.zeros_like(dg_ref)
        db_ref[...] = jnp.zeros_like(db_ref)

    dy = dx * g_ref[...]
    dy_ref[...] = dy.astype(BF16)
    dg_ref[...] += jnp.sum(dx * y_ref[...].astype(F32), axis=0, keepdims=True)
    db_ref[...] += jnp.sum(dy, axis=0, keepdims=True)


def _norm_bwd_part(first, dhv, x_ref, g_ref, sc_ref, dres_ref, dx_ref, dsc_ref, dsh_ref, dg_ref):
    @pl.when(first)
    def _():
        dsc_ref[...] = jnp.zeros_like(dsc_ref)
        dsh_ref[...] = jnp.zeros_like(dsh_ref)
        dg_ref[...] = jnp.zeros_like(dg_ref)

    xv = x_ref[...]
    r = lax.rsqrt(jnp.mean(xv * xv, axis=-1, keepdims=True) + EPS)
    xhat = xv * r
    gain_v = g_ref[...]
    dsc_ref[...] += jnp.sum(dhv * (xhat * gain_v), axis=0, keepdims=True)
    dsh_ref[...] += jnp.sum(dhv, axis=0, keepdims=True)
    dxn = dhv * (1.0 + sc_ref[...])
    dg_ref[...] += jnp.sum(dxn * xhat, axis=0, keepdims=True)
    dxhat = dxn * gain_v
    dx = dres_ref[...] + r * (dxhat - xhat * jnp.mean(dxhat * xhat, axis=-1, keepdims=True))
    dx_ref[...] = dx
    return dx


def _final_loss(x, gain, target, y_prev, gate_prev, *, name):
    S, D = x.shape
    tm = _row_tile(S, 512)

    def body(x_ref, g_ref, t_ref, y_ref, gp_ref, loss_ref, dx_ref, dg_ref, dy_ref, dgp_ref, dbp_ref):
        first = pl.program_id(0) == 0

        @pl.when(first)
        def _():
            loss_ref[...] = jnp.zeros_like(loss_ref)
            dg_ref[...] = jnp.zeros_like(dg_ref)

        xv = x_ref[...]
        r = lax.rsqrt(jnp.mean(xv * xv, axis=-1, keepdims=True) + EPS)
        xhat = xv * r
        gv = g_ref[...]
        err = xhat * gv - t_ref[...]
        row_loss = jnp.mean(err * err, axis=-1, keepdims=True)
        loss_ref[...] += 0.5 * jnp.sum(row_loss, axis=0, keepdims=True)
        dy = err * (1.0 / D)
        dg_ref[...] += jnp.sum(dy * xhat, axis=0, keepdims=True)
        dxhat = dy * gv
        dx = r * (dxhat - xhat * jnp.mean(dxhat * xhat, axis=-1, keepdims=True))
        dx_ref[...] = dx
        _gate_part(first, dx, y_ref, gp_ref, dy_ref, dgp_ref, dbp_ref)

    row = pl.BlockSpec((tm, D), lambda i: (i, 0))
    vec = pl.BlockSpec((1, D), lambda i: (0, 0))
    one = pl.BlockSpec((1, 1), lambda i: (0, 0))
    vsh = _sds((1, D), F32)
    return _call(body, name=name, grid=(S // tm,), in_specs=[row, vec, row, row, vec],
                 out_specs=(one, row, vec, row, vec, vec),
                 out_shape=(_sds((1, 1), F32), _sds((S, D), F32), vsh, _sds((S, D), BF16), vsh, vsh),
                 sem=("arbitrary",))(x, gain, target, y_prev, gate_prev)


def _pick_tm(M, bytes_per_row, fixed_bytes):
    for tm in (1024, 512, 256, 128):
        if M % tm == 0 and 2 * tm * bytes_per_row + fixed_bytes <= VMEM_BLOCK_BUDGET:
            return tm
    return _row_tile(M, 128)


def _mm_nn(a, w, *, name, bias=None, relu2=False, ln=None, norm=None, res=None, gate=None, out_dtype=BF16, after=None):
    M, K = a.shape
    col = w.ndim == 3
    if col:
        nsh, ns = w.shape[0], w.shape[2]
        w_spec = pl.BlockSpec((nsh, K, ns), lambda i: (0, 0, 0))
    else:
        nsh, ns = 1, w.shape[1]
        w_spec = pl.BlockSpec((K, ns), lambda i: (0, 0))
    N = nsh * ns
    residual = res is not None
    out_bytes = (4 + 4 + 2) if residual else jnp.dtype(out_dtype).itemsize
    tm = _pick_tm(M, K * a.dtype.itemsize + N * out_bytes + (K * 2 if norm is not None else 0), 2 * K * N * 2)

    def body(*refs):
        it = iter(refs)
        a_ref, w_ref = next(it), next(it)
        b_ref = next(it) if bias is not None else None
        lg_ref, lb_ref = (next(it), next(it)) if ln is not None else (None, None)
        ng_ref, nsc_ref, nsh_ref = (next(it), next(it), next(it)) if norm is not None else (None, None, None)
        res_ref, gate_ref = (next(it), next(it)) if residual else (None, None)
        if after is not None:
            next(it)
        out_ref = next(it)
        raw_ref = next(it) if residual else None
        av = a_ref[...]
        if relu2:
            av = jnp.square(jnp.maximum(av.astype(F32), 0.0))
        if ln is not None:
            av, _ = _ln_silu(av, lg_ref[...], lb_ref[...])
        if norm is not None:
            r = lax.rsqrt(jnp.mean(av * av, axis=-1, keepdims=True) + EPS)
            av = (av * r) * ng_ref[...] * (1.0 + nsc_ref[...]) + nsh_ref[...]
        ab = av.astype(BF16)
        if norm is not None:
            next(it)[...] = ab
        for d in range(nsh):
            cols = slice(d * ns, (d + 1) * ns)
            acc = jnp.dot(ab, w_ref[d] if col else w_ref[...], preferred_element_type=F32)
            if b_ref is not None:
                acc = acc + b_ref[:, cols]
            if residual:
                raw_ref[:, cols] = acc.astype(BF16)
                out_ref[:, cols] = res_ref[:, cols] + gate_ref[:, cols] * acc
            else:
                out_ref[:, cols] = acc.astype(out_dtype)

    tile = pl.BlockSpec((tm, N), lambda i: (i, 0))
    vec = pl.BlockSpec((1, N), lambda i: (0, 0))
    in_specs, args = [pl.BlockSpec((tm, K), lambda i: (i, 0)), w_spec], [a, w]
    if bias is not None:
        in_specs.append(vec)
        args.append(bias)
    if ln is not None:
        in_specs += [pl.BlockSpec((1, K), lambda i: (0, 0))] * 2
        args += list(ln)
    if norm is not None:
        in_specs += [pl.BlockSpec((1, K), lambda i: (0, 0))] * 3
        args += list(norm)
    if residual:
        in_specs += [tile, vec]
        args += [res, gate]
        out_specs = [tile, tile]
        out_shape = [_sds((M, N), F32), _sds((M, N), BF16)]
    else:
        out_specs = [tile]
        out_shape = [_sds((M, N), out_dtype)]
    if after is not None:
        in_specs.append(pl.BlockSpec(memory_space=pl.ANY))
        args.append(after)
    if norm is not None:
        out_specs.append(pl.BlockSpec((tm, K), lambda i: (i, 0)))
        out_shape.append(_sds((M, K), BF16))
    outs = _call(body, name=name, grid=(M // tm,), in_specs=in_specs, out_specs=tuple(out_specs), out_shape=tuple(out_shape),
                 sem=("parallel",))(*args)
    return outs[0] if len(outs) == 1 else outs


def _mm_nt(g, w, *, name, z=None, out_dtype=F32, after=None, norm=None, gated=None):
    M, N = g.shape
    col = w.ndim == 3
    if col:
        nsh, K, ns = w.shape
        w_spec = pl.BlockSpec((nsh, K, ns), lambda i: (0, 0, 0))
    else:
        K = w.shape[0]
        w_spec = pl.BlockSpec((K, N), lambda i: (0, 0))
    assert norm is None or col
    kc = min(K, 1024)
    obytes = jnp.dtype(out_dtype).itemsize
    row_bytes = N * g.dtype.itemsize + K * obytes + (K * 2 if z is not None else 0)
    if norm is not None:
        row_bytes += 2 * K * 4 + (K * 4 if gated is not None else 0)
    tm = _pick_tm(M, row_bytes, 2 * K * N * 2 + 512 * K * 4)

    def body(*refs):
        it = iter(refs)
        g_ref, w_ref = next(it), next(it)
        z_ref = next(it) if z is not None else None
        norm_in = [next(it) for _ in range(4)] if norm is not None else None
        gate_in = [next(it) for _ in range(2)] if gated is not None else None
        if after is not None:
            next(it)
        out_ref = next(it)
        if col:
            acc = None
            for d in range(nsh):
                part = lax.dot_general(g_ref[:, d * ns:(d + 1) * ns].astype(BF16), w_ref[d], NT_DIMS,
                                       preferred_element_type=F32)
                acc = part if acc is None else acc + part
            if norm is None:
                out_ref[...] = acc.astype(out_dtype)
            else:
                first = pl.program_id(0) == 0
                dx = _norm_bwd_part(first, acc, *norm_in, out_ref, next(it), next(it), next(it))
                if gated is not None:
                    _gate_part(first, dx, *gate_in, next(it), next(it), next(it))
        else:
            gb = g_ref[...].astype(BF16)
            for cki in range(K // kc):
                cols = slice(cki * kc, (cki + 1) * kc)
                part = lax.dot_general(gb, w_ref[cols, :], NT_DIMS, preferred_element_type=F32)
                if z_ref is not None:
                    part = part * (2.0 * jnp.maximum(z_ref[:, cols].astype(F32), 0.0))
                out_ref[:, cols] = part.astype(out_dtype)

    row = pl.BlockSpec((tm, K), lambda i: (i, 0))
    vec = pl.BlockSpec((1, K), lambda i: (0, 0))
    vsh = _sds((1, K), F32)
    in_specs, args = [pl.BlockSpec((tm, N), lambda i: (i, 0)), w_spec], [g, w]
    out_specs, out_shape = [row], [_sds((M, K), out_dtype)]
    if z is not None:
        in_specs.append(row)
        args.append(z)
    if norm is not None:
        x, gain, sc, dres = norm
        in_specs += [row, vec, vec, row]
        args += [x, gain, sc, dres]
        out_specs += [vec, vec, vec]
        out_shape += [vsh, vsh, vsh]
    if gated is not None:
        in_specs += [row, vec]
        args += list(gated)
        out_specs += [row, vec, vec]
        out_shape += [_sds((M, K), BF16), vsh, vsh]
    if after is not None:
        in_specs.append(pl.BlockSpec(memory_space=pl.ANY))
        args.append(after)
    outs = _call(body, name=name, grid=(M // tm,), in_specs=in_specs, out_specs=tuple(out_specs), out_shape=tuple(out_shape),
                 sem=("parallel",) if norm is None else ("arbitrary",))(*args)
    return outs[0] if len(outs) == 1 else outs


def _mm_tn(a, g, *, name, col_shards=None, relu2=False, ln=None):
    M, K = a.shape
    N = g.shape[1]
    acc_budget = 8 * 1024 * 1024
    tm = _row_tile(M, 512)
    nm = M // tm
    if col_shards:
        ns = N // col_shards
        spc = col_shards
        while spc > 1 and K * ns * spc * 4 > acc_budget:
            spc //= 2
        grid = (col_shards // spc, nm)
        a_spec = pl.BlockSpec((tm, K), lambda c, m: (m, 0))
        g_spec = pl.BlockSpec((tm, spc * ns), lambda c, m: (m, c))
        out_spec = pl.BlockSpec((spc, K, ns), lambda c, m: (c, 0, 0))
        out_shape = _sds((col_shards, K, ns), BF16)
        acc_shape = (K, spc * ns)
    else:
        tk = K
        while tk > 128 and tk * N * 4 > acc_budget:
            tk //= 2
        grid = (K // tk, nm)
        a_spec = pl.BlockSpec((tm, tk), lambda c, m: (m, c))
        g_spec = pl.BlockSpec((tm, N), lambda c, m: (m, 0))
        out_spec = pl.BlockSpec((tk, N), lambda c, m: (c, 0))
        out_shape = _sds((K, N), BF16)
        acc_shape = (tk, N)
        assert ln is None or tk == K
    in_specs, args = [a_spec, g_spec], [a, g]
    if ln is not None:
        in_specs += [pl.BlockSpec((1, K), lambda c, m: (0, 0))] * 2
        args += list(ln)

    def body(a_ref, g_ref, *rest):
        out_ref, acc_ref = rest[-2:]
        m = pl.program_id(1)

        @pl.when(m == 0)
        def _():
            acc_ref[...] = jnp.zeros_like(acc_ref)

        av = a_ref[...]
        if relu2:
            av = jnp.square(jnp.maximum(av.astype(F32), 0.0))
        if ln is not None:
            av, _ = _ln_silu(av, rest[0][...], rest[1][...])
        acc_ref[...] += lax.dot_general(av.astype(BF16), g_ref[...].astype(BF16), TN_DIMS, preferred_element_type=F32)

        @pl.when(m == nm - 1)
        def _():
            if col_shards:
                for s in range(spc):
                    out_ref[s] = acc_ref[:, s * ns:(s + 1) * ns].astype(BF16)
            else:
                out_ref[...] = acc_ref[...].astype(BF16)

    return _call(body, name=name, grid=grid, in_specs=in_specs, out_specs=out_spec, out_shape=out_shape,
                 scratch=[pltpu.VMEM(acc_shape, F32)], sem=("parallel", "arbitrary"))(*args)


CONV_TILE = 256
CONV_GROUP = 32


def _glu(u, ch):
    return u[:, :ch] * _sigmoid(u[:, ch:])


def _fill_glu(buf, u_ref, uh_ref, ch, tile):
    first = pl.program_id(0) == 0
    buf[0:HALO] = jnp.where(first, 0.0, _glu(uh_ref[...], ch))
    buf[HALO:HALO + tile] = _glu(u_ref[...], ch)


CONV_SUB = 4


def _conv_specs(S, ch, tile):
    per = tile // HALO
    u_spec = pl.BlockSpec((tile, 2 * ch, LANES), lambda i: (i, 0, 0))
    uh_spec = pl.BlockSpec((HALO, 2 * ch, LANES), lambda i: (jnp.maximum(i * per - 1, 0), 0, 0))
    x_spec = pl.BlockSpec((tile, ch, LANES), lambda i: (i, 0, 0))
    xn_spec = pl.BlockSpec((HALO, ch, LANES), lambda i: (jnp.minimum((i + 1) * per, S // HALO - 1), 0, 0))
    w_spec = pl.BlockSpec((CONV_WIDTH, ch, LANES), lambda i: (0, 0, 0))
    v_spec = pl.BlockSpec((1, ch, LANES), lambda i: (0, 0, 0))
    return u_spec, uh_spec, x_spec, xn_spec, w_spec, v_spec


def _conv_mid_fwd(u3, w3, bdw3, *, name):
    S, ch2, _ = u3.shape
    ch = ch2 // 2
    tile = _row_tile(S, CONV_TILE)
    sub = _row_tile(tile, CONV_SUB)
    u_spec, uh_spec, x_spec, _, w_spec, v_spec = _conv_specs(S, ch, tile)

    def body(u_ref, uh_ref, w_ref, b_ref, o_ref, buf):
        _fill_glu(buf, u_ref, uh_ref, ch, tile)

        def step(q, carry):
            acc = [b_ref[...], None]
            for k in range(CONV_WIDTH):
                term = buf[pl.ds(q * sub + (HALO - CONV_WIDTH + 1 + k), sub)] * w_ref[k]
                acc[k % 2] = term if acc[k % 2] is None else acc[k % 2] + term
            o_ref[pl.ds(q * sub, sub)] = acc[0] + acc[1]
            return carry

        lax.fori_loop(0, tile // sub, step, 0)

    return _call(body, name=name, grid=(S // tile,), in_specs=[u_spec, uh_spec, w_spec, v_spec], out_specs=x_spec,
                 out_shape=_sds((S, ch, LANES), F32), scratch=[pltpu.VMEM((tile + HALO, ch, LANES), F32)],
                 sem=("parallel",))(u3, u3, w3, bdw3)


def _ln_silu(v, gv, bv):
    mu = jnp.mean(v, axis=-1, keepdims=True)
    cen = v - mu
    rstd = lax.rsqrt(jnp.mean(cen * cen, axis=-1, keepdims=True) + EPS)
    nrm = cen * rstd
    ln = nrm * gv + bv
    sg = _sigmoid(ln)
    return ln * sg, (nrm, rstd, ln, sg)


def _ln_silu_bwd(dwo, ds, lng, lnb, *, name):
    S, D = dwo.shape
    tm = _row_tile(S, 512)

    def body(v_ref, ds_ref, g_ref, b_ref, ddw_ref, dg_ref, db_ref, dbdw_ref):
        @pl.when(pl.program_id(0) == 0)
        def _():
            dg_ref[...] = jnp.zeros_like(dg_ref)
            db_ref[...] = jnp.zeros_like(db_ref)
            dbdw_ref[...] = jnp.zeros_like(dbdw_ref)

        gv = g_ref[...]
        _, (nrm, rstd, ln, sg) = _ln_silu(v_ref[...], gv, b_ref[...])
        dln = ds_ref[...] * (sg * (1.0 + ln * (1.0 - sg)))
        dg_ref[...] += jnp.sum(dln * nrm, axis=0, keepdims=True)
        db_ref[...] += jnp.sum(dln, axis=0, keepdims=True)
        dn = dln * gv
        ddw = rstd * (dn - jnp.mean(dn, axis=-1, keepdims=True) - nrm * jnp.mean(dn * nrm, axis=-1, keepdims=True))
        dbdw_ref[...] += jnp.sum(ddw, axis=0, keepdims=True)
        ddw_ref[...] = ddw

    row = pl.BlockSpec((tm, D), lambda i: (i, 0))
    vec = pl.BlockSpec((1, D), lambda i: (0, 0))
    vsh = _sds((1, D), F32)
    return _call(body, name=name, grid=(S // tm,), in_specs=[row, row, vec, vec], out_specs=(row, vec, vec, vec),
                 out_shape=(_sds((S, D), F32), vsh, vsh, vsh), sem=("arbitrary",))(dwo, ds, lng, lnb)


def _conv_mid_bwd_dw(u3, ddw3, w3, *, name):
    S, ch2, _ = u3.shape
    ch = ch2 // 2
    tile = _row_tile(S, CONV_TILE)
    rows = _row_tile(tile, CONV_GROUP)
    sub = _row_tile(rows, CONV_SUB)
    last = S // tile - 1
    u_spec, uh_spec, x_spec, xn_spec, w_spec, _ = _conv_specs(S, ch, tile)
    b_spec = pl.BlockSpec((1, 2 * ch, LANES), lambda i: (0, 0, 0))

    def body(u_ref, uh_ref, d_ref, dn_ref, w_ref, du_ref, dw_ref, db_ref, gbuf, dbuf, stage):
        @pl.when(pl.program_id(0) == 0)
        def _():
            dw_ref[...] = jnp.zeros_like(dw_ref)
            db_ref[...] = jnp.zeros_like(db_ref)

        _fill_glu(gbuf, u_ref, uh_ref, ch, tile)
        dbuf[0:tile] = d_ref[...]
        dbuf[tile:tile + HALO] = jnp.where(pl.program_id(0) == last, 0.0, dn_ref[...])

        def group(r, carry):
            t0 = r * rows
            def step(q, c):
                s0 = t0 + q * sub
                ddw = dbuf[pl.ds(s0, sub)]
                acc = [None, None]
                for k in range(CONV_WIDTH):
                    term = dbuf[pl.ds(s0 + (CONV_WIDTH - 1 - k), sub)] * w_ref[k]
                    acc[k % 2] = term if acc[k % 2] is None else acc[k % 2] + term
                    dw_ref[k] += jnp.sum(ddw * gbuf[pl.ds(s0 + (HALO - CONV_WIDTH + 1 + k), sub)], axis=0)
                stage[pl.ds(q * sub, sub)] = acc[0] + acc[1]
                return c

            lax.fori_loop(0, rows // sub, step, 0)
            dglu = stage[...]
            uv = u_ref[pl.ds(t0, rows)]
            av, sg = uv[:, :ch], _sigmoid(uv[:, ch:])
            da = dglu * sg
            dg = dglu * av * sg * (1.0 - sg)
            du_ref[pl.ds(t0, rows), 0:ch] = da
            du_ref[pl.ds(t0, rows), ch:2 * ch] = dg
            db_ref[:, 0:ch] += jnp.sum(da, axis=0, keepdims=True)
            db_ref[:, ch:2 * ch] += jnp.sum(dg, axis=0, keepdims=True)
            return carry

        lax.fori_loop(0, tile // rows, group, 0)

    return _call(body, name=name, grid=(S // tile,), in_specs=[u_spec, uh_spec, x_spec, xn_spec, w_spec],
                 out_specs=(u_spec, w_spec, b_spec),
                 out_shape=(_sds((S, 2 * ch, LANES), F32), _sds((CONV_WIDTH, ch, LANES), F32), _sds((1, 2 * ch, LANES), F32)),
                 scratch=[pltpu.VMEM((tile + HALO, ch, LANES), F32), pltpu.VMEM((tile + HALO, ch, LANES), F32),
                          pltpu.VMEM((rows, ch, LANES), F32)],
                 sem=("arbitrary",))(u3, u3, ddw3, ddw3, w3)


def _ret_tables(S, dk):
    f32 = np.float32
    B = min(RET_BLOCK, S)
    lg = np.log(f32(1.0) - f32(2.0) ** (f32(-5.0) - np.arange(RET_HEADS, dtype=f32)))
    idx = np.arange(B, dtype=f32)
    diff = idx[:, None] - idx[None, :]
    cq, ck = (np.arange(B) // CHUNK)[:, None], (np.arange(B) // CHUNK)[None, :]
    dist = np.where(cq == ck, np.abs(diff), diff)
    mask = np.where(ck <= cq, np.exp(lg[:, None, None] * dist[None]), f32(0.0)).astype(f32)
    xi = np.exp(lg[:, None] * (idx + f32(1.0)))[..., None].astype(f32)
    zeta = np.exp(lg[:, None] * (f32(B - 1.0) - idx))[..., None].astype(f32)
    gam = np.broadcast_to(np.exp(lg * f32(B))[:, None, None], (RET_HEADS, 8, LANES)).astype(f32)
    pos = np.arange(S, dtype=f32)
    inv = (f32(ROPE_BASE) ** (-np.arange(0, dk, 2, dtype=f32) / f32(dk))).astype(f32)
    ang = (pos[:, None] * inv[None, :]).astype(f32)
    tb = dict(mask=mask, xi=xi, zeta=zeta, gam=gam, cos=np.cos(ang).astype(f32), sin=np.sin(ang).astype(f32))
    return dict(B=B, **{k: jnp.asarray(v) for k, v in tb.items()})


def _rope(v, cs, sn):
    half = v.shape[1] // 2
    v1, v2 = v[:, :half], v[:, half:]
    return jnp.concatenate([v1 * cs - v2 * sn, v2 * cs + v1 * sn], axis=-1)


def _rope_t(d, cs, sn):
    half = d.shape[1] // 2
    d1, d2 = d[:, :half], d[:, half:]
    return jnp.concatenate([d1 * cs + d2 * sn, d2 * cs - d1 * sn], axis=-1)


def _dot(a, b):
    return jnp.dot(a.astype(BF16), b.astype(BF16), preferred_element_type=F32)


def _dot_nt(a, b):
    return lax.dot_general(a.astype(BF16), b.astype(BF16), NT_DIMS, preferred_element_type=F32)


def _dot_tn(a, b):
    return lax.dot_general(a.astype(BF16), b.astype(BF16), TN_DIMS, preferred_element_type=F32)


def _ret_specs(S, D, B, RB, reverse):
    dk, dv = D // RET_HEADS, 2 * D // RET_HEADS
    nb = S // RB
    blk = (lambda ib: nb - 1 - ib) if reverse else (lambda ib: ib)
    q = pl.BlockSpec((RB, dk), lambda h, ib: (blk(ib), h))
    k = pl.BlockSpec((RB, dk), lambda h, ib: (blk(ib), RET_HEADS + h))
    v = pl.BlockSpec((RB, dv), lambda h, ib: (blk(ib), RET_HEADS + h))
    gate = pl.BlockSpec((RB, dv), lambda h, ib: (blk(ib), 2 * RET_HEADS + h))
    yv = pl.BlockSpec((RB, dv), lambda h, ib: (blk(ib), h))
    rope = pl.BlockSpec((RB, dk // 2), lambda h, ib: (blk(ib), 0))
    mask = pl.BlockSpec((None, B, B), lambda h, ib: (h, 0, 0))
    dec = pl.BlockSpec((None, B, 1), lambda h, ib: (h, 0, 0))
    gam = pl.BlockSpec((None, 8, LANES), lambda h, ib: (h, 0, 0))
    gn = pl.BlockSpec((1, dv), lambda h, ib: (0, h))
    return dict(q=q, k=k, v=v, gate=gate, yv=yv, rope=rope, mask=mask, dec=dec, gam=gam, gn=gn)


def _group_norm(yr, gv, bv):
    mu = jnp.mean(yr, axis=-1, keepdims=True)
    cen = yr - mu
    rstd = lax.rsqrt(jnp.mean(cen * cen, axis=-1, keepdims=True) + EPS)
    nrm = cen * rstd
    return nrm, rstd, nrm * gv + bv


def _ret_fwd(proj, tb, gng, gnb, *, name):
    S, D = proj.shape[0], proj.shape[1] // 6
    dk, dv = D // RET_HEADS, 2 * D // RET_HEADS
    B = tb["B"]
    RB = _row_tile(S, 2 * B)
    nsub = RB // B
    sp = _ret_specs(S, D, B, RB, False)
    scale = dk ** -0.5

    def body(q_ref, k_ref, v_ref, gt_ref, cos_ref, sin_ref, mask_ref, xi_ref, zeta_ref, gam_ref, gng_ref, gnb_ref,
             yr_ref, yg_ref, qr_ref, kr_ref, state):
        @pl.when(pl.program_id(1) == 0)
        def _():
            state[...] = jnp.zeros_like(state)

        for sb in range(nsub):
            rows = slice(sb * B, (sb + 1) * B)
            cs, sn = cos_ref[rows, :], sin_ref[rows, :]
            q = _rope(q_ref[rows, :].astype(F32), cs, sn)
            k = _rope(k_ref[rows, :].astype(F32), cs, sn) * scale
            qr_ref[rows, :] = q.astype(BF16)
            kr_ref[rows, :] = k.astype(BF16)
            vb = v_ref[rows, :]
            p = _dot_nt(q, k) * mask_ref[...]
            st = state[...]
            yr = _dot(p, vb) + _dot(q * xi_ref[...], st)
            state[...] = st * gam_ref[0:1, 0:1] + _dot_tn(k * zeta_ref[...], vb)
            _, _, gn = _group_norm(yr, gng_ref[...], gnb_ref[...])
            gt = gt_ref[rows, :].astype(F32)
            yr_ref[rows, :] = yr.astype(BF16)
            yg_ref[rows, :] = (gt * _sigmoid(gt) * gn).astype(BF16)

    return _call(body, name=name, grid=(RET_HEADS, S // RB),
                 in_specs=[sp["q"], sp["k"], sp["v"], sp["gate"], sp["rope"], sp["rope"], sp["mask"], sp["dec"], sp["dec"],
                           sp["gam"], sp["gn"], sp["gn"]],
                 out_specs=(sp["yv"], sp["yv"], sp["q"], sp["q"]),
                 out_shape=(_sds((S, 2 * D), BF16), _sds((S, 2 * D), BF16), _sds((S, D), BF16), _sds((S, D), BF16)),
                 scratch=[pltpu.VMEM((dk, dv), F32)], sem=("parallel", "arbitrary"))(
                     proj, proj, proj, proj, tb["cos"], tb["sin"], tb["mask"], tb["xi"], tb["zeta"], tb["gam"], gng, gnb)


def _ret_bwd_q(proj, kr, yr, dyg, tb, gng, gnb, *, name):
    S, D = proj.shape[0], proj.shape[1] // 6
    dk, dv = D // RET_HEADS, 2 * D // RET_HEADS
    B = tb["B"]
    RB = _row_tile(S, 2 * B)
    nsub = RB // B
    sp = _ret_specs(S, D, B, RB, False)

    def body(k_ref, v_ref, gt_ref, yr_ref, dyg_ref, cos_ref, sin_ref, mask_ref, xi_ref, zeta_ref, gam_ref,
             gng_ref, gnb_ref, dq_ref, dgt_ref, dyr_ref, dgg_ref, dgb_ref, state):
        @pl.when(pl.program_id(1) == 0)
        def _():
            state[...] = jnp.zeros_like(state)
            dgg_ref[...] = jnp.zeros_like(dgg_ref)
            dgb_ref[...] = jnp.zeros_like(dgb_ref)

        for sb in range(nsub):
            rows = slice(sb * B, (sb + 1) * B)
            cs, sn = cos_ref[rows, :], sin_ref[rows, :]
            k = k_ref[rows, :]
            vb = v_ref[rows, :]
            gv = gng_ref[...]
            nrm, rstd, gn = _group_norm(yr_ref[rows, :].astype(F32), gv, gnb_ref[...])
            gt = gt_ref[rows, :].astype(F32)
            sg = _sigmoid(gt)
            dyg = dyg_ref[rows, :].astype(F32)
            dgt_ref[rows, :] = (dyg * gn * (sg * (1.0 + gt * (1.0 - sg)))).astype(BF16)
            dgn = dyg * (gt * sg)
            dgg_ref[...] += jnp.sum(dgn * nrm, axis=0, keepdims=True)
            dgb_ref[...] += jnp.sum(dgn, axis=0, keepdims=True)
            dn = dgn * gv
            dyr = rstd * (dn - jnp.mean(dn, axis=-1, keepdims=True) - nrm * jnp.mean(dn * nrm, axis=-1, keepdims=True))
            dyr_ref[rows, :] = dyr.astype(BF16)
            dp = _dot_nt(dyr, vb) * mask_ref[...]
            st = state[...]
            dq = _dot(dp, k) + _dot_nt(dyr, st) * xi_ref[...]
            dq_ref[rows, :] = _rope_t(dq, cs, sn).astype(BF16)
            state[...] = st * gam_ref[0:1, 0:1] + _dot_tn(k.astype(F32) * zeta_ref[...], vb)

    return _call(body, name=name, grid=(RET_HEADS, S // RB),
                 in_specs=[sp["q"], sp["v"], sp["gate"], sp["yv"], sp["yv"], sp["rope"], sp["rope"], sp["mask"],
                           sp["dec"], sp["dec"], sp["gam"], sp["gn"], sp["gn"]],
                 out_specs=(sp["q"], sp["yv"], sp["yv"], sp["gn"], sp["gn"]),
                 out_shape=(_sds((S, D), BF16), _sds((S, 2 * D), BF16), _sds((S, 2 * D), BF16), _sds((1, 2 * D), F32),
                            _sds((1, 2 * D), F32)),
                 scratch=[pltpu.VMEM((dk, dv), F32)], sem=("parallel", "arbitrary"))(
                     kr, proj, proj, yr, dyg, tb["cos"], tb["sin"], tb["mask"], tb["xi"], tb["zeta"], tb["gam"], gng, gnb)


def _ret_bwd_kv(proj, qr, kr, dyr, dq, dgt, tb, *, name):
    S, D = proj.shape[0], proj.shape[1] // 6
    dk, dv = D // RET_HEADS, 2 * D // RET_HEADS
    B = tb["B"]
    RB = _row_tile(S, 2 * B)
    nsub = RB // B
    nb = S // RB
    scale = dk ** -0.5

    def body(v_ref, qr_ref, kr_ref, dyr_ref, dq_ref, dgt_ref, cos_ref, sin_ref, mask_ref, xi_ref, zeta_ref, gam_ref, out_ref,
             dstate):
        @pl.when(pl.program_id(0) == 0)
        def _():
            dstate[...] = jnp.zeros_like(dstate)

        out_ref[:, 0:D] = dq_ref[...]
        out_ref[:, 4 * D:6 * D] = dgt_ref[...]
        for sb in reversed(range(nsub)):
            rows = slice(sb * B, (sb + 1) * B)
            cs, sn = cos_ref[rows, :], sin_ref[rows, :]
            for h in range(RET_HEADS):
                kcols = slice(D + h * dk, D + (h + 1) * dk)
                vcols = slice(2 * D + h * dv, 2 * D + (h + 1) * dv)
                q = qr_ref[rows, h * dk:(h + 1) * dk]
                k = kr_ref[rows, h * dk:(h + 1) * dk]
                vb = v_ref[rows, h * dv:(h + 1) * dv]
                dyr_h = dyr_ref[rows, h * dv:(h + 1) * dv]
                mk = mask_ref[h]
                p = _dot_nt(q, k) * mk
                dp = _dot_nt(dyr_h, vb) * mk
                ds = dstate[h]
                zt = zeta_ref[h]
                dkr = _dot_tn(dp, q) + _dot_nt(vb, ds) * zt
                out_ref[rows, kcols] = _rope_t(dkr * scale, cs, sn).astype(BF16)
                out_ref[rows, vcols] = (_dot_tn(p, dyr_h) + _dot(k.astype(F32) * zt, ds)).astype(BF16)
                dstate[h] = ds * gam_ref[h, 0:1, 0:1] + _dot_tn(q.astype(F32) * xi_ref[h], dyr_h)

    def rev(width):
        return pl.BlockSpec((RB, width), lambda ib: (nb - 1 - ib, 0))

    def whole(a):
        return pl.BlockSpec(a.shape, lambda ib: (0,) * a.ndim)

    return _call(body, name=name, grid=(nb,),
                 in_specs=[pl.BlockSpec((RB, 2 * D), lambda ib: (nb - 1 - ib, 1)), rev(D), rev(D), rev(2 * D), rev(D), rev(2 * D),
                           rev(dk // 2), rev(dk // 2), whole(tb["mask"]), whole(tb["xi"]), whole(tb["zeta"]), whole(tb["gam"])],
                 out_specs=rev(6 * D), out_shape=_sds((S, 6 * D), BF16), scratch=[pltpu.VMEM((RET_HEADS, dk, dv), F32)],
                 sem=("arbitrary",))(proj, qr, kr, dyr, dq, dgt, tb["cos"], tb["sin"], tb["mask"], tb["xi"], tb["zeta"],
                                     tb["gam"])


def _ada_fwd(c_all, ada_w, *, name):
    L, D, ns = ada_w.shape

    def body(c_ref, w_ref, out_ref):
        cv = c_ref[...]
        cond = cv * _sigmoid(cv)
        out_ref[...] = jnp.dot(cond.astype(BF16), w_ref[...].astype(BF16), preferred_element_type=F32)

    return _call(body, name=name, grid=(L,), in_specs=[pl.BlockSpec((NDEV, D), lambda l: (0, 0)),
                                                      pl.BlockSpec((None, D, ns), lambda l: (l, 0, 0))],
                 out_specs=pl.BlockSpec((None, NDEV, ns), lambda l: (l, 0, 0)), out_shape=_sds((L, NDEV, ns), F32),
                 sem=("parallel",))(c_all, ada_w)


def _ada_bwd(c_all, dmod_cols, *, name):
    L, _, ns = dmod_cols.shape
    D = c_all.shape[1]

    def body(c_ref, d_ref, out_ref):
        cv = c_ref[...]
        cond = cv * _sigmoid(cv)
        out_ref[...] = lax.dot_general(cond.astype(BF16), d_ref[...].astype(BF16), TN_DIMS, preferred_element_type=F32)

    return _call(body, name=name, grid=(L,), in_specs=[pl.BlockSpec((NDEV, D), lambda l: (0, 0)),
                                                      pl.BlockSpec((None, NDEV, ns), lambda l: (l, 0, 0))],
                 out_specs=pl.BlockSpec((None, D, ns), lambda l: (l, 0, 0)), out_shape=_sds((L, D, ns), F32),
                 sem=("parallel",))(c_all, dmod_cols)


def _adamw(w, m, v, parts, *, name):
    shape = w.shape
    L, cols = len(parts), shape[-1]
    rows = w.size // (cols * L)
    n = parts[0].shape[0]
    tr = rows
    for cand in (256, 128, 64, 32, 16, 8):
        if rows % cand == 0:
            tr = cand
            break
    c1 = 1.0 - ADAM_B1 ** ADAM_STEP
    c2 = 1.0 - ADAM_B2 ** ADAM_STEP

    def body(w_ref, m_ref, v_ref, *rest):
        p_refs = rest[:L]
        g_ref, d_ref, m2_ref, v2_ref = rest[L:]
        layer = pl.program_id(0)
        for l in range(L):
            @pl.when(layer == l)
            def _(p_ref=p_refs[l]):
                g = p_ref[0].astype(F32)
                for i in range(1, n):
                    g = g + p_ref[i].astype(F32)
                m2 = ADAM_B1 * m_ref[...] + (1.0 - ADAM_B1) * g
                v2 = ADAM_B2 * v_ref[...] + (1.0 - ADAM_B2) * (g * g)
                g_ref[...] = g
                m2_ref[...] = m2
                v2_ref[...] = v2
                d_ref[...] = -ADAM_LR * ((m2 / c1) / (jnp.sqrt(v2 / c2) + ADAM_EPS) + ADAM_WD * w_ref[...])

    mat = pl.BlockSpec((None, tr, cols), lambda l, i: (l, i, 0))

    def part_spec(k):
        return pl.BlockSpec((n, tr, cols), lambda l, i: (0, jnp.where(l == k, i, 0), 0))

    outs = _call(body, name=name, grid=(L, rows // tr), in_specs=[mat, mat, mat] + [part_spec(k) for k in range(L)],
                 out_specs=(mat, mat, mat, mat), out_shape=tuple(_sds((L, rows, cols), F32) for _ in range(4)),
                 sem=("parallel", "parallel"))(w.reshape(L, rows, cols), m.reshape(L, rows, cols), v.reshape(L, rows, cols),
                                               *[p.reshape(n, rows, cols) for p in parts])
    return tuple(o.reshape(shape) for o in outs)


SMALL = ("ada_b", "norm_mix_g", "norm_mlp_g", "conv_b_pw1", "conv_b_dw", "conv_ln_g", "conv_ln_b", "conv_b_pw2",
         "final_norm_g")
WEIGHTS = ("ada_w", "ada_b", "norm_mix_g", "norm_mlp_g", "conv_w_pw1", "conv_b_pw1", "conv_w_dw", "conv_b_dw", "conv_ln_g",
           "conv_ln_b", "conv_w_pw2", "conv_b_pw2", "ret_w_in", "ret_gn_g", "ret_gn_b", "ret_w_out", "mlp_w1", "mlp_w2",
           "final_norm_g")


def kernel(x, c, ada_w, ada_b, norm_mix_g, norm_mlp_g, conv_w_pw1, conv_b_pw1, conv_w_dw, conv_b_dw, conv_ln_g, conv_ln_b, conv_w_pw2, conv_b_pw2, ret_w_in, ret_gn_g, ret_gn_b, ret_w_out, mlp_w1, mlp_w2, final_norm_g, loss_target, m_ada_w, m_ada_b, m_norm_mix_g, m_norm_mlp_g, m_conv_w_pw1, m_conv_b_pw1, m_conv_w_dw, m_conv_b_dw, m_conv_ln_g, m_conv_ln_b, m_conv_w_pw2, m_conv_b_pw2, m_ret_w_in, m_ret_gn_g, m_ret_gn_b, m_ret_w_out, m_mlp_w1, m_mlp_w2, m_final_norm_g, v_ada_w, v_ada_b, v_norm_mix_g, v_norm_mlp_g, v_conv_w_pw1, v_conv_b_pw1, v_conv_w_dw, v_conv_b_dw, v_conv_ln_g, v_conv_ln_b, v_conv_w_pw2, v_conv_b_pw2, v_ret_w_in, v_ret_gn_g, v_ret_gn_b, v_ret_w_out, v_mlp_w1, v_mlp_w2, v_final_norm_g):
    W = dict(ada_w=ada_w, ada_b=ada_b, norm_mix_g=norm_mix_g, norm_mlp_g=norm_mlp_g, conv_w_pw1=conv_w_pw1,
             conv_b_pw1=conv_b_pw1, conv_w_dw=conv_w_dw, conv_b_dw=conv_b_dw, conv_ln_g=conv_ln_g, conv_ln_b=conv_ln_b,
             conv_w_pw2=conv_w_pw2, conv_b_pw2=conv_b_pw2, ret_w_in=ret_w_in, ret_gn_g=ret_gn_g, ret_gn_b=ret_gn_b,
             ret_w_out=ret_w_out, mlp_w1=mlp_w1, mlp_w2=mlp_w2, final_norm_g=final_norm_g)
    Mo = dict(ada_w=m_ada_w, ada_b=m_ada_b, norm_mix_g=m_norm_mix_g, norm_mlp_g=m_norm_mlp_g, conv_w_pw1=m_conv_w_pw1,
              conv_b_pw1=m_conv_b_pw1, conv_w_dw=m_conv_w_dw, conv_b_dw=m_conv_b_dw, conv_ln_g=m_conv_ln_g,
              conv_ln_b=m_conv_ln_b, conv_w_pw2=m_conv_w_pw2, conv_b_pw2=m_conv_b_pw2, ret_w_in=m_ret_w_in,
              ret_gn_g=m_ret_gn_g, ret_gn_b=m_ret_gn_b, ret_w_out=m_ret_w_out, mlp_w1=m_mlp_w1, mlp_w2=m_mlp_w2,
              final_norm_g=m_final_norm_g)
    Vo = dict(ada_w=v_ada_w, ada_b=v_ada_b, norm_mix_g=v_norm_mix_g, norm_mlp_g=v_norm_mlp_g, conv_w_pw1=v_conv_w_pw1,
              conv_b_pw1=v_conv_b_pw1, conv_w_dw=v_conv_w_dw, conv_b_dw=v_conv_b_dw, conv_ln_g=v_conv_ln_g,
              conv_ln_b=v_conv_ln_b, conv_w_pw2=v_conv_w_pw2, conv_b_pw2=v_conv_b_pw2, ret_w_in=v_ret_w_in,
              ret_gn_g=v_ret_gn_g, ret_gn_b=v_ret_gn_b, ret_w_out=v_ret_w_out, mlp_w1=v_mlp_w1, mlp_w2=v_mlp_w2,
              final_norm_g=v_final_norm_g)

    S, D = x.shape[1], x.shape[2]
    CH = D // LANES
    n_conv, n_ret = conv_w_pw1.shape[0], ret_w_in.shape[0]
    me = 4 * lax.axis_index("x") + 2 * lax.axis_index("y") + lax.axis_index("c")
    xs = x.reshape(S, D)
    target = loss_target.reshape(S, D)

    def mixer_shards(i):
        j = i // 2
        if i % 2 == 0:
            return [[conv_w_pw1[j].astype(BF16)], [conv_w_pw2[j].astype(BF16)]]
        return [[ret_w_in[j].astype(BF16)], [ret_w_out[j].astype(BF16)]]

    def mlp_shards(i):
        return [[mlp_w1[i].astype(BF16)], [mlp_w2[i].astype(BF16)]]

    def mixer_weights(i, got):
        return got[0], got[1].reshape(-1, D)

    def mlp_weights(got):
        return got[0], got[1].reshape(4 * D, D)

    first_handle, _ = _exchange_start(mixer_shards(0)[:1], gather=True, name="gather_start_first")
    small = _exchange([[conv_w_dw], [ret_gn_g], [ret_gn_b], [c]], gather=True, name="gather_small")
    dw_g, gng_g, gnb_g, c_g = small
    dw3 = jnp.transpose(dw_g, (1, 2, 0, 3)).reshape(n_conv, CONV_WIDTH, CH, LANES)
    gng_full = jnp.transpose(gng_g, (1, 2, 0, 3)).reshape(n_ret, 1, 2 * D)
    gnb_full = jnp.transpose(gnb_g, (1, 2, 0, 3)).reshape(n_ret, 1, 2 * D)
    c_all = c_g.reshape(NDEV, D)

    mod_cols = _ada_fwd(c_all, ada_w, name="ada_fwd")
    mod_all = _exchange([[mod_cols]], gather=True, name="gather_mod")[0]
    mod = lax.dynamic_index_in_dim(mod_all, me, axis=2, keepdims=False)
    mod = jnp.transpose(mod, (1, 0, 2)).reshape(DEPTH, 6 * D) + ada_b
    mods = [[mod[i, j * D:(j + 1) * D].reshape(1, D) for j in range(6)] for i in range(DEPTH)]
    tb = _ret_tables(S, D // RET_HEADS)

    def vec(a):
        return a.reshape(1, -1)

    mix_w = (_exchange_wait(first_handle, name="gather_wait_first", after=mod)[0], None)
    handle, token = _exchange_start(mixer_shards(0)[1:] + mlp_shards(0), gather=True, name="gather_start_rest0",
                                    after=mix_w[0])
    saved = []
    weights = []
    xcur = xs
    for i in range(DEPTH):
        sh1, sc1, g1, sh2, sc2, g2 = mods[i]
        j = i // 2
        if i > 0:
            mix_w = mixer_weights(i, _exchange_wait(handle, name=f"gather_wait_mix{i}", after=xcur))
            handle, token = _exchange_start(mlp_shards(i), gather=True, name=f"gather_start_mlp{i}", after=mix_w[0])
        st = dict(x_in=xcur)
        norm1 = (vec(norm_mix_g[i]), sc1, sh1)
        if i % 2 == 0:
            u, h = _mm_nn(xcur, mix_w[0], norm=norm1, bias=vec(conv_b_pw1[j]), out_dtype=F32, name=f"pw1_fwd{i}", after=token)
            u3 = u.reshape(S, 2 * CH, LANES)
            dwo = _conv_mid_fwd(u3, dw3[j], conv_b_dw[j].reshape(1, CH, LANES), name=f"conv_mid_fwd{i}").reshape(S, D)
            if i == 0:
                got = _exchange_wait(handle, name="gather_wait_rest0", after=dwo)
                mix_w, mlp_w = (mix_w[0], got[0].reshape(-1, D)), mlp_weights(got[1:3])
                handle, token = _exchange_start(mixer_shards(1), gather=True, name="gather_start_mix1", after=got[0])
            xcur, y_raw = _mm_nn(dwo, mix_w[1], ln=(vec(conv_ln_g[j]), vec(conv_ln_b[j])), bias=vec(conv_b_pw2[j]), res=xcur,
                                 gate=g1, name=f"pw2_fwd{i}")
            st.update(u3=u3, dwo=dwo, y_raw=y_raw)
        else:
            proj, h = _mm_nn(xcur, mix_w[0], norm=norm1, name=f"ret_in_fwd{i}", after=token)
            yr, yg, qr, kr = _ret_fwd(proj, tb, gng_full[j], gnb_full[j], name=f"ret_fwd{i}")
            xcur, y_raw = _mm_nn(yg, mix_w[1], res=xcur, gate=g1, name=f"ret_out_fwd{i}")
            st.update(proj=proj, yr=yr, yg=yg, qr=qr, kr=kr, y_raw=y_raw)
        st.update(h=h, x_mid=xcur)
        if i > 0:
            mlp_w = mlp_weights(_exchange_wait(handle, name=f"gather_wait_mlp{i}", after=xcur))
            if i + 1 < DEPTH:
                handle, token = _exchange_start(mixer_shards(i + 1), gather=True, name=f"gather_start_mix{i + 1}",
                                                after=mlp_w[0])
        z, h2 = _mm_nn(xcur, mlp_w[0], norm=(vec(norm_mlp_g[i]), sc2, sh2), name=f"mlp1_fwd{i}", after=token)
        xcur, o_raw = _mm_nn(z, mlp_w[1], relu2=True, res=xcur, gate=g2, name=f"mlp2_fwd{i}")
        st.update(h2=h2, z=z, o_raw=o_raw)
        saved.append(st)
        weights.append(mix_w + mlp_w)

    g2_last = mods[DEPTH - 1][5]
    loss_local, dx, d_final_g, dy, dgate, _ = _final_loss(xcur, vec(final_norm_g), target, saved[-1]["o_raw"], g2_last,
                                                          name="final_loss")
    loss = lax.psum(loss_local[0, 0], AXES)

    dmod_rows = [None] * DEPTH
    d_mix_g, d_mlp_g = [None] * DEPTH, [None] * DEPTH
    d_pw1, d_pw2, d_win, d_wout = [None] * n_conv, [None] * n_conv, [None] * n_ret, [None] * n_ret
    d_w1, d_w2 = [None] * DEPTH, [None] * DEPTH
    d_bpw1, d_bdw, d_lng, d_lnb, d_bpw2, d_dw = ([None] * n_conv for _ in range(6))
    d_gng, d_gnb = [None] * n_ret, [None] * n_ret

    def gn_parts(d):
        return jnp.transpose(d.reshape(RET_HEADS, NDEV, -1), (1, 0, 2))

    grad_handles = [None] * DEPTH
    token = None
    for i in reversed(range(DEPTH)):
        sh1, sc1, g1, sh2, sc2, g2 = mods[i]
        j = i // 2
        st = saved[i]
        mix_a, mix_b, w1_i, w2_i = weights[i]
        do, dg2 = dy, dgate
        dz = _mm_nt(do, w2_i, z=st["z"], out_dtype=BF16, name=f"mlp2_bwd_x{i}", after=token)
        d_w2[i] = _mm_tn(st["z"], do, relu2=True, name=f"mlp2_bwd_w{i}")
        dx, dsc2, dsh2, d_mlp_g[i], dy, dg1, dby = _mm_nt(dz, w1_i, norm=(st["x_mid"], vec(norm_mlp_g[i]), sc2, dx),
                                                          gated=(st["y_raw"], g1), name=f"mlp1_bwd_x{i}")
        d_w1[i] = _mm_tn(st["h2"], dz, col_shards=NDEV, name=f"mlp1_bwd_w{i}")
        mlp_groups = [[d_w1[i]], [d_w2[i].reshape(NDEV, 4 * D // NDEV, D)]]
        token = None
        if i == 0:
            mlp0_handle, token = _exchange_start(mlp_groups, gather=False, name="grads_start_mlp0")
            mlp_groups = []
        if i % 2 == 0:
            d_bpw2[j] = dby
            ds = _mm_nt(dy, mix_b, name=f"pw2_bwd_x{i}", after=token)
            ln_gb = (vec(conv_ln_g[j]), vec(conv_ln_b[j]))
            d_pw2[j] = _mm_tn(st["dwo"], dy, ln=ln_gb, name=f"pw2_bwd_w{i}")
            ddw, d_lng[j], d_lnb[j], d_bdw[j] = _ln_silu_bwd(st["dwo"], ds, *ln_gb, name=f"conv_ln_bwd{i}")
            du3, ddw_w, dbu = _conv_mid_bwd_dw(st["u3"], ddw.reshape(S, CH, LANES), dw3[j], name=f"conv_mid_bwd_dw{i}")
            d_dw[j], d_bpw1[j] = ddw_w.reshape(CONV_WIDTH, D), dbu.reshape(2, D)
            du = du3.reshape(S, 2 * D)
            d_pw1[j] = _mm_tn(st["h"], du, col_shards=NDEV, name=f"pw1_bwd_w{i}")
            mix_groups = [[d_pw1[j]], [d_pw2[j].reshape(NDEV, D // NDEV, D)],
                          [jnp.transpose(d_dw[j].reshape(CONV_WIDTH, NDEV, D // NDEV), (1, 0, 2))]]
            mix_in, mix_name = du, f"pw1_bwd_x{i}"
        else:
            dyg = _mm_nt(dy, mix_b, out_dtype=BF16, name=f"ret_out_bwd_x{i}")
            d_wout[j] = _mm_tn(st["yg"], dy, name=f"ret_out_bwd_w{i}")
            dq, dgt, dyr, d_gng[j], d_gnb[j] = _ret_bwd_q(st["proj"], st["kr"], st["yr"], dyg, tb, gng_full[j], gnb_full[j],
                                                          name=f"ret_bwd_q{i}")
            dproj = _ret_bwd_kv(st["proj"], st["qr"], st["kr"], dyr, dq, dgt, tb, name=f"ret_bwd_kv{i}")
            d_win[j] = _mm_tn(st["h"], dproj, col_shards=NDEV, name=f"ret_in_bwd_w{i}")
            mix_in, mix_name = dproj, f"ret_in_bwd_x{i}"
            mix_groups = [[d_win[j]], [d_wout[j].reshape(NDEV, 2 * D // NDEV, D)], [gn_parts(d_gng[j])],
                          [gn_parts(d_gnb[j])]]
        grad_handles[i], token = _exchange_start(mix_groups + mlp_groups, gather=False, name=f"grads_start{i}")
        gated = (saved[i - 1]["o_raw"], mods[i - 1][5]) if i > 0 else None
        outs = _mm_nt(mix_in, mix_a, norm=(st["x_in"], vec(norm_mix_g[i]), sc1, dx), gated=gated, name=mix_name, after=token)
        dx, dsc1, dsh1, d_mix_g[i] = outs[:4]
        if i > 0:
            dy, dgate = outs[4], outs[5]
        dmod_rows[i] = jnp.concatenate([dsh1, dsc1, dg1, dsh2, dsc2, dg2], axis=0)
    grad_x = dx.reshape(1, S, D)

    small_local = jnp.concatenate(dmod_rows + d_mix_g + d_mlp_g + d_bpw1 + d_bdw + d_lng + d_lnb + d_bpw2 + [d_final_g],
                                  axis=0)
    small_all = _exchange([[small_local]], gather=True, name="gather_small_grads")[0]

    def pack(src):
        return jnp.concatenate([src[n].reshape(-1, D) for n in SMALL], axis=0)[None]

    sm = _adamw(pack(W), pack(Mo), pack(Vo), [small_all], name="adamw_small")
    results = {}
    row = 0
    for n in SMALL:
        cnt = W[n].size // D
        results[n] = tuple(o[0, row:row + cnt].reshape(W[n].shape) for o in sm)
        row += cnt

    ns_ada = ada_w.shape[2]
    dmod_all = small_all[:, :6 * DEPTH, :].reshape(NDEV, DEPTH, 6 * D)
    dmod_cols = jnp.transpose(lax.dynamic_slice_in_dim(dmod_all, me * ns_ada, ns_ada, axis=2), (1, 0, 2))
    g_ada = _ada_bwd(c_all, dmod_cols, name="ada_bwd")
    flat_ada = (1, DEPTH * D, ns_ada)
    ada_res = _adamw(ada_w.reshape(flat_ada), m_ada_w.reshape(flat_ada), v_ada_w.reshape(flat_ada),
                     [g_ada.reshape(flat_ada)], name="adamw_ada_w")
    results["ada_w"] = tuple(o.reshape(ada_w.shape) for o in ada_res)

    def update(names, parts):
        for n in names:
            results[n] = _adamw(W[n], Mo[n], Vo[n], parts[n], name=f"adamw_{n}")

    got = {i: _exchange_wait(grad_handles[i], name=f"grads_wait{i}", after=dx) for i in range(DEPTH - 1, 0, -1)}
    ret_layers = [i for i in range(DEPTH) if i % 2 == 1]
    update(("ret_w_in", "ret_w_out", "ret_gn_g", "ret_gn_b"),
           dict(ret_w_in=[got[i][0] for i in ret_layers], ret_w_out=[got[i][1] for i in ret_layers],
                ret_gn_g=[got[i][2] for i in ret_layers], ret_gn_b=[got[i][3] for i in ret_layers]))
    got_mlp0 = _exchange_wait(mlp0_handle, name="grads_wait_mlp0", after=results["ret_w_in"][0])
    update(("mlp_w1", "mlp_w2"),
           dict(mlp_w1=[got_mlp0[0]] + [got[i][-2] for i in range(1, DEPTH)],
                mlp_w2=[got_mlp0[1]] + [got[i][-1] for i in range(1, DEPTH)]))
    got[0] = _exchange_wait(grad_handles[0], name="grads_wait0", after=results["mlp_w1"][0])
    conv_layers = [i for i in range(DEPTH) if i % 2 == 0]
    update(("conv_w_pw1", "conv_w_pw2", "conv_w_dw"),
           dict(conv_w_pw1=[got[i][0] for i in conv_layers], conv_w_pw2=[got[i][1] for i in conv_layers],
                conv_w_dw=[got[i][2] for i in conv_layers]))

    outs = [loss, grad_x]
    for kind in range(4):
        outs += [results[n][kind] for n in WEIGHTS]
    return tuple(outs)
```

```python
import functools

import jax
import jax.numpy as jnp
import numpy as np
from jax import lax
from jax.experimental import pallas as pl
from jax.experimental.pallas import tpu as pltpu

F32, BF16 = jnp.float32, jnp.bfloat16
AXES = ("x", "y", "c")
NDEV = 8
DEPTH = 4
EPS = 1e-6
CHUNK = 64
CONV_WIDTH = 31
HALO = 32
RET_HEADS = 4
RET_BLOCK = 256
ROPE_BASE = 10000.0
LANES = 128
ADAM_LR, ADAM_B1, ADAM_B2, ADAM_EPS, ADAM_WD, ADAM_STEP = 0.001, 0.9, 0.999, 1e-08, 0.01, 10
VMEM_LIMIT = 56 * 1024 * 1024
VMEM_BLOCK_BUDGET = 44 * 1024 * 1024
MESH = pl.DeviceIdType.MESH
NT_DIMS = (((1,), (1,)), ((), ()))
TN_DIMS = (((0,), (0,)), ((), ()))


def _call(body, *, name, out_shape, in_specs, out_specs, grid=(), scratch=(), sem=None, aliases=None):
    params = dict(vmem_limit_bytes=VMEM_LIMIT)
    if sem is not None:
        params["dimension_semantics"] = sem
    return pl.pallas_call(body, name=name, grid=grid, in_specs=in_specs, out_specs=out_specs, out_shape=out_shape,
                          scratch_shapes=list(scratch), input_output_aliases=aliases or {},
                          compiler_params=pltpu.CompilerParams(**params))


def _row_tile(rows, want):
    t = min(rows, want)
    while rows % t:
        t //= 2
    return t


def _sds(shape, dtype):
    return jax.ShapeDtypeStruct(tuple(shape), dtype)


def _sigmoid(v):
    return 1.0 / (1.0 + jnp.exp(-v))


def _exchange(groups, *, gather, name):
    flat = [a for g in groups for a in g]
    n_in = len(flat)
    out_shapes = []
    for g in groups:
        s = g[0].shape if gather else g[0].shape[1:]
        lead = (NDEV,) if len(g) == 1 else (NDEV, len(g))
        out_shapes.append(_sds(lead + tuple(s), g[0].dtype))
    n_g = len(groups)

    def body(*refs):
        ins, outs = refs[:n_in], refs[n_in:n_in + n_g]
        send_sems, recv_sems, loc_sems = refs[n_in + n_g:]
        x, y, c = lax.axis_index("x"), lax.axis_index("y"), lax.axis_index("c")
        me = 4 * x + 2 * y + c
        locs, k = [], 0
        for gi, g in enumerate(groups):
            for li in range(len(g)):
                src = ins[k] if gather else ins[k].at[me]
                dst = outs[gi].at[me] if len(g) == 1 else outs[gi].at[me, li]
                cp = pltpu.make_async_copy(src, dst, loc_sems.at[k])
                cp.start()
                locs.append(cp)
                k += 1
        k0 = 0
        for gi, g in enumerate(groups):
            for r in range(1, NDEV):
                px = 1 - x if r & 4 else x
                py = 1 - y if r & 2 else y
                pc = 1 - c if r & 1 else c
                peer = 4 * px + 2 * py + pc
                for li in range(len(g)):
                    src = ins[k0 + li] if gather else ins[k0 + li].at[peer]
                    dst = outs[gi].at[me] if len(g) == 1 else outs[gi].at[me, li]
                    pltpu.make_async_remote_copy(src_ref=src, dst_ref=dst, send_sem=send_sems.at[gi * (NDEV - 1) + r - 1],
                                                 recv_sem=recv_sems.at[gi * (NDEV - 1) + r - 1], device_id=(px, py, pc),
                                                 device_id_type=MESH).start()
            k0 += len(g)
        for gi, g in enumerate(groups):
            for r in range(1, NDEV):
                px = 1 - x if r & 4 else x
                py = 1 - y if r & 2 else y
                pc = 1 - c if r & 1 else c
                peer = 4 * px + 2 * py + pc
                slab = pltpu.make_async_remote_copy(src_ref=outs[gi].at[me], dst_ref=outs[gi].at[peer],
                                                    send_sem=send_sems.at[gi * (NDEV - 1) + r - 1], recv_sem=recv_sems.at[gi * (NDEV - 1) + r - 1],
                                                    device_id=(px, py, pc), device_id_type=MESH)
                slab.wait_send()
                slab.wait_recv()
        for cp in locs:
            cp.wait()

    hbm = pl.BlockSpec(memory_space=pltpu.HBM)
    outs = _call(body, name=name, out_shape=tuple(out_shapes), in_specs=[hbm] * n_in, out_specs=tuple([hbm] * n_g),
                 scratch=[pltpu.SemaphoreType.DMA((n_g * (NDEV - 1),)), pltpu.SemaphoreType.DMA((n_g * (NDEV - 1),)),
                          pltpu.SemaphoreType.DMA((n_in,))])(*flat)
    return list(outs)


def _peer_of(x, y, c, r):
    return (1 - x if r & 4 else x, 1 - y if r & 2 else y, 1 - c if r & 1 else c)


def _exchange_start(groups, *, gather, name, after=None):
    flat = [pltpu.with_memory_space_constraint(a, pltpu.HBM) for g in groups for a in g]
    n_in, n_g = len(flat), len(groups)
    land_shapes = []
    for g in groups:
        s = g[0].shape if gather else g[0].shape[1:]
        lead = (NDEV,) if len(g) == 1 else (NDEV, len(g))
        land_shapes.append((lead + tuple(s), g[0].dtype))
    lands = [pltpu.with_memory_space_constraint(lax.empty(s, d), pltpu.HBM) for s, d in land_shapes]
    n_after = 0 if after is None else 1

    def body(*refs):
        ins, land = refs[:n_in], refs[n_in:n_in + n_g]
        send_sems, recv_sems, loc_sems = refs[n_in + n_g + n_after:n_in + n_g + n_after + 3]
        token = refs[-1]
        x, y, c = lax.axis_index("x"), lax.axis_index("y"), lax.axis_index("c")
        me = 4 * x + 2 * y + c
        k = 0
        for gi, g in enumerate(groups):
            for li in range(len(g)):
                dst = land[gi].at[me] if len(g) == 1 else land[gi].at[me, li]
                pltpu.make_async_copy(ins[k] if gather else ins[k].at[me], dst, loc_sems.at[k]).start()
                k += 1
        k0 = 0
        for gi, g in enumerate(groups):
            for r in range(1, NDEV):
                px, py, pc = _peer_of(x, y, c, r)
                peer = 4 * px + 2 * py + pc
                for li in range(len(g)):
                    dst = land[gi].at[me] if len(g) == 1 else land[gi].at[me, li]
                    pltpu.make_async_remote_copy(src_ref=ins[k0 + li] if gather else ins[k0 + li].at[peer], dst_ref=dst,
                                                 send_sem=send_sems.at[gi * (NDEV - 1) + r - 1], recv_sem=recv_sems.at[gi * (NDEV - 1) + r - 1],
                                                 device_id=(px, py, pc), device_id_type=MESH).start()
            k0 += len(g)
        token[...] = jnp.zeros_like(token)

    hbm = pl.BlockSpec(memory_space=pltpu.HBM)
    sem = pl.BlockSpec(memory_space=pltpu.SEMAPHORE)
    args = flat + lands + ([after] if n_after else [])
    outs = pl.pallas_call(body, name=name,
        out_shape=(pltpu.SemaphoreType.DMA((n_g * (NDEV - 1),)), pltpu.SemaphoreType.DMA((n_g * (NDEV - 1),)),
                   pltpu.SemaphoreType.DMA((n_in,)), *[pltpu.HBM(a.shape, a.dtype) for a in flat],
                   *[pltpu.HBM(s, d) for s, d in land_shapes], _sds((8, LANES), F32)),
        in_specs=[hbm] * (n_in + n_g) + [pl.BlockSpec(memory_space=pl.ANY)] * n_after,
        out_specs=(sem, sem, sem, *[hbm] * (n_in + n_g), pl.BlockSpec(memory_space=pltpu.VMEM)),
        input_output_aliases={k: 3 + k for k in range(n_in + n_g)},
        compiler_params=pltpu.CompilerParams(has_side_effects=pltpu.SideEffectType.DATAFLOW_SIDE_EFFECTING))(*args)
    handle = dict(sems=outs[0:3], srcs=list(outs[3:3 + n_in]), lands=list(outs[3 + n_in:3 + n_in + n_g]),
                  sizes=[len(g) for g in groups], gather=gather)
    return handle, outs[-1]


def _exchange_wait(handle, *, name, after):
    srcs, lands, sizes, gather = handle["srcs"], handle["lands"], handle["sizes"], handle["gather"]
    n_in, n_g = len(srcs), len(lands)

    def body(*refs):
        ins, land = refs[:n_in], refs[n_in:n_in + n_g]
        send_sems, recv_sems, loc_sems = refs[n_in + n_g:n_in + n_g + 3]
        x, y, c = lax.axis_index("x"), lax.axis_index("y"), lax.axis_index("c")
        me = 4 * x + 2 * y + c
        for gi in range(n_g):
            for r in range(1, NDEV):
                px, py, pc = _peer_of(x, y, c, r)
                peer = 4 * px + 2 * py + pc
                slab = pltpu.make_async_remote_copy(src_ref=land[gi].at[me], dst_ref=land[gi].at[peer],
                                                    send_sem=send_sems.at[gi * (NDEV - 1) + r - 1], recv_sem=recv_sems.at[gi * (NDEV - 1) + r - 1],
                                                    device_id=(px, py, pc), device_id_type=MESH)
                slab.wait_send()
                slab.wait_recv()
        k = 0
        for gi in range(n_g):
            for li in range(sizes[gi]):
                dst = land[gi].at[me] if sizes[gi] == 1 else land[gi].at[me, li]
                pltpu.make_async_copy(ins[k] if gather else ins[k].at[me], dst, loc_sems.at[k]).wait()
                k += 1

    hbm = pl.BlockSpec(memory_space=pltpu.HBM)
    sem = pl.BlockSpec(memory_space=pltpu.SEMAPHORE)
    outs = pl.pallas_call(body, name=name, out_shape=tuple(pltpu.HBM(a.shape, a.dtype) for a in srcs + lands),
        in_specs=[hbm] * (n_in + n_g) + [sem] * 3 + [pl.BlockSpec(memory_space=pl.ANY)],
        out_specs=tuple([hbm] * (n_in + n_g)), input_output_aliases={k: k for k in range(n_in + n_g)},
        compiler_params=pltpu.CompilerParams(has_side_effects=pltpu.SideEffectType.DATAFLOW_SIDE_EFFECTING))(
            *srcs, *lands, *handle["sems"], after)
    return list(outs[n_in:])


def _gate_part(first, dx, y_ref, g_ref, dy_ref, dg_ref, db_ref):
    @pl.when(first)
    def _():
        dg_ref[...] = jnp.zeros_like(dg_ref)
        db_ref[...] = jnp.zeros_like(db_ref)

    dy = dx * g_ref[...]
    dy_ref[...] = dy.astype(BF16)
    dg_ref[...] += jnp.sum(dx * y_ref[...].astype(F32), axis=0, keepdims=True)
    db_ref[...] += jnp.sum(dy, axis=0, keepdims=True)


def _norm_bwd_part(first, dhv, x_ref, g_ref, sc_ref, dres_ref, dx_ref, dsc_ref, dsh_ref, dg_ref):
    @pl.when(first)
    def _():
        dsc_ref[...] = jnp.zeros_like(dsc_ref)
        dsh_ref[...] = jnp.zeros_like(dsh_ref)
        dg_ref[...] = jnp.zeros_like(dg_ref)

    xv = x_ref[...]
    r = lax.rsqrt(jnp.mean(xv * xv, axis=-1, keepdims=True) + EPS)
    xhat = xv * r
    gain_v = g_ref[...]
    dsc_ref[...] += jnp.sum(dhv * (xhat * gain_v), axis=0, keepdims=True)
    dsh_ref[...] += jnp.sum(dhv, axis=0, keepdims=True)
    dxn = dhv * (1.0 + sc_ref[...])
    dg_ref[...] += jnp.sum(dxn * xhat, axis=0, keepdims=True)
    dxhat = dxn * gain_v
    dx = dres_ref[...] + r * (dxhat - xhat * jnp.mean(dxhat * xhat, axis=-1, keepdims=True))
    dx_ref[...] = dx
    return dx


def _final_loss(x, gain, target, y_prev, gate_prev, *, name):
    S, D = x.shape
    tm = _row_tile(S, 512)

    def body(x_ref, g_ref, t_ref, y_ref, gp_ref, loss_ref, dx_ref, dg_ref, dy_ref, dgp_ref, dbp_ref):
        first = pl.program_id(0) == 0

        @pl.when(first)
        def _():
            loss_ref[...] = jnp.zeros_like(loss_ref)
            dg_ref[...] = jnp.zeros_like(dg_ref)

        xv = x_ref[...]
        r = lax.rsqrt(jnp.mean(xv * xv, axis=-1, keepdims=True) + EPS)
        xhat = xv * r
        gv = g_ref[...]
        err = xhat * gv - t_ref[...]
        row_loss = jnp.mean(err * err, axis=-1, keepdims=True)
        loss_ref[...] += 0.5 * jnp.sum(row_loss, axis=0, keepdims=True)
        dy = err * (1.0 / D)
        dg_ref[...] += jnp.sum(dy * xhat, axis=0, keepdims=True)
        dxhat = dy * gv
        dx = r * (dxhat - xhat * jnp.mean(dxhat * xhat, axis=-1, keepdims=True))
        dx_ref[...] = dx
        _gate_part(first, dx, y_ref, gp_ref, dy_ref, dgp_ref, dbp_ref)

    row = pl.BlockSpec((tm, D), lambda i: (i, 0))
    vec = pl.BlockSpec((1, D), lambda i: (0, 0))
    one = pl.BlockSpec((1, 1), lambda i: (0, 0))
    vsh = _sds((1, D), F32)
    return _call(body, name=name, grid=(S // tm,), in_specs=[row, vec, row, row, vec],
                 out_specs=(one, row, vec, row, vec, vec),
                 out_shape=(_sds((1, 1), F32), _sds((S, D), F32), vsh, _sds((S, D), BF16), vsh, vsh),
                 sem=("arbitrary",))(x, gain, target, y_prev, gate_prev)


def _pick_tm(M, bytes_per_row, fixed_bytes):
    for tm in (1024, 512, 256, 128):
        if M % tm == 0 and 2 * tm * bytes_per_row + fixed_bytes <= VMEM_BLOCK_BUDGET:
            return tm
    return _row_tile(M, 128)


def _mm_nn(a, w, *, name, bias=None, relu2=False, ln=None, norm=None, res=None, gate=None, out_dtype=BF16, after=None):
    M, K = a.shape
    col = w.ndim == 3
    if col:
        nsh, ns = w.shape[0], w.shape[2]
        w_spec = pl.BlockSpec((nsh, K, ns), lambda i: (0, 0, 0))
    else:
        nsh, ns = 1, w.shape[1]
        w_spec = pl.BlockSpec((K, ns), lambda i: (0, 0))
    N = nsh * ns
    residual = res is not None
    out_bytes = (4 + 4 + 2) if residual else jnp.dtype(out_dtype).itemsize
    tm = _pick_tm(M, K * a.dtype.itemsize + N * out_bytes + (K * 2 if norm is not None else 0), 2 * K * N * 2)

    def body(*refs):
        it = iter(refs)
        a_ref, w_ref = next(it), next(it)
        b_ref = next(it) if bias is not None else None
        lg_ref, lb_ref = (next(it), next(it)) if ln is not None else (None, None)
        ng_ref, nsc_ref, nsh_ref = (next(it), next(it), next(it)) if norm is not None else (None, None, None)
        res_ref, gate_ref = (next(it), next(it)) if residual else (None, None)
        if after is not None:
            next(it)
        out_ref = next(it)
        raw_ref = next(it) if residual else None
        av = a_ref[...]
        if relu2:
            av = jnp.square(jnp.maximum(av.astype(F32), 0.0))
        if ln is not None:
            av, _ = _ln_silu(av, lg_ref[...], lb_ref[...])
        if norm is not None:
            r = lax.rsqrt(jnp.mean(av * av, axis=-1, keepdims=True) + EPS)
            av = (av * r) * ng_ref[...] * (1.0 + nsc_ref[...]) + nsh_ref[...]
        ab = av.astype(BF16)
        if norm is not None:
            next(it)[...] = ab
        for d in range(nsh):
            cols = slice(d * ns, (d + 1) * ns)
            acc = jnp.dot(ab, w_ref[d] if col else w_ref[...], preferred_element_type=F32)
            if b_ref is not None:
                acc = acc + b_ref[:, cols]
            if residual:
                raw_ref[:, cols] = acc.astype(BF16)
                out_ref[:, cols] = res_ref[:, cols] + gate_ref[:, cols] * acc
            else:
                out_ref[:, cols] = acc.astype(out_dtype)

    tile = pl.BlockSpec((tm, N), lambda i: (i, 0))
    vec = pl.BlockSpec((1, N), lambda i: (0, 0))
    in_specs, args = [pl.BlockSpec((tm, K), lambda i: (i, 0)), w_spec], [a, w]
    if bias is not None:
        in_specs.append(vec)
        args.append(bias)
    if ln is not None:
        in_specs += [pl.BlockSpec((1, K), lambda i: (0, 0))] * 2
        args += list(ln)
    if norm is not None:
        in_specs += [pl.BlockSpec((1, K), lambda i: (0, 0))] * 3
        args += list(norm)
    if residual:
        in_specs += [tile, vec]
        args += [res, gate]
        out_specs = [tile, tile]
        out_shape = [_sds((M, N), F32), _sds((M, N), BF16)]
    else:
        out_specs = [tile]
        out_shape = [_sds((M, N), out_dtype)]
    if after is not None:
        in_specs.append(pl.BlockSpec(memory_space=pl.ANY))
        args.append(after)
    if norm is not None:
        out_specs.append(pl.BlockSpec((tm, K), lambda i: (i, 0)))
        out_shape.append(_sds((M, K), BF16))
    outs = _call(body, name=name, grid=(M // tm,), in_specs=in_specs, out_specs=tuple(out_specs), out_shape=tuple(out_shape),
                 sem=("parallel",))(*args)
    return outs[0] if len(outs) == 1 else outs


def _mm_nt(g, w, *, name, z=None, out_dtype=F32, after=None, norm=None, gated=None):
    M, N = g.shape
    col = w.ndim == 3
    if col:
        nsh, K, ns = w.shape
        w_spec = pl.BlockSpec((nsh, K, ns), lambda i: (0, 0, 0))
    else:
        K = w.shape[0]
        w_spec = pl.BlockSpec((K, N), lambda i: (0, 0))
    assert norm is None or col
    kc = min(K, 1024)
    obytes = jnp.dtype(out_dtype).itemsize
    row_bytes = N * g.dtype.itemsize + K * obytes + (K * 2 if z is not None else 0)
    if norm is not None:
        row_bytes += 2 * K * 4 + (K * 4 if gated is not None else 0)
    tm = _pick_tm(M, row_bytes, 2 * K * N * 2 + 512 * K * 4)

    def body(*refs):
        it = iter(refs)
        g_ref, w_ref = next(it), next(it)
        z_ref = next(it) if z is not None else None
        norm_in = [next(it) for _ in range(4)] if norm is not None else None
        gate_in = [next(it) for _ in range(2)] if gated is not None else None
        if after is not None:
            next(it)
        out_ref = next(it)
        if col:
            acc = None
            for d in range(nsh):
                part = lax.dot_general(g_ref[:, d * ns:(d + 1) * ns].astype(BF16), w_ref[d], NT_DIMS,
                                       preferred_element_type=F32)
                acc = part if acc is None else acc + part
            if norm is None:
                out_ref[...] = acc.astype(out_dtype)
            else:
                first = pl.program_id(0) == 0
                dx = _norm_bwd_part(first, acc, *norm_in, out_ref, next(it), next(it), next(it))
                if gated is not None:
                    _gate_part(first, dx, *gate_in, next(it), next(it), next(it))
        else:
            gb = g_ref[...].astype(BF16)
            for cki in range(K // kc):
                cols = slice(cki * kc, (cki + 1) * kc)
                part = lax.dot_general(gb, w_ref[cols, :], NT_DIMS, preferred_element_type=F32)
                if z_ref is not None:
                    part = part * (2.0 * jnp.maximum(z_ref[:, cols].astype(F32), 0.0))
                out_ref[:, cols] = part.astype(out_dtype)

    row = pl.BlockSpec((tm, K), lambda i: (i, 0))
    vec = pl.BlockSpec((1, K), lambda i: (0, 0))
    vsh = _sds((1, K), F32)
    in_specs, args = [pl.BlockSpec((tm, N), lambda i: (i, 0)), w_spec], [g, w]
    out_specs, out_shape = [row], [_sds((M, K), out_dtype)]
    if z is not None:
        in_specs.append(row)
        args.append(z)
    if norm is not None:
        x, gain, sc, dres = norm
        in_specs += [row, vec, vec, row]
        args += [x, gain, sc, dres]
        out_specs += [vec, vec, vec]
        out_shape += [vsh, vsh, vsh]
    if gated is not None:
        in_specs += [row, vec]
        args += list(gated)
        out_specs += [row, vec, vec]
        out_shape += [_sds((M, K), BF16), vsh, vsh]
    if after is not None:
        in_specs.append(pl.BlockSpec(memory_space=pl.ANY))
        args.append(after)
    outs = _call(body, name=name, grid=(M // tm,), in_specs=in_specs, out_specs=tuple(out_specs), out_shape=tuple(out_shape),
                 sem=("parallel",) if norm is None else ("arbitrary",))(*args)
    return outs[0] if len(outs) == 1 else outs


def _mm_tn(a, g, *, name, col_shards=None, relu2=False, ln=None):
    M, K = a.shape
    N = g.shape[1]
    acc_budget = 8 * 1024 * 1024
    tm = _row_tile(M, 1024)
    nm = M // tm
    if col_shards:
        ns = N // col_shards
        spc = col_shards
        while spc > 1 and K * ns * spc * 4 > acc_budget:
            spc //= 2
        grid = (col_shards // spc, nm)
        a_spec = pl.BlockSpec((tm, K), lambda c, m: (m, 0))
        g_spec = pl.BlockSpec((tm, spc * ns), lambda c, m: (m, c))
        out_spec = pl.BlockSpec((spc, K, ns), lambda c, m: (c, 0, 0))
        out_shape = _sds((col_shards, K, ns), BF16)
        acc_shape = (K, spc * ns)
    else:
        tk = K
        while tk > 128 and tk * N * 4 > acc_budget:
            tk //= 2
        grid = (K // tk, nm)
        a_spec = pl.BlockSpec((tm, tk), lambda c, m: (m, c))
        g_spec = pl.BlockSpec((tm, N), lambda c, m: (m, 0))
        out_spec = pl.BlockSpec((tk, N), lambda c, m: (c, 0))
        out_shape = _sds((K, N), BF16)
        acc_shape = (tk, N)
        assert ln is None or tk == K
    in_specs, args = [a_spec, g_spec], [a, g]
    if ln is not None:
        in_specs += [pl.BlockSpec((1, K), lambda c, m: (0, 0))] * 2
        args += list(ln)

    def body(a_ref, g_ref, *rest):
        out_ref, acc_ref = rest[-2:]
        m = pl.program_id(1)

        @pl.when(m == 0)
        def _():
            acc_ref[...] = jnp.zeros_like(acc_ref)

        av = a_ref[...]
        if relu2:
            av = jnp.square(jnp.maximum(av.astype(F32), 0.0))
        if ln is not None:
            av, _ = _ln_silu(av, rest[0][...], rest[1][...])
        acc_ref[...] += lax.dot_general(av.astype(BF16), g_ref[...].astype(BF16), TN_DIMS, preferred_element_type=F32)

        @pl.when(m == nm - 1)
        def _():
            if col_shards:
                for s in range(spc):
                    out_ref[s] = acc_ref[:, s * ns:(s + 1) * ns].astype(BF16)
            else:
                out_ref[...] = acc_ref[...].astype(BF16)

    return _call(body, name=name, grid=grid, in_specs=in_specs, out_specs=out_spec, out_shape=out_shape,
                 scratch=[pltpu.VMEM(acc_shape, F32)], sem=("parallel", "arbitrary"))(*args)


CONV_TILE = 256
CONV_GROUP = 32


def _glu(u, ch):
    return u[:, :ch] * _sigmoid(u[:, ch:])


def _fill_glu(buf, u_ref, uh_ref, ch, tile):
    first = pl.program_id(0) == 0
    buf[0:HALO] = jnp.where(first, 0.0, _glu(uh_ref[...], ch))
    buf[HALO:HALO + tile] = _glu(u_ref[...], ch)


CONV_SUB = 4


def _conv_specs(S, ch, tile):
    per = tile // HALO
    u_spec = pl.BlockSpec((tile, 2 * ch, LANES), lambda i: (i, 0, 0))
    uh_spec = pl.BlockSpec((HALO, 2 * ch, LANES), lambda i: (jnp.maximum(i * per - 1, 0), 0, 0))
    x_spec = pl.BlockSpec((tile, ch, LANES), lambda i: (i, 0, 0))
    xn_spec = pl.BlockSpec((HALO, ch, LANES), lambda i: (jnp.minimum((i + 1) * per, S // HALO - 1), 0, 0))
    w_spec = pl.BlockSpec((CONV_WIDTH, ch, LANES), lambda i: (0, 0, 0))
    v_spec = pl.BlockSpec((1, ch, LANES), lambda i: (0, 0, 0))
    return u_spec, uh_spec, x_spec, xn_spec, w_spec, v_spec


def _conv_mid_fwd(u3, w3, bdw3, *, name):
    S, ch2, _ = u3.shape
    ch = ch2 // 2
    tile = _row_tile(S, CONV_TILE)
    sub = _row_tile(tile, CONV_SUB)
    u_spec, uh_spec, x_spec, _, w_spec, v_spec = _conv_specs(S, ch, tile)

    def body(u_ref, uh_ref, w_ref, b_ref, o_ref, buf):
        _fill_glu(buf, u_ref, uh_ref, ch, tile)

        def step(q, carry):
            acc = [b_ref[...], None]
            for k in range(CONV_WIDTH):
                term = buf[pl.ds(q * sub + (HALO - CONV_WIDTH + 1 + k), sub)] * w_ref[k]
                acc[k % 2] = term if acc[k % 2] is None else acc[k % 2] + term
            o_ref[pl.ds(q * sub, sub)] = acc[0] + acc[1]
            return carry

        lax.fori_loop(0, tile // sub, step, 0)

    return _call(body, name=name, grid=(S // tile,), in_specs=[u_spec, uh_spec, w_spec, v_spec], out_specs=x_spec,
                 out_shape=_sds((S, ch, LANES), F32), scratch=[pltpu.VMEM((tile + HALO, ch, LANES), F32)],
                 sem=("parallel",))(u3, u3, w3, bdw3)


def _ln_silu(v, gv, bv):
    mu = jnp.mean(v, axis=-1, keepdims=True)
    cen = v - mu
    rstd = lax.rsqrt(jnp.mean(cen * cen, axis=-1, keepdims=True) + EPS)
    nrm = cen * rstd
    ln = nrm * gv + bv
    sg = _sigmoid(ln)
    return ln * sg, (nrm, rstd, ln, sg)


def _ln_silu_bwd(dwo, ds, lng, lnb, *, name):
    S, D = dwo.shape
    tm = _row_tile(S, 512)

    def body(v_ref, ds_ref, g_ref, b_ref, ddw_ref, dg_ref, db_ref, dbdw_ref):
        @pl.when(pl.program_id(0) == 0)
        def _():
            dg_ref[...] = jnp.zeros_like(dg_ref)
            db_ref[...] = jnp.zeros_like(db_ref)
            dbdw_ref[...] = jnp.zeros_like(dbdw_ref)

        gv = g_ref[...]
        _, (nrm, rstd, ln, sg) = _ln_silu(v_ref[...], gv, b_ref[...])
        dln = ds_ref[...] * (sg * (1.0 + ln * (1.0 - sg)))
        dg_ref[...] += jnp.sum(dln * nrm, axis=0, keepdims=True)
        db_ref[...] += jnp.sum(dln, axis=0, keepdims=True)
        dn = dln * gv
        ddw = rstd * (dn - jnp.mean(dn, axis=-1, keepdims=True) - nrm * jnp.mean(dn * nrm, axis=-1, keepdims=True))
        dbdw_ref[...] += jnp.sum(ddw, axis=0, keepdims=True)
        ddw_ref[...] = ddw

    row = pl.BlockSpec((tm, D), lambda i: (i, 0))
    vec = pl.BlockSpec((1, D), lambda i: (0, 0))
    vsh = _sds((1, D), F32)
    return _call(body, name=name, grid=(S // tm,), in_specs=[row, row, vec, vec], out_specs=(row, vec, vec, vec),
                 out_shape=(_sds((S, D), F32), vsh, vsh, vsh), sem=("arbitrary",))(dwo, ds, lng, lnb)


def _conv_mid_bwd_dw(u3, ddw3, w3, *, name):
    S, ch2, _ = u3.shape
    ch = ch2 // 2
    tile = _row_tile(S, CONV_TILE)
    rows = _row_tile(tile, CONV_GROUP)
    sub = _row_tile(rows, CONV_SUB)
    last = S // tile - 1
    u_spec, uh_spec, x_spec, xn_spec, w_spec, _ = _conv_specs(S, ch, tile)
    b_spec = pl.BlockSpec((1, 2 * ch, LANES), lambda i: (0, 0, 0))

    def body(u_ref, uh_ref, d_ref, dn_ref, w_ref, du_ref, dw_ref, db_ref, gbuf, dbuf, stage):
        @pl.when(pl.program_id(0) == 0)
        def _():
            dw_ref[...] = jnp.zeros_like(dw_ref)
            db_ref[...] = jnp.zeros_like(db_ref)

        _fill_glu(gbuf, u_ref, uh_ref, ch, tile)
        dbuf[0:tile] = d_ref[...]
        dbuf[tile:tile + HALO] = jnp.where(pl.program_id(0) == last, 0.0, dn_ref[...])

        def group(r, carry):
            t0 = r * rows
            def step(q, c):
                s0 = t0 + q * sub
                ddw = dbuf[pl.ds(s0, sub)]
                acc = [None, None]
                for k in range(CONV_WIDTH):
                    term = dbuf[pl.ds(s0 + (CONV_WIDTH - 1 - k), sub)] * w_ref[k]
                    acc[k % 2] = term if acc[k % 2] is None else acc[k % 2] + term
                    dw_ref[k] += jnp.sum(ddw * gbuf[pl.ds(s0 + (HALO - CONV_WIDTH + 1 + k), sub)], axis=0)
                stage[pl.ds(q * sub, sub)] = acc[0] + acc[1]
                return c

            lax.fori_loop(0, rows // sub, step, 0)
            dglu = stage[...]
            uv = u_ref[pl.ds(t0, rows)]
            av, sg = uv[:, :ch], _sigmoid(uv[:, ch:])
            da = dglu * sg
            dg = dglu * av * sg * (1.0 - sg)
            du_ref[pl.ds(t0, rows), 0:ch] = da
            du_ref[pl.ds(t0, rows), ch:2 * ch] = dg
            db_ref[:, 0:ch] += jnp.sum(da, axis=0, keepdims=True)
            db_ref[:, ch:2 * ch] += jnp.sum(dg, axis=0, keepdims=True)
            return carry

        lax.fori_loop(0, tile // rows, group, 0)

    return _call(body, name=name, grid=(S // tile,), in_specs=[u_spec, uh_spec, x_spec, xn_spec, w_spec],
                 out_specs=(u_spec, w_spec, b_spec),
                 out_shape=(_sds((S, 2 * ch, LANES), F32), _sds((CONV_WIDTH, ch, LANES), F32), _sds((1, 2 * ch, LANES), F32)),
                 scratch=[pltpu.VMEM((tile + HALO, ch, LANES), F32), pltpu.VMEM((tile + HALO, ch, LANES), F32),
                          pltpu.VMEM((rows, ch, LANES), F32)],
                 sem=("arbitrary",))(u3, u3, ddw3, ddw3, w3)


def _ret_tables(S, dk):
    f32 = np.float32
    B = min(RET_BLOCK, S)
    lg = np.log(f32(1.0) - f32(2.0) ** (f32(-5.0) - np.arange(RET_HEADS, dtype=f32)))
    idx = np.arange(B, dtype=f32)
    diff = idx[:, None] - idx[None, :]
    cq, ck = (np.arange(B) // CHUNK)[:, None], (np.arange(B) // CHUNK)[None, :]
    dist = np.where(cq == ck, np.abs(diff), diff)
    mask = np.where(ck <= cq, np.exp(lg[:, None, None] * dist[None]), f32(0.0)).astype(f32)
    xi = np.exp(lg[:, None] * (idx + f32(1.0)))[..., None].astype(f32)
    zeta = np.exp(lg[:, None] * (f32(B - 1.0) - idx))[..., None].astype(f32)
    gam = np.broadcast_to(np.exp(lg * f32(B))[:, None, None], (RET_HEADS, 8, LANES)).astype(f32)
    pos = np.arange(S, dtype=f32)
    inv = (f32(ROPE_BASE) ** (-np.arange(0, dk, 2, dtype=f32) / f32(dk))).astype(f32)
    ang = (pos[:, None] * inv[None, :]).astype(f32)
    tb = dict(mask=mask, xi=xi, zeta=zeta, gam=gam, cos=np.cos(ang).astype(f32), sin=np.sin(ang).astype(f32))
    return dict(B=B, **{k: jnp.asarray(v) for k, v in tb.items()})


def _rope(v, cs, sn):
    half = v.shape[1] // 2
    v1, v2 = v[:, :half], v[:, half:]
    return jnp.concatenate([v1 * cs - v2 * sn, v2 * cs + v1 * sn], axis=-1)


def _rope_t(d, cs, sn):
    half = d.shape[1] // 2
    d1, d2 = d[:, :half], d[:, half:]
    return jnp.concatenate([d1 * cs + d2 * sn, d2 * cs - d1 * sn], axis=-1)


def _dot(a, b):
    return jnp.dot(a.astype(BF16), b.astype(BF16), preferred_element_type=F32)


def _dot_nt(a, b):
    return lax.dot_general(a.astype(BF16), b.astype(BF16), NT_DIMS, preferred_element_type=F32)


def _dot_tn(a, b):
    return lax.dot_general(a.astype(BF16), b.astype(BF16), TN_DIMS, preferred_element_type=F32)


def _ret_specs(S, D, B, RB, reverse):
    dk, dv = D // RET_HEADS, 2 * D // RET_HEADS
    nb = S // RB
    blk = (lambda ib: nb - 1 - ib) if reverse else (lambda ib: ib)
    q = pl.BlockSpec((RB, dk), lambda h, ib: (blk(ib), h))
    k = pl.BlockSpec((RB, dk), lambda h, ib: (blk(ib), RET_HEADS + h))
    v = pl.BlockSpec((RB, dv), lambda h, ib: (blk(ib), RET_HEADS + h))
    gate = pl.BlockSpec((RB, dv), lambda h, ib: (blk(ib), 2 * RET_HEADS + h))
    yv = pl.BlockSpec((RB, dv), lambda h, ib: (blk(ib), h))
    rope = pl.BlockSpec((RB, dk // 2), lambda h, ib: (blk(ib), 0))
    mask = pl.BlockSpec((None, B, B), lambda h, ib: (h, 0, 0))
    dec = pl.BlockSpec((None, B, 1), lambda h, ib: (h, 0, 0))
    gam = pl.BlockSpec((None, 8, LANES), lambda h, ib: (h, 0, 0))
    gn = pl.BlockSpec((1, dv), lambda h, ib: (0, h))
    return dict(q=q, k=k, v=v, gate=gate, yv=yv, rope=rope, mask=mask, dec=dec, gam=gam, gn=gn)


def _group_norm(yr, gv, bv):
    mu = jnp.mean(yr, axis=-1, keepdims=True)
    cen = yr - mu
    rstd = lax.rsqrt(jnp.mean(cen * cen, axis=-1, keepdims=True) + EPS)
    nrm = cen * rstd
    return nrm, rstd, nrm * gv + bv


def _ret_fwd(proj, tb, gng, gnb, *, name):
    S, D = proj.shape[0], proj.shape[1] // 6
    dk, dv = D // RET_HEADS, 2 * D // RET_HEADS
    B = tb["B"]
    RB = _row_tile(S, 2 * B)
    nsub = RB // B
    sp = _ret_specs(S, D, B, RB, False)
    scale = dk ** -0.5

    def body(q_ref, k_ref, v_ref, gt_ref, cos_ref, sin_ref, mask_ref, xi_ref, zeta_ref, gam_ref, gng_ref, gnb_ref,
             yr_ref, yg_ref, qr_ref, kr_ref, state):
        @pl.when(pl.program_id(1) == 0)
        def _():
            state[...] = jnp.zeros_like(state)

        for sb in range(nsub):
            rows = slice(sb * B, (sb + 1) * B)
            cs, sn = cos_ref[rows, :], sin_ref[rows, :]
            q = _rope(q_ref[rows, :].astype(F32), cs, sn)
            k = _rope(k_ref[rows, :].astype(F32), cs, sn) * scale
            qr_ref[rows, :] = q.astype(BF16)
            kr_ref[rows, :] = k.astype(BF16)
            vb = v_ref[rows, :]
            p = _dot_nt(q, k) * mask_ref[...]
            st = state[...]
            yr = _dot(p, vb) + _dot(q * xi_ref[...], st)
            state[...] = st * gam_ref[0:1, 0:1] + _dot_tn(k * zeta_ref[...], vb)
            _, _, gn = _group_norm(yr, gng_ref[...], gnb_ref[...])
            gt = gt_ref[rows, :].astype(F32)
            yr_ref[rows, :] = yr.astype(BF16)
            yg_ref[rows, :] = (gt * _sigmoid(gt) * gn).astype(BF16)

    return _call(body, name=name, grid=(RET_HEADS, S // RB),
                 in_specs=[sp["q"], sp["k"], sp["v"], sp["gate"], sp["rope"], sp["rope"], sp["mask"], sp["dec"], sp["dec"],
                           sp["gam"], sp["gn"], sp["gn"]],
                 out_specs=(sp["yv"], sp["yv"], sp["q"], sp["q"]),
                 out_shape=(_sds((S, 2 * D), BF16), _sds((S, 2 * D), BF16), _sds((S, D), BF16), _sds((S, D), BF16)),
                 scratch=[pltpu.VMEM((dk, dv), F32)], sem=("parallel", "arbitrary"))(
                     proj, proj, proj, proj, tb["cos"], tb["sin"], tb["mask"], tb["xi"], tb["zeta"], tb["gam"], gng, gnb)


def _ret_bwd_q(proj, kr, yr, dyg, tb, gng, gnb, *, name):
    S, D = proj.shape[0], proj.shape[1] // 6
    dk, dv = D // RET_HEADS, 2 * D // RET_HEADS
    B = tb["B"]
    RB = _row_tile(S, 2 * B)
    nsub = RB // B
    sp = _ret_specs(S, D, B, RB, False)

    def body(k_ref, v_ref, gt_ref, yr_ref, dyg_ref, cos_ref, sin_ref, mask_ref, xi_ref, zeta_ref, gam_ref,
             gng_ref, gnb_ref, dq_ref, dgt_ref, dyr_ref, dgg_ref, dgb_ref, state):
        @pl.when(pl.program_id(1) == 0)
        def _():
            state[...] = jnp.zeros_like(state)
            dgg_ref[...] = jnp.zeros_like(dgg_ref)
            dgb_ref[...] = jnp.zeros_like(dgb_ref)

        for sb in range(nsub):
            rows = slice(sb * B, (sb + 1) * B)
            cs, sn = cos_ref[rows, :], sin_ref[rows, :]
            k = k_ref[rows, :]
            vb = v_ref[rows, :]
            gv = gng_ref[...]
            nrm, rstd, gn = _group_norm(yr_ref[rows, :].astype(F32), gv, gnb_ref[...])
            gt = gt_ref[rows, :].astype(F32)
            sg = _sigmoid(gt)
            dyg = dyg_ref[rows, :].astype(F32)
            dgt_ref[rows, :] = (dyg * gn * (sg * (1.0 + gt * (1.0 - sg)))).astype(BF16)
            dgn = dyg * (gt * sg)
            dgg_ref[...] += jnp.sum(dgn * nrm, axis=0, keepdims=True)
            dgb_ref[...] += jnp.sum(dgn, axis=0, keepdims=True)
            dn = dgn * gv
            dyr = rstd * (dn - jnp.mean(dn, axis=-1, keepdims=True) - nrm * jnp.mean(dn * nrm, axis=-1, keepdims=True))
            dyr_ref[rows, :] = dyr.astype(BF16)
            dp = _dot_nt(dyr, vb) * mask_ref[...]
            st = state[...]
            dq = _dot(dp, k) + _dot_nt(dyr, st) * xi_ref[...]
            dq_ref[rows, :] = _rope_t(dq, cs, sn).astype(BF16)
            state[...] = st * gam_ref[0:1, 0:1] + _dot_tn(k.astype(F32) * zeta_ref[...], vb)

    return _call(body, name=name, grid=(RET_HEADS, S // RB),
                 in_specs=[sp["q"], sp["v"], sp["gate"], sp["yv"], sp["yv"], sp["rope"], sp["rope"], sp["mask"],
                           sp["dec"], sp["dec"], sp["gam"], sp["gn"], sp["gn"]],
                 out_specs=(sp["q"], sp["yv"], sp["yv"], sp["gn"], sp["gn"]),
                 out_shape=(_sds((S, D), BF16), _sds((S, 2 * D), BF16), _sds((S, 2 * D), BF16), _sds((1, 2 * D), F32),
                            _sds((1, 2 * D), F32)),
                 scratch=[pltpu.VMEM((dk, dv), F32)], sem=("parallel", "arbitrary"))(
                     kr, proj, proj, yr, dyg, tb["cos"], tb["sin"], tb["mask"], tb["xi"], tb["zeta"], tb["gam"], gng, gnb)


def _ret_bwd_kv(proj, qr, kr, dyr, dq, dgt, tb, *, name):
    S, D = proj.shape[0], proj.shape[1] // 6
    dk, dv = D // RET_HEADS, 2 * D // RET_HEADS
    B = tb["B"]
    RB = _row_tile(S, 2 * B)
    nsub = RB // B
    nb = S // RB
    scale = dk ** -0.5

    def body(v_ref, qr_ref, kr_ref, dyr_ref, dq_ref, dgt_ref, cos_ref, sin_ref, mask_ref, xi_ref, zeta_ref, gam_ref, out_ref,
             dstate):
        @pl.when(pl.program_id(0) == 0)
        def _():
            dstate[...] = jnp.zeros_like(dstate)

        out_ref[:, 0:D] = dq_ref[...]
        out_ref[:, 4 * D:6 * D] = dgt_ref[...]
        for sb in reversed(range(nsub)):
            rows = slice(sb * B, (sb + 1) * B)
            cs, sn = cos_ref[rows, :], sin_ref[rows, :]
            for h in range(RET_HEADS):
                kcols = slice(D + h * dk, D + (h + 1) * dk)
                vcols = slice(2 * D + h * dv, 2 * D + (h + 1) * dv)
                q = qr_ref[rows, h * dk:(h + 1) * dk]
                k = kr_ref[rows, h * dk:(h + 1) * dk]
                vb = v_ref[rows, h * dv:(h + 1) * dv]
                dyr_h = dyr_ref[rows, h * dv:(h + 1) * dv]
                mk = mask_ref[h]
                p = _dot_nt(q, k) * mk
                dp = _dot_nt(dyr_h, vb) * mk
                ds = dstate[h]
                zt = zeta_ref[h]
                dkr = _dot_tn(dp, q) + _dot_nt(vb, ds) * zt
                out_ref[rows, kcols] = _rope_t(dkr * scale, cs, sn).astype(BF16)
                out_ref[rows, vcols] = (_dot_tn(p, dyr_h) + _dot(k.astype(F32) * zt, ds)).astype(BF16)
                dstate[h] = ds * gam_ref[h, 0:1, 0:1] + _dot_tn(q.astype(F32) * xi_ref[h], dyr_h)

    def rev(width):
        return pl.BlockSpec((RB, width), lambda ib: (nb - 1 - ib, 0))

    def whole(a):
        return pl.BlockSpec(a.shape, lambda ib: (0,) * a.ndim)

    return _call(body, name=name, grid=(nb,),
                 in_specs=[pl.BlockSpec((RB, 2 * D), lambda ib: (nb - 1 - ib, 1)), rev(D), rev(D), rev(2 * D), rev(D), rev(2 * D),
                           rev(dk // 2), rev(dk // 2), whole(tb["mask"]), whole(tb["xi"]), whole(tb["zeta"]), whole(tb["gam"])],
                 out_specs=rev(6 * D), out_shape=_sds((S, 6 * D), BF16), scratch=[pltpu.VMEM((RET_HEADS, dk, dv), F32)],
                 sem=("arbitrary",))(proj, qr, kr, dyr, dq, dgt, tb["cos"], tb["sin"], tb["mask"], tb["xi"], tb["zeta"],
                                     tb["gam"])


def _ada_fwd(c_all, ada_w, *, name):
    L, D, ns = ada_w.shape

    def body(c_ref, w_ref, out_ref):
        cv = c_ref[...]
        cond = cv * _sigmoid(cv)
        out_ref[...] = jnp.dot(cond.astype(BF16), w_ref[...].astype(BF16), preferred_element_type=F32)

    return _call(body, name=name, grid=(L,), in_specs=[pl.BlockSpec((NDEV, D), lambda l: (0, 0)),
                                                      pl.BlockSpec((None, D, ns), lambda l: (l, 0, 0))],
                 out_specs=pl.BlockSpec((None, NDEV, ns), lambda l: (l, 0, 0)), out_shape=_sds((L, NDEV, ns), F32),
                 sem=("parallel",))(c_all, ada_w)


def _ada_bwd(c_all, dmod_cols, *, name):
    L, _, ns = dmod_cols.shape
    D = c_all.shape[1]

    def body(c_ref, d_ref, out_ref):
        cv = c_ref[...]
        cond = cv * _sigmoid(cv)
        out_ref[...] = lax.dot_general(cond.astype(BF16), d_ref[...].astype(BF16), TN_DIMS, preferred_element_type=F32)

    return _call(body, name=name, grid=(L,), in_specs=[pl.BlockSpec((NDEV, D), lambda l: (0, 0)),
                                                      pl.BlockSpec((None, NDEV, ns), lambda l: (l, 0, 0))],
                 out_specs=pl.BlockSpec((None, D, ns), lambda l: (l, 0, 0)), out_shape=_sds((L, D, ns), F32),
                 sem=("parallel",))(c_all, dmod_cols)


def _adamw(w, m, v, parts, *, name):
    shape = w.shape
    L, cols = len(parts), shape[-1]
    rows = w.size // (cols * L)
    n = parts[0].shape[0]
    tr = rows
    for cand in (256, 128, 64, 32, 16, 8):
        if rows % cand == 0:
            tr = cand
            break
    c1 = 1.0 - ADAM_B1 ** ADAM_STEP
    c2 = 1.0 - ADAM_B2 ** ADAM_STEP

    def body(w_ref, m_ref, v_ref, *rest):
        p_refs = rest[:L]
        g_ref, d_ref, m2_ref, v2_ref = rest[L:]
        layer = pl.program_id(0)
        for l in range(L):
            @pl.when(layer == l)
            def _(p_ref=p_refs[l]):
                g = p_ref[0].astype(F32)
                for i in range(1, n):
                    g = g + p_ref[i].astype(F32)
                m2 = ADAM_B1 * m_ref[...] + (1.0 - ADAM_B1) * g
                v2 = ADAM_B2 * v_ref[...] + (1.0 - ADAM_B2) * (g * g)
                g_ref[...] = g
                m2_ref[...] = m2
                v2_ref[...] = v2
                d_ref[...] = -ADAM_LR * ((m2 / c1) / (jnp.sqrt(v2 / c2) + ADAM_EPS) + ADAM_WD * w_ref[...])

    mat = pl.BlockSpec((None, tr, cols), lambda l, i: (l, i, 0))

    def part_spec(k):
        return pl.BlockSpec((n, tr, cols), lambda l, i: (0, jnp.where(l == k, i, 0), 0))

    outs = _call(body, name=name, grid=(L, rows // tr), in_specs=[mat, mat, mat] + [part_spec(k) for k in range(L)],
                 out_specs=(mat, mat, mat, mat), out_shape=tuple(_sds((L, rows, cols), F32) for _ in range(4)),
                 sem=("parallel", "parallel"))(w.reshape(L, rows, cols), m.reshape(L, rows, cols), v.reshape(L, rows, cols),
                                               *[p.reshape(n, rows, cols) for p in parts])
    return tuple(o.reshape(shape) for o in outs)


SMALL = ("ada_b", "norm_mix_g", "norm_mlp_g", "conv_b_pw1", "conv_b_dw", "conv_ln_g", "conv_ln_b", "conv_b_pw2",
         "final_norm_g")
WEIGHTS = ("ada_w", "ada_b", "norm_mix_g", "norm_mlp_g", "conv_w_pw1", "conv_b_pw1", "conv_w_dw", "conv_b_dw", "conv_ln_g",
           "conv_ln_b", "conv_w_pw2", "conv_b_pw2", "ret_w_in", "ret_gn_g", "ret_gn_b", "ret_w_out", "mlp_w1", "mlp_w2",
           "final_norm_g")


def kernel(x, c, ada_w, ada_b, norm_mix_g, norm_mlp_g, conv_w_pw1, conv_b_pw1, conv_w_dw, conv_b_dw, conv_ln_g, conv_ln_b, conv_w_pw2, conv_b_pw2, ret_w_in, ret_gn_g, ret_gn_b, ret_w_out, mlp_w1, mlp_w2, final_norm_g, loss_target, m_ada_w, m_ada_b, m_norm_mix_g, m_norm_mlp_g, m_conv_w_pw1, m_conv_b_pw1, m_conv_w_dw, m_conv_b_dw, m_conv_ln_g, m_conv_ln_b, m_conv_w_pw2, m_conv_b_pw2, m_ret_w_in, m_ret_gn_g, m_ret_gn_b, m_ret_w_out, m_mlp_w1, m_mlp_w2, m_final_norm_g, v_ada_w, v_ada_b, v_norm_mix_g, v_norm_mlp_g, v_conv_w_pw1, v_conv_b_pw1, v_conv_w_dw, v_conv_b_dw, v_conv_ln_g, v_conv_ln_b, v_conv_w_pw2, v_conv_b_pw2, v_ret_w_in, v_ret_gn_g, v_ret_gn_b, v_ret_w_out, v_mlp_w1, v_mlp_w2, v_final_norm_g):
    W = dict(ada_w=ada_w, ada_b=ada_b, norm_mix_g=norm_mix_g, norm_mlp_g=norm_mlp_g, conv_w_pw1=conv_w_pw1,
             conv_b_pw1=conv_b_pw1, conv_w_dw=conv_w_dw, conv_b_dw=conv_b_dw, conv_ln_g=conv_ln_g, conv_ln_b=conv_ln_b,
             conv_w_pw2=conv_w_pw2, conv_b_pw2=conv_b_pw2, ret_w_in=ret_w_in, ret_gn_g=ret_gn_g, ret_gn_b=ret_gn_b,
             ret_w_out=ret_w_out, mlp_w1=mlp_w1, mlp_w2=mlp_w2, final_norm_g=final_norm_g)
    Mo = dict(ada_w=m_ada_w, ada_b=m_ada_b, norm_mix_g=m_norm_mix_g, norm_mlp_g=m_norm_mlp_g, conv_w_pw1=m_conv_w_pw1,
              conv_b_pw1=m_conv_b_pw1, conv_w_dw=m_conv_w_dw, conv_b_dw=m_conv_b_dw, conv_ln_g=m_conv_ln_g,
              conv_ln_b=m_conv_ln_b, conv_w_pw2=m_conv_w_pw2, conv_b_pw2=m_conv_b_pw2, ret_w_in=m_ret_w_in,
              ret_gn_g=m_ret_gn_g, ret_gn_b=m_ret_gn_b, ret_w_out=m_ret_w_out, mlp_w1=m_mlp_w1, mlp_w2=m_mlp_w2,
              final_norm_g=m_final_norm_g)
    Vo = dict(ada_w=v_ada_w, ada_b=v_ada_b, norm_mix_g=v_norm_mix_g, norm_mlp_g=v_norm_mlp_g, conv_w_pw1=v_conv_w_pw1,
              conv_b_pw1=v_conv_b_pw1, conv_w_dw=v_conv_w_dw, conv_b_dw=v_conv_b_dw, conv_ln_g=v_conv_ln_g,
              conv_ln_b=v_conv_ln_b, conv_w_pw2=v_conv_w_pw2, conv_b_pw2=v_conv_b_pw2, ret_w_in=v_ret_w_in,
              ret_gn_g=v_ret_gn_g, ret_gn_b=v_ret_gn_b, ret_w_out=v_ret_w_out, mlp_w1=v_mlp_w1, mlp_w2=v_mlp_w2,
              final_norm_g=v_final_norm_g)

    S, D = x.shape[1], x.shape[2]
    CH = D // LANES
    n_conv, n_ret = conv_w_pw1.shape[0], ret_w_in.shape[0]
    me = 4 * lax.axis_index("x") + 2 * lax.axis_index("y") + lax.axis_index("c")
    xs = x.reshape(S, D)
    target = loss_target.reshape(S, D)

    def mixer_shards(i):
        j = i // 2
        if i % 2 == 0:
            return [[conv_w_pw1[j].astype(BF16)], [conv_w_pw2[j].astype(BF16)]]
        return [[ret_w_in[j].astype(BF16)], [ret_w_out[j].astype(BF16)]]

    def mlp_shards(i):
        return [[mlp_w1[i].astype(BF16)], [mlp_w2[i].astype(BF16)]]

    def mixer_weights(i, got):
        return got[0], got[1].reshape(-1, D)

    def mlp_weights(got):
        return got[0], got[1].reshape(4 * D, D)

    first_handle, _ = _exchange_start(mixer_shards(0)[:1], gather=True, name="gather_start_first")
    small = _exchange([[conv_w_dw], [ret_gn_g], [ret_gn_b], [c]], gather=True, name="gather_small")
    dw_g, gng_g, gnb_g, c_g = small
    dw3 = jnp.transpose(dw_g, (1, 2, 0, 3)).reshape(n_conv, CONV_WIDTH, CH, LANES)
    gng_full = jnp.transpose(gng_g, (1, 2, 0, 3)).reshape(n_ret, 1, 2 * D)
    gnb_full = jnp.transpose(gnb_g, (1, 2, 0, 3)).reshape(n_ret, 1, 2 * D)
    c_all = c_g.reshape(NDEV, D)

    mod_cols = _ada_fwd(c_all, ada_w, name="ada_fwd")
    mod_all = _exchange([[mod_cols]], gather=True, name="gather_mod")[0]
    mod = lax.dynamic_index_in_dim(mod_all, me, axis=2, keepdims=False)
    mod = jnp.transpose(mod, (1, 0, 2)).reshape(DEPTH, 6 * D) + ada_b
    mods = [[mod[i, j * D:(j + 1) * D].reshape(1, D) for j in range(6)] for i in range(DEPTH)]
    tb = _ret_tables(S, D // RET_HEADS)

    def vec(a):
        return a.reshape(1, -1)

    mix_w = (_exchange_wait(first_handle, name="gather_wait_first", after=mod)[0], None)
    handle, token = _exchange_start(mixer_shards(0)[1:] + mlp_shards(0), gather=True, name="gather_start_rest0",
                                    after=mix_w[0])
    saved = []
    weights = []
    xcur = xs
    for i in range(DEPTH):
        sh1, sc1, g1, sh2, sc2, g2 = mods[i]
        j = i // 2
        if i > 0:
            mix_w = mixer_weights(i, _exchange_wait(handle, name=f"gather_wait_mix{i}", after=xcur))
            handle, token = _exchange_start(mlp_shards(i), gather=True, name=f"gather_start_mlp{i}", after=mix_w[0])
        st = dict(x_in=xcur)
        norm1 = (vec(norm_mix_g[i]), sc1, sh1)
        if i % 2 == 0:
            u, h = _mm_nn(xcur, mix_w[0], norm=norm1, bias=vec(conv_b_pw1[j]), out_dtype=F32, name=f"pw1_fwd{i}", after=token)
            u3 = u.reshape(S, 2 * CH, LANES)
            dwo = _conv_mid_fwd(u3, dw3[j], conv_b_dw[j].reshape(1, CH, LANES), name=f"conv_mid_fwd{i}").reshape(S, D)
            if i == 0:
                got = _exchange_wait(handle, name="gather_wait_rest0", after=dwo)
                mix_w, mlp_w = (mix_w[0], got[0].reshape(-1, D)), mlp_weights(got[1:3])
                handle, token = _exchange_start(mixer_shards(1), gather=True, name="gather_start_mix1", after=got[0])
            xcur, y_raw = _mm_nn(dwo, mix_w[1], ln=(vec(conv_ln_g[j]), vec(conv_ln_b[j])), bias=vec(conv_b_pw2[j]), res=xcur,
                                 gate=g1, name=f"pw2_fwd{i}")
            st.update(u3=u3, dwo=dwo, y_raw=y_raw)
        else:
            proj, h = _mm_nn(xcur, mix_w[0], norm=norm1, name=f"ret_in_fwd{i}", after=token)
            yr, yg, qr, kr = _ret_fwd(proj, tb, gng_full[j], gnb_full[j], name=f"ret_fwd{i}")
            xcur, y_raw = _mm_nn(yg, mix_w[1], res=xcur, gate=g1, name=f"ret_out_fwd{i}")
            st.update(proj=proj, yr=yr, yg=yg, qr=qr, kr=kr, y_raw=y_raw)
        st.update(h=h, x_mid=xcur)
        if i > 0:
            mlp_w = mlp_weights(_exchange_wait(handle, name=f"gather_wait_mlp{i}", after=xcur))
            if i + 1 < DEPTH:
                handle, token = _exchange_start(mixer_shards(i + 1), gather=True, name=f"gather_start_mix{i + 1}",
                                                after=mlp_w[0])
        z, h2 = _mm_nn(xcur, mlp_w[0], norm=(vec(norm_mlp_g[i]), sc2, sh2), name=f"mlp1_fwd{i}", after=token)
        xcur, o_raw = _mm_nn(z, mlp_w[1], relu2=True, res=xcur, gate=g2, name=f"mlp2_fwd{i}")
        st.update(h2=h2, z=z, o_raw=o_raw)
        saved.append(st)
        weights.append(mix_w + mlp_w)

    g2_last = mods[DEPTH - 1][5]
    loss_local, dx, d_final_g, dy, dgate, _ = _final_loss(xcur, vec(final_norm_g), target, saved[-1]["o_raw"], g2_last,
                                                          name="final_loss")
    loss = lax.psum(loss_local[0, 0], AXES)

    dmod_rows = [None] * DEPTH
    d_mix_g, d_mlp_g = [None] * DEPTH, [None] * DEPTH
    d_pw1, d_pw2, d_win, d_wout = [None] * n_conv, [None] * n_conv, [None] * n_ret, [None] * n_ret
    d_w1, d_w2 = [None] * DEPTH, [None] * DEPTH
    d_bpw1, d_bdw, d_lng, d_lnb, d_bpw2, d_dw = ([None] * n_conv for _ in range(6))
    d_gng, d_gnb = [None] * n_ret, [None] * n_ret

    def gn_parts(d):
        return jnp.transpose(d.reshape(RET_HEADS, NDEV, -1), (1, 0, 2))

    grad_handles = [None] * DEPTH
    token = None
    for i in reversed(range(DEPTH)):
        sh1, sc1, g1, sh2, sc2, g2 = mods[i]
        j = i // 2
        st = saved[i]
        mix_a, mix_b, w1_i, w2_i = weights[i]
        do, dg2 = dy, dgate
        dz = _mm_nt(do, w2_i, z=st["z"], out_dtype=BF16, name=f"mlp2_bwd_x{i}", after=token)
        d_w2[i] = _mm_tn(st["z"], do, relu2=True, name=f"mlp2_bwd_w{i}")
        dx, dsc2, dsh2, d_mlp_g[i], dy, dg1, dby = _mm_nt(dz, w1_i, norm=(st["x_mid"], vec(norm_mlp_g[i]), sc2, dx),
                                                          gated=(st["y_raw"], g1), name=f"mlp1_bwd_x{i}")
        d_w1[i] = _mm_tn(st["h2"], dz, col_shards=NDEV, name=f"mlp1_bwd_w{i}")
        mlp_groups = [[d_w1[i]], [d_w2[i].reshape(NDEV, 4 * D // NDEV, D)]]
        token = None
        if i == 0:
            mlp0_handle, token = _exchange_start(mlp_groups, gather=False, name="grads_start_mlp0")
            mlp_groups = []
        if i % 2 == 0:
            d_bpw2[j] = dby
            ds = _mm_nt(dy, mix_b, name=f"pw2_bwd_x{i}", after=token)
            ln_gb = (vec(conv_ln_g[j]), vec(conv_ln_b[j]))
            d_pw2[j] = _mm_tn(st["dwo"], dy, ln=ln_gb, name=f"pw2_bwd_w{i}")
            ddw, d_lng[j], d_lnb[j], d_bdw[j] = _ln_silu_bwd(st["dwo"], ds, *ln_gb, name=f"conv_ln_bwd{i}")
            du3, ddw_w, dbu = _conv_mid_bwd_dw(st["u3"], ddw.reshape(S, CH, LANES), dw3[j], name=f"conv_mid_bwd_dw{i}")
            d_dw[j], d_bpw1[j] = ddw_w.reshape(CONV_WIDTH, D), dbu.reshape(2, D)
            du = du3.reshape(S, 2 * D)
            d_pw1[j] = _mm_tn(st["h"], du, col_shards=NDEV, name=f"pw1_bwd_w{i}")
            mix_groups = [[d_pw1[j]], [d_pw2[j].reshape(NDEV, D // NDEV, D)],
                          [jnp.transpose(d_dw[j].reshape(CONV_WIDTH, NDEV, D // NDEV), (1, 0, 2))]]
            mix_in, mix_name = du, f"pw1_bwd_x{i}"
        else:
            dyg = _mm_nt(dy, mix_b, out_dtype=BF16, name=f"ret_out_bwd_x{i}")
            d_wout[j] = _mm_tn(st["yg"], dy, name=f"ret_out_bwd_w{i}")
            dq, dgt, dyr, d_gng[j], d_gnb[j] = _ret_bwd_q(st["proj"], st["kr"], st["yr"], dyg, tb, gng_full[j], gnb_full[j],
                                                          name=f"ret_bwd_q{i}")
            dproj = _ret_bwd_kv(st["proj"], st["qr"], st["kr"], dyr, dq, dgt, tb, name=f"ret_bwd_kv{i}")
            d_win[j] = _mm_tn(st["h"], dproj, col_shards=NDEV, name=f"ret_in_bwd_w{i}")
            mix_in, mix_name = dproj, f"ret_in_bwd_x{i}"
            mix_groups = [[d_win[j]], [d_wout[j].reshape(NDEV, 2 * D // NDEV, D)], [gn_parts(d_gng[j])],
                          [gn_parts(d_gnb[j])]]
        grad_handles[i], token = _exchange_start(mix_groups + mlp_groups, gather=False, name=f"grads_start{i}")
        gated = (saved[i - 1]["o_raw"], mods[i - 1][5]) if i > 0 else None
        outs = _mm_nt(mix_in, mix_a, norm=(st["x_in"], vec(norm_mix_g[i]), sc1, dx), gated=gated, name=mix_name, after=token)
        dx, dsc1, dsh1, d_mix_g[i] = outs[:4]
        if i > 0:
            dy, dgate = outs[4], outs[5]
        dmod_rows[i] = jnp.concatenate([dsh1, dsc1, dg1, dsh2, dsc2, dg2], axis=0)
    grad_x = dx.reshape(1, S, D)

    small_local = jnp.concatenate(dmod_rows + d_mix_g + d_mlp_g + d_bpw1 + d_bdw + d_lng + d_lnb + d_bpw2 + [d_final_g],
                                  axis=0)
    small_all = _exchange([[small_local]], gather=True, name="gather_small_grads")[0]

    def pack(src):
        return jnp.concatenate([src[n].reshape(-1, D) for n in SMALL], axis=0)[None]

    sm = _adamw(pack(W), pack(Mo), pack(Vo), [small_all], name="adamw_small")
    results = {}
    row = 0
    for n in SMALL:
        cnt = W[n].size // D
        results[n] = tuple(o[0, row:row + cnt].reshape(W[n].shape) for o in sm)
        row += cnt

    ns_ada = ada_w.shape[2]
    dmod_all = small_all[:, :6 * DEPTH, :].reshape(NDEV, DEPTH, 6 * D)
    dmod_cols = jnp.transpose(lax.dynamic_slice_in_dim(dmod_all, me * ns_ada, ns_ada, axis=2), (1, 0, 2))
    g_ada = _ada_bwd(c_all, dmod_cols, name="ada_bwd")
    flat_ada = (1, DEPTH * D, ns_ada)
    ada_res = _adamw(ada_w.reshape(flat_ada), m_ada_w.reshape(flat_ada), v_ada_w.reshape(flat_ada),
                     [g_ada.reshape(flat_ada)], name="adamw_ada_w")
    results["ada_w"] = tuple(o.reshape(ada_w.shape) for o in ada_res)

    def update(names, parts):
        for n in names:
            results[n] = _adamw(W[n], Mo[n], Vo[n], parts[n], name=f"adamw_{n}")

    got = {i: _exchange_wait(grad_handles[i], name=f"grads_wait{i}", after=dx) for i in range(DEPTH - 1, 0, -1)}
    ret_layers = [i for i in range(DEPTH) if i % 2 == 1]
    update(("ret_w_in", "ret_w_out", "ret_gn_g", "ret_gn_b"),
           dict(ret_w_in=[got[i][0] for i in ret_layers], ret_w_out=[got[i][1] for i in ret_layers],
                ret_gn_g=[got[i][2] for i in ret_layers], ret_gn_b=[got[i][3] for i in ret_layers]))
    got_mlp0 = _exchange_wait(mlp0_handle, name="grads_wait_mlp0", after=results["ret_w_in"][0])
    update(("mlp_w1", "mlp_w2"),
           dict(mlp_w1=[got_mlp0[0]] + [got[i][-2] for i in range(1, DEPTH)],
                mlp_w2=[got_mlp0[1]] + [got[i][-1] for i in range(1, DEPTH)]))
    got[0] = _exchange_wait(grad_handles[0], name="grads_wait0", after=results["mlp_w1"][0])
    conv_layers = [i for i in range(DEPTH) if i % 2 == 0]
    update(("conv_w_pw1", "conv_w_pw2", "conv_w_dw"),
           dict(conv_w_pw1=[got[i][0] for i in conv_layers], conv_w_pw2=[got[i][1] for i in conv_layers],
                conv_w_dw=[got[i][2] for i in conv_layers]))

    outs = [loss, grad_x]
    for kind in range(4):
        outs += [results[n][kind] for n in WEIGHTS]
    return tuple(outs)
```

```python
import functools

import jax
import jax.numpy as jnp
import numpy as np
from jax import lax
from jax.experimental import pallas as pl
from jax.experimental.pallas import tpu as pltpu

F32, BF16 = jnp.float32, jnp.bfloat16
AXES = ("x", "y", "c")
NDEV = 8
DEPTH = 4
EPS = 1e-6
CHUNK = 64
CONV_WIDTH = 31
HALO = 32
RET_HEADS = 4
RET_BLOCK = 256
ROPE_BASE = 10000.0
LANES = 128
ADAM_LR, ADAM_B1, ADAM_B2, ADAM_EPS, ADAM_WD, ADAM_STEP = 0.001, 0.9, 0.999, 1e-08, 0.01, 10
VMEM_LIMIT = 56 * 1024 * 1024
VMEM_BLOCK_BUDGET = 44 * 1024 * 1024
MESH = pl.DeviceIdType.MESH
NT_DIMS = (((1,), (1,)), ((), ()))
TN_DIMS = (((0,), (0,)), ((), ()))


def _call(body, *, name, out_shape, in_specs, out_specs, grid=(), scratch=(), sem=None, aliases=None):
    params = dict(vmem_limit_bytes=VMEM_LIMIT)
    if sem is not None:
        params["dimension_semantics"] = sem
    return pl.pallas_call(body, name=name, grid=grid, in_specs=in_specs, out_specs=out_specs, out_shape=out_shape,
                          scratch_shapes=list(scratch), input_output_aliases=aliases or {},
                          compiler_params=pltpu.CompilerParams(**params))


def _row_tile(rows, want):
    t = min(rows, want)
    while rows % t:
        t //= 2
    return t


def _sds(shape, dtype):
    return jax.ShapeDtypeStruct(tuple(shape), dtype)


def _sigmoid(v):
    return 1.0 / (1.0 + jnp.exp(-v))


def _exchange(groups, *, gather, name):
    flat = [a for g in groups for a in g]
    n_in = len(flat)
    out_shapes = []
    for g in groups:
        s = g[0].shape if gather else g[0].shape[1:]
        lead = (NDEV,) if len(g) == 1 else (NDEV, len(g))
        out_shapes.append(_sds(lead + tuple(s), g[0].dtype))
    n_g = len(groups)

    def body(*refs):
        ins, outs = refs[:n_in], refs[n_in:n_in + n_g]
        send_sems, recv_sems, loc_sems = refs[n_in + n_g:]
        x, y, c = lax.axis_index("x"), lax.axis_index("y"), lax.axis_index("c")
        me = 4 * x + 2 * y + c
        locs, k = [], 0
        for gi, g in enumerate(groups):
            for li in range(len(g)):
                src = ins[k] if gather else ins[k].at[me]
                dst = outs[gi].at[me] if len(g) == 1 else outs[gi].at[me, li]
                cp = pltpu.make_async_copy(src, dst, loc_sems.at[k])
                cp.start()
                locs.append(cp)
                k += 1
        k0 = 0
        for gi, g in enumerate(groups):
            for r in range(1, NDEV):
                px = 1 - x if r & 4 else x
                py = 1 - y if r & 2 else y
                pc = 1 - c if r & 1 else c
                peer = 4 * px + 2 * py + pc
                for li in range(len(g)):
                    src = ins[k0 + li] if gather else ins[k0 + li].at[peer]
                    dst = outs[gi].at[me] if len(g) == 1 else outs[gi].at[me, li]
                    pltpu.make_async_remote_copy(src_ref=src, dst_ref=dst, send_sem=send_sems.at[gi * (NDEV - 1) + r - 1],
                                                 recv_sem=recv_sems.at[gi * (NDEV - 1) + r - 1], device_id=(px, py, pc),
                                                 device_id_type=MESH).start()
            k0 += len(g)
        for gi, g in enumerate(groups):
            for r in range(1, NDEV):
                px = 1 - x if r & 4 else x
                py = 1 - y if r & 2 else y
                pc = 1 - c if r & 1 else c
                peer = 4 * px + 2 * py + pc
                slab = pltpu.make_async_remote_copy(src_ref=outs[gi].at[me], dst_ref=outs[gi].at[peer],
                                                    send_sem=send_sems.at[gi * (NDEV - 1) + r - 1], recv_sem=recv_sems.at[gi * (NDEV - 1) + r - 1],
                                                    device_id=(px, py, pc), device_id_type=MESH)
                slab.wait_send()
                slab.wait_recv()
        for cp in locs:
            cp.wait()

    hbm = pl.BlockSpec(memory_space=pltpu.HBM)
    outs = _call(body, name=name, out_shape=tuple(out_shapes), in_specs=[hbm] * n_in, out_specs=tuple([hbm] * n_g),
                 scratch=[pltpu.SemaphoreType.DMA((n_g * (NDEV - 1),)), pltpu.SemaphoreType.DMA((n_g * (NDEV - 1),)),
                          pltpu.SemaphoreType.DMA((n_in,))])(*flat)
    return list(outs)


def _peer_of(x, y, c, r):
    return (1 - x if r & 4 else x, 1 - y if r & 2 else y, 1 - c if r & 1 else c)


def _exchange_start(groups, *, gather, name, after=None):
    flat = [pltpu.with_memory_space_constraint(a, pltpu.HBM) for g in groups for a in g]
    n_in, n_g = len(flat), len(groups)
    land_shapes = []
    for g in groups:
        s = g[0].shape if gather else g[0].shape[1:]
        lead = (NDEV,) if len(g) == 1 else (NDEV, len(g))
        land_shapes.append((lead + tuple(s), g[0].dtype))
    lands = [pltpu.with_memory_space_constraint(lax.empty(s, d), pltpu.HBM) for s, d in land_shapes]
    n_after = 0 if after is None else 1

    def body(*refs):
        ins, land = refs[:n_in], refs[n_in:n_in + n_g]
        send_sems, recv_sems, loc_sems = refs[n_in + n_g + n_after:n_in + n_g + n_after + 3]
        token = refs[-1]
        x, y, c = lax.axis_index("x"), lax.axis_index("y"), lax.axis_index("c")
        me = 4 * x + 2 * y + c
        k = 0
        for gi, g in enumerate(groups):
            for li in range(len(g)):
                dst = land[gi].at[me] if len(g) == 1 else land[gi].at[me, li]
                pltpu.make_async_copy(ins[k] if gather else ins[k].at[me], dst, loc_sems.at[k]).start()
                k += 1
        k0 = 0
        for gi, g in enumerate(groups):
            for r in range(1, NDEV):
                px, py, pc = _peer_of(x, y, c, r)
                peer = 4 * px + 2 * py + pc
                for li in range(len(g)):
                    dst = land[gi].at[me] if len(g) == 1 else land[gi].at[me, li]
                    pltpu.make_async_remote_copy(src_ref=ins[k0 + li] if gather else ins[k0 + li].at[peer], dst_ref=dst,
                                                 send_sem=send_sems.at[gi * (NDEV - 1) + r - 1], recv_sem=recv_sems.at[gi * (NDEV - 1) + r - 1],
                                                 device_id=(px, py, pc), device_id_type=MESH).start()
            k0 += len(g)
        token[...] = jnp.zeros_like(token)

    hbm = pl.BlockSpec(memory_space=pltpu.HBM)
    sem = pl.BlockSpec(memory_space=pltpu.SEMAPHORE)
    args = flat + lands + ([after] if n_after else [])
    outs = pl.pallas_call(body, name=name,
        out_shape=(pltpu.SemaphoreType.DMA((n_g * (NDEV - 1),)), pltpu.SemaphoreType.DMA((n_g * (NDEV - 1),)),
                   pltpu.SemaphoreType.DMA((n_in,)), *[pltpu.HBM(a.shape, a.dtype) for a in flat],
                   *[pltpu.HBM(s, d) for s, d in land_shapes], _sds((8, LANES), F32)),
        in_specs=[hbm] * (n_in + n_g) + [pl.BlockSpec(memory_space=pl.ANY)] * n_after,
        out_specs=(sem, sem, sem, *[hbm] * (n_in + n_g), pl.BlockSpec(memory_space=pltpu.VMEM)),
        input_output_aliases={k: 3 + k for k in range(n_in + n_g)},
        compiler_params=pltpu.CompilerParams(has_side_effects=pltpu.SideEffectType.DATAFLOW_SIDE_EFFECTING))(*args)
    handle = dict(sems=outs[0:3], srcs=list(outs[3:3 + n_in]), lands=list(outs[3 + n_in:3 + n_in + n_g]),
                  sizes=[len(g) for g in groups], gather=gather)
    return handle, outs[-1]


def _exchange_wait(handle, *, name, after):
    srcs, lands, sizes, gather = handle["srcs"], handle["lands"], handle["sizes"], handle["gather"]
    n_in, n_g = len(srcs), len(lands)

    def body(*refs):
        ins, land = refs[:n_in], refs[n_in:n_in + n_g]
        send_sems, recv_sems, loc_sems = refs[n_in + n_g:n_in + n_g + 3]
        x, y, c = lax.axis_index("x"), lax.axis_index("y"), lax.axis_index("c")
        me = 4 * x + 2 * y + c
        for gi in range(n_g):
            for r in range(1, NDEV):
                px, py, pc = _peer_of(x, y, c, r)
                peer = 4 * px + 2 * py + pc
                slab = pltpu.make_async_remote_copy(src_ref=land[gi].at[me], dst_ref=land[gi].at[peer],
                                                    send_sem=send_sems.at[gi * (NDEV - 1) + r - 1], recv_sem=recv_sems.at[gi * (NDEV - 1) + r - 1],
                                                    device_id=(px, py, pc), device_id_type=MESH)
                slab.wait_send()
                slab.wait_recv()
        k = 0
        for gi in range(n_g):
            for li in range(sizes[gi]):
                dst = land[gi].at[me] if sizes[gi] == 1 else land[gi].at[me, li]
                pltpu.make_async_copy(ins[k] if gather else ins[k].at[me], dst, loc_sems.at[k]).wait()
                k += 1

    hbm = pl.BlockSpec(memory_space=pltpu.HBM)
    sem = pl.BlockSpec(memory_space=pltpu.SEMAPHORE)
    outs = pl.pallas_call(body, name=name, out_shape=tuple(pltpu.HBM(a.shape, a.dtype) for a in srcs + lands),
        in_specs=[hbm] * (n_in + n_g) + [sem] * 3 + [pl.BlockSpec(memory_space=pl.ANY)],
        out_specs=tuple([hbm] * (n_in + n_g)), input_output_aliases={k: k for k in range(n_in + n_g)},
        compiler_params=pltpu.CompilerParams(has_side_effects=pltpu.SideEffectType.DATAFLOW_SIDE_EFFECTING))(
            *srcs, *lands, *handle["sems"], after)
    return list(outs[n_in:])


def _gate_part(first, dx, y_ref, g_ref, dy_ref, dg_ref, db_ref):
    @pl.when(first)
    def _():
        dg_ref[...] = jnp.zeros_like(dg_ref)
        db_ref[...] = jnp.zeros_like(db_ref)

    dy = dx * g_ref[...]
    dy_ref[...] = dy.astype(BF16)
    dg_ref[...] += jnp.sum(dx * y_ref[...].astype(F32), axis=0, keepdims=True)
    db_ref[...] += jnp.sum(dy, axis=0, keepdims=True)


def _norm_bwd_part(first, dhv, x_ref, g_ref, sc_ref, dres_ref, dx_ref, dsc_ref, dsh_ref, dg_ref):
    @pl.when(first)
    def _():
        dsc_ref[...] = jnp.zeros_like(dsc_ref)
        dsh_ref[...] = jnp.zeros_like(dsh_ref)
        dg_ref[...] = jnp.zeros_like(dg_ref)

    xv = x_ref[...]
    r = lax.rsqrt(jnp.mean(xv * xv, axis=-1, keepdims=True) + EPS)
    xhat = xv * r
    gain_v = g_ref[...]
    dsc_ref[...] += jnp.sum(dhv * (xhat * gain_v), axis=0, keepdims=True)
    dsh_ref[...] += jnp.sum(dhv, axis=0, keepdims=True)
    dxn = dhv * (1.0 + sc_ref[...])
    dg_ref[...] += jnp.sum(dxn * xhat, axis=0, keepdims=True)
    dxhat = dxn * gain_v
    dx = dres_ref[...] + r * (dxhat - xhat * jnp.mean(dxhat * xhat, axis=-1, keepdims=True))
    dx_ref[...] = dx
    return dx


def _final_loss(x, gain, target, y_prev, gate_prev, *, name):
    S, D = x.shape
    tm = _row_tile(S, 512)

    def body(x_ref, g_ref, t_ref, y_ref, gp_ref, loss_ref, dx_ref, dg_ref, dy_ref, dgp_ref, dbp_ref):
        first = pl.program_id(0) == 0

        @pl.when(first)
        def _():
            loss_ref[...] = jnp.zeros_like(loss_ref)
            dg_ref[...] = jnp.zeros_like(dg_ref)

        xv = x_ref[...]
        r = lax.rsqrt(jnp.mean(xv * xv, axis=-1, keepdims=True) + EPS)
        xhat = xv * r
        gv = g_ref[...]
        err = xhat * gv - t_ref[...]
        row_loss = jnp.mean(err * err, axis=-1, keepdims=True)
        loss_ref[...] += 0.5 * jnp.sum(row_loss, axis=0, keepdims=True)
        dy = err * (1.0 / D)
        dg_ref[...] += jnp.sum(dy * xhat, axis=0, keepdims=True)
        dxhat = dy * gv
        dx = r * (dxhat - xhat * jnp.mean(dxhat * xhat, axis=-1, keepdims=True))
        dx_ref[...] = dx
        _gate_part(first, dx, y_ref, gp_ref, dy_ref, dgp_ref, dbp_ref)

    row = pl.BlockSpec((tm, D), lambda i: (i, 0))
    vec = pl.BlockSpec((1, D), lambda i: (0, 0))
    one = pl.BlockSpec((1, 1), lambda i: (0, 0))
    vsh = _sds((1, D), F32)
    return _call(body, name=name, grid=(S // tm,), in_specs=[row, vec, row, row, vec],
                 out_specs=(one, row, vec, row, vec, vec),
                 out_shape=(_sds((1, 1), F32), _sds((S, D), F32), vsh, _sds((S, D), BF16), vsh, vsh),
                 sem=("arbitrary",))(x, gain, target, y_prev, gate_prev)


def _pick_tm(M, bytes_per_row, fixed_bytes):
    for tm in (1024, 512, 256, 128):
        if M % tm == 0 and 2 * tm * bytes_per_row + fixed_bytes <= VMEM_BLOCK_BUDGET:
            return tm
    return _row_tile(M, 128)


def _mm_nn(a, w, *, name, bias=None, relu2=False, ln=None, norm=None, res=None, gate=None, out_dtype=BF16, after=None):
    M, K = a.shape
    col = w.ndim == 3
    if col:
        nsh, ns = w.shape[0], w.shape[2]
        w_spec = pl.BlockSpec((nsh, K, ns), lambda i: (0, 0, 0))
    else:
        nsh, ns = 1, w.shape[1]
        w_spec = pl.BlockSpec((K, ns), lambda i: (0, 0))
    N = nsh * ns
    residual = res is not None
    out_bytes = (4 + 4 + 2) if residual else jnp.dtype(out_dtype).itemsize
    tm = _pick_tm(M, K * a.dtype.itemsize + N * out_bytes + (K * 2 if norm is not None else 0), 2 * K * N * 2)

    def body(*refs):
        it = iter(refs)
        a_ref, w_ref = next(it), next(it)
        b_ref = next(it) if bias is not None else None
        lg_ref, lb_ref = (next(it), next(it)) if ln is not None else (None, None)
        ng_ref, nsc_ref, nsh_ref = (next(it), next(it), next(it)) if norm is not None else (None, None, None)
        res_ref, gate_ref = (next(it), next(it)) if residual else (None, None)
        if after is not None:
            next(it)
        out_ref = next(it)
        raw_ref = next(it) if residual else None
        av = a_ref[...]
        if relu2:
            av = jnp.square(jnp.maximum(av.astype(F32), 0.0))
        if ln is not None:
            av, _ = _ln_silu(av, lg_ref[...], lb_ref[...])
        if norm is not None:
            r = lax.rsqrt(jnp.mean(av * av, axis=-1, keepdims=True) + EPS)
            av = (av * r) * ng_ref[...] * (1.0 + nsc_ref[...]) + nsh_ref[...]
        ab = av.astype(BF16)
        if norm is not None:
            next(it)[...] = ab
        for d in range(nsh):
            cols = slice(d * ns, (d + 1) * ns)
            acc = jnp.dot(ab, w_ref[d] if col else w_ref[...], preferred_element_type=F32)
            if b_ref is not None:
                acc = acc + b_ref[:, cols]
            if residual:
                raw_ref[:, cols] = acc.astype(BF16)
                out_ref[:, cols] = res_ref[:, cols] + gate_ref[:, cols] * acc
            else:
                out_ref[:, cols] = acc.astype(out_dtype)

    tile = pl.BlockSpec((tm, N), lambda i: (i, 0))
    vec = pl.BlockSpec((1, N), lambda i: (0, 0))
    in_specs, args = [pl.BlockSpec((tm, K), lambda i: (i, 0)), w_spec], [a, w]
    if bias is not None:
        in_specs.append(vec)
        args.append(bias)
    if ln is not None:
        in_specs += [pl.BlockSpec((1, K), lambda i: (0, 0))] * 2
        args += list(ln)
    if norm is not None:
        in_specs += [pl.BlockSpec((1, K), lambda i: (0, 0))] * 3
        args += list(norm)
    if residual:
        in_specs += [tile, vec]
        args += [res, gate]
        out_specs = [tile, tile]
        out_shape = [_sds((M, N), F32), _sds((M, N), BF16)]
    else:
        out_specs = [tile]
        out_shape = [_sds((M, N), out_dtype)]
    if after is not None:
        in_specs.append(pl.BlockSpec(memory_space=pl.ANY))
        args.append(after)
    if norm is not None:
        out_specs.append(pl.BlockSpec((tm, K), lambda i: (i, 0)))
        out_shape.append(_sds((M, K), BF16))
    outs = _call(body, name=name, grid=(M // tm,), in_specs=in_specs, out_specs=tuple(out_specs), out_shape=tuple(out_shape),
                 sem=("parallel",))(*args)
    return outs[0] if len(outs) == 1 else outs


def _mm_nt(g, w, *, name, z=None, out_dtype=F32, after=None, norm=None, gated=None):
    M, N = g.shape
    col = w.ndim == 3
    if col:
        nsh, K, ns = w.shape
        w_spec = pl.BlockSpec((nsh, K, ns), lambda i: (0, 0, 0))
    else:
        K = w.shape[0]
        w_spec = pl.BlockSpec((K, N), lambda i: (0, 0))
    assert norm is None or col
    kc = min(K, 1024)
    obytes = jnp.dtype(out_dtype).itemsize
    row_bytes = N * g.dtype.itemsize + K * obytes + (K * 2 if z is not None else 0)
    if norm is not None:
        row_bytes += 2 * K * 4 + (K * 4 if gated is not None else 0)
    tm = _pick_tm(M, row_bytes, 2 * K * N * 2 + 512 * K * 4)

    def body(*refs):
        it = iter(refs)
        g_ref, w_ref = next(it), next(it)
        z_ref = next(it) if z is not None else None
        norm_in = [next(it) for _ in range(4)] if norm is not None else None
        gate_in = [next(it) for _ in range(2)] if gated is not None else None
        if after is not None:
            next(it)
        out_ref = next(it)
        if col:
            acc = None
            for d in range(nsh):
                part = lax.dot_general(g_ref[:, d * ns:(d + 1) * ns].astype(BF16), w_ref[d], NT_DIMS,
                                       preferred_element_type=F32)
                acc = part if acc is None else acc + part
            if norm is None:
                out_ref[...] = acc.astype(out_dtype)
            else:
                first = pl.program_id(0) == 0
                dx = _norm_bwd_part(first, acc, *norm_in, out_ref, next(it), next(it), next(it))
                if gated is not None:
                    _gate_part(first, dx, *gate_in, next(it), next(it), next(it))
        else:
            gb = g_ref[...].astype(BF16)
            for cki in range(K // kc):
                cols = slice(cki * kc, (cki + 1) * kc)
                part = lax.dot_general(gb, w_ref[cols, :], NT_DIMS, preferred_element_type=F32)
                if z_ref is not None:
                    part = part * (2.0 * jnp.maximum(z_ref[:, cols].astype(F32), 0.0))
                out_ref[:, cols] = part.astype(out_dtype)

    row = pl.BlockSpec((tm, K), lambda i: (i, 0))
    vec = pl.BlockSpec((1, K), lambda i: (0, 0))
    vsh = _sds((1, K), F32)
    in_specs, args = [pl.BlockSpec((tm, N), lambda i: (i, 0)), w_spec], [g, w]
    out_specs, out_shape = [row], [_sds((M, K), out_dtype)]
    if z is not None:
        in_specs.append(row)
        args.append(z)
    if norm is not None:
        x, gain, sc, dres = norm
        in_specs += [row, vec, vec, row]
        args += [x, gain, sc, dres]
        out_specs += [vec, vec, vec]
        out_shape += [vsh, vsh, vsh]
    if gated is not None:
        in_specs += [row, vec]
        args += list(gated)
        out_specs += [row, vec, vec]
        out_shape += [_sds((M, K), BF16), vsh, vsh]
    if after is not None:
        in_specs.append(pl.BlockSpec(memory_space=pl.ANY))
        args.append(after)
    outs = _call(body, name=name, grid=(M // tm,), in_specs=in_specs, out_specs=tuple(out_specs), out_shape=tuple(out_shape),
                 sem=("parallel",) if norm is None else ("arbitrary",))(*args)
    return outs[0] if len(outs) == 1 else outs


def _mm_tn(a, g, *, name, col_shards=None, relu2=False, ln=None):
    M, K = a.shape
    N = g.shape[1]
    acc_budget = 8 * 1024 * 1024
    tm = _row_tile(M, 1024)
    nm = M // tm
    if col_shards:
        ns = N // col_shards
        spc = col_shards
        while spc > 1 and K * ns * spc * 4 > acc_budget:
            spc //= 2
        grid = (col_shards // spc, nm)
        a_spec = pl.BlockSpec((tm, K), lambda c, m: (m, 0))
        g_spec = pl.BlockSpec((tm, spc * ns), lambda c, m: (m, c))
        out_spec = pl.BlockSpec((spc, K, ns), lambda c, m: (c, 0, 0))
        out_shape = _sds((col_shards, K, ns), BF16)
        acc_shape = (K, spc * ns)
    else:
        tk = K
        while tk > 128 and tk * N * 4 > acc_budget:
            tk //= 2
        grid = (K // tk, nm)
        a_spec = pl.BlockSpec((tm, tk), lambda c, m: (m, c))
        g_spec = pl.BlockSpec((tm, N), lambda c, m: (m, 0))
        out_spec = pl.BlockSpec((tk, N), lambda c, m: (c, 0))
        out_shape = _sds((K, N), BF16)
        acc_shape = (tk, N)
        assert ln is None or tk == K
    in_specs, args = [a_spec, g_spec], [a, g]
    if ln is not None:
        in_specs += [pl.BlockSpec((1, K), lambda c, m: (0, 0))] * 2
        args += list(ln)

    def body(a_ref, g_ref, *rest):
        out_ref, acc_ref = rest[-2:]
        m = pl.program_id(1)

        @pl.when(m == 0)
        def _():
            acc_ref[...] = jnp.zeros_like(acc_ref)

        av = a_ref[...]
        if relu2:
            av = jnp.square(jnp.maximum(av.astype(F32), 0.0))
        if ln is not None:
            av, _ = _ln_silu(av, rest[0][...], rest[1][...])
        acc_ref[...] += lax.dot_general(av.astype(BF16), g_ref[...].astype(BF16), TN_DIMS, preferred_element_type=F32)

        @pl.when(m == nm - 1)
        def _():
            if col_shards:
                for s in range(spc):
                    out_ref[s] = acc_ref[:, s * ns:(s + 1) * ns].astype(BF16)
            else:
                out_ref[...] = acc_ref[...].astype(BF16)

    return _call(body, name=name, grid=grid, in_specs=in_specs, out_specs=out_spec, out_shape=out_shape,
                 scratch=[pltpu.VMEM(acc_shape, F32)], sem=("parallel", "arbitrary"))(*args)


CONV_TILE = 256
CONV_GROUP = 32


def _glu_rows(u2, ch):
    d = u2.shape[1] // 2
    return (u2[:, :d] * _sigmoid(u2[:, d:])).reshape(u2.shape[0], ch, LANES)


def _fill_glu(buf, u_ref, uh_ref, ch, tile):
    first = pl.program_id(0) == 0
    buf[0:HALO] = jnp.where(first, 0.0, _glu_rows(uh_ref[...], ch))
    buf[HALO:HALO + tile] = _glu_rows(u_ref[...], ch)


CONV_SUB = 4


def _conv_specs(S, D, tile):
    per = tile // HALO
    u_spec = pl.BlockSpec((tile, 2 * D), lambda i: (i, 0))
    uh_spec = pl.BlockSpec((HALO, 2 * D), lambda i: (jnp.maximum(i * per - 1, 0), 0))
    x_spec = pl.BlockSpec((tile, D), lambda i: (i, 0))
    xn_spec = pl.BlockSpec((HALO, D), lambda i: (jnp.minimum((i + 1) * per, S // HALO - 1), 0))
    w_spec = pl.BlockSpec((CONV_WIDTH, D // LANES, LANES), lambda i: (0, 0, 0))
    v_spec = pl.BlockSpec((1, D // LANES, LANES), lambda i: (0, 0, 0))
    return u_spec, uh_spec, x_spec, xn_spec, w_spec, v_spec


def _conv_mid_fwd(u, w3, bdw3, *, name):
    S, D = u.shape[0], u.shape[1] // 2
    ch = D // LANES
    tile = _row_tile(S, CONV_TILE)
    sub = _row_tile(tile, CONV_SUB)
    u_spec, uh_spec, x_spec, _, w_spec, v_spec = _conv_specs(S, D, tile)

    def body(u_ref, uh_ref, w_ref, b_ref, o_ref, buf, stage):
        _fill_glu(buf, u_ref, uh_ref, ch, tile)

        def step(q, carry):
            acc = [b_ref[...], None]
            for k in range(CONV_WIDTH):
                term = buf[pl.ds(q * sub + (HALO - CONV_WIDTH + 1 + k), sub)] * w_ref[k]
                acc[k % 2] = term if acc[k % 2] is None else acc[k % 2] + term
            stage[pl.ds(q * sub, sub)] = acc[0] + acc[1]
            return carry

        lax.fori_loop(0, tile // sub, step, 0)
        o_ref[...] = stage[...].reshape(tile, D)

    return _call(body, name=name, grid=(S // tile,), in_specs=[u_spec, uh_spec, w_spec, v_spec], out_specs=x_spec,
                 out_shape=_sds((S, D), F32),
                 scratch=[pltpu.VMEM((tile + HALO, ch, LANES), F32), pltpu.VMEM((tile, ch, LANES), F32)],
                 sem=("parallel",))(u, u, w3, bdw3)


def _ln_silu(v, gv, bv):
    mu = jnp.mean(v, axis=-1, keepdims=True)
    cen = v - mu
    rstd = lax.rsqrt(jnp.mean(cen * cen, axis=-1, keepdims=True) + EPS)
    nrm = cen * rstd
    ln = nrm * gv + bv
    sg = _sigmoid(ln)
    return ln * sg, (nrm, rstd, ln, sg)


def _ln_silu_bwd(dwo, ds, lng, lnb, *, name):
    S, D = dwo.shape
    tm = _row_tile(S, 512)

    def body(v_ref, ds_ref, g_ref, b_ref, ddw_ref, dg_ref, db_ref, dbdw_ref):
        @pl.when(pl.program_id(0) == 0)
        def _():
            dg_ref[...] = jnp.zeros_like(dg_ref)
            db_ref[...] = jnp.zeros_like(db_ref)
            dbdw_ref[...] = jnp.zeros_like(dbdw_ref)

        gv = g_ref[...]
        _, (nrm, rstd, ln, sg) = _ln_silu(v_ref[...], gv, b_ref[...])
        dln = ds_ref[...] * (sg * (1.0 + ln * (1.0 - sg)))
        dg_ref[...] += jnp.sum(dln * nrm, axis=0, keepdims=True)
        db_ref[...] += jnp.sum(dln, axis=0, keepdims=True)
        dn = dln * gv
        ddw = rstd * (dn - jnp.mean(dn, axis=-1, keepdims=True) - nrm * jnp.mean(dn * nrm, axis=-1, keepdims=True))
        dbdw_ref[...] += jnp.sum(ddw, axis=0, keepdims=True)
        ddw_ref[...] = ddw

    row = pl.BlockSpec((tm, D), lambda i: (i, 0))
    vec = pl.BlockSpec((1, D), lambda i: (0, 0))
    vsh = _sds((1, D), F32)
    return _call(body, name=name, grid=(S // tm,), in_specs=[row, row, vec, vec], out_specs=(row, vec, vec, vec),
                 out_shape=(_sds((S, D), F32), vsh, vsh, vsh), sem=("arbitrary",))(dwo, ds, lng, lnb)


def _conv_mid_bwd_dw(u, ddw, w3, *, name):
    S, D = ddw.shape
    ch = D // LANES
    tile = _row_tile(S, CONV_TILE)
    sub = _row_tile(tile, CONV_SUB)
    last = S // tile - 1
    u_spec, uh_spec, x_spec, xn_spec, w_spec, _ = _conv_specs(S, D, tile)
    b_spec = pl.BlockSpec((1, 2 * D), lambda i: (0, 0))

    def body(u_ref, uh_ref, d_ref, dn_ref, w_ref, du_ref, dw_ref, db_ref, gbuf, dbuf, stage):
        @pl.when(pl.program_id(0) == 0)
        def _():
            dw_ref[...] = jnp.zeros_like(dw_ref)
            db_ref[...] = jnp.zeros_like(db_ref)

        _fill_glu(gbuf, u_ref, uh_ref, ch, tile)
        dbuf[0:tile] = d_ref[...].reshape(tile, ch, LANES)
        dbuf[tile:tile + HALO] = jnp.where(pl.program_id(0) == last, 0.0, dn_ref[...].reshape(HALO, ch, LANES))

        def step(q, c):
            s0 = q * sub
            ddw_q = dbuf[pl.ds(s0, sub)]
            acc = [None, None]
            for k in range(CONV_WIDTH):
                term = dbuf[pl.ds(s0 + (CONV_WIDTH - 1 - k), sub)] * w_ref[k]
                acc[k % 2] = term if acc[k % 2] is None else acc[k % 2] + term
                dw_ref[k] += jnp.sum(ddw_q * gbuf[pl.ds(s0 + (HALO - CONV_WIDTH + 1 + k), sub)], axis=0)
            stage[pl.ds(s0, sub)] = acc[0] + acc[1]
            return c

        lax.fori_loop(0, tile // sub, step, 0)
        dglu = stage[...].reshape(tile, D)
        uv = u_ref[...]
        av, sg = uv[:, :D], _sigmoid(uv[:, D:])
        da = dglu * sg
        dg = da * av * (1.0 - sg)
        du_ref[:, 0:D] = da
        du_ref[:, D:2 * D] = dg
        db_ref[:, 0:D] += jnp.sum(da, axis=0, keepdims=True)
        db_ref[:, D:2 * D] += jnp.sum(dg, axis=0, keepdims=True)

    return _call(body, name=name, grid=(S // tile,), in_specs=[u_spec, uh_spec, x_spec, xn_spec, w_spec],
                 out_specs=(u_spec, w_spec, b_spec),
                 out_shape=(_sds((S, 2 * D), F32), _sds((CONV_WIDTH, ch, LANES), F32), _sds((1, 2 * D), F32)),
                 scratch=[pltpu.VMEM((tile + HALO, ch, LANES), F32), pltpu.VMEM((tile + HALO, ch, LANES), F32),
                          pltpu.VMEM((tile, ch, LANES), F32)],
                 sem=("arbitrary",))(u, u, ddw, ddw, w3)


def _ret_tables(S, dk):
    f32 = np.float32
    B = min(RET_BLOCK, S)
    lg = np.log(f32(1.0) - f32(2.0) ** (f32(-5.0) - np.arange(RET_HEADS, dtype=f32)))
    idx = np.arange(B, dtype=f32)
    diff = idx[:, None] - idx[None, :]
    cq, ck = (np.arange(B) // CHUNK)[:, None], (np.arange(B) // CHUNK)[None, :]
    dist = np.where(cq == ck, np.abs(diff), diff)
    mask = np.where(ck <= cq, np.exp(lg[:, None, None] * dist[None]), f32(0.0)).astype(f32)
    xi = np.exp(lg[:, None] * (idx + f32(1.0)))[..., None].astype(f32)
    zeta = np.exp(lg[:, None] * (f32(B - 1.0) - idx))[..., None].astype(f32)
    gam = np.broadcast_to(np.exp(lg * f32(B))[:, None, None], (RET_HEADS, 8, LANES)).astype(f32)
    pos = np.arange(S, dtype=f32)
    inv = (f32(ROPE_BASE) ** (-np.arange(0, dk, 2, dtype=f32) / f32(dk))).astype(f32)
    ang = (pos[:, None] * inv[None, :]).astype(f32)
    tb = dict(mask=mask, xi=xi, zeta=zeta, gam=gam, cos=np.cos(ang).astype(f32), sin=np.sin(ang).astype(f32))
    return dict(B=B, **{k: jnp.asarray(v) for k, v in tb.items()})


def _rope(v, cs, sn):
    half = v.shape[1] // 2
    v1, v2 = v[:, :half], v[:, half:]
    return jnp.concatenate([v1 * cs - v2 * sn, v2 * cs + v1 * sn], axis=-1)


def _rope_t(d, cs, sn):
    half = d.shape[1] // 2
    d1, d2 = d[:, :half], d[:, half:]
    return jnp.concatenate([d1 * cs + d2 * sn, d2 * cs - d1 * sn], axis=-1)


def _dot(a, b):
    return jnp.dot(a.astype(BF16), b.astype(BF16), preferred_element_type=F32)


def _dot_nt(a, b):
    return lax.dot_general(a.astype(BF16), b.astype(BF16), NT_DIMS, preferred_element_type=F32)


def _dot_tn(a, b):
    return lax.dot_general(a.astype(BF16), b.astype(BF16), TN_DIMS, preferred_element_type=F32)


def _ret_specs(S, D, B, RB, reverse):
    dk, dv = D // RET_HEADS, 2 * D // RET_HEADS
    nb = S // RB
    blk = (lambda ib: nb - 1 - ib) if reverse else (lambda ib: ib)
    q = pl.BlockSpec((RB, dk), lambda h, ib: (blk(ib), h))
    k = pl.BlockSpec((RB, dk), lambda h, ib: (blk(ib), RET_HEADS + h))
    v = pl.BlockSpec((RB, dv), lambda h, ib: (blk(ib), RET_HEADS + h))
    gate = pl.BlockSpec((RB, dv), lambda h, ib: (blk(ib), 2 * RET_HEADS + h))
    yv = pl.BlockSpec((RB, dv), lambda h, ib: (blk(ib), h))
    rope = pl.BlockSpec((RB, dk // 2), lambda h, ib: (blk(ib), 0))
    mask = pl.BlockSpec((None, B, B), lambda h, ib: (h, 0, 0))
    dec = pl.BlockSpec((None, B, 1), lambda h, ib: (h, 0, 0))
    gam = pl.BlockSpec((None, 8, LANES), lambda h, ib: (h, 0, 0))
    gn = pl.BlockSpec((1, dv), lambda h, ib: (0, h))
    return dict(q=q, k=k, v=v, gate=gate, yv=yv, rope=rope, mask=mask, dec=dec, gam=gam, gn=gn)


def _group_norm(yr, gv, bv):
    mu = jnp.mean(yr, axis=-1, keepdims=True)
    cen = yr - mu
    rstd = lax.rsqrt(jnp.mean(cen * cen, axis=-1, keepdims=True) + EPS)
    nrm = cen * rstd
    return nrm, rstd, nrm * gv + bv


def _ret_fwd(proj, tb, gng, gnb, *, name):
    S, D = proj.shape[0], proj.shape[1] // 6
    dk, dv = D // RET_HEADS, 2 * D // RET_HEADS
    B = tb["B"]
    RB = _row_tile(S, 2 * B)
    nsub = RB // B
    sp = _ret_specs(S, D, B, RB, False)
    scale = dk ** -0.5

    def body(q_ref, k_ref, v_ref, gt_ref, cos_ref, sin_ref, mask_ref, xi_ref, zeta_ref, gam_ref, gng_ref, gnb_ref,
             yr_ref, yg_ref, qr_ref, kr_ref, state):
        @pl.when(pl.program_id(1) == 0)
        def _():
            state[...] = jnp.zeros_like(state)

        for sb in range(nsub):
            rows = slice(sb * B, (sb + 1) * B)
            cs, sn = cos_ref[rows, :], sin_ref[rows, :]
            q = _rope(q_ref[rows, :].astype(F32), cs, sn)
            k = _rope(k_ref[rows, :].astype(F32), cs, sn) * scale
            qr_ref[rows, :] = q.astype(BF16)
            kr_ref[rows, :] = k.astype(BF16)
            vb = v_ref[rows, :]
            p = _dot_nt(q, k) * mask_ref[...]
            st = state[...]
            yr = _dot(p, vb) + _dot(q * xi_ref[...], st)
            state[...] = st * gam_ref[0:1, 0:1] + _dot_tn(k * zeta_ref[...], vb)
            _, _, gn = _group_norm(yr, gng_ref[...], gnb_ref[...])
            gt = gt_ref[rows, :].astype(F32)
            yr_ref[rows, :] = yr.astype(BF16)
            yg_ref[rows, :] = (gt * _sigmoid(gt) * gn).astype(BF16)

    return _call(body, name=name, grid=(RET_HEADS, S // RB),
                 in_specs=[sp["q"], sp["k"], sp["v"], sp["gate"], sp["rope"], sp["rope"], sp["mask"], sp["dec"], sp["dec"],
                           sp["gam"], sp["gn"], sp["gn"]],
                 out_specs=(sp["yv"], sp["yv"], sp["q"], sp["q"]),
                 out_shape=(_sds((S, 2 * D), BF16), _sds((S, 2 * D), BF16), _sds((S, D), BF16), _sds((S, D), BF16)),
                 scratch=[pltpu.VMEM((dk, dv), F32)], sem=("parallel", "arbitrary"))(
                     proj, proj, proj, proj, tb["cos"], tb["sin"], tb["mask"], tb["xi"], tb["zeta"], tb["gam"], gng, gnb)


def _ret_bwd_q(proj, kr, yr, dyg, tb, gng, gnb, *, name):
    S, D = proj.shape[0], proj.shape[1] // 6
    dk, dv = D // RET_HEADS, 2 * D // RET_HEADS
    B = tb["B"]
    RB = _row_tile(S, 2 * B)
    nsub = RB // B
    sp = _ret_specs(S, D, B, RB, False)

    def body(k_ref, v_ref, gt_ref, yr_ref, dyg_ref, cos_ref, sin_ref, mask_ref, xi_ref, zeta_ref, gam_ref,
             gng_ref, gnb_ref, dq_ref, dgt_ref, dyr_ref, dgg_ref, dgb_ref, state):
        @pl.when(pl.program_id(1) == 0)
        def _():
            state[...] = jnp.zeros_like(state)
            dgg_ref[...] = jnp.zeros_like(dgg_ref)
            dgb_ref[...] = jnp.zeros_like(dgb_ref)

        for sb in range(nsub):
            rows = slice(sb * B, (sb + 1) * B)
            cs, sn = cos_ref[rows, :], sin_ref[rows, :]
            k = k_ref[rows, :]
            vb = v_ref[rows, :]
            gv = gng_ref[...]
            nrm, rstd, gn = _group_norm(yr_ref[rows, :].astype(F32), gv, gnb_ref[...])
            gt = gt_ref[rows, :].astype(F32)
            sg = _sigmoid(gt)
            dyg = dyg_ref[rows, :].astype(F32)
            dgt_ref[rows, :] = (dyg * gn * (sg * (1.0 + gt * (1.0 - sg)))).astype(BF16)
            dgn = dyg * (gt * sg)
            dgg_ref[...] += jnp.sum(dgn * nrm, axis=0, keepdims=True)
            dgb_ref[...] += jnp.sum(dgn, axis=0, keepdims=True)
            dn = dgn * gv
            dyr = rstd * (dn - jnp.mean(dn, axis=-1, keepdims=True) - nrm * jnp.mean(dn * nrm, axis=-1, keepdims=True))
            dyr_ref[rows, :] = dyr.astype(BF16)
            dp = _dot_nt(dyr, vb) * mask_ref[...]
            st = state[...]
            dq = _dot(dp, k) + _dot_nt(dyr, st) * xi_ref[...]
            dq_ref[rows, :] = _rope_t(dq, cs, sn).astype(BF16)
            state[...] = st * gam_ref[0:1, 0:1] + _dot_tn(k.astype(F32) * zeta_ref[...], vb)

    return _call(body, name=name, grid=(RET_HEADS, S // RB),
                 in_specs=[sp["q"], sp["v"], sp["gate"], sp["yv"], sp["yv"], sp["rope"], sp["rope"], sp["mask"],
                           sp["dec"], sp["dec"], sp["gam"], sp["gn"], sp["gn"]],
                 out_specs=(sp["q"], sp["yv"], sp["yv"], sp["gn"], sp["gn"]),
                 out_shape=(_sds((S, D), BF16), _sds((S, 2 * D), BF16), _sds((S, 2 * D), BF16), _sds((1, 2 * D), F32),
                            _sds((1, 2 * D), F32)),
                 scratch=[pltpu.VMEM((dk, dv), F32)], sem=("parallel", "arbitrary"))(
                     kr, proj, proj, yr, dyg, tb["cos"], tb["sin"], tb["mask"], tb["xi"], tb["zeta"], tb["gam"], gng, gnb)


def _ret_bwd_kv(proj, qr, kr, dyr, dq, dgt, tb, *, name):
    S, D = proj.shape[0], proj.shape[1] // 6
    dk, dv = D // RET_HEADS, 2 * D // RET_HEADS
    B = tb["B"]
    RB = _row_tile(S, 2 * B)
    nsub = RB // B
    nb = S // RB
    scale = dk ** -0.5

    def body(v_ref, qr_ref, kr_ref, dyr_ref, dq_ref, dgt_ref, cos_ref, sin_ref, mask_ref, xi_ref, zeta_ref, gam_ref, out_ref,
             dstate):
        @pl.when(pl.program_id(0) == 0)
        def _():
            dstate[...] = jnp.zeros_like(dstate)

        out_ref[:, 0:D] = dq_ref[...]
        out_ref[:, 4 * D:6 * D] = dgt_ref[...]
        for sb in reversed(range(nsub)):
            rows = slice(sb * B, (sb + 1) * B)
            cs, sn = cos_ref[rows, :], sin_ref[rows, :]
            for h in range(RET_HEADS):
                kcols = slice(D + h * dk, D + (h + 1) * dk)
                vcols = slice(2 * D + h * dv, 2 * D + (h + 1) * dv)
                q = qr_ref[rows, h * dk:(h + 1) * dk]
                k = kr_ref[rows, h * dk:(h + 1) * dk]
                vb = v_ref[rows, h * dv:(h + 1) * dv]
                dyr_h = dyr_ref[rows, h * dv:(h + 1) * dv]
                mk = mask_ref[h]
                p = _dot_nt(q, k) * mk
                dp = _dot_nt(dyr_h, vb) * mk
                ds = dstate[h]
                zt = zeta_ref[h]
                dkr = _dot_tn(dp, q) + _dot_nt(vb, ds) * zt
                out_ref[rows, kcols] = _rope_t(dkr * scale, cs, sn).astype(BF16)
                out_ref[rows, vcols] = (_dot_tn(p, dyr_h) + _dot(k.astype(F32) * zt, ds)).astype(BF16)
                dstate[h] = ds * gam_ref[h, 0:1, 0:1] + _dot_tn(q.astype(F32) * xi_ref[h], dyr_h)

    def rev(width):
        return pl.BlockSpec((RB, width), lambda ib: (nb - 1 - ib, 0))

    def whole(a):
        return pl.BlockSpec(a.shape, lambda ib: (0,) * a.ndim)

    return _call(body, name=name, grid=(nb,),
                 in_specs=[pl.BlockSpec((RB, 2 * D), lambda ib: (nb - 1 - ib, 1)), rev(D), rev(D), rev(2 * D), rev(D), rev(2 * D),
                           rev(dk // 2), rev(dk // 2), whole(tb["mask"]), whole(tb["xi"]), whole(tb["zeta"]), whole(tb["gam"])],
                 out_specs=rev(6 * D), out_shape=_sds((S, 6 * D), BF16), scratch=[pltpu.VMEM((RET_HEADS, dk, dv), F32)],
                 sem=("arbitrary",))(proj, qr, kr, dyr, dq, dgt, tb["cos"], tb["sin"], tb["mask"], tb["xi"], tb["zeta"],
                                     tb["gam"])


def _ada_fwd(c_all, ada_w, *, name):
    L, D, ns = ada_w.shape

    def body(c_ref, w_ref, out_ref):
        cv = c_ref[...]
        cond = cv * _sigmoid(cv)
        out_ref[...] = jnp.dot(cond.astype(BF16), w_ref[...].astype(BF16), preferred_element_type=F32)

    return _call(body, name=name, grid=(L,), in_specs=[pl.BlockSpec((NDEV, D), lambda l: (0, 0)),
                                                      pl.BlockSpec((None, D, ns), lambda l: (l, 0, 0))],
                 out_specs=pl.BlockSpec((None, NDEV, ns), lambda l: (l, 0, 0)), out_shape=_sds((L, NDEV, ns), F32),
                 sem=("parallel",))(c_all, ada_w)


def _ada_bwd(c_all, dmod_cols, *, name):
    L, _, ns = dmod_cols.shape
    D = c_all.shape[1]

    def body(c_ref, d_ref, out_ref):
        cv = c_ref[...]
        cond = cv * _sigmoid(cv)
        out_ref[...] = lax.dot_general(cond.astype(BF16), d_ref[...].astype(BF16), TN_DIMS, preferred_element_type=F32)

    return _call(body, name=name, grid=(L,), in_specs=[pl.BlockSpec((NDEV, D), lambda l: (0, 0)),
                                                      pl.BlockSpec((None, NDEV, ns), lambda l: (l, 0, 0))],
                 out_specs=pl.BlockSpec((None, D, ns), lambda l: (l, 0, 0)), out_shape=_sds((L, D, ns), F32),
                 sem=("parallel",))(c_all, dmod_cols)


def _adamw(w, m, v, parts, *, name):
    shape = w.shape
    L, cols = len(parts), shape[-1]
    rows = w.size // (cols * L)
    n = parts[0].shape[0]
    tr = rows
    for cand in (256, 128, 64, 32, 16, 8):
        if rows % cand == 0:
            tr = cand
            break
    c1 = 1.0 - ADAM_B1 ** ADAM_STEP
    c2 = 1.0 - ADAM_B2 ** ADAM_STEP

    def body(w_ref, m_ref, v_ref, *rest):
        p_refs = rest[:L]
        g_ref, d_ref, m2_ref, v2_ref = rest[L:]
        layer = pl.program_id(0)
        for l in range(L):
            @pl.when(layer == l)
            def _(p_ref=p_refs[l]):
                g = p_ref[0].astype(F32)
                for i in range(1, n):
                    g = g + p_ref[i].astype(F32)
                m2 = ADAM_B1 * m_ref[...] + (1.0 - ADAM_B1) * g
                v2 = ADAM_B2 * v_ref[...] + (1.0 - ADAM_B2) * (g * g)
                g_ref[...] = g
                m2_ref[...] = m2
                v2_ref[...] = v2
                d_ref[...] = -ADAM_LR * ((m2 / c1) / (jnp.sqrt(v2 / c2) + ADAM_EPS) + ADAM_WD * w_ref[...])

    mat = pl.BlockSpec((None, tr, cols), lambda l, i: (l, i, 0))

    def part_spec(k):
        return pl.BlockSpec((n, tr, cols), lambda l, i: (0, jnp.where(l == k, i, 0), 0))

    outs = _call(body, name=name, grid=(L, rows // tr), in_specs=[mat, mat, mat] + [part_spec(k) for k in range(L)],
                 out_specs=(mat, mat, mat, mat), out_shape=tuple(_sds((L, rows, cols), F32) for _ in range(4)),
                 sem=("parallel", "parallel"))(w.reshape(L, rows, cols), m.reshape(L, rows, cols), v.reshape(L, rows, cols),
                                               *[p.reshape(n, rows, cols) for p in parts])
    return tuple(o.reshape(shape) for o in outs)


SMALL = ("ada_b", "norm_mix_g", "norm_mlp_g", "conv_b_pw1", "conv_b_dw", "conv_ln_g", "conv_ln_b", "conv_b_pw2",
         "final_norm_g")
WEIGHTS = ("ada_w", "ada_b", "norm_mix_g", "norm_mlp_g", "conv_w_pw1", "conv_b_pw1", "conv_w_dw", "conv_b_dw", "conv_ln_g",
           "conv_ln_b", "conv_w_pw2", "conv_b_pw2", "ret_w_in", "ret_gn_g", "ret_gn_b", "ret_w_out", "mlp_w1", "mlp_w2",
           "final_norm_g")


def kernel(x, c, ada_w, ada_b, norm_mix_g, norm_mlp_g, conv_w_pw1, conv_b_pw1, conv_w_dw, conv_b_dw, conv_ln_g, conv_ln_b, conv_w_pw2, conv_b_pw2, ret_w_in, ret_gn_g, ret_gn_b, ret_w_out, mlp_w1, mlp_w2, final_norm_g, loss_target, m_ada_w, m_ada_b, m_norm_mix_g, m_norm_mlp_g, m_conv_w_pw1, m_conv_b_pw1, m_conv_w_dw, m_conv_b_dw, m_conv_ln_g, m_conv_ln_b, m_conv_w_pw2, m_conv_b_pw2, m_ret_w_in, m_ret_gn_g, m_ret_gn_b, m_ret_w_out, m_mlp_w1, m_mlp_w2, m_final_norm_g, v_ada_w, v_ada_b, v_norm_mix_g, v_norm_mlp_g, v_conv_w_pw1, v_conv_b_pw1, v_conv_w_dw, v_conv_b_dw, v_conv_ln_g, v_conv_ln_b, v_conv_w_pw2, v_conv_b_pw2, v_ret_w_in, v_ret_gn_g, v_ret_gn_b, v_ret_w_out, v_mlp_w1, v_mlp_w2, v_final_norm_g):
    W = dict(ada_w=ada_w, ada_b=ada_b, norm_mix_g=norm_mix_g, norm_mlp_g=norm_mlp_g, conv_w_pw1=conv_w_pw1,
             conv_b_pw1=conv_b_pw1, conv_w_dw=conv_w_dw, conv_b_dw=conv_b_dw, conv_ln_g=conv_ln_g, conv_ln_b=conv_ln_b,
             conv_w_pw2=conv_w_pw2, conv_b_pw2=conv_b_pw2, ret_w_in=ret_w_in, ret_gn_g=ret_gn_g, ret_gn_b=ret_gn_b,
             ret_w_out=ret_w_out, mlp_w1=mlp_w1, mlp_w2=mlp_w2, final_norm_g=final_norm_g)
    Mo = dict(ada_w=m_ada_w, ada_b=m_ada_b, norm_mix_g=m_norm_mix_g, norm_mlp_g=m_norm_mlp_g, conv_w_pw1=m_conv_w_pw1,
              conv_b_pw1=m_conv_b_pw1, conv_w_dw=m_conv_w_dw, conv_b_dw=m_conv_b_dw, conv_ln_g=m_conv_ln_g,
              conv_ln_b=m_conv_ln_b, conv_w_pw2=m_conv_w_pw2, conv_b_pw2=m_conv_b_pw2, ret_w_in=m_ret_w_in,
              ret_gn_g=m_ret_gn_g, ret_gn_b=m_ret_gn_b, ret_w_out=m_ret_w_out, mlp_w1=m_mlp_w1, mlp_w2=m_mlp_w2,
              final_norm_g=m_final_norm_g)
    Vo = dict(ada_w=v_ada_w, ada_b=v_ada_b, norm_mix_g=v_norm_mix_g, norm_mlp_g=v_norm_mlp_g, conv_w_pw1=v_conv_w_pw1,
              conv_b_pw1=v_conv_b_pw1, conv_w_dw=v_conv_w_dw, conv_b_dw=v_conv_b_dw, conv_ln_g=v_conv_ln_g,
              conv_ln_b=v_conv_ln_b, conv_w_pw2=v_conv_w_pw2, conv_b_pw2=v_conv_b_pw2, ret_w_in=v_ret_w_in,
              ret_gn_g=v_ret_gn_g, ret_gn_b=v_ret_gn_b, ret_w_out=v_ret_w_out, mlp_w1=v_mlp_w1, mlp_w2=v_mlp_w2,
              final_norm_g=v_final_norm_g)

    S, D = x.shape[1], x.shape[2]
    CH = D // LANES
    n_conv, n_ret = conv_w_pw1.shape[0], ret_w_in.shape[0]
    me = 4 * lax.axis_index("x") + 2 * lax.axis_index("y") + lax.axis_index("c")
    xs = x.reshape(S, D)
    target = loss_target.reshape(S, D)

    def mixer_shards(i):
        j = i // 2
        if i % 2 == 0:
            return [[conv_w_pw1[j].astype(BF16)], [conv_w_pw2[j].astype(BF16)]]
        return [[ret_w_in[j].astype(BF16)], [ret_w_out[j].astype(BF16)]]

    def mlp_shards(i):
        return [[mlp_w1[i].astype(BF16)], [mlp_w2[i].astype(BF16)]]

    def mixer_weights(i, got):
        return got[0], got[1].reshape(-1, D)

    def mlp_weights(got):
        return got[0], got[1].reshape(4 * D, D)

    first_handle, _ = _exchange_start(mixer_shards(0)[:1], gather=True, name="gather_start_first")
    small = _exchange([[conv_w_dw], [ret_gn_g], [ret_gn_b], [c]], gather=True, name="gather_small")
    dw_g, gng_g, gnb_g, c_g = small
    dw3 = jnp.transpose(dw_g, (1, 2, 0, 3)).reshape(n_conv, CONV_WIDTH, CH, LANES)
    gng_full = jnp.transpose(gng_g, (1, 2, 0, 3)).reshape(n_ret, 1, 2 * D)
    gnb_full = jnp.transpose(gnb_g, (1, 2, 0, 3)).reshape(n_ret, 1, 2 * D)
    c_all = c_g.reshape(NDEV, D)

    mod_cols = _ada_fwd(c_all, ada_w, name="ada_fwd")
    mod_all = _exchange([[mod_cols]], gather=True, name="gather_mod")[0]
    mod = lax.dynamic_index_in_dim(mod_all, me, axis=2, keepdims=False)
    mod = jnp.transpose(mod, (1, 0, 2)).reshape(DEPTH, 6 * D) + ada_b
    mods = [[mod[i, j * D:(j + 1) * D].reshape(1, D) for j in range(6)] for i in range(DEPTH)]
    tb = _ret_tables(S, D // RET_HEADS)

    def vec(a):
        return a.reshape(1, -1)

    mix_w = (_exchange_wait(first_handle, name="gather_wait_first", after=mod)[0], None)
    handle, token = _exchange_start(mixer_shards(0)[1:] + mlp_shards(0), gather=True, name="gather_start_rest0",
                                    after=mix_w[0])
    saved = []
    weights = []
    xcur = xs
    for i in range(DEPTH):
        sh1, sc1, g1, sh2, sc2, g2 = mods[i]
        j = i // 2
        if i > 0:
            mix_w = mixer_weights(i, _exchange_wait(handle, name=f"gather_wait_mix{i}", after=xcur))
            handle, token = _exchange_start(mlp_shards(i), gather=True, name=f"gather_start_mlp{i}", after=mix_w[0])
        st = dict(x_in=xcur)
        norm1 = (vec(norm_mix_g[i]), sc1, sh1)
        if i % 2 == 0:
            u, h = _mm_nn(xcur, mix_w[0], norm=norm1, bias=vec(conv_b_pw1[j]), out_dtype=F32, name=f"pw1_fwd{i}", after=token)
            dwo = _conv_mid_fwd(u, dw3[j], conv_b_dw[j].reshape(1, CH, LANES), name=f"conv_mid_fwd{i}")
            if i == 0:
                got = _exchange_wait(handle, name="gather_wait_rest0", after=dwo)
                mix_w, mlp_w = (mix_w[0], got[0].reshape(-1, D)), mlp_weights(got[1:3])
                handle, token = _exchange_start(mixer_shards(1), gather=True, name="gather_start_mix1", after=got[0])
            xcur, y_raw = _mm_nn(dwo, mix_w[1], ln=(vec(conv_ln_g[j]), vec(conv_ln_b[j])), bias=vec(conv_b_pw2[j]), res=xcur,
                                 gate=g1, name=f"pw2_fwd{i}")
            st.update(u=u, dwo=dwo, y_raw=y_raw)
        else:
            proj, h = _mm_nn(xcur, mix_w[0], norm=norm1, name=f"ret_in_fwd{i}", after=token)
            yr, yg, qr, kr = _ret_fwd(proj, tb, gng_full[j], gnb_full[j], name=f"ret_fwd{i}")
            xcur, y_raw = _mm_nn(yg, mix_w[1], res=xcur, gate=g1, name=f"ret_out_fwd{i}")
            st.update(proj=proj, yr=yr, yg=yg, qr=qr, kr=kr, y_raw=y_raw)
        st.update(h=h, x_mid=xcur)
        if i > 0:
            mlp_w = mlp_weights(_exchange_wait(handle, name=f"gather_wait_mlp{i}", after=xcur))
            if i + 1 < DEPTH:
                handle, token = _exchange_start(mixer_shards(i + 1), gather=True, name=f"gather_start_mix{i + 1}",
                                                after=mlp_w[0])
        z, h2 = _mm_nn(xcur, mlp_w[0], norm=(vec(norm_mlp_g[i]), sc2, sh2), name=f"mlp1_fwd{i}", after=token)
        xcur, o_raw = _mm_nn(z, mlp_w[1], relu2=True, res=xcur, gate=g2, name=f"mlp2_fwd{i}")
        st.update(h2=h2, z=z, o_raw=o_raw)
        saved.append(st)
        weights.append(mix_w + mlp_w)

    g2_last = mods[DEPTH - 1][5]
    loss_local, dx, d_final_g, dy, dgate, _ = _final_loss(xcur, vec(final_norm_g), target, saved[-1]["o_raw"], g2_last,
                                                          name="final_loss")
    loss = lax.psum(loss_local[0, 0], AXES)

    dmod_rows = [None] * DEPTH
    d_mix_g, d_mlp_g = [None] * DEPTH, [None] * DEPTH
    d_pw1, d_pw2, d_win, d_wout = [None] * n_conv, [None] * n_conv, [None] * n_ret, [None] * n_ret
    d_w1, d_w2 = [None] * DEPTH, [None] * DEPTH
    d_bpw1, d_bdw, d_lng, d_lnb, d_bpw2, d_dw = ([None] * n_conv for _ in range(6))
    d_gng, d_gnb = [None] * n_ret, [None] * n_ret

    def gn_parts(d):
        return jnp.transpose(d.reshape(RET_HEADS, NDEV, -1), (1, 0, 2))

    grad_handles = [None] * DEPTH
    token = None
    for i in reversed(range(DEPTH)):
        sh1, sc1, g1, sh2, sc2, g2 = mods[i]
        j = i // 2
        st = saved[i]
        mix_a, mix_b, w1_i, w2_i = weights[i]
        do, dg2 = dy, dgate
        dz = _mm_nt(do, w2_i, z=st["z"], out_dtype=BF16, name=f"mlp2_bwd_x{i}", after=token)
        d_w2[i] = _mm_tn(st["z"], do, relu2=True, name=f"mlp2_bwd_w{i}")
        dx, dsc2, dsh2, d_mlp_g[i], dy, dg1, dby = _mm_nt(dz, w1_i, norm=(st["x_mid"], vec(norm_mlp_g[i]), sc2, dx),
                                                          gated=(st["y_raw"], g1), name=f"mlp1_bwd_x{i}")
        d_w1[i] = _mm_tn(st["h2"], dz, col_shards=NDEV, name=f"mlp1_bwd_w{i}")
        mlp_groups = [[d_w1[i]], [d_w2[i].reshape(NDEV, 4 * D // NDEV, D)]]
        token = None
        if i == 0:
            mlp0_handle, token = _exchange_start(mlp_groups, gather=False, name="grads_start_mlp0")
            mlp_groups = []
        if i % 2 == 0:
            d_bpw2[j] = dby
            ds = _mm_nt(dy, mix_b, name=f"pw2_bwd_x{i}", after=token)
            ln_gb = (vec(conv_ln_g[j]), vec(conv_ln_b[j]))
            d_pw2[j] = _mm_tn(st["dwo"], dy, ln=ln_gb, name=f"pw2_bwd_w{i}")
            ddw, d_lng[j], d_lnb[j], d_bdw[j] = _ln_silu_bwd(st["dwo"], ds, *ln_gb, name=f"conv_ln_bwd{i}")
            du, ddw_w, dbu = _conv_mid_bwd_dw(st["u"], ddw, dw3[j], name=f"conv_mid_bwd_dw{i}")
            d_dw[j], d_bpw1[j] = ddw_w.reshape(CONV_WIDTH, D), dbu.reshape(2, D)
            d_pw1[j] = _mm_tn(st["h"], du, col_shards=NDEV, name=f"pw1_bwd_w{i}")
            mix_groups = [[d_pw1[j]], [d_pw2[j].reshape(NDEV, D // NDEV, D)],
                          [jnp.transpose(d_dw[j].reshape(CONV_WIDTH, NDEV, D // NDEV), (1, 0, 2))]]
            mix_in, mix_name = du, f"pw1_bwd_x{i}"
        else:
            dyg = _mm_nt(dy, mix_b, out_dtype=BF16, name=f"ret_out_bwd_x{i}")
            d_wout[j] = _mm_tn(st["yg"], dy, name=f"ret_out_bwd_w{i}")
            dq, dgt, dyr, d_gng[j], d_gnb[j] = _ret_bwd_q(st["proj"], st["kr"], st["yr"], dyg, tb, gng_full[j], gnb_full[j],
                                                          name=f"ret_bwd_q{i}")
            dproj = _ret_bwd_kv(st["proj"], st["qr"], st["kr"], dyr, dq, dgt, tb, name=f"ret_bwd_kv{i}")
            d_win[j] = _mm_tn(st["h"], dproj, col_shards=NDEV, name=f"ret_in_bwd_w{i}")
            mix_in, mix_name = dproj, f"ret_in_bwd_x{i}"
            mix_groups = [[d_win[j]], [d_wout[j].reshape(NDEV, 2 * D // NDEV, D)], [gn_parts(d_gng[j])],
                          [gn_parts(d_gnb[j])]]
        grad_handles[i], token = _exchange_start(mix_groups + mlp_groups, gather=False, name=f"grads_start{i}")
        gated = (saved[i - 1]["o_raw"], mods[i - 1][5]) if i > 0 else None
        outs = _mm_nt(mix_in, mix_a, norm=(st["x_in"], vec(norm_mix_g[i]), sc1, dx), gated=gated, name=mix_name, after=token)
        dx, dsc1, dsh1, d_mix_g[i] = outs[:4]
        if i > 0:
            dy, dgate = outs[4], outs[5]
        dmod_rows[i] = jnp.concatenate([dsh1, dsc1, dg1, dsh2, dsc2, dg2], axis=0)
    grad_x = dx.reshape(1, S, D)

    small_local = jnp.concatenate(dmod_rows + d_mix_g + d_mlp_g + d_bpw1 + d_bdw + d_lng + d_lnb + d_bpw2 + [d_final_g],
                                  axis=0)
    small_all = _exchange([[small_local]], gather=True, name="gather_small_grads")[0]

    def pack(src):
        return jnp.concatenate([src[n].reshape(-1, D) for n in SMALL], axis=0)[None]

    sm = _adamw(pack(W), pack(Mo), pack(Vo), [small_all], name="adamw_small")
    results = {}
    row = 0
    for n in SMALL:
        cnt = W[n].size // D
        results[n] = tuple(o[0, row:row + cnt].reshape(W[n].shape) for o in sm)
        row += cnt

    ns_ada = ada_w.shape[2]
    dmod_all = small_all[:, :6 * DEPTH, :].reshape(NDEV, DEPTH, 6 * D)
    dmod_cols = jnp.transpose(lax.dynamic_slice_in_dim(dmod_all, me * ns_ada, ns_ada, axis=2), (1, 0, 2))
    g_ada = _ada_bwd(c_all, dmod_cols, name="ada_bwd")
    flat_ada = (1, DEPTH * D, ns_ada)
    ada_res = _adamw(ada_w.reshape(flat_ada), m_ada_w.reshape(flat_ada), v_ada_w.reshape(flat_ada),
                     [g_ada.reshape(flat_ada)], name="adamw_ada_w")
    results["ada_w"] = tuple(o.reshape(ada_w.shape) for o in ada_res)

    def update(names, parts):
        for n in names:
            results[n] = _adamw(W[n], Mo[n], Vo[n], parts[n], name=f"adamw_{n}")

    got = {i: _exchange_wait(grad_handles[i], name=f"grads_wait{i}", after=dx) for i in range(DEPTH - 1, 0, -1)}
    ret_layers = [i for i in range(DEPTH) if i % 2 == 1]
    update(("ret_w_in", "ret_w_out", "ret_gn_g", "ret_gn_b"),
           dict(ret_w_in=[got[i][0] for i in ret_layers], ret_w_out=[got[i][1] for i in ret_layers],
                ret_gn_g=[got[i][2] for i in ret_layers], ret_gn_b=[got[i][3] for i in ret_layers]))
    got_mlp0 = _exchange_wait(mlp0_handle, name="grads_wait_mlp0", after=results["ret_w_in"][0])
    update(("mlp_w1", "mlp_w2"),
           dict(mlp_w1=[got_mlp0[0]] + [got[i][-2] for i in range(1, DEPTH)],
                mlp_w2=[got_mlp0[1]] + [got[i][-1] for i in range(1, DEPTH)]))
    got[0] = _exchange_wait(grad_handles[0], name="grads_wait0", after=results["mlp_w1"][0])
    conv_layers = [i for i in range(DEPTH) if i % 2 == 0]
    update(("conv_w_pw1", "conv_w_pw2", "conv_w_dw"),
           dict(conv_w_pw1=[got[i][0] for i in conv_layers], conv_w_pw2=[got[i][1] for i in conv_layers],
                conv_w_dw=[got[i][2] for i in conv_layers]))

    outs = [loss, grad_x]
    for kind in range(4):
        outs += [results[n][kind] for n in WEIGHTS]
    return tuple(outs)
```

```python
import functools

import jax
import jax.numpy as jnp
import numpy as np
from jax import lax
from jax.experimental import pallas as pl
from jax.experimental.pallas import tpu as pltpu

F32, BF16 = jnp.float32, jnp.bfloat16
AXES = ("x", "y", "c")
NDEV = 8
DEPTH = 4
EPS = 1e-6
CHUNK = 64
CONV_WIDTH = 31
HALO = 32
RET_HEADS = 4
RET_BLOCK = 256
ROPE_BASE = 10000.0
LANES = 128
ADAM_LR, ADAM_B1, ADAM_B2, ADAM_EPS, ADAM_WD, ADAM_STEP = 0.001, 0.9, 0.999, 1e-08, 0.01, 10
VMEM_LIMIT = 56 * 1024 * 1024
VMEM_BLOCK_BUDGET = 44 * 1024 * 1024
MESH = pl.DeviceIdType.MESH
NT_DIMS = (((1,), (1,)), ((), ()))
TN_DIMS = (((0,), (0,)), ((), ()))


def _call(body, *, name, out_shape, in_specs, out_specs, grid=(), scratch=(), sem=None, aliases=None):
    params = dict(vmem_limit_bytes=VMEM_LIMIT)
    if sem is not None:
        params["dimension_semantics"] = sem
    return pl.pallas_call(body, name=name, grid=grid, in_specs=in_specs, out_specs=out_specs, out_shape=out_shape,
                          scratch_shapes=list(scratch), input_output_aliases=aliases or {},
                          compiler_params=pltpu.CompilerParams(**params))


def _row_tile(rows, want):
    t = min(rows, want)
    while rows % t:
        t //= 2
    return t


def _sds(shape, dtype):
    return jax.ShapeDtypeStruct(tuple(shape), dtype)


def _sigmoid(v):
    return 1.0 / (1.0 + jnp.exp(-v))


def _exchange(groups, *, gather, name):
    flat = [a for g in groups for a in g]
    n_in = len(flat)
    out_shapes = []
    for g in groups:
        s = g[0].shape if gather else g[0].shape[1:]
        lead = (NDEV,) if len(g) == 1 else (NDEV, len(g))
        out_shapes.append(_sds(lead + tuple(s), g[0].dtype))
    n_g = len(groups)

    def body(*refs):
        ins, outs = refs[:n_in], refs[n_in:n_in + n_g]
        send_sems, recv_sems, loc_sems = refs[n_in + n_g:]
        x, y, c = lax.axis_index("x"), lax.axis_index("y"), lax.axis_index("c")
        me = 4 * x + 2 * y + c
        locs, k = [], 0
        for gi, g in enumerate(groups):
            for li in range(len(g)):
                src = ins[k] if gather else ins[k].at[me]
                dst = outs[gi].at[me] if len(g) == 1 else outs[gi].at[me, li]
                cp = pltpu.make_async_copy(src, dst, loc_sems.at[k])
                cp.start()
                locs.append(cp)
                k += 1
        k0 = 0
        for gi, g in enumerate(groups):
            for r in range(1, NDEV):
                px = 1 - x if r & 4 else x
                py = 1 - y if r & 2 else y
                pc = 1 - c if r & 1 else c
                peer = 4 * px + 2 * py + pc
                for li in range(len(g)):
                    src = ins[k0 + li] if gather else ins[k0 + li].at[peer]
                    dst = outs[gi].at[me] if len(g) == 1 else outs[gi].at[me, li]
                    pltpu.make_async_remote_copy(src_ref=src, dst_ref=dst, send_sem=send_sems.at[gi * (NDEV - 1) + r - 1],
                                                 recv_sem=recv_sems.at[gi * (NDEV - 1) + r - 1], device_id=(px, py, pc),
                                                 device_id_type=MESH).start()
            k0 += len(g)
        for gi, g in enumerate(groups):
            for r in range(1, NDEV):
                px = 1 - x if r & 4 else x
                py = 1 - y if r & 2 else y
                pc = 1 - c if r & 1 else c
                peer = 4 * px + 2 * py + pc
                slab = pltpu.make_async_remote_copy(src_ref=outs[gi].at[me], dst_ref=outs[gi].at[peer],
                                                    send_sem=send_sems.at[gi * (NDEV - 1) + r - 1], recv_sem=recv_sems.at[gi * (NDEV - 1) + r - 1],
                                                    device_id=(px, py, pc), device_id_type=MESH)
                slab.wait_send()
                slab.wait_recv()
        for cp in locs:
            cp.wait()

    hbm = pl.BlockSpec(memory_space=pltpu.HBM)
    outs = _call(body, name=name, out_shape=tuple(out_shapes), in_specs=[hbm] * n_in, out_specs=tuple([hbm] * n_g),
                 scratch=[pltpu.SemaphoreType.DMA((n_g * (NDEV - 1),)), pltpu.SemaphoreType.DMA((n_g * (NDEV - 1),)),
                          pltpu.SemaphoreType.DMA((n_in,))])(*flat)
    return list(outs)


def _peer_of(x, y, c, r):
    return (1 - x if r & 4 else x, 1 - y if r & 2 else y, 1 - c if r & 1 else c)


def _exchange_start(groups, *, gather, name, after=None):
    flat = [pltpu.with_memory_space_constraint(a, pltpu.HBM) for g in groups for a in g]
    n_in, n_g = len(flat), len(groups)
    land_shapes = []
    for g in groups:
        s = g[0].shape if gather else g[0].shape[1:]
        lead = (NDEV,) if len(g) == 1 else (NDEV, len(g))
        land_shapes.append((lead + tuple(s), g[0].dtype))
    lands = [pltpu.with_memory_space_constraint(lax.empty(s, d), pltpu.HBM) for s, d in land_shapes]
    n_after = 0 if after is None else 1

    def body(*refs):
        ins, land = refs[:n_in], refs[n_in:n_in + n_g]
        send_sems, recv_sems, loc_sems = refs[n_in + n_g + n_after:n_in + n_g + n_after + 3]
        token = refs[-1]
        x, y, c = lax.axis_index("x"), lax.axis_index("y"), lax.axis_index("c")
        me = 4 * x + 2 * y + c
        k = 0
        for gi, g in enumerate(groups):
            for li in range(len(g)):
                dst = land[gi].at[me] if len(g) == 1 else land[gi].at[me, li]
                pltpu.make_async_copy(ins[k] if gather else ins[k].at[me], dst, loc_sems.at[k]).start()
                k += 1
        k0 = 0
        for gi, g in enumerate(groups):
            for r in range(1, NDEV):
                px, py, pc = _peer_of(x, y, c, r)
                peer = 4 * px + 2 * py + pc
                for li in range(len(g)):
                    dst = land[gi].at[me] if len(g) == 1 else land[gi].at[me, li]
                    pltpu.make_async_remote_copy(src_ref=ins[k0 + li] if gather else ins[k0 + li].at[peer], dst_ref=dst,
                                                 send_sem=send_sems.at[gi * (NDEV - 1) + r - 1], recv_sem=recv_sems.at[gi * (NDEV - 1) + r - 1],
                                                 device_id=(px, py, pc), device_id_type=MESH).start()
            k0 += len(g)
        token[...] = jnp.zeros_like(token)

    hbm = pl.BlockSpec(memory_space=pltpu.HBM)
    sem = pl.BlockSpec(memory_space=pltpu.SEMAPHORE)
    args = flat + lands + ([after] if n_after else [])
    outs = pl.pallas_call(body, name=name,
        out_shape=(pltpu.SemaphoreType.DMA((n_g * (NDEV - 1),)), pltpu.SemaphoreType.DMA((n_g * (NDEV - 1),)),
                   pltpu.SemaphoreType.DMA((n_in,)), *[pltpu.HBM(a.shape, a.dtype) for a in flat],
                   *[pltpu.HBM(s, d) for s, d in land_shapes], _sds((8, LANES), F32)),
        in_specs=[hbm] * (n_in + n_g) + [pl.BlockSpec(memory_space=pl.ANY)] * n_after,
        out_specs=(sem, sem, sem, *[hbm] * (n_in + n_g), pl.BlockSpec(memory_space=pltpu.VMEM)),
        input_output_aliases={k: 3 + k for k in range(n_in + n_g)},
        compiler_params=pltpu.CompilerParams(has_side_effects=pltpu.SideEffectType.DATAFLOW_SIDE_EFFECTING))(*args)
    handle = dict(sems=outs[0:3], srcs=list(outs[3:3 + n_in]), lands=list(outs[3 + n_in:3 + n_in + n_g]),
                  sizes=[len(g) for g in groups], gather=gather)
    return handle, outs[-1]


def _exchange_wait(handle, *, name, after):
    srcs, lands, sizes, gather = handle["srcs"], handle["lands"], handle["sizes"], handle["gather"]
    n_in, n_g = len(srcs), len(lands)

    def body(*refs):
        ins, land = refs[:n_in], refs[n_in:n_in + n_g]
        send_sems, recv_sems, loc_sems = refs[n_in + n_g:n_in + n_g + 3]
        x, y, c = lax.axis_index("x"), lax.axis_index("y"), lax.axis_index("c")
        me = 4 * x + 2 * y + c
        for gi in range(n_g):
            for r in range(1, NDEV):
                px, py, pc = _peer_of(x, y, c, r)
                peer = 4 * px + 2 * py + pc
                slab = pltpu.make_async_remote_copy(src_ref=land[gi].at[me], dst_ref=land[gi].at[peer],
                                                    send_sem=send_sems.at[gi * (NDEV - 1) + r - 1], recv_sem=recv_sems.at[gi * (NDEV - 1) + r - 1],
                                                    device_id=(px, py, pc), device_id_type=MESH)
                slab.wait_send()
                slab.wait_recv()
        k = 0
        for gi in range(n_g):
            for li in range(sizes[gi]):
                dst = land[gi].at[me] if sizes[gi] == 1 else land[gi].at[me, li]
                pltpu.make_async_copy(ins[k] if gather else ins[k].at[me], dst, loc_sems.at[k]).wait()
                k += 1

    hbm = pl.BlockSpec(memory_space=pltpu.HBM)
    sem = pl.BlockSpec(memory_space=pltpu.SEMAPHORE)
    outs = pl.pallas_call(body, name=name, out_shape=tuple(pltpu.HBM(a.shape, a.dtype) for a in srcs + lands),
        in_specs=[hbm] * (n_in + n_g) + [sem] * 3 + [pl.BlockSpec(memory_space=pl.ANY)],
        out_specs=tuple([hbm] * (n_in + n_g)), input_output_aliases={k: k for k in range(n_in + n_g)},
        compiler_params=pltpu.CompilerParams(has_side_effects=pltpu.SideEffectType.DATAFLOW_SIDE_EFFECTING))(
            *srcs, *lands, *handle["sems"], after)
    return list(outs[n_in:])


def _gate_part(first, dx, y_ref, g_ref, dy_ref, dg_ref, db_ref):
    @pl.when(first)
    def _():
        dg_ref[...] = jnp.zeros_like(dg_ref)
        db_ref[...] = jnp.zeros_like(db_ref)

    dy = dx * g_ref[...]
    dy_ref[...] = dy.astype(BF16)
    dg_ref[...] += jnp.sum(dx * y_ref[...].astype(F32), axis=0, keepdims=True)
    db_ref[...] += jnp.sum(dy, axis=0, keepdims=True)


def _norm_bwd_part(first, dhv, x_ref, g_ref, sc_ref, dres_ref, dx_ref, dsc_ref, dsh_ref, dg_ref):
    @pl.when(first)
    def _():
        dsc_ref[...] = jnp.zeros_like(dsc_ref)
        dsh_ref[...] = jnp.zeros_like(dsh_ref)
        dg_ref[...] = jnp.zeros_like(dg_ref)

    xv = x_ref[...]
    r = lax.rsqrt(jnp.mean(xv * xv, axis=-1, keepdims=True) + EPS)
    xhat = xv * r
    gain_v = g_ref[...]
    dsc_ref[...] += jnp.sum(dhv * (xhat * gain_v), axis=0, keepdims=True)
    dsh_ref[...] += jnp.sum(dhv, axis=0, keepdims=True)
    dxn = dhv * (1.0 + sc_ref[...])
    dg_ref[...] += jnp.sum(dxn * xhat, axis=0, keepdims=True)
    dxhat = dxn * gain_v
    dx = dres_ref[...] + r * (dxhat - xhat * jnp.mean(dxhat * xhat, axis=-1, keepdims=True))
    dx_ref[...] = dx
    return dx


def _final_loss(x, gain, target, y_prev, gate_prev, *, name):
    S, D = x.shape
    tm = _row_tile(S, 512)

    def body(x_ref, g_ref, t_ref, y_ref, gp_ref, loss_ref, dx_ref, dg_ref, dy_ref, dgp_ref, dbp_ref):
        first = pl.program_id(0) == 0

        @pl.when(first)
        def _():
            loss_ref[...] = jnp.zeros_like(loss_ref)
            dg_ref[...] = jnp.zeros_like(dg_ref)

        xv = x_ref[...]
        r = lax.rsqrt(jnp.mean(xv * xv, axis=-1, keepdims=True) + EPS)
        xhat = xv * r
        gv = g_ref[...]
        err = xhat * gv - t_ref[...]
        row_loss = jnp.mean(err * err, axis=-1, keepdims=True)
        loss_ref[...] += 0.5 * jnp.sum(row_loss, axis=0, keepdims=True)
        dy = err * (1.0 / D)
        dg_ref[...] += jnp.sum(dy * xhat, axis=0, keepdims=True)
        dxhat = dy * gv
        dx = r * (dxhat - xhat * jnp.mean(dxhat * xhat, axis=-1, keepdims=True))
        dx_ref[...] = dx
        _gate_part(first, dx, y_ref, gp_ref, dy_ref, dgp_ref, dbp_ref)

    row = pl.BlockSpec((tm, D), lambda i: (i, 0))
    vec = pl.BlockSpec((1, D), lambda i: (0, 0))
    one = pl.BlockSpec((1, 1), lambda i: (0, 0))
    vsh = _sds((1, D), F32)
    return _call(body, name=name, grid=(S // tm,), in_specs=[row, vec, row, row, vec],
                 out_specs=(one, row, vec, row, vec, vec),
                 out_shape=(_sds((1, 1), F32), _sds((S, D), F32), vsh, _sds((S, D), BF16), vsh, vsh),
                 sem=("arbitrary",))(x, gain, target, y_prev, gate_prev)


def _pick_tm(M, bytes_per_row, fixed_bytes):
    for tm in (1024, 512, 256, 128):
        if M % tm == 0 and 2 * tm * bytes_per_row + fixed_bytes <= VMEM_BLOCK_BUDGET:
            return tm
    return _row_tile(M, 128)


def _mm_nn(a, w, *, name, bias=None, relu2=False, ln=None, norm=None, res=None, gate=None, out_dtype=BF16, after=None):
    M, K = a.shape
    col = w.ndim == 3
    if col:
        nsh, ns = w.shape[0], w.shape[2]
        w_spec = pl.BlockSpec((nsh, K, ns), lambda i: (0, 0, 0))
    else:
        nsh, ns = 1, w.shape[1]
        w_spec = pl.BlockSpec((K, ns), lambda i: (0, 0))
    N = nsh * ns
    residual = res is not None
    out_bytes = (4 + 4 + 2) if residual else jnp.dtype(out_dtype).itemsize
    tm = _pick_tm(M, K * a.dtype.itemsize + N * out_bytes + (K * 2 if norm is not None else 0), 2 * K * N * 2)

    def body(*refs):
        it = iter(refs)
        a_ref, w_ref = next(it), next(it)
        b_ref = next(it) if bias is not None else None
        lg_ref, lb_ref = (next(it), next(it)) if ln is not None else (None, None)
        ng_ref, nsc_ref, nsh_ref = (next(it), next(it), next(it)) if norm is not None else (None, None, None)
        res_ref, gate_ref = (next(it), next(it)) if residual else (None, None)
        if after is not None:
            next(it)
        out_ref = next(it)
        raw_ref = next(it) if residual else None
        av = a_ref[...]
        if relu2:
            av = jnp.square(jnp.maximum(av.astype(F32), 0.0))
        if ln is not None:
            av, _ = _ln_silu(av, lg_ref[...], lb_ref[...])
        if norm is not None:
            r = lax.rsqrt(jnp.mean(av * av, axis=-1, keepdims=True) + EPS)
            av = (av * r) * ng_ref[...] * (1.0 + nsc_ref[...]) + nsh_ref[...]
        ab = av.astype(BF16)
        if norm is not None:
            next(it)[...] = ab
        for d in range(nsh):
            cols = slice(d * ns, (d + 1) * ns)
            acc = jnp.dot(ab, w_ref[d] if col else w_ref[...], preferred_element_type=F32)
            if b_ref is not None:
                acc = acc + b_ref[:, cols]
            if residual:
                raw_ref[:, cols] = acc.astype(BF16)
                out_ref[:, cols] = res_ref[:, cols] + gate_ref[:, cols] * acc
            else:
                out_ref[:, cols] = acc.astype(out_dtype)

    tile = pl.BlockSpec((tm, N), lambda i: (i, 0))
    vec = pl.BlockSpec((1, N), lambda i: (0, 0))
    in_specs, args = [pl.BlockSpec((tm, K), lambda i: (i, 0)), w_spec], [a, w]
    if bias is not None:
        in_specs.append(vec)
        args.append(bias)
    if ln is not None:
        in_specs += [pl.BlockSpec((1, K), lambda i: (0, 0))] * 2
        args += list(ln)
    if norm is not None:
        in_specs += [pl.BlockSpec((1, K), lambda i: (0, 0))] * 3
        args += list(norm)
    if residual:
        in_specs += [tile, vec]
        args += [res, gate]
        out_specs = [tile, tile]
        out_shape = [_sds((M, N), F32), _sds((M, N), BF16)]
    else:
        out_specs = [tile]
        out_shape = [_sds((M, N), out_dtype)]
    if after is not None:
        in_specs.append(pl.BlockSpec(memory_space=pl.ANY))
        args.append(after)
    if norm is not None:
        out_specs.append(pl.BlockSpec((tm, K), lambda i: (i, 0)))
        out_shape.append(_sds((M, K), BF16))
    outs = _call(body, name=name, grid=(M // tm,), in_specs=in_specs, out_specs=tuple(out_specs), out_shape=tuple(out_shape),
                 sem=("parallel",))(*args)
    return outs[0] if len(outs) == 1 else outs


def _mm_nt(g, w, *, name, z=None, out_dtype=F32, after=None, norm=None, gated=None):
    M, N = g.shape
    col = w.ndim == 3
    if col:
        nsh, K, ns = w.shape
        w_spec = pl.BlockSpec((nsh, K, ns), lambda i: (0, 0, 0))
    else:
        K = w.shape[0]
        w_spec = pl.BlockSpec((K, N), lambda i: (0, 0))
    assert norm is None or col
    kc = min(K, 1024)
    obytes = jnp.dtype(out_dtype).itemsize
    row_bytes = N * g.dtype.itemsize + K * obytes + (K * 2 if z is not None else 0)
    if norm is not None:
        row_bytes += 2 * K * 4 + (K * 4 if gated is not None else 0)
    tm = _pick_tm(M, row_bytes, 2 * K * N * 2 + 512 * K * 4)

    def body(*refs):
        it = iter(refs)
        g_ref, w_ref = next(it), next(it)
        z_ref = next(it) if z is not None else None
        norm_in = [next(it) for _ in range(4)] if norm is not None else None
        gate_in = [next(it) for _ in range(2)] if gated is not None else None
        if after is not None:
            next(it)
        out_ref = next(it)
        if col:
            acc = None
            for d in range(nsh):
                part = lax.dot_general(g_ref[:, d * ns:(d + 1) * ns].astype(BF16), w_ref[d], NT_DIMS,
                                       preferred_element_type=F32)
                acc = part if acc is None else acc + part
            if norm is None:
                out_ref[...] = acc.astype(out_dtype)
            else:
                first = pl.program_id(0) == 0
                dx = _norm_bwd_part(first, acc, *norm_in, out_ref, next(it), next(it), next(it))
                if gated is not None:
                    _gate_part(first, dx, *gate_in, next(it), next(it), next(it))
        else:
            gb = g_ref[...].astype(BF16)
            for cki in range(K // kc):
                cols = slice(cki * kc, (cki + 1) * kc)
                part = lax.dot_general(gb, w_ref[cols, :], NT_DIMS, preferred_element_type=F32)
                if z_ref is not None:
                    part = part * (2.0 * jnp.maximum(z_ref[:, cols].astype(F32), 0.0))
                out_ref[:, cols] = part.astype(out_dtype)

    row = pl.BlockSpec((tm, K), lambda i: (i, 0))
    vec = pl.BlockSpec((1, K), lambda i: (0, 0))
    vsh = _sds((1, K), F32)
    in_specs, args = [pl.BlockSpec((tm, N), lambda i: (i, 0)), w_spec], [g, w]
    out_specs, out_shape = [row], [_sds((M, K), out_dtype)]
    if z is not None:
        in_specs.append(row)
        args.append(z)
    if norm is not None:
        x, gain, sc, dres = norm
        in_specs += [row, vec, vec, row]
        args += [x, gain, sc, dres]
        out_specs += [vec, vec, vec]
        out_shape += [vsh, vsh, vsh]
    if gated is not None:
        in_specs += [row, vec]
        args += list(gated)
        out_specs += [row, vec, vec]
        out_shape += [_sds((M, K), BF16), vsh, vsh]
    if after is not None:
        in_specs.append(pl.BlockSpec(memory_space=pl.ANY))
        args.append(after)
    outs = _call(body, name=name, grid=(M // tm,), in_specs=in_specs, out_specs=tuple(out_specs), out_shape=tuple(out_shape),
                 sem=("parallel",) if norm is None else ("arbitrary",))(*args)
    return outs[0] if len(outs) == 1 else outs


def _mm_tn(a, g, *, name, col_shards=None, relu2=False, ln=None):
    M, K = a.shape
    N = g.shape[1]
    acc_budget = 8 * 1024 * 1024
    tm = _row_tile(M, 1024)
    nm = M // tm
    if col_shards:
        ns = N // col_shards
        spc = col_shards
        while spc > 1 and K * ns * spc * 4 > acc_budget:
            spc //= 2
        grid = (col_shards // spc, nm)
        a_spec = pl.BlockSpec((tm, K), lambda c, m: (m, 0))
        g_spec = pl.BlockSpec((tm, spc * ns), lambda c, m: (m, c))
        out_spec = pl.BlockSpec((spc, K, ns), lambda c, m: (c, 0, 0))
        out_shape = _sds((col_shards, K, ns), BF16)
        acc_shape = (K, spc * ns)
    else:
        tk = K
        while tk > 128 and tk * N * 4 > acc_budget:
            tk //= 2
        grid = (K // tk, nm)
        a_spec = pl.BlockSpec((tm, tk), lambda c, m: (m, c))
        g_spec = pl.BlockSpec((tm, N), lambda c, m: (m, 0))
        out_spec = pl.BlockSpec((tk, N), lambda c, m: (c, 0))
        out_shape = _sds((K, N), BF16)
        acc_shape = (tk, N)
        assert ln is None or tk == K
    in_specs, args = [a_spec, g_spec], [a, g]
    if ln is not None:
        in_specs += [pl.BlockSpec((1, K), lambda c, m: (0, 0))] * 2
        args += list(ln)

    def body(a_ref, g_ref, *rest):
        out_ref, acc_ref = rest[-2:]
        m = pl.program_id(1)

        @pl.when(m == 0)
        def _():
            acc_ref[...] = jnp.zeros_like(acc_ref)

        av = a_ref[...]
        if relu2:
            av = jnp.square(jnp.maximum(av.astype(F32), 0.0))
        if ln is not None:
            av, _ = _ln_silu(av, rest[0][...], rest[1][...])
        acc_ref[...] += lax.dot_general(av.astype(BF16), g_ref[...].astype(BF16), TN_DIMS, preferred_element_type=F32)

        @pl.when(m == nm - 1)
        def _():
            if col_shards:
                for s in range(spc):
                    out_ref[s] = acc_ref[:, s * ns:(s + 1) * ns].astype(BF16)
            else:
                out_ref[...] = acc_ref[...].astype(BF16)

    return _call(body, name=name, grid=grid, in_specs=in_specs, out_specs=out_spec, out_shape=out_shape,
                 scratch=[pltpu.VMEM(acc_shape, F32)], sem=("parallel", "arbitrary"))(*args)


CONV_TILE = 256


def _glu_rows(u2, ch):
    d = u2.shape[1] // 2
    return (u2[:, :d] * _sigmoid(u2[:, d:])).reshape(u2.shape[0], ch, LANES)


def _fill_glu(buf, u_ref, uh_ref, ch, tile):
    first = pl.program_id(0) == 0
    buf[0:HALO] = jnp.where(first, 0.0, _glu_rows(uh_ref[...], ch))
    buf[HALO:HALO + tile] = _glu_rows(u_ref[...], ch)


CONV_SUB = 4


def _conv_specs(S, D, tile):
    per = tile // HALO
    u_spec = pl.BlockSpec((tile, 2 * D), lambda i: (i, 0))
    uh_spec = pl.BlockSpec((HALO, 2 * D), lambda i: (jnp.maximum(i * per - 1, 0), 0))
    x_spec = pl.BlockSpec((tile, D), lambda i: (i, 0))
    xn_spec = pl.BlockSpec((HALO, D), lambda i: (jnp.minimum((i + 1) * per, S // HALO - 1), 0))
    w_spec = pl.BlockSpec((CONV_WIDTH, D // LANES, LANES), lambda i: (0, 0, 0))
    v_spec = pl.BlockSpec((1, D // LANES, LANES), lambda i: (0, 0, 0))
    return u_spec, uh_spec, x_spec, xn_spec, w_spec, v_spec


def _conv_mid_fwd(u, w3, bdw3, *, name):
    S, D = u.shape[0], u.shape[1] // 2
    ch = D // LANES
    tile = _row_tile(S, CONV_TILE)
    sub = _row_tile(tile, CONV_SUB)
    u_spec, uh_spec, x_spec, _, w_spec, v_spec = _conv_specs(S, D, tile)

    def body(u_ref, uh_ref, w_ref, b_ref, o_ref, buf, stage):
        _fill_glu(buf, u_ref, uh_ref, ch, tile)

        def step(q, carry):
            acc = [b_ref[...], None]
            for k in range(CONV_WIDTH):
                term = buf[pl.ds(q * sub + (HALO - CONV_WIDTH + 1 + k), sub)] * w_ref[k]
                acc[k % 2] = term if acc[k % 2] is None else acc[k % 2] + term
            stage[pl.ds(q * sub, sub)] = acc[0] + acc[1]
            return carry

        lax.fori_loop(0, tile // sub, step, 0)
        o_ref[...] = stage[...].reshape(tile, D)

    return _call(body, name=name, grid=(S // tile,), in_specs=[u_spec, uh_spec, w_spec, v_spec], out_specs=x_spec,
                 out_shape=_sds((S, D), F32),
                 scratch=[pltpu.VMEM((tile + HALO, ch, LANES), F32), pltpu.VMEM((tile, ch, LANES), F32)],
                 sem=("parallel",))(u, u, w3, bdw3)


def _ln_silu(v, gv, bv):
    mu = jnp.mean(v, axis=-1, keepdims=True)
    cen = v - mu
    rstd = lax.rsqrt(jnp.mean(cen * cen, axis=-1, keepdims=True) + EPS)
    nrm = cen * rstd
    ln = nrm * gv + bv
    sg = _sigmoid(ln)
    return ln * sg, (nrm, rstd, ln, sg)


def _ln_silu_bwd(dwo, ds, lng, lnb, *, name):
    S, D = dwo.shape
    tm = _row_tile(S, 512)

    def body(v_ref, ds_ref, g_ref, b_ref, ddw_ref, dg_ref, db_ref, dbdw_ref):
        @pl.when(pl.program_id(0) == 0)
        def _():
            dg_ref[...] = jnp.zeros_like(dg_ref)
            db_ref[...] = jnp.zeros_like(db_ref)
            dbdw_ref[...] = jnp.zeros_like(dbdw_ref)

        gv = g_ref[...]
        _, (nrm, rstd, ln, sg) = _ln_silu(v_ref[...], gv, b_ref[...])
        dln = ds_ref[...] * (sg * (1.0 + ln * (1.0 - sg)))
        dg_ref[...] += jnp.sum(dln * nrm, axis=0, keepdims=True)
        db_ref[...] += jnp.sum(dln, axis=0, keepdims=True)
        dn = dln * gv
        ddw = rstd * (dn - jnp.mean(dn, axis=-1, keepdims=True) - nrm * jnp.mean(dn * nrm, axis=-1, keepdims=True))
        dbdw_ref[...] += jnp.sum(ddw, axis=0, keepdims=True)
        ddw_ref[...] = ddw

    row = pl.BlockSpec((tm, D), lambda i: (i, 0))
    vec = pl.BlockSpec((1, D), lambda i: (0, 0))
    vsh = _sds((1, D), F32)
    return _call(body, name=name, grid=(S // tm,), in_specs=[row, row, vec, vec], out_specs=(row, vec, vec, vec),
                 out_shape=(_sds((S, D), F32), vsh, vsh, vsh), sem=("arbitrary",))(dwo, ds, lng, lnb)


def _conv_mid_bwd_dw(u, ddw, w3, *, name):
    S, D = ddw.shape
    ch = D // LANES
    tile = _row_tile(S, CONV_TILE)
    sub = _row_tile(tile, CONV_SUB)
    last = S // tile - 1
    u_spec, uh_spec, x_spec, xn_spec, w_spec, _ = _conv_specs(S, D, tile)
    b_spec = pl.BlockSpec((1, 2 * D), lambda i: (0, 0))

    def body(u_ref, uh_ref, d_ref, dn_ref, w_ref, du_ref, dw_ref, db_ref, gbuf, dbuf, stage):
        @pl.when(pl.program_id(0) == 0)
        def _():
            dw_ref[...] = jnp.zeros_like(dw_ref)
            db_ref[...] = jnp.zeros_like(db_ref)

        _fill_glu(gbuf, u_ref, uh_ref, ch, tile)
        dbuf[0:tile] = d_ref[...].reshape(tile, ch, LANES)
        dbuf[tile:tile + HALO] = jnp.where(pl.program_id(0) == last, 0.0, dn_ref[...].reshape(HALO, ch, LANES))

        def step(q, c):
            s0 = q * sub
            ddw_q = dbuf[pl.ds(s0, sub)]
            acc = [None, None]
            for k in range(CONV_WIDTH):
                term = dbuf[pl.ds(s0 + (CONV_WIDTH - 1 - k), sub)] * w_ref[k]
                acc[k % 2] = term if acc[k % 2] is None else acc[k % 2] + term
                dw_ref[k] += jnp.sum(ddw_q * gbuf[pl.ds(s0 + (HALO - CONV_WIDTH + 1 + k), sub)], axis=0)
            stage[pl.ds(s0, sub)] = acc[0] + acc[1]
            return c

        lax.fori_loop(0, tile // sub, step, 0)
        dglu = stage[...].reshape(tile, D)
        uv = u_ref[...]
        av, sg = uv[:, :D], _sigmoid(uv[:, D:])
        da = dglu * sg
        dg = da * av * (1.0 - sg)
        du_ref[:, 0:D] = da
        du_ref[:, D:2 * D] = dg
        db_ref[:, 0:D] += jnp.sum(da, axis=0, keepdims=True)
        db_ref[:, D:2 * D] += jnp.sum(dg, axis=0, keepdims=True)

    return _call(body, name=name, grid=(S // tile,), in_specs=[u_spec, uh_spec, x_spec, xn_spec, w_spec],
                 out_specs=(u_spec, w_spec, b_spec),
                 out_shape=(_sds((S, 2 * D), F32), _sds((CONV_WIDTH, ch, LANES), F32), _sds((1, 2 * D), F32)),
                 scratch=[pltpu.VMEM((tile + HALO, ch, LANES), F32), pltpu.VMEM((tile + HALO, ch, LANES), F32),
                          pltpu.VMEM((tile, ch, LANES), F32)],
                 sem=("arbitrary",))(u, u, ddw, ddw, w3)


def _ret_tables(S, dk):
    f32 = np.float32
    B = min(RET_BLOCK, S)
    lg = np.log(f32(1.0) - f32(2.0) ** (f32(-5.0) - np.arange(RET_HEADS, dtype=f32)))
    idx = np.arange(B, dtype=f32)
    diff = idx[:, None] - idx[None, :]
    cq, ck = (np.arange(B) // CHUNK)[:, None], (np.arange(B) // CHUNK)[None, :]
    dist = np.where(cq == ck, np.abs(diff), diff)
    mask = np.where(ck <= cq, np.exp(lg[:, None, None] * dist[None]), f32(0.0)).astype(f32)
    xi = np.exp(lg[:, None] * (idx + f32(1.0)))[..., None].astype(f32)
    zeta = np.exp(lg[:, None] * (f32(B - 1.0) - idx))[..., None].astype(f32)
    gam = np.broadcast_to(np.exp(lg * f32(B))[:, None, None], (RET_HEADS, 8, LANES)).astype(f32)
    pos = np.arange(S, dtype=f32)
    inv = (f32(ROPE_BASE) ** (-np.arange(0, dk, 2, dtype=f32) / f32(dk))).astype(f32)
    ang = (pos[:, None] * inv[None, :]).astype(f32)
    tb = dict(mask=mask, xi=xi, zeta=zeta, gam=gam, cos=np.cos(ang).astype(f32), sin=np.sin(ang).astype(f32))
    return dict(B=B, **{k: jnp.asarray(v) for k, v in tb.items()})


def _rope(v, cs, sn):
    half = v.shape[1] // 2
    v1, v2 = v[:, :half], v[:, half:]
    return jnp.concatenate([v1 * cs - v2 * sn, v2 * cs + v1 * sn], axis=-1)


def _rope_t(d, cs, sn):
    half = d.shape[1] // 2
    d1, d2 = d[:, :half], d[:, half:]
    return jnp.concatenate([d1 * cs + d2 * sn, d2 * cs - d1 * sn], axis=-1)


def _dot(a, b):
    return jnp.dot(a.astype(BF16), b.astype(BF16), preferred_element_type=F32)


def _dot_nt(a, b):
    return lax.dot_general(a.astype(BF16), b.astype(BF16), NT_DIMS, preferred_element_type=F32)


def _dot_tn(a, b):
    return lax.dot_general(a.astype(BF16), b.astype(BF16), TN_DIMS, preferred_element_type=F32)


def _ret_specs(S, D, B, RB, reverse):
    dk, dv = D // RET_HEADS, 2 * D // RET_HEADS
    nb = S // RB
    blk = (lambda ib: nb - 1 - ib) if reverse else (lambda ib: ib)
    q = pl.BlockSpec((RB, dk), lambda h, ib: (blk(ib), h))
    k = pl.BlockSpec((RB, dk), lambda h, ib: (blk(ib), RET_HEADS + h))
    v = pl.BlockSpec((RB, dv), lambda h, ib: (blk(ib), RET_HEADS + h))
    gate = pl.BlockSpec((RB, dv), lambda h, ib: (blk(ib), 2 * RET_HEADS + h))
    yv = pl.BlockSpec((RB, dv), lambda h, ib: (blk(ib), h))
    rope = pl.BlockSpec((RB, dk // 2), lambda h, ib: (blk(ib), 0))
    mask = pl.BlockSpec((None, B, B), lambda h, ib: (h, 0, 0))
    dec = pl.BlockSpec((None, B, 1), lambda h, ib: (h, 0, 0))
    gam = pl.BlockSpec((None, 8, LANES), lambda h, ib: (h, 0, 0))
    gn = pl.BlockSpec((1, dv), lambda h, ib: (0, h))
    return dict(q=q, k=k, v=v, gate=gate, yv=yv, rope=rope, mask=mask, dec=dec, gam=gam, gn=gn)


def _group_norm(yr, gv, bv):
    mu = jnp.mean(yr, axis=-1, keepdims=True)
    cen = yr - mu
    rstd = lax.rsqrt(jnp.mean(cen * cen, axis=-1, keepdims=True) + EPS)
    nrm = cen * rstd
    return nrm, rstd, nrm * gv + bv


def _ret_fwd(proj, tb, gng, gnb, *, name):
    S, D = proj.shape[0], proj.shape[1] // 6
    dk, dv = D // RET_HEADS, 2 * D // RET_HEADS
    B = tb["B"]
    RB = _row_tile(S, 4 * B)
    nsub = RB // B
    sp = _ret_specs(S, D, B, RB, False)
    scale = dk ** -0.5

    def body(q_ref, k_ref, v_ref, gt_ref, cos_ref, sin_ref, mask_ref, xi_ref, zeta_ref, gam_ref, gng_ref, gnb_ref,
             yr_ref, yg_ref, qr_ref, kr_ref, state):
        @pl.when(pl.program_id(1) == 0)
        def _():
            state[...] = jnp.zeros_like(state)

        for sb in range(nsub):
            rows = slice(sb * B, (sb + 1) * B)
            cs, sn = cos_ref[rows, :], sin_ref[rows, :]
            q = _rope(q_ref[rows, :].astype(F32), cs, sn)
            k = _rope(k_ref[rows, :].astype(F32), cs, sn) * scale
            qr_ref[rows, :] = q.astype(BF16)
            kr_ref[rows, :] = k.astype(BF16)
            vb = v_ref[rows, :]
            p = _dot_nt(q, k) * mask_ref[...]
            st = state[...]
            yr = _dot(p, vb) + _dot(q * xi_ref[...], st)
            state[...] = st * gam_ref[0:1, 0:1] + _dot_tn(k * zeta_ref[...], vb)
            _, _, gn = _group_norm(yr, gng_ref[...], gnb_ref[...])
            gt = gt_ref[rows, :].astype(F32)
            yr_ref[rows, :] = yr.astype(BF16)
            yg_ref[rows, :] = (gt * _sigmoid(gt) * gn).astype(BF16)

    return _call(body, name=name, grid=(RET_HEADS, S // RB),
                 in_specs=[sp["q"], sp["k"], sp["v"], sp["gate"], sp["rope"], sp["rope"], sp["mask"], sp["dec"], sp["dec"],
                           sp["gam"], sp["gn"], sp["gn"]],
                 out_specs=(sp["yv"], sp["yv"], sp["q"], sp["q"]),
                 out_shape=(_sds((S, 2 * D), BF16), _sds((S, 2 * D), BF16), _sds((S, D), BF16), _sds((S, D), BF16)),
                 scratch=[pltpu.VMEM((dk, dv), F32)], sem=("parallel", "arbitrary"))(
                     proj, proj, proj, proj, tb["cos"], tb["sin"], tb["mask"], tb["xi"], tb["zeta"], tb["gam"], gng, gnb)


def _ret_bwd_q(proj, kr, yr, dyg, tb, gng, gnb, *, name):
    S, D = proj.shape[0], proj.shape[1] // 6
    dk, dv = D // RET_HEADS, 2 * D // RET_HEADS
    B = tb["B"]
    RB = _row_tile(S, 4 * B)
    nsub = RB // B
    sp = _ret_specs(S, D, B, RB, False)

    def body(k_ref, v_ref, gt_ref, yr_ref, dyg_ref, cos_ref, sin_ref, mask_ref, xi_ref, zeta_ref, gam_ref,
             gng_ref, gnb_ref, dq_ref, dgt_ref, dyr_ref, dgg_ref, dgb_ref, state):
        @pl.when(pl.program_id(1) == 0)
        def _():
            state[...] = jnp.zeros_like(state)
            dgg_ref[...] = jnp.zeros_like(dgg_ref)
            dgb_ref[...] = jnp.zeros_like(dgb_ref)

        for sb in range(nsub):
            rows = slice(sb * B, (sb + 1) * B)
            cs, sn = cos_ref[rows, :], sin_ref[rows, :]
            k = k_ref[rows, :]
            vb = v_ref[rows, :]
            gv = gng_ref[...]
            nrm, rstd, gn = _group_norm(yr_ref[rows, :].astype(F32), gv, gnb_ref[...])
            gt = gt_ref[rows, :].astype(F32)
            sg = _sigmoid(gt)
            dyg = dyg_ref[rows, :].astype(F32)
            dgt_ref[rows, :] = (dyg * gn * (sg * (1.0 + gt * (1.0 - sg)))).astype(BF16)
            dgn = dyg * (gt * sg)
            dgg_ref[...] += jnp.sum(dgn * nrm, axis=0, keepdims=True)
            dgb_ref[...] += jnp.sum(dgn, axis=0, keepdims=True)
            dn = dgn * gv
            dyr = rstd * (dn - jnp.mean(dn, axis=-1, keepdims=True) - nrm * jnp.mean(dn * nrm, axis=-1, keepdims=True))
            dyr_ref[rows, :] = dyr.astype(BF16)
            dp = _dot_nt(dyr, vb) * mask_ref[...]
            st = state[...]
            dq = _dot(dp, k) + _dot_nt(dyr, st) * xi_ref[...]
            dq_ref[rows, :] = _rope_t(dq, cs, sn).astype(BF16)
            state[...] = st * gam_ref[0:1, 0:1] + _dot_tn(k.astype(F32) * zeta_ref[...], vb)

    return _call(body, name=name, grid=(RET_HEADS, S // RB),
                 in_specs=[sp["q"], sp["v"], sp["gate"], sp["yv"], sp["yv"], sp["rope"], sp["rope"], sp["mask"],
                           sp["dec"], sp["dec"], sp["gam"], sp["gn"], sp["gn"]],
                 out_specs=(sp["q"], sp["yv"], sp["yv"], sp["gn"], sp["gn"]),
                 out_shape=(_sds((S, D), BF16), _sds((S, 2 * D), BF16), _sds((S, 2 * D), BF16), _sds((1, 2 * D), F32),
                            _sds((1, 2 * D), F32)),
                 scratch=[pltpu.VMEM((dk, dv), F32)], sem=("parallel", "arbitrary"))(
                     kr, proj, proj, yr, dyg, tb["cos"], tb["sin"], tb["mask"], tb["xi"], tb["zeta"], tb["gam"], gng, gnb)


def _ret_bwd_kv(proj, qr, kr, dyr, dq, dgt, tb, *, name):
    S, D = proj.shape[0], proj.shape[1] // 6
    dk, dv = D // RET_HEADS, 2 * D // RET_HEADS
    B = tb["B"]
    RB = _row_tile(S, 2 * B)
    nsub = RB // B
    nb = S // RB
    scale = dk ** -0.5

    def body(v_ref, qr_ref, kr_ref, dyr_ref, dq_ref, dgt_ref, cos_ref, sin_ref, mask_ref, xi_ref, zeta_ref, gam_ref, out_ref,
             dstate):
        @pl.when(pl.program_id(0) == 0)
        def _():
            dstate[...] = jnp.zeros_like(dstate)

        out_ref[:, 0:D] = dq_ref[...]
        out_ref[:, 4 * D:6 * D] = dgt_ref[...]
        for sb in reversed(range(nsub)):
            rows = slice(sb * B, (sb + 1) * B)
            cs, sn = cos_ref[rows, :], sin_ref[rows, :]
            for h in range(RET_HEADS):
                kcols = slice(D + h * dk, D + (h + 1) * dk)
                vcols = slice(2 * D + h * dv, 2 * D + (h + 1) * dv)
                q = qr_ref[rows, h * dk:(h + 1) * dk]
                k = kr_ref[rows, h * dk:(h + 1) * dk]
                vb = v_ref[rows, h * dv:(h + 1) * dv]
                dyr_h = dyr_ref[rows, h * dv:(h + 1) * dv]
                mk = mask_ref[h]
                p = _dot_nt(q, k) * mk
                dp = _dot_nt(dyr_h, vb) * mk
                ds = dstate[h]
                zt = zeta_ref[h]
                dkr = _dot_tn(dp, q) + _dot_nt(vb, ds) * zt
                out_ref[rows, kcols] = _rope_t(dkr * scale, cs, sn).astype(BF16)
                out_ref[rows, vcols] = (_dot_tn(p, dyr_h) + _dot(k.astype(F32) * zt, ds)).astype(BF16)
                dstate[h] = ds * gam_ref[h, 0:1, 0:1] + _dot_tn(q.astype(F32) * xi_ref[h], dyr_h)

    def rev(width):
        return pl.BlockSpec((RB, width), lambda ib: (nb - 1 - ib, 0))

    def whole(a):
        return pl.BlockSpec(a.shape, lambda ib: (0,) * a.ndim)

    return _call(body, name=name, grid=(nb,),
                 in_specs=[pl.BlockSpec((RB, 2 * D), lambda ib: (nb - 1 - ib, 1)), rev(D), rev(D), rev(2 * D), rev(D), rev(2 * D),
                           rev(dk // 2), rev(dk // 2), whole(tb["mask"]), whole(tb["xi"]), whole(tb["zeta"]), whole(tb["gam"])],
                 out_specs=rev(6 * D), out_shape=_sds((S, 6 * D), BF16), scratch=[pltpu.VMEM((RET_HEADS, dk, dv), F32)],
                 sem=("arbitrary",))(proj, qr, kr, dyr, dq, dgt, tb["cos"], tb["sin"], tb["mask"], tb["xi"], tb["zeta"],
                                     tb["gam"])


def _ada_fwd(c_all, ada_w, *, name):
    L, D, ns = ada_w.shape

    def body(c_ref, w_ref, out_ref):
        cv = c_ref[...]
        cond = cv * _sigmoid(cv)
        out_ref[...] = jnp.dot(cond.astype(BF16), w_ref[...].astype(BF16), preferred_element_type=F32)

    return _call(body, name=name, grid=(L,), in_specs=[pl.BlockSpec((NDEV, D), lambda l: (0, 0)),
                                                      pl.BlockSpec((None, D, ns), lambda l: (l, 0, 0))],
                 out_specs=pl.BlockSpec((None, NDEV, ns), lambda l: (l, 0, 0)), out_shape=_sds((L, NDEV, ns), F32),
                 sem=("parallel",))(c_all, ada_w)


def _ada_bwd(c_all, dmod_cols, *, name):
    L, _, ns = dmod_cols.shape
    D = c_all.shape[1]

    def body(c_ref, d_ref, out_ref):
        cv = c_ref[...]
        cond = cv * _sigmoid(cv)
        out_ref[...] = lax.dot_general(cond.astype(BF16), d_ref[...].astype(BF16), TN_DIMS, preferred_element_type=F32)

    return _call(body, name=name, grid=(L,), in_specs=[pl.BlockSpec((NDEV, D), lambda l: (0, 0)),
                                                      pl.BlockSpec((None, NDEV, ns), lambda l: (l, 0, 0))],
                 out_specs=pl.BlockSpec((None, D, ns), lambda l: (l, 0, 0)), out_shape=_sds((L, D, ns), F32),
                 sem=("parallel",))(c_all, dmod_cols)


def _adamw(w, m, v, parts, *, name):
    shape = w.shape
    L, cols = len(parts), shape[-1]
    rows = w.size // (cols * L)
    n = parts[0].shape[0]
    tr = rows
    for cand in (256, 128, 64, 32, 16, 8):
        if rows % cand == 0:
            tr = cand
            break
    c1 = 1.0 - ADAM_B1 ** ADAM_STEP
    c2 = 1.0 - ADAM_B2 ** ADAM_STEP

    def body(w_ref, m_ref, v_ref, *rest):
        p_refs = rest[:L]
        g_ref, d_ref, m2_ref, v2_ref = rest[L:]
        layer = pl.program_id(0)
        for l in range(L):
            @pl.when(layer == l)
            def _(p_ref=p_refs[l]):
                g = p_ref[0].astype(F32)
                for i in range(1, n):
                    g = g + p_ref[i].astype(F32)
                m2 = ADAM_B1 * m_ref[...] + (1.0 - ADAM_B1) * g
                v2 = ADAM_B2 * v_ref[...] + (1.0 - ADAM_B2) * (g * g)
                g_ref[...] = g
                m2_ref[...] = m2
                v2_ref[...] = v2
                d_ref[...] = -ADAM_LR * ((m2 / c1) / (jnp.sqrt(v2 / c2) + ADAM_EPS) + ADAM_WD * w_ref[...])

    mat = pl.BlockSpec((None, tr, cols), lambda l, i: (l, i, 0))

    def part_spec(k):
        return pl.BlockSpec((n, tr, cols), lambda l, i: (0, jnp.where(l == k, i, 0), 0))

    outs = _call(body, name=name, grid=(L, rows // tr), in_specs=[mat, mat, mat] + [part_spec(k) for k in range(L)],
                 out_specs=(mat, mat, mat, mat), out_shape=tuple(_sds((L, rows, cols), F32) for _ in range(4)),
                 sem=("parallel", "parallel"))(w.reshape(L, rows, cols), m.reshape(L, rows, cols), v.reshape(L, rows, cols),
                                               *[p.reshape(n, rows, cols) for p in parts])
    return tuple(o.reshape(shape) for o in outs)


SMALL = ("ada_b", "norm_mix_g", "norm_mlp_g", "conv_b_pw1", "conv_b_dw", "conv_ln_g", "conv_ln_b", "conv_b_pw2",
         "final_norm_g")
WEIGHTS = ("ada_w", "ada_b", "norm_mix_g", "norm_mlp_g", "conv_w_pw1", "conv_b_pw1", "conv_w_dw", "conv_b_dw", "conv_ln_g",
           "conv_ln_b", "conv_w_pw2", "conv_b_pw2", "ret_w_in", "ret_gn_g", "ret_gn_b", "ret_w_out", "mlp_w1", "mlp_w2",
           "final_norm_g")


def kernel(x, c, ada_w, ada_b, norm_mix_g, norm_mlp_g, conv_w_pw1, conv_b_pw1, conv_w_dw, conv_b_dw, conv_ln_g, conv_ln_b, conv_w_pw2, conv_b_pw2, ret_w_in, ret_gn_g, ret_gn_b, ret_w_out, mlp_w1, mlp_w2, final_norm_g, loss_target, m_ada_w, m_ada_b, m_norm_mix_g, m_norm_mlp_g, m_conv_w_pw1, m_conv_b_pw1, m_conv_w_dw, m_conv_b_dw, m_conv_ln_g, m_conv_ln_b, m_conv_w_pw2, m_conv_b_pw2, m_ret_w_in, m_ret_gn_g, m_ret_gn_b, m_ret_w_out, m_mlp_w1, m_mlp_w2, m_final_norm_g, v_ada_w, v_ada_b, v_norm_mix_g, v_norm_mlp_g, v_conv_w_pw1, v_conv_b_pw1, v_conv_w_dw, v_conv_b_dw, v_conv_ln_g, v_conv_ln_b, v_conv_w_pw2, v_conv_b_pw2, v_ret_w_in, v_ret_gn_g, v_ret_gn_b, v_ret_w_out, v_mlp_w1, v_mlp_w2, v_final_norm_g):
    W = dict(ada_w=ada_w, ada_b=ada_b, norm_mix_g=norm_mix_g, norm_mlp_g=norm_mlp_g, conv_w_pw1=conv_w_pw1,
             conv_b_pw1=conv_b_pw1, conv_w_dw=conv_w_dw, conv_b_dw=conv_b_dw, conv_ln_g=conv_ln_g, conv_ln_b=conv_ln_b,
             conv_w_pw2=conv_w_pw2, conv_b_pw2=conv_b_pw2, ret_w_in=ret_w_in, ret_gn_g=ret_gn_g, ret_gn_b=ret_gn_b,
             ret_w_out=ret_w_out, mlp_w1=mlp_w1, mlp_w2=mlp_w2, final_norm_g=final_norm_g)
    Mo = dict(ada_w=m_ada_w, ada_b=m_ada_b, norm_mix_g=m_norm_mix_g, norm_mlp_g=m_norm_mlp_g, conv_w_pw1=m_conv_w_pw1,
              conv_b_pw1=m_conv_b_pw1, conv_w_dw=m_conv_w_dw, conv_b_dw=m_conv_b_dw, conv_ln_g=m_conv_ln_g,
              conv_ln_b=m_conv_ln_b, conv_w_pw2=m_conv_w_pw2, conv_b_pw2=m_conv_b_pw2, ret_w_in=m_ret_w_in,
              ret_gn_g=m_ret_gn_g, ret_gn_b=m_ret_gn_b, ret_w_out=m_ret_w_out, mlp_w1=m_mlp_w1, mlp_w2=m_mlp_w2,
              final_norm_g=m_final_norm_g)
    Vo = dict(ada_w=v_ada_w, ada_b=v_ada_b, norm_mix_g=v_norm_mix_g, norm_mlp_g=v_norm_mlp_g, conv_w_pw1=v_conv_w_pw1,
              conv_b_pw1=v_conv_b_pw1, conv_w_dw=v_conv_w_dw, conv_b_dw=v_conv_b_dw, conv_ln_g=v_conv_ln_g,
              conv_ln_b=v_conv_ln_b, conv_w_pw2=v_conv_w_pw2, conv_b_pw2=v_conv_b_pw2, ret_w_in=v_ret_w_in,
              ret_gn_g=v_ret_gn_g, ret_gn_b=v_ret_gn_b, ret_w_out=v_ret_w_out, mlp_w1=v_mlp_w1, mlp_w2=v_mlp_w2,
              final_norm_g=v_final_norm_g)

    S, D = x.shape[1], x.shape[2]
    CH = D // LANES
    n_conv, n_ret = conv_w_pw1.shape[0], ret_w_in.shape[0]
    me = 4 * lax.axis_index("x") + 2 * lax.axis_index("y") + lax.axis_index("c")
    xs = x.reshape(S, D)
    target = loss_target.reshape(S, D)

    def mixer_shards(i):
        j = i // 2
        if i % 2 == 0:
            return [[conv_w_pw1[j].astype(BF16)], [conv_w_pw2[j].astype(BF16)]]
        return [[ret_w_in[j].astype(BF16)], [ret_w_out[j].astype(BF16)]]

    def mlp_shards(i):
        return [[mlp_w1[i].astype(BF16)], [mlp_w2[i].astype(BF16)]]

    def mlp_weights(got):
        return got[0], got[1].reshape(4 * D, D)

    first_handle, _ = _exchange_start(mixer_shards(0)[:1], gather=True, name="gather_start_first")
    small = _exchange([[conv_w_dw], [ret_gn_g], [ret_gn_b], [c]], gather=True, name="gather_small")
    dw_g, gng_g, gnb_g, c_g = small
    dw3 = jnp.transpose(dw_g, (1, 2, 0, 3)).reshape(n_conv, CONV_WIDTH, CH, LANES)
    gng_full = jnp.transpose(gng_g, (1, 2, 0, 3)).reshape(n_ret, 1, 2 * D)
    gnb_full = jnp.transpose(gnb_g, (1, 2, 0, 3)).reshape(n_ret, 1, 2 * D)
    c_all = c_g.reshape(NDEV, D)

    mod_cols = _ada_fwd(c_all, ada_w, name="ada_fwd")
    mod_all = _exchange([[mod_cols]], gather=True, name="gather_mod")[0]
    mod = lax.dynamic_index_in_dim(mod_all, me, axis=2, keepdims=False)
    mod = jnp.transpose(mod, (1, 0, 2)).reshape(DEPTH, 6 * D) + ada_b
    mods = [[mod[i, j * D:(j + 1) * D].reshape(1, D) for j in range(6)] for i in range(DEPTH)]
    tb = _ret_tables(S, D // RET_HEADS)

    def vec(a):
        return a.reshape(1, -1)

    def group_a(i):
        return mixer_shards(i) if i % 2 == 0 else mixer_shards(i)[:1]

    def group_b(i):
        return mlp_shards(i) if i % 2 == 0 else mixer_shards(i)[1:] + mlp_shards(i)

    mix_first = _exchange_wait(first_handle, name="gather_wait_first", after=mod)[0]
    pw2_handle, token = _exchange_start(mixer_shards(0)[1:], gather=True, name="gather_start_pw2_0", after=mix_first)
    handle_b, token = _exchange_start(mlp_shards(0), gather=True, name="gather_start_b0", after=token)
    saved = []
    weights = []
    xcur = xs
    for i in range(DEPTH):
        sh1, sc1, g1, sh2, sc2, g2 = mods[i]
        j = i // 2
        if i > 0:
            got = _exchange_wait(handle_a, name=f"gather_wait_a{i}", after=xcur)
            mix_first = got[0]
            mix_second = got[1].reshape(-1, D) if i % 2 == 0 else None
            handle_b, token = _exchange_start(group_b(i), gather=True, name=f"gather_start_b{i}", after=got[0])
        st = dict(x_in=xcur)
        norm1 = (vec(norm_mix_g[i]), sc1, sh1)
        if i % 2 == 0:
            u, h = _mm_nn(xcur, mix_first, norm=norm1, bias=vec(conv_b_pw1[j]), out_dtype=F32, name=f"pw1_fwd{i}", after=token)
            dwo = _conv_mid_fwd(u, dw3[j], conv_b_dw[j].reshape(1, CH, LANES), name=f"conv_mid_fwd{i}")
            if i == 0:
                mix_second = _exchange_wait(pw2_handle, name="gather_wait_pw2_0", after=dwo)[0].reshape(-1, D)
            xcur, y_raw = _mm_nn(dwo, mix_second, ln=(vec(conv_ln_g[j]), vec(conv_ln_b[j])), bias=vec(conv_b_pw2[j]), res=xcur,
                                 gate=g1, name=f"pw2_fwd{i}")
            st.update(u=u, dwo=dwo, y_raw=y_raw)
            got = _exchange_wait(handle_b, name=f"gather_wait_b{i}", after=xcur)
            mlp_w = mlp_weights(got)
        else:
            proj, h = _mm_nn(xcur, mix_first, norm=norm1, name=f"ret_in_fwd{i}", after=token)
            yr, yg, qr, kr = _ret_fwd(proj, tb, gng_full[j], gnb_full[j], name=f"ret_fwd{i}")
            got = _exchange_wait(handle_b, name=f"gather_wait_b{i}", after=yg)
            mix_second, mlp_w = got[0].reshape(-1, D), mlp_weights(got[1:3])
            xcur, y_raw = _mm_nn(yg, mix_second, res=xcur, gate=g1, name=f"ret_out_fwd{i}")
            st.update(proj=proj, yr=yr, yg=yg, qr=qr, kr=kr, y_raw=y_raw)
        st.update(h=h, x_mid=xcur)
        if i + 1 < DEPTH:
            handle_a, token = _exchange_start(group_a(i + 1), gather=True, name=f"gather_start_a{i + 1}", after=got[0])
        z, h2 = _mm_nn(xcur, mlp_w[0], norm=(vec(norm_mlp_g[i]), sc2, sh2), name=f"mlp1_fwd{i}", after=token)
        xcur, o_raw = _mm_nn(z, mlp_w[1], relu2=True, res=xcur, gate=g2, name=f"mlp2_fwd{i}")
        st.update(h2=h2, z=z, o_raw=o_raw)
        saved.append(st)
        weights.append((mix_first, mix_second) + mlp_w)

    g2_last = mods[DEPTH - 1][5]
    loss_local, dx, d_final_g, dy, dgate, _ = _final_loss(xcur, vec(final_norm_g), target, saved[-1]["o_raw"], g2_last,
                                                          name="final_loss")
    loss = lax.psum(loss_local[0, 0], AXES)

    dmod_rows = [None] * DEPTH
    d_mix_g, d_mlp_g = [None] * DEPTH, [None] * DEPTH
    d_pw1, d_pw2, d_win, d_wout = [None] * n_conv, [None] * n_conv, [None] * n_ret, [None] * n_ret
    d_w1, d_w2 = [None] * DEPTH, [None] * DEPTH
    d_bpw1, d_bdw, d_lng, d_lnb, d_bpw2, d_dw = ([None] * n_conv for _ in range(6))
    d_gng, d_gnb = [None] * n_ret, [None] * n_ret

    def gn_parts(d):
        return jnp.transpose(d.reshape(RET_HEADS, NDEV, -1), (1, 0, 2))

    grad_handles = [None] * DEPTH
    token = None
    for i in reversed(range(DEPTH)):
        sh1, sc1, g1, sh2, sc2, g2 = mods[i]
        j = i // 2
        st = saved[i]
        mix_a, mix_b, w1_i, w2_i = weights[i]
        do, dg2 = dy, dgate
        dz = _mm_nt(do, w2_i, z=st["z"], out_dtype=BF16, name=f"mlp2_bwd_x{i}", after=token)
        d_w2[i] = _mm_tn(st["z"], do, relu2=True, name=f"mlp2_bwd_w{i}")
        dx, dsc2, dsh2, d_mlp_g[i], dy, dg1, dby = _mm_nt(dz, w1_i, norm=(st["x_mid"], vec(norm_mlp_g[i]), sc2, dx),
                                                          gated=(st["y_raw"], g1), name=f"mlp1_bwd_x{i}")
        d_w1[i] = _mm_tn(st["h2"], dz, col_shards=NDEV, name=f"mlp1_bwd_w{i}")
        mlp_groups = [[d_w1[i]], [d_w2[i].reshape(NDEV, 4 * D // NDEV, D)]]
        token = None
        if i == 0:
            mlp0_handle, token = _exchange_start(mlp_groups, gather=False, name="grads_start_mlp0")
            mlp_groups = []
        if i % 2 == 0:
            d_bpw2[j] = dby
            ds = _mm_nt(dy, mix_b, name=f"pw2_bwd_x{i}", after=token)
            ln_gb = (vec(conv_ln_g[j]), vec(conv_ln_b[j]))
            d_pw2[j] = _mm_tn(st["dwo"], dy, ln=ln_gb, name=f"pw2_bwd_w{i}")
            ddw, d_lng[j], d_lnb[j], d_bdw[j] = _ln_silu_bwd(st["dwo"], ds, *ln_gb, name=f"conv_ln_bwd{i}")
            du, ddw_w, dbu = _conv_mid_bwd_dw(st["u"], ddw, dw3[j], name=f"conv_mid_bwd_dw{i}")
            d_dw[j], d_bpw1[j] = ddw_w.reshape(CONV_WIDTH, D), dbu.reshape(2, D)
            d_pw1[j] = _mm_tn(st["h"], du, col_shards=NDEV, name=f"pw1_bwd_w{i}")
            mix_groups = [[d_pw1[j]], [d_pw2[j].reshape(NDEV, D // NDEV, D)],
                          [jnp.transpose(d_dw[j].reshape(CONV_WIDTH, NDEV, D // NDEV), (1, 0, 2))]]
            mix_in, mix_name = du, f"pw1_bwd_x{i}"
        else:
            dyg = _mm_nt(dy, mix_b, out_dtype=BF16, name=f"ret_out_bwd_x{i}")
            d_wout[j] = _mm_tn(st["yg"], dy, name=f"ret_out_bwd_w{i}")
            dq, dgt, dyr, d_gng[j], d_gnb[j] = _ret_bwd_q(st["proj"], st["kr"], st["yr"], dyg, tb, gng_full[j], gnb_full[j],
                                                          name=f"ret_bwd_q{i}")
            dproj = _ret_bwd_kv(st["proj"], st["qr"], st["kr"], dyr, dq, dgt, tb, name=f"ret_bwd_kv{i}")
            d_win[j] = _mm_tn(st["h"], dproj, col_shards=NDEV, name=f"ret_in_bwd_w{i}")
            mix_in, mix_name = dproj, f"ret_in_bwd_x{i}"
            mix_groups = [[d_win[j]], [d_wout[j].reshape(NDEV, 2 * D // NDEV, D)], [gn_parts(d_gng[j])],
                          [gn_parts(d_gnb[j])]]
        grad_handles[i], token = _exchange_start(mix_groups + mlp_groups, gather=False, name=f"grads_start{i}")
        gated = (saved[i - 1]["o_raw"], mods[i - 1][5]) if i > 0 else None
        outs = _mm_nt(mix_in, mix_a, norm=(st["x_in"], vec(norm_mix_g[i]), sc1, dx), gated=gated, name=mix_name, after=token)
        dx, dsc1, dsh1, d_mix_g[i] = outs[:4]
        if i > 0:
            dy, dgate = outs[4], outs[5]
        dmod_rows[i] = jnp.concatenate([dsh1, dsc1, dg1, dsh2, dsc2, dg2], axis=0)
    grad_x = dx.reshape(1, S, D)

    small_local = jnp.concatenate(dmod_rows + d_mix_g + d_mlp_g + d_bpw1 + d_bdw + d_lng + d_lnb + d_bpw2 + [d_final_g],
                                  axis=0)
    small_all = _exchange([[small_local]], gather=True, name="gather_small_grads")[0]

    def pack(src):
        return jnp.concatenate([src[n].reshape(-1, D) for n in SMALL], axis=0)[None]

    sm = _adamw(pack(W), pack(Mo), pack(Vo), [small_all], name="adamw_small")
    results = {}
    row = 0
    for n in SMALL:
        cnt = W[n].size // D
        results[n] = tuple(o[0, row:row + cnt].reshape(W[n].shape) for o in sm)
        row += cnt

    ns_ada = ada_w.shape[2]
    dmod_all = small_all[:, :6 * DEPTH, :].reshape(NDEV, DEPTH, 6 * D)
    dmod_cols = jnp.transpose(lax.dynamic_slice_in_dim(dmod_all, me * ns_ada, ns_ada, axis=2), (1, 0, 2))
    g_ada = _ada_bwd(c_all, dmod_cols, name="ada_bwd")
    flat_ada = (1, DEPTH * D, ns_ada)
    ada_res = _adamw(ada_w.reshape(flat_ada), m_ada_w.reshape(flat_ada), v_ada_w.reshape(flat_ada),
                     [g_ada.reshape(flat_ada)], name="adamw_ada_w")
    results["ada_w"] = tuple(o.reshape(ada_w.shape) for o in ada_res)

    def update(names, parts):
        for n in names:
            results[n] = _adamw(W[n], Mo[n], Vo[n], parts[n], name=f"adamw_{n}")

    got = {i: _exchange_wait(grad_handles[i], name=f"grads_wait{i}", after=dx) for i in range(DEPTH - 1, 0, -1)}
    ret_layers = [i for i in range(DEPTH) if i % 2 == 1]
    update(("ret_w_in", "ret_w_out", "ret_gn_g", "ret_gn_b"),
           dict(ret_w_in=[got[i][0] for i in ret_layers], ret_w_out=[got[i][1] for i in ret_layers],
                ret_gn_g=[got[i][2] for i in ret_layers], ret_gn_b=[got[i][3] for i in ret_layers]))
    got_mlp0 = _exchange_wait(mlp0_handle, name="grads_wait_mlp0", after=results["ret_w_in"][0])
    update(("mlp_w1", "mlp_w2"),
           dict(mlp_w1=[got_mlp0[0]] + [got[i][-2] for i in range(1, DEPTH)],
                mlp_w2=[got_mlp0[1]] + [got[i][-1] for i in range(1, DEPTH)]))
    got[0] = _exchange_wait(grad_handles[0], name="grads_wait0", after=results["mlp_w1"][0])
    conv_layers = [i for i in range(DEPTH) if i % 2 == 0]
    update(("conv_w_pw1", "conv_w_pw2", "conv_w_dw"),
           dict(conv_w_pw1=[got[i][0] for i in conv_layers], conv_w_pw2=[got[i][1] for i in conv_layers],
                conv_w_dw=[got[i][2] for i in conv_layers]))

    outs = [loss, grad_x]
    for kind in range(4):
        outs += [results[n][kind] for n in WEIGHTS]
    return tuple(outs)
```

```python
import functools

import jax
import jax.numpy as jnp
import numpy as np
from jax import lax
from jax.experimental import pallas as pl
from jax.experimental.pallas import tpu as pltpu

F32, BF16 = jnp.float32, jnp.bfloat16
AXES = ("x", "y", "c")
NDEV = 8
DEPTH = 4
EPS = 1e-6
CHUNK = 64
CONV_WIDTH = 31
HALO = 32
RET_HEADS = 4
RET_BLOCK = 256
ROPE_BASE = 10000.0
LANES = 128
ADAM_LR, ADAM_B1, ADAM_B2, ADAM_EPS, ADAM_WD, ADAM_STEP = 0.001, 0.9, 0.999, 1e-08, 0.01, 10
VMEM_LIMIT = 56 * 1024 * 1024
VMEM_BLOCK_BUDGET = 44 * 1024 * 1024
MESH = pl.DeviceIdType.MESH
NT_DIMS = (((1,), (1,)), ((), ()))
TN_DIMS = (((0,), (0,)), ((), ()))


def _call(body, *, name, out_shape, in_specs, out_specs, grid=(), scratch=(), sem=None, aliases=None):
    params = dict(vmem_limit_bytes=VMEM_LIMIT)
    if sem is not None:
        params["dimension_semantics"] = sem
    return pl.pallas_call(body, name=name, grid=grid, in_specs=in_specs, out_specs=out_specs, out_shape=out_shape,
                          scratch_shapes=list(scratch), input_output_aliases=aliases or {},
                          compiler_params=pltpu.CompilerParams(**params))


def _row_tile(rows, want):
    t = min(rows, want)
    while rows % t:
        t //= 2
    return t


def _sds(shape, dtype):
    return jax.ShapeDtypeStruct(tuple(shape), dtype)


def _sigmoid(v):
    return 1.0 / (1.0 + jnp.exp(-v))


def _exchange(groups, *, gather, name):
    flat = [a for g in groups for a in g]
    n_in = len(flat)
    out_shapes = []
    for g in groups:
        s = g[0].shape if gather else g[0].shape[1:]
        lead = (NDEV,) if len(g) == 1 else (NDEV, len(g))
        out_shapes.append(_sds(lead + tuple(s), g[0].dtype))
    n_g = len(groups)

    def body(*refs):
        ins, outs = refs[:n_in], refs[n_in:n_in + n_g]
        send_sems, recv_sems, loc_sems = refs[n_in + n_g:]
        x, y, c = lax.axis_index("x"), lax.axis_index("y"), lax.axis_index("c")
        me = 4 * x + 2 * y + c
        locs, k = [], 0
        for gi, g in enumerate(groups):
            for li in range(len(g)):
                src = ins[k] if gather else ins[k].at[me]
                dst = outs[gi].at[me] if len(g) == 1 else outs[gi].at[me, li]
                cp = pltpu.make_async_copy(src, dst, loc_sems.at[k])
                cp.start()
                locs.append(cp)
                k += 1
        k0 = 0
        for gi, g in enumerate(groups):
            for r in range(1, NDEV):
                px = 1 - x if r & 4 else x
                py = 1 - y if r & 2 else y
                pc = 1 - c if r & 1 else c
                peer = 4 * px + 2 * py + pc
                for li in range(len(g)):
                    src = ins[k0 + li] if gather else ins[k0 + li].at[peer]
                    dst = outs[gi].at[me] if len(g) == 1 else outs[gi].at[me, li]
                    pltpu.make_async_remote_copy(src_ref=src, dst_ref=dst, send_sem=send_sems.at[gi * (NDEV - 1) + r - 1],
                                                 recv_sem=recv_sems.at[gi * (NDEV - 1) + r - 1], device_id=(px, py, pc),
                                                 device_id_type=MESH).start()
            k0 += len(g)
        for gi, g in enumerate(groups):
            for r in range(1, NDEV):
                px = 1 - x if r & 4 else x
                py = 1 - y if r & 2 else y
                pc = 1 - c if r & 1 else c
                peer = 4 * px + 2 * py + pc
                slab = pltpu.make_async_remote_copy(src_ref=outs[gi].at[me], dst_ref=outs[gi].at[peer],
                                                    send_sem=send_sems.at[gi * (NDEV - 1) + r - 1], recv_sem=recv_sems.at[gi * (NDEV - 1) + r - 1],
                                                    device_id=(px, py, pc), device_id_type=MESH)
                slab.wait_send()
                slab.wait_recv()
        for cp in locs:
            cp.wait()

    hbm = pl.BlockSpec(memory_space=pltpu.HBM)
    outs = _call(body, name=name, out_shape=tuple(out_shapes), in_specs=[hbm] * n_in, out_specs=tuple([hbm] * n_g),
                 scratch=[pltpu.SemaphoreType.DMA((n_g * (NDEV - 1),)), pltpu.SemaphoreType.DMA((n_g * (NDEV - 1),)),
                          pltpu.SemaphoreType.DMA((n_in,))])(*flat)
    return list(outs)


def _peer_of(x, y, c, r):
    return (1 - x if r & 4 else x, 1 - y if r & 2 else y, 1 - c if r & 1 else c)


def _exchange_start(groups, *, gather, name, after=None):
    flat = [pltpu.with_memory_space_constraint(a, pltpu.HBM) for g in groups for a in g]
    n_in, n_g = len(flat), len(groups)
    land_shapes = []
    for g in groups:
        s = g[0].shape if gather else g[0].shape[1:]
        lead = (NDEV,) if len(g) == 1 else (NDEV, len(g))
        land_shapes.append((lead + tuple(s), g[0].dtype))
    lands = [pltpu.with_memory_space_constraint(lax.empty(s, d), pltpu.HBM) for s, d in land_shapes]
    n_after = 0 if after is None else 1

    def body(*refs):
        ins, land = refs[:n_in], refs[n_in:n_in + n_g]
        send_sems, recv_sems, loc_sems = refs[n_in + n_g + n_after:n_in + n_g + n_after + 3]
        token = refs[-1]
        x, y, c = lax.axis_index("x"), lax.axis_index("y"), lax.axis_index("c")
        me = 4 * x + 2 * y + c
        k = 0
        for gi, g in enumerate(groups):
            for li in range(len(g)):
                dst = land[gi].at[me] if len(g) == 1 else land[gi].at[me, li]
                pltpu.make_async_copy(ins[k] if gather else ins[k].at[me], dst, loc_sems.at[k]).start()
                k += 1
        k0 = 0
        for gi, g in enumerate(groups):
            for r in range(1, NDEV):
                px, py, pc = _peer_of(x, y, c, r)
                peer = 4 * px + 2 * py + pc
                for li in range(len(g)):
                    dst = land[gi].at[me] if len(g) == 1 else land[gi].at[me, li]
                    pltpu.make_async_remote_copy(src_ref=ins[k0 + li] if gather else ins[k0 + li].at[peer], dst_ref=dst,
                                                 send_sem=send_sems.at[gi * (NDEV - 1) + r - 1], recv_sem=recv_sems.at[gi * (NDEV - 1) + r - 1],
                                                 device_id=(px, py, pc), device_id_type=MESH).start()
            k0 += len(g)
        token[...] = jnp.zeros_like(token)

    hbm = pl.BlockSpec(memory_space=pltpu.HBM)
    sem = pl.BlockSpec(memory_space=pltpu.SEMAPHORE)
    args = flat + lands + ([after] if n_after else [])
    outs = pl.pallas_call(body, name=name,
        out_shape=(pltpu.SemaphoreType.DMA((n_g * (NDEV - 1),)), pltpu.SemaphoreType.DMA((n_g * (NDEV - 1),)),
                   pltpu.SemaphoreType.DMA((n_in,)), *[pltpu.HBM(a.shape, a.dtype) for a in flat],
                   *[pltpu.HBM(s, d) for s, d in land_shapes], _sds((8, LANES), F32)),
        in_specs=[hbm] * (n_in + n_g) + [pl.BlockSpec(memory_space=pl.ANY)] * n_after,
        out_specs=(sem, sem, sem, *[hbm] * (n_in + n_g), pl.BlockSpec(memory_space=pltpu.VMEM)),
        input_output_aliases={k: 3 + k for k in range(n_in + n_g)},
        compiler_params=pltpu.CompilerParams(has_side_effects=pltpu.SideEffectType.DATAFLOW_SIDE_EFFECTING))(*args)
    handle = dict(sems=outs[0:3], srcs=list(outs[3:3 + n_in]), lands=list(outs[3 + n_in:3 + n_in + n_g]),
                  sizes=[len(g) for g in groups], gather=gather)
    return handle, outs[-1]


def _exchange_wait(handle, *, name, after):
    srcs, lands, sizes, gather = handle["srcs"], handle["lands"], handle["sizes"], handle["gather"]
    n_in, n_g = len(srcs), len(lands)

    def body(*refs):
        ins, land = refs[:n_in], refs[n_in:n_in + n_g]
        send_sems, recv_sems, loc_sems = refs[n_in + n_g:n_in + n_g + 3]
        x, y, c = lax.axis_index("x"), lax.axis_index("y"), lax.axis_index("c")
        me = 4 * x + 2 * y + c
        for gi in range(n_g):
            for r in range(1, NDEV):
                px, py, pc = _peer_of(x, y, c, r)
                peer = 4 * px + 2 * py + pc
                slab = pltpu.make_async_remote_copy(src_ref=land[gi].at[me], dst_ref=land[gi].at[peer],
                                                    send_sem=send_sems.at[gi * (NDEV - 1) + r - 1], recv_sem=recv_sems.at[gi * (NDEV - 1) + r - 1],
                                                    device_id=(px, py, pc), device_id_type=MESH)
                slab.wait_send()
                slab.wait_recv()
        k = 0
        for gi in range(n_g):
            for li in range(sizes[gi]):
                dst = land[gi].at[me] if sizes[gi] == 1 else land[gi].at[me, li]
                pltpu.make_async_copy(ins[k] if gather else ins[k].at[me], dst, loc_sems.at[k]).wait()
                k += 1

    hbm = pl.BlockSpec(memory_space=pltpu.HBM)
    sem = pl.BlockSpec(memory_space=pltpu.SEMAPHORE)
    outs = pl.pallas_call(body, name=name, out_shape=tuple(pltpu.HBM(a.shape, a.dtype) for a in srcs + lands),
        in_specs=[hbm] * (n_in + n_g) + [sem] * 3 + [pl.BlockSpec(memory_space=pl.ANY)],
        out_specs=tuple([hbm] * (n_in + n_g)), input_output_aliases={k: k for k in range(n_in + n_g)},
        compiler_params=pltpu.CompilerParams(has_side_effects=pltpu.SideEffectType.DATAFLOW_SIDE_EFFECTING))(
            *srcs, *lands, *handle["sems"], after)
    return list(outs[n_in:])


def _gate_part(first, dx, y_ref, g_ref, dy_ref, dg_ref, db_ref):
    @pl.when(first)
    def _():
        dg_ref[...] = jnp.zeros_like(dg_ref)
        db_ref[...] = jnp.zeros_like(db_ref)

    dy = dx * g_ref[...]
    dy_ref[...] = dy.astype(BF16)
    dg_ref[...] += jnp.sum(dx * y_ref[...].astype(F32), axis=0, keepdims=True)
    db_ref[...] += jnp.sum(dy, axis=0, keepdims=True)


def _norm_bwd_part(first, dhv, x_ref, g_ref, sc_ref, dres_ref, dx_ref, dsc_ref, dsh_ref, dg_ref):
    @pl.when(first)
    def _():
        dsc_ref[...] = jnp.zeros_like(dsc_ref)
        dsh_ref[...] = jnp.zeros_like(dsh_ref)
        dg_ref[...] = jnp.zeros_like(dg_ref)

    xv = x_ref[...]
    r = lax.rsqrt(jnp.mean(xv * xv, axis=-1, keepdims=True) + EPS)
    xhat = xv * r
    gain_v = g_ref[...]
    dsc_ref[...] += jnp.sum(dhv * (xhat * gain_v), axis=0, keepdims=True)
    dsh_ref[...] += jnp.sum(dhv, axis=0, keepdims=True)
    dxn = dhv * (1.0 + sc_ref[...])
    dg_ref[...] += jnp.sum(dxn * xhat, axis=0, keepdims=True)
    dxhat = dxn * gain_v
    dx = dres_ref[...] + r * (dxhat - xhat * jnp.mean(dxhat * xhat, axis=-1, keepdims=True))
    dx_ref[...] = dx
    return dx


def _final_loss(x, gain, target, y_prev, gate_prev, *, name):
    S, D = x.shape
    tm = _row_tile(S, 512)

    def body(x_ref, g_ref, t_ref, y_ref, gp_ref, loss_ref, dx_ref, dg_ref, dy_ref, dgp_ref, dbp_ref):
        first = pl.program_id(0) == 0

        @pl.when(first)
        def _():
            loss_ref[...] = jnp.zeros_like(loss_ref)
            dg_ref[...] = jnp.zeros_like(dg_ref)

        xv = x_ref[...]
        r = lax.rsqrt(jnp.mean(xv * xv, axis=-1, keepdims=True) + EPS)
        xhat = xv * r
        gv = g_ref[...]
        err = xhat * gv - t_ref[...]
        row_loss = jnp.mean(err * err, axis=-1, keepdims=True)
        loss_ref[...] += 0.5 * jnp.sum(row_loss, axis=0, keepdims=True)
        dy = err * (1.0 / D)
        dg_ref[...] += jnp.sum(dy * xhat, axis=0, keepdims=True)
        dxhat = dy * gv
        dx = r * (dxhat - xhat * jnp.mean(dxhat * xhat, axis=-1, keepdims=True))
        dx_ref[...] = dx
        _gate_part(first, dx, y_ref, gp_ref, dy_ref, dgp_ref, dbp_ref)

    row = pl.BlockSpec((tm, D), lambda i: (i, 0))
    vec = pl.BlockSpec((1, D), lambda i: (0, 0))
    one = pl.BlockSpec((1, 1), lambda i: (0, 0))
    vsh = _sds((1, D), F32)
    return _call(body, name=name, grid=(S // tm,), in_specs=[row, vec, row, row, vec],
                 out_specs=(one, row, vec, row, vec, vec),
                 out_shape=(_sds((1, 1), F32), _sds((S, D), F32), vsh, _sds((S, D), BF16), vsh, vsh),
                 sem=("arbitrary",))(x, gain, target, y_prev, gate_prev)


def _pick_tm(M, bytes_per_row, fixed_bytes):
    for tm in (1024, 512, 256, 128):
        if M % tm == 0 and 2 * tm * bytes_per_row + fixed_bytes <= VMEM_BLOCK_BUDGET:
            return tm
    return _row_tile(M, 128)


def _mm_nn(a, w, *, name, bias=None, relu2=False, ln=None, norm=None, res=None, gate=None, out_dtype=BF16, after=None):
    M, K = a.shape
    col = w.ndim == 3
    if col:
        nsh, ns = w.shape[0], w.shape[2]
        w_spec = pl.BlockSpec((nsh, K, ns), lambda i: (0, 0, 0))
    else:
        nsh, ns = 1, w.shape[1]
        w_spec = pl.BlockSpec((K, ns), lambda i: (0, 0))
    N = nsh * ns
    residual = res is not None
    out_bytes = (4 + 4 + 2) if residual else jnp.dtype(out_dtype).itemsize
    tm = _pick_tm(M, K * a.dtype.itemsize + N * out_bytes + (K * 2 if norm is not None else 0), 2 * K * N * 2)

    def body(*refs):
        it = iter(refs)
        a_ref, w_ref = next(it), next(it)
        b_ref = next(it) if bias is not None else None
        lg_ref, lb_ref = (next(it), next(it)) if ln is not None else (None, None)
        ng_ref, nsc_ref, nsh_ref = (next(it), next(it), next(it)) if norm is not None else (None, None, None)
        res_ref, gate_ref = (next(it), next(it)) if residual else (None, None)
        if after is not None:
            next(it)
        out_ref = next(it)
        raw_ref = next(it) if residual else None
        av = a_ref[...]
        if relu2:
            av = jnp.square(jnp.maximum(av.astype(F32), 0.0))
        if ln is not None:
            av, _ = _ln_silu(av, lg_ref[...], lb_ref[...])
        if norm is not None:
            r = lax.rsqrt(jnp.mean(av * av, axis=-1, keepdims=True) + EPS)
            av = (av * r) * ng_ref[...] * (1.0 + nsc_ref[...]) + nsh_ref[...]
        ab = av.astype(BF16)
        if norm is not None:
            next(it)[...] = ab
        for d in range(nsh):
            cols = slice(d * ns, (d + 1) * ns)
            acc = jnp.dot(ab, w_ref[d] if col else w_ref[...], preferred_element_type=F32)
            if b_ref is not None:
                acc = acc + b_ref[:, cols]
            if residual:
                raw_ref[:, cols] = acc.astype(BF16)
                out_ref[:, cols] = res_ref[:, cols] + gate_ref[:, cols] * acc
            else:
                out_ref[:, cols] = acc.astype(out_dtype)

    tile = pl.BlockSpec((tm, N), lambda i: (i, 0))
    vec = pl.BlockSpec((1, N), lambda i: (0, 0))
    in_specs, args = [pl.BlockSpec((tm, K), lambda i: (i, 0)), w_spec], [a, w]
    if bias is not None:
        in_specs.append(vec)
        args.append(bias)
    if ln is not None:
        in_specs += [pl.BlockSpec((1, K), lambda i: (0, 0))] * 2
        args += list(ln)
    if norm is not None:
        in_specs += [pl.BlockSpec((1, K), lambda i: (0, 0))] * 3
        args += list(norm)
    if residual:
        in_specs += [tile, vec]
        args += [res, gate]
        out_specs = [tile, tile]
        out_shape = [_sds((M, N), F32), _sds((M, N), BF16)]
    else:
        out_specs = [tile]
        out_shape = [_sds((M, N), out_dtype)]
    if after is not None:
        in_specs.append(pl.BlockSpec(memory_space=pl.ANY))
        args.append(after)
    if norm is not None:
        out_specs.append(pl.BlockSpec((tm, K), lambda i: (i, 0)))
        out_shape.append(_sds((M, K), BF16))
    outs = _call(body, name=name, grid=(M // tm,), in_specs=in_specs, out_specs=tuple(out_specs), out_shape=tuple(out_shape),
                 sem=("parallel",))(*args)
    return outs[0] if len(outs) == 1 else outs


def _mm_nt(g, w, *, name, z=None, out_dtype=F32, after=None, norm=None, gated=None):
    M, N = g.shape
    col = w.ndim == 3
    if col:
        nsh, K, ns = w.shape
        w_spec = pl.BlockSpec((nsh, K, ns), lambda i: (0, 0, 0))
    else:
        K = w.shape[0]
        w_spec = pl.BlockSpec((K, N), lambda i: (0, 0))
    assert norm is None or col
    kc = min(K, 1024)
    obytes = jnp.dtype(out_dtype).itemsize
    row_bytes = N * g.dtype.itemsize + K * obytes + (K * 2 if z is not None else 0)
    if norm is not None:
        row_bytes += 2 * K * 4 + (K * 4 if gated is not None else 0)
    tm = _pick_tm(M, row_bytes, 2 * K * N * 2 + 512 * K * 4)

    def body(*refs):
        it = iter(refs)
        g_ref, w_ref = next(it), next(it)
        z_ref = next(it) if z is not None else None
        norm_in = [next(it) for _ in range(4)] if norm is not None else None
        gate_in = [next(it) for _ in range(2)] if gated is not None else None
        if after is not None:
            next(it)
        out_ref = next(it)
        if col:
            acc = None
            for d in range(nsh):
                part = lax.dot_general(g_ref[:, d * ns:(d + 1) * ns].astype(BF16), w_ref[d], NT_DIMS,
                                       preferred_element_type=F32)
                acc = part if acc is None else acc + part
            if norm is None:
                out_ref[...] = acc.astype(out_dtype)
            else:
                first = pl.program_id(0) == 0
                dx = _norm_bwd_part(first, acc, *norm_in, out_ref, next(it), next(it), next(it))
                if gated is not None:
                    _gate_part(first, dx, *gate_in, next(it), next(it), next(it))
        else:
            gb = g_ref[...].astype(BF16)
            for cki in range(K // kc):
                cols = slice(cki * kc, (cki + 1) * kc)
                part = lax.dot_general(gb, w_ref[cols, :], NT_DIMS, preferred_element_type=F32)
                if z_ref is not None:
                    part = part * (2.0 * jnp.maximum(z_ref[:, cols].astype(F32), 0.0))
                out_ref[:, cols] = part.astype(out_dtype)

    row = pl.BlockSpec((tm, K), lambda i: (i, 0))
    vec = pl.BlockSpec((1, K), lambda i: (0, 0))
    vsh = _sds((1, K), F32)
    in_specs, args = [pl.BlockSpec((tm, N), lambda i: (i, 0)), w_spec], [g, w]
    out_specs, out_shape = [row], [_sds((M, K), out_dtype)]
    if z is not None:
        in_specs.append(row)
        args.append(z)
    if norm is not None:
        x, gain, sc, dres = norm
        in_specs += [row, vec, vec, row]
        args += [x, gain, sc, dres]
        out_specs += [vec, vec, vec]
        out_shape += [vsh, vsh, vsh]
    if gated is not None:
        in_specs += [row, vec]
        args += list(gated)
        out_specs += [row, vec, vec]
        out_shape += [_sds((M, K), BF16), vsh, vsh]
    if after is not None:
        in_specs.append(pl.BlockSpec(memory_space=pl.ANY))
        args.append(after)
    outs = _call(body, name=name, grid=(M // tm,), in_specs=in_specs, out_specs=tuple(out_specs), out_shape=tuple(out_shape),
                 sem=("parallel",) if norm is None else ("arbitrary",))(*args)
    return outs[0] if len(outs) == 1 else outs


def _mm_tn(a, g, *, name, col_shards=None, relu2=False, ln=None):
    M, K = a.shape
    N = g.shape[1]
    acc_budget = 8 * 1024 * 1024
    tm = _row_tile(M, 1024)
    nm = M // tm
    if col_shards:
        ns = N // col_shards
        spc = col_shards
        while spc > 1 and K * ns * spc * 4 > acc_budget:
            spc //= 2
        grid = (col_shards // spc, nm)
        a_spec = pl.BlockSpec((tm, K), lambda c, m: (m, 0))
        g_spec = pl.BlockSpec((tm, spc * ns), lambda c, m: (m, c))
        out_spec = pl.BlockSpec((spc, K, ns), lambda c, m: (c, 0, 0))
        out_shape = _sds((col_shards, K, ns), BF16)
        acc_shape = (K, spc * ns)
    else:
        tk = K
        while tk > 128 and tk * N * 4 > acc_budget:
            tk //= 2
        grid = (K // tk, nm)
        a_spec = pl.BlockSpec((tm, tk), lambda c, m: (m, c))
        g_spec = pl.BlockSpec((tm, N), lambda c, m: (m, 0))
        out_spec = pl.BlockSpec((tk, N), lambda c, m: (c, 0))
        out_shape = _sds((K, N), BF16)
        acc_shape = (tk, N)
        assert ln is None or tk == K
    in_specs, args = [a_spec, g_spec], [a, g]
    if ln is not None:
        in_specs += [pl.BlockSpec((1, K), lambda c, m: (0, 0))] * 2
        args += list(ln)

    def body(a_ref, g_ref, *rest):
        out_ref, acc_ref = rest[-2:]
        m = pl.program_id(1)

        @pl.when(m == 0)
        def _():
            acc_ref[...] = jnp.zeros_like(acc_ref)

        av = a_ref[...]
        if relu2:
            av = jnp.square(jnp.maximum(av.astype(F32), 0.0))
        if ln is not None:
            av, _ = _ln_silu(av, rest[0][...], rest[1][...])
        acc_ref[...] += lax.dot_general(av.astype(BF16), g_ref[...].astype(BF16), TN_DIMS, preferred_element_type=F32)

        @pl.when(m == nm - 1)
        def _():
            if col_shards:
                for s in range(spc):
                    out_ref[s] = acc_ref[:, s * ns:(s + 1) * ns].astype(BF16)
            else:
                out_ref[...] = acc_ref[...].astype(BF16)

    return _call(body, name=name, grid=grid, in_specs=in_specs, out_specs=out_spec, out_shape=out_shape,
                 scratch=[pltpu.VMEM(acc_shape, F32)], sem=("parallel", "arbitrary"))(*args)


CONV_TILE = 256


def _glu_rows(u2, ch):
    d = u2.shape[1] // 2
    return (u2[:, :d] * _sigmoid(u2[:, d:])).reshape(u2.shape[0], ch, LANES)


def _fill_glu(buf, u_ref, uh_ref, ch, tile):
    first = pl.program_id(0) == 0
    buf[0:HALO] = jnp.where(first, 0.0, _glu_rows(uh_ref[...], ch))
    buf[HALO:HALO + tile] = _glu_rows(u_ref[...], ch)


CONV_SUB = 4


def _conv_specs(S, D, tile):
    per = tile // HALO
    u_spec = pl.BlockSpec((tile, 2 * D), lambda i: (i, 0))
    uh_spec = pl.BlockSpec((HALO, 2 * D), lambda i: (jnp.maximum(i * per - 1, 0), 0))
    x_spec = pl.BlockSpec((tile, D), lambda i: (i, 0))
    xn_spec = pl.BlockSpec((HALO, D), lambda i: (jnp.minimum((i + 1) * per, S // HALO - 1), 0))
    w_spec = pl.BlockSpec((CONV_WIDTH, D // LANES, LANES), lambda i: (0, 0, 0))
    v_spec = pl.BlockSpec((1, D // LANES, LANES), lambda i: (0, 0, 0))
    return u_spec, uh_spec, x_spec, xn_spec, w_spec, v_spec


def _conv_mid_fwd(u, w3, bdw3, *, name):
    S, D = u.shape[0], u.shape[1] // 2
    ch = D // LANES
    tile = _row_tile(S, CONV_TILE)
    sub = _row_tile(tile, 2 * CONV_SUB)
    half = (CONV_WIDTH + 1) // 2
    u_spec, uh_spec, x_spec, _, w_spec, v_spec = _conv_specs(S, D, tile)

    def body(u_ref, uh_ref, w_ref, b_ref, o_ref, buf, stage):
        _fill_glu(buf, u_ref, uh_ref, ch, tile)

        def taps(lo, hi, start):
            def step(q, carry):
                rows = pl.ds(q * sub, sub)
                acc = [b_ref[...] if start else stage[rows], None]
                for k in range(lo, hi):
                    term = buf[pl.ds(q * sub + (HALO - CONV_WIDTH + 1 + k), sub)] * w_ref[k]
                    acc[k % 2] = term if acc[k % 2] is None else acc[k % 2] + term
                stage[rows] = acc[0] + acc[1]
                return carry

            lax.fori_loop(0, tile // sub, step, 0)

        taps(0, half, True)
        taps(half, CONV_WIDTH, False)
        o_ref[...] = stage[...].reshape(tile, D)

    return _call(body, name=name, grid=(S // tile,), in_specs=[u_spec, uh_spec, w_spec, v_spec], out_specs=x_spec,
                 out_shape=_sds((S, D), F32),
                 scratch=[pltpu.VMEM((tile + HALO, ch, LANES), F32), pltpu.VMEM((tile, ch, LANES), F32)],
                 sem=("parallel",))(u, u, w3, bdw3)


def _ln_silu(v, gv, bv):
    mu = jnp.mean(v, axis=-1, keepdims=True)
    cen = v - mu
    rstd = lax.rsqrt(jnp.mean(cen * cen, axis=-1, keepdims=True) + EPS)
    nrm = cen * rstd
    ln = nrm * gv + bv
    sg = _sigmoid(ln)
    return ln * sg, (nrm, rstd, ln, sg)


def _ln_silu_bwd(dwo, ds, lng, lnb, *, name):
    S, D = dwo.shape
    tm = _row_tile(S, 512)

    def body(v_ref, ds_ref, g_ref, b_ref, ddw_ref, dg_ref, db_ref, dbdw_ref):
        @pl.when(pl.program_id(0) == 0)
        def _():
            dg_ref[...] = jnp.zeros_like(dg_ref)
            db_ref[...] = jnp.zeros_like(db_ref)
            dbdw_ref[...] = jnp.zeros_like(dbdw_ref)

        gv = g_ref[...]
        _, (nrm, rstd, ln, sg) = _ln_silu(v_ref[...], gv, b_ref[...])
        dln = ds_ref[...] * (sg * (1.0 + ln * (1.0 - sg)))
        dg_ref[...] += jnp.sum(dln * nrm, axis=0, keepdims=True)
        db_ref[...] += jnp.sum(dln, axis=0, keepdims=True)
        dn = dln * gv
        ddw = rstd * (dn - jnp.mean(dn, axis=-1, keepdims=True) - nrm * jnp.mean(dn * nrm, axis=-1, keepdims=True))
        dbdw_ref[...] += jnp.sum(ddw, axis=0, keepdims=True)
        ddw_ref[...] = ddw

    row = pl.BlockSpec((tm, D), lambda i: (i, 0))
    vec = pl.BlockSpec((1, D), lambda i: (0, 0))
    vsh = _sds((1, D), F32)
    return _call(body, name=name, grid=(S // tm,), in_specs=[row, row, vec, vec], out_specs=(row, vec, vec, vec),
                 out_shape=(_sds((S, D), F32), vsh, vsh, vsh), sem=("arbitrary",))(dwo, ds, lng, lnb)


def _conv_mid_bwd_dw(u, ddw, w3, *, name):
    S, D = ddw.shape
    ch = D // LANES
    tile = _row_tile(S, CONV_TILE)
    sub = _row_tile(tile, 2 * CONV_SUB)
    last = S // tile - 1
    u_spec, uh_spec, x_spec, xn_spec, w_spec, _ = _conv_specs(S, D, tile)
    b_spec = pl.BlockSpec((1, 2 * D), lambda i: (0, 0))

    def body(u_ref, uh_ref, d_ref, dn_ref, w_ref, du_ref, dw_ref, db_ref, gbuf, dbuf, stage):
        @pl.when(pl.program_id(0) == 0)
        def _():
            dw_ref[...] = jnp.zeros_like(dw_ref)
            db_ref[...] = jnp.zeros_like(db_ref)

        _fill_glu(gbuf, u_ref, uh_ref, ch, tile)
        dbuf[0:tile] = d_ref[...].reshape(tile, ch, LANES)
        dbuf[tile:tile + HALO] = jnp.where(pl.program_id(0) == last, 0.0, dn_ref[...].reshape(HALO, ch, LANES))

        def taps(lo, hi, start):
            def step(q, c):
                s0 = q * sub
                ddw_q = dbuf[pl.ds(s0, sub)]
                acc = [None if start else stage[pl.ds(s0, sub)], None]
                for k in range(lo, hi):
                    term = dbuf[pl.ds(s0 + (CONV_WIDTH - 1 - k), sub)] * w_ref[k]
                    acc[k % 2] = term if acc[k % 2] is None else acc[k % 2] + term
                    dw_ref[k] += jnp.sum(ddw_q * gbuf[pl.ds(s0 + (HALO - CONV_WIDTH + 1 + k), sub)], axis=0)
                stage[pl.ds(s0, sub)] = acc[0] + acc[1]
                return c

            lax.fori_loop(0, tile // sub, step, 0)

        half = (CONV_WIDTH + 1) // 2
        taps(0, half, True)
        taps(half, CONV_WIDTH, False)
        dglu = stage[...].reshape(tile, D)
        uv = u_ref[...]
        av, sg = uv[:, :D], _sigmoid(uv[:, D:])
        da = dglu * sg
        dg = da * av * (1.0 - sg)
        du_ref[:, 0:D] = da
        du_ref[:, D:2 * D] = dg
        db_ref[:, 0:D] += jnp.sum(da, axis=0, keepdims=True)
        db_ref[:, D:2 * D] += jnp.sum(dg, axis=0, keepdims=True)

    return _call(body, name=name, grid=(S // tile,), in_specs=[u_spec, uh_spec, x_spec, xn_spec, w_spec],
                 out_specs=(u_spec, w_spec, b_spec),
                 out_shape=(_sds((S, 2 * D), F32), _sds((CONV_WIDTH, ch, LANES), F32), _sds((1, 2 * D), F32)),
                 scratch=[pltpu.VMEM((tile + HALO, ch, LANES), F32), pltpu.VMEM((tile + HALO, ch, LANES), F32),
                          pltpu.VMEM((tile, ch, LANES), F32)],
                 sem=("arbitrary",))(u, u, ddw, ddw, w3)


def _ret_tables(S, dk):
    f32 = np.float32
    B = min(RET_BLOCK, S)
    lg = np.log(f32(1.0) - f32(2.0) ** (f32(-5.0) - np.arange(RET_HEADS, dtype=f32)))
    idx = np.arange(B, dtype=f32)
    diff = idx[:, None] - idx[None, :]
    cq, ck = (np.arange(B) // CHUNK)[:, None], (np.arange(B) // CHUNK)[None, :]
    dist = np.where(cq == ck, np.abs(diff), diff)
    mask = np.where(ck <= cq, np.exp(lg[:, None, None] * dist[None]), f32(0.0)).astype(f32)
    xi = np.exp(lg[:, None] * (idx + f32(1.0)))[..., None].astype(f32)
    zeta = np.exp(lg[:, None] * (f32(B - 1.0) - idx))[..., None].astype(f32)
    gam = np.broadcast_to(np.exp(lg * f32(B))[:, None, None], (RET_HEADS, 8, LANES)).astype(f32)
    pos = np.arange(S, dtype=f32)
    inv = (f32(ROPE_BASE) ** (-np.arange(0, dk, 2, dtype=f32) / f32(dk))).astype(f32)
    ang = (pos[:, None] * inv[None, :]).astype(f32)
    tb = dict(mask=mask, xi=xi, zeta=zeta, gam=gam, cos=np.cos(ang).astype(f32), sin=np.sin(ang).astype(f32))
    return dict(B=B, **{k: jnp.asarray(v) for k, v in tb.items()})


def _rope(v, cs, sn):
    half = v.shape[1] // 2
    v1, v2 = v[:, :half], v[:, half:]
    return jnp.concatenate([v1 * cs - v2 * sn, v2 * cs + v1 * sn], axis=-1)


def _rope_t(d, cs, sn):
    half = d.shape[1] // 2
    d1, d2 = d[:, :half], d[:, half:]
    return jnp.concatenate([d1 * cs + d2 * sn, d2 * cs - d1 * sn], axis=-1)


def _dot(a, b):
    return jnp.dot(a.astype(BF16), b.astype(BF16), preferred_element_type=F32)


def _dot_nt(a, b):
    return lax.dot_general(a.astype(BF16), b.astype(BF16), NT_DIMS, preferred_element_type=F32)


def _dot_tn(a, b):
    return lax.dot_general(a.astype(BF16), b.astype(BF16), TN_DIMS, preferred_element_type=F32)


def _ret_specs(S, D, B, RB, reverse):
    dk, dv = D // RET_HEADS, 2 * D // RET_HEADS
    nb = S // RB
    blk = (lambda ib: nb - 1 - ib) if reverse else (lambda ib: ib)
    q = pl.BlockSpec((RB, dk), lambda h, ib: (blk(ib), h))
    k = pl.BlockSpec((RB, dk), lambda h, ib: (blk(ib), RET_HEADS + h))
    v = pl.BlockSpec((RB, dv), lambda h, ib: (blk(ib), RET_HEADS + h))
    gate = pl.BlockSpec((RB, dv), lambda h, ib: (blk(ib), 2 * RET_HEADS + h))
    yv = pl.BlockSpec((RB, dv), lambda h, ib: (blk(ib), h))
    rope = pl.BlockSpec((RB, dk // 2), lambda h, ib: (blk(ib), 0))
    mask = pl.BlockSpec((None, B, B), lambda h, ib: (h, 0, 0))
    dec = pl.BlockSpec((None, B, 1), lambda h, ib: (h, 0, 0))
    gam = pl.BlockSpec((None, 8, LANES), lambda h, ib: (h, 0, 0))
    gn = pl.BlockSpec((1, dv), lambda h, ib: (0, h))
    return dict(q=q, k=k, v=v, gate=gate, yv=yv, rope=rope, mask=mask, dec=dec, gam=gam, gn=gn)


def _group_norm(yr, gv, bv):
    mu = jnp.mean(yr, axis=-1, keepdims=True)
    cen = yr - mu
    rstd = lax.rsqrt(jnp.mean(cen * cen, axis=-1, keepdims=True) + EPS)
    nrm = cen * rstd
    return nrm, rstd, nrm * gv + bv


def _ret_fwd(proj, tb, gng, gnb, *, name):
    S, D = proj.shape[0], proj.shape[1] // 6
    dk, dv = D // RET_HEADS, 2 * D // RET_HEADS
    B = tb["B"]
    RB = _row_tile(S, 8 * B)
    nsub = RB // B
    sp = _ret_specs(S, D, B, RB, False)
    scale = dk ** -0.5

    def body(q_ref, k_ref, v_ref, gt_ref, cos_ref, sin_ref, mask_ref, xi_ref, zeta_ref, gam_ref, gng_ref, gnb_ref,
             yr_ref, yg_ref, qr_ref, kr_ref, state):
        @pl.when(pl.program_id(1) == 0)
        def _():
            state[...] = jnp.zeros_like(state)

        for sb in range(nsub):
            rows = slice(sb * B, (sb + 1) * B)
            cs, sn = cos_ref[rows, :], sin_ref[rows, :]
            q = _rope(q_ref[rows, :].astype(F32), cs, sn)
            k = _rope(k_ref[rows, :].astype(F32), cs, sn) * scale
            qr_ref[rows, :] = q.astype(BF16)
            kr_ref[rows, :] = k.astype(BF16)
            vb = v_ref[rows, :]
            p = _dot_nt(q, k) * mask_ref[...]
            st = state[...]
            yr = _dot(p, vb) + _dot(q * xi_ref[...], st)
            state[...] = st * gam_ref[0:1, 0:1] + _dot_tn(k * zeta_ref[...], vb)
            _, _, gn = _group_norm(yr, gng_ref[...], gnb_ref[...])
            gt = gt_ref[rows, :].astype(F32)
            yr_ref[rows, :] = yr.astype(BF16)
            yg_ref[rows, :] = (gt * _sigmoid(gt) * gn).astype(BF16)

    return _call(body, name=name, grid=(RET_HEADS, S // RB),
                 in_specs=[sp["q"], sp["k"], sp["v"], sp["gate"], sp["rope"], sp["rope"], sp["mask"], sp["dec"], sp["dec"],
                           sp["gam"], sp["gn"], sp["gn"]],
                 out_specs=(sp["yv"], sp["yv"], sp["q"], sp["q"]),
                 out_shape=(_sds((S, 2 * D), BF16), _sds((S, 2 * D), BF16), _sds((S, D), BF16), _sds((S, D), BF16)),
                 scratch=[pltpu.VMEM((dk, dv), F32)], sem=("parallel", "arbitrary"))(
                     proj, proj, proj, proj, tb["cos"], tb["sin"], tb["mask"], tb["xi"], tb["zeta"], tb["gam"], gng, gnb)


def _ret_bwd_q(proj, kr, yr, dyg, tb, gng, gnb, *, name):
    S, D = proj.shape[0], proj.shape[1] // 6
    dk, dv = D // RET_HEADS, 2 * D // RET_HEADS
    B = tb["B"]
    RB = _row_tile(S, 8 * B)
    nsub = RB // B
    sp = _ret_specs(S, D, B, RB, False)

    def body(k_ref, v_ref, gt_ref, yr_ref, dyg_ref, cos_ref, sin_ref, mask_ref, xi_ref, zeta_ref, gam_ref,
             gng_ref, gnb_ref, dq_ref, dgt_ref, dyr_ref, dgg_ref, dgb_ref, state):
        @pl.when(pl.program_id(1) == 0)
        def _():
            state[...] = jnp.zeros_like(state)
            dgg_ref[...] = jnp.zeros_like(dgg_ref)
            dgb_ref[...] = jnp.zeros_like(dgb_ref)

        for sb in range(nsub):
            rows = slice(sb * B, (sb + 1) * B)
            cs, sn = cos_ref[rows, :], sin_ref[rows, :]
            k = k_ref[rows, :]
            vb = v_ref[rows, :]
            gv = gng_ref[...]
            nrm, rstd, gn = _group_norm(yr_ref[rows, :].astype(F32), gv, gnb_ref[...])
            gt = gt_ref[rows, :].astype(F32)
            sg = _sigmoid(gt)
            dyg = dyg_ref[rows, :].astype(F32)
            dgt_ref[rows, :] = (dyg * gn * (sg * (1.0 + gt * (1.0 - sg)))).astype(BF16)
            dgn = dyg * (gt * sg)
            dgg_ref[...] += jnp.sum(dgn * nrm, axis=0, keepdims=True)
            dgb_ref[...] += jnp.sum(dgn, axis=0, keepdims=True)
            dn = dgn * gv
            dyr = rstd * (dn - jnp.mean(dn, axis=-1, keepdims=True) - nrm * jnp.mean(dn * nrm, axis=-1, keepdims=True))
            dyr_ref[rows, :] = dyr.astype(BF16)
            dp = _dot_nt(dyr, vb) * mask_ref[...]
            st = state[...]
            dq = _dot(dp, k) + _dot_nt(dyr, st) * xi_ref[...]
            dq_ref[rows, :] = _rope_t(dq, cs, sn).astype(BF16)
            state[...] = st * gam_ref[0:1, 0:1] + _dot_tn(k.astype(F32) * zeta_ref[...], vb)

    return _call(body, name=name, grid=(RET_HEADS, S // RB),
                 in_specs=[sp["q"], sp["v"], sp["gate"], sp["yv"], sp["yv"], sp["rope"], sp["rope"], sp["mask"],
                           sp["dec"], sp["dec"], sp["gam"], sp["gn"], sp["gn"]],
                 out_specs=(sp["q"], sp["yv"], sp["yv"], sp["gn"], sp["gn"]),
                 out_shape=(_sds((S, D), BF16), _sds((S, 2 * D), BF16), _sds((S, 2 * D), BF16), _sds((1, 2 * D), F32),
                            _sds((1, 2 * D), F32)),
                 scratch=[pltpu.VMEM((dk, dv), F32)], sem=("parallel", "arbitrary"))(
                     kr, proj, proj, yr, dyg, tb["cos"], tb["sin"], tb["mask"], tb["xi"], tb["zeta"], tb["gam"], gng, gnb)


def _ret_bwd_kv(proj, qr, kr, dyr, dq, dgt, tb, *, name):
    S, D = proj.shape[0], proj.shape[1] // 6
    dk, dv = D // RET_HEADS, 2 * D // RET_HEADS
    B = tb["B"]
    RB = _row_tile(S, 2 * B)
    nsub = RB // B
    nb = S // RB
    scale = dk ** -0.5

    def body(v_ref, qr_ref, kr_ref, dyr_ref, dq_ref, dgt_ref, cos_ref, sin_ref, mask_ref, xi_ref, zeta_ref, gam_ref, out_ref,
             dstate):
        @pl.when(pl.program_id(0) == 0)
        def _():
            dstate[...] = jnp.zeros_like(dstate)

        out_ref[:, 0:D] = dq_ref[...]
        out_ref[:, 4 * D:6 * D] = dgt_ref[...]
        for sb in reversed(range(nsub)):
            rows = slice(sb * B, (sb + 1) * B)
            cs, sn = cos_ref[rows, :], sin_ref[rows, :]
            for h in range(RET_HEADS):
                kcols = slice(D + h * dk, D + (h + 1) * dk)
                vcols = slice(2 * D + h * dv, 2 * D + (h + 1) * dv)
                q = qr_ref[rows, h * dk:(h + 1) * dk]
                k = kr_ref[rows, h * dk:(h + 1) * dk]
                vb = v_ref[rows, h * dv:(h + 1) * dv]
                dyr_h = dyr_ref[rows, h * dv:(h + 1) * dv]
                mk = mask_ref[h]
                p = _dot_nt(q, k) * mk
                dp = _dot_nt(dyr_h, vb) * mk
                ds = dstate[h]
                zt = zeta_ref[h]
                dkr = _dot_tn(dp, q) + _dot_nt(vb, ds) * zt
                out_ref[rows, kcols] = _rope_t(dkr * scale, cs, sn).astype(BF16)
                out_ref[rows, vcols] = (_dot_tn(p, dyr_h) + _dot(k.astype(F32) * zt, ds)).astype(BF16)
                dstate[h] = ds * gam_ref[h, 0:1, 0:1] + _dot_tn(q.astype(F32) * xi_ref[h], dyr_h)

    def rev(width):
        return pl.BlockSpec((RB, width), lambda ib: (nb - 1 - ib, 0))

    def whole(a):
        return pl.BlockSpec(a.shape, lambda ib: (0,) * a.ndim)

    return _call(body, name=name, grid=(nb,),
                 in_specs=[pl.BlockSpec((RB, 2 * D), lambda ib: (nb - 1 - ib, 1)), rev(D), rev(D), rev(2 * D), rev(D), rev(2 * D),
                           rev(dk // 2), rev(dk // 2), whole(tb["mask"]), whole(tb["xi"]), whole(tb["zeta"]), whole(tb["gam"])],
                 out_specs=rev(6 * D), out_shape=_sds((S, 6 * D), BF16), scratch=[pltpu.VMEM((RET_HEADS, dk, dv), F32)],
                 sem=("arbitrary",))(proj, qr, kr, dyr, dq, dgt, tb["cos"], tb["sin"], tb["mask"], tb["xi"], tb["zeta"],
                                     tb["gam"])


def _ada_fwd(c_all, ada_w, *, name):
    L, D, ns = ada_w.shape

    def body(c_ref, w_ref, out_ref):
        cv = c_ref[...]
        cond = cv * _sigmoid(cv)
        out_ref[...] = jnp.dot(cond.astype(BF16), w_ref[...].astype(BF16), preferred_element_type=F32)

    return _call(body, name=name, grid=(L,), in_specs=[pl.BlockSpec((NDEV, D), lambda l: (0, 0)),
                                                      pl.BlockSpec((None, D, ns), lambda l: (l, 0, 0))],
                 out_specs=pl.BlockSpec((None, NDEV, ns), lambda l: (l, 0, 0)), out_shape=_sds((L, NDEV, ns), F32),
                 sem=("parallel",))(c_all, ada_w)


def _ada_bwd(c_all, dmod_cols, *, name):
    L, _, ns = dmod_cols.shape
    D = c_all.shape[1]

    def body(c_ref, d_ref, out_ref):
        cv = c_ref[...]
        cond = cv * _sigmoid(cv)
        out_ref[...] = lax.dot_general(cond.astype(BF16), d_ref[...].astype(BF16), TN_DIMS, preferred_element_type=F32)

    return _call(body, name=name, grid=(L,), in_specs=[pl.BlockSpec((NDEV, D), lambda l: (0, 0)),
                                                      pl.BlockSpec((None, NDEV, ns), lambda l: (l, 0, 0))],
                 out_specs=pl.BlockSpec((None, D, ns), lambda l: (l, 0, 0)), out_shape=_sds((L, D, ns), F32),
                 sem=("parallel",))(c_all, dmod_cols)


def _adamw(w, m, v, parts, *, name):
    shape = w.shape
    L, cols = len(parts), shape[-1]
    rows = w.size // (cols * L)
    n = parts[0].shape[0]
    tr = rows
    for cand in (256, 128, 64, 32, 16, 8):
        if rows % cand == 0:
            tr = cand
            break
    c1 = 1.0 - ADAM_B1 ** ADAM_STEP
    c2 = 1.0 - ADAM_B2 ** ADAM_STEP

    def body(w_ref, m_ref, v_ref, *rest):
        p_refs = rest[:L]
        g_ref, d_ref, m2_ref, v2_ref = rest[L:]
        layer = pl.program_id(0)
        for l in range(L):
            @pl.when(layer == l)
            def _(p_ref=p_refs[l]):
                g = p_ref[0].astype(F32)
                for i in range(1, n):
                    g = g + p_ref[i].astype(F32)
                m2 = ADAM_B1 * m_ref[...] + (1.0 - ADAM_B1) * g
                v2 = ADAM_B2 * v_ref[...] + (1.0 - ADAM_B2) * (g * g)
                g_ref[...] = g
                m2_ref[...] = m2
                v2_ref[...] = v2
                d_ref[...] = -ADAM_LR * ((m2 / c1) / (jnp.sqrt(v2 / c2) + ADAM_EPS) + ADAM_WD * w_ref[...])

    mat = pl.BlockSpec((None, tr, cols), lambda l, i: (l, i, 0))

    def part_spec(k):
        return pl.BlockSpec((n, tr, cols), lambda l, i: (0, jnp.where(l == k, i, 0), 0))

    outs = _call(body, name=name, grid=(L, rows // tr), in_specs=[mat, mat, mat] + [part_spec(k) for k in range(L)],
                 out_specs=(mat, mat, mat, mat), out_shape=tuple(_sds((L, rows, cols), F32) for _ in range(4)),
                 sem=("parallel", "parallel"))(w.reshape(L, rows, cols), m.reshape(L, rows, cols), v.reshape(L, rows, cols),
                                               *[p.reshape(n, rows, cols) for p in parts])
    return tuple(o.reshape(shape) for o in outs)


SMALL = ("ada_b", "norm_mix_g", "norm_mlp_g", "conv_b_pw1", "conv_b_dw", "conv_ln_g", "conv_ln_b", "conv_b_pw2",
         "final_norm_g")
WEIGHTS = ("ada_w", "ada_b", "norm_mix_g", "norm_mlp_g", "conv_w_pw1", "conv_b_pw1", "conv_w_dw", "conv_b_dw", "conv_ln_g",
           "conv_ln_b", "conv_w_pw2", "conv_b_pw2", "ret_w_in", "ret_gn_g", "ret_gn_b", "ret_w_out", "mlp_w1", "mlp_w2",
           "final_norm_g")


def kernel(x, c, ada_w, ada_b, norm_mix_g, norm_mlp_g, conv_w_pw1, conv_b_pw1, conv_w_dw, conv_b_dw, conv_ln_g, conv_ln_b, conv_w_pw2, conv_b_pw2, ret_w_in, ret_gn_g, ret_gn_b, ret_w_out, mlp_w1, mlp_w2, final_norm_g, loss_target, m_ada_w, m_ada_b, m_norm_mix_g, m_norm_mlp_g, m_conv_w_pw1, m_conv_b_pw1, m_conv_w_dw, m_conv_b_dw, m_conv_ln_g, m_conv_ln_b, m_conv_w_pw2, m_conv_b_pw2, m_ret_w_in, m_ret_gn_g, m_ret_gn_b, m_ret_w_out, m_mlp_w1, m_mlp_w2, m_final_norm_g, v_ada_w, v_ada_b, v_norm_mix_g, v_norm_mlp_g, v_conv_w_pw1, v_conv_b_pw1, v_conv_w_dw, v_conv_b_dw, v_conv_ln_g, v_conv_ln_b, v_conv_w_pw2, v_conv_b_pw2, v_ret_w_in, v_ret_gn_g, v_ret_gn_b, v_ret_w_out, v_mlp_w1, v_mlp_w2, v_final_norm_g):
    W = dict(ada_w=ada_w, ada_b=ada_b, norm_mix_g=norm_mix_g, norm_mlp_g=norm_mlp_g, conv_w_pw1=conv_w_pw1,
             conv_b_pw1=conv_b_pw1, conv_w_dw=conv_w_dw, conv_b_dw=conv_b_dw, conv_ln_g=conv_ln_g, conv_ln_b=conv_ln_b,
             conv_w_pw2=conv_w_pw2, conv_b_pw2=conv_b_pw2, ret_w_in=ret_w_in, ret_gn_g=ret_gn_g, ret_gn_b=ret_gn_b,
             ret_w_out=ret_w_out, mlp_w1=mlp_w1, mlp_w2=mlp_w2, final_norm_g=final_norm_g)
    Mo = dict(ada_w=m_ada_w, ada_b=m_ada_b, norm_mix_g=m_norm_mix_g, norm_mlp_g=m_norm_mlp_g, conv_w_pw1=m_conv_w_pw1,
              conv_b_pw1=m_conv_b_pw1, conv_w_dw=m_conv_w_dw, conv_b_dw=m_conv_b_dw, conv_ln_g=m_conv_ln_g,
              conv_ln_b=m_conv_ln_b, conv_w_pw2=m_conv_w_pw2, conv_b_pw2=m_conv_b_pw2, ret_w_in=m_ret_w_in,
              ret_gn_g=m_ret_gn_g, ret_gn_b=m_ret_gn_b, ret_w_out=m_ret_w_out, mlp_w1=m_mlp_w1, mlp_w2=m_mlp_w2,
              final_norm_g=m_final_norm_g)
    Vo = dict(ada_w=v_ada_w, ada_b=v_ada_b, norm_mix_g=v_norm_mix_g, norm_mlp_g=v_norm_mlp_g, conv_w_pw1=v_conv_w_pw1,
              conv_b_pw1=v_conv_b_pw1, conv_w_dw=v_conv_w_dw, conv_b_dw=v_conv_b_dw, conv_ln_g=v_conv_ln_g,
              conv_ln_b=v_conv_ln_b, conv_w_pw2=v_conv_w_pw2, conv_b_pw2=v_conv_b_pw2, ret_w_in=v_ret_w_in,
              ret_gn_g=v_ret_gn_g, ret_gn_b=v_ret_gn_b, ret_w_out=v_ret_w_out, mlp_w1=v_mlp_w1, mlp_w2=v_mlp_w2,
              final_norm_g=v_final_norm_g)

    S, D = x.shape[1], x.shape[2]
    CH = D // LANES
    n_conv, n_ret = conv_w_pw1.shape[0], ret_w_in.shape[0]
    me = 4 * lax.axis_index("x") + 2 * lax.axis_index("y") + lax.axis_index("c")
    xs = x.reshape(S, D)
    target = loss_target.reshape(S, D)

    def mixer_shards(i):
        j = i // 2
        if i % 2 == 0:
            return [[conv_w_pw1[j].astype(BF16)], [conv_w_pw2[j].astype(BF16)]]
        return [[ret_w_in[j].astype(BF16)], [ret_w_out[j].astype(BF16)]]

    def mlp_shards(i):
        return [[mlp_w1[i].astype(BF16)], [mlp_w2[i].astype(BF16)]]

    def mlp_weights(got):
        return got[0], got[1].reshape(4 * D, D)

    first_handle, _ = _exchange_start(mixer_shards(0)[:1], gather=True, name="gather_start_first")
    small = _exchange([[conv_w_dw], [ret_gn_g], [ret_gn_b], [c]], gather=True, name="gather_small")
    dw_g, gng_g, gnb_g, c_g = small
    dw3 = jnp.transpose(dw_g, (1, 2, 0, 3)).reshape(n_conv, CONV_WIDTH, CH, LANES)
    gng_full = jnp.transpose(gng_g, (1, 2, 0, 3)).reshape(n_ret, 1, 2 * D)
    gnb_full = jnp.transpose(gnb_g, (1, 2, 0, 3)).reshape(n_ret, 1, 2 * D)
    c_all = c_g.reshape(NDEV, D)

    mod_cols = _ada_fwd(c_all, ada_w, name="ada_fwd")
    mod_all = _exchange([[mod_cols]], gather=True, name="gather_mod")[0]
    mod = lax.dynamic_index_in_dim(mod_all, me, axis=2, keepdims=False)
    mod = jnp.transpose(mod, (1, 0, 2)).reshape(DEPTH, 6 * D) + ada_b
    mods = [[mod[i, j * D:(j + 1) * D].reshape(1, D) for j in range(6)] for i in range(DEPTH)]
    tb = _ret_tables(S, D // RET_HEADS)

    def vec(a):
        return a.reshape(1, -1)

    def group_a(i):
        return mixer_shards(i) if i % 2 == 0 else mixer_shards(i)[:1]

    def group_b(i):
        return mlp_shards(i) if i % 2 == 0 else mixer_shards(i)[1:] + mlp_shards(i)

    mix_first = _exchange_wait(first_handle, name="gather_wait_first", after=mod)[0]
    pw2_handle, token = _exchange_start(mixer_shards(0)[1:], gather=True, name="gather_start_pw2_0", after=mix_first)
    handle_b, token = _exchange_start(mlp_shards(0), gather=True, name="gather_start_b0", after=token)
    saved = []
    weights = []
    xcur = xs
    for i in range(DEPTH):
        sh1, sc1, g1, sh2, sc2, g2 = mods[i]
        j = i // 2
        if i > 0:
            got = _exchange_wait(handle_a, name=f"gather_wait_a{i}", after=xcur)
            mix_first = got[0]
            mix_second = got[1].reshape(-1, D) if i % 2 == 0 else None
            handle_b, token = _exchange_start(group_b(i), gather=True, name=f"gather_start_b{i}", after=got[0])
        st = dict(x_in=xcur)
        norm1 = (vec(norm_mix_g[i]), sc1, sh1)
        if i % 2 == 0:
            u, h = _mm_nn(xcur, mix_first, norm=norm1, bias=vec(conv_b_pw1[j]), out_dtype=F32, name=f"pw1_fwd{i}", after=token)
            dwo = _conv_mid_fwd(u, dw3[j], conv_b_dw[j].reshape(1, CH, LANES), name=f"conv_mid_fwd{i}")
            if i == 0:
                mix_second = _exchange_wait(pw2_handle, name="gather_wait_pw2_0", after=dwo)[0].reshape(-1, D)
            xcur, y_raw = _mm_nn(dwo, mix_second, ln=(vec(conv_ln_g[j]), vec(conv_ln_b[j])), bias=vec(conv_b_pw2[j]), res=xcur,
                                 gate=g1, name=f"pw2_fwd{i}")
            st.update(u=u, dwo=dwo, y_raw=y_raw)
            got = _exchange_wait(handle_b, name=f"gather_wait_b{i}", after=xcur)
            mlp_w = mlp_weights(got)
        else:
            proj, h = _mm_nn(xcur, mix_first, norm=norm1, name=f"ret_in_fwd{i}", after=token)
            yr, yg, qr, kr = _ret_fwd(proj, tb, gng_full[j], gnb_full[j], name=f"ret_fwd{i}")
            got = _exchange_wait(handle_b, name=f"gather_wait_b{i}", after=yg)
            mix_second, mlp_w = got[0].reshape(-1, D), mlp_weights(got[1:3])
            xcur, y_raw = _mm_nn(yg, mix_second, res=xcur, gate=g1, name=f"ret_out_fwd{i}")
            st.update(proj=proj, yr=yr, yg=yg, qr=qr, kr=kr, y_raw=y_raw)
        st.update(h=h, x_mid=xcur)
        if i + 1 < DEPTH:
            handle_a, token = _exchange_start(group_a(i + 1), gather=True, name=f"gather_start_a{i + 1}", after=got[0])
        z, h2 = _mm_nn(xcur, mlp_w[0], norm=(vec(norm_mlp_g[i]), sc2, sh2), name=f"mlp1_fwd{i}", after=token)
        xcur, o_raw = _mm_nn(z, mlp_w[1], relu2=True, res=xcur, gate=g2, name=f"mlp2_fwd{i}")
        st.update(h2=h2, z=z, o_raw=o_raw)
        saved.append(st)
        weights.append((mix_first, mix_second) + mlp_w)

    g2_last = mods[DEPTH - 1][5]
    loss_local, dx, d_final_g, dy, dgate, _ = _final_loss(xcur, vec(final_norm_g), target, saved[-1]["o_raw"], g2_last,
                                                          name="final_loss")
    loss = lax.psum(loss_local[0, 0], AXES)

    dmod_rows = [None] * DEPTH
    d_mix_g, d_mlp_g = [None] * DEPTH, [None] * DEPTH
    d_pw1, d_pw2, d_win, d_wout = [None] * n_conv, [None] * n_conv, [None] * n_ret, [None] * n_ret
    d_w1, d_w2 = [None] * DEPTH, [None] * DEPTH
    d_bpw1, d_bdw, d_lng, d_lnb, d_bpw2, d_dw = ([None] * n_conv for _ in range(6))
    d_gng, d_gnb = [None] * n_ret, [None] * n_ret

    def gn_parts(d):
        return jnp.transpose(d.reshape(RET_HEADS, NDEV, -1), (1, 0, 2))

    grad_handles = [None] * DEPTH
    token = None
    for i in reversed(range(DEPTH)):
        sh1, sc1, g1, sh2, sc2, g2 = mods[i]
        j = i // 2
        st = saved[i]
        mix_a, mix_b, w1_i, w2_i = weights[i]
        do, dg2 = dy, dgate
        dz = _mm_nt(do, w2_i, z=st["z"], out_dtype=BF16, name=f"mlp2_bwd_x{i}", after=token)
        d_w2[i] = _mm_tn(st["z"], do, relu2=True, name=f"mlp2_bwd_w{i}")
        dx, dsc2, dsh2, d_mlp_g[i], dy, dg1, dby = _mm_nt(dz, w1_i, norm=(st["x_mid"], vec(norm_mlp_g[i]), sc2, dx),
                                                          gated=(st["y_raw"], g1), name=f"mlp1_bwd_x{i}")
        d_w1[i] = _mm_tn(st["h2"], dz, col_shards=NDEV, name=f"mlp1_bwd_w{i}")
        mlp_groups = [[d_w1[i]], [d_w2[i].reshape(NDEV, 4 * D // NDEV, D)]]
        token = None
        if i == 0:
            mlp0_handle, token = _exchange_start(mlp_groups, gather=False, name="grads_start_mlp0")
            mlp_groups = []
        if i % 2 == 0:
            d_bpw2[j] = dby
            ds = _mm_nt(dy, mix_b, name=f"pw2_bwd_x{i}", after=token)
            ln_gb = (vec(conv_ln_g[j]), vec(conv_ln_b[j]))
            d_pw2[j] = _mm_tn(st["dwo"], dy, ln=ln_gb, name=f"pw2_bwd_w{i}")
            ddw, d_lng[j], d_lnb[j], d_bdw[j] = _ln_silu_bwd(st["dwo"], ds, *ln_gb, name=f"conv_ln_bwd{i}")
            du, ddw_w, dbu = _conv_mid_bwd_dw(st["u"], ddw, dw3[j], name=f"conv_mid_bwd_dw{i}")
            d_dw[j], d_bpw1[j] = ddw_w.reshape(CONV_WIDTH, D), dbu.reshape(2, D)
            d_pw1[j] = _mm_tn(st["h"], du, col_shards=NDEV, name=f"pw1_bwd_w{i}")
            mix_groups = [[d_pw1[j]], [d_pw2[j].reshape(NDEV, D // NDEV, D)],
                          [jnp.transpose(d_dw[j].reshape(CONV_WIDTH, NDEV, D // NDEV), (1, 0, 2))]]
            mix_in, mix_name = du, f"pw1_bwd_x{i}"
        else:
            dyg = _mm_nt(dy, mix_b, out_dtype=BF16, name=f"ret_out_bwd_x{i}")
            d_wout[j] = _mm_tn(st["yg"], dy, name=f"ret_out_bwd_w{i}")
            dq, dgt, dyr, d_gng[j], d_gnb[j] = _ret_bwd_q(st["proj"], st["kr"], st["yr"], dyg, tb, gng_full[j], gnb_full[j],
                                                          name=f"ret_bwd_q{i}")
            dproj = _ret_bwd_kv(st["proj"], st["qr"], st["kr"], dyr, dq, dgt, tb, name=f"ret_bwd_kv{i}")
            d_win[j] = _mm_tn(st["h"], dproj, col_shards=NDEV, name=f"ret_in_bwd_w{i}")
            mix_in, mix_name = dproj, f"ret_in_bwd_x{i}"
            mix_groups = [[d_win[j]], [d_wout[j].reshape(NDEV, 2 * D // NDEV, D)], [gn_parts(d_gng[j])],
                          [gn_parts(d_gnb[j])]]
        grad_handles[i], token = _exchange_start(mix_groups + mlp_groups, gather=False, name=f"grads_start{i}")
        gated = (saved[i - 1]["o_raw"], mods[i - 1][5]) if i > 0 else None
        outs = _mm_nt(mix_in, mix_a, norm=(st["x_in"], vec(norm_mix_g[i]), sc1, dx), gated=gated, name=mix_name, after=token)
        dx, dsc1, dsh1, d_mix_g[i] = outs[:4]
        if i > 0:
            dy, dgate = outs[4], outs[5]
        dmod_rows[i] = jnp.concatenate([dsh1, dsc1, dg1, dsh2, dsc2, dg2], axis=0)
    grad_x = dx.reshape(1, S, D)

    small_local = jnp.concatenate(dmod_rows + d_mix_g + d_mlp_g + d_bpw1 + d_bdw + d_lng + d_lnb + d_bpw2 + [d_final_g],
                                  axis=0)
    small_all = _exchange([[small_local]], gather=True, name="gather_small_grads")[0]

    def pack(src):
        return jnp.concatenate([src[n].reshape(-1, D) for n in SMALL], axis=0)[None]

    sm = _adamw(pack(W), pack(Mo), pack(Vo), [small_all], name="adamw_small")
    results = {}
    row = 0
    for n in SMALL:
        cnt = W[n].size // D
        results[n] = tuple(o[0, row:row + cnt].reshape(W[n].shape) for o in sm)
        row += cnt

    ns_ada = ada_w.shape[2]
    dmod_all = small_all[:, :6 * DEPTH, :].reshape(NDEV, DEPTH, 6 * D)
    dmod_cols = jnp.transpose(lax.dynamic_slice_in_dim(dmod_all, me * ns_ada, ns_ada, axis=2), (1, 0, 2))
    g_ada = _ada_bwd(c_all, dmod_cols, name="ada_bwd")
    flat_ada = (1, DEPTH * D, ns_ada)
    ada_res = _adamw(ada_w.reshape(flat_ada), m_ada_w.reshape(flat_ada), v_ada_w.reshape(flat_ada),
                     [g_ada.reshape(flat_ada)], name="adamw_ada_w")
    results["ada_w"] = tuple(o.reshape(ada_w.shape) for o in ada_res)

    def update(names, parts):
        for n in names:
            results[n] = _adamw(W[n], Mo[n], Vo[n], parts[n], name=f"adamw_{n}")

    got = {i: _exchange_wait(grad_handles[i], name=f"grads_wait{i}", after=dx) for i in range(DEPTH - 1, 0, -1)}
    ret_layers = [i for i in range(DEPTH) if i % 2 == 1]
    update(("ret_w_in", "ret_w_out", "ret_gn_g", "ret_gn_b"),
           dict(ret_w_in=[got[i][0] for i in ret_layers], ret_w_out=[got[i][1] for i in ret_layers],
                ret_gn_g=[got[i][2] for i in ret_layers], ret_gn_b=[got[i][3] for i in ret_layers]))
    got_mlp0 = _exchange_wait(mlp0_handle, name="grads_wait_mlp0", after=results["ret_w_in"][0])
    update(("mlp_w1", "mlp_w2"),
           dict(mlp_w1=[got_mlp0[0]] + [got[i][-2] for i in range(1, DEPTH)],
                mlp_w2=[got_mlp0[1]] + [got[i][-1] for i in range(1, DEPTH)]))
    got[0] = _exchange_wait(grad_handles[0], name="grads_wait0", after=results["mlp_w1"][0])
    conv_layers = [i for i in range(DEPTH) if i % 2 == 0]
    update(("conv_w_pw1", "conv_w_pw2", "conv_w_dw"),
           dict(conv_w_pw1=[got[i][0] for i in conv_layers], conv_w_pw2=[got[i][1] for i in conv_layers],
                conv_w_dw=[got[i][2] for i in conv_layers]))

    outs = [loss, grad_x]
    for kind in range(4):
        outs += [results[n][kind] for n in WEIGHTS]
    return tuple(outs)
```

```python
import functools

import jax
import jax.numpy as jnp
import numpy as np
from jax import lax
from jax.experimental import pallas as pl
from jax.experimental.pallas import tpu as pltpu

F32, BF16 = jnp.float32, jnp.bfloat16
AXES = ("x", "y", "c")
NDEV = 8
DEPTH = 4
EPS = 1e-6
CHUNK = 64
CONV_WIDTH = 31
HALO = 32
RET_HEADS = 4
RET_BLOCK = 256
ROPE_BASE = 10000.0
LANES = 128
ADAM_LR, ADAM_B1, ADAM_B2, ADAM_EPS, ADAM_WD, ADAM_STEP = 0.001, 0.9, 0.999, 1e-08, 0.01, 10
VMEM_LIMIT = 56 * 1024 * 1024
VMEM_BLOCK_BUDGET = 44 * 1024 * 1024
MESH = pl.DeviceIdType.MESH
NT_DIMS = (((1,), (1,)), ((), ()))
TN_DIMS = (((0,), (0,)), ((), ()))


def _call(body, *, name, out_shape, in_specs, out_specs, grid=(), scratch=(), sem=None, aliases=None):
    params = dict(vmem_limit_bytes=VMEM_LIMIT)
    if sem is not None:
        params["dimension_semantics"] = sem
    return pl.pallas_call(body, name=name, grid=grid, in_specs=in_specs, out_specs=out_specs, out_shape=out_shape,
                          scratch_shapes=list(scratch), input_output_aliases=aliases or {},
                          compiler_params=pltpu.CompilerParams(**params))


def _row_tile(rows, want):
    t = min(rows, want)
    while rows % t:
        t //= 2
    return t


def _sds(shape, dtype):
    return jax.ShapeDtypeStruct(tuple(shape), dtype)


def _sigmoid(v):
    return 1.0 / (1.0 + jnp.exp(-v))


def _exchange(groups, *, gather, name):
    flat = [a for g in groups for a in g]
    n_in = len(flat)
    out_shapes = []
    for g in groups:
        s = g[0].shape if gather else g[0].shape[1:]
        lead = (NDEV,) if len(g) == 1 else (NDEV, len(g))
        out_shapes.append(_sds(lead + tuple(s), g[0].dtype))
    n_g = len(groups)

    def body(*refs):
        ins, outs = refs[:n_in], refs[n_in:n_in + n_g]
        send_sems, recv_sems, loc_sems = refs[n_in + n_g:]
        x, y, c = lax.axis_index("x"), lax.axis_index("y"), lax.axis_index("c")
        me = 4 * x + 2 * y + c
        locs, k = [], 0
        for gi, g in enumerate(groups):
            for li in range(len(g)):
                src = ins[k] if gather else ins[k].at[me]
                dst = outs[gi].at[me] if len(g) == 1 else outs[gi].at[me, li]
                cp = pltpu.make_async_copy(src, dst, loc_sems.at[k])
                cp.start()
                locs.append(cp)
                k += 1
        k0 = 0
        for gi, g in enumerate(groups):
            for r in range(1, NDEV):
                px = 1 - x if r & 4 else x
                py = 1 - y if r & 2 else y
                pc = 1 - c if r & 1 else c
                peer = 4 * px + 2 * py + pc
                for li in range(len(g)):
                    src = ins[k0 + li] if gather else ins[k0 + li].at[peer]
                    dst = outs[gi].at[me] if len(g) == 1 else outs[gi].at[me, li]
                    pltpu.make_async_remote_copy(src_ref=src, dst_ref=dst, send_sem=send_sems.at[gi * (NDEV - 1) + r - 1],
                                                 recv_sem=recv_sems.at[gi * (NDEV - 1) + r - 1], device_id=(px, py, pc),
                                                 device_id_type=MESH).start()
            k0 += len(g)
        for gi, g in enumerate(groups):
            for r in range(1, NDEV):
                px = 1 - x if r & 4 else x
                py = 1 - y if r & 2 else y
                pc = 1 - c if r & 1 else c
                peer = 4 * px + 2 * py + pc
                slab = pltpu.make_async_remote_copy(src_ref=outs[gi].at[me], dst_ref=outs[gi].at[peer],
                                                    send_sem=send_sems.at[gi * (NDEV - 1) + r - 1], recv_sem=recv_sems.at[gi * (NDEV - 1) + r - 1],
                                                    device_id=(px, py, pc), device_id_type=MESH)
                slab.wait_send()
                slab.wait_recv()
        for cp in locs:
            cp.wait()

    hbm = pl.BlockSpec(memory_space=pltpu.HBM)
    outs = _call(body, name=name, out_shape=tuple(out_shapes), in_specs=[hbm] * n_in, out_specs=tuple([hbm] * n_g),
                 scratch=[pltpu.SemaphoreType.DMA((n_g * (NDEV - 1),)), pltpu.SemaphoreType.DMA((n_g * (NDEV - 1),)),
                          pltpu.SemaphoreType.DMA((n_in,))])(*flat)
    return list(outs)


def _peer_of(x, y, c, r):
    return (1 - x if r & 4 else x, 1 - y if r & 2 else y, 1 - c if r & 1 else c)


def _exchange_start(groups, *, gather, name, after=None):
    flat = [pltpu.with_memory_space_constraint(a, pltpu.HBM) for g in groups for a in g]
    n_in, n_g = len(flat), len(groups)
    land_shapes = []
    for g in groups:
        s = g[0].shape if gather else g[0].shape[1:]
        lead = (NDEV,) if len(g) == 1 else (NDEV, len(g))
        land_shapes.append((lead + tuple(s), g[0].dtype))
    lands = [pltpu.with_memory_space_constraint(lax.empty(s, d), pltpu.HBM) for s, d in land_shapes]
    n_after = 0 if after is None else 1

    def body(*refs):
        ins, land = refs[:n_in], refs[n_in:n_in + n_g]
        send_sems, recv_sems, loc_sems = refs[n_in + n_g + n_after:n_in + n_g + n_after + 3]
        token = refs[-1]
        x, y, c = lax.axis_index("x"), lax.axis_index("y"), lax.axis_index("c")
        me = 4 * x + 2 * y + c
        k = 0
        for gi, g in enumerate(groups):
            for li in range(len(g)):
                dst = land[gi].at[me] if len(g) == 1 else land[gi].at[me, li]
                pltpu.make_async_copy(ins[k] if gather else ins[k].at[me], dst, loc_sems.at[k]).start()
                k += 1
        k0 = 0
        for gi, g in enumerate(groups):
            for r in range(1, NDEV):
                px, py, pc = _peer_of(x, y, c, r)
                peer = 4 * px + 2 * py + pc
                for li in range(len(g)):
                    dst = land[gi].at[me] if len(g) == 1 else land[gi].at[me, li]
                    pltpu.make_async_remote_copy(src_ref=ins[k0 + li] if gather else ins[k0 + li].at[peer], dst_ref=dst,
                                                 send_sem=send_sems.at[gi * (NDEV - 1) + r - 1], recv_sem=recv_sems.at[gi * (NDEV - 1) + r - 1],
                                                 device_id=(px, py, pc), device_id_type=MESH).start()
            k0 += len(g)
        token[...] = jnp.zeros_like(token)

    hbm = pl.BlockSpec(memory_space=pltpu.HBM)
    sem = pl.BlockSpec(memory_space=pltpu.SEMAPHORE)
    args = flat + lands + ([after] if n_after else [])
    outs = pl.pallas_call(body, name=name,
        out_shape=(pltpu.SemaphoreType.DMA((n_g * (NDEV - 1),)), pltpu.SemaphoreType.DMA((n_g * (NDEV - 1),)),
                   pltpu.SemaphoreType.DMA((n_in,)), *[pltpu.HBM(a.shape, a.dtype) for a in flat],
                   *[pltpu.HBM(s, d) for s, d in land_shapes], _sds((8, LANES), F32)),
        in_specs=[hbm] * (n_in + n_g) + [pl.BlockSpec(memory_space=pl.ANY)] * n_after,
        out_specs=(sem, sem, sem, *[hbm] * (n_in + n_g), pl.BlockSpec(memory_space=pltpu.VMEM)),
        input_output_aliases={k: 3 + k for k in range(n_in + n_g)},
        compiler_params=pltpu.CompilerParams(has_side_effects=pltpu.SideEffectType.DATAFLOW_SIDE_EFFECTING))(*args)
    handle = dict(sems=outs[0:3], srcs=list(outs[3:3 + n_in]), lands=list(outs[3 + n_in:3 + n_in + n_g]),
                  sizes=[len(g) for g in groups], gather=gather)
    return handle, outs[-1]


def _exchange_wait(handle, *, name, after):
    srcs, lands, sizes, gather = handle["srcs"], handle["lands"], handle["sizes"], handle["gather"]
    n_in, n_g = len(srcs), len(lands)

    def body(*refs):
        ins, land = refs[:n_in], refs[n_in:n_in + n_g]
        send_sems, recv_sems, loc_sems = refs[n_in + n_g:n_in + n_g + 3]
        x, y, c = lax.axis_index("x"), lax.axis_index("y"), lax.axis_index("c")
        me = 4 * x + 2 * y + c
        for gi in range(n_g):
            for r in range(1, NDEV):
                px, py, pc = _peer_of(x, y, c, r)
                peer = 4 * px + 2 * py + pc
                slab = pltpu.make_async_remote_copy(src_ref=land[gi].at[me], dst_ref=land[gi].at[peer],
                                                    send_sem=send_sems.at[gi * (NDEV - 1) + r - 1], recv_sem=recv_sems.at[gi * (NDEV - 1) + r - 1],
                                                    device_id=(px, py, pc), device_id_type=MESH)
                slab.wait_send()
                slab.wait_recv()
        k = 0
        for gi in range(n_g):
            for li in range(sizes[gi]):
                dst = land[gi].at[me] if sizes[gi] == 1 else land[gi].at[me, li]
                pltpu.make_async_copy(ins[k] if gather else ins[k].at[me], dst, loc_sems.at[k]).wait()
                k += 1

    hbm = pl.BlockSpec(memory_space=pltpu.HBM)
    sem = pl.BlockSpec(memory_space=pltpu.SEMAPHORE)
    outs = pl.pallas_call(body, name=name, out_shape=tuple(pltpu.HBM(a.shape, a.dtype) for a in srcs + lands),
        in_specs=[hbm] * (n_in + n_g) + [sem] * 3 + [pl.BlockSpec(memory_space=pl.ANY)],
        out_specs=tuple([hbm] * (n_in + n_g)), input_output_aliases={k: k for k in range(n_in + n_g)},
        compiler_params=pltpu.CompilerParams(has_side_effects=pltpu.SideEffectType.DATAFLOW_SIDE_EFFECTING))(
            *srcs, *lands, *handle["sems"], after)
    return list(outs[n_in:])


def _gate_part(first, dx, y_ref, g_ref, dy_ref, dg_ref, db_ref):
    @pl.when(first)
    def _():
        dg_ref[...] = jnp.zeros_like(dg_ref)
        db_ref[...] = jnp.zeros_like(db_ref)

    dy = dx * g_ref[...]
    dy_ref[...] = dy.astype(BF16)
    dg_ref[...] += jnp.sum(dx * y_ref[...].astype(F32), axis=0, keepdims=True)
    db_ref[...] += jnp.sum(dy, axis=0, keepdims=True)


def _norm_bwd_part(first, dhv, x_ref, g_ref, sc_ref, dres_ref, dx_ref, dsc_ref, dsh_ref, dg_ref):
    @pl.when(first)
    def _():
        dsc_ref[...] = jnp.zeros_like(dsc_ref)
        dsh_ref[...] = jnp.zeros_like(dsh_ref)
        dg_ref[...] = jnp.zeros_like(dg_ref)

    xv = x_ref[...]
    r = lax.rsqrt(jnp.mean(xv * xv, axis=-1, keepdims=True) + EPS)
    xhat = xv * r
    gain_v = g_ref[...]
    dsc_ref[...] += jnp.sum(dhv * (xhat * gain_v), axis=0, keepdims=True)
    dsh_ref[...] += jnp.sum(dhv, axis=0, keepdims=True)
    dxn = dhv * (1.0 + sc_ref[...])
    dg_ref[...] += jnp.sum(dxn * xhat, axis=0, keepdims=True)
    dxhat = dxn * gain_v
    dx = dres_ref[...] + r * (dxhat - xhat * jnp.mean(dxhat * xhat, axis=-1, keepdims=True))
    dx_ref[...] = dx
    return dx


def _final_loss(x, gain, target, y_prev, gate_prev, *, name):
    S, D = x.shape
    tm = _row_tile(S, 512)

    def body(x_ref, g_ref, t_ref, y_ref, gp_ref, loss_ref, dx_ref, dg_ref, dy_ref, dgp_ref, dbp_ref):
        first = pl.program_id(0) == 0

        @pl.when(first)
        def _():
            loss_ref[...] = jnp.zeros_like(loss_ref)
            dg_ref[...] = jnp.zeros_like(dg_ref)

        xv = x_ref[...]
        r = lax.rsqrt(jnp.mean(xv * xv, axis=-1, keepdims=True) + EPS)
        xhat = xv * r
        gv = g_ref[...]
        err = xhat * gv - t_ref[...]
        row_loss = jnp.mean(err * err, axis=-1, keepdims=True)
        loss_ref[...] += 0.5 * jnp.sum(row_loss, axis=0, keepdims=True)
        dy = err * (1.0 / D)
        dg_ref[...] += jnp.sum(dy * xhat, axis=0, keepdims=True)
        dxhat = dy * gv
        dx = r * (dxhat - xhat * jnp.mean(dxhat * xhat, axis=-1, keepdims=True))
        dx_ref[...] = dx
        _gate_part(first, dx, y_ref, gp_ref, dy_ref, dgp_ref, dbp_ref)

    row = pl.BlockSpec((tm, D), lambda i: (i, 0))
    vec = pl.BlockSpec((1, D), lambda i: (0, 0))
    one = pl.BlockSpec((1, 1), lambda i: (0, 0))
    vsh = _sds((1, D), F32)
    return _call(body, name=name, grid=(S // tm,), in_specs=[row, vec, row, row, vec],
                 out_specs=(one, row, vec, row, vec, vec),
                 out_shape=(_sds((1, 1), F32), _sds((S, D), F32), vsh, _sds((S, D), BF16), vsh, vsh),
                 sem=("arbitrary",))(x, gain, target, y_prev, gate_prev)


def _pick_tm(M, bytes_per_row, fixed_bytes):
    for tm in (1024, 512, 256, 128):
        if M % tm == 0 and 2 * tm * bytes_per_row + fixed_bytes <= VMEM_BLOCK_BUDGET:
            return tm
    return _row_tile(M, 128)


def _mm_nn(a, w, *, name, bias=None, relu2=False, ln=None, norm=None, res=None, gate=None, out_dtype=BF16, after=None):
    M, K = a.shape
    col = w.ndim == 3
    if col:
        nsh, ns = w.shape[0], w.shape[2]
        w_spec = pl.BlockSpec((nsh, K, ns), lambda i: (0, 0, 0))
    else:
        nsh, ns = 1, w.shape[1]
        w_spec = pl.BlockSpec((K, ns), lambda i: (0, 0))
    N = nsh * ns
    residual = res is not None
    out_bytes = (4 + 4 + 2) if residual else jnp.dtype(out_dtype).itemsize
    tm = _pick_tm(M, K * a.dtype.itemsize + N * out_bytes + (K * 2 if norm is not None else 0), 2 * K * N * 2)

    def body(*refs):
        it = iter(refs)
        a_ref, w_ref = next(it), next(it)
        b_ref = next(it) if bias is not None else None
        lg_ref, lb_ref = (next(it), next(it)) if ln is not None else (None, None)
        ng_ref, nsc_ref, nsh_ref = (next(it), next(it), next(it)) if norm is not None else (None, None, None)
        res_ref, gate_ref = (next(it), next(it)) if residual else (None, None)
        if after is not None:
            next(it)
        out_ref = next(it)
        raw_ref = next(it) if residual else None
        av = a_ref[...]
        if relu2:
            av = jnp.square(jnp.maximum(av.astype(F32), 0.0))
        if ln is not None:
            av, _ = _ln_silu(av, lg_ref[...], lb_ref[...])
        if norm is not None:
            r = lax.rsqrt(jnp.mean(av * av, axis=-1, keepdims=True) + EPS)
            av = (av * r) * ng_ref[...] * (1.0 + nsc_ref[...]) + nsh_ref[...]
        ab = av.astype(BF16)
        if norm is not None:
            next(it)[...] = ab
        for d in range(nsh):
            cols = slice(d * ns, (d + 1) * ns)
            acc = jnp.dot(ab, w_ref[d] if col else w_ref[...], preferred_element_type=F32)
            if b_ref is not None:
                acc = acc + b_ref[:, cols]
            if residual:
                raw_ref[:, cols] = acc.astype(BF16)
                out_ref[:, cols] = res_ref[:, cols] + gate_ref[:, cols] * acc
            else:
                out_ref[:, cols] = acc.astype(out_dtype)

    tile = pl.BlockSpec((tm, N), lambda i: (i, 0))
    vec = pl.BlockSpec((1, N), lambda i: (0, 0))
    in_specs, args = [pl.BlockSpec((tm, K), lambda i: (i, 0)), w_spec], [a, w]
    if bias is not None:
        in_specs.append(vec)
        args.append(bias)
    if ln is not None:
        in_specs += [pl.BlockSpec((1, K), lambda i: (0, 0))] * 2
        args += list(ln)
    if norm is not None:
        in_specs += [pl.BlockSpec((1, K), lambda i: (0, 0))] * 3
        args += list(norm)
    if residual:
        in_specs += [tile, vec]
        args += [res, gate]
        out_specs = [tile, tile]
        out_shape = [_sds((M, N), F32), _sds((M, N), BF16)]
    else:
        out_specs = [tile]
        out_shape = [_sds((M, N), out_dtype)]
    if after is not None:
        in_specs.append(pl.BlockSpec(memory_space=pl.ANY))
        args.append(after)
    if norm is not None:
        out_specs.append(pl.BlockSpec((tm, K), lambda i: (i, 0)))
        out_shape.append(_sds((M, K), BF16))
    outs = _call(body, name=name, grid=(M // tm,), in_specs=in_specs, out_specs=tuple(out_specs), out_shape=tuple(out_shape),
                 sem=("parallel",))(*args)
    return outs[0] if len(outs) == 1 else outs


def _mm_nt(g, w, *, name, z=None, out_dtype=F32, after=None, norm=None, gated=None):
    M, N = g.shape
    col = w.ndim == 3
    if col:
        nsh, K, ns = w.shape
        w_spec = pl.BlockSpec((nsh, K, ns), lambda i: (0, 0, 0))
    else:
        K = w.shape[0]
        w_spec = pl.BlockSpec((K, N), lambda i: (0, 0))
    assert norm is None or col
    kc = min(K, 1024)
    obytes = jnp.dtype(out_dtype).itemsize
    row_bytes = N * g.dtype.itemsize + K * obytes + (K * 2 if z is not None else 0)
    if norm is not None:
        row_bytes += 2 * K * 4 + (K * 4 if gated is not None else 0)
    tm = _pick_tm(M, row_bytes, 2 * K * N * 2 + 512 * K * 4)

    def body(*refs):
        it = iter(refs)
        g_ref, w_ref = next(it), next(it)
        z_ref = next(it) if z is not None else None
        norm_in = [next(it) for _ in range(4)] if norm is not None else None
        gate_in = [next(it) for _ in range(2)] if gated is not None else None
        if after is not None:
            next(it)
        out_ref = next(it)
        if col:
            acc = None
            for d in range(nsh):
                part = lax.dot_general(g_ref[:, d * ns:(d + 1) * ns].astype(BF16), w_ref[d], NT_DIMS,
                                       preferred_element_type=F32)
                acc = part if acc is None else acc + part
            if norm is None:
                out_ref[...] = acc.astype(out_dtype)
            else:
                first = pl.program_id(0) == 0
                dx = _norm_bwd_part(first, acc, *norm_in, out_ref, next(it), next(it), next(it))
                if gated is not None:
                    _gate_part(first, dx, *gate_in, next(it), next(it), next(it))
        else:
            gb = g_ref[...].astype(BF16)
            for cki in range(K // kc):
                cols = slice(cki * kc, (cki + 1) * kc)
                part = lax.dot_general(gb, w_ref[cols, :], NT_DIMS, preferred_element_type=F32)
                if z_ref is not None:
                    part = part * (2.0 * jnp.maximum(z_ref[:, cols].astype(F32), 0.0))
                out_ref[:, cols] = part.astype(out_dtype)

    row = pl.BlockSpec((tm, K), lambda i: (i, 0))
    vec = pl.BlockSpec((1, K), lambda i: (0, 0))
    vsh = _sds((1, K), F32)
    in_specs, args = [pl.BlockSpec((tm, N), lambda i: (i, 0)), w_spec], [g, w]
    out_specs, out_shape = [row], [_sds((M, K), out_dtype)]
    if z is not None:
        in_specs.append(row)
        args.append(z)
    if norm is not None:
        x, gain, sc, dres = norm
        in_specs += [row, vec, vec, row]
        args += [x, gain, sc, dres]
        out_specs += [vec, vec, vec]
        out_shape += [vsh, vsh, vsh]
    if gated is not None:
        in_specs += [row, vec]
        args += list(gated)
        out_specs += [row, vec, vec]
        out_shape += [_sds((M, K), BF16), vsh, vsh]
    if after is not None:
        in_specs.append(pl.BlockSpec(memory_space=pl.ANY))
        args.append(after)
    outs = _call(body, name=name, grid=(M // tm,), in_specs=in_specs, out_specs=tuple(out_specs), out_shape=tuple(out_shape),
                 sem=("parallel",) if norm is None else ("arbitrary",))(*args)
    return outs[0] if len(outs) == 1 else outs


def _mm_tn(a, g, *, name, col_shards=None, relu2=False, ln=None):
    M, K = a.shape
    N = g.shape[1]
    acc_budget = 8 * 1024 * 1024
    if col_shards:
        ns = N // col_shards
        spc = col_shards
        while spc > 1 and K * ns * spc * 4 > acc_budget:
            spc //= 2
        step_cols = K * a.dtype.itemsize + spc * ns * g.dtype.itemsize
    else:
        tk = K
        while tk > 128 and tk * N * 4 > acc_budget:
            tk //= 2
        step_cols = tk * a.dtype.itemsize + N * g.dtype.itemsize
    tm = _row_tile(M, 2048)
    while tm > 256 and 2 * tm * step_cols + 2 * acc_budget > VMEM_BLOCK_BUDGET:
        tm //= 2
    nm = M // tm
    if col_shards:
        grid = (col_shards // spc, nm)
        a_spec = pl.BlockSpec((tm, K), lambda c, m: (m, 0))
        g_spec = pl.BlockSpec((tm, spc * ns), lambda c, m: (m, c))
        out_spec = pl.BlockSpec((spc, K, ns), lambda c, m: (c, 0, 0))
        out_shape = _sds((col_shards, K, ns), BF16)
        acc_shape = (K, spc * ns)
    else:
        grid = (K // tk, nm)
        a_spec = pl.BlockSpec((tm, tk), lambda c, m: (m, c))
        g_spec = pl.BlockSpec((tm, N), lambda c, m: (m, 0))
        out_spec = pl.BlockSpec((tk, N), lambda c, m: (c, 0))
        out_shape = _sds((K, N), BF16)
        acc_shape = (tk, N)
        assert ln is None or tk == K
    in_specs, args = [a_spec, g_spec], [a, g]
    if ln is not None:
        in_specs += [pl.BlockSpec((1, K), lambda c, m: (0, 0))] * 2
        args += list(ln)

    def body(a_ref, g_ref, *rest):
        out_ref, acc_ref = rest[-2:]
        m = pl.program_id(1)

        @pl.when(m == 0)
        def _():
            acc_ref[...] = jnp.zeros_like(acc_ref)

        av = a_ref[...]
        if relu2:
            av = jnp.square(jnp.maximum(av.astype(F32), 0.0))
        if ln is not None:
            av, _ = _ln_silu(av, rest[0][...], rest[1][...])
        acc_ref[...] += lax.dot_general(av.astype(BF16), g_ref[...].astype(BF16), TN_DIMS, preferred_element_type=F32)

        @pl.when(m == nm - 1)
        def _():
            if col_shards:
                for s in range(spc):
                    out_ref[s] = acc_ref[:, s * ns:(s + 1) * ns].astype(BF16)
            else:
                out_ref[...] = acc_ref[...].astype(BF16)

    return _call(body, name=name, grid=grid, in_specs=in_specs, out_specs=out_spec, out_shape=out_shape,
                 scratch=[pltpu.VMEM(acc_shape, F32)], sem=("parallel", "arbitrary"))(*args)


CONV_TILE = 256


def _glu_rows(u2, ch):
    d = u2.shape[1] // 2
    return (u2[:, :d] * _sigmoid(u2[:, d:])).reshape(u2.shape[0], ch, LANES)


def _fill_glu(buf, u_ref, uh_ref, ch, tile):
    first = pl.program_id(0) == 0
    buf[0:HALO] = jnp.where(first, 0.0, _glu_rows(uh_ref[...], ch))
    buf[HALO:HALO + tile] = _glu_rows(u_ref[...], ch)


CONV_SUB = 4


def _conv_specs(S, D, tile):
    per = tile // HALO
    u_spec = pl.BlockSpec((tile, 2 * D), lambda i: (i, 0))
    uh_spec = pl.BlockSpec((HALO, 2 * D), lambda i: (jnp.maximum(i * per - 1, 0), 0))
    x_spec = pl.BlockSpec((tile, D), lambda i: (i, 0))
    xn_spec = pl.BlockSpec((HALO, D), lambda i: (jnp.minimum((i + 1) * per, S // HALO - 1), 0))
    w_spec = pl.BlockSpec((CONV_WIDTH, D // LANES, LANES), lambda i: (0, 0, 0))
    v_spec = pl.BlockSpec((1, D // LANES, LANES), lambda i: (0, 0, 0))
    return u_spec, uh_spec, x_spec, xn_spec, w_spec, v_spec


def _conv_mid_fwd(u, w3, bdw3, *, name):
    S, D = u.shape[0], u.shape[1] // 2
    ch = D // LANES
    tile = _row_tile(S, CONV_TILE)
    sub = _row_tile(tile, 2 * CONV_SUB)
    half = (CONV_WIDTH + 1) // 2
    u_spec, uh_spec, x_spec, _, w_spec, v_spec = _conv_specs(S, D, tile)

    def body(u_ref, uh_ref, w_ref, b_ref, o_ref, buf, stage):
        _fill_glu(buf, u_ref, uh_ref, ch, tile)

        def taps(lo, hi, start):
            def step(q, carry):
                rows = pl.ds(q * sub, sub)
                acc = [b_ref[...] if start else stage[rows], None]
                for k in range(lo, hi):
                    term = buf[pl.ds(q * sub + (HALO - CONV_WIDTH + 1 + k), sub)] * w_ref[k]
                    acc[k % 2] = term if acc[k % 2] is None else acc[k % 2] + term
                stage[rows] = acc[0] + acc[1]
                return carry

            lax.fori_loop(0, tile // sub, step, 0)

        taps(0, half, True)
        taps(half, CONV_WIDTH, False)
        o_ref[...] = stage[...].reshape(tile, D)

    return _call(body, name=name, grid=(S // tile,), in_specs=[u_spec, uh_spec, w_spec, v_spec], out_specs=x_spec,
                 out_shape=_sds((S, D), F32),
                 scratch=[pltpu.VMEM((tile + HALO, ch, LANES), F32), pltpu.VMEM((tile, ch, LANES), F32)],
                 sem=("parallel",))(u, u, w3, bdw3)


def _ln_silu(v, gv, bv):
    mu = jnp.mean(v, axis=-1, keepdims=True)
    cen = v - mu
    rstd = lax.rsqrt(jnp.mean(cen * cen, axis=-1, keepdims=True) + EPS)
    nrm = cen * rstd
    ln = nrm * gv + bv
    sg = _sigmoid(ln)
    return ln * sg, (nrm, rstd, ln, sg)


def _ln_silu_bwd(dwo, ds, lng, lnb, *, name):
    S, D = dwo.shape
    tm = _row_tile(S, 512)

    def body(v_ref, ds_ref, g_ref, b_ref, ddw_ref, dg_ref, db_ref, dbdw_ref):
        @pl.when(pl.program_id(0) == 0)
        def _():
            dg_ref[...] = jnp.zeros_like(dg_ref)
            db_ref[...] = jnp.zeros_like(db_ref)
            dbdw_ref[...] = jnp.zeros_like(dbdw_ref)

        gv = g_ref[...]
        _, (nrm, rstd, ln, sg) = _ln_silu(v_ref[...], gv, b_ref[...])
        dln = ds_ref[...] * (sg * (1.0 + ln * (1.0 - sg)))
        dg_ref[...] += jnp.sum(dln * nrm, axis=0, keepdims=True)
        db_ref[...] += jnp.sum(dln, axis=0, keepdims=True)
        dn = dln * gv
        ddw = rstd * (dn - jnp.mean(dn, axis=-1, keepdims=True) - nrm * jnp.mean(dn * nrm, axis=-1, keepdims=True))
        dbdw_ref[...] += jnp.sum(ddw, axis=0, keepdims=True)
        ddw_ref[...] = ddw

    row = pl.BlockSpec((tm, D), lambda i: (i, 0))
    vec = pl.BlockSpec((1, D), lambda i: (0, 0))
    vsh = _sds((1, D), F32)
    return _call(body, name=name, grid=(S // tm,), in_specs=[row, row, vec, vec], out_specs=(row, vec, vec, vec),
                 out_shape=(_sds((S, D), F32), vsh, vsh, vsh), sem=("arbitrary",))(dwo, ds, lng, lnb)


def _conv_mid_bwd_dw(u, ddw, w3, *, name):
    S, D = ddw.shape
    ch = D // LANES
    tile = _row_tile(S, CONV_TILE)
    sub = _row_tile(tile, 2 * CONV_SUB)
    last = S // tile - 1
    u_spec, uh_spec, x_spec, xn_spec, w_spec, _ = _conv_specs(S, D, tile)
    b_spec = pl.BlockSpec((1, 2 * D), lambda i: (0, 0))

    def body(u_ref, uh_ref, d_ref, dn_ref, w_ref, du_ref, dw_ref, db_ref, gbuf, dbuf, stage):
        @pl.when(pl.program_id(0) == 0)
        def _():
            dw_ref[...] = jnp.zeros_like(dw_ref)
            db_ref[...] = jnp.zeros_like(db_ref)

        _fill_glu(gbuf, u_ref, uh_ref, ch, tile)
        dbuf[0:tile] = d_ref[...].reshape(tile, ch, LANES)
        dbuf[tile:tile + HALO] = jnp.where(pl.program_id(0) == last, 0.0, dn_ref[...].reshape(HALO, ch, LANES))

        def taps(lo, hi, start):
            def step(q, c):
                s0 = q * sub
                ddw_q = dbuf[pl.ds(s0, sub)]
                acc = [None if start else stage[pl.ds(s0, sub)], None]
                for k in range(lo, hi):
                    term = dbuf[pl.ds(s0 + (CONV_WIDTH - 1 - k), sub)] * w_ref[k]
                    acc[k % 2] = term if acc[k % 2] is None else acc[k % 2] + term
                    dw_ref[k] += jnp.sum(ddw_q * gbuf[pl.ds(s0 + (HALO - CONV_WIDTH + 1 + k), sub)], axis=0)
                stage[pl.ds(s0, sub)] = acc[0] + acc[1]
                return c

            lax.fori_loop(0, tile // sub, step, 0)

        half = (CONV_WIDTH + 1) // 2
        taps(0, half, True)
        taps(half, CONV_WIDTH, False)
        dglu = stage[...].reshape(tile, D)
        uv = u_ref[...]
        av, sg = uv[:, :D], _sigmoid(uv[:, D:])
        da = dglu * sg
        dg = da * av * (1.0 - sg)
        du_ref[:, 0:D] = da
        du_ref[:, D:2 * D] = dg
        db_ref[:, 0:D] += jnp.sum(da, axis=0, keepdims=True)
        db_ref[:, D:2 * D] += jnp.sum(dg, axis=0, keepdims=True)

    return _call(body, name=name, grid=(S // tile,), in_specs=[u_spec, uh_spec, x_spec, xn_spec, w_spec],
                 out_specs=(u_spec, w_spec, b_spec),
                 out_shape=(_sds((S, 2 * D), F32), _sds((CONV_WIDTH, ch, LANES), F32), _sds((1, 2 * D), F32)),
                 scratch=[pltpu.VMEM((tile + HALO, ch, LANES), F32), pltpu.VMEM((tile + HALO, ch, LANES), F32),
                          pltpu.VMEM((tile, ch, LANES), F32)],
                 sem=("arbitrary",))(u, u, ddw, ddw, w3)


def _ret_tables(S, dk):
    f32 = np.float32
    B = min(RET_BLOCK, S)
    lg = np.log(f32(1.0) - f32(2.0) ** (f32(-5.0) - np.arange(RET_HEADS, dtype=f32)))
    idx = np.arange(B, dtype=f32)
    diff = idx[:, None] - idx[None, :]
    cq, ck = (np.arange(B) // CHUNK)[:, None], (np.arange(B) // CHUNK)[None, :]
    dist = np.where(cq == ck, np.abs(diff), diff)
    mask = np.where(ck <= cq, np.exp(lg[:, None, None] * dist[None]), f32(0.0)).astype(f32)
    xi = np.exp(lg[:, None] * (idx + f32(1.0)))[..., None].astype(f32)
    zeta = np.exp(lg[:, None] * (f32(B - 1.0) - idx))[..., None].astype(f32)
    gam = np.broadcast_to(np.exp(lg * f32(B))[:, None, None], (RET_HEADS, 8, LANES)).astype(f32)
    pos = np.arange(S, dtype=f32)
    inv = (f32(ROPE_BASE) ** (-np.arange(0, dk, 2, dtype=f32) / f32(dk))).astype(f32)
    ang = (pos[:, None] * inv[None, :]).astype(f32)
    tb = dict(mask=mask, xi=xi, zeta=zeta, gam=gam, cos=np.cos(ang).astype(f32), sin=np.sin(ang).astype(f32))
    return dict(B=B, **{k: jnp.asarray(v) for k, v in tb.items()})


def _rope(v, cs, sn):
    half = v.shape[1] // 2
    v1, v2 = v[:, :half], v[:, half:]
    return jnp.concatenate([v1 * cs - v2 * sn, v2 * cs + v1 * sn], axis=-1)


def _rope_t(d, cs, sn):
    half = d.shape[1] // 2
    d1, d2 = d[:, :half], d[:, half:]
    return jnp.concatenate([d1 * cs + d2 * sn, d2 * cs - d1 * sn], axis=-1)


def _dot(a, b):
    return jnp.dot(a.astype(BF16), b.astype(BF16), preferred_element_type=F32)


def _dot_nt(a, b):
    return lax.dot_general(a.astype(BF16), b.astype(BF16), NT_DIMS, preferred_element_type=F32)


def _dot_tn(a, b):
    return lax.dot_general(a.astype(BF16), b.astype(BF16), TN_DIMS, preferred_element_type=F32)


def _ret_specs(S, D, B, RB, reverse):
    dk, dv = D // RET_HEADS, 2 * D // RET_HEADS
    nb = S // RB
    blk = (lambda ib: nb - 1 - ib) if reverse else (lambda ib: ib)
    q = pl.BlockSpec((RB, dk), lambda h, ib: (blk(ib), h))
    k = pl.BlockSpec((RB, dk), lambda h, ib: (blk(ib), RET_HEADS + h))
    v = pl.BlockSpec((RB, dv), lambda h, ib: (blk(ib), RET_HEADS + h))
    gate = pl.BlockSpec((RB, dv), lambda h, ib: (blk(ib), 2 * RET_HEADS + h))
    yv = pl.BlockSpec((RB, dv), lambda h, ib: (blk(ib), h))
    rope = pl.BlockSpec((RB, dk // 2), lambda h, ib: (blk(ib), 0))
    mask = pl.BlockSpec((None, B, B), lambda h, ib: (h, 0, 0))
    dec = pl.BlockSpec((None, B, 1), lambda h, ib: (h, 0, 0))
    gam = pl.BlockSpec((None, 8, LANES), lambda h, ib: (h, 0, 0))
    gn = pl.BlockSpec((1, dv), lambda h, ib: (0, h))
    return dict(q=q, k=k, v=v, gate=gate, yv=yv, rope=rope, mask=mask, dec=dec, gam=gam, gn=gn)


def _group_norm(yr, gv, bv):
    mu = jnp.mean(yr, axis=-1, keepdims=True)
    cen = yr - mu
    rstd = lax.rsqrt(jnp.mean(cen * cen, axis=-1, keepdims=True) + EPS)
    nrm = cen * rstd
    return nrm, rstd, nrm * gv + bv


def _ret_fwd(proj, tb, gng, gnb, *, name):
    S, D = proj.shape[0], proj.shape[1] // 6
    dk, dv = D // RET_HEADS, 2 * D // RET_HEADS
    B = tb["B"]
    RB = _row_tile(S, 8 * B)
    nsub = RB // B
    sp = _ret_specs(S, D, B, RB, False)
    scale = dk ** -0.5

    def body(q_ref, k_ref, v_ref, gt_ref, cos_ref, sin_ref, mask_ref, xi_ref, zeta_ref, gam_ref, gng_ref, gnb_ref,
             yr_ref, yg_ref, qr_ref, kr_ref, state):
        @pl.when(pl.program_id(1) == 0)
        def _():
            state[...] = jnp.zeros_like(state)

        for sb in range(nsub):
            rows = slice(sb * B, (sb + 1) * B)
            cs, sn = cos_ref[rows, :], sin_ref[rows, :]
            q = _rope(q_ref[rows, :].astype(F32), cs, sn)
            k = _rope(k_ref[rows, :].astype(F32), cs, sn) * scale
            qr_ref[rows, :] = q.astype(BF16)
            kr_ref[rows, :] = k.astype(BF16)
            vb = v_ref[rows, :]
            p = _dot_nt(q, k) * mask_ref[...]
            st = state[...]
            yr = _dot(p, vb) + _dot(q * xi_ref[...], st)
            state[...] = st * gam_ref[0:1, 0:1] + _dot_tn(k * zeta_ref[...], vb)
            _, _, gn = _group_norm(yr, gng_ref[...], gnb_ref[...])
            gt = gt_ref[rows, :].astype(F32)
            yr_ref[rows, :] = yr.astype(BF16)
            yg_ref[rows, :] = (gt * _sigmoid(gt) * gn).astype(BF16)

    return _call(body, name=name, grid=(RET_HEADS, S // RB),
                 in_specs=[sp["q"], sp["k"], sp["v"], sp["gate"], sp["rope"], sp["rope"], sp["mask"], sp["dec"], sp["dec"],
                           sp["gam"], sp["gn"], sp["gn"]],
                 out_specs=(sp["yv"], sp["yv"], sp["q"], sp["q"]),
                 out_shape=(_sds((S, 2 * D), BF16), _sds((S, 2 * D), BF16), _sds((S, D), BF16), _sds((S, D), BF16)),
                 scratch=[pltpu.VMEM((dk, dv), F32)], sem=("parallel", "arbitrary"))(
                     proj, proj, proj, proj, tb["cos"], tb["sin"], tb["mask"], tb["xi"], tb["zeta"], tb["gam"], gng, gnb)


def _ret_bwd_q(proj, kr, yr, dyg, tb, gng, gnb, *, name):
    S, D = proj.shape[0], proj.shape[1] // 6
    dk, dv = D // RET_HEADS, 2 * D // RET_HEADS
    B = tb["B"]
    RB = _row_tile(S, 8 * B)
    nsub = RB // B
    sp = _ret_specs(S, D, B, RB, False)

    def body(k_ref, v_ref, gt_ref, yr_ref, dyg_ref, cos_ref, sin_ref, mask_ref, xi_ref, zeta_ref, gam_ref,
             gng_ref, gnb_ref, dq_ref, dgt_ref, dyr_ref, dgg_ref, dgb_ref, state):
        @pl.when(pl.program_id(1) == 0)
        def _():
            state[...] = jnp.zeros_like(state)
            dgg_ref[...] = jnp.zeros_like(dgg_ref)
            dgb_ref[...] = jnp.zeros_like(dgb_ref)

        for sb in range(nsub):
            rows = slice(sb * B, (sb + 1) * B)
            cs, sn = cos_ref[rows, :], sin_ref[rows, :]
            k = k_ref[rows, :]
            vb = v_ref[rows, :]
            gv = gng_ref[...]
            nrm, rstd, gn = _group_norm(yr_ref[rows, :].astype(F32), gv, gnb_ref[...])
            gt = gt_ref[rows, :].astype(F32)
            sg = _sigmoid(gt)
            dyg = dyg_ref[rows, :].astype(F32)
            dgt_ref[rows, :] = (dyg * gn * (sg * (1.0 + gt * (1.0 - sg)))).astype(BF16)
            dgn = dyg * (gt * sg)
            dgg_ref[...] += jnp.sum(dgn * nrm, axis=0, keepdims=True)
            dgb_ref[...] += jnp.sum(dgn, axis=0, keepdims=True)
            dn = dgn * gv
            dyr = rstd * (dn - jnp.mean(dn, axis=-1, keepdims=True) - nrm * jnp.mean(dn * nrm, axis=-1, keepdims=True))
            dyr_ref[rows, :] = dyr.astype(BF16)
            dp = _dot_nt(dyr, vb) * mask_ref[...]
            st = state[...]
            dq = _dot(dp, k) + _dot_nt(dyr, st) * xi_ref[...]
            dq_ref[rows, :] = _rope_t(dq, cs, sn).astype(BF16)
            state[...] = st * gam_ref[0:1, 0:1] + _dot_tn(k.astype(F32) * zeta_ref[...], vb)

    return _call(body, name=name, grid=(RET_HEADS, S // RB),
                 in_specs=[sp["q"], sp["v"], sp["gate"], sp["yv"], sp["yv"], sp["rope"], sp["rope"], sp["mask"],
                           sp["dec"], sp["dec"], sp["gam"], sp["gn"], sp["gn"]],
                 out_specs=(sp["q"], sp["yv"], sp["yv"], sp["gn"], sp["gn"]),
                 out_shape=(_sds((S, D), BF16), _sds((S, 2 * D), BF16), _sds((S, 2 * D), BF16), _sds((1, 2 * D), F32),
                            _sds((1, 2 * D), F32)),
                 scratch=[pltpu.VMEM((dk, dv), F32)], sem=("parallel", "arbitrary"))(
                     kr, proj, proj, yr, dyg, tb["cos"], tb["sin"], tb["mask"], tb["xi"], tb["zeta"], tb["gam"], gng, gnb)


def _ret_bwd_kv(proj, qr, kr, dyr, dq, dgt, tb, *, name):
    S, D = proj.shape[0], proj.shape[1] // 6
    dk, dv = D // RET_HEADS, 2 * D // RET_HEADS
    B = tb["B"]
    RB = _row_tile(S, 2 * B)
    nsub = RB // B
    nb = S // RB
    scale = dk ** -0.5

    def body(v_ref, qr_ref, kr_ref, dyr_ref, dq_ref, dgt_ref, cos_ref, sin_ref, mask_ref, xi_ref, zeta_ref, gam_ref, out_ref,
             dstate):
        @pl.when(pl.program_id(0) == 0)
        def _():
            dstate[...] = jnp.zeros_like(dstate)

        out_ref[:, 0:D] = dq_ref[...]
        out_ref[:, 4 * D:6 * D] = dgt_ref[...]
        for sb in reversed(range(nsub)):
            rows = slice(sb * B, (sb + 1) * B)
            cs, sn = cos_ref[rows, :], sin_ref[rows, :]
            for h in range(RET_HEADS):
                kcols = slice(D + h * dk, D + (h + 1) * dk)
                vcols = slice(2 * D + h * dv, 2 * D + (h + 1) * dv)
                q = qr_ref[rows, h * dk:(h + 1) * dk]
                k = kr_ref[rows, h * dk:(h + 1) * dk]
                vb = v_ref[rows, h * dv:(h + 1) * dv]
                dyr_h = dyr_ref[rows, h * dv:(h + 1) * dv]
                mk = mask_ref[h]
                p = _dot_nt(q, k) * mk
                dp = _dot_nt(dyr_h, vb) * mk
                ds = dstate[h]
                zt = zeta_ref[h]
                dkr = _dot_tn(dp, q) + _dot_nt(vb, ds) * zt
                out_ref[rows, kcols] = _rope_t(dkr * scale, cs, sn).astype(BF16)
                out_ref[rows, vcols] = (_dot_tn(p, dyr_h) + _dot(k.astype(F32) * zt, ds)).astype(BF16)
                dstate[h] = ds * gam_ref[h, 0:1, 0:1] + _dot_tn(q.astype(F32) * xi_ref[h], dyr_h)

    def rev(width):
        return pl.BlockSpec((RB, width), lambda ib: (nb - 1 - ib, 0))

    def whole(a):
        return pl.BlockSpec(a.shape, lambda ib: (0,) * a.ndim)

    return _call(body, name=name, grid=(nb,),
                 in_specs=[pl.BlockSpec((RB, 2 * D), lambda ib: (nb - 1 - ib, 1)), rev(D), rev(D), rev(2 * D), rev(D), rev(2 * D),
                           rev(dk // 2), rev(dk // 2), whole(tb["mask"]), whole(tb["xi"]), whole(tb["zeta"]), whole(tb["gam"])],
                 out_specs=rev(6 * D), out_shape=_sds((S, 6 * D), BF16), scratch=[pltpu.VMEM((RET_HEADS, dk, dv), F32)],
                 sem=("arbitrary",))(proj, qr, kr, dyr, dq, dgt, tb["cos"], tb["sin"], tb["mask"], tb["xi"], tb["zeta"],
                                     tb["gam"])


def _ada_fwd(c_all, ada_w, *, name):
    L, D, ns = ada_w.shape

    def body(c_ref, w_ref, out_ref):
        cv = c_ref[...]
        cond = cv * _sigmoid(cv)
        out_ref[...] = jnp.dot(cond.astype(BF16), w_ref[...].astype(BF16), preferred_element_type=F32)

    return _call(body, name=name, grid=(L,), in_specs=[pl.BlockSpec((NDEV, D), lambda l: (0, 0)),
                                                      pl.BlockSpec((None, D, ns), lambda l: (l, 0, 0))],
                 out_specs=pl.BlockSpec((None, NDEV, ns), lambda l: (l, 0, 0)), out_shape=_sds((L, NDEV, ns), F32),
                 sem=("parallel",))(c_all, ada_w)


def _ada_bwd(c_all, dmod_cols, *, name):
    L, _, ns = dmod_cols.shape
    D = c_all.shape[1]

    def body(c_ref, d_ref, out_ref):
        cv = c_ref[...]
        cond = cv * _sigmoid(cv)
        out_ref[...] = lax.dot_general(cond.astype(BF16), d_ref[...].astype(BF16), TN_DIMS, preferred_element_type=F32)

    return _call(body, name=name, grid=(L,), in_specs=[pl.BlockSpec((NDEV, D), lambda l: (0, 0)),
                                                      pl.BlockSpec((None, NDEV, ns), lambda l: (l, 0, 0))],
                 out_specs=pl.BlockSpec((None, D, ns), lambda l: (l, 0, 0)), out_shape=_sds((L, D, ns), F32),
                 sem=("parallel",))(c_all, dmod_cols)


def _adamw(w, m, v, parts, *, name):
    shape = w.shape
    L, cols = len(parts), shape[-1]
    rows = w.size // (cols * L)
    n = parts[0].shape[0]
    tr = rows
    for cand in (256, 128, 64, 32, 16, 8):
        if rows % cand == 0:
            tr = cand
            break
    c1 = 1.0 - ADAM_B1 ** ADAM_STEP
    c2 = 1.0 - ADAM_B2 ** ADAM_STEP

    def body(w_ref, m_ref, v_ref, *rest):
        p_refs = rest[:L]
        g_ref, d_ref, m2_ref, v2_ref = rest[L:]
        layer = pl.program_id(0)
        for l in range(L):
            @pl.when(layer == l)
            def _(p_ref=p_refs[l]):
                g = p_ref[0].astype(F32)
                for i in range(1, n):
                    g = g + p_ref[i].astype(F32)
                m2 = ADAM_B1 * m_ref[...] + (1.0 - ADAM_B1) * g
                v2 = ADAM_B2 * v_ref[...] + (1.0 - ADAM_B2) * (g * g)
                g_ref[...] = g
                m2_ref[...] = m2
                v2_ref[...] = v2
                d_ref[...] = -ADAM_LR * ((m2 / c1) / (jnp.sqrt(v2 / c2) + ADAM_EPS) + ADAM_WD * w_ref[...])

    mat = pl.BlockSpec((None, tr, cols), lambda l, i: (l, i, 0))

    def part_spec(k):
        return pl.BlockSpec((n, tr, cols), lambda l, i: (0, jnp.where(l == k, i, 0), 0))

    outs = _call(body, name=name, grid=(L, rows // tr), in_specs=[mat, mat, mat] + [part_spec(k) for k in range(L)],
                 out_specs=(mat, mat, mat, mat), out_shape=tuple(_sds((L, rows, cols), F32) for _ in range(4)),
                 sem=("parallel", "parallel"))(w.reshape(L, rows, cols), m.reshape(L, rows, cols), v.reshape(L, rows, cols),
                                               *[p.reshape(n, rows, cols) for p in parts])
    return tuple(o.reshape(shape) for o in outs)


SMALL = ("ada_b", "norm_mix_g", "norm_mlp_g", "conv_b_pw1", "conv_b_dw", "conv_ln_g", "conv_ln_b", "conv_b_pw2",
         "final_norm_g")
WEIGHTS = ("ada_w", "ada_b", "norm_mix_g", "norm_mlp_g", "conv_w_pw1", "conv_b_pw1", "conv_w_dw", "conv_b_dw", "conv_ln_g",
           "conv_ln_b", "conv_w_pw2", "conv_b_pw2", "ret_w_in", "ret_gn_g", "ret_gn_b", "ret_w_out", "mlp_w1", "mlp_w2",
           "final_norm_g")


def kernel(x, c, ada_w, ada_b, norm_mix_g, norm_mlp_g, conv_w_pw1, conv_b_pw1, conv_w_dw, conv_b_dw, conv_ln_g, conv_ln_b, conv_w_pw2, conv_b_pw2, ret_w_in, ret_gn_g, ret_gn_b, ret_w_out, mlp_w1, mlp_w2, final_norm_g, loss_target, m_ada_w, m_ada_b, m_norm_mix_g, m_norm_mlp_g, m_conv_w_pw1, m_conv_b_pw1, m_conv_w_dw, m_conv_b_dw, m_conv_ln_g, m_conv_ln_b, m_conv_w_pw2, m_conv_b_pw2, m_ret_w_in, m_ret_gn_g, m_ret_gn_b, m_ret_w_out, m_mlp_w1, m_mlp_w2, m_final_norm_g, v_ada_w, v_ada_b, v_norm_mix_g, v_norm_mlp_g, v_conv_w_pw1, v_conv_b_pw1, v_conv_w_dw, v_conv_b_dw, v_conv_ln_g, v_conv_ln_b, v_conv_w_pw2, v_conv_b_pw2, v_ret_w_in, v_ret_gn_g, v_ret_gn_b, v_ret_w_out, v_mlp_w1, v_mlp_w2, v_final_norm_g):
    W = dict(ada_w=ada_w, ada_b=ada_b, norm_mix_g=norm_mix_g, norm_mlp_g=norm_mlp_g, conv_w_pw1=conv_w_pw1,
             conv_b_pw1=conv_b_pw1, conv_w_dw=conv_w_dw, conv_b_dw=conv_b_dw, conv_ln_g=conv_ln_g, conv_ln_b=conv_ln_b,
             conv_w_pw2=conv_w_pw2, conv_b_pw2=conv_b_pw2, ret_w_in=ret_w_in, ret_gn_g=ret_gn_g, ret_gn_b=ret_gn_b,
             ret_w_out=ret_w_out, mlp_w1=mlp_w1, mlp_w2=mlp_w2, final_norm_g=final_norm_g)
    Mo = dict(ada_w=m_ada_w, ada_b=m_ada_b, norm_mix_g=m_norm_mix_g, norm_mlp_g=m_norm_mlp_g, conv_w_pw1=m_conv_w_pw1,
              conv_b_pw1=m_conv_b_pw1, conv_w_dw=m_conv_w_dw, conv_b_dw=m_conv_b_dw, conv_ln_g=m_conv_ln_g,
              conv_ln_b=m_conv_ln_b, conv_w_pw2=m_conv_w_pw2, conv_b_pw2=m_conv_b_pw2, ret_w_in=m_ret_w_in,
              ret_gn_g=m_ret_gn_g, ret_gn_b=m_ret_gn_b, ret_w_out=m_ret_w_out, mlp_w1=m_mlp_w1, mlp_w2=m_mlp_w2,
              final_norm_g=m_final_norm_g)
    Vo = dict(ada_w=v_ada_w, ada_b=v_ada_b, norm_mix_g=v_norm_mix_g, norm_mlp_g=v_norm_mlp_g, conv_w_pw1=v_conv_w_pw1,
              conv_b_pw1=v_conv_b_pw1, conv_w_dw=v_conv_w_dw, conv_b_dw=v_conv_b_dw, conv_ln_g=v_conv_ln_g,
              conv_ln_b=v_conv_ln_b, conv_w_pw2=v_conv_w_pw2, conv_b_pw2=v_conv_b_pw2, ret_w_in=v_ret_w_in,
              ret_gn_g=v_ret_gn_g, ret_gn_b=v_ret_gn_b, ret_w_out=v_ret_w_out, mlp_w1=v_mlp_w1, mlp_w2=v_mlp_w2,
              final_norm_g=v_final_norm_g)

    S, D = x.shape[1], x.shape[2]
    CH = D // LANES
    n_conv, n_ret = conv_w_pw1.shape[0], ret_w_in.shape[0]
    me = 4 * lax.axis_index("x") + 2 * lax.axis_index("y") + lax.axis_index("c")
    xs = x.reshape(S, D)
    target = loss_target.reshape(S, D)

    def mixer_shards(i):
        j = i // 2
        if i % 2 == 0:
            return [[conv_w_pw1[j].astype(BF16)], [conv_w_pw2[j].astype(BF16)]]
        return [[ret_w_in[j].astype(BF16)], [ret_w_out[j].astype(BF16)]]

    def mlp_shards(i):
        return [[mlp_w1[i].astype(BF16)], [mlp_w2[i].astype(BF16)]]

    def mlp_weights(got):
        return got[0], got[1].reshape(4 * D, D)

    first_handle, _ = _exchange_start(mixer_shards(0)[:1], gather=True, name="gather_start_first")
    small = _exchange([[conv_w_dw], [ret_gn_g], [ret_gn_b], [c]], gather=True, name="gather_small")
    dw_g, gng_g, gnb_g, c_g = small
    dw3 = jnp.transpose(dw_g, (1, 2, 0, 3)).reshape(n_conv, CONV_WIDTH, CH, LANES)
    gng_full = jnp.transpose(gng_g, (1, 2, 0, 3)).reshape(n_ret, 1, 2 * D)
    gnb_full = jnp.transpose(gnb_g, (1, 2, 0, 3)).reshape(n_ret, 1, 2 * D)
    c_all = c_g.reshape(NDEV, D)

    mod_cols = _ada_fwd(c_all, ada_w, name="ada_fwd")
    mod_all = _exchange([[mod_cols]], gather=True, name="gather_mod")[0]
    mod = lax.dynamic_index_in_dim(mod_all, me, axis=2, keepdims=False)
    mod = jnp.transpose(mod, (1, 0, 2)).reshape(DEPTH, 6 * D) + ada_b
    mods = [[mod[i, j * D:(j + 1) * D].reshape(1, D) for j in range(6)] for i in range(DEPTH)]
    tb = _ret_tables(S, D // RET_HEADS)

    def vec(a):
        return a.reshape(1, -1)

    def group_a(i):
        return mixer_shards(i) if i % 2 == 0 else mixer_shards(i)[:1]

    def group_b(i):
        return mlp_shards(i) if i % 2 == 0 else mixer_shards(i)[1:] + mlp_shards(i)

    mix_first = _exchange_wait(first_handle, name="gather_wait_first", after=mod)[0]
    pw2_handle, token = _exchange_start(mixer_shards(0)[1:], gather=True, name="gather_start_pw2_0", after=mix_first)
    handle_b, token = _exchange_start(mlp_shards(0), gather=True, name="gather_start_b0", after=token)
    saved = []
    weights = []
    xcur = xs
    for i in range(DEPTH):
        sh1, sc1, g1, sh2, sc2, g2 = mods[i]
        j = i // 2
        if i > 0:
            got = _exchange_wait(handle_a, name=f"gather_wait_a{i}", after=xcur)
            mix_first = got[0]
            mix_second = got[1].reshape(-1, D) if i % 2 == 0 else None
            handle_b, token = _exchange_start(group_b(i), gather=True, name=f"gather_start_b{i}", after=got[0])
        st = dict(x_in=xcur)
        norm1 = (vec(norm_mix_g[i]), sc1, sh1)
        if i % 2 == 0:
            u, h = _mm_nn(xcur, mix_first, norm=norm1, bias=vec(conv_b_pw1[j]), out_dtype=F32, name=f"pw1_fwd{i}", after=token)
            dwo = _conv_mid_fwd(u, dw3[j], conv_b_dw[j].reshape(1, CH, LANES), name=f"conv_mid_fwd{i}")
            if i == 0:
                mix_second = _exchange_wait(pw2_handle, name="gather_wait_pw2_0", after=dwo)[0].reshape(-1, D)
            xcur, y_raw = _mm_nn(dwo, mix_second, ln=(vec(conv_ln_g[j]), vec(conv_ln_b[j])), bias=vec(conv_b_pw2[j]), res=xcur,
                                 gate=g1, name=f"pw2_fwd{i}")
            st.update(u=u, dwo=dwo, y_raw=y_raw)
            got = _exchange_wait(handle_b, name=f"gather_wait_b{i}", after=xcur)
            mlp_w = mlp_weights(got)
        else:
            proj, h = _mm_nn(xcur, mix_first, norm=norm1, name=f"ret_in_fwd{i}", after=token)
            yr, yg, qr, kr = _ret_fwd(proj, tb, gng_full[j], gnb_full[j], name=f"ret_fwd{i}")
            got = _exchange_wait(handle_b, name=f"gather_wait_b{i}", after=yg)
            mix_second, mlp_w = got[0].reshape(-1, D), mlp_weights(got[1:3])
            xcur, y_raw = _mm_nn(yg, mix_second, res=xcur, gate=g1, name=f"ret_out_fwd{i}")
            st.update(proj=proj, yr=yr, yg=yg, qr=qr, kr=kr, y_raw=y_raw)
        st.update(h=h, x_mid=xcur)
        if i + 1 < DEPTH:
            handle_a, token = _exchange_start(group_a(i + 1), gather=True, name=f"gather_start_a{i + 1}", after=got[0])
        z, h2 = _mm_nn(xcur, mlp_w[0], norm=(vec(norm_mlp_g[i]), sc2, sh2), name=f"mlp1_fwd{i}", after=token)
        xcur, o_raw = _mm_nn(z, mlp_w[1], relu2=True, res=xcur, gate=g2, name=f"mlp2_fwd{i}")
        st.update(h2=h2, z=z, o_raw=o_raw)
        saved.append(st)
        weights.append((mix_first, mix_second) + mlp_w)

    g2_last = mods[DEPTH - 1][5]
    loss_local, dx, d_final_g, dy, dgate, _ = _final_loss(xcur, vec(final_norm_g), target, saved[-1]["o_raw"], g2_last,
                                                          name="final_loss")
    loss = lax.psum(loss_local[0, 0], AXES)

    dmod_rows = [None] * DEPTH
    d_mix_g, d_mlp_g = [None] * DEPTH, [None] * DEPTH
    d_pw1, d_pw2, d_win, d_wout = [None] * n_conv, [None] * n_conv, [None] * n_ret, [None] * n_ret
    d_w1, d_w2 = [None] * DEPTH, [None] * DEPTH
    d_bpw1, d_bdw, d_lng, d_lnb, d_bpw2, d_dw = ([None] * n_conv for _ in range(6))
    d_gng, d_gnb = [None] * n_ret, [None] * n_ret

    def gn_parts(d):
        return jnp.transpose(d.reshape(RET_HEADS, NDEV, -1), (1, 0, 2))

    grad_handles = [None] * DEPTH
    token = None
    for i in reversed(range(DEPTH)):
        sh1, sc1, g1, sh2, sc2, g2 = mods[i]
        j = i // 2
        st = saved[i]
        mix_a, mix_b, w1_i, w2_i = weights[i]
        do, dg2 = dy, dgate
        dz = _mm_nt(do, w2_i, z=st["z"], out_dtype=BF16, name=f"mlp2_bwd_x{i}", after=token)
        d_w2[i] = _mm_tn(st["z"], do, relu2=True, name=f"mlp2_bwd_w{i}")
        dx, dsc2, dsh2, d_mlp_g[i], dy, dg1, dby = _mm_nt(dz, w1_i, norm=(st["x_mid"], vec(norm_mlp_g[i]), sc2, dx),
                                                          gated=(st["y_raw"], g1), name=f"mlp1_bwd_x{i}")
        d_w1[i] = _mm_tn(st["h2"], dz, col_shards=NDEV, name=f"mlp1_bwd_w{i}")
        mlp_groups = [[d_w1[i]], [d_w2[i].reshape(NDEV, 4 * D // NDEV, D)]]
        token = None
        if i == 0:
            mlp0_handle, token = _exchange_start(mlp_groups, gather=False, name="grads_start_mlp0")
            mlp_groups = []
        if i % 2 == 0:
            d_bpw2[j] = dby
            ds = _mm_nt(dy, mix_b, name=f"pw2_bwd_x{i}", after=token)
            ln_gb = (vec(conv_ln_g[j]), vec(conv_ln_b[j]))
            d_pw2[j] = _mm_tn(st["dwo"], dy, ln=ln_gb, name=f"pw2_bwd_w{i}")
            ddw, d_lng[j], d_lnb[j], d_bdw[j] = _ln_silu_bwd(st["dwo"], ds, *ln_gb, name=f"conv_ln_bwd{i}")
            du, ddw_w, dbu = _conv_mid_bwd_dw(st["u"], ddw, dw3[j], name=f"conv_mid_bwd_dw{i}")
            d_dw[j], d_bpw1[j] = ddw_w.reshape(CONV_WIDTH, D), dbu.reshape(2, D)
            d_pw1[j] = _mm_tn(st["h"], du, col_shards=NDEV, name=f"pw1_bwd_w{i}")
            mix_groups = [[d_pw1[j]], [d_pw2[j].reshape(NDEV, D // NDEV, D)],
                          [jnp.transpose(d_dw[j].reshape(CONV_WIDTH, NDEV, D // NDEV), (1, 0, 2))]]
            mix_in, mix_name = du, f"pw1_bwd_x{i}"
        else:
            dyg = _mm_nt(dy, mix_b, out_dtype=BF16, name=f"ret_out_bwd_x{i}")
            d_wout[j] = _mm_tn(st["yg"], dy, name=f"ret_out_bwd_w{i}")
            dq, dgt, dyr, d_gng[j], d_gnb[j] = _ret_bwd_q(st["proj"], st["kr"], st["yr"], dyg, tb, gng_full[j], gnb_full[j],
                                                          name=f"ret_bwd_q{i}")
            dproj = _ret_bwd_kv(st["proj"], st["qr"], st["kr"], dyr, dq, dgt, tb, name=f"ret_bwd_kv{i}")
            d_win[j] = _mm_tn(st["h"], dproj, col_shards=NDEV, name=f"ret_in_bwd_w{i}")
            mix_in, mix_name = dproj, f"ret_in_bwd_x{i}"
            mix_groups = [[d_win[j]], [d_wout[j].reshape(NDEV, 2 * D // NDEV, D)], [gn_parts(d_gng[j])],
                          [gn_parts(d_gnb[j])]]
        grad_handles[i], token = _exchange_start(mix_groups + mlp_groups, gather=False, name=f"grads_start{i}")
        gated = (saved[i - 1]["o_raw"], mods[i - 1][5]) if i > 0 else None
        outs = _mm_nt(mix_in, mix_a, norm=(st["x_in"], vec(norm_mix_g[i]), sc1, dx), gated=gated, name=mix_name, after=token)
        dx, dsc1, dsh1, d_mix_g[i] = outs[:4]
        if i > 0:
            dy, dgate = outs[4], outs[5]
        dmod_rows[i] = jnp.concatenate([dsh1, dsc1, dg1, dsh2, dsc2, dg2], axis=0)
    grad_x = dx.reshape(1, S, D)

    small_local = jnp.concatenate(dmod_rows + d_mix_g + d_mlp_g + d_bpw1 + d_bdw + d_lng + d_lnb + d_bpw2 + [d_final_g],
                                  axis=0)
    small_all = _exchange([[small_local]], gather=True, name="gather_small_grads")[0]

    def pack(src):
        return jnp.concatenate([src[n].reshape(-1, D) for n in SMALL], axis=0)[None]

    sm = _adamw(pack(W), pack(Mo), pack(Vo), [small_all], name="adamw_small")
    results = {}
    row = 0
    for n in SMALL:
        cnt = W[n].size // D
        results[n] = tuple(o[0, row:row + cnt].reshape(W[n].shape) for o in sm)
        row += cnt

    ns_ada = ada_w.shape[2]
    dmod_all = small_all[:, :6 * DEPTH, :].reshape(NDEV, DEPTH, 6 * D)
    dmod_cols = jnp.transpose(lax.dynamic_slice_in_dim(dmod_all, me * ns_ada, ns_ada, axis=2), (1, 0, 2))
    g_ada = _ada_bwd(c_all, dmod_cols, name="ada_bwd")
    flat_ada = (1, DEPTH * D, ns_ada)
    ada_res = _adamw(ada_w.reshape(flat_ada), m_ada_w.reshape(flat_ada), v_ada_w.reshape(flat_ada),
                     [g_ada.reshape(flat_ada)], name="adamw_ada_w")
    results["ada_w"] = tuple(o.reshape(ada_w.shape) for o in ada_res)

    def update(names, parts):
        for n in names:
            results[n] = _adamw(W[n], Mo[n], Vo[n], parts[n], name=f"adamw_{n}")

    got = {i: _exchange_wait(grad_handles[i], name=f"grads_wait{i}", after=dx) for i in range(DEPTH - 1, 0, -1)}
    ret_layers = [i for i in range(DEPTH) if i % 2 == 1]
    update(("ret_w_in", "ret_w_out", "ret_gn_g", "ret_gn_b"),
           dict(ret_w_in=[got[i][0] for i in ret_layers], ret_w_out=[got[i][1] for i in ret_layers],
                ret_gn_g=[got[i][2] for i in ret_layers], ret_gn_b=[got[i][3] for i in ret_layers]))
    got_mlp0 = _exchange_wait(mlp0_handle, name="grads_wait_mlp0", after=results["ret_w_in"][0])
    update(("mlp_w1", "mlp_w2"),
           dict(mlp_w1=[got_mlp0[0]] + [got[i][-2] for i in range(1, DEPTH)],
                mlp_w2=[got_mlp0[1]] + [got[i][-1] for i in range(1, DEPTH)]))
    got[0] = _exchange_wait(grad_handles[0], name="grads_wait0", after=results["mlp_w1"][0])
    conv_layers = [i for i in range(DEPTH) if i % 2 == 0]
    update(("conv_w_pw1", "conv_w_pw2", "conv_w_dw"),
           dict(conv_w_pw1=[got[i][0] for i in conv_layers], conv_w_pw2=[got[i][1] for i in conv_layers],
                conv_w_dw=[got[i][2] for i in conv_layers]))

    outs = [loss, grad_x]
    for kind in range(4):
        outs += [results[n][kind] for n in WEIGHTS]
    return tuple(outs)
```

```python
import functools

import jax
import jax.numpy as jnp
import numpy as np
from jax import lax
from jax.experimental import pallas as pl
from jax.experimental.pallas import tpu as pltpu

F32, BF16 = jnp.float32, jnp.bfloat16
AXES = ("x", "y", "c")
NDEV = 8
DEPTH = 4
EPS = 1e-6
CHUNK = 64
CONV_WIDTH = 31
HALO = 32
RET_HEADS = 4
RET_BLOCK = 256
ROPE_BASE = 10000.0
LANES = 128
ADAM_LR, ADAM_B1, ADAM_B2, ADAM_EPS, ADAM_WD, ADAM_STEP = 0.001, 0.9, 0.999, 1e-08, 0.01, 10
VMEM_LIMIT = 56 * 1024 * 1024
VMEM_BLOCK_BUDGET = 44 * 1024 * 1024
MESH = pl.DeviceIdType.MESH
NT_DIMS = (((1,), (1,)), ((), ()))
TN_DIMS = (((0,), (0,)), ((), ()))


def _call(body, *, name, out_shape, in_specs, out_specs, grid=(), scratch=(), sem=None, aliases=None):
    params = dict(vmem_limit_bytes=VMEM_LIMIT)
    if sem is not None:
        params["dimension_semantics"] = sem
    return pl.pallas_call(body, name=name, grid=grid, in_specs=in_specs, out_specs=out_specs, out_shape=out_shape,
                          scratch_shapes=list(scratch), input_output_aliases=aliases or {},
                          compiler_params=pltpu.CompilerParams(**params))


def _row_tile(rows, want):
    t = min(rows, want)
    while rows % t:
        t //= 2
    return t


def _sds(shape, dtype):
    return jax.ShapeDtypeStruct(tuple(shape), dtype)


def _sigmoid(v):
    return 1.0 / (1.0 + jnp.exp(-v))


def _exchange(groups, *, gather, name):
    flat = [a for g in groups for a in g]
    n_in = len(flat)
    out_shapes = []
    for g in groups:
        s = g[0].shape if gather else g[0].shape[1:]
        lead = (NDEV,) if len(g) == 1 else (NDEV, len(g))
        out_shapes.append(_sds(lead + tuple(s), g[0].dtype))
    n_g = len(groups)

    def body(*refs):
        ins, outs = refs[:n_in], refs[n_in:n_in + n_g]
        send_sems, recv_sems, loc_sems = refs[n_in + n_g:]
        x, y, c = lax.axis_index("x"), lax.axis_index("y"), lax.axis_index("c")
        me = 4 * x + 2 * y + c
        locs, k = [], 0
        for gi, g in enumerate(groups):
            for li in range(len(g)):
                src = ins[k] if gather else ins[k].at[me]
                dst = outs[gi].at[me] if len(g) == 1 else outs[gi].at[me, li]
                cp = pltpu.make_async_copy(src, dst, loc_sems.at[k])
                cp.start()
                locs.append(cp)
                k += 1
        k0 = 0
        for gi, g in enumerate(groups):
            for r in range(1, NDEV):
                px = 1 - x if r & 4 else x
                py = 1 - y if r & 2 else y
                pc = 1 - c if r & 1 else c
                peer = 4 * px + 2 * py + pc
                for li in range(len(g)):
                    src = ins[k0 + li] if gather else ins[k0 + li].at[peer]
                    dst = outs[gi].at[me] if len(g) == 1 else outs[gi].at[me, li]
                    pltpu.make_async_remote_copy(src_ref=src, dst_ref=dst, send_sem=send_sems.at[gi * (NDEV - 1) + r - 1],
                                                 recv_sem=recv_sems.at[gi * (NDEV - 1) + r - 1], device_id=(px, py, pc),
                                                 device_id_type=MESH).start()
            k0 += len(g)
        for gi, g in enumerate(groups):
            for r in range(1, NDEV):
                px = 1 - x if r & 4 else x
                py = 1 - y if r & 2 else y
                pc = 1 - c if r & 1 else c
                peer = 4 * px + 2 * py + pc
                slab = pltpu.make_async_remote_copy(src_ref=outs[gi].at[me], dst_ref=outs[gi].at[peer],
                                                    send_sem=send_sems.at[gi * (NDEV - 1) + r - 1], recv_sem=recv_sems.at[gi * (NDEV - 1) + r - 1],
                                                    device_id=(px, py, pc), device_id_type=MESH)
                slab.wait_send()
                slab.wait_recv()
        for cp in locs:
            cp.wait()

    hbm = pl.BlockSpec(memory_space=pltpu.HBM)
    outs = _call(body, name=name, out_shape=tuple(out_shapes), in_specs=[hbm] * n_in, out_specs=tuple([hbm] * n_g),
                 scratch=[pltpu.SemaphoreType.DMA((n_g * (NDEV - 1),)), pltpu.SemaphoreType.DMA((n_g * (NDEV - 1),)),
                          pltpu.SemaphoreType.DMA((n_in,))])(*flat)
    return list(outs)


def _peer_of(x, y, c, r):
    return (1 - x if r & 4 else x, 1 - y if r & 2 else y, 1 - c if r & 1 else c)


def _exchange_start(groups, *, gather, name, after=None):
    flat = [pltpu.with_memory_space_constraint(a, pltpu.HBM) for g in groups for a in g]
    n_in, n_g = len(flat), len(groups)
    land_shapes = []
    for g in groups:
        s = g[0].shape if gather else g[0].shape[1:]
        lead = (NDEV,) if len(g) == 1 else (NDEV, len(g))
        land_shapes.append((lead + tuple(s), g[0].dtype))
    lands = [pltpu.with_memory_space_constraint(lax.empty(s, d), pltpu.HBM) for s, d in land_shapes]
    n_after = 0 if after is None else 1

    def body(*refs):
        ins, land = refs[:n_in], refs[n_in:n_in + n_g]
        send_sems, recv_sems, loc_sems = refs[n_in + n_g + n_after:n_in + n_g + n_after + 3]
        token = refs[-1]
        x, y, c = lax.axis_index("x"), lax.axis_index("y"), lax.axis_index("c")
        me = 4 * x + 2 * y + c
        k = 0
        for gi, g in enumerate(groups):
            for li in range(len(g)):
                dst = land[gi].at[me] if len(g) == 1 else land[gi].at[me, li]
                pltpu.make_async_copy(ins[k] if gather else ins[k].at[me], dst, loc_sems.at[k]).start()
                k += 1
        k0 = 0
        for gi, g in enumerate(groups):
            for r in range(1, NDEV):
                px, py, pc = _peer_of(x, y, c, r)
                peer = 4 * px + 2 * py + pc
                for li in range(len(g)):
                    dst = land[gi].at[me] if len(g) == 1 else land[gi].at[me, li]
                    pltpu.make_async_remote_copy(src_ref=ins[k0 + li] if gather else ins[k0 + li].at[peer], dst_ref=dst,
                                                 send_sem=send_sems.at[gi * (NDEV - 1) + r - 1], recv_sem=recv_sems.at[gi * (NDEV - 1) + r - 1],
                                                 device_id=(px, py, pc), device_id_type=MESH).start()
            k0 += len(g)
        token[...] = jnp.zeros_like(token)

    hbm = pl.BlockSpec(memory_space=pltpu.HBM)
    sem = pl.BlockSpec(memory_space=pltpu.SEMAPHORE)
    args = flat + lands + ([after] if n_after else [])
    outs = pl.pallas_call(body, name=name,
        out_shape=(pltpu.SemaphoreType.DMA((n_g * (NDEV - 1),)), pltpu.SemaphoreType.DMA((n_g * (NDEV - 1),)),
                   pltpu.SemaphoreType.DMA((n_in,)), *[pltpu.HBM(a.shape, a.dtype) for a in flat],
                   *[pltpu.HBM(s, d) for s, d in land_shapes], _sds((8, LANES), F32)),
        in_specs=[hbm] * (n_in + n_g) + [pl.BlockSpec(memory_space=pl.ANY)] * n_after,
        out_specs=(sem, sem, sem, *[hbm] * (n_in + n_g), pl.BlockSpec(memory_space=pltpu.VMEM)),
        input_output_aliases={k: 3 + k for k in range(n_in + n_g)},
        compiler_params=pltpu.CompilerParams(has_side_effects=pltpu.SideEffectType.DATAFLOW_SIDE_EFFECTING))(*args)
    handle = dict(sems=outs[0:3], srcs=list(outs[3:3 + n_in]), lands=list(outs[3 + n_in:3 + n_in + n_g]),
                  sizes=[len(g) for g in groups], gather=gather)
    return handle, outs[-1]


def _exchange_wait(handle, *, name, after):
    srcs, lands, sizes, gather = handle["srcs"], handle["lands"], handle["sizes"], handle["gather"]
    n_in, n_g = len(srcs), len(lands)

    def body(*refs):
        ins, land = refs[:n_in], refs[n_in:n_in + n_g]
        send_sems, recv_sems, loc_sems = refs[n_in + n_g:n_in + n_g + 3]
        x, y, c = lax.axis_index("x"), lax.axis_index("y"), lax.axis_index("c")
        me = 4 * x + 2 * y + c
        for gi in range(n_g):
            for r in range(1, NDEV):
                px, py, pc = _peer_of(x, y, c, r)
                peer = 4 * px + 2 * py + pc
                slab = pltpu.make_async_remote_copy(src_ref=land[gi].at[me], dst_ref=land[gi].at[peer],
                                                    send_sem=send_sems.at[gi * (NDEV - 1) + r - 1], recv_sem=recv_sems.at[gi * (NDEV - 1) + r - 1],
                                                    device_id=(px, py, pc), device_id_type=MESH)
                slab.wait_send()
                slab.wait_recv()
        k = 0
        for gi in range(n_g):
            for li in range(sizes[gi]):
                dst = land[gi].at[me] if sizes[gi] == 1 else land[gi].at[me, li]
                pltpu.make_async_copy(ins[k] if gather else ins[k].at[me], dst, loc_sems.at[k]).wait()
                k += 1

    hbm = pl.BlockSpec(memory_space=pltpu.HBM)
    sem = pl.BlockSpec(memory_space=pltpu.SEMAPHORE)
    outs = pl.pallas_call(body, name=name, out_shape=tuple(pltpu.HBM(a.shape, a.dtype) for a in srcs + lands),
        in_specs=[hbm] * (n_in + n_g) + [sem] * 3 + [pl.BlockSpec(memory_space=pl.ANY)],
        out_specs=tuple([hbm] * (n_in + n_g)), input_output_aliases={k: k for k in range(n_in + n_g)},
        compiler_params=pltpu.CompilerParams(has_side_effects=pltpu.SideEffectType.DATAFLOW_SIDE_EFFECTING))(
            *srcs, *lands, *handle["sems"], after)
    return list(outs[n_in:])


def _gate_part(first, dx, y_ref, g_ref, dy_ref, dg_ref, db_ref):
    @pl.when(first)
    def _():
        dg_ref[...] = jnp.zeros_like(dg_ref)
        db_ref[...] = jnp.zeros_like(db_ref)

    dy = dx * g_ref[...]
    dy_ref[...] = dy.astype(BF16)
    dg_ref[...] += jnp.sum(dx * y_ref[...].astype(F32), axis=0, keepdims=True)
    db_ref[...] += jnp.sum(dy, axis=0, keepdims=True)


def _norm_bwd_part(first, dhv, x_ref, g_ref, sc_ref, dres_ref, dx_ref, dsc_ref, dsh_ref, dg_ref):
    @pl.when(first)
    def _():
        dsc_ref[...] = jnp.zeros_like(dsc_ref)
        dsh_ref[...] = jnp.zeros_like(dsh_ref)
        dg_ref[...] = jnp.zeros_like(dg_ref)

    xv = x_ref[...]
    r = lax.rsqrt(jnp.mean(xv * xv, axis=-1, keepdims=True) + EPS)
    xhat = xv * r
    gain_v = g_ref[...]
    dsc_ref[...] += jnp.sum(dhv * (xhat * gain_v), axis=0, keepdims=True)
    dsh_ref[...] += jnp.sum(dhv, axis=0, keepdims=True)
    dxn = dhv * (1.0 + sc_ref[...])
    dg_ref[...] += jnp.sum(dxn * xhat, axis=0, keepdims=True)
    dxhat = dxn * gain_v
    dx = dres_ref[...] + r * (dxhat - xhat * jnp.mean(dxhat * xhat, axis=-1, keepdims=True))
    dx_ref[...] = dx
    return dx


def _final_loss(x, gain, target, y_prev, gate_prev, *, name):
    S, D = x.shape
    tm = _row_tile(S, 512)

    def body(x_ref, g_ref, t_ref, y_ref, gp_ref, loss_ref, dx_ref, dg_ref, dy_ref, dgp_ref, dbp_ref):
        first = pl.program_id(0) == 0

        @pl.when(first)
        def _():
            loss_ref[...] = jnp.zeros_like(loss_ref)
            dg_ref[...] = jnp.zeros_like(dg_ref)

        xv = x_ref[...]
        r = lax.rsqrt(jnp.mean(xv * xv, axis=-1, keepdims=True) + EPS)
        xhat = xv * r
        gv = g_ref[...]
        err = xhat * gv - t_ref[...]
        row_loss = jnp.mean(err * err, axis=-1, keepdims=True)
        loss_ref[...] += 0.5 * jnp.sum(row_loss, axis=0, keepdims=True)
        dy = err * (1.0 / D)
        dg_ref[...] += jnp.sum(dy * xhat, axis=0, keepdims=True)
        dxhat = dy * gv
        dx = r * (dxhat - xhat * jnp.mean(dxhat * xhat, axis=-1, keepdims=True))
        dx_ref[...] = dx
        _gate_part(first, dx, y_ref, gp_ref, dy_ref, dgp_ref, dbp_ref)

    row = pl.BlockSpec((tm, D), lambda i: (i, 0))
    vec = pl.BlockSpec((1, D), lambda i: (0, 0))
    one = pl.BlockSpec((1, 1), lambda i: (0, 0))
    vsh = _sds((1, D), F32)
    return _call(body, name=name, grid=(S // tm,), in_specs=[row, vec, row, row, vec],
                 out_specs=(one, row, vec, row, vec, vec),
                 out_shape=(_sds((1, 1), F32), _sds((S, D), F32), vsh, _sds((S, D), BF16), vsh, vsh),
                 sem=("arbitrary",))(x, gain, target, y_prev, gate_prev)


def _pick_tm(M, bytes_per_row, fixed_bytes):
    for tm in (1024, 512, 256, 128):
        if M % tm == 0 and 2 * tm * bytes_per_row + fixed_bytes <= VMEM_BLOCK_BUDGET:
            return tm
    return _row_tile(M, 128)


def _mm_nn(a, w, *, name, bias=None, relu2=False, ln=None, norm=None, res=None, gate=None, out_dtype=BF16, after=None):
    M, K = a.shape
    col = w.ndim == 3
    if col:
        nsh, ns = w.shape[0], w.shape[2]
        w_spec = pl.BlockSpec((nsh, K, ns), lambda i: (0, 0, 0))
    else:
        nsh, ns = 1, w.shape[1]
        w_spec = pl.BlockSpec((K, ns), lambda i: (0, 0))
    N = nsh * ns
    residual = res is not None
    out_bytes = (4 + 4 + 2) if residual else jnp.dtype(out_dtype).itemsize
    tm = _pick_tm(M, K * a.dtype.itemsize + N * out_bytes + (K * 2 if norm is not None else 0), 2 * K * N * 2)

    def body(*refs):
        it = iter(refs)
        a_ref, w_ref = next(it), next(it)
        b_ref = next(it) if bias is not None else None
        lg_ref, lb_ref = (next(it), next(it)) if ln is not None else (None, None)
        ng_ref, nsc_ref, nsh_ref = (next(it), next(it), next(it)) if norm is not None else (None, None, None)
        res_ref, gate_ref = (next(it), next(it)) if residual else (None, None)
        if after is not None:
            next(it)
        out_ref = next(it)
        raw_ref = next(it) if residual else None
        av = a_ref[...]
        if relu2:
            av = jnp.square(jnp.maximum(av.astype(F32), 0.0))
        if ln is not None:
            av, _ = _ln_silu(av, lg_ref[...], lb_ref[...])
        if norm is not None:
            r = lax.rsqrt(jnp.mean(av * av, axis=-1, keepdims=True) + EPS)
            av = (av * r) * ng_ref[...] * (1.0 + nsc_ref[...]) + nsh_ref[...]
        ab = av.astype(BF16)
        if norm is not None:
            next(it)[...] = ab
        for d in range(nsh):
            cols = slice(d * ns, (d + 1) * ns)
            acc = jnp.dot(ab, w_ref[d] if col else w_ref[...], preferred_element_type=F32)
            if b_ref is not None:
                acc = acc + b_ref[:, cols]
            if residual:
                raw_ref[:, cols] = acc.astype(BF16)
                out_ref[:, cols] = res_ref[:, cols] + gate_ref[:, cols] * acc
            else:
                out_ref[:, cols] = acc.astype(out_dtype)

    tile = pl.BlockSpec((tm, N), lambda i: (i, 0))
    vec = pl.BlockSpec((1, N), lambda i: (0, 0))
    in_specs, args = [pl.BlockSpec((tm, K), lambda i: (i, 0)), w_spec], [a, w]
    if bias is not None:
        in_specs.append(vec)
        args.append(bias)
    if ln is not None:
        in_specs += [pl.BlockSpec((1, K), lambda i: (0, 0))] * 2
        args += list(ln)
    if norm is not None:
        in_specs += [pl.BlockSpec((1, K), lambda i: (0, 0))] * 3
        args += list(norm)
    if residual:
        in_specs += [tile, vec]
        args += [res, gate]
        out_specs = [tile, tile]
        out_shape = [_sds((M, N), F32), _sds((M, N), BF16)]
    else:
        out_specs = [tile]
        out_shape = [_sds((M, N), out_dtype)]
    if after is not None:
        in_specs.append(pl.BlockSpec(memory_space=pl.ANY))
        args.append(after)
    if norm is not None:
        out_specs.append(pl.BlockSpec((tm, K), lambda i: (i, 0)))
        out_shape.append(_sds((M, K), BF16))
    outs = _call(body, name=name, grid=(M // tm,), in_specs=in_specs, out_specs=tuple(out_specs), out_shape=tuple(out_shape),
                 sem=("parallel",))(*args)
    return outs[0] if len(outs) == 1 else outs


def _mm_nt(g, w, *, name, z=None, out_dtype=F32, after=None, norm=None, gated=None, lnbwd=None):
    M, N = g.shape
    col = w.ndim == 3
    if col:
        nsh, K, ns = w.shape
        w_spec = pl.BlockSpec((nsh, K, ns), lambda i: (0, 0, 0))
    else:
        K = w.shape[0]
        w_spec = pl.BlockSpec((K, N), lambda i: (0, 0))
    assert norm is None or col
    kc = min(K, 1024)
    assert lnbwd is None or (not col and kc == K)
    obytes = jnp.dtype(out_dtype).itemsize
    row_bytes = N * g.dtype.itemsize + K * obytes + (K * 2 if z is not None else 0)
    if norm is not None:
        row_bytes += 2 * K * 4 + (K * 4 if gated is not None else 0)
    if lnbwd is not None:
        row_bytes += K * 4
    tm = _pick_tm(M, row_bytes, 2 * K * N * 2 + 512 * K * 4)

    def body(*refs):
        it = iter(refs)
        g_ref, w_ref = next(it), next(it)
        z_ref = next(it) if z is not None else None
        norm_in = [next(it) for _ in range(4)] if norm is not None else None
        gate_in = [next(it) for _ in range(2)] if gated is not None else None
        ln_in = [next(it) for _ in range(3)] if lnbwd is not None else None
        if after is not None:
            next(it)
        out_ref = next(it)
        if col:
            acc = None
            for d in range(nsh):
                part = lax.dot_general(g_ref[:, d * ns:(d + 1) * ns].astype(BF16), w_ref[d], NT_DIMS,
                                       preferred_element_type=F32)
                acc = part if acc is None else acc + part
            if norm is None:
                out_ref[...] = acc.astype(out_dtype)
            else:
                first = pl.program_id(0) == 0
                dx = _norm_bwd_part(first, acc, *norm_in, out_ref, next(it), next(it), next(it))
                if gated is not None:
                    _gate_part(first, dx, *gate_in, next(it), next(it), next(it))
        else:
            gb = g_ref[...].astype(BF16)
            for cki in range(K // kc):
                cols = slice(cki * kc, (cki + 1) * kc)
                part = lax.dot_general(gb, w_ref[cols, :], NT_DIMS, preferred_element_type=F32)
                if z_ref is not None:
                    part = part * (2.0 * jnp.maximum(z_ref[:, cols].astype(F32), 0.0))
                if lnbwd is not None:
                    _ln_silu_bwd_part(pl.program_id(0) == 0, part, *ln_in, out_ref, next(it), next(it), next(it))
                else:
                    out_ref[:, cols] = part.astype(out_dtype)

    row = pl.BlockSpec((tm, K), lambda i: (i, 0))
    vec = pl.BlockSpec((1, K), lambda i: (0, 0))
    vsh = _sds((1, K), F32)
    in_specs, args = [pl.BlockSpec((tm, N), lambda i: (i, 0)), w_spec], [g, w]
    out_specs, out_shape = [row], [_sds((M, K), out_dtype)]
    if z is not None:
        in_specs.append(row)
        args.append(z)
    if norm is not None:
        x, gain, sc, dres = norm
        in_specs += [row, vec, vec, row]
        args += [x, gain, sc, dres]
        out_specs += [vec, vec, vec]
        out_shape += [vsh, vsh, vsh]
    if gated is not None:
        in_specs += [row, vec]
        args += list(gated)
        out_specs += [row, vec, vec]
        out_shape += [_sds((M, K), BF16), vsh, vsh]
    if lnbwd is not None:
        in_specs += [row, vec, vec]
        args += list(lnbwd)
        out_specs += [vec, vec, vec]
        out_shape += [vsh, vsh, vsh]
    if after is not None:
        in_specs.append(pl.BlockSpec(memory_space=pl.ANY))
        args.append(after)
    outs = _call(body, name=name, grid=(M // tm,), in_specs=in_specs, out_specs=tuple(out_specs), out_shape=tuple(out_shape),
                 sem=("parallel",) if norm is None and lnbwd is None else ("arbitrary",))(*args)
    return outs[0] if len(outs) == 1 else outs


def _mm_tn(a, g, *, name, col_shards=None, relu2=False, ln=None):
    M, K = a.shape
    N = g.shape[1]
    acc_budget = 8 * 1024 * 1024
    if col_shards:
        ns = N // col_shards
        spc = col_shards
        while spc > 1 and K * ns * spc * 4 > acc_budget:
            spc //= 2
        step_cols = K * a.dtype.itemsize + spc * ns * g.dtype.itemsize
    else:
        tk = K
        while tk > 128 and tk * N * 4 > acc_budget:
            tk //= 2
        step_cols = tk * a.dtype.itemsize + N * g.dtype.itemsize
    tm = _row_tile(M, 2048)
    while tm > 256 and 2 * tm * step_cols + 2 * acc_budget > VMEM_BLOCK_BUDGET:
        tm //= 2
    nm = M // tm
    if col_shards:
        grid = (col_shards // spc, nm)
        a_spec = pl.BlockSpec((tm, K), lambda c, m: (m, 0))
        g_spec = pl.BlockSpec((tm, spc * ns), lambda c, m: (m, c))
        out_spec = pl.BlockSpec((spc, K, ns), lambda c, m: (c, 0, 0))
        out_shape = _sds((col_shards, K, ns), BF16)
        acc_shape = (K, spc * ns)
    else:
        grid = (K // tk, nm)
        a_spec = pl.BlockSpec((tm, tk), lambda c, m: (m, c))
        g_spec = pl.BlockSpec((tm, N), lambda c, m: (m, 0))
        out_spec = pl.BlockSpec((tk, N), lambda c, m: (c, 0))
        out_shape = _sds((K, N), BF16)
        acc_shape = (tk, N)
        assert ln is None or tk == K
    in_specs, args = [a_spec, g_spec], [a, g]
    if ln is not None:
        in_specs += [pl.BlockSpec((1, K), lambda c, m: (0, 0))] * 2
        args += list(ln)

    def body(a_ref, g_ref, *rest):
        out_ref, acc_ref = rest[-2:]
        m = pl.program_id(1)

        @pl.when(m == 0)
        def _():
            acc_ref[...] = jnp.zeros_like(acc_ref)

        av = a_ref[...]
        if relu2:
            av = jnp.square(jnp.maximum(av.astype(F32), 0.0))
        if ln is not None:
            av, _ = _ln_silu(av, rest[0][...], rest[1][...])
        acc_ref[...] += lax.dot_general(av.astype(BF16), g_ref[...].astype(BF16), TN_DIMS, preferred_element_type=F32)

        @pl.when(m == nm - 1)
        def _():
            if col_shards:
                for s in range(spc):
                    out_ref[s] = acc_ref[:, s * ns:(s + 1) * ns].astype(BF16)
            else:
                out_ref[...] = acc_ref[...].astype(BF16)

    return _call(body, name=name, grid=grid, in_specs=in_specs, out_specs=out_spec, out_shape=out_shape,
                 scratch=[pltpu.VMEM(acc_shape, F32)], sem=("parallel", "arbitrary"))(*args)


CONV_TILE = 256


def _glu_rows(u2, ch):
    d = u2.shape[1] // 2
    return (u2[:, :d] * _sigmoid(u2[:, d:])).reshape(u2.shape[0], ch, LANES)


def _fill_glu(buf, u_ref, uh_ref, ch, tile):
    first = pl.program_id(0) == 0
    buf[0:HALO] = jnp.where(first, 0.0, _glu_rows(uh_ref[...], ch))
    buf[HALO:HALO + tile] = _glu_rows(u_ref[...], ch)


CONV_SUB = 4


def _conv_specs(S, D, tile):
    per = tile // HALO
    u_spec = pl.BlockSpec((tile, 2 * D), lambda i: (i, 0))
    uh_spec = pl.BlockSpec((HALO, 2 * D), lambda i: (jnp.maximum(i * per - 1, 0), 0))
    x_spec = pl.BlockSpec((tile, D), lambda i: (i, 0))
    xn_spec = pl.BlockSpec((HALO, D), lambda i: (jnp.minimum((i + 1) * per, S // HALO - 1), 0))
    w_spec = pl.BlockSpec((CONV_WIDTH, D // LANES, LANES), lambda i: (0, 0, 0))
    v_spec = pl.BlockSpec((1, D // LANES, LANES), lambda i: (0, 0, 0))
    return u_spec, uh_spec, x_spec, xn_spec, w_spec, v_spec


def _conv_mid_fwd(u, w3, bdw3, *, name):
    S, D = u.shape[0], u.shape[1] // 2
    ch = D // LANES
    tile = _row_tile(S, CONV_TILE)
    sub = _row_tile(tile, 2 * CONV_SUB)
    half = (CONV_WIDTH + 1) // 2
    u_spec, uh_spec, x_spec, _, w_spec, v_spec = _conv_specs(S, D, tile)

    def body(u_ref, uh_ref, w_ref, b_ref, o_ref, buf, stage):
        _fill_glu(buf, u_ref, uh_ref, ch, tile)

        def taps(lo, hi, start):
            def step(q, carry):
                rows = pl.ds(q * sub, sub)
                acc = [b_ref[...] if start else stage[rows], None]
                for k in range(lo, hi):
                    term = buf[pl.ds(q * sub + (HALO - CONV_WIDTH + 1 + k), sub)] * w_ref[k]
                    acc[k % 2] = term if acc[k % 2] is None else acc[k % 2] + term
                stage[rows] = acc[0] + acc[1]
                return carry

            lax.fori_loop(0, tile // sub, step, 0)

        taps(0, half, True)
        taps(half, CONV_WIDTH, False)
        o_ref[...] = stage[...].reshape(tile, D)

    return _call(body, name=name, grid=(S // tile,), in_specs=[u_spec, uh_spec, w_spec, v_spec], out_specs=x_spec,
                 out_shape=_sds((S, D), F32),
                 scratch=[pltpu.VMEM((tile + HALO, ch, LANES), F32), pltpu.VMEM((tile, ch, LANES), F32)],
                 sem=("parallel",))(u, u, w3, bdw3)


def _ln_silu(v, gv, bv):
    mu = jnp.mean(v, axis=-1, keepdims=True)
    cen = v - mu
    rstd = lax.rsqrt(jnp.mean(cen * cen, axis=-1, keepdims=True) + EPS)
    nrm = cen * rstd
    ln = nrm * gv + bv
    sg = _sigmoid(ln)
    return ln * sg, (nrm, rstd, ln, sg)


def _ln_silu_bwd_part(first, ds, v_ref, g_ref, b_ref, ddw_ref, dg_ref, db_ref, dbdw_ref):
    @pl.when(first)
    def _():
        dg_ref[...] = jnp.zeros_like(dg_ref)
        db_ref[...] = jnp.zeros_like(db_ref)
        dbdw_ref[...] = jnp.zeros_like(dbdw_ref)

    gv = g_ref[...]
    _, (nrm, rstd, ln, sg) = _ln_silu(v_ref[...], gv, b_ref[...])
    dln = ds * (sg * (1.0 + ln * (1.0 - sg)))
    dg_ref[...] += jnp.sum(dln * nrm, axis=0, keepdims=True)
    db_ref[...] += jnp.sum(dln, axis=0, keepdims=True)
    dn = dln * gv
    ddw = rstd * (dn - jnp.mean(dn, axis=-1, keepdims=True) - nrm * jnp.mean(dn * nrm, axis=-1, keepdims=True))
    dbdw_ref[...] += jnp.sum(ddw, axis=0, keepdims=True)
    ddw_ref[...] = ddw


def _conv_mid_bwd_dw(u, ddw, w3, *, name):
    S, D = ddw.shape
    ch = D // LANES
    tile = _row_tile(S, CONV_TILE)
    sub = _row_tile(tile, 2 * CONV_SUB)
    last = S // tile - 1
    u_spec, uh_spec, x_spec, xn_spec, w_spec, _ = _conv_specs(S, D, tile)
    b_spec = pl.BlockSpec((1, 2 * D), lambda i: (0, 0))

    def body(u_ref, uh_ref, d_ref, dn_ref, w_ref, du_ref, dw_ref, db_ref, gbuf, dbuf, stage):
        @pl.when(pl.program_id(0) == 0)
        def _():
            dw_ref[...] = jnp.zeros_like(dw_ref)
            db_ref[...] = jnp.zeros_like(db_ref)

        _fill_glu(gbuf, u_ref, uh_ref, ch, tile)
        dbuf[0:tile] = d_ref[...].reshape(tile, ch, LANES)
        dbuf[tile:tile + HALO] = jnp.where(pl.program_id(0) == last, 0.0, dn_ref[...].reshape(HALO, ch, LANES))

        def taps(lo, hi, start):
            def step(q, c):
                s0 = q * sub
                ddw_q = dbuf[pl.ds(s0, sub)]
                acc = [None if start else stage[pl.ds(s0, sub)], None]
                for k in range(lo, hi):
                    term = dbuf[pl.ds(s0 + (CONV_WIDTH - 1 - k), sub)] * w_ref[k]
                    acc[k % 2] = term if acc[k % 2] is None else acc[k % 2] + term
                    dw_ref[k] += jnp.sum(ddw_q * gbuf[pl.ds(s0 + (HALO - CONV_WIDTH + 1 + k), sub)], axis=0)
                stage[pl.ds(s0, sub)] = acc[0] + acc[1]
                return c

            lax.fori_loop(0, tile // sub, step, 0)

        half = (CONV_WIDTH + 1) // 2
        taps(0, half, True)
        taps(half, CONV_WIDTH, False)
        dglu = stage[...].reshape(tile, D)
        uv = u_ref[...]
        av, sg = uv[:, :D], _sigmoid(uv[:, D:])
        da = dglu * sg
        dg = da * av * (1.0 - sg)
        du_ref[:, 0:D] = da
        du_ref[:, D:2 * D] = dg
        db_ref[:, 0:D] += jnp.sum(da, axis=0, keepdims=True)
        db_ref[:, D:2 * D] += jnp.sum(dg, axis=0, keepdims=True)

    return _call(body, name=name, grid=(S // tile,), in_specs=[u_spec, uh_spec, x_spec, xn_spec, w_spec],
                 out_specs=(u_spec, w_spec, b_spec),
                 out_shape=(_sds((S, 2 * D), F32), _sds((CONV_WIDTH, ch, LANES), F32), _sds((1, 2 * D), F32)),
                 scratch=[pltpu.VMEM((tile + HALO, ch, LANES), F32), pltpu.VMEM((tile + HALO, ch, LANES), F32),
                          pltpu.VMEM((tile, ch, LANES), F32)],
                 sem=("arbitrary",))(u, u, ddw, ddw, w3)


def _ret_tables(S, dk):
    f32 = np.float32
    B = min(RET_BLOCK, S)
    lg = np.log(f32(1.0) - f32(2.0) ** (f32(-5.0) - np.arange(RET_HEADS, dtype=f32)))
    idx = np.arange(B, dtype=f32)
    diff = idx[:, None] - idx[None, :]
    cq, ck = (np.arange(B) // CHUNK)[:, None], (np.arange(B) // CHUNK)[None, :]
    dist = np.where(cq == ck, np.abs(diff), diff)
    mask = np.where(ck <= cq, np.exp(lg[:, None, None] * dist[None]), f32(0.0)).astype(f32)
    xi = np.exp(lg[:, None] * (idx + f32(1.0)))[..., None].astype(f32)
    zeta = np.exp(lg[:, None] * (f32(B - 1.0) - idx))[..., None].astype(f32)
    gam = np.broadcast_to(np.exp(lg * f32(B))[:, None, None], (RET_HEADS, 8, LANES)).astype(f32)
    pos = np.arange(S, dtype=f32)
    inv = (f32(ROPE_BASE) ** (-np.arange(0, dk, 2, dtype=f32) / f32(dk))).astype(f32)
    ang = (pos[:, None] * inv[None, :]).astype(f32)
    tb = dict(mask=mask, xi=xi, zeta=zeta, gam=gam, cos=np.cos(ang).astype(f32), sin=np.sin(ang).astype(f32))
    return dict(B=B, **{k: jnp.asarray(v) for k, v in tb.items()})


def _rope(v, cs, sn):
    half = v.shape[1] // 2
    v1, v2 = v[:, :half], v[:, half:]
    return jnp.concatenate([v1 * cs - v2 * sn, v2 * cs + v1 * sn], axis=-1)


def _rope_t(d, cs, sn):
    half = d.shape[1] // 2
    d1, d2 = d[:, :half], d[:, half:]
    return jnp.concatenate([d1 * cs + d2 * sn, d2 * cs - d1 * sn], axis=-1)


def _dot(a, b):
    return jnp.dot(a.astype(BF16), b.astype(BF16), preferred_element_type=F32)


def _dot_nt(a, b):
    return lax.dot_general(a.astype(BF16), b.astype(BF16), NT_DIMS, preferred_element_type=F32)


def _dot_tn(a, b):
    return lax.dot_general(a.astype(BF16), b.astype(BF16), TN_DIMS, preferred_element_type=F32)


def _ret_specs(S, D, B, RB, reverse):
    dk, dv = D // RET_HEADS, 2 * D // RET_HEADS
    nb = S // RB
    blk = (lambda ib: nb - 1 - ib) if reverse else (lambda ib: ib)
    q = pl.BlockSpec((RB, dk), lambda h, ib: (blk(ib), h))
    k = pl.BlockSpec((RB, dk), lambda h, ib: (blk(ib), RET_HEADS + h))
    v = pl.BlockSpec((RB, dv), lambda h, ib: (blk(ib), RET_HEADS + h))
    gate = pl.BlockSpec((RB, dv), lambda h, ib: (blk(ib), 2 * RET_HEADS + h))
    yv = pl.BlockSpec((RB, dv), lambda h, ib: (blk(ib), h))
    rope = pl.BlockSpec((RB, dk // 2), lambda h, ib: (blk(ib), 0))
    mask = pl.BlockSpec((None, B, B), lambda h, ib: (h, 0, 0))
    dec = pl.BlockSpec((None, B, 1), lambda h, ib: (h, 0, 0))
    gam = pl.BlockSpec((None, 8, LANES), lambda h, ib: (h, 0, 0))
    gn = pl.BlockSpec((1, dv), lambda h, ib: (0, h))
    return dict(q=q, k=k, v=v, gate=gate, yv=yv, rope=rope, mask=mask, dec=dec, gam=gam, gn=gn)


def _group_norm(yr, gv, bv):
    mu = jnp.mean(yr, axis=-1, keepdims=True)
    cen = yr - mu
    rstd = lax.rsqrt(jnp.mean(cen * cen, axis=-1, keepdims=True) + EPS)
    nrm = cen * rstd
    return nrm, rstd, nrm * gv + bv


def _ret_fwd(proj, tb, gng, gnb, *, name):
    S, D = proj.shape[0], proj.shape[1] // 6
    dk, dv = D // RET_HEADS, 2 * D // RET_HEADS
    B = tb["B"]
    RB = _row_tile(S, 8 * B)
    nsub = RB // B
    sp = _ret_specs(S, D, B, RB, False)
    scale = dk ** -0.5

    def body(q_ref, k_ref, v_ref, gt_ref, cos_ref, sin_ref, mask_ref, xi_ref, zeta_ref, gam_ref, gng_ref, gnb_ref,
             yr_ref, yg_ref, qr_ref, kr_ref, state):
        @pl.when(pl.program_id(1) == 0)
        def _():
            state[...] = jnp.zeros_like(state)

        for sb in range(nsub):
            rows = slice(sb * B, (sb + 1) * B)
            cs, sn = cos_ref[rows, :], sin_ref[rows, :]
            q = _rope(q_ref[rows, :].astype(F32), cs, sn)
            k = _rope(k_ref[rows, :].astype(F32), cs, sn) * scale
            qr_ref[rows, :] = q.astype(BF16)
            kr_ref[rows, :] = k.astype(BF16)
            vb = v_ref[rows, :]
            p = _dot_nt(q, k) * mask_ref[...]
            st = state[...]
            yr = _dot(p, vb) + _dot(q * xi_ref[...], st)
            state[...] = st * gam_ref[0:1, 0:1] + _dot_tn(k * zeta_ref[...], vb)
            _, _, gn = _group_norm(yr, gng_ref[...], gnb_ref[...])
            gt = gt_ref[rows, :].astype(F32)
            yr_ref[rows, :] = yr.astype(BF16)
            yg_ref[rows, :] = (gt * _sigmoid(gt) * gn).astype(BF16)

    return _call(body, name=name, grid=(RET_HEADS, S // RB),
                 in_specs=[sp["q"], sp["k"], sp["v"], sp["gate"], sp["rope"], sp["rope"], sp["mask"], sp["dec"], sp["dec"],
                           sp["gam"], sp["gn"], sp["gn"]],
                 out_specs=(sp["yv"], sp["yv"], sp["q"], sp["q"]),
                 out_shape=(_sds((S, 2 * D), BF16), _sds((S, 2 * D), BF16), _sds((S, D), BF16), _sds((S, D), BF16)),
                 scratch=[pltpu.VMEM((dk, dv), F32)], sem=("parallel", "arbitrary"))(
                     proj, proj, proj, proj, tb["cos"], tb["sin"], tb["mask"], tb["xi"], tb["zeta"], tb["gam"], gng, gnb)


def _ret_bwd_q(proj, kr, yr, dyg, tb, gng, gnb, *, name):
    S, D = proj.shape[0], proj.shape[1] // 6
    dk, dv = D // RET_HEADS, 2 * D // RET_HEADS
    B = tb["B"]
    RB = _row_tile(S, 8 * B)
    nsub = RB // B
    sp = _ret_specs(S, D, B, RB, False)

    def body(k_ref, v_ref, gt_ref, yr_ref, dyg_ref, cos_ref, sin_ref, mask_ref, xi_ref, zeta_ref, gam_ref,
             gng_ref, gnb_ref, dq_ref, dgt_ref, dyr_ref, dgg_ref, dgb_ref, state):
        @pl.when(pl.program_id(1) == 0)
        def _():
            state[...] = jnp.zeros_like(state)
            dgg_ref[...] = jnp.zeros_like(dgg_ref)
            dgb_ref[...] = jnp.zeros_like(dgb_ref)

        for sb in range(nsub):
            rows = slice(sb * B, (sb + 1) * B)
            cs, sn = cos_ref[rows, :], sin_ref[rows, :]
            k = k_ref[rows, :]
            vb = v_ref[rows, :]
            gv = gng_ref[...]
            nrm, rstd, gn = _group_norm(yr_ref[rows, :].astype(F32), gv, gnb_ref[...])
            gt = gt_ref[rows, :].astype(F32)
            sg = _sigmoid(gt)
            dyg = dyg_ref[rows, :].astype(F32)
            dgt_ref[rows, :] = (dyg * gn * (sg * (1.0 + gt * (1.0 - sg)))).astype(BF16)
            dgn = dyg * (gt * sg)
            dgg_ref[...] += jnp.sum(dgn * nrm, axis=0, keepdims=True)
            dgb_ref[...] += jnp.sum(dgn, axis=0, keepdims=True)
            dn = dgn * gv
            dyr = rstd * (dn - jnp.mean(dn, axis=-1, keepdims=True) - nrm * jnp.mean(dn * nrm, axis=-1, keepdims=True))
            dyr_ref[rows, :] = dyr.astype(BF16)
            dp = _dot_nt(dyr, vb) * mask_ref[...]
            st = state[...]
            dq = _dot(dp, k) + _dot_nt(dyr, st) * xi_ref[...]
            dq_ref[rows, :] = _rope_t(dq, cs, sn).astype(BF16)
            state[...] = st * gam_ref[0:1, 0:1] + _dot_tn(k.astype(F32) * zeta_ref[...], vb)

    return _call(body, name=name, grid=(RET_HEADS, S // RB),
                 in_specs=[sp["q"], sp["v"], sp["gate"], sp["yv"], sp["yv"], sp["rope"], sp["rope"], sp["mask"],
                           sp["dec"], sp["dec"], sp["gam"], sp["gn"], sp["gn"]],
                 out_specs=(sp["q"], sp["yv"], sp["yv"], sp["gn"], sp["gn"]),
                 out_shape=(_sds((S, D), BF16), _sds((S, 2 * D), BF16), _sds((S, 2 * D), BF16), _sds((1, 2 * D), F32),
                            _sds((1, 2 * D), F32)),
                 scratch=[pltpu.VMEM((dk, dv), F32)], sem=("parallel", "arbitrary"))(
                     kr, proj, proj, yr, dyg, tb["cos"], tb["sin"], tb["mask"], tb["xi"], tb["zeta"], tb["gam"], gng, gnb)


def _ret_bwd_kv(proj, qr, kr, dyr, dq, dgt, tb, *, name):
    S, D = proj.shape[0], proj.shape[1] // 6
    dk, dv = D // RET_HEADS, 2 * D // RET_HEADS
    B = tb["B"]
    RB = _row_tile(S, 2 * B)
    nsub = RB // B
    nb = S // RB
    scale = dk ** -0.5

    def body(v_ref, qr_ref, kr_ref, dyr_ref, dq_ref, dgt_ref, cos_ref, sin_ref, mask_ref, xi_ref, zeta_ref, gam_ref, out_ref,
             dstate):
        @pl.when(pl.program_id(0) == 0)
        def _():
            dstate[...] = jnp.zeros_like(dstate)

        out_ref[:, 0:D] = dq_ref[...]
        out_ref[:, 4 * D:6 * D] = dgt_ref[...]
        for sb in reversed(range(nsub)):
            rows = slice(sb * B, (sb + 1) * B)
            cs, sn = cos_ref[rows, :], sin_ref[rows, :]
            for h in range(RET_HEADS):
                kcols = slice(D + h * dk, D + (h + 1) * dk)
                vcols = slice(2 * D + h * dv, 2 * D + (h + 1) * dv)
                q = qr_ref[rows, h * dk:(h + 1) * dk]
                k = kr_ref[rows, h * dk:(h + 1) * dk]
                vb = v_ref[rows, h * dv:(h + 1) * dv]
                dyr_h = dyr_ref[rows, h * dv:(h + 1) * dv]
                mk = mask_ref[h]
                p = _dot_nt(q, k) * mk
                dp = _dot_nt(dyr_h, vb) * mk
                ds = dstate[h]
                zt = zeta_ref[h]
                dkr = _dot_tn(dp, q) + _dot_nt(vb, ds) * zt
                out_ref[rows, kcols] = _rope_t(dkr * scale, cs, sn).astype(BF16)
                out_ref[rows, vcols] = (_dot_tn(p, dyr_h) + _dot(k.astype(F32) * zt, ds)).astype(BF16)
                dstate[h] = ds * gam_ref[h, 0:1, 0:1] + _dot_tn(q.astype(F32) * xi_ref[h], dyr_h)

    def rev(width):
        return pl.BlockSpec((RB, width), lambda ib: (nb - 1 - ib, 0))

    def whole(a):
        return pl.BlockSpec(a.shape, lambda ib: (0,) * a.ndim)

    return _call(body, name=name, grid=(nb,),
                 in_specs=[pl.BlockSpec((RB, 2 * D), lambda ib: (nb - 1 - ib, 1)), rev(D), rev(D), rev(2 * D), rev(D), rev(2 * D),
                           rev(dk // 2), rev(dk // 2), whole(tb["mask"]), whole(tb["xi"]), whole(tb["zeta"]), whole(tb["gam"])],
                 out_specs=rev(6 * D), out_shape=_sds((S, 6 * D), BF16), scratch=[pltpu.VMEM((RET_HEADS, dk, dv), F32)],
                 sem=("arbitrary",))(proj, qr, kr, dyr, dq, dgt, tb["cos"], tb["sin"], tb["mask"], tb["xi"], tb["zeta"],
                                     tb["gam"])


def _ada_fwd(c_all, ada_w, *, name):
    L, D, ns = ada_w.shape

    def body(c_ref, w_ref, out_ref):
        cv = c_ref[...]
        cond = cv * _sigmoid(cv)
        out_ref[...] = jnp.dot(cond.astype(BF16), w_ref[...].astype(BF16), preferred_element_type=F32)

    return _call(body, name=name, grid=(L,), in_specs=[pl.BlockSpec((NDEV, D), lambda l: (0, 0)),
                                                      pl.BlockSpec((None, D, ns), lambda l: (l, 0, 0))],
                 out_specs=pl.BlockSpec((None, NDEV, ns), lambda l: (l, 0, 0)), out_shape=_sds((L, NDEV, ns), F32),
                 sem=("parallel",))(c_all, ada_w)


def _ada_bwd(c_all, dmod_cols, *, name):
    L, _, ns = dmod_cols.shape
    D = c_all.shape[1]

    def body(c_ref, d_ref, out_ref):
        cv = c_ref[...]
        cond = cv * _sigmoid(cv)
        out_ref[...] = lax.dot_general(cond.astype(BF16), d_ref[...].astype(BF16), TN_DIMS, preferred_element_type=F32)

    return _call(body, name=name, grid=(L,), in_specs=[pl.BlockSpec((NDEV, D), lambda l: (0, 0)),
                                                      pl.BlockSpec((None, NDEV, ns), lambda l: (l, 0, 0))],
                 out_specs=pl.BlockSpec((None, D, ns), lambda l: (l, 0, 0)), out_shape=_sds((L, D, ns), F32),
                 sem=("parallel",))(c_all, dmod_cols)


def _adamw(w, m, v, parts, *, name):
    shape = w.shape
    L, cols = len(parts), shape[-1]
    rows = w.size // (cols * L)
    n = parts[0].shape[0]
    tr = rows
    for cand in (256, 128, 64, 32, 16, 8):
        if rows % cand == 0:
            tr = cand
            break
    c1 = 1.0 - ADAM_B1 ** ADAM_STEP
    c2 = 1.0 - ADAM_B2 ** ADAM_STEP

    def body(w_ref, m_ref, v_ref, *rest):
        p_refs = rest[:L]
        g_ref, d_ref, m2_ref, v2_ref = rest[L:]
        layer = pl.program_id(0)
        for l in range(L):
            @pl.when(layer == l)
            def _(p_ref=p_refs[l]):
                g = p_ref[0].astype(F32)
                for i in range(1, n):
                    g = g + p_ref[i].astype(F32)
                m2 = ADAM_B1 * m_ref[...] + (1.0 - ADAM_B1) * g
                v2 = ADAM_B2 * v_ref[...] + (1.0 - ADAM_B2) * (g * g)
                g_ref[...] = g
                m2_ref[...] = m2
                v2_ref[...] = v2
                d_ref[...] = -ADAM_LR * ((m2 / c1) / (jnp.sqrt(v2 / c2) + ADAM_EPS) + ADAM_WD * w_ref[...])

    mat = pl.BlockSpec((None, tr, cols), lambda l, i: (l, i, 0))

    def part_spec(k):
        return pl.BlockSpec((n, tr, cols), lambda l, i: (0, jnp.where(l == k, i, 0), 0))

    outs = _call(body, name=name, grid=(L, rows // tr), in_specs=[mat, mat, mat] + [part_spec(k) for k in range(L)],
                 out_specs=(mat, mat, mat, mat), out_shape=tuple(_sds((L, rows, cols), F32) for _ in range(4)),
                 sem=("parallel", "parallel"))(w.reshape(L, rows, cols), m.reshape(L, rows, cols), v.reshape(L, rows, cols),
                                               *[p.reshape(n, rows, cols) for p in parts])
    return tuple(o.reshape(shape) for o in outs)


SMALL = ("ada_b", "norm_mix_g", "norm_mlp_g", "conv_b_pw1", "conv_b_dw", "conv_ln_g", "conv_ln_b", "conv_b_pw2",
         "final_norm_g")
WEIGHTS = ("ada_w", "ada_b", "norm_mix_g", "norm_mlp_g", "conv_w_pw1", "conv_b_pw1", "conv_w_dw", "conv_b_dw", "conv_ln_g",
           "conv_ln_b", "conv_w_pw2", "conv_b_pw2", "ret_w_in", "ret_gn_g", "ret_gn_b", "ret_w_out", "mlp_w1", "mlp_w2",
           "final_norm_g")


def kernel(x, c, ada_w, ada_b, norm_mix_g, norm_mlp_g, conv_w_pw1, conv_b_pw1, conv_w_dw, conv_b_dw, conv_ln_g, conv_ln_b, conv_w_pw2, conv_b_pw2, ret_w_in, ret_gn_g, ret_gn_b, ret_w_out, mlp_w1, mlp_w2, final_norm_g, loss_target, m_ada_w, m_ada_b, m_norm_mix_g, m_norm_mlp_g, m_conv_w_pw1, m_conv_b_pw1, m_conv_w_dw, m_conv_b_dw, m_conv_ln_g, m_conv_ln_b, m_conv_w_pw2, m_conv_b_pw2, m_ret_w_in, m_ret_gn_g, m_ret_gn_b, m_ret_w_out, m_mlp_w1, m_mlp_w2, m_final_norm_g, v_ada_w, v_ada_b, v_norm_mix_g, v_norm_mlp_g, v_conv_w_pw1, v_conv_b_pw1, v_conv_w_dw, v_conv_b_dw, v_conv_ln_g, v_conv_ln_b, v_conv_w_pw2, v_conv_b_pw2, v_ret_w_in, v_ret_gn_g, v_ret_gn_b, v_ret_w_out, v_mlp_w1, v_mlp_w2, v_final_norm_g):
    W = dict(ada_w=ada_w, ada_b=ada_b, norm_mix_g=norm_mix_g, norm_mlp_g=norm_mlp_g, conv_w_pw1=conv_w_pw1,
             conv_b_pw1=conv_b_pw1, conv_w_dw=conv_w_dw, conv_b_dw=conv_b_dw, conv_ln_g=conv_ln_g, conv_ln_b=conv_ln_b,
             conv_w_pw2=conv_w_pw2, conv_b_pw2=conv_b_pw2, ret_w_in=ret_w_in, ret_gn_g=ret_gn_g, ret_gn_b=ret_gn_b,
             ret_w_out=ret_w_out, mlp_w1=mlp_w1, mlp_w2=mlp_w2, final_norm_g=final_norm_g)
    Mo = dict(ada_w=m_ada_w, ada_b=m_ada_b, norm_mix_g=m_norm_mix_g, norm_mlp_g=m_norm_mlp_g, conv_w_pw1=m_conv_w_pw1,
              conv_b_pw1=m_conv_b_pw1, conv_w_dw=m_conv_w_dw, conv_b_dw=m_conv_b_dw, conv_ln_g=m_conv_ln_g,
              conv_ln_b=m_conv_ln_b, conv_w_pw2=m_conv_w_pw2, conv_b_pw2=m_conv_b_pw2, ret_w_in=m_ret_w_in,
              ret_gn_g=m_ret_gn_g, ret_gn_b=m_ret_gn_b, ret_w_out=m_ret_w_out, mlp_w1=m_mlp_w1, mlp_w2=m_mlp_w2,
              final_norm_g=m_final_norm_g)
    Vo = dict(ada_w=v_ada_w, ada_b=v_ada_b, norm_mix_g=v_norm_mix_g, norm_mlp_g=v_norm_mlp_g, conv_w_pw1=v_conv_w_pw1,
              conv_b_pw1=v_conv_b_pw1, conv_w_dw=v_conv_w_dw, conv_b_dw=v_conv_b_dw, conv_ln_g=v_conv_ln_g,
              conv_ln_b=v_conv_ln_b, conv_w_pw2=v_conv_w_pw2, conv_b_pw2=v_conv_b_pw2, ret_w_in=v_ret_w_in,
              ret_gn_g=v_ret_gn_g, ret_gn_b=v_ret_gn_b, ret_w_out=v_ret_w_out, mlp_w1=v_mlp_w1, mlp_w2=v_mlp_w2,
              final_norm_g=v_final_norm_g)

    S, D = x.shape[1], x.shape[2]
    CH = D // LANES
    n_conv, n_ret = conv_w_pw1.shape[0], ret_w_in.shape[0]
    me = 4 * lax.axis_index("x") + 2 * lax.axis_index("y") + lax.axis_index("c")
    xs = x.reshape(S, D)
    target = loss_target.reshape(S, D)

    def mixer_shards(i):
        j = i // 2
        if i % 2 == 0:
            return [[conv_w_pw1[j].astype(BF16)], [conv_w_pw2[j].astype(BF16)]]
        return [[ret_w_in[j].astype(BF16)], [ret_w_out[j].astype(BF16)]]

    def mlp_shards(i):
        return [[mlp_w1[i].astype(BF16)], [mlp_w2[i].astype(BF16)]]

    def mlp_weights(got):
        return got[0], got[1].reshape(4 * D, D)

    first_handle, _ = _exchange_start(mixer_shards(0)[:1], gather=True, name="gather_start_first")
    small = _exchange([[conv_w_dw], [ret_gn_g], [ret_gn_b], [c]], gather=True, name="gather_small")
    dw_g, gng_g, gnb_g, c_g = small
    dw3 = jnp.transpose(dw_g, (1, 2, 0, 3)).reshape(n_conv, CONV_WIDTH, CH, LANES)
    gng_full = jnp.transpose(gng_g, (1, 2, 0, 3)).reshape(n_ret, 1, 2 * D)
    gnb_full = jnp.transpose(gnb_g, (1, 2, 0, 3)).reshape(n_ret, 1, 2 * D)
    c_all = c_g.reshape(NDEV, D)

    mod_cols = _ada_fwd(c_all, ada_w, name="ada_fwd")
    mod_all = _exchange([[mod_cols]], gather=True, name="gather_mod")[0]
    mod = lax.dynamic_index_in_dim(mod_all, me, axis=2, keepdims=False)
    mod = jnp.transpose(mod, (1, 0, 2)).reshape(DEPTH, 6 * D) + ada_b
    mods = [[mod[i, j * D:(j + 1) * D].reshape(1, D) for j in range(6)] for i in range(DEPTH)]
    tb = _ret_tables(S, D // RET_HEADS)

    def vec(a):
        return a.reshape(1, -1)

    def group_a(i):
        return mixer_shards(i) if i % 2 == 0 else mixer_shards(i)[:1]

    def group_b(i):
        return mlp_shards(i) if i % 2 == 0 else mixer_shards(i)[1:] + mlp_shards(i)

    mix_first = _exchange_wait(first_handle, name="gather_wait_first", after=mod)[0]
    pw2_handle, token = _exchange_start(mixer_shards(0)[1:], gather=True, name="gather_start_pw2_0", after=mix_first)
    handle_b, token = _exchange_start(mlp_shards(0), gather=True, name="gather_start_b0", after=token)
    saved = []
    weights = []
    xcur = xs
    for i in range(DEPTH):
        sh1, sc1, g1, sh2, sc2, g2 = mods[i]
        j = i // 2
        if i > 0:
            got = _exchange_wait(handle_a, name=f"gather_wait_a{i}", after=xcur)
            mix_first = got[0]
            mix_second = got[1].reshape(-1, D) if i % 2 == 0 else None
            handle_b, token = _exchange_start(group_b(i), gather=True, name=f"gather_start_b{i}", after=got[0])
        st = dict(x_in=xcur)
        norm1 = (vec(norm_mix_g[i]), sc1, sh1)
        if i % 2 == 0:
            u, h = _mm_nn(xcur, mix_first, norm=norm1, bias=vec(conv_b_pw1[j]), out_dtype=F32, name=f"pw1_fwd{i}", after=token)
            dwo = _conv_mid_fwd(u, dw3[j], conv_b_dw[j].reshape(1, CH, LANES), name=f"conv_mid_fwd{i}")
            if i == 0:
                mix_second = _exchange_wait(pw2_handle, name="gather_wait_pw2_0", after=dwo)[0].reshape(-1, D)
            xcur, y_raw = _mm_nn(dwo, mix_second, ln=(vec(conv_ln_g[j]), vec(conv_ln_b[j])), bias=vec(conv_b_pw2[j]), res=xcur,
                                 gate=g1, name=f"pw2_fwd{i}")
            st.update(u=u, dwo=dwo, y_raw=y_raw)
            got = _exchange_wait(handle_b, name=f"gather_wait_b{i}", after=xcur)
            mlp_w = mlp_weights(got)
        else:
            proj, h = _mm_nn(xcur, mix_first, norm=norm1, name=f"ret_in_fwd{i}", after=token)
            yr, yg, qr, kr = _ret_fwd(proj, tb, gng_full[j], gnb_full[j], name=f"ret_fwd{i}")
            got = _exchange_wait(handle_b, name=f"gather_wait_b{i}", after=yg)
            mix_second, mlp_w = got[0].reshape(-1, D), mlp_weights(got[1:3])
            xcur, y_raw = _mm_nn(yg, mix_second, res=xcur, gate=g1, name=f"ret_out_fwd{i}")
            st.update(proj=proj, yr=yr, yg=yg, qr=qr, kr=kr, y_raw=y_raw)
        st.update(h=h, x_mid=xcur)
        if i + 1 < DEPTH:
            handle_a, token = _exchange_start(group_a(i + 1), gather=True, name=f"gather_start_a{i + 1}", after=got[0])
        z, h2 = _mm_nn(xcur, mlp_w[0], norm=(vec(norm_mlp_g[i]), sc2, sh2), name=f"mlp1_fwd{i}", after=token)
        xcur, o_raw = _mm_nn(z, mlp_w[1], relu2=True, res=xcur, gate=g2, name=f"mlp2_fwd{i}")
        st.update(h2=h2, z=z, o_raw=o_raw)
        saved.append(st)
        weights.append((mix_first, mix_second) + mlp_w)

    g2_last = mods[DEPTH - 1][5]
    loss_local, dx, d_final_g, dy, dgate, _ = _final_loss(xcur, vec(final_norm_g), target, saved[-1]["o_raw"], g2_last,
                                                          name="final_loss")
    loss = lax.psum(loss_local[0, 0], AXES)

    dmod_rows = [None] * DEPTH
    d_mix_g, d_mlp_g = [None] * DEPTH, [None] * DEPTH
    d_pw1, d_pw2, d_win, d_wout = [None] * n_conv, [None] * n_conv, [None] * n_ret, [None] * n_ret
    d_w1, d_w2 = [None] * DEPTH, [None] * DEPTH
    d_bpw1, d_bdw, d_lng, d_lnb, d_bpw2, d_dw = ([None] * n_conv for _ in range(6))
    d_gng, d_gnb = [None] * n_ret, [None] * n_ret

    def gn_parts(d):
        return jnp.transpose(d.reshape(RET_HEADS, NDEV, -1), (1, 0, 2))

    grad_handles = [None] * DEPTH
    token = None
    for i in reversed(range(DEPTH)):
        sh1, sc1, g1, sh2, sc2, g2 = mods[i]
        j = i // 2
        st = saved[i]
        mix_a, mix_b, w1_i, w2_i = weights[i]
        do, dg2 = dy, dgate
        dz = _mm_nt(do, w2_i, z=st["z"], out_dtype=BF16, name=f"mlp2_bwd_x{i}", after=token)
        d_w2[i] = _mm_tn(st["z"], do, relu2=True, name=f"mlp2_bwd_w{i}")
        dx, dsc2, dsh2, d_mlp_g[i], dy, dg1, dby = _mm_nt(dz, w1_i, norm=(st["x_mid"], vec(norm_mlp_g[i]), sc2, dx),
                                                          gated=(st["y_raw"], g1), name=f"mlp1_bwd_x{i}")
        d_w1[i] = _mm_tn(st["h2"], dz, col_shards=NDEV, name=f"mlp1_bwd_w{i}")
        mlp_groups = [[d_w1[i]], [d_w2[i].reshape(NDEV, 4 * D // NDEV, D)]]
        token = None
        if i == 0:
            mlp0_handle, token = _exchange_start(mlp_groups, gather=False, name="grads_start_mlp0")
            mlp_groups = []
        if i % 2 == 0:
            d_bpw2[j] = dby
            ln_gb = (vec(conv_ln_g[j]), vec(conv_ln_b[j]))
            ddw, d_lng[j], d_lnb[j], d_bdw[j] = _mm_nt(dy, mix_b, lnbwd=(st["dwo"],) + ln_gb, name=f"pw2_bwd_x{i}", after=token)
            d_pw2[j] = _mm_tn(st["dwo"], dy, ln=ln_gb, name=f"pw2_bwd_w{i}")
            du, ddw_w, dbu = _conv_mid_bwd_dw(st["u"], ddw, dw3[j], name=f"conv_mid_bwd_dw{i}")
            d_dw[j], d_bpw1[j] = ddw_w.reshape(CONV_WIDTH, D), dbu.reshape(2, D)
            d_pw1[j] = _mm_tn(st["h"], du, col_shards=NDEV, name=f"pw1_bwd_w{i}")
            mix_groups = [[d_pw1[j]], [d_pw2[j].reshape(NDEV, D // NDEV, D)],
                          [jnp.transpose(d_dw[j].reshape(CONV_WIDTH, NDEV, D // NDEV), (1, 0, 2))]]
            mix_in, mix_name = du, f"pw1_bwd_x{i}"
        else:
            dyg = _mm_nt(dy, mix_b, out_dtype=BF16, name=f"ret_out_bwd_x{i}")
            d_wout[j] = _mm_tn(st["yg"], dy, name=f"ret_out_bwd_w{i}")
            dq, dgt, dyr, d_gng[j], d_gnb[j] = _ret_bwd_q(st["proj"], st["kr"], st["yr"], dyg, tb, gng_full[j], gnb_full[j],
                                                          name=f"ret_bwd_q{i}")
            dproj = _ret_bwd_kv(st["proj"], st["qr"], st["kr"], dyr, dq, dgt, tb, name=f"ret_bwd_kv{i}")
            d_win[j] = _mm_tn(st["h"], dproj, col_shards=NDEV, name=f"ret_in_bwd_w{i}")
            mix_in, mix_name = dproj, f"ret_in_bwd_x{i}"
            mix_groups = [[d_win[j]], [d_wout[j].reshape(NDEV, 2 * D // NDEV, D)], [gn_parts(d_gng[j])],
                          [gn_parts(d_gnb[j])]]
        grad_handles[i], token = _exchange_start(mix_groups + mlp_groups, gather=False, name=f"grads_start{i}")
        gated = (saved[i - 1]["o_raw"], mods[i - 1][5]) if i > 0 else None
        outs = _mm_nt(mix_in, mix_a, norm=(st["x_in"], vec(norm_mix_g[i]), sc1, dx), gated=gated, name=mix_name, after=token)
        dx, dsc1, dsh1, d_mix_g[i] = outs[:4]
        if i > 0:
            dy, dgate = outs[4], outs[5]
        dmod_rows[i] = jnp.concatenate([dsh1, dsc1, dg1, dsh2, dsc2, dg2], axis=0)
    grad_x = dx.reshape(1, S, D)

    small_local = jnp.concatenate(dmod_rows + d_mix_g + d_mlp_g + d_bpw1 + d_bdw + d_lng + d_lnb + d_bpw2 + [d_final_g],
                                  axis=0)
    small_all = _exchange([[small_local]], gather=True, name="gather_small_grads")[0]

    def pack(src):
        return jnp.concatenate([src[n].reshape(-1, D) for n in SMALL], axis=0)[None]

    sm = _adamw(pack(W), pack(Mo), pack(Vo), [small_all], name="adamw_small")
    results = {}
    row = 0
    for n in SMALL:
        cnt = W[n].size // D
        results[n] = tuple(o[0, row:row + cnt].reshape(W[n].shape) for o in sm)
        row += cnt

    ns_ada = ada_w.shape[2]
    dmod_all = small_all[:, :6 * DEPTH, :].reshape(NDEV, DEPTH, 6 * D)
    dmod_cols = jnp.transpose(lax.dynamic_slice_in_dim(dmod_all, me * ns_ada, ns_ada, axis=2), (1, 0, 2))
    g_ada = _ada_bwd(c_all, dmod_cols, name="ada_bwd")
    flat_ada = (1, DEPTH * D, ns_ada)
    ada_res = _adamw(ada_w.reshape(flat_ada), m_ada_w.reshape(flat_ada), v_ada_w.reshape(flat_ada),
                     [g_ada.reshape(flat_ada)], name="adamw_ada_w")
    results["ada_w"] = tuple(o.reshape(ada_w.shape) for o in ada_res)

    def update(names, parts):
        for n in names:
            results[n] = _adamw(W[n], Mo[n], Vo[n], parts[n], name=f"adamw_{n}")

    got = {i: _exchange_wait(grad_handles[i], name=f"grads_wait{i}", after=dx) for i in range(DEPTH - 1, 0, -1)}
    ret_layers = [i for i in range(DEPTH) if i % 2 == 1]
    update(("ret_w_in", "ret_w_out", "ret_gn_g", "ret_gn_b"),
           dict(ret_w_in=[got[i][0] for i in ret_layers], ret_w_out=[got[i][1] for i in ret_layers],
                ret_gn_g=[got[i][2] for i in ret_layers], ret_gn_b=[got[i][3] for i in ret_layers]))
    got_mlp0 = _exchange_wait(mlp0_handle, name="grads_wait_mlp0", after=results["ret_w_in"][0])
    update(("mlp_w1", "mlp_w2"),
           dict(mlp_w1=[got_mlp0[0]] + [got[i][-2] for i in range(1, DEPTH)],
                mlp_w2=[got_mlp0[1]] + [got[i][-1] for i in range(1, DEPTH)]))
    got[0] = _exchange_wait(grad_handles[0], name="grads_wait0", after=results["mlp_w1"][0])
    conv_layers = [i for i in range(DEPTH) if i % 2 == 0]
    update(("conv_w_pw1", "conv_w_pw2", "conv_w_dw"),
           dict(conv_w_pw1=[got[i][0] for i in conv_layers], conv_w_pw2=[got[i][1] for i in conv_layers],
                conv_w_dw=[got[i][2] for i in conv_layers]))

    outs = [loss, grad_x]
    for kind in range(4):
        outs += [results[n][kind] for n in WEIGHTS]
    return tuple(outs)
```

```python
import functools

import jax
import jax.numpy as jnp
import numpy as np
from jax import lax
from jax.experimental import pallas as pl
from jax.experimental.pallas import tpu as pltpu

F32, BF16 = jnp.float32, jnp.bfloat16
AXES = ("x", "y", "c")
NDEV = 8
DEPTH = 4
EPS = 1e-6
CHUNK = 64
CONV_WIDTH = 31
HALO = 32
RET_HEADS = 4
RET_BLOCK = 256
ROPE_BASE = 10000.0
LANES = 128
ADAM_LR, ADAM_B1, ADAM_B2, ADAM_EPS, ADAM_WD, ADAM_STEP = 0.001, 0.9, 0.999, 1e-08, 0.01, 10
VMEM_LIMIT = 56 * 1024 * 1024
VMEM_BLOCK_BUDGET = 44 * 1024 * 1024
MESH = pl.DeviceIdType.MESH
NT_DIMS = (((1,), (1,)), ((), ()))
TN_DIMS = (((0,), (0,)), ((), ()))


def _call(body, *, name, out_shape, in_specs, out_specs, grid=(), scratch=(), sem=None, aliases=None):
    params = dict(vmem_limit_bytes=VMEM_LIMIT)
    if sem is not None:
        params["dimension_semantics"] = sem
    return pl.pallas_call(body, name=name, grid=grid, in_specs=in_specs, out_specs=out_specs, out_shape=out_shape,
                          scratch_shapes=list(scratch), input_output_aliases=aliases or {},
                          compiler_params=pltpu.CompilerParams(**params))


def _row_tile(rows, want):
    t = min(rows, want)
    while rows % t:
        t //= 2
    return t


def _sds(shape, dtype):
    return jax.ShapeDtypeStruct(tuple(shape), dtype)


def _sigmoid(v):
    return 1.0 / (1.0 + jnp.exp(-v))


def _exchange(groups, *, gather, name):
    flat = [a for g in groups for a in g]
    n_in = len(flat)
    out_shapes = []
    for g in groups:
        s = g[0].shape if gather else g[0].shape[1:]
        lead = (NDEV,) if len(g) == 1 else (NDEV, len(g))
        out_shapes.append(_sds(lead + tuple(s), g[0].dtype))
    n_g = len(groups)

    def body(*refs):
        ins, outs = refs[:n_in], refs[n_in:n_in + n_g]
        send_sems, recv_sems, loc_sems = refs[n_in + n_g:]
        x, y, c = lax.axis_index("x"), lax.axis_index("y"), lax.axis_index("c")
        me = 4 * x + 2 * y + c
        locs, k = [], 0
        for gi, g in enumerate(groups):
            for li in range(len(g)):
                src = ins[k] if gather else ins[k].at[me]
                dst = outs[gi].at[me] if len(g) == 1 else outs[gi].at[me, li]
                cp = pltpu.make_async_copy(src, dst, loc_sems.at[k])
                cp.start()
                locs.append(cp)
                k += 1
        k0 = 0
        for gi, g in enumerate(groups):
            for r in range(1, NDEV):
                px = 1 - x if r & 4 else x
                py = 1 - y if r & 2 else y
                pc = 1 - c if r & 1 else c
                peer = 4 * px + 2 * py + pc
                for li in range(len(g)):
                    src = ins[k0 + li] if gather else ins[k0 + li].at[peer]
                    dst = outs[gi].at[me] if len(g) == 1 else outs[gi].at[me, li]
                    pltpu.make_async_remote_copy(src_ref=src, dst_ref=dst, send_sem=send_sems.at[gi * (NDEV - 1) + r - 1],
                                                 recv_sem=recv_sems.at[gi * (NDEV - 1) + r - 1], device_id=(px, py, pc),
                                                 device_id_type=MESH).start()
            k0 += len(g)
        for gi, g in enumerate(groups):
            for r in range(1, NDEV):
                px = 1 - x if r & 4 else x
                py = 1 - y if r & 2 else y
                pc = 1 - c if r & 1 else c
                peer = 4 * px + 2 * py + pc
                slab = pltpu.make_async_remote_copy(src_ref=outs[gi].at[me], dst_ref=outs[gi].at[peer],
                                                    send_sem=send_sems.at[gi * (NDEV - 1) + r - 1], recv_sem=recv_sems.at[gi * (NDEV - 1) + r - 1],
                                                    device_id=(px, py, pc), device_id_type=MESH)
                slab.wait_send()
                slab.wait_recv()
        for cp in locs:
            cp.wait()

    hbm = pl.BlockSpec(memory_space=pltpu.HBM)
    outs = _call(body, name=name, out_shape=tuple(out_shapes), in_specs=[hbm] * n_in, out_specs=tuple([hbm] * n_g),
                 scratch=[pltpu.SemaphoreType.DMA((n_g * (NDEV - 1),)), pltpu.SemaphoreType.DMA((n_g * (NDEV - 1),)),
                          pltpu.SemaphoreType.DMA((n_in,))])(*flat)
    return list(outs)


def _peer_of(x, y, c, r):
    return (1 - x if r & 4 else x, 1 - y if r & 2 else y, 1 - c if r & 1 else c)


def _exchange_start(groups, *, gather, name, after=None):
    flat = [pltpu.with_memory_space_constraint(a, pltpu.HBM) for g in groups for a in g]
    n_in, n_g = len(flat), len(groups)
    land_shapes = []
    for g in groups:
        s = g[0].shape if gather else g[0].shape[1:]
        lead = (NDEV,) if len(g) == 1 else (NDEV, len(g))
        land_shapes.append((lead + tuple(s), g[0].dtype))
    lands = [pltpu.with_memory_space_constraint(lax.empty(s, d), pltpu.HBM) for s, d in land_shapes]
    n_after = 0 if after is None else 1

    def body(*refs):
        ins, land = refs[:n_in], refs[n_in:n_in + n_g]
        send_sems, recv_sems, loc_sems = refs[n_in + n_g + n_after:n_in + n_g + n_after + 3]
        token = refs[-1]
        x, y, c = lax.axis_index("x"), lax.axis_index("y"), lax.axis_index("c")
        me = 4 * x + 2 * y + c
        k = 0
        for gi, g in enumerate(groups):
            for li in range(len(g)):
                dst = land[gi].at[me] if len(g) == 1 else land[gi].at[me, li]
                pltpu.make_async_copy(ins[k] if gather else ins[k].at[me], dst, loc_sems.at[k]).start()
                k += 1
        k0 = 0
        for gi, g in enumerate(groups):
            for r in range(1, NDEV):
                px, py, pc = _peer_of(x, y, c, r)
                peer = 4 * px + 2 * py + pc
                for li in range(len(g)):
                    dst = land[gi].at[me] if len(g) == 1 else land[gi].at[me, li]
                    pltpu.make_async_remote_copy(src_ref=ins[k0 + li] if gather else ins[k0 + li].at[peer], dst_ref=dst,
                                                 send_sem=send_sems.at[gi * (NDEV - 1) + r - 1], recv_sem=recv_sems.at[gi * (NDEV - 1) + r - 1],
                                                 device_id=(px, py, pc), device_id_type=MESH).start()
            k0 += len(g)
        token[...] = jnp.zeros_like(token)

    hbm = pl.BlockSpec(memory_space=pltpu.HBM)
    sem = pl.BlockSpec(memory_space=pltpu.SEMAPHORE)
    args = flat + lands + ([after] if n_after else [])
    outs = pl.pallas_call(body, name=name,
        out_shape=(pltpu.SemaphoreType.DMA((n_g * (NDEV - 1),)), pltpu.SemaphoreType.DMA((n_g * (NDEV - 1),)),
                   pltpu.SemaphoreType.DMA((n_in,)), *[pltpu.HBM(a.shape, a.dtype) for a in flat],
                   *[pltpu.HBM(s, d) for s, d in land_shapes], _sds((8, LANES), F32)),
        in_specs=[hbm] * (n_in + n_g) + [pl.BlockSpec(memory_space=pl.ANY)] * n_after,
        out_specs=(sem, sem, sem, *[hbm] * (n_in + n_g), pl.BlockSpec(memory_space=pltpu.VMEM)),
        input_output_aliases={k: 3 + k for k in range(n_in + n_g)},
        compiler_params=pltpu.CompilerParams(has_side_effects=pltpu.SideEffectType.DATAFLOW_SIDE_EFFECTING))(*args)
    handle = dict(sems=outs[0:3], srcs=list(outs[3:3 + n_in]), lands=list(outs[3 + n_in:3 + n_in + n_g]),
                  sizes=[len(g) for g in groups], gather=gather)
    return handle, outs[-1]


def _exchange_wait(handle, *, name, after):
    srcs, lands, sizes, gather = handle["srcs"], handle["lands"], handle["sizes"], handle["gather"]
    n_in, n_g = len(srcs), len(lands)

    def body(*refs):
        ins, land = refs[:n_in], refs[n_in:n_in + n_g]
        send_sems, recv_sems, loc_sems = refs[n_in + n_g:n_in + n_g + 3]
        x, y, c = lax.axis_index("x"), lax.axis_index("y"), lax.axis_index("c")
        me = 4 * x + 2 * y + c
        for gi in range(n_g):
            for r in range(1, NDEV):
                px, py, pc = _peer_of(x, y, c, r)
                peer = 4 * px + 2 * py + pc
                slab = pltpu.make_async_remote_copy(src_ref=land[gi].at[me], dst_ref=land[gi].at[peer],
                                                    send_sem=send_sems.at[gi * (NDEV - 1) + r - 1], recv_sem=recv_sems.at[gi * (NDEV - 1) + r - 1],
                                                    device_id=(px, py, pc), device_id_type=MESH)
                slab.wait_send()
                slab.wait_recv()
        k = 0
        for gi in range(n_g):
            for li in range(sizes[gi]):
                dst = land[gi].at[me] if sizes[gi] == 1 else land[gi].at[me, li]
                pltpu.make_async_copy(ins[k] if gather else ins[k].at[me], dst, loc_sems.at[k]).wait()
                k += 1

    hbm = pl.BlockSpec(memory_space=pltpu.HBM)
    sem = pl.BlockSpec(memory_space=pltpu.SEMAPHORE)
    outs = pl.pallas_call(body, name=name, out_shape=tuple(pltpu.HBM(a.shape, a.dtype) for a in srcs + lands),
        in_specs=[hbm] * (n_in + n_g) + [sem] * 3 + [pl.BlockSpec(memory_space=pl.ANY)],
        out_specs=tuple([hbm] * (n_in + n_g)), input_output_aliases={k: k for k in range(n_in + n_g)},
        compiler_params=pltpu.CompilerParams(has_side_effects=pltpu.SideEffectType.DATAFLOW_SIDE_EFFECTING))(
            *srcs, *lands, *handle["sems"], after)
    return list(outs[n_in:])


def _gate_part(first, dx, y_ref, g_ref, dy_ref, dg_ref, db_ref):
    @pl.when(first)
    def _():
        dg_ref[...] = jnp.zeros_like(dg_ref)
        db_ref[...] = jnp.zeros_like(db_ref)

    dy = dx * g_ref[...]
    dy_ref[...] = dy.astype(BF16)
    dg_ref[...] += jnp.sum(dx * y_ref[...].astype(F32), axis=0, keepdims=True)
    db_ref[...] += jnp.sum(dy, axis=0, keepdims=True)


def _norm_bwd_part(first, dhv, x_ref, g_ref, sc_ref, dres_ref, dx_ref, dsc_ref, dsh_ref, dg_ref):
    @pl.when(first)
    def _():
        dsc_ref[...] = jnp.zeros_like(dsc_ref)
        dsh_ref[...] = jnp.zeros_like(dsh_ref)
        dg_ref[...] = jnp.zeros_like(dg_ref)

    xv = x_ref[...]
    r = lax.rsqrt(jnp.mean(xv * xv, axis=-1, keepdims=True) + EPS)
    xhat = xv * r
    gain_v = g_ref[...]
    dsc_ref[...] += jnp.sum(dhv * (xhat * gain_v), axis=0, keepdims=True)
    dsh_ref[...] += jnp.sum(dhv, axis=0, keepdims=True)
    dxn = dhv * (1.0 + sc_ref[...])
    dg_ref[...] += jnp.sum(dxn * xhat, axis=0, keepdims=True)
    dxhat = dxn * gain_v
    dx = dres_ref[...] + r * (dxhat - xhat * jnp.mean(dxhat * xhat, axis=-1, keepdims=True))
    dx_ref[...] = dx
    return dx


def _final_loss(x, gain, target, y_prev, gate_prev, *, name):
    S, D = x.shape
    tm = _row_tile(S, 512)

    def body(x_ref, g_ref, t_ref, y_ref, gp_ref, loss_ref, dx_ref, dg_ref, dy_ref, dgp_ref, dbp_ref):
        first = pl.program_id(0) == 0

        @pl.when(first)
        def _():
            loss_ref[...] = jnp.zeros_like(loss_ref)
            dg_ref[...] = jnp.zeros_like(dg_ref)

        xv = x_ref[...]
        r = lax.rsqrt(jnp.mean(xv * xv, axis=-1, keepdims=True) + EPS)
        xhat = xv * r
        gv = g_ref[...]
        err = xhat * gv - t_ref[...]
        row_loss = jnp.mean(err * err, axis=-1, keepdims=True)
        loss_ref[...] += 0.5 * jnp.sum(row_loss, axis=0, keepdims=True)
        dy = err * (1.0 / D)
        dg_ref[...] += jnp.sum(dy * xhat, axis=0, keepdims=True)
        dxhat = dy * gv
        dx = r * (dxhat - xhat * jnp.mean(dxhat * xhat, axis=-1, keepdims=True))
        dx_ref[...] = dx
        _gate_part(first, dx, y_ref, gp_ref, dy_ref, dgp_ref, dbp_ref)

    row = pl.BlockSpec((tm, D), lambda i: (i, 0))
    vec = pl.BlockSpec((1, D), lambda i: (0, 0))
    one = pl.BlockSpec((1, 1), lambda i: (0, 0))
    vsh = _sds((1, D), F32)
    return _call(body, name=name, grid=(S // tm,), in_specs=[row, vec, row, row, vec],
                 out_specs=(one, row, vec, row, vec, vec),
                 out_shape=(_sds((1, 1), F32), _sds((S, D), F32), vsh, _sds((S, D), BF16), vsh, vsh),
                 sem=("arbitrary",))(x, gain, target, y_prev, gate_prev)


def _pick_tm(M, bytes_per_row, fixed_bytes):
    for tm in (1024, 512, 256, 128):
        if M % tm == 0 and 2 * tm * bytes_per_row + fixed_bytes <= VMEM_BLOCK_BUDGET:
            return tm
    return _row_tile(M, 128)


def _mm_nn(a, w, *, name, bias=None, relu2=False, ln=None, norm=None, res=None, gate=None, out_dtype=BF16, after=None):
    M, K = a.shape
    col = w.ndim == 3
    if col:
        nsh, ns = w.shape[0], w.shape[2]
        w_spec = pl.BlockSpec((nsh, K, ns), lambda i: (0, 0, 0))
    else:
        nsh, ns = 1, w.shape[1]
        w_spec = pl.BlockSpec((K, ns), lambda i: (0, 0))
    N = nsh * ns
    residual = res is not None
    out_bytes = (4 + 4 + 2) if residual else jnp.dtype(out_dtype).itemsize
    tm = _pick_tm(M, K * a.dtype.itemsize + N * out_bytes + (K * 2 if norm is not None else 0), 2 * K * N * 2)

    def body(*refs):
        it = iter(refs)
        a_ref, w_ref = next(it), next(it)
        b_ref = next(it) if bias is not None else None
        lg_ref, lb_ref = (next(it), next(it)) if ln is not None else (None, None)
        ng_ref, nsc_ref, nsh_ref = (next(it), next(it), next(it)) if norm is not None else (None, None, None)
        res_ref, gate_ref = (next(it), next(it)) if residual else (None, None)
        if after is not None:
            next(it)
        out_ref = next(it)
        raw_ref = next(it) if residual else None
        av = a_ref[...]
        if relu2:
            av = jnp.square(jnp.maximum(av.astype(F32), 0.0))
        if ln is not None:
            av, _ = _ln_silu(av, lg_ref[...], lb_ref[...])
        if norm is not None:
            r = lax.rsqrt(jnp.mean(av * av, axis=-1, keepdims=True) + EPS)
            av = (av * r) * ng_ref[...] * (1.0 + nsc_ref[...]) + nsh_ref[...]
        ab = av.astype(BF16)
        if norm is not None:
            next(it)[...] = ab
        for d in range(nsh):
            cols = slice(d * ns, (d + 1) * ns)
            acc = jnp.dot(ab, w_ref[d] if col else w_ref[...], preferred_element_type=F32)
            if b_ref is not None:
                acc = acc + b_ref[:, cols]
            if residual:
                raw_ref[:, cols] = acc.astype(BF16)
                out_ref[:, cols] = res_ref[:, cols] + gate_ref[:, cols] * acc
            else:
                out_ref[:, cols] = acc.astype(out_dtype)

    tile = pl.BlockSpec((tm, N), lambda i: (i, 0))
    vec = pl.BlockSpec((1, N), lambda i: (0, 0))
    in_specs, args = [pl.BlockSpec((tm, K), lambda i: (i, 0)), w_spec], [a, w]
    if bias is not None:
        in_specs.append(vec)
        args.append(bias)
    if ln is not None:
        in_specs += [pl.BlockSpec((1, K), lambda i: (0, 0))] * 2
        args += list(ln)
    if norm is not None:
        in_specs += [pl.BlockSpec((1, K), lambda i: (0, 0))] * 3
        args += list(norm)
    if residual:
        in_specs += [tile, vec]
        args += [res, gate]
        out_specs = [tile, tile]
        out_shape = [_sds((M, N), F32), _sds((M, N), BF16)]
    else:
        out_specs = [tile]
        out_shape = [_sds((M, N), out_dtype)]
    if after is not None:
        in_specs.append(pl.BlockSpec(memory_space=pl.ANY))
        args.append(after)
    if norm is not None:
        out_specs.append(pl.BlockSpec((tm, K), lambda i: (i, 0)))
        out_shape.append(_sds((M, K), BF16))
    outs = _call(body, name=name, grid=(M // tm,), in_specs=in_specs, out_specs=tuple(out_specs), out_shape=tuple(out_shape),
                 sem=("parallel",))(*args)
    return outs[0] if len(outs) == 1 else outs


def _mm_nt(g, w, *, name, z=None, out_dtype=F32, after=None, norm=None, gated=None, lnbwd=None):
    M, N = g.shape
    col = w.ndim == 3
    if col:
        nsh, K, ns = w.shape
        w_spec = pl.BlockSpec((nsh, K, ns), lambda i: (0, 0, 0))
    else:
        K = w.shape[0]
        w_spec = pl.BlockSpec((K, N), lambda i: (0, 0))
    assert norm is None or col
    kc = min(K, 1024)
    assert lnbwd is None or (not col and kc == K)
    obytes = jnp.dtype(out_dtype).itemsize
    row_bytes = N * g.dtype.itemsize + K * obytes + (K * 2 if z is not None else 0)
    if norm is not None:
        row_bytes += 2 * K * 4 + (K * 4 if gated is not None else 0)
    if lnbwd is not None:
        row_bytes += K * 4
    tm = _pick_tm(M, row_bytes, 2 * K * N * 2 + 512 * K * 4)

    def body(*refs):
        it = iter(refs)
        g_ref, w_ref = next(it), next(it)
        z_ref = next(it) if z is not None else None
        norm_in = [next(it) for _ in range(4)] if norm is not None else None
        gate_in = [next(it) for _ in range(2)] if gated is not None else None
        ln_in = [next(it) for _ in range(3)] if lnbwd is not None else None
        if after is not None:
            next(it)
        out_ref = next(it)
        if col:
            acc = None
            for d in range(nsh):
                part = lax.dot_general(g_ref[:, d * ns:(d + 1) * ns].astype(BF16), w_ref[d], NT_DIMS,
                                       preferred_element_type=F32)
                acc = part if acc is None else acc + part
            if norm is None:
                out_ref[...] = acc.astype(out_dtype)
            else:
                first = pl.program_id(0) == 0
                dx = _norm_bwd_part(first, acc, *norm_in, out_ref, next(it), next(it), next(it))
                if gated is not None:
                    _gate_part(first, dx, *gate_in, next(it), next(it), next(it))
        else:
            gb = g_ref[...].astype(BF16)
            for cki in range(K // kc):
                cols = slice(cki * kc, (cki + 1) * kc)
                part = lax.dot_general(gb, w_ref[cols, :], NT_DIMS, preferred_element_type=F32)
                if z_ref is not None:
                    part = part * (2.0 * jnp.maximum(z_ref[:, cols].astype(F32), 0.0))
                if lnbwd is not None:
                    _ln_silu_bwd_part(pl.program_id(0) == 0, part, *ln_in, out_ref, next(it), next(it), next(it))
                else:
                    out_ref[:, cols] = part.astype(out_dtype)

    row = pl.BlockSpec((tm, K), lambda i: (i, 0))
    vec = pl.BlockSpec((1, K), lambda i: (0, 0))
    vsh = _sds((1, K), F32)
    in_specs, args = [pl.BlockSpec((tm, N), lambda i: (i, 0)), w_spec], [g, w]
    out_specs, out_shape = [row], [_sds((M, K), out_dtype)]
    if z is not None:
        in_specs.append(row)
        args.append(z)
    if norm is not None:
        x, gain, sc, dres = norm
        in_specs += [row, vec, vec, row]
        args += [x, gain, sc, dres]
        out_specs += [vec, vec, vec]
        out_shape += [vsh, vsh, vsh]
    if gated is not None:
        in_specs += [row, vec]
        args += list(gated)
        out_specs += [row, vec, vec]
        out_shape += [_sds((M, K), BF16), vsh, vsh]
    if lnbwd is not None:
        in_specs += [row, vec, vec]
        args += list(lnbwd)
        out_specs += [vec, vec, vec]
        out_shape += [vsh, vsh, vsh]
    if after is not None:
        in_specs.append(pl.BlockSpec(memory_space=pl.ANY))
        args.append(after)
    outs = _call(body, name=name, grid=(M // tm,), in_specs=in_specs, out_specs=tuple(out_specs), out_shape=tuple(out_shape),
                 sem=("parallel",) if norm is None and lnbwd is None else ("arbitrary",))(*args)
    return outs[0] if len(outs) == 1 else outs


def _mm_tn(a, g, *, name, col_shards=None, relu2=False, ln=None):
    M, K = a.shape
    N = g.shape[1]
    acc_budget = 8 * 1024 * 1024
    if col_shards:
        ns = N // col_shards
        spc = col_shards
        while spc > 1 and K * ns * spc * 4 > acc_budget:
            spc //= 2
        step_cols = K * a.dtype.itemsize + spc * ns * g.dtype.itemsize
    else:
        tk = K
        while tk > 128 and tk * N * 4 > acc_budget:
            tk //= 2
        step_cols = tk * a.dtype.itemsize + N * g.dtype.itemsize
    tm = _row_tile(M, 2048)
    while tm > 256 and 2 * tm * step_cols + 2 * acc_budget > VMEM_BLOCK_BUDGET:
        tm //= 2
    nm = M // tm
    if col_shards:
        grid = (col_shards // spc, nm)
        a_spec = pl.BlockSpec((tm, K), lambda c, m: (m, 0))
        g_spec = pl.BlockSpec((tm, spc * ns), lambda c, m: (m, c))
        out_spec = pl.BlockSpec((spc, K, ns), lambda c, m: (c, 0, 0))
        out_shape = _sds((col_shards, K, ns), BF16)
        acc_shape = (K, spc * ns)
    else:
        grid = (K // tk, nm)
        a_spec = pl.BlockSpec((tm, tk), lambda c, m: (m, c))
        g_spec = pl.BlockSpec((tm, N), lambda c, m: (m, 0))
        out_spec = pl.BlockSpec((tk, N), lambda c, m: (c, 0))
        out_shape = _sds((K, N), BF16)
        acc_shape = (tk, N)
        assert ln is None or tk == K
    in_specs, args = [a_spec, g_spec], [a, g]
    if ln is not None:
        in_specs += [pl.BlockSpec((1, K), lambda c, m: (0, 0))] * 2
        args += list(ln)

    def body(a_ref, g_ref, *rest):
        out_ref, acc_ref = rest[-2:]
        m = pl.program_id(1)

        @pl.when(m == 0)
        def _():
            acc_ref[...] = jnp.zeros_like(acc_ref)

        av = a_ref[...]
        if relu2:
            av = jnp.square(jnp.maximum(av.astype(F32), 0.0))
        if ln is not None:
            av, _ = _ln_silu(av, rest[0][...], rest[1][...])
        acc_ref[...] += lax.dot_general(av.astype(BF16), g_ref[...].astype(BF16), TN_DIMS, preferred_element_type=F32)

        @pl.when(m == nm - 1)
        def _():
            if col_shards:
                for s in range(spc):
                    out_ref[s] = acc_ref[:, s * ns:(s + 1) * ns].astype(BF16)
            else:
                out_ref[...] = acc_ref[...].astype(BF16)

    return _call(body, name=name, grid=grid, in_specs=in_specs, out_specs=out_spec, out_shape=out_shape,
                 scratch=[pltpu.VMEM(acc_shape, F32)], sem=("parallel", "arbitrary"))(*args)


CONV_TILE = 512


def _glu_rows(u2, ch):
    d = u2.shape[1] // 2
    return (u2[:, :d] * _sigmoid(u2[:, d:])).reshape(u2.shape[0], ch, LANES)


def _fill_glu(buf, u_ref, uh_ref, ch, tile):
    first = pl.program_id(0) == 0
    buf[0:HALO] = jnp.where(first, 0.0, _glu_rows(uh_ref[...], ch))
    buf[HALO:HALO + tile] = _glu_rows(u_ref[...], ch)


CONV_SUB = 4


def _conv_specs(S, D, tile):
    per = tile // HALO
    u_spec = pl.BlockSpec((tile, 2 * D), lambda i: (i, 0))
    uh_spec = pl.BlockSpec((HALO, 2 * D), lambda i: (jnp.maximum(i * per - 1, 0), 0))
    x_spec = pl.BlockSpec((tile, D), lambda i: (i, 0))
    xn_spec = pl.BlockSpec((HALO, D), lambda i: (jnp.minimum((i + 1) * per, S // HALO - 1), 0))
    w_spec = pl.BlockSpec((CONV_WIDTH, D // LANES, LANES), lambda i: (0, 0, 0))
    v_spec = pl.BlockSpec((1, D // LANES, LANES), lambda i: (0, 0, 0))
    return u_spec, uh_spec, x_spec, xn_spec, w_spec, v_spec


def _conv_mid_fwd(u, w3, bdw3, *, name):
    S, D = u.shape[0], u.shape[1] // 2
    ch = D // LANES
    tile = _row_tile(S, CONV_TILE)
    sub = _row_tile(tile, 2 * CONV_SUB)
    half = (CONV_WIDTH + 1) // 2
    u_spec, uh_spec, x_spec, _, w_spec, v_spec = _conv_specs(S, D, tile)

    def body(u_ref, uh_ref, w_ref, b_ref, o_ref, buf, stage):
        _fill_glu(buf, u_ref, uh_ref, ch, tile)

        def taps(lo, hi, start):
            def step(q, carry):
                rows = pl.ds(q * sub, sub)
                acc = [b_ref[...] if start else stage[rows], None]
                for k in range(lo, hi):
                    term = buf[pl.ds(q * sub + (HALO - CONV_WIDTH + 1 + k), sub)] * w_ref[k]
                    acc[k % 2] = term if acc[k % 2] is None else acc[k % 2] + term
                stage[rows] = acc[0] + acc[1]
                return carry

            lax.fori_loop(0, tile // sub, step, 0)

        taps(0, half, True)
        taps(half, CONV_WIDTH, False)
        o_ref[...] = stage[...].reshape(tile, D)

    return _call(body, name=name, grid=(S // tile,), in_specs=[u_spec, uh_spec, w_spec, v_spec], out_specs=x_spec,
                 out_shape=_sds((S, D), F32),
                 scratch=[pltpu.VMEM((tile + HALO, ch, LANES), F32), pltpu.VMEM((tile, ch, LANES), F32)],
                 sem=("parallel",))(u, u, w3, bdw3)


def _ln_silu(v, gv, bv):
    mu = jnp.mean(v, axis=-1, keepdims=True)
    cen = v - mu
    rstd = lax.rsqrt(jnp.mean(cen * cen, axis=-1, keepdims=True) + EPS)
    nrm = cen * rstd
    ln = nrm * gv + bv
    sg = _sigmoid(ln)
    return ln * sg, (nrm, rstd, ln, sg)


def _ln_silu_bwd_part(first, ds, v_ref, g_ref, b_ref, ddw_ref, dg_ref, db_ref, dbdw_ref):
    @pl.when(first)
    def _():
        dg_ref[...] = jnp.zeros_like(dg_ref)
        db_ref[...] = jnp.zeros_like(db_ref)
        dbdw_ref[...] = jnp.zeros_like(dbdw_ref)

    gv = g_ref[...]
    _, (nrm, rstd, ln, sg) = _ln_silu(v_ref[...], gv, b_ref[...])
    dln = ds * (sg * (1.0 + ln * (1.0 - sg)))
    dg_ref[...] += jnp.sum(dln * nrm, axis=0, keepdims=True)
    db_ref[...] += jnp.sum(dln, axis=0, keepdims=True)
    dn = dln * gv
    ddw = rstd * (dn - jnp.mean(dn, axis=-1, keepdims=True) - nrm * jnp.mean(dn * nrm, axis=-1, keepdims=True))
    dbdw_ref[...] += jnp.sum(ddw, axis=0, keepdims=True)
    ddw_ref[...] = ddw


def _conv_mid_bwd_dw(u, ddw, w3, *, name):
    S, D = ddw.shape
    ch = D // LANES
    tile = _row_tile(S, CONV_TILE)
    sub = _row_tile(tile, 2 * CONV_SUB)
    last = S // tile - 1
    u_spec, uh_spec, x_spec, xn_spec, w_spec, _ = _conv_specs(S, D, tile)
    b_spec = pl.BlockSpec((1, 2 * D), lambda i: (0, 0))

    def body(u_ref, uh_ref, d_ref, dn_ref, w_ref, du_ref, dw_ref, db_ref, gbuf, dbuf, stage):
        @pl.when(pl.program_id(0) == 0)
        def _():
            dw_ref[...] = jnp.zeros_like(dw_ref)
            db_ref[...] = jnp.zeros_like(db_ref)

        _fill_glu(gbuf, u_ref, uh_ref, ch, tile)
        dbuf[0:tile] = d_ref[...].reshape(tile, ch, LANES)
        dbuf[tile:tile + HALO] = jnp.where(pl.program_id(0) == last, 0.0, dn_ref[...].reshape(HALO, ch, LANES))

        def taps(lo, hi, start):
            def step(q, c):
                s0 = q * sub
                ddw_q = dbuf[pl.ds(s0, sub)]
                acc = [None if start else stage[pl.ds(s0, sub)], None]
                for k in range(lo, hi):
                    term = dbuf[pl.ds(s0 + (CONV_WIDTH - 1 - k), sub)] * w_ref[k]
                    acc[k % 2] = term if acc[k % 2] is None else acc[k % 2] + term
                    dw_ref[k] += jnp.sum(ddw_q * gbuf[pl.ds(s0 + (HALO - CONV_WIDTH + 1 + k), sub)], axis=0)
                stage[pl.ds(s0, sub)] = acc[0] + acc[1]
                return c

            lax.fori_loop(0, tile // sub, step, 0)

        half = (CONV_WIDTH + 1) // 2
        taps(0, half, True)
        taps(half, CONV_WIDTH, False)
        dglu = stage[...].reshape(tile, D)
        uv = u_ref[...]
        av, sg = uv[:, :D], _sigmoid(uv[:, D:])
        da = dglu * sg
        dg = da * av * (1.0 - sg)
        du_ref[:, 0:D] = da
        du_ref[:, D:2 * D] = dg
        db_ref[:, 0:D] += jnp.sum(da, axis=0, keepdims=True)
        db_ref[:, D:2 * D] += jnp.sum(dg, axis=0, keepdims=True)

    return _call(body, name=name, grid=(S // tile,), in_specs=[u_spec, uh_spec, x_spec, xn_spec, w_spec],
                 out_specs=(u_spec, w_spec, b_spec),
                 out_shape=(_sds((S, 2 * D), F32), _sds((CONV_WIDTH, ch, LANES), F32), _sds((1, 2 * D), F32)),
                 scratch=[pltpu.VMEM((tile + HALO, ch, LANES), F32), pltpu.VMEM((tile + HALO, ch, LANES), F32),
                          pltpu.VMEM((tile, ch, LANES), F32)],
                 sem=("arbitrary",))(u, u, ddw, ddw, w3)


def _ret_tables(S, dk):
    f32 = np.float32
    B = min(RET_BLOCK, S)
    lg = np.log(f32(1.0) - f32(2.0) ** (f32(-5.0) - np.arange(RET_HEADS, dtype=f32)))
    idx = np.arange(B, dtype=f32)
    diff = idx[:, None] - idx[None, :]
    cq, ck = (np.arange(B) // CHUNK)[:, None], (np.arange(B) // CHUNK)[None, :]
    dist = np.where(cq == ck, np.abs(diff), diff)
    mask = np.where(ck <= cq, np.exp(lg[:, None, None] * dist[None]), f32(0.0)).astype(f32)
    xi = np.exp(lg[:, None] * (idx + f32(1.0)))[..., None].astype(f32)
    zeta = np.exp(lg[:, None] * (f32(B - 1.0) - idx))[..., None].astype(f32)
    gam = np.broadcast_to(np.exp(lg * f32(B))[:, None, None], (RET_HEADS, 8, LANES)).astype(f32)
    pos = np.arange(S, dtype=f32)
    inv = (f32(ROPE_BASE) ** (-np.arange(0, dk, 2, dtype=f32) / f32(dk))).astype(f32)
    ang = (pos[:, None] * inv[None, :]).astype(f32)
    tb = dict(mask=mask, xi=xi, zeta=zeta, gam=gam, cos=np.cos(ang).astype(f32), sin=np.sin(ang).astype(f32))
    return dict(B=B, **{k: jnp.asarray(v) for k, v in tb.items()})


def _rope(v, cs, sn):
    half = v.shape[1] // 2
    v1, v2 = v[:, :half], v[:, half:]
    return jnp.concatenate([v1 * cs - v2 * sn, v2 * cs + v1 * sn], axis=-1)


def _rope_t(d, cs, sn):
    half = d.shape[1] // 2
    d1, d2 = d[:, :half], d[:, half:]
    return jnp.concatenate([d1 * cs + d2 * sn, d2 * cs - d1 * sn], axis=-1)


def _dot(a, b):
    return jnp.dot(a.astype(BF16), b.astype(BF16), preferred_element_type=F32)


def _dot_nt(a, b):
    return lax.dot_general(a.astype(BF16), b.astype(BF16), NT_DIMS, preferred_element_type=F32)


def _dot_tn(a, b):
    return lax.dot_general(a.astype(BF16), b.astype(BF16), TN_DIMS, preferred_element_type=F32)


def _ret_specs(S, D, B, RB, reverse):
    dk, dv = D // RET_HEADS, 2 * D // RET_HEADS
    nb = S // RB
    blk = (lambda ib: nb - 1 - ib) if reverse else (lambda ib: ib)
    q = pl.BlockSpec((RB, dk), lambda h, ib: (blk(ib), h))
    k = pl.BlockSpec((RB, dk), lambda h, ib: (blk(ib), RET_HEADS + h))
    v = pl.BlockSpec((RB, dv), lambda h, ib: (blk(ib), RET_HEADS + h))
    gate = pl.BlockSpec((RB, dv), lambda h, ib: (blk(ib), 2 * RET_HEADS + h))
    yv = pl.BlockSpec((RB, dv), lambda h, ib: (blk(ib), h))
    rope = pl.BlockSpec((RB, dk // 2), lambda h, ib: (blk(ib), 0))
    mask = pl.BlockSpec((None, B, B), lambda h, ib: (h, 0, 0))
    dec = pl.BlockSpec((None, B, 1), lambda h, ib: (h, 0, 0))
    gam = pl.BlockSpec((None, 8, LANES), lambda h, ib: (h, 0, 0))
    gn = pl.BlockSpec((1, dv), lambda h, ib: (0, h))
    return dict(q=q, k=k, v=v, gate=gate, yv=yv, rope=rope, mask=mask, dec=dec, gam=gam, gn=gn)


def _group_norm(yr, gv, bv):
    mu = jnp.mean(yr, axis=-1, keepdims=True)
    cen = yr - mu
    rstd = lax.rsqrt(jnp.mean(cen * cen, axis=-1, keepdims=True) + EPS)
    nrm = cen * rstd
    return nrm, rstd, nrm * gv + bv


def _ret_fwd(proj, tb, gng, gnb, *, name):
    S, D = proj.shape[0], proj.shape[1] // 6
    dk, dv = D // RET_HEADS, 2 * D // RET_HEADS
    B = tb["B"]
    RB = _row_tile(S, 8 * B)
    nsub = RB // B
    sp = _ret_specs(S, D, B, RB, False)
    scale = dk ** -0.5

    def body(q_ref, k_ref, v_ref, gt_ref, cos_ref, sin_ref, mask_ref, xi_ref, zeta_ref, gam_ref, gng_ref, gnb_ref,
             yr_ref, yg_ref, qr_ref, kr_ref, state):
        @pl.when(pl.program_id(1) == 0)
        def _():
            state[...] = jnp.zeros_like(state)

        for sb in range(nsub):
            rows = slice(sb * B, (sb + 1) * B)
            cs, sn = cos_ref[rows, :], sin_ref[rows, :]
            q = _rope(q_ref[rows, :].astype(F32), cs, sn)
            k = _rope(k_ref[rows, :].astype(F32), cs, sn) * scale
            qr_ref[rows, :] = q.astype(BF16)
            kr_ref[rows, :] = k.astype(BF16)
            vb = v_ref[rows, :]
            p = _dot_nt(q, k) * mask_ref[...]
            st = state[...]
            yr = _dot(p, vb) + _dot(q * xi_ref[...], st)
            state[...] = st * gam_ref[0:1, 0:1] + _dot_tn(k * zeta_ref[...], vb)
            _, _, gn = _group_norm(yr, gng_ref[...], gnb_ref[...])
            gt = gt_ref[rows, :].astype(F32)
            yr_ref[rows, :] = yr.astype(BF16)
            yg_ref[rows, :] = (gt * _sigmoid(gt) * gn).astype(BF16)

    return _call(body, name=name, grid=(RET_HEADS, S // RB),
                 in_specs=[sp["q"], sp["k"], sp["v"], sp["gate"], sp["rope"], sp["rope"], sp["mask"], sp["dec"], sp["dec"],
                           sp["gam"], sp["gn"], sp["gn"]],
                 out_specs=(sp["yv"], sp["yv"], sp["q"], sp["q"]),
                 out_shape=(_sds((S, 2 * D), BF16), _sds((S, 2 * D), BF16), _sds((S, D), BF16), _sds((S, D), BF16)),
                 scratch=[pltpu.VMEM((dk, dv), F32)], sem=("parallel", "arbitrary"))(
                     proj, proj, proj, proj, tb["cos"], tb["sin"], tb["mask"], tb["xi"], tb["zeta"], tb["gam"], gng, gnb)


def _ret_bwd_q(proj, kr, yr, dyg, tb, gng, gnb, *, name):
    S, D = proj.shape[0], proj.shape[1] // 6
    dk, dv = D // RET_HEADS, 2 * D // RET_HEADS
    B = tb["B"]
    RB = _row_tile(S, 8 * B)
    nsub = RB // B
    sp = _ret_specs(S, D, B, RB, False)

    def body(k_ref, v_ref, gt_ref, yr_ref, dyg_ref, cos_ref, sin_ref, mask_ref, xi_ref, zeta_ref, gam_ref,
             gng_ref, gnb_ref, dq_ref, dgt_ref, dyr_ref, dgg_ref, dgb_ref, state):
        @pl.when(pl.program_id(1) == 0)
        def _():
            state[...] = jnp.zeros_like(state)
            dgg_ref[...] = jnp.zeros_like(dgg_ref)
            dgb_ref[...] = jnp.zeros_like(dgb_ref)

        for sb in range(nsub):
            rows = slice(sb * B, (sb + 1) * B)
            cs, sn = cos_ref[rows, :], sin_ref[rows, :]
            k = k_ref[rows, :]
            vb = v_ref[rows, :]
            gv = gng_ref[...]
            nrm, rstd, gn = _group_norm(yr_ref[rows, :].astype(F32), gv, gnb_ref[...])
            gt = gt_ref[rows, :].astype(F32)
            sg = _sigmoid(gt)
            dyg = dyg_ref[rows, :].astype(F32)
            dgt_ref[rows, :] = (dyg * gn * (sg * (1.0 + gt * (1.0 - sg)))).astype(BF16)
            dgn = dyg * (gt * sg)
            dgg_ref[...] += jnp.sum(dgn * nrm, axis=0, keepdims=True)
            dgb_ref[...] += jnp.sum(dgn, axis=0, keepdims=True)
            dn = dgn * gv
            dyr = rstd * (dn - jnp.mean(dn, axis=-1, keepdims=True) - nrm * jnp.mean(dn * nrm, axis=-1, keepdims=True))
            dyr_ref[rows, :] = dyr.astype(BF16)
            dp = _dot_nt(dyr, vb) * mask_ref[...]
            st = state[...]
            dq = _dot(dp, k) + _dot_nt(dyr, st) * xi_ref[...]
            dq_ref[rows, :] = _rope_t(dq, cs, sn).astype(BF16)
            state[...] = st * gam_ref[0:1, 0:1] + _dot_tn(k.astype(F32) * zeta_ref[...], vb)

    return _call(body, name=name, grid=(RET_HEADS, S // RB),
                 in_specs=[sp["q"], sp["v"], sp["gate"], sp["yv"], sp["yv"], sp["rope"], sp["rope"], sp["mask"],
                           sp["dec"], sp["dec"], sp["gam"], sp["gn"], sp["gn"]],
                 out_specs=(sp["q"], sp["yv"], sp["yv"], sp["gn"], sp["gn"]),
                 out_shape=(_sds((S, D), BF16), _sds((S, 2 * D), BF16), _sds((S, 2 * D), BF16), _sds((1, 2 * D), F32),
                            _sds((1, 2 * D), F32)),
                 scratch=[pltpu.VMEM((dk, dv), F32)], sem=("parallel", "arbitrary"))(
                     kr, proj, proj, yr, dyg, tb["cos"], tb["sin"], tb["mask"], tb["xi"], tb["zeta"], tb["gam"], gng, gnb)


def _ret_bwd_kv(proj, qr, kr, dyr, dq, dgt, tb, *, name):
    S, D = proj.shape[0], proj.shape[1] // 6
    dk, dv = D // RET_HEADS, 2 * D // RET_HEADS
    B = tb["B"]
    RB = _row_tile(S, 2 * B)
    nsub = RB // B
    nb = S // RB
    scale = dk ** -0.5

    def body(v_ref, qr_ref, kr_ref, dyr_ref, dq_ref, dgt_ref, cos_ref, sin_ref, mask_ref, xi_ref, zeta_ref, gam_ref, out_ref,
             dstate):
        @pl.when(pl.program_id(0) == 0)
        def _():
            dstate[...] = jnp.zeros_like(dstate)

        out_ref[:, 0:D] = dq_ref[...]
        out_ref[:, 4 * D:6 * D] = dgt_ref[...]
        for sb in reversed(range(nsub)):
            rows = slice(sb * B, (sb + 1) * B)
            cs, sn = cos_ref[rows, :], sin_ref[rows, :]
            for h in range(RET_HEADS):
                kcols = slice(D + h * dk, D + (h + 1) * dk)
                vcols = slice(2 * D + h * dv, 2 * D + (h + 1) * dv)
                q = qr_ref[rows, h * dk:(h + 1) * dk]
                k = kr_ref[rows, h * dk:(h + 1) * dk]
                vb = v_ref[rows, h * dv:(h + 1) * dv]
                dyr_h = dyr_ref[rows, h * dv:(h + 1) * dv]
                mk = mask_ref[h]
                p = _dot_nt(q, k) * mk
                dp = _dot_nt(dyr_h, vb) * mk
                ds = dstate[h]
                zt = zeta_ref[h]
                dkr = _dot_tn(dp, q) + _dot_nt(vb, ds) * zt
                out_ref[rows, kcols] = _rope_t(dkr * scale, cs, sn).astype(BF16)
                out_ref[rows, vcols] = (_dot_tn(p, dyr_h) + _dot(k.astype(F32) * zt, ds)).astype(BF16)
                dstate[h] = ds * gam_ref[h, 0:1, 0:1] + _dot_tn(q.astype(F32) * xi_ref[h], dyr_h)

    def rev(width):
        return pl.BlockSpec((RB, width), lambda ib: (nb - 1 - ib, 0))

    def whole(a):
        return pl.BlockSpec(a.shape, lambda ib: (0,) * a.ndim)

    return _call(body, name=name, grid=(nb,),
                 in_specs=[pl.BlockSpec((RB, 2 * D), lambda ib: (nb - 1 - ib, 1)), rev(D), rev(D), rev(2 * D), rev(D), rev(2 * D),
                           rev(dk // 2), rev(dk // 2), whole(tb["mask"]), whole(tb["xi"]), whole(tb["zeta"]), whole(tb["gam"])],
                 out_specs=rev(6 * D), out_shape=_sds((S, 6 * D), BF16), scratch=[pltpu.VMEM((RET_HEADS, dk, dv), F32)],
                 sem=("arbitrary",))(proj, qr, kr, dyr, dq, dgt, tb["cos"], tb["sin"], tb["mask"], tb["xi"], tb["zeta"],
                                     tb["gam"])


def _ada_fwd(c_all, ada_w, *, name):
    L, D, ns = ada_w.shape

    def body(c_ref, w_ref, out_ref):
        cv = c_ref[...]
        cond = cv * _sigmoid(cv)
        out_ref[...] = jnp.dot(cond.astype(BF16), w_ref[...].astype(BF16), preferred_element_type=F32)

    return _call(body, name=name, grid=(L,), in_specs=[pl.BlockSpec((NDEV, D), lambda l: (0, 0)),
                                                      pl.BlockSpec((None, D, ns), lambda l: (l, 0, 0))],
                 out_specs=pl.BlockSpec((None, NDEV, ns), lambda l: (l, 0, 0)), out_shape=_sds((L, NDEV, ns), F32),
                 sem=("parallel",))(c_all, ada_w)


def _ada_bwd(c_all, dmod_cols, *, name):
    L, _, ns = dmod_cols.shape
    D = c_all.shape[1]

    def body(c_ref, d_ref, out_ref):
        cv = c_ref[...]
        cond = cv * _sigmoid(cv)
        out_ref[...] = lax.dot_general(cond.astype(BF16), d_ref[...].astype(BF16), TN_DIMS, preferred_element_type=F32)

    return _call(body, name=name, grid=(L,), in_specs=[pl.BlockSpec((NDEV, D), lambda l: (0, 0)),
                                                      pl.BlockSpec((None, NDEV, ns), lambda l: (l, 0, 0))],
                 out_specs=pl.BlockSpec((None, D, ns), lambda l: (l, 0, 0)), out_shape=_sds((L, D, ns), F32),
                 sem=("parallel",))(c_all, dmod_cols)


def _adamw(w, m, v, parts, *, name):
    shape = w.shape
    L, cols = len(parts), shape[-1]
    rows = w.size // (cols * L)
    n = parts[0].shape[0]
    tr = rows
    for cand in (256, 128, 64, 32, 16, 8):
        if rows % cand == 0:
            tr = cand
            break
    c1 = 1.0 - ADAM_B1 ** ADAM_STEP
    c2 = 1.0 - ADAM_B2 ** ADAM_STEP

    def body(w_ref, m_ref, v_ref, *rest):
        p_refs = rest[:L]
        g_ref, d_ref, m2_ref, v2_ref = rest[L:]
        layer = pl.program_id(0)
        for l in range(L):
            @pl.when(layer == l)
            def _(p_ref=p_refs[l]):
                g = p_ref[0].astype(F32)
                for i in range(1, n):
                    g = g + p_ref[i].astype(F32)
                m2 = ADAM_B1 * m_ref[...] + (1.0 - ADAM_B1) * g
                v2 = ADAM_B2 * v_ref[...] + (1.0 - ADAM_B2) * (g * g)
                g_ref[...] = g
                m2_ref[...] = m2
                v2_ref[...] = v2
                d_ref[...] = -ADAM_LR * ((m2 / c1) / (jnp.sqrt(v2 / c2) + ADAM_EPS) + ADAM_WD * w_ref[...])

    mat = pl.BlockSpec((None, tr, cols), lambda l, i: (l, i, 0))

    def part_spec(k):
        return pl.BlockSpec((n, tr, cols), lambda l, i: (0, jnp.where(l == k, i, 0), 0))

    outs = _call(body, name=name, grid=(L, rows // tr), in_specs=[mat, mat, mat] + [part_spec(k) for k in range(L)],
                 out_specs=(mat, mat, mat, mat), out_shape=tuple(_sds((L, rows, cols), F32) for _ in range(4)),
                 sem=("parallel", "parallel"))(w.reshape(L, rows, cols), m.reshape(L, rows, cols), v.reshape(L, rows, cols),
                                               *[p.reshape(n, rows, cols) for p in parts])
    return tuple(o.reshape(shape) for o in outs)


SMALL = ("ada_b", "norm_mix_g", "norm_mlp_g", "conv_b_pw1", "conv_b_dw", "conv_ln_g", "conv_ln_b", "conv_b_pw2",
         "final_norm_g")
WEIGHTS = ("ada_w", "ada_b", "norm_mix_g", "norm_mlp_g", "conv_w_pw1", "conv_b_pw1", "conv_w_dw", "conv_b_dw", "conv_ln_g",
           "conv_ln_b", "conv_w_pw2", "conv_b_pw2", "ret_w_in", "ret_gn_g", "ret_gn_b", "ret_w_out", "mlp_w1", "mlp_w2",
           "final_norm_g")


def kernel(x, c, ada_w, ada_b, norm_mix_g, norm_mlp_g, conv_w_pw1, conv_b_pw1, conv_w_dw, conv_b_dw, conv_ln_g, conv_ln_b, conv_w_pw2, conv_b_pw2, ret_w_in, ret_gn_g, ret_gn_b, ret_w_out, mlp_w1, mlp_w2, final_norm_g, loss_target, m_ada_w, m_ada_b, m_norm_mix_g, m_norm_mlp_g, m_conv_w_pw1, m_conv_b_pw1, m_conv_w_dw, m_conv_b_dw, m_conv_ln_g, m_conv_ln_b, m_conv_w_pw2, m_conv_b_pw2, m_ret_w_in, m_ret_gn_g, m_ret_gn_b, m_ret_w_out, m_mlp_w1, m_mlp_w2, m_final_norm_g, v_ada_w, v_ada_b, v_norm_mix_g, v_norm_mlp_g, v_conv_w_pw1, v_conv_b_pw1, v_conv_w_dw, v_conv_b_dw, v_conv_ln_g, v_conv_ln_b, v_conv_w_pw2, v_conv_b_pw2, v_ret_w_in, v_ret_gn_g, v_ret_gn_b, v_ret_w_out, v_mlp_w1, v_mlp_w2, v_final_norm_g):
    W = dict(ada_w=ada_w, ada_b=ada_b, norm_mix_g=norm_mix_g, norm_mlp_g=norm_mlp_g, conv_w_pw1=conv_w_pw1,
             conv_b_pw1=conv_b_pw1, conv_w_dw=conv_w_dw, conv_b_dw=conv_b_dw, conv_ln_g=conv_ln_g, conv_ln_b=conv_ln_b,
             conv_w_pw2=conv_w_pw2, conv_b_pw2=conv_b_pw2, ret_w_in=ret_w_in, ret_gn_g=ret_gn_g, ret_gn_b=ret_gn_b,
             ret_w_out=ret_w_out, mlp_w1=mlp_w1, mlp_w2=mlp_w2, final_norm_g=final_norm_g)
    Mo = dict(ada_w=m_ada_w, ada_b=m_ada_b, norm_mix_g=m_norm_mix_g, norm_mlp_g=m_norm_mlp_g, conv_w_pw1=m_conv_w_pw1,
              conv_b_pw1=m_conv_b_pw1, conv_w_dw=m_conv_w_dw, conv_b_dw=m_conv_b_dw, conv_ln_g=m_conv_ln_g,
              conv_ln_b=m_conv_ln_b, conv_w_pw2=m_conv_w_pw2, conv_b_pw2=m_conv_b_pw2, ret_w_in=m_ret_w_in,
              ret_gn_g=m_ret_gn_g, ret_gn_b=m_ret_gn_b, ret_w_out=m_ret_w_out, mlp_w1=m_mlp_w1, mlp_w2=m_mlp_w2,
              final_norm_g=m_final_norm_g)
    Vo = dict(ada_w=v_ada_w, ada_b=v_ada_b, norm_mix_g=v_norm_mix_g, norm_mlp_g=v_norm_mlp_g, conv_w_pw1=v_conv_w_pw1,
              conv_b_pw1=v_conv_b_pw1, conv_w_dw=v_conv_w_dw, conv_b_dw=v_conv_b_dw, conv_ln_g=v_conv_ln_g,
              conv_ln_b=v_conv_ln_b, conv_w_pw2=v_conv_w_pw2, conv_b_pw2=v_conv_b_pw2, ret_w_in=v_ret_w_in,
              ret_gn_g=v_ret_gn_g, ret_gn_b=v_ret_gn_b, ret_w_out=v_ret_w_out, mlp_w1=v_mlp_w1, mlp_w2=v_mlp_w2,
              final_norm_g=v_final_norm_g)

    S, D = x.shape[1], x.shape[2]
    CH = D // LANES
    n_conv, n_ret = conv_w_pw1.shape[0], ret_w_in.shape[0]
    me = 4 * lax.axis_index("x") + 2 * lax.axis_index("y") + lax.axis_index("c")
    xs = x.reshape(S, D)
    target = loss_target.reshape(S, D)

    def mixer_shards(i):
        j = i // 2
        if i % 2 == 0:
            return [[conv_w_pw1[j].astype(BF16)], [conv_w_pw2[j].astype(BF16)]]
        return [[ret_w_in[j].astype(BF16)], [ret_w_out[j].astype(BF16)]]

    def mlp_shards(i):
        return [[mlp_w1[i].astype(BF16)], [mlp_w2[i].astype(BF16)]]

    def mlp_weights(got):
        return got[0], got[1].reshape(4 * D, D)

    first_handle, _ = _exchange_start(mixer_shards(0)[:1], gather=True, name="gather_start_first")
    small = _exchange([[conv_w_dw], [ret_gn_g], [ret_gn_b], [c]], gather=True, name="gather_small")
    dw_g, gng_g, gnb_g, c_g = small
    dw3 = jnp.transpose(dw_g, (1, 2, 0, 3)).reshape(n_conv, CONV_WIDTH, CH, LANES)
    gng_full = jnp.transpose(gng_g, (1, 2, 0, 3)).reshape(n_ret, 1, 2 * D)
    gnb_full = jnp.transpose(gnb_g, (1, 2, 0, 3)).reshape(n_ret, 1, 2 * D)
    c_all = c_g.reshape(NDEV, D)

    mod_cols = _ada_fwd(c_all, ada_w, name="ada_fwd")
    mod_all = _exchange([[mod_cols]], gather=True, name="gather_mod")[0]
    mod = lax.dynamic_index_in_dim(mod_all, me, axis=2, keepdims=False)
    mod = jnp.transpose(mod, (1, 0, 2)).reshape(DEPTH, 6 * D) + ada_b
    mods = [[mod[i, j * D:(j + 1) * D].reshape(1, D) for j in range(6)] for i in range(DEPTH)]
    tb = _ret_tables(S, D // RET_HEADS)

    def vec(a):
        return a.reshape(1, -1)

    def group_a(i):
        return mixer_shards(i) if i % 2 == 0 else mixer_shards(i)[:1]

    def group_b(i):
        return mlp_shards(i) if i % 2 == 0 else mixer_shards(i)[1:] + mlp_shards(i)

    mix_first = _exchange_wait(first_handle, name="gather_wait_first", after=mod)[0]
    pw2_handle, token = _exchange_start(mixer_shards(0)[1:], gather=True, name="gather_start_pw2_0", after=mix_first)
    handle_b, token = _exchange_start(mlp_shards(0), gather=True, name="gather_start_b0", after=token)
    saved = []
    weights = []
    xcur = xs
    for i in range(DEPTH):
        sh1, sc1, g1, sh2, sc2, g2 = mods[i]
        j = i // 2
        if i > 0:
            got = _exchange_wait(handle_a, name=f"gather_wait_a{i}", after=xcur)
            mix_first = got[0]
            mix_second = got[1].reshape(-1, D) if i % 2 == 0 else None
            handle_b, token = _exchange_start(group_b(i), gather=True, name=f"gather_start_b{i}", after=got[0])
        st = dict(x_in=xcur)
        norm1 = (vec(norm_mix_g[i]), sc1, sh1)
        if i % 2 == 0:
            u, h = _mm_nn(xcur, mix_first, norm=norm1, bias=vec(conv_b_pw1[j]), out_dtype=F32, name=f"pw1_fwd{i}", after=token)
            dwo = _conv_mid_fwd(u, dw3[j], conv_b_dw[j].reshape(1, CH, LANES), name=f"conv_mid_fwd{i}")
            if i == 0:
                mix_second = _exchange_wait(pw2_handle, name="gather_wait_pw2_0", after=dwo)[0].reshape(-1, D)
            xcur, y_raw = _mm_nn(dwo, mix_second, ln=(vec(conv_ln_g[j]), vec(conv_ln_b[j])), bias=vec(conv_b_pw2[j]), res=xcur,
                                 gate=g1, name=f"pw2_fwd{i}")
            st.update(u=u, dwo=dwo, y_raw=y_raw)
            got = _exchange_wait(handle_b, name=f"gather_wait_b{i}", after=xcur)
            mlp_w = mlp_weights(got)
        else:
            proj, h = _mm_nn(xcur, mix_first, norm=norm1, name=f"ret_in_fwd{i}", after=token)
            yr, yg, qr, kr = _ret_fwd(proj, tb, gng_full[j], gnb_full[j], name=f"ret_fwd{i}")
            got = _exchange_wait(handle_b, name=f"gather_wait_b{i}", after=yg)
            mix_second, mlp_w = got[0].reshape(-1, D), mlp_weights(got[1:3])
            xcur, y_raw = _mm_nn(yg, mix_second, res=xcur, gate=g1, name=f"ret_out_fwd{i}")
            st.update(proj=proj, yr=yr, yg=yg, qr=qr, kr=kr, y_raw=y_raw)
        st.update(h=h, x_mid=xcur)
        if i + 1 < DEPTH:
            handle_a, token = _exchange_start(group_a(i + 1), gather=True, name=f"gather_start_a{i + 1}", after=got[0])
        z, h2 = _mm_nn(xcur, mlp_w[0], norm=(vec(norm_mlp_g[i]), sc2, sh2), name=f"mlp1_fwd{i}", after=token)
        xcur, o_raw = _mm_nn(z, mlp_w[1], relu2=True, res=xcur, gate=g2, name=f"mlp2_fwd{i}")
        st.update(h2=h2, z=z, o_raw=o_raw)
        saved.append(st)
        weights.append((mix_first, mix_second) + mlp_w)

    g2_last = mods[DEPTH - 1][5]
    loss_local, dx, d_final_g, dy, dgate, _ = _final_loss(xcur, vec(final_norm_g), target, saved[-1]["o_raw"], g2_last,
                                                          name="final_loss")
    loss = lax.psum(loss_local[0, 0], AXES)

    dmod_rows = [None] * DEPTH
    d_mix_g, d_mlp_g = [None] * DEPTH, [None] * DEPTH
    d_pw1, d_pw2, d_win, d_wout = [None] * n_conv, [None] * n_conv, [None] * n_ret, [None] * n_ret
    d_w1, d_w2 = [None] * DEPTH, [None] * DEPTH
    d_bpw1, d_bdw, d_lng, d_lnb, d_bpw2, d_dw = ([None] * n_conv for _ in range(6))
    d_gng, d_gnb = [None] * n_ret, [None] * n_ret

    def gn_parts(d):
        return jnp.transpose(d.reshape(RET_HEADS, NDEV, -1), (1, 0, 2))

    grad_handles = [None] * DEPTH
    token = None
    for i in reversed(range(DEPTH)):
        sh1, sc1, g1, sh2, sc2, g2 = mods[i]
        j = i // 2
        st = saved[i]
        mix_a, mix_b, w1_i, w2_i = weights[i]
        do, dg2 = dy, dgate
        dz = _mm_nt(do, w2_i, z=st["z"], out_dtype=BF16, name=f"mlp2_bwd_x{i}", after=token)
        d_w2[i] = _mm_tn(st["z"], do, relu2=True, name=f"mlp2_bwd_w{i}")
        dx, dsc2, dsh2, d_mlp_g[i], dy, dg1, dby = _mm_nt(dz, w1_i, norm=(st["x_mid"], vec(norm_mlp_g[i]), sc2, dx),
                                                          gated=(st["y_raw"], g1), name=f"mlp1_bwd_x{i}")
        d_w1[i] = _mm_tn(st["h2"], dz, col_shards=NDEV, name=f"mlp1_bwd_w{i}")
        mlp_groups = [[d_w1[i]], [d_w2[i].reshape(NDEV, 4 * D // NDEV, D)]]
        token = None
        if i == 0:
            mlp0_handle, token = _exchange_start(mlp_groups, gather=False, name="grads_start_mlp0")
            mlp_groups = []
        if i % 2 == 0:
            d_bpw2[j] = dby
            ln_gb = (vec(conv_ln_g[j]), vec(conv_ln_b[j]))
            ddw, d_lng[j], d_lnb[j], d_bdw[j] = _mm_nt(dy, mix_b, lnbwd=(st["dwo"],) + ln_gb, name=f"pw2_bwd_x{i}", after=token)
            d_pw2[j] = _mm_tn(st["dwo"], dy, ln=ln_gb, name=f"pw2_bwd_w{i}")
            du, ddw_w, dbu = _conv_mid_bwd_dw(st["u"], ddw, dw3[j], name=f"conv_mid_bwd_dw{i}")
            d_dw[j], d_bpw1[j] = ddw_w.reshape(CONV_WIDTH, D), dbu.reshape(2, D)
            d_pw1[j] = _mm_tn(st["h"], du, col_shards=NDEV, name=f"pw1_bwd_w{i}")
            mix_groups = [[d_pw1[j]], [d_pw2[j].reshape(NDEV, D // NDEV, D)],
                          [jnp.transpose(d_dw[j].reshape(CONV_WIDTH, NDEV, D // NDEV), (1, 0, 2))]]
            mix_in, mix_name = du, f"pw1_bwd_x{i}"
        else:
            dyg = _mm_nt(dy, mix_b, out_dtype=BF16, name=f"ret_out_bwd_x{i}")
            d_wout[j] = _mm_tn(st["yg"], dy, name=f"ret_out_bwd_w{i}")
            dq, dgt, dyr, d_gng[j], d_gnb[j] = _ret_bwd_q(st["proj"], st["kr"], st["yr"], dyg, tb, gng_full[j], gnb_full[j],
                                                          name=f"ret_bwd_q{i}")
            dproj = _ret_bwd_kv(st["proj"], st["qr"], st["kr"], dyr, dq, dgt, tb, name=f"ret_bwd_kv{i}")
            d_win[j] = _mm_tn(st["h"], dproj, col_shards=NDEV, name=f"ret_in_bwd_w{i}")
            mix_in, mix_name = dproj, f"ret_in_bwd_x{i}"
            mix_groups = [[d_win[j]], [d_wout[j].reshape(NDEV, 2 * D // NDEV, D)], [gn_parts(d_gng[j])],
                          [gn_parts(d_gnb[j])]]
        grad_handles[i], token = _exchange_start(mix_groups + mlp_groups, gather=False, name=f"grads_start{i}")
        gated = (saved[i - 1]["o_raw"], mods[i - 1][5]) if i > 0 else None
        outs = _mm_nt(mix_in, mix_a, norm=(st["x_in"], vec(norm_mix_g[i]), sc1, dx), gated=gated, name=mix_name, after=token)
        dx, dsc1, dsh1, d_mix_g[i] = outs[:4]
        if i > 0:
            dy, dgate = outs[4], outs[5]
        dmod_rows[i] = jnp.concatenate([dsh1, dsc1, dg1, dsh2, dsc2, dg2], axis=0)
    grad_x = dx.reshape(1, S, D)

    small_local = jnp.concatenate(dmod_rows + d_mix_g + d_mlp_g + d_bpw1 + d_bdw + d_lng + d_lnb + d_bpw2 + [d_final_g],
                                  axis=0)
    small_all = _exchange([[small_local]], gather=True, name="gather_small_grads")[0]

    def pack(src):
        return jnp.concatenate([src[n].reshape(-1, D) for n in SMALL], axis=0)[None]

    sm = _adamw(pack(W), pack(Mo), pack(Vo), [small_all], name="adamw_small")
    results = {}
    row = 0
    for n in SMALL:
        cnt = W[n].size // D
        results[n] = tuple(o[0, row:row + cnt].reshape(W[n].shape) for o in sm)
        row += cnt

    ns_ada = ada_w.shape[2]
    dmod_all = small_all[:, :6 * DEPTH, :].reshape(NDEV, DEPTH, 6 * D)
    dmod_cols = jnp.transpose(lax.dynamic_slice_in_dim(dmod_all, me * ns_ada, ns_ada, axis=2), (1, 0, 2))
    g_ada = _ada_bwd(c_all, dmod_cols, name="ada_bwd")
    flat_ada = (1, DEPTH * D, ns_ada)
    ada_res = _adamw(ada_w.reshape(flat_ada), m_ada_w.reshape(flat_ada), v_ada_w.reshape(flat_ada),
                     [g_ada.reshape(flat_ada)], name="adamw_ada_w")
    results["ada_w"] = tuple(o.reshape(ada_w.shape) for o in ada_res)

    def update(names, parts):
        for n in names:
            results[n] = _adamw(W[n], Mo[n], Vo[n], parts[n], name=f"adamw_{n}")

    got = {i: _exchange_wait(grad_handles[i], name=f"grads_wait{i}", after=dx) for i in range(DEPTH - 1, 0, -1)}
    ret_layers = [i for i in range(DEPTH) if i % 2 == 1]
    update(("ret_w_in", "ret_w_out", "ret_gn_g", "ret_gn_b"),
           dict(ret_w_in=[got[i][0] for i in ret_layers], ret_w_out=[got[i][1] for i in ret_layers],
                ret_gn_g=[got[i][2] for i in ret_layers], ret_gn_b=[got[i][3] for i in ret_layers]))
    got_mlp0 = _exchange_wait(mlp0_handle, name="grads_wait_mlp0", after=results["ret_w_in"][0])
    update(("mlp_w1", "mlp_w2"),
           dict(mlp_w1=[got_mlp0[0]] + [got[i][-2] for i in range(1, DEPTH)],
                mlp_w2=[got_mlp0[1]] + [got[i][-1] for i in range(1, DEPTH)]))
    got[0] = _exchange_wait(grad_handles[0], name="grads_wait0", after=results["mlp_w1"][0])
    conv_layers = [i for i in range(DEPTH) if i % 2 == 0]
    update(("conv_w_pw1", "conv_w_pw2", "conv_w_dw"),
           dict(conv_w_pw1=[got[i][0] for i in conv_layers], conv_w_pw2=[got[i][1] for i in conv_layers],
                conv_w_dw=[got[i][2] for i in conv_layers]))

    outs = [loss, grad_x]
    for kind in range(4):
        outs += [results[n][kind] for n in WEIGHTS]
    return tuple(outs)
```

```python
import functools

import jax
import jax.numpy as jnp
import numpy as np
from jax import lax
from jax.experimental import pallas as pl
from jax.experimental.pallas import tpu as pltpu

F32, BF16 = jnp.float32, jnp.bfloat16
AXES = ("x", "y", "c")
NDEV = 8
DEPTH = 4
EPS = 1e-6
CHUNK = 64
CONV_WIDTH = 31
HALO = 32
RET_HEADS = 4
RET_BLOCK = 256
ROPE_BASE = 10000.0
LANES = 128
ADAM_LR, ADAM_B1, ADAM_B2, ADAM_EPS, ADAM_WD, ADAM_STEP = 0.001, 0.9, 0.999, 1e-08, 0.01, 10
VMEM_LIMIT = 56 * 1024 * 1024
VMEM_BLOCK_BUDGET = 44 * 1024 * 1024
MESH = pl.DeviceIdType.MESH
RESIDENT = pl.Buffered(1)
NT_DIMS = (((1,), (1,)), ((), ()))
TN_DIMS = (((0,), (0,)), ((), ()))


def _call(body, *, name, out_shape, in_specs, out_specs, grid=(), scratch=(), sem=None, aliases=None):
    params = dict(vmem_limit_bytes=VMEM_LIMIT)
    if sem is not None:
        params["dimension_semantics"] = sem
    return pl.pallas_call(body, name=name, grid=grid, in_specs=in_specs, out_specs=out_specs, out_shape=out_shape,
                          scratch_shapes=list(scratch), input_output_aliases=aliases or {},
                          compiler_params=pltpu.CompilerParams(**params))


def _row_tile(rows, want):
    t = min(rows, want)
    while rows % t:
        t //= 2
    return t


def _sds(shape, dtype):
    return jax.ShapeDtypeStruct(tuple(shape), dtype)


def _sigmoid(v):
    return 1.0 / (1.0 + jnp.exp(-v))


def _exchange(groups, *, gather, name):
    flat = [a for g in groups for a in g]
    n_in = len(flat)
    out_shapes = []
    for g in groups:
        s = g[0].shape if gather else g[0].shape[1:]
        lead = (NDEV,) if len(g) == 1 else (NDEV, len(g))
        out_shapes.append(_sds(lead + tuple(s), g[0].dtype))
    n_g = len(groups)

    def body(*refs):
        ins, outs = refs[:n_in], refs[n_in:n_in + n_g]
        send_sems, recv_sems, loc_sems = refs[n_in + n_g:]
        x, y, c = lax.axis_index("x"), lax.axis_index("y"), lax.axis_index("c")
        me = 4 * x + 2 * y + c
        locs, k = [], 0
        for gi, g in enumerate(groups):
            for li in range(len(g)):
                src = ins[k] if gather else ins[k].at[me]
                dst = outs[gi].at[me] if len(g) == 1 else outs[gi].at[me, li]
                cp = pltpu.make_async_copy(src, dst, loc_sems.at[k])
                cp.start()
                locs.append(cp)
                k += 1
        k0 = 0
        for gi, g in enumerate(groups):
            for r in range(1, NDEV):
                px = 1 - x if r & 4 else x
                py = 1 - y if r & 2 else y
                pc = 1 - c if r & 1 else c
                peer = 4 * px + 2 * py + pc
                for li in range(len(g)):
                    src = ins[k0 + li] if gather else ins[k0 + li].at[peer]
                    dst = outs[gi].at[me] if len(g) == 1 else outs[gi].at[me, li]
                    pltpu.make_async_remote_copy(src_ref=src, dst_ref=dst, send_sem=send_sems.at[gi * (NDEV - 1) + r - 1],
                                                 recv_sem=recv_sems.at[gi * (NDEV - 1) + r - 1], device_id=(px, py, pc),
                                                 device_id_type=MESH).start()
            k0 += len(g)
        for gi, g in enumerate(groups):
            for r in range(1, NDEV):
                px = 1 - x if r & 4 else x
                py = 1 - y if r & 2 else y
                pc = 1 - c if r & 1 else c
                peer = 4 * px + 2 * py + pc
                slab = pltpu.make_async_remote_copy(src_ref=outs[gi].at[me], dst_ref=outs[gi].at[peer],
                                                    send_sem=send_sems.at[gi * (NDEV - 1) + r - 1], recv_sem=recv_sems.at[gi * (NDEV - 1) + r - 1],
                                                    device_id=(px, py, pc), device_id_type=MESH)
                slab.wait_send()
                slab.wait_recv()
        for cp in locs:
            cp.wait()

    hbm = pl.BlockSpec(memory_space=pltpu.HBM)
    outs = _call(body, name=name, out_shape=tuple(out_shapes), in_specs=[hbm] * n_in, out_specs=tuple([hbm] * n_g),
                 scratch=[pltpu.SemaphoreType.DMA((n_g * (NDEV - 1),)), pltpu.SemaphoreType.DMA((n_g * (NDEV - 1),)),
                          pltpu.SemaphoreType.DMA((n_in,))])(*flat)
    return list(outs)


def _peer_of(x, y, c, r):
    return (1 - x if r & 4 else x, 1 - y if r & 2 else y, 1 - c if r & 1 else c)


def _exchange_start(groups, *, gather, name, after=None):
    flat = [pltpu.with_memory_space_constraint(a, pltpu.HBM) for g in groups for a in g]
    n_in, n_g = len(flat), len(groups)
    land_shapes = []
    for g in groups:
        s = g[0].shape if gather else g[0].shape[1:]
        lead = (NDEV,) if len(g) == 1 else (NDEV, len(g))
        land_shapes.append((lead + tuple(s), g[0].dtype))
    lands = [pltpu.with_memory_space_constraint(lax.empty(s, d), pltpu.HBM) for s, d in land_shapes]
    n_after = 0 if after is None else 1

    def body(*refs):
        ins, land = refs[:n_in], refs[n_in:n_in + n_g]
        send_sems, recv_sems, loc_sems = refs[n_in + n_g + n_after:n_in + n_g + n_after + 3]
        token = refs[-1]
        x, y, c = lax.axis_index("x"), lax.axis_index("y"), lax.axis_index("c")
        me = 4 * x + 2 * y + c
        k = 0
        for gi, g in enumerate(groups):
            for li in range(len(g)):
                dst = land[gi].at[me] if len(g) == 1 else land[gi].at[me, li]
                pltpu.make_async_copy(ins[k] if gather else ins[k].at[me], dst, loc_sems.at[k]).start()
                k += 1
        k0 = 0
        for gi, g in enumerate(groups):
            for r in range(1, NDEV):
                px, py, pc = _peer_of(x, y, c, r)
                peer = 4 * px + 2 * py + pc
                for li in range(len(g)):
                    dst = land[gi].at[me] if len(g) == 1 else land[gi].at[me, li]
                    pltpu.make_async_remote_copy(src_ref=ins[k0 + li] if gather else ins[k0 + li].at[peer], dst_ref=dst,
                                                 send_sem=send_sems.at[gi * (NDEV - 1) + r - 1], recv_sem=recv_sems.at[gi * (NDEV - 1) + r - 1],
                                                 device_id=(px, py, pc), device_id_type=MESH).start()
            k0 += len(g)
        token[...] = jnp.zeros_like(token)

    hbm = pl.BlockSpec(memory_space=pltpu.HBM)
    sem = pl.BlockSpec(memory_space=pltpu.SEMAPHORE)
    args = flat + lands + ([after] if n_after else [])
    outs = pl.pallas_call(body, name=name,
        out_shape=(pltpu.SemaphoreType.DMA((n_g * (NDEV - 1),)), pltpu.SemaphoreType.DMA((n_g * (NDEV - 1),)),
                   pltpu.SemaphoreType.DMA((n_in,)), *[pltpu.HBM(a.shape, a.dtype) for a in flat],
                   *[pltpu.HBM(s, d) for s, d in land_shapes], _sds((8, LANES), F32)),
        in_specs=[hbm] * (n_in + n_g) + [pl.BlockSpec(memory_space=pl.ANY)] * n_after,
        out_specs=(sem, sem, sem, *[hbm] * (n_in + n_g), pl.BlockSpec(memory_space=pltpu.VMEM)),
        input_output_aliases={k: 3 + k for k in range(n_in + n_g)},
        compiler_params=pltpu.CompilerParams(has_side_effects=pltpu.SideEffectType.DATAFLOW_SIDE_EFFECTING))(*args)
    handle = dict(sems=outs[0:3], srcs=list(outs[3:3 + n_in]), lands=list(outs[3 + n_in:3 + n_in + n_g]),
                  sizes=[len(g) for g in groups], gather=gather)
    return handle, outs[-1]


def _exchange_wait(handle, *, name, after):
    srcs, lands, sizes, gather = handle["srcs"], handle["lands"], handle["sizes"], handle["gather"]
    n_in, n_g = len(srcs), len(lands)

    def body(*refs):
        ins, land = refs[:n_in], refs[n_in:n_in + n_g]
        send_sems, recv_sems, loc_sems = refs[n_in + n_g:n_in + n_g + 3]
        x, y, c = lax.axis_index("x"), lax.axis_index("y"), lax.axis_index("c")
        me = 4 * x + 2 * y + c
        for gi in range(n_g):
            for r in range(1, NDEV):
                px, py, pc = _peer_of(x, y, c, r)
                peer = 4 * px + 2 * py + pc
                slab = pltpu.make_async_remote_copy(src_ref=land[gi].at[me], dst_ref=land[gi].at[peer],
                                                    send_sem=send_sems.at[gi * (NDEV - 1) + r - 1], recv_sem=recv_sems.at[gi * (NDEV - 1) + r - 1],
                                                    device_id=(px, py, pc), device_id_type=MESH)
                slab.wait_send()
                slab.wait_recv()
        k = 0
        for gi in range(n_g):
            for li in range(sizes[gi]):
                dst = land[gi].at[me] if sizes[gi] == 1 else land[gi].at[me, li]
                pltpu.make_async_copy(ins[k] if gather else ins[k].at[me], dst, loc_sems.at[k]).wait()
                k += 1

    hbm = pl.BlockSpec(memory_space=pltpu.HBM)
    sem = pl.BlockSpec(memory_space=pltpu.SEMAPHORE)
    outs = pl.pallas_call(body, name=name, out_shape=tuple(pltpu.HBM(a.shape, a.dtype) for a in srcs + lands),
        in_specs=[hbm] * (n_in + n_g) + [sem] * 3 + [pl.BlockSpec(memory_space=pl.ANY)],
        out_specs=tuple([hbm] * (n_in + n_g)), input_output_aliases={k: k for k in range(n_in + n_g)},
        compiler_params=pltpu.CompilerParams(has_side_effects=pltpu.SideEffectType.DATAFLOW_SIDE_EFFECTING))(
            *srcs, *lands, *handle["sems"], after)
    return list(outs[n_in:])


def _gate_part(first, dx, y_ref, g_ref, dy_ref, dg_ref, db_ref):
    @pl.when(first)
    def _():
        dg_ref[...] = jnp.zeros_like(dg_ref)
        db_ref[...] = jnp.zeros_like(db_ref)

    dy = dx * g_ref[...]
    dy_ref[...] = dy.astype(BF16)
    dg_ref[...] += jnp.sum(dx * y_ref[...].astype(F32), axis=0, keepdims=True)
    db_ref[...] += jnp.sum(dy, axis=0, keepdims=True)


def _norm_bwd_part(first, dhv, x_ref, g_ref, sc_ref, dres_ref, dx_ref, dsc_ref, dsh_ref, dg_ref):
    @pl.when(first)
    def _():
        dsc_ref[...] = jnp.zeros_like(dsc_ref)
        dsh_ref[...] = jnp.zeros_like(dsh_ref)
        dg_ref[...] = jnp.zeros_like(dg_ref)

    xv = x_ref[...]
    r = lax.rsqrt(jnp.mean(xv * xv, axis=-1, keepdims=True) + EPS)
    xhat = xv * r
    gain_v = g_ref[...]
    dsc_ref[...] += jnp.sum(dhv * (xhat * gain_v), axis=0, keepdims=True)
    dsh_ref[...] += jnp.sum(dhv, axis=0, keepdims=True)
    dxn = dhv * (1.0 + sc_ref[...])
    dg_ref[...] += jnp.sum(dxn * xhat, axis=0, keepdims=True)
    dxhat = dxn * gain_v
    dx = dres_ref[...] + r * (dxhat - xhat * jnp.mean(dxhat * xhat, axis=-1, keepdims=True))
    dx_ref[...] = dx
    return dx


def _final_loss(x, gain, target, y_prev, gate_prev, *, name):
    S, D = x.shape
    tm = _row_tile(S, 512)

    def body(x_ref, g_ref, t_ref, y_ref, gp_ref, loss_ref, dx_ref, dg_ref, dy_ref, dgp_ref, dbp_ref):
        first = pl.program_id(0) == 0

        @pl.when(first)
        def _():
            loss_ref[...] = jnp.zeros_like(loss_ref)
            dg_ref[...] = jnp.zeros_like(dg_ref)

        xv = x_ref[...]
        r = lax.rsqrt(jnp.mean(xv * xv, axis=-1, keepdims=True) + EPS)
        xhat = xv * r
        gv = g_ref[...]
        err = xhat * gv - t_ref[...]
        row_loss = jnp.mean(err * err, axis=-1, keepdims=True)
        loss_ref[...] += 0.5 * jnp.sum(row_loss, axis=0, keepdims=True)
        dy = err * (1.0 / D)
        dg_ref[...] += jnp.sum(dy * xhat, axis=0, keepdims=True)
        dxhat = dy * gv
        dx = r * (dxhat - xhat * jnp.mean(dxhat * xhat, axis=-1, keepdims=True))
        dx_ref[...] = dx
        _gate_part(first, dx, y_ref, gp_ref, dy_ref, dgp_ref, dbp_ref)

    row = pl.BlockSpec((tm, D), lambda i: (i, 0))
    vec = pl.BlockSpec((1, D), lambda i: (0, 0))
    one = pl.BlockSpec((1, 1), lambda i: (0, 0))
    vsh = _sds((1, D), F32)
    return _call(body, name=name, grid=(S // tm,), in_specs=[row, vec, row, row, vec],
                 out_specs=(one, row, vec, row, vec, vec),
                 out_shape=(_sds((1, 1), F32), _sds((S, D), F32), vsh, _sds((S, D), BF16), vsh, vsh),
                 sem=("arbitrary",))(x, gain, target, y_prev, gate_prev)


def _pick_tm(M, bytes_per_row, fixed_bytes):
    for tm in (1024, 512, 256, 128):
        if M % tm == 0 and 2 * tm * bytes_per_row + fixed_bytes <= VMEM_BLOCK_BUDGET:
            return tm
    return _row_tile(M, 128)


def _mm_nn(a, w, *, name, bias=None, relu2=False, ln=None, norm=None, res=None, gate=None, out_dtype=BF16, after=None):
    M, K = a.shape
    col = w.ndim == 3
    if col:
        nsh, ns = w.shape[0], w.shape[2]
        w_spec = pl.BlockSpec((nsh, K, ns), lambda i: (0, 0, 0), pipeline_mode=RESIDENT)
    else:
        nsh, ns = 1, w.shape[1]
        w_spec = pl.BlockSpec((K, ns), lambda i: (0, 0), pipeline_mode=RESIDENT)
    N = nsh * ns
    residual = res is not None
    out_bytes = (4 + 4 + 2) if residual else jnp.dtype(out_dtype).itemsize
    tm = _pick_tm(M, K * a.dtype.itemsize + N * out_bytes + (K * 2 if norm is not None else 0), K * N * 2)

    def body(*refs):
        it = iter(refs)
        a_ref, w_ref = next(it), next(it)
        b_ref = next(it) if bias is not None else None
        lg_ref, lb_ref = (next(it), next(it)) if ln is not None else (None, None)
        ng_ref, nsc_ref, nsh_ref = (next(it), next(it), next(it)) if norm is not None else (None, None, None)
        res_ref, gate_ref = (next(it), next(it)) if residual else (None, None)
        if after is not None:
            next(it)
        out_ref = next(it)
        raw_ref = next(it) if residual else None
        av = a_ref[...]
        if relu2:
            av = jnp.square(jnp.maximum(av.astype(F32), 0.0))
        if ln is not None:
            av, _ = _ln_silu(av, lg_ref[...], lb_ref[...])
        if norm is not None:
            r = lax.rsqrt(jnp.mean(av * av, axis=-1, keepdims=True) + EPS)
            av = (av * r) * ng_ref[...] * (1.0 + nsc_ref[...]) + nsh_ref[...]
        ab = av.astype(BF16)
        if norm is not None:
            next(it)[...] = ab
        for d in range(nsh):
            cols = slice(d * ns, (d + 1) * ns)
            acc = jnp.dot(ab, w_ref[d] if col else w_ref[...], preferred_element_type=F32)
            if b_ref is not None:
                acc = acc + b_ref[:, cols]
            if residual:
                raw_ref[:, cols] = acc.astype(BF16)
                out_ref[:, cols] = res_ref[:, cols] + gate_ref[:, cols] * acc
            else:
                out_ref[:, cols] = acc.astype(out_dtype)

    tile = pl.BlockSpec((tm, N), lambda i: (i, 0))
    vec = pl.BlockSpec((1, N), lambda i: (0, 0))
    in_specs, args = [pl.BlockSpec((tm, K), lambda i: (i, 0)), w_spec], [a, w]
    if bias is not None:
        in_specs.append(vec)
        args.append(bias)
    if ln is not None:
        in_specs += [pl.BlockSpec((1, K), lambda i: (0, 0))] * 2
        args += list(ln)
    if norm is not None:
        in_specs += [pl.BlockSpec((1, K), lambda i: (0, 0))] * 3
        args += list(norm)
    if residual:
        in_specs += [tile, vec]
        args += [res, gate]
        out_specs = [tile, tile]
        out_shape = [_sds((M, N), F32), _sds((M, N), BF16)]
    else:
        out_specs = [tile]
        out_shape = [_sds((M, N), out_dtype)]
    if after is not None:
        in_specs.append(pl.BlockSpec(memory_space=pl.ANY))
        args.append(after)
    if norm is not None:
        out_specs.append(pl.BlockSpec((tm, K), lambda i: (i, 0)))
        out_shape.append(_sds((M, K), BF16))
    outs = _call(body, name=name, grid=(M // tm,), in_specs=in_specs, out_specs=tuple(out_specs), out_shape=tuple(out_shape),
                 sem=("parallel",))(*args)
    return outs[0] if len(outs) == 1 else outs


def _mm_nt(g, w, *, name, z=None, out_dtype=F32, after=None, norm=None, gated=None, lnbwd=None):
    M, N = g.shape
    col = w.ndim == 3
    if col:
        nsh, K, ns = w.shape
        w_spec = pl.BlockSpec((nsh, K, ns), lambda i: (0, 0, 0), pipeline_mode=RESIDENT)
    else:
        K = w.shape[0]
        w_spec = pl.BlockSpec((K, N), lambda i: (0, 0), pipeline_mode=RESIDENT)
    assert norm is None or col
    kc = min(K, 1024)
    assert lnbwd is None or (not col and kc == K)
    obytes = jnp.dtype(out_dtype).itemsize
    row_bytes = N * g.dtype.itemsize + K * obytes + (K * 2 if z is not None else 0)
    if norm is not None:
        row_bytes += 2 * K * 4 + (K * 4 if gated is not None else 0)
    if lnbwd is not None:
        row_bytes += K * 4
    tm = _pick_tm(M, row_bytes, K * N * 2 + 512 * K * 4)

    def body(*refs):
        it = iter(refs)
        g_ref, w_ref = next(it), next(it)
        z_ref = next(it) if z is not None else None
        norm_in = [next(it) for _ in range(4)] if norm is not None else None
        gate_in = [next(it) for _ in range(2)] if gated is not None else None
        ln_in = [next(it) for _ in range(3)] if lnbwd is not None else None
        if after is not None:
            next(it)
        out_ref = next(it)
        if col:
            acc = None
            for d in range(nsh):
                part = lax.dot_general(g_ref[:, d * ns:(d + 1) * ns].astype(BF16), w_ref[d], NT_DIMS,
                                       preferred_element_type=F32)
                acc = part if acc is None else acc + part
            if norm is None:
                out_ref[...] = acc.astype(out_dtype)
            else:
                first = pl.program_id(0) == 0
                dx = _norm_bwd_part(first, acc, *norm_in, out_ref, next(it), next(it), next(it))
                if gated is not None:
                    _gate_part(first, dx, *gate_in, next(it), next(it), next(it))
        else:
            gb = g_ref[...].astype(BF16)
            for cki in range(K // kc):
                cols = slice(cki * kc, (cki + 1) * kc)
                part = lax.dot_general(gb, w_ref[cols, :], NT_DIMS, preferred_element_type=F32)
                if z_ref is not None:
                    part = part * (2.0 * jnp.maximum(z_ref[:, cols].astype(F32), 0.0))
                if lnbwd is not None:
                    _ln_silu_bwd_part(pl.program_id(0) == 0, part, *ln_in, out_ref, next(it), next(it), next(it))
                else:
                    out_ref[:, cols] = part.astype(out_dtype)

    row = pl.BlockSpec((tm, K), lambda i: (i, 0))
    vec = pl.BlockSpec((1, K), lambda i: (0, 0))
    vsh = _sds((1, K), F32)
    in_specs, args = [pl.BlockSpec((tm, N), lambda i: (i, 0)), w_spec], [g, w]
    out_specs, out_shape = [row], [_sds((M, K), out_dtype)]
    if z is not None:
        in_specs.append(row)
        args.append(z)
    if norm is not None:
        x, gain, sc, dres = norm
        in_specs += [row, vec, vec, row]
        args += [x, gain, sc, dres]
        out_specs += [vec, vec, vec]
        out_shape += [vsh, vsh, vsh]
    if gated is not None:
        in_specs += [row, vec]
        args += list(gated)
        out_specs += [row, vec, vec]
        out_shape += [_sds((M, K), BF16), vsh, vsh]
    if lnbwd is not None:
        in_specs += [row, vec, vec]
        args += list(lnbwd)
        out_specs += [vec, vec, vec]
        out_shape += [vsh, vsh, vsh]
    if after is not None:
        in_specs.append(pl.BlockSpec(memory_space=pl.ANY))
        args.append(after)
    outs = _call(body, name=name, grid=(M // tm,), in_specs=in_specs, out_specs=tuple(out_specs), out_shape=tuple(out_shape),
                 sem=("parallel",) if norm is None and lnbwd is None else ("arbitrary",))(*args)
    return outs[0] if len(outs) == 1 else outs


def _mm_tn(a, g, *, name, col_shards=None, relu2=False, ln=None):
    M, K = a.shape
    N = g.shape[1]
    acc_budget = 8 * 1024 * 1024
    if col_shards:
        ns = N // col_shards
        spc = col_shards
        while spc > 1 and K * ns * spc * 4 > acc_budget:
            spc //= 2
        step_cols = K * a.dtype.itemsize + spc * ns * g.dtype.itemsize
    else:
        tk = K
        while tk > 128 and tk * N * 4 > acc_budget:
            tk //= 2
        step_cols = tk * a.dtype.itemsize + N * g.dtype.itemsize
    tm = _row_tile(M, 2048)
    while tm > 256 and 2 * tm * step_cols + 2 * acc_budget > VMEM_BLOCK_BUDGET:
        tm //= 2
    nm = M // tm
    if col_shards:
        grid = (col_shards // spc, nm)
        a_spec = pl.BlockSpec((tm, K), lambda c, m: (m, 0))
        g_spec = pl.BlockSpec((tm, spc * ns), lambda c, m: (m, c))
        out_spec = pl.BlockSpec((spc, K, ns), lambda c, m: (c, 0, 0))
        out_shape = _sds((col_shards, K, ns), BF16)
        acc_shape = (K, spc * ns)
    else:
        grid = (K // tk, nm)
        a_spec = pl.BlockSpec((tm, tk), lambda c, m: (m, c))
        g_spec = pl.BlockSpec((tm, N), lambda c, m: (m, 0))
        out_spec = pl.BlockSpec((tk, N), lambda c, m: (c, 0))
        out_shape = _sds((K, N), BF16)
        acc_shape = (tk, N)
        assert ln is None or tk == K
    in_specs, args = [a_spec, g_spec], [a, g]
    if ln is not None:
        in_specs += [pl.BlockSpec((1, K), lambda c, m: (0, 0))] * 2
        args += list(ln)

    def body(a_ref, g_ref, *rest):
        out_ref, acc_ref = rest[-2:]
        m = pl.program_id(1)

        @pl.when(m == 0)
        def _():
            acc_ref[...] = jnp.zeros_like(acc_ref)

        av = a_ref[...]
        if relu2:
            av = jnp.square(jnp.maximum(av.astype(F32), 0.0))
        if ln is not None:
            av, _ = _ln_silu(av, rest[0][...], rest[1][...])
        acc_ref[...] += lax.dot_general(av.astype(BF16), g_ref[...].astype(BF16), TN_DIMS, preferred_element_type=F32)

        @pl.when(m == nm - 1)
        def _():
            if col_shards:
                for s in range(spc):
                    out_ref[s] = acc_ref[:, s * ns:(s + 1) * ns].astype(BF16)
            else:
                out_ref[...] = acc_ref[...].astype(BF16)

    return _call(body, name=name, grid=grid, in_specs=in_specs, out_specs=out_spec, out_shape=out_shape,
                 scratch=[pltpu.VMEM(acc_shape, F32)], sem=("parallel", "arbitrary"))(*args)


CONV_TILE = 256


def _glu_rows(u2, ch):
    d = u2.shape[1] // 2
    return (u2[:, :d] * _sigmoid(u2[:, d:])).reshape(u2.shape[0], ch, LANES)


def _fill_glu(buf, u_ref, uh_ref, ch, tile):
    first = pl.program_id(0) == 0
    buf[0:HALO] = jnp.where(first, 0.0, _glu_rows(uh_ref[...], ch))
    buf[HALO:HALO + tile] = _glu_rows(u_ref[...], ch)


CONV_SUB = 4


def _conv_specs(S, D, tile):
    per = tile // HALO
    u_spec = pl.BlockSpec((tile, 2 * D), lambda i: (i, 0))
    uh_spec = pl.BlockSpec((HALO, 2 * D), lambda i: (jnp.maximum(i * per - 1, 0), 0))
    x_spec = pl.BlockSpec((tile, D), lambda i: (i, 0))
    xn_spec = pl.BlockSpec((HALO, D), lambda i: (jnp.minimum((i + 1) * per, S // HALO - 1), 0))
    w_spec = pl.BlockSpec((CONV_WIDTH, D // LANES, LANES), lambda i: (0, 0, 0))
    v_spec = pl.BlockSpec((1, D // LANES, LANES), lambda i: (0, 0, 0))
    return u_spec, uh_spec, x_spec, xn_spec, w_spec, v_spec


def _conv_mid_fwd(u, w3, bdw3, *, name):
    S, D = u.shape[0], u.shape[1] // 2
    ch = D // LANES
    tile = _row_tile(S, CONV_TILE)
    sub = _row_tile(tile, 2 * CONV_SUB)
    half = (CONV_WIDTH + 1) // 2
    u_spec, uh_spec, x_spec, _, w_spec, v_spec = _conv_specs(S, D, tile)

    def body(u_ref, uh_ref, w_ref, b_ref, o_ref, buf, stage):
        _fill_glu(buf, u_ref, uh_ref, ch, tile)

        def taps(lo, hi, start):
            def step(q, carry):
                rows = pl.ds(q * sub, sub)
                acc = [b_ref[...] if start else stage[rows], None]
                for k in range(lo, hi):
                    term = buf[pl.ds(q * sub + (HALO - CONV_WIDTH + 1 + k), sub)] * w_ref[k]
                    acc[k % 2] = term if acc[k % 2] is None else acc[k % 2] + term
                stage[rows] = acc[0] + acc[1]
                return carry

            lax.fori_loop(0, tile // sub, step, 0)

        taps(0, half, True)
        taps(half, CONV_WIDTH, False)
        o_ref[...] = stage[...].reshape(tile, D)

    return _call(body, name=name, grid=(S // tile,), in_specs=[u_spec, uh_spec, w_spec, v_spec], out_specs=x_spec,
                 out_shape=_sds((S, D), F32),
                 scratch=[pltpu.VMEM((tile + HALO, ch, LANES), F32), pltpu.VMEM((tile, ch, LANES), F32)],
                 sem=("parallel",))(u, u, w3, bdw3)


def _ln_silu(v, gv, bv):
    mu = jnp.mean(v, axis=-1, keepdims=True)
    cen = v - mu
    rstd = lax.rsqrt(jnp.mean(cen * cen, axis=-1, keepdims=True) + EPS)
    nrm = cen * rstd
    ln = nrm * gv + bv
    sg = _sigmoid(ln)
    return ln * sg, (nrm, rstd, ln, sg)


def _ln_silu_bwd_part(first, ds, v_ref, g_ref, b_ref, ddw_ref, dg_ref, db_ref, dbdw_ref):
    @pl.when(first)
    def _():
        dg_ref[...] = jnp.zeros_like(dg_ref)
        db_ref[...] = jnp.zeros_like(db_ref)
        dbdw_ref[...] = jnp.zeros_like(dbdw_ref)

    gv = g_ref[...]
    _, (nrm, rstd, ln, sg) = _ln_silu(v_ref[...], gv, b_ref[...])
    dln = ds * (sg * (1.0 + ln * (1.0 - sg)))
    dg_ref[...] += jnp.sum(dln * nrm, axis=0, keepdims=True)
    db_ref[...] += jnp.sum(dln, axis=0, keepdims=True)
    dn = dln * gv
    ddw = rstd * (dn - jnp.mean(dn, axis=-1, keepdims=True) - nrm * jnp.mean(dn * nrm, axis=-1, keepdims=True))
    dbdw_ref[...] += jnp.sum(ddw, axis=0, keepdims=True)
    ddw_ref[...] = ddw


def _conv_mid_bwd_dw(u, ddw, w3, *, name):
    S, D = ddw.shape
    ch = D // LANES
    tile = _row_tile(S, CONV_TILE)
    sub = _row_tile(tile, 2 * CONV_SUB)
    last = S // tile - 1
    u_spec, uh_spec, x_spec, xn_spec, w_spec, _ = _conv_specs(S, D, tile)
    b_spec = pl.BlockSpec((1, 2 * D), lambda i: (0, 0))

    def body(u_ref, uh_ref, d_ref, dn_ref, w_ref, du_ref, dw_ref, db_ref, gbuf, dbuf, stage):
        @pl.when(pl.program_id(0) == 0)
        def _():
            dw_ref[...] = jnp.zeros_like(dw_ref)
            db_ref[...] = jnp.zeros_like(db_ref)

        _fill_glu(gbuf, u_ref, uh_ref, ch, tile)
        dbuf[0:tile] = d_ref[...].reshape(tile, ch, LANES)
        dbuf[tile:tile + HALO] = jnp.where(pl.program_id(0) == last, 0.0, dn_ref[...].reshape(HALO, ch, LANES))

        def taps(lo, hi, start):
            def step(q, c):
                s0 = q * sub
                ddw_q = dbuf[pl.ds(s0, sub)]
                acc = [None if start else stage[pl.ds(s0, sub)], None]
                for k in range(lo, hi):
                    term = dbuf[pl.ds(s0 + (CONV_WIDTH - 1 - k), sub)] * w_ref[k]
                    acc[k % 2] = term if acc[k % 2] is None else acc[k % 2] + term
                    dw_ref[k] += jnp.sum(ddw_q * gbuf[pl.ds(s0 + (HALO - CONV_WIDTH + 1 + k), sub)], axis=0)
                stage[pl.ds(s0, sub)] = acc[0] + acc[1]
                return c

            lax.fori_loop(0, tile // sub, step, 0)

        half = (CONV_WIDTH + 1) // 2
        taps(0, half, True)
        taps(half, CONV_WIDTH, False)
        dglu = stage[...].reshape(tile, D)
        uv = u_ref[...]
        av, sg = uv[:, :D], _sigmoid(uv[:, D:])
        da = dglu * sg
        dg = da * av * (1.0 - sg)
        du_ref[:, 0:D] = da
        du_ref[:, D:2 * D] = dg
        db_ref[:, 0:D] += jnp.sum(da, axis=0, keepdims=True)
        db_ref[:, D:2 * D] += jnp.sum(dg, axis=0, keepdims=True)

    return _call(body, name=name, grid=(S // tile,), in_specs=[u_spec, uh_spec, x_spec, xn_spec, w_spec],
                 out_specs=(u_spec, w_spec, b_spec),
                 out_shape=(_sds((S, 2 * D), F32), _sds((CONV_WIDTH, ch, LANES), F32), _sds((1, 2 * D), F32)),
                 scratch=[pltpu.VMEM((tile + HALO, ch, LANES), F32), pltpu.VMEM((tile + HALO, ch, LANES), F32),
                          pltpu.VMEM((tile, ch, LANES), F32)],
                 sem=("arbitrary",))(u, u, ddw, ddw, w3)


def _ret_tables(S, dk):
    f32 = np.float32
    B = min(RET_BLOCK, S)
    lg = np.log(f32(1.0) - f32(2.0) ** (f32(-5.0) - np.arange(RET_HEADS, dtype=f32)))
    idx = np.arange(B, dtype=f32)
    diff = idx[:, None] - idx[None, :]
    cq, ck = (np.arange(B) // CHUNK)[:, None], (np.arange(B) // CHUNK)[None, :]
    dist = np.where(cq == ck, np.abs(diff), diff)
    mask = np.where(ck <= cq, np.exp(lg[:, None, None] * dist[None]), f32(0.0)).astype(f32)
    xi = np.exp(lg[:, None] * (idx + f32(1.0)))[..., None].astype(f32)
    zeta = np.exp(lg[:, None] * (f32(B - 1.0) - idx))[..., None].astype(f32)
    gam = np.broadcast_to(np.exp(lg * f32(B))[:, None, None], (RET_HEADS, 8, LANES)).astype(f32)
    pos = np.arange(S, dtype=f32)
    inv = (f32(ROPE_BASE) ** (-np.arange(0, dk, 2, dtype=f32) / f32(dk))).astype(f32)
    ang = (pos[:, None] * inv[None, :]).astype(f32)
    tb = dict(mask=mask, xi=xi, zeta=zeta, gam=gam, cos=np.cos(ang).astype(f32), sin=np.sin(ang).astype(f32))
    return dict(B=B, **{k: jnp.asarray(v) for k, v in tb.items()})


def _rope(v, cs, sn):
    half = v.shape[1] // 2
    v1, v2 = v[:, :half], v[:, half:]
    return jnp.concatenate([v1 * cs - v2 * sn, v2 * cs + v1 * sn], axis=-1)


def _rope_t(d, cs, sn):
    half = d.shape[1] // 2
    d1, d2 = d[:, :half], d[:, half:]
    return jnp.concatenate([d1 * cs + d2 * sn, d2 * cs - d1 * sn], axis=-1)


def _dot(a, b):
    return jnp.dot(a.astype(BF16), b.astype(BF16), preferred_element_type=F32)


def _dot_nt(a, b):
    return lax.dot_general(a.astype(BF16), b.astype(BF16), NT_DIMS, preferred_element_type=F32)


def _dot_tn(a, b):
    return lax.dot_general(a.astype(BF16), b.astype(BF16), TN_DIMS, preferred_element_type=F32)


def _ret_specs(S, D, B, RB, reverse):
    dk, dv = D // RET_HEADS, 2 * D // RET_HEADS
    nb = S // RB
    blk = (lambda ib: nb - 1 - ib) if reverse else (lambda ib: ib)
    q = pl.BlockSpec((RB, dk), lambda h, ib: (blk(ib), h))
    k = pl.BlockSpec((RB, dk), lambda h, ib: (blk(ib), RET_HEADS + h))
    v = pl.BlockSpec((RB, dv), lambda h, ib: (blk(ib), RET_HEADS + h))
    gate = pl.BlockSpec((RB, dv), lambda h, ib: (blk(ib), 2 * RET_HEADS + h))
    yv = pl.BlockSpec((RB, dv), lambda h, ib: (blk(ib), h))
    rope = pl.BlockSpec((RB, dk // 2), lambda h, ib: (blk(ib), 0))
    mask = pl.BlockSpec((None, B, B), lambda h, ib: (h, 0, 0))
    dec = pl.BlockSpec((None, B, 1), lambda h, ib: (h, 0, 0))
    gam = pl.BlockSpec((None, 8, LANES), lambda h, ib: (h, 0, 0))
    gn = pl.BlockSpec((1, dv), lambda h, ib: (0, h))
    return dict(q=q, k=k, v=v, gate=gate, yv=yv, rope=rope, mask=mask, dec=dec, gam=gam, gn=gn)


def _group_norm(yr, gv, bv):
    mu = jnp.mean(yr, axis=-1, keepdims=True)
    cen = yr - mu
    rstd = lax.rsqrt(jnp.mean(cen * cen, axis=-1, keepdims=True) + EPS)
    nrm = cen * rstd
    return nrm, rstd, nrm * gv + bv


def _ret_fwd(proj, tb, gng, gnb, *, name):
    S, D = proj.shape[0], proj.shape[1] // 6
    dk, dv = D // RET_HEADS, 2 * D // RET_HEADS
    B = tb["B"]
    RB = _row_tile(S, 8 * B)
    nsub = RB // B
    sp = _ret_specs(S, D, B, RB, False)
    scale = dk ** -0.5

    def body(q_ref, k_ref, v_ref, gt_ref, cos_ref, sin_ref, mask_ref, xi_ref, zeta_ref, gam_ref, gng_ref, gnb_ref,
             yr_ref, yg_ref, qr_ref, kr_ref, state):
        @pl.when(pl.program_id(1) == 0)
        def _():
            state[...] = jnp.zeros_like(state)

        for sb in range(nsub):
            rows = slice(sb * B, (sb + 1) * B)
            cs, sn = cos_ref[rows, :], sin_ref[rows, :]
            q = _rope(q_ref[rows, :].astype(F32), cs, sn)
            k = _rope(k_ref[rows, :].astype(F32), cs, sn) * scale
            qr_ref[rows, :] = q.astype(BF16)
            kr_ref[rows, :] = k.astype(BF16)
            vb = v_ref[rows, :]
            p = _dot_nt(q, k) * mask_ref[...]
            st = state[...]
            yr = _dot(p, vb) + _dot(q * xi_ref[...], st)
            state[...] = st * gam_ref[0:1, 0:1] + _dot_tn(k * zeta_ref[...], vb)
            _, _, gn = _group_norm(yr, gng_ref[...], gnb_ref[...])
            gt = gt_ref[rows, :].astype(F32)
            yr_ref[rows, :] = yr.astype(BF16)
            yg_ref[rows, :] = (gt * _sigmoid(gt) * gn).astype(BF16)

    return _call(body, name=name, grid=(RET_HEADS, S // RB),
                 in_specs=[sp["q"], sp["k"], sp["v"], sp["gate"], sp["rope"], sp["rope"], sp["mask"], sp["dec"], sp["dec"],
                           sp["gam"], sp["gn"], sp["gn"]],
                 out_specs=(sp["yv"], sp["yv"], sp["q"], sp["q"]),
                 out_shape=(_sds((S, 2 * D), BF16), _sds((S, 2 * D), BF16), _sds((S, D), BF16), _sds((S, D), BF16)),
                 scratch=[pltpu.VMEM((dk, dv), F32)], sem=("parallel", "arbitrary"))(
                     proj, proj, proj, proj, tb["cos"], tb["sin"], tb["mask"], tb["xi"], tb["zeta"], tb["gam"], gng, gnb)


def _ret_bwd_q(proj, kr, yr, dyg, tb, gng, gnb, *, name):
    S, D = proj.shape[0], proj.shape[1] // 6
    dk, dv = D // RET_HEADS, 2 * D // RET_HEADS
    B = tb["B"]
    RB = _row_tile(S, 8 * B)
    nsub = RB // B
    sp = _ret_specs(S, D, B, RB, False)

    def body(k_ref, v_ref, gt_ref, yr_ref, dyg_ref, cos_ref, sin_ref, mask_ref, xi_ref, zeta_ref, gam_ref,
             gng_ref, gnb_ref, dq_ref, dgt_ref, dyr_ref, dgg_ref, dgb_ref, state):
        @pl.when(pl.program_id(1) == 0)
        def _():
            state[...] = jnp.zeros_like(state)
            dgg_ref[...] = jnp.zeros_like(dgg_ref)
            dgb_ref[...] = jnp.zeros_like(dgb_ref)

        for sb in range(nsub):
            rows = slice(sb * B, (sb + 1) * B)
            cs, sn = cos_ref[rows, :], sin_ref[rows, :]
            k = k_ref[rows, :]
            vb = v_ref[rows, :]
            gv = gng_ref[...]
            nrm, rstd, gn = _group_norm(yr_ref[rows, :].astype(F32), gv, gnb_ref[...])
            gt = gt_ref[rows, :].astype(F32)
            sg = _sigmoid(gt)
            dyg = dyg_ref[rows, :].astype(F32)
            dgt_ref[rows, :] = (dyg * gn * (sg * (1.0 + gt * (1.0 - sg)))).astype(BF16)
            dgn = dyg * (gt * sg)
            dgg_ref[...] += jnp.sum(dgn * nrm, axis=0, keepdims=True)
            dgb_ref[...] += jnp.sum(dgn, axis=0, keepdims=True)
            dn = dgn * gv
            dyr = rstd * (dn - jnp.mean(dn, axis=-1, keepdims=True) - nrm * jnp.mean(dn * nrm, axis=-1, keepdims=True))
            dyr_ref[rows, :] = dyr.astype(BF16)
            dp = _dot_nt(dyr, vb) * mask_ref[...]
            st = state[...]
            dq = _dot(dp, k) + _dot_nt(dyr, st) * xi_ref[...]
            dq_ref[rows, :] = _rope_t(dq, cs, sn).astype(BF16)
            state[...] = st * gam_ref[0:1, 0:1] + _dot_tn(k.astype(F32) * zeta_ref[...], vb)

    return _call(body, name=name, grid=(RET_HEADS, S // RB),
                 in_specs=[sp["q"], sp["v"], sp["gate"], sp["yv"], sp["yv"], sp["rope"], sp["rope"], sp["mask"],
                           sp["dec"], sp["dec"], sp["gam"], sp["gn"], sp["gn"]],
                 out_specs=(sp["q"], sp["yv"], sp["yv"], sp["gn"], sp["gn"]),
                 out_shape=(_sds((S, D), BF16), _sds((S, 2 * D), BF16), _sds((S, 2 * D), BF16), _sds((1, 2 * D), F32),
                            _sds((1, 2 * D), F32)),
                 scratch=[pltpu.VMEM((dk, dv), F32)], sem=("parallel", "arbitrary"))(
                     kr, proj, proj, yr, dyg, tb["cos"], tb["sin"], tb["mask"], tb["xi"], tb["zeta"], tb["gam"], gng, gnb)


def _ret_bwd_kv(proj, qr, kr, dyr, dq, dgt, tb, *, name):
    S, D = proj.shape[0], proj.shape[1] // 6
    dk, dv = D // RET_HEADS, 2 * D // RET_HEADS
    B = tb["B"]
    RB = _row_tile(S, 2 * B)
    nsub = RB // B
    nb = S // RB
    scale = dk ** -0.5

    def body(v_ref, qr_ref, kr_ref, dyr_ref, dq_ref, dgt_ref, cos_ref, sin_ref, mask_ref, xi_ref, zeta_ref, gam_ref, out_ref,
             dstate):
        @pl.when(pl.program_id(0) == 0)
        def _():
            dstate[...] = jnp.zeros_like(dstate)

        out_ref[:, 0:D] = dq_ref[...]
        out_ref[:, 4 * D:6 * D] = dgt_ref[...]
        for sb in reversed(range(nsub)):
            rows = slice(sb * B, (sb + 1) * B)
            cs, sn = cos_ref[rows, :], sin_ref[rows, :]
            for h in range(RET_HEADS):
                kcols = slice(D + h * dk, D + (h + 1) * dk)
                vcols = slice(2 * D + h * dv, 2 * D + (h + 1) * dv)
                q = qr_ref[rows, h * dk:(h + 1) * dk]
                k = kr_ref[rows, h * dk:(h + 1) * dk]
                vb = v_ref[rows, h * dv:(h + 1) * dv]
                dyr_h = dyr_ref[rows, h * dv:(h + 1) * dv]
                mk = mask_ref[h]
                p = _dot_nt(q, k) * mk
                dp = _dot_nt(dyr_h, vb) * mk
                ds = dstate[h]
                zt = zeta_ref[h]
                dkr = _dot_tn(dp, q) + _dot_nt(vb, ds) * zt
                out_ref[rows, kcols] = _rope_t(dkr * scale, cs, sn).astype(BF16)
                out_ref[rows, vcols] = (_dot_tn(p, dyr_h) + _dot(k.astype(F32) * zt, ds)).astype(BF16)
                dstate[h] = ds * gam_ref[h, 0:1, 0:1] + _dot_tn(q.astype(F32) * xi_ref[h], dyr_h)

    def rev(width):
        return pl.BlockSpec((RB, width), lambda ib: (nb - 1 - ib, 0))

    def whole(a):
        return pl.BlockSpec(a.shape, lambda ib: (0,) * a.ndim)

    return _call(body, name=name, grid=(nb,),
                 in_specs=[pl.BlockSpec((RB, 2 * D), lambda ib: (nb - 1 - ib, 1)), rev(D), rev(D), rev(2 * D), rev(D), rev(2 * D),
                           rev(dk // 2), rev(dk // 2), whole(tb["mask"]), whole(tb["xi"]), whole(tb["zeta"]), whole(tb["gam"])],
                 out_specs=rev(6 * D), out_shape=_sds((S, 6 * D), BF16), scratch=[pltpu.VMEM((RET_HEADS, dk, dv), F32)],
                 sem=("arbitrary",))(proj, qr, kr, dyr, dq, dgt, tb["cos"], tb["sin"], tb["mask"], tb["xi"], tb["zeta"],
                                     tb["gam"])


def _ada_fwd(c_all, ada_w, *, name):
    L, D, ns = ada_w.shape

    def body(c_ref, w_ref, out_ref):
        cv = c_ref[...]
        cond = cv * _sigmoid(cv)
        out_ref[...] = jnp.dot(cond.astype(BF16), w_ref[...].astype(BF16), preferred_element_type=F32)

    return _call(body, name=name, grid=(L,), in_specs=[pl.BlockSpec((NDEV, D), lambda l: (0, 0)),
                                                      pl.BlockSpec((None, D, ns), lambda l: (l, 0, 0))],
                 out_specs=pl.BlockSpec((None, NDEV, ns), lambda l: (l, 0, 0)), out_shape=_sds((L, NDEV, ns), F32),
                 sem=("parallel",))(c_all, ada_w)


def _ada_bwd(c_all, dmod_cols, *, name):
    L, _, ns = dmod_cols.shape
    D = c_all.shape[1]

    def body(c_ref, d_ref, out_ref):
        cv = c_ref[...]
        cond = cv * _sigmoid(cv)
        out_ref[...] = lax.dot_general(cond.astype(BF16), d_ref[...].astype(BF16), TN_DIMS, preferred_element_type=F32)

    return _call(body, name=name, grid=(L,), in_specs=[pl.BlockSpec((NDEV, D), lambda l: (0, 0)),
                                                      pl.BlockSpec((None, NDEV, ns), lambda l: (l, 0, 0))],
                 out_specs=pl.BlockSpec((None, D, ns), lambda l: (l, 0, 0)), out_shape=_sds((L, D, ns), F32),
                 sem=("parallel",))(c_all, dmod_cols)


def _adamw(w, m, v, parts, *, name):
    shape = w.shape
    L, cols = len(parts), shape[-1]
    rows = w.size // (cols * L)
    n = parts[0].shape[0]
    tr = rows
    for cand in (256, 128, 64, 32, 16, 8):
        if rows % cand == 0:
            tr = cand
            break
    c1 = 1.0 - ADAM_B1 ** ADAM_STEP
    c2 = 1.0 - ADAM_B2 ** ADAM_STEP

    def body(w_ref, m_ref, v_ref, *rest):
        p_refs = rest[:L]
        g_ref, d_ref, m2_ref, v2_ref = rest[L:]
        layer = pl.program_id(0)
        for l in range(L):
            @pl.when(layer == l)
            def _(p_ref=p_refs[l]):
                g = p_ref[0].astype(F32)
                for i in range(1, n):
                    g = g + p_ref[i].astype(F32)
                m2 = ADAM_B1 * m_ref[...] + (1.0 - ADAM_B1) * g
                v2 = ADAM_B2 * v_ref[...] + (1.0 - ADAM_B2) * (g * g)
                g_ref[...] = g
                m2_ref[...] = m2
                v2_ref[...] = v2
                d_ref[...] = -ADAM_LR * ((m2 / c1) / (jnp.sqrt(v2 / c2) + ADAM_EPS) + ADAM_WD * w_ref[...])

    mat = pl.BlockSpec((None, tr, cols), lambda l, i: (l, i, 0))

    def part_spec(k):
        return pl.BlockSpec((n, tr, cols), lambda l, i: (0, jnp.where(l == k, i, 0), 0))

    outs = _call(body, name=name, grid=(L, rows // tr), in_specs=[mat, mat, mat] + [part_spec(k) for k in range(L)],
                 out_specs=(mat, mat, mat, mat), out_shape=tuple(_sds((L, rows, cols), F32) for _ in range(4)),
                 sem=("parallel", "parallel"))(w.reshape(L, rows, cols), m.reshape(L, rows, cols), v.reshape(L, rows, cols),
                                               *[p.reshape(n, rows, cols) for p in parts])
    return tuple(o.reshape(shape) for o in outs)


SMALL = ("ada_b", "norm_mix_g", "norm_mlp_g", "conv_b_pw1", "conv_b_dw", "conv_ln_g", "conv_ln_b", "conv_b_pw2",
         "final_norm_g")
WEIGHTS = ("ada_w", "ada_b", "norm_mix_g", "norm_mlp_g", "conv_w_pw1", "conv_b_pw1", "conv_w_dw", "conv_b_dw", "conv_ln_g",
           "conv_ln_b", "conv_w_pw2", "conv_b_pw2", "ret_w_in", "ret_gn_g", "ret_gn_b", "ret_w_out", "mlp_w1", "mlp_w2",
           "final_norm_g")


def kernel(x, c, ada_w, ada_b, norm_mix_g, norm_mlp_g, conv_w_pw1, conv_b_pw1, conv_w_dw, conv_b_dw, conv_ln_g, conv_ln_b, conv_w_pw2, conv_b_pw2, ret_w_in, ret_gn_g, ret_gn_b, ret_w_out, mlp_w1, mlp_w2, final_norm_g, loss_target, m_ada_w, m_ada_b, m_norm_mix_g, m_norm_mlp_g, m_conv_w_pw1, m_conv_b_pw1, m_conv_w_dw, m_conv_b_dw, m_conv_ln_g, m_conv_ln_b, m_conv_w_pw2, m_conv_b_pw2, m_ret_w_in, m_ret_gn_g, m_ret_gn_b, m_ret_w_out, m_mlp_w1, m_mlp_w2, m_final_norm_g, v_ada_w, v_ada_b, v_norm_mix_g, v_norm_mlp_g, v_conv_w_pw1, v_conv_b_pw1, v_conv_w_dw, v_conv_b_dw, v_conv_ln_g, v_conv_ln_b, v_conv_w_pw2, v_conv_b_pw2, v_ret_w_in, v_ret_gn_g, v_ret_gn_b, v_ret_w_out, v_mlp_w1, v_mlp_w2, v_final_norm_g):
    W = dict(ada_w=ada_w, ada_b=ada_b, norm_mix_g=norm_mix_g, norm_mlp_g=norm_mlp_g, conv_w_pw1=conv_w_pw1,
             conv_b_pw1=conv_b_pw1, conv_w_dw=conv_w_dw, conv_b_dw=conv_b_dw, conv_ln_g=conv_ln_g, conv_ln_b=conv_ln_b,
             conv_w_pw2=conv_w_pw2, conv_b_pw2=conv_b_pw2, ret_w_in=ret_w_in, ret_gn_g=ret_gn_g, ret_gn_b=ret_gn_b,
             ret_w_out=ret_w_out, mlp_w1=mlp_w1, mlp_w2=mlp_w2, final_norm_g=final_norm_g)
    Mo = dict(ada_w=m_ada_w, ada_b=m_ada_b, norm_mix_g=m_norm_mix_g, norm_mlp_g=m_norm_mlp_g, conv_w_pw1=m_conv_w_pw1,
              conv_b_pw1=m_conv_b_pw1, conv_w_dw=m_conv_w_dw, conv_b_dw=m_conv_b_dw, conv_ln_g=m_conv_ln_g,
              conv_ln_b=m_conv_ln_b, conv_w_pw2=m_conv_w_pw2, conv_b_pw2=m_conv_b_pw2, ret_w_in=m_ret_w_in,
              ret_gn_g=m_ret_gn_g, ret_gn_b=m_ret_gn_b, ret_w_out=m_ret_w_out, mlp_w1=m_mlp_w1, mlp_w2=m_mlp_w2,
              final_norm_g=m_final_norm_g)
    Vo = dict(ada_w=v_ada_w, ada_b=v_ada_b, norm_mix_g=v_norm_mix_g, norm_mlp_g=v_norm_mlp_g, conv_w_pw1=v_conv_w_pw1,
              conv_b_pw1=v_conv_b_pw1, conv_w_dw=v_conv_w_dw, conv_b_dw=v_conv_b_dw, conv_ln_g=v_conv_ln_g,
              conv_ln_b=v_conv_ln_b, conv_w_pw2=v_conv_w_pw2, conv_b_pw2=v_conv_b_pw2, ret_w_in=v_ret_w_in,
              ret_gn_g=v_ret_gn_g, ret_gn_b=v_ret_gn_b, ret_w_out=v_ret_w_out, mlp_w1=v_mlp_w1, mlp_w2=v_mlp_w2,
              final_norm_g=v_final_norm_g)

    S, D = x.shape[1], x.shape[2]
    CH = D // LANES
    n_conv, n_ret = conv_w_pw1.shape[0], ret_w_in.shape[0]
    me = 4 * lax.axis_index("x") + 2 * lax.axis_index("y") + lax.axis_index("c")
    xs = x.reshape(S, D)
    target = loss_target.reshape(S, D)

    def mixer_shards(i):
        j = i // 2
        if i % 2 == 0:
            return [[conv_w_pw1[j].astype(BF16)], [conv_w_pw2[j].astype(BF16)]]
        return [[ret_w_in[j].astype(BF16)], [ret_w_out[j].astype(BF16)]]

    def mlp_shards(i):
        return [[mlp_w1[i].astype(BF16)], [mlp_w2[i].astype(BF16)]]

    def mlp_weights(got):
        return got[0], got[1].reshape(4 * D, D)

    first_handle, _ = _exchange_start(mixer_shards(0)[:1], gather=True, name="gather_start_first")
    small = _exchange([[conv_w_dw], [ret_gn_g], [ret_gn_b], [c]], gather=True, name="gather_small")
    dw_g, gng_g, gnb_g, c_g = small
    dw3 = jnp.transpose(dw_g, (1, 2, 0, 3)).reshape(n_conv, CONV_WIDTH, CH, LANES)
    gng_full = jnp.transpose(gng_g, (1, 2, 0, 3)).reshape(n_ret, 1, 2 * D)
    gnb_full = jnp.transpose(gnb_g, (1, 2, 0, 3)).reshape(n_ret, 1, 2 * D)
    c_all = c_g.reshape(NDEV, D)

    mod_cols = _ada_fwd(c_all, ada_w, name="ada_fwd")
    mod_all = _exchange([[mod_cols]], gather=True, name="gather_mod")[0]
    mod = lax.dynamic_index_in_dim(mod_all, me, axis=2, keepdims=False)
    mod = jnp.transpose(mod, (1, 0, 2)).reshape(DEPTH, 6 * D) + ada_b
    mods = [[mod[i, j * D:(j + 1) * D].reshape(1, D) for j in range(6)] for i in range(DEPTH)]
    tb = _ret_tables(S, D // RET_HEADS)

    def vec(a):
        return a.reshape(1, -1)

    def group_a(i):
        return mixer_shards(i) if i % 2 == 0 else mixer_shards(i)[:1]

    def group_b(i):
        return mlp_shards(i) if i % 2 == 0 else mixer_shards(i)[1:] + mlp_shards(i)

    mix_first = _exchange_wait(first_handle, name="gather_wait_first", after=mod)[0]
    pw2_handle, token = _exchange_start(mixer_shards(0)[1:], gather=True, name="gather_start_pw2_0", after=mix_first)
    handle_b, token = _exchange_start(mlp_shards(0), gather=True, name="gather_start_b0", after=token)
    saved = []
    weights = []
    xcur = xs
    for i in range(DEPTH):
        sh1, sc1, g1, sh2, sc2, g2 = mods[i]
        j = i // 2
        if i > 0:
            got = _exchange_wait(handle_a, name=f"gather_wait_a{i}", after=xcur)
            mix_first = got[0]
            mix_second = got[1].reshape(-1, D) if i % 2 == 0 else None
            handle_b, token = _exchange_start(group_b(i), gather=True, name=f"gather_start_b{i}", after=got[0])
        st = dict(x_in=xcur)
        norm1 = (vec(norm_mix_g[i]), sc1, sh1)
        if i % 2 == 0:
            u, h = _mm_nn(xcur, mix_first, norm=norm1, bias=vec(conv_b_pw1[j]), out_dtype=F32, name=f"pw1_fwd{i}", after=token)
            dwo = _conv_mid_fwd(u, dw3[j], conv_b_dw[j].reshape(1, CH, LANES), name=f"conv_mid_fwd{i}")
            if i == 0:
                mix_second = _exchange_wait(pw2_handle, name="gather_wait_pw2_0", after=dwo)[0].reshape(-1, D)
            xcur, y_raw = _mm_nn(dwo, mix_second, ln=(vec(conv_ln_g[j]), vec(conv_ln_b[j])), bias=vec(conv_b_pw2[j]), res=xcur,
                                 gate=g1, name=f"pw2_fwd{i}")
            st.update(u=u, dwo=dwo, y_raw=y_raw)
            got = _exchange_wait(handle_b, name=f"gather_wait_b{i}", after=xcur)
            mlp_w = mlp_weights(got)
        else:
            proj, h = _mm_nn(xcur, mix_first, norm=norm1, name=f"ret_in_fwd{i}", after=token)
            yr, yg, qr, kr = _ret_fwd(proj, tb, gng_full[j], gnb_full[j], name=f"ret_fwd{i}")
            got = _exchange_wait(handle_b, name=f"gather_wait_b{i}", after=yg)
            mix_second, mlp_w = got[0].reshape(-1, D), mlp_weights(got[1:3])
            xcur, y_raw = _mm_nn(yg, mix_second, res=xcur, gate=g1, name=f"ret_out_fwd{i}")
            st.update(proj=proj, yr=yr, yg=yg, qr=qr, kr=kr, y_raw=y_raw)
        st.update(h=h, x_mid=xcur)
        if i + 1 < DEPTH:
            handle_a, token = _exchange_start(group_a(i + 1), gather=True, name=f"gather_start_a{i + 1}", after=got[0])
        z, h2 = _mm_nn(xcur, mlp_w[0], norm=(vec(norm_mlp_g[i]), sc2, sh2), name=f"mlp1_fwd{i}", after=token)
        xcur, o_raw = _mm_nn(z, mlp_w[1], relu2=True, res=xcur, gate=g2, name=f"mlp2_fwd{i}")
        st.update(h2=h2, z=z, o_raw=o_raw)
        saved.append(st)
        weights.append((mix_first, mix_second) + mlp_w)

    g2_last = mods[DEPTH - 1][5]
    loss_local, dx, d_final_g, dy, dgate, _ = _final_loss(xcur, vec(final_norm_g), target, saved[-1]["o_raw"], g2_last,
                                                          name="final_loss")
    loss = lax.psum(loss_local[0, 0], AXES)

    dmod_rows = [None] * DEPTH
    d_mix_g, d_mlp_g = [None] * DEPTH, [None] * DEPTH
    d_pw1, d_pw2, d_win, d_wout = [None] * n_conv, [None] * n_conv, [None] * n_ret, [None] * n_ret
    d_w1, d_w2 = [None] * DEPTH, [None] * DEPTH
    d_bpw1, d_bdw, d_lng, d_lnb, d_bpw2, d_dw = ([None] * n_conv for _ in range(6))
    d_gng, d_gnb = [None] * n_ret, [None] * n_ret

    def gn_parts(d):
        return jnp.transpose(d.reshape(RET_HEADS, NDEV, -1), (1, 0, 2))

    grad_handles = [None] * DEPTH
    token = None
    for i in reversed(range(DEPTH)):
        sh1, sc1, g1, sh2, sc2, g2 = mods[i]
        j = i // 2
        st = saved[i]
        mix_a, mix_b, w1_i, w2_i = weights[i]
        do, dg2 = dy, dgate
        dz = _mm_nt(do, w2_i, z=st["z"], out_dtype=BF16, name=f"mlp2_bwd_x{i}", after=token)
        d_w2[i] = _mm_tn(st["z"], do, relu2=True, name=f"mlp2_bwd_w{i}")
        dx, dsc2, dsh2, d_mlp_g[i], dy, dg1, dby = _mm_nt(dz, w1_i, norm=(st["x_mid"], vec(norm_mlp_g[i]), sc2, dx),
                                                          gated=(st["y_raw"], g1), name=f"mlp1_bwd_x{i}")
        d_w1[i] = _mm_tn(st["h2"], dz, col_shards=NDEV, name=f"mlp1_bwd_w{i}")
        mlp_groups = [[d_w1[i]], [d_w2[i].reshape(NDEV, 4 * D // NDEV, D)]]
        token = None
        if i == 0:
            mlp0_handle, token = _exchange_start(mlp_groups, gather=False, name="grads_start_mlp0")
            mlp_groups = []
        if i % 2 == 0:
            d_bpw2[j] = dby
            ln_gb = (vec(conv_ln_g[j]), vec(conv_ln_b[j]))
            ddw, d_lng[j], d_lnb[j], d_bdw[j] = _mm_nt(dy, mix_b, lnbwd=(st["dwo"],) + ln_gb, name=f"pw2_bwd_x{i}", after=token)
            d_pw2[j] = _mm_tn(st["dwo"], dy, ln=ln_gb, name=f"pw2_bwd_w{i}")
            du, ddw_w, dbu = _conv_mid_bwd_dw(st["u"], ddw, dw3[j], name=f"conv_mid_bwd_dw{i}")
            d_dw[j], d_bpw1[j] = ddw_w.reshape(CONV_WIDTH, D), dbu.reshape(2, D)
            d_pw1[j] = _mm_tn(st["h"], du, col_shards=NDEV, name=f"pw1_bwd_w{i}")
            mix_groups = [[d_pw1[j]], [d_pw2[j].reshape(NDEV, D // NDEV, D)],
                          [jnp.transpose(d_dw[j].reshape(CONV_WIDTH, NDEV, D // NDEV), (1, 0, 2))]]
            mix_in, mix_name = du, f"pw1_bwd_x{i}"
        else:
            dyg = _mm_nt(dy, mix_b, out_dtype=BF16, name=f"ret_out_bwd_x{i}")
            d_wout[j] = _mm_tn(st["yg"], dy, name=f"ret_out_bwd_w{i}")
            dq, dgt, dyr, d_gng[j], d_gnb[j] = _ret_bwd_q(st["proj"], st["kr"], st["yr"], dyg, tb, gng_full[j], gnb_full[j],
                                                          name=f"ret_bwd_q{i}")
            dproj = _ret_bwd_kv(st["proj"], st["qr"], st["kr"], dyr, dq, dgt, tb, name=f"ret_bwd_kv{i}")
            d_win[j] = _mm_tn(st["h"], dproj, col_shards=NDEV, name=f"ret_in_bwd_w{i}")
            mix_in, mix_name = dproj, f"ret_in_bwd_x{i}"
            mix_groups = [[d_win[j]], [d_wout[j].reshape(NDEV, 2 * D // NDEV, D)], [gn_parts(d_gng[j])],
                          [gn_parts(d_gnb[j])]]
        grad_handles[i], token = _exchange_start(mix_groups + mlp_groups, gather=False, name=f"grads_start{i}")
        gated = (saved[i - 1]["o_raw"], mods[i - 1][5]) if i > 0 else None
        outs = _mm_nt(mix_in, mix_a, norm=(st["x_in"], vec(norm_mix_g[i]), sc1, dx), gated=gated, name=mix_name, after=token)
        dx, dsc1, dsh1, d_mix_g[i] = outs[:4]
        if i > 0:
            dy, dgate = outs[4], outs[5]
        dmod_rows[i] = jnp.concatenate([dsh1, dsc1, dg1, dsh2, dsc2, dg2], axis=0)
    grad_x = dx.reshape(1, S, D)

    small_local = jnp.concatenate(dmod_rows + d_mix_g + d_mlp_g + d_bpw1 + d_bdw + d_lng + d_lnb + d_bpw2 + [d_final_g],
                                  axis=0)
    small_all = _exchange([[small_local]], gather=True, name="gather_small_grads")[0]

    def pack(src):
        return jnp.concatenate([src[n].reshape(-1, D) for n in SMALL], axis=0)[None]

    sm = _adamw(pack(W), pack(Mo), pack(Vo), [small_all], name="adamw_small")
    results = {}
    row = 0
    for n in SMALL:
        cnt = W[n].size // D
        results[n] = tuple(o[0, row:row + cnt].reshape(W[n].shape) for o in sm)
        row += cnt

    ns_ada = ada_w.shape[2]
    dmod_all = small_all[:, :6 * DEPTH, :].reshape(NDEV, DEPTH, 6 * D)
    dmod_cols = jnp.transpose(lax.dynamic_slice_in_dim(dmod_all, me * ns_ada, ns_ada, axis=2), (1, 0, 2))
    g_ada = _ada_bwd(c_all, dmod_cols, name="ada_bwd")
    flat_ada = (1, DEPTH * D, ns_ada)
    ada_res = _adamw(ada_w.reshape(flat_ada), m_ada_w.reshape(flat_ada), v_ada_w.reshape(flat_ada),
                     [g_ada.reshape(flat_ada)], name="adamw_ada_w")
    results["ada_w"] = tuple(o.reshape(ada_w.shape) for o in ada_res)

    def update(names, parts):
        for n in names:
            results[n] = _adamw(W[n], Mo[n], Vo[n], parts[n], name=f"adamw_{n}")

    got = {i: _exchange_wait(grad_handles[i], name=f"grads_wait{i}", after=dx) for i in range(DEPTH - 1, 0, -1)}
    ret_layers = [i for i in range(DEPTH) if i % 2 == 1]
    update(("ret_w_in", "ret_w_out", "ret_gn_g", "ret_gn_b"),
           dict(ret_w_in=[got[i][0] for i in ret_layers], ret_w_out=[got[i][1] for i in ret_layers],
                ret_gn_g=[got[i][2] for i in ret_layers], ret_gn_b=[got[i][3] for i in ret_layers]))
    got_mlp0 = _exchange_wait(mlp0_handle, name="grads_wait_mlp0", after=results["ret_w_in"][0])
    update(("mlp_w1", "mlp_w2"),
           dict(mlp_w1=[got_mlp0[0]] + [got[i][-2] for i in range(1, DEPTH)],
                mlp_w2=[got_mlp0[1]] + [got[i][-1] for i in range(1, DEPTH)]))
    got[0] = _exchange_wait(grad_handles[0], name="grads_wait0", after=results["mlp_w1"][0])
    conv_layers = [i for i in range(DEPTH) if i % 2 == 0]
    update(("conv_w_pw1", "conv_w_pw2", "conv_w_dw"),
           dict(conv_w_pw1=[got[i][0] for i in conv_layers], conv_w_pw2=[got[i][1] for i in conv_layers],
                conv_w_dw=[got[i][2] for i in conv_layers]))

    outs = [loss, grad_x]
    for kind in range(4):
        outs += [results[n][kind] for n in WEIGHTS]
    return tuple(outs)
```
